```python
import jax, jax.numpy as jnp
from jax import lax
import numpy as np

D_MODEL = 1024
BATCH = 16
SEQ = 4096
DEPTH = 2

N_MIXERS = 2
N_META = 16
BLOCK = 128
N_PAD = BLOCK - N_META
BRANCH = D_MODEL
HG_DK = 128
HG_DV = 128
HG_HEADS = BRANCH // HG_DV
HG_CHUNK = BLOCK
HG_SUB = 16
N_SUB = HG_CHUNK // HG_SUB
SB_DH = 128
SB_HEADS = BRANCH // SB_DH
SB_SCALE = SB_DH ** -0.5
N_HGRN = (DEPTH + 1) // 2
N_SB = DEPTH // 2
EPS = 1e-6

kernel_name = "hybrid_hgrn2_stickbreaking_meta"


def _rmsnorm(x, w):
    xf = x.astype(jnp.float32)
    y = xf * lax.rsqrt(jnp.mean(xf * xf, axis=-1, keepdims=True) + EPS)
    return (y * w.astype(jnp.float32)).astype(x.dtype)


def _heads(t, n):
    b, l, _ = t.shape
    return t.reshape(b, l, n, -1).transpose(0, 2, 1, 3)


def _pad_front(t):
    return jnp.pad(t, ((0, 0), (0, 0), (N_PAD, 0), (0, 0)))


def _to_blocks(t):
    b, h, lp, d = t.shape
    return t.reshape(b, h, lp // BLOCK, BLOCK, d).transpose(2, 0, 1, 3, 4)


def _hgrn2_chunk(S, inp):
    q, k, v, g = inp
    bsz, h = q.shape[:2]
    b = jnp.cumsum(g, axis=2)
    o = jnp.einsum('bhtd,bhde->bhte', q * jnp.exp(b), S)
    qs = q.reshape(bsz, h, N_SUB, HG_SUB, HG_DK)
    ks = k.reshape(bsz, h, N_SUB, HG_SUB, HG_DK)
    vs = v.reshape(bsz, h, N_SUB, HG_SUB, HG_DV)
    bs = b.reshape(bsz, h, N_SUB, HG_SUB, HG_DK)
    tri = jnp.tril(jnp.ones((HG_SUB, HG_SUB), bool))[:, :, None]
    diff = bs[:, :, :, :, None, :] - bs[:, :, :, None, :, :]
    dec = jnp.exp(jnp.where(tri, diff, -jnp.inf))
    a_diag = jnp.einsum('bhntd,bhntsd,bhnsd->bhnts', qs, dec, ks)
    o_diag = jnp.einsum('bhnts,bhnse->bhnte', a_diag, vs)
    ref = jnp.concatenate([jnp.zeros_like(bs[:, :, :1, 0, :]), bs[:, :, :-1, -1, :]], axis=2)
    q_hat = qs * jnp.exp(bs - ref[:, :, :, None, :])
    lower = (jnp.arange(N_SUB)[None, :] < jnp.arange(N_SUB)[:, None])[:, :, None, None]
    expo = ref[:, :, :, None, None, :] - bs[:, :, None, :, :, :]
    k_hat = ks[:, :, None] * jnp.exp(jnp.where(lower, expo, -jnp.inf))
    a_off = jnp.einsum('bhitd,bhijsd->bhitjs', q_hat, k_hat)
    o_off = jnp.einsum('bhitjs,bhjse->bhite', a_off, vs)
    o = o + (o_diag + o_off).reshape(bsz, h, HG_CHUNK, HG_DV)
    b_last = b[:, :, -1, :]
    S = jnp.exp(b_last)[..., None] * S + jnp.einsum('bhsd,bhse->bhde', k * jnp.exp(b_last[:, :, None, :] - b), v)
    return S, o


def _hgrn2_mixer(y, w_in, lb, out_norm, w_out):
    bsz, l, _ = y.shape
    q, fz, v, gate = jnp.split(y @ w_in, 4, axis=-1)
    q, fz, v = (_heads(t, HG_HEADS).astype(jnp.float32) for t in (q, fz, v))
    lb = lb.reshape(1, HG_HEADS, 1, HG_DK)
    g = jnp.log(lb + (1.0 - lb) * jax.nn.sigmoid(fz))
    k = (1.0 - lb) * jax.nn.sigmoid(-fz)
    xs = tuple(_to_blocks(_pad_front(t)) for t in (q, k, v, g))
    s0 = jnp.zeros((bsz, HG_HEADS, HG_DK, HG_DV), jnp.float32)
    _, o = lax.scan(_hgrn2_chunk, s0, xs)
    nb = o.shape[0]
    o = o.transpose(1, 2, 0, 3, 4).reshape(bsz, HG_HEADS, nb * HG_CHUNK, HG_DV)[:, :, N_PAD:]
    o = _rmsnorm(o.transpose(0, 2, 1, 3), out_norm).reshape(bsz, l, BRANCH)
    return (o.astype(y.dtype) * jax.nn.silu(gate)) @ w_out


def _stick_breaking_mixer(y, w_in, w_out):
    bsz, l, _ = y.shape
    q, k, v, gate = jnp.split(y @ w_in, 4, axis=-1)
    q, k, v = (_pad_front(_heads(t, SB_HEADS).astype(jnp.float32)) for t in (q, k, v))
    lp = l + N_PAD
    s_pos = jnp.arange(lp)
    key_ok = s_pos >= N_PAD

    def block(args):
        qb, n = args
        t_pos = n * BLOCK + jnp.arange(BLOCK)
        valid = (s_pos[None, :] < t_pos[:, None]) & key_ok[None, :]
        z = jnp.einsum('bhtd,bhsd->bhts', qb, k) * SB_SCALE
        log_beta = jax.nn.log_sigmoid(z)
        log_keep = jnp.where(valid, log_beta - z, 0.0)
        later = lax.cumsum(log_keep, axis=3, reverse=True) - log_keep
        a = jnp.where(valid, jnp.exp(log_beta + later), 0.0)
        return jnp.einsum('bhts,bhse->bhte', a, v)

    o = lax.map(block, (_to_blocks(q), jnp.arange(lp // BLOCK)))
    o = o.transpose(1, 0, 3, 2, 4).reshape(bsz, lp, BRANCH)[:, N_PAD:]
    return (o.astype(y.dtype) * jax.nn.silu(gate)) @ w_out


def _fwd_setup_inputs(seed: int = 0) -> dict:
    key = jax.random.key(seed)
    ks = jax.random.split(key, 11)
    f32 = jnp.float32
    x = jax.random.normal(ks[0], (BATCH, SEQ, D_MODEL), f32)
    meta_tokens = jax.random.normal(ks[1], (N_META, D_MODEL), f32)
    pre_norm = 1.0 + 0.05 * jax.random.normal(ks[2], (DEPTH, D_MODEL), f32)
    post_norm = 1.0 + 0.05 * jax.random.normal(ks[3], (DEPTH, D_MODEL), f32)
    hgrn_w_in = jax.random.normal(ks[4], (N_HGRN, D_MODEL, 4 * BRANCH), f32) * D_MODEL ** -0.5
    hgrn_lb = 0.5 * jax.random.normal(ks[5], (N_HGRN + 1, BRANCH), f32)
    hgrn_out_norm = 1.0 + 0.05 * jax.random.normal(ks[6], (N_HGRN, HG_DV), f32)
    hgrn_w_out = jax.random.normal(ks[7], (N_HGRN, BRANCH, D_MODEL), f32) * BRANCH ** -0.5
    sb_w_in = jax.random.normal(ks[8], (N_SB, D_MODEL, 4 * BRANCH), f32) * D_MODEL ** -0.5
    sb_w_out = jax.random.normal(ks[9], (N_SB, BRANCH, D_MODEL), f32) * BRANCH ** -0.5
    return {"x": x, "meta_tokens": meta_tokens, "pre_norm": pre_norm, "post_norm": post_norm,
            "hgrn_w_in": hgrn_w_in, "hgrn_lb": hgrn_lb, "hgrn_out_norm": hgrn_out_norm,
            "hgrn_w_out": hgrn_w_out, "sb_w_in": sb_w_in, "sb_w_out": sb_w_out}


def _fwd_reference(x, meta_tokens, pre_norm, post_norm, hgrn_w_in, hgrn_lb, hgrn_out_norm, hgrn_w_out, sb_w_in, sb_w_out):
    bsz = x.shape[0]
    meta = jnp.broadcast_to(meta_tokens[None].astype(x.dtype), (bsz, N_META, D_MODEL))
    h = jnp.concatenate([meta, x], axis=1)
    lbs = jnp.cumsum(jax.nn.softmax(hgrn_lb.astype(jnp.float32), axis=0), axis=0)
    for i in range(DEPTH):
        y = _rmsnorm(h, pre_norm[i])
        j = i // N_MIXERS
        if i % N_MIXERS == 0:
            y = _hgrn2_mixer(y, hgrn_w_in[j], lbs[j], hgrn_out_norm[j], hgrn_w_out[j])
        else:
            y = _stick_breaking_mixer(y, sb_w_in[j], sb_w_out[j])
        h = h + _rmsnorm(y, post_norm[i])
    return h[:, N_META:]


import jax as _jax
import jax.numpy as _jnp

TWIN_FORMAT = 'train_step'
FWD_PARAMS = ['x', 'meta_tokens', 'pre_norm', 'post_norm', 'hgrn_w_in', 'hgrn_lb', 'hgrn_out_norm', 'hgrn_w_out', 'sb_w_in', 'sb_w_out']
TWIN_WEIGHTS = ['meta_tokens', 'pre_norm', 'post_norm', 'hgrn_w_in', 'hgrn_lb', 'hgrn_out_norm', 'hgrn_w_out', 'sb_w_in', 'sb_w_out']
TWIN_DIFF_INPUT = 'x'
TWIN_INPUTS = ['x', 'meta_tokens', 'pre_norm', 'post_norm', 'hgrn_w_in', 'hgrn_lb', 'hgrn_out_norm', 'hgrn_w_out', 'sb_w_in', 'sb_w_out', 'loss_target', 'm_meta_tokens', 'm_pre_norm', 'm_post_norm', 'm_hgrn_w_in', 'm_hgrn_lb', 'm_hgrn_out_norm', 'm_hgrn_w_out', 'm_sb_w_in', 'm_sb_w_out', 'v_meta_tokens', 'v_pre_norm', 'v_post_norm', 'v_hgrn_w_in', 'v_hgrn_lb', 'v_hgrn_out_norm', 'v_hgrn_w_out', 'v_sb_w_in', 'v_sb_w_out']
TWIN_OUTPUTS = ['loss', 'grad_x', 'grad_meta_tokens', 'grad_pre_norm', 'grad_post_norm', 'grad_hgrn_w_in', 'grad_hgrn_lb', 'grad_hgrn_out_norm', 'grad_hgrn_w_out', 'grad_sb_w_in', 'grad_sb_w_out', 'delta_meta_tokens', 'delta_pre_norm', 'delta_post_norm', 'delta_hgrn_w_in', 'delta_hgrn_lb', 'delta_hgrn_out_norm', 'delta_hgrn_w_out', 'delta_sb_w_in', 'delta_sb_w_out', 'new_m_meta_tokens', 'new_m_pre_norm', 'new_m_post_norm', 'new_m_hgrn_w_in', 'new_m_hgrn_lb', 'new_m_hgrn_out_norm', 'new_m_hgrn_w_out', 'new_m_sb_w_in', 'new_m_sb_w_out', 'new_v_meta_tokens', 'new_v_pre_norm', 'new_v_post_norm', 'new_v_hgrn_w_in', 'new_v_hgrn_lb', 'new_v_hgrn_out_norm', 'new_v_hgrn_w_out', 'new_v_sb_w_in', 'new_v_sb_w_out']
TWIN_LEAF_KINDS = {'loss': 'loss', 'grad_x': 'grad_x', 'grad_meta_tokens': 'grad_w', 'grad_pre_norm': 'grad_w', 'grad_post_norm': 'grad_w', 'grad_hgrn_w_in': 'grad_w', 'grad_hgrn_lb': 'grad_w', 'grad_hgrn_out_norm': 'grad_w', 'grad_hgrn_w_out': 'grad_w', 'grad_sb_w_in': 'grad_w', 'grad_sb_w_out': 'grad_w', 'delta_meta_tokens': 'delta_w', 'delta_pre_norm': 'delta_w', 'delta_post_norm': 'delta_w', 'delta_hgrn_w_in': 'delta_w', 'delta_hgrn_lb': 'delta_w', 'delta_hgrn_out_norm': 'delta_w', 'delta_hgrn_w_out': 'delta_w', 'delta_sb_w_in': 'delta_w', 'delta_sb_w_out': 'delta_w', 'new_m_meta_tokens': 'new_m', 'new_m_pre_norm': 'new_m', 'new_m_post_norm': 'new_m', 'new_m_hgrn_w_in': 'new_m', 'new_m_hgrn_lb': 'new_m', 'new_m_hgrn_out_norm': 'new_m', 'new_m_hgrn_w_out': 'new_m', 'new_m_sb_w_in': 'new_m', 'new_m_sb_w_out': 'new_m', 'new_v_meta_tokens': 'new_v', 'new_v_pre_norm': 'new_v', 'new_v_post_norm': 'new_v', 'new_v_hgrn_w_in': 'new_v', 'new_v_hgrn_lb': 'new_v', 'new_v_hgrn_out_norm': 'new_v', 'new_v_hgrn_w_out': 'new_v', 'new_v_sb_w_in': 'new_v', 'new_v_sb_w_out': 'new_v'}


def _forward(args):
    return _fwd_reference(*[args[k] for k in FWD_PARAMS])


def _output_shape():
    out = _jax.eval_shape(lambda: _forward(_fwd_setup_inputs(0)))
    return out.shape, out.dtype

N_MICROBATCH = 1
ADAM_LR = 0.001
ADAM_B1 = 0.9
ADAM_B2 = 0.999
ADAM_EPS = 1e-08
ADAM_WD = 0.01
ADAM_STEP = 10
PER_EXAMPLE_BATCH_AXIS = {'x': 0, 'loss_target': 0}
SHARED_INPUTS = []
_WEIGHT_DTYPES = {'meta_tokens': _jnp.float32, 'pre_norm': _jnp.float32, 'post_norm': _jnp.float32, 'hgrn_w_in': _jnp.float32, 'hgrn_lb': _jnp.float32, 'hgrn_out_norm': _jnp.float32, 'hgrn_w_out': _jnp.float32, 'sb_w_in': _jnp.float32, 'sb_w_out': _jnp.float32}
MOMENT_SCALE = {'meta_tokens': 2.439534e-02, 'pre_norm': 1.117152e+00, 'post_norm': 6.389893e+01, 'hgrn_w_in': 7.042664e-01, 'hgrn_lb': 4.099717e-01, 'hgrn_out_norm': 1.767025e+00, 'hgrn_w_out': 6.550496e-01, 'sb_w_in': 3.457607e-01, 'sb_w_out': 4.412229e-01}


def _to_microbatches(a, axis):
    t = _jnp.moveaxis(a, axis, 0)
    t = t.reshape((N_MICROBATCH, t.shape[0] // N_MICROBATCH) + t.shape[1:])
    return _jnp.moveaxis(t, 1, axis + 1)


def setup_inputs(seed: int = 0) -> dict:
    inp = _fwd_setup_inputs(seed)
    key = _jax.random.fold_in(_jax.random.key(seed), 7919)
    shape, _ = _output_shape()
    out = dict(inp)
    out["loss_target"] = _jax.random.normal(_jax.random.fold_in(key, 0), shape, _jnp.float32)
    for i, name in enumerate(TWIN_WEIGHTS):
        w = inp[name].astype(_jnp.float32)
        if MOMENT_SCALE is None:
            s = _jnp.sqrt(_jnp.mean(_jnp.square(w)) + 1e-30)
        else:
            s = MOMENT_SCALE[name]
        km, kv = _jax.random.split(_jax.random.fold_in(key, i + 1))
        out[name] = w
        out["m_" + name] = s * _jax.random.normal(km, w.shape, _jnp.float32)
        out["v_" + name] = (s * s) * _jax.random.uniform(kv, w.shape, _jnp.float32, 0.5, 1.5)
    if N_MICROBATCH > 1:
        for name, axis in PER_EXAMPLE_BATCH_AXIS.items():
            out[name] = _to_microbatches(out[name], axis)
    return {'x': out['x'], 'meta_tokens': out['meta_tokens'], 'pre_norm': out['pre_norm'], 'post_norm': out['post_norm'], 'hgrn_w_in': out['hgrn_w_in'], 'hgrn_lb': out['hgrn_lb'], 'hgrn_out_norm': out['hgrn_out_norm'], 'hgrn_w_out': out['hgrn_w_out'], 'sb_w_in': out['sb_w_in'], 'sb_w_out': out['sb_w_out'], 'loss_target': out['loss_target'], 'm_meta_tokens': out['m_meta_tokens'], 'm_pre_norm': out['m_pre_norm'], 'm_post_norm': out['m_post_norm'], 'm_hgrn_w_in': out['m_hgrn_w_in'], 'm_hgrn_lb': out['m_hgrn_lb'], 'm_hgrn_out_norm': out['m_hgrn_out_norm'], 'm_hgrn_w_out': out['m_hgrn_w_out'], 'm_sb_w_in': out['m_sb_w_in'], 'm_sb_w_out': out['m_sb_w_out'], 'v_meta_tokens': out['v_meta_tokens'], 'v_pre_norm': out['v_pre_norm'], 'v_post_norm': out['v_post_norm'], 'v_hgrn_w_in': out['v_hgrn_w_in'], 'v_hgrn_lb': out['v_hgrn_lb'], 'v_hgrn_out_norm': out['v_hgrn_out_norm'], 'v_hgrn_w_out': out['v_hgrn_w_out'], 'v_sb_w_in': out['v_sb_w_in'], 'v_sb_w_out': out['v_sb_w_out']}


def _loss(weights, diff, rest, loss_target):
    with _jax.named_scope("forward"):
        args = {**rest, TWIN_DIFF_INPUT: diff, **{k: w.astype(_WEIGHT_DTYPES[k]) for k, w in weights.items()}}
        y = _forward(args)
    with _jax.named_scope("loss_head"):
        err = _jnp.square(y.astype(_jnp.float32) - loss_target)
        return 0.5 * _jnp.sum(_jnp.mean(err, axis=-1)) if err.ndim else 0.5 * err


def _adamw(w, g, m, v):
    m = ADAM_B1 * m + (1.0 - ADAM_B1) * g
    v = ADAM_B2 * v + (1.0 - ADAM_B2) * _jnp.square(g)
    m_hat = m / (1.0 - ADAM_B1 ** ADAM_STEP)
    v_hat = v / (1.0 - ADAM_B2 ** ADAM_STEP)
    delta = -ADAM_LR * (m_hat / (_jnp.sqrt(v_hat) + ADAM_EPS) + ADAM_WD * w)
    return delta, m, v


def reference(x, meta_tokens, pre_norm, post_norm, hgrn_w_in, hgrn_lb, hgrn_out_norm, hgrn_w_out, sb_w_in, sb_w_out, loss_target, m_meta_tokens, m_pre_norm, m_post_norm, m_hgrn_w_in, m_hgrn_lb, m_hgrn_out_norm, m_hgrn_w_out, m_sb_w_in, m_sb_w_out, v_meta_tokens, v_pre_norm, v_post_norm, v_hgrn_w_in, v_hgrn_lb, v_hgrn_out_norm, v_hgrn_w_out, v_sb_w_in, v_sb_w_out):
    given = dict(x=x, meta_tokens=meta_tokens, pre_norm=pre_norm, post_norm=post_norm, hgrn_w_in=hgrn_w_in, hgrn_lb=hgrn_lb, hgrn_out_norm=hgrn_out_norm, hgrn_w_out=hgrn_w_out, sb_w_in=sb_w_in, sb_w_out=sb_w_out, loss_target=loss_target, m_meta_tokens=m_meta_tokens, m_pre_norm=m_pre_norm, m_post_norm=m_post_norm, m_hgrn_w_in=m_hgrn_w_in, m_hgrn_lb=m_hgrn_lb, m_hgrn_out_norm=m_hgrn_out_norm, m_hgrn_w_out=m_hgrn_w_out, m_sb_w_in=m_sb_w_in, m_sb_w_out=m_sb_w_out, v_meta_tokens=v_meta_tokens, v_pre_norm=v_pre_norm, v_post_norm=v_post_norm, v_hgrn_w_in=v_hgrn_w_in, v_hgrn_lb=v_hgrn_lb, v_hgrn_out_norm=v_hgrn_out_norm, v_hgrn_w_out=v_hgrn_w_out, v_sb_w_in=v_sb_w_in, v_sb_w_out=v_sb_w_out)
    weights = {n: given[n] for n in TWIN_WEIGHTS}
    shared = {n: given[n] for n in SHARED_INPUTS}
    per_example = {n: given[n] for n in ['x']}
    grad_fn = _jax.value_and_grad(_loss, argnums=(0, 1))

    def one_microbatch(ex, loss_target):
        ex = dict(ex)
        diff = ex.pop(TWIN_DIFF_INPUT)
        return grad_fn(weights, diff, {**shared, **ex}, loss_target)

    if N_MICROBATCH == 1:
        loss, (grad_w, grad_x) = one_microbatch(per_example, given["loss_target"])
    else:
        def body(carry, xs):
            loss_sum, grad_sum = carry
            l_k, (gw_k, gx_k) = one_microbatch(xs[0], xs[1])
            with _jax.named_scope("update"):
                return (loss_sum + l_k, _jax.tree.map(_jnp.add, grad_sum, gw_k)), gx_k

        init = (_jnp.zeros((), _jnp.float32), _jax.tree.map(_jnp.zeros_like, weights))
        (loss, grad_w), grad_x = _jax.lax.scan(body, init, (per_example, given["loss_target"]))
    with _jax.named_scope("update"):
        delta_w, new_m, new_v = {}, {}, {}
        for n in TWIN_WEIGHTS:
            delta_w[n], new_m[n], new_v[n] = _adamw(weights[n], grad_w[n], given["m_" + n], given["v_" + n])
    return (loss, grad_x, *[grad_w[n] for n in TWIN_WEIGHTS], *[delta_w[n] for n in TWIN_WEIGHTS],
            *[new_m[n] for n in TWIN_WEIGHTS], *[new_v[n] for n in TWIN_WEIGHTS])
```

```python
import functools

import jax
import jax.numpy as jnp
from jax import lax
from jax.experimental import pallas as pl
from jax.experimental.pallas import tpu as pltpu

F32 = jnp.float32
_MXU = jnp.bfloat16

D_MODEL = 1024
N_HEADS = 8
D_HEAD = 128
BLOCK = 128
N_META = 16
N_PAD = BLOCK - N_META
SUB = 16
N_SUB = BLOCK // SUB
EPS = 1e-6
SB_SCALE = D_HEAD ** -0.5
ADAM_LR, ADAM_B1, ADAM_B2, ADAM_EPS, ADAM_WD, ADAM_STEP = 0.001, 0.9, 0.999, 1e-08, 0.01, 10
N_CHIPS = 4
N_DEV = 8
SMALL_ROWS = 24
VMEM_LIMIT = 56 * 1024 * 1024
MESH = pl.DeviceIdType.MESH

_NT = (((1,), (1,)), ((), ()))
_TN = (((0,), (0,)), ((), ()))


def _mm(a, b):
    return jnp.dot(a.astype(_MXU), b.astype(_MXU), preferred_element_type=F32)


def _mm_nt(a, b):
    return lax.dot_general(a.astype(_MXU), b.astype(_MXU), _NT, preferred_element_type=F32)


def _mm_tn(a, b):
    return lax.dot_general(a.astype(_MXU), b.astype(_MXU), _TN, preferred_element_type=F32)


def _split2(x):
    hi = x.astype(_MXU)
    return hi, (x - hi.astype(F32)).astype(_MXU)


def _mm_s(a16, state):
    hi, lo = _split2(state)
    return jnp.dot(a16, hi, preferred_element_type=F32) + jnp.dot(a16, lo, preferred_element_type=F32)


def _mm_nt_s(a16, state):
    hi, lo = _split2(state)
    return (lax.dot_general(a16, hi, _NT, preferred_element_type=F32)
            + lax.dot_general(a16, lo, _NT, preferred_element_type=F32))


def _split3(x):
    hi = x.astype(_MXU)
    r1 = x - hi.astype(F32)
    mid = r1.astype(_MXU)
    lo = (r1 - mid.astype(F32)).astype(_MXU)
    return hi, mid, lo


def _mm01_right(x, m01):
    hi, mid, lo = _split3(x)
    d = functools.partial(jnp.dot, preferred_element_type=F32)
    return d(hi, m01) + d(mid, m01) + d(lo, m01)


def _mm01_left(m01, x):
    hi, mid, lo = _split3(x)
    d = functools.partial(jnp.dot, preferred_element_type=F32)
    return d(m01, hi) + d(m01, mid) + d(m01, lo)


def _iota2(shape, dim):
    return lax.broadcasted_iota(jnp.int32, shape, dim)


def _row_tile(total, pref):
    t = pref
    while total % t:
        t -= BLOCK
    return t


def _params(sem, limit=VMEM_LIMIT):
    return pltpu.CompilerParams(dimension_semantics=sem, vmem_limit_bytes=limit)


def _sigmoid(x):
    return 1.0 / (1.0 + jnp.exp(-x))


def _norm_proj(h, gain, w4, name):
    t = h.shape[0]
    tm = _row_tile(t, 256)

    def body(h_ref, g_ref, w_ref, y_ref, s0, s1, s2, s3):
        x = h_ref[...]
        r = lax.rsqrt(jnp.mean(x * x, axis=-1, keepdims=True) + EPS)
        y = (x * r * g_ref[...]).astype(_MXU)
        y_ref[...] = y
        for j, s in enumerate((s0, s1, s2, s3)):
            s[...] = jnp.dot(y, w_ref[j], preferred_element_type=F32)

    row = pl.BlockSpec((tm, D_MODEL), lambda i: (i, 0))
    return pl.pallas_call(
        body, name=name, grid=(t // tm,),
        in_specs=[row, pl.BlockSpec((1, D_MODEL), lambda i: (0, 0)),
                  pl.BlockSpec((4, D_MODEL, D_MODEL), lambda i: (0, 0, 0))],
        out_specs=[row] * 5,
        out_shape=[jax.ShapeDtypeStruct((t, D_MODEL), _MXU)] + [jax.ShapeDtypeStruct((t, D_MODEL), F32)] * 4,
        compiler_params=_params(("arbitrary",)),
    )(h, gain, w4)


def _hgrn_gates(fz, lam, chunk):
    pos = chunk * BLOCK + _iota2((BLOCK, D_HEAD), 0)
    live = pos >= N_PAD
    sg = _sigmoid(fz)
    f = lam + (1.0 - lam) * sg
    g = jnp.where(live, jnp.log(f), 0.0)
    k = jnp.where(live, (1.0 - lam) * (1.0 - sg), 0.0)
    return sg, f, g, k, live


def _sub_cumsum_matrix():
    r, c = _iota2((BLOCK, BLOCK), 0), _iota2((BLOCK, BLOCK), 1)
    return ((r // SUB == c // SUB) & (c <= r)).astype(_MXU)


def _hgrn_fwd(qs, fs, vs, lam, bsz, nb):
    t = qs.shape[0]

    def body(q_ref, f_ref, v_ref, lam_ref, o_ref, sst_ref, st_scr, b_scr, k_scr):
        n = pl.program_id(2)

        @pl.when(n == 0)
        def _():
            st_scr[...] = jnp.zeros_like(st_scr)

        sst_ref[0] = st_scr[...]
        _, _, g, k, _ = _hgrn_gates(f_ref[...], lam_ref[...], n)
        b_scr[...] = _mm01_left(_sub_cumsum_matrix(), g)
        k_scr[...] = k
        row = _iota2((SUB, D_HEAD), 0)
        st = st_scr[...]
        for i in range(N_SUB):
            rs = pl.ds(i * SUB, SUB)
            qi, vi, bi, ki = q_ref[rs, :], v_ref[rs, :], b_scr[rs, :], k_scr[rs, :]
            acc = _mm_nt_s((qi * jnp.exp(bi)).astype(_MXU), st)
            for s in range(SUB):
                w = jnp.where(row >= s, jnp.exp(bi - bi[s:s + 1, :]), 0.0)
                col = jnp.sum(qi * w * ki[s:s + 1, :], axis=1, keepdims=True)
                acc = acc + col * vi[s:s + 1, :]
            o_ref[rs, :] = acc
            b_end = bi[SUB - 1:SUB, :]
            st = st * jnp.exp(b_end) + _mm_tn(vi, ki * jnp.exp(b_end - bi))
        st_scr[...] = st

    blk = pl.BlockSpec((BLOCK, D_HEAD), lambda b, h, n: (b * nb + n, h))
    return pl.pallas_call(
        body, name="hgrn_fwd", grid=(bsz, N_HEADS, nb),
        in_specs=[blk, blk, blk, pl.BlockSpec((1, D_HEAD), lambda b, h, n: (0, h))],
        out_specs=[blk, pl.BlockSpec((1, D_HEAD, D_HEAD), lambda b, h, n: ((b * N_HEADS + h) * nb + n, 0, 0))],
        out_shape=[jax.ShapeDtypeStruct((t, D_MODEL), F32),
                   jax.ShapeDtypeStruct((bsz * N_HEADS * nb, D_HEAD, D_HEAD), F32)],
        scratch_shapes=[pltpu.VMEM((D_HEAD, D_HEAD), F32), pltpu.VMEM((BLOCK, D_HEAD), F32),
                        pltpu.VMEM((BLOCK, D_HEAD), F32)],
        compiler_params=_params(("arbitrary", "arbitrary", "arbitrary")),
    )(qs, fs, vs, lam)


def _hgrn_bwd(qs, fs, vs, lam, sst, do, bsz, nb):
    t = qs.shape[0]

    def body(q_ref, f_ref, v_ref, lam_ref, sst_ref, do_ref, dq_ref, df_ref, dv_ref, dlam_ref,
             dst_scr, gsum_scr, b_scr, k_scr, sub_scr, dk_scr, db_scr):
        n = pl.program_id(2)
        chunk = nb - 1 - n

        @pl.when(n == 0)
        def _():
            dst_scr[...] = jnp.zeros_like(dst_scr)
            gsum_scr[...] = jnp.zeros_like(gsum_scr)
            dlam_ref[...] = jnp.zeros_like(dlam_ref)

        lam = lam_ref[...]
        sg, f, g, k, live = _hgrn_gates(f_ref[...], lam, chunk)
        b_scr[...] = _mm01_left(_sub_cumsum_matrix(), g)
        k_scr[...] = k
        st = sst_ref[0]
        for i in range(N_SUB):
            rs = pl.ds(i * SUB, SUB)
            sub_scr[i] = st
            bi, ki = b_scr[rs, :], k_scr[rs, :]
            b_end = bi[SUB - 1:SUB, :]
            st = st * jnp.exp(b_end) + _mm_tn(v_ref[rs, :], ki * jnp.exp(b_end - bi))
        row = _iota2((SUB, D_HEAD), 0)
        dst = dst_scr[...]
        for i in reversed(range(N_SUB)):
            rs = pl.ds(i * SUB, SUB)
            qi, vi, bi, ki, doi = q_ref[rs, :], v_ref[rs, :], b_scr[rs, :], k_scr[rs, :], do_ref[rs, :]
            eb = jnp.exp(bi)
            b_end = bi[SUB - 1:SUB, :]
            dec = jnp.exp(b_end - bi)
            qh16, kt16, do16, v16 = (qi * eb).astype(_MXU), (ki * dec).astype(_MXU), doi.astype(_MXU), vi.astype(_MXU)
            dv = _mm_nt_s(kt16, dst)
            dq_st = _mm_s(do16, sub_scr[i])
            dk_st = _mm_s(v16, dst)
            dst = dst * jnp.exp(b_end) + lax.dot_general(do16, qh16, _TN, preferred_element_type=F32)
            dq = jnp.zeros((SUB, D_HEAD), F32)
            dk = jnp.zeros((SUB, D_HEAD), F32)
            for s in range(SUB):
                ks, vrow = ki[s:s + 1, :], vi[s:s + 1, :]
                w = jnp.where(row >= s, jnp.exp(bi - bi[s:s + 1, :]), 0.0)
                qw = qi * w
                a_col = jnp.sum(qw * ks, axis=1, keepdims=True)
                da_col = jnp.sum(doi * vrow, axis=1, keepdims=True)
                dq = dq + da_col * (ks * w)
                dk = jnp.where(row == s, dk + jnp.sum(da_col * qw, axis=0, keepdims=True), dk)
                dv = jnp.where(row == s, dv + jnp.sum(a_col * doi, axis=0, keepdims=True), dv)
            dq_ref[rs, :] = dq + eb * dq_st
            dv_ref[rs, :] = dv
            dk_scr[rs, :] = dk + dec * dk_st
            db_scr[rs, :] = (qi * dq - ki * dk) + (qh16.astype(F32) * dq_st - kt16.astype(F32) * dk_st)
        dst_scr[...] = dst
        r, c = _iota2((BLOCK, BLOCK), 0), _iota2((BLOCK, BLOCK), 1)
        db = db_scr[...]
        dg = _mm01_left((c >= r).astype(_MXU), db) + gsum_scr[...]
        gsum_scr[...] = gsum_scr[...] + jnp.sum(db, axis=0, keepdims=True)
        dk = dk_scr[...]
        slope = (1.0 - lam) * sg * (1.0 - sg)
        df_ref[...] = jnp.where(live, dg * slope / f - dk * slope, 0.0)
        dl = jnp.where(live, (dg / f - dk) * (1.0 - sg), 0.0)
        dlam_ref[0] = dlam_ref[0] + jnp.sum(dl, axis=0, keepdims=True)

    blk = pl.BlockSpec((BLOCK, D_HEAD), lambda b, h, n: (b * nb + nb - 1 - n, h))
    return pl.pallas_call(
        body, name="hgrn_bwd", grid=(bsz, N_HEADS, nb),
        in_specs=[blk, blk, blk, pl.BlockSpec((1, D_HEAD), lambda b, h, n: (0, h)),
                  pl.BlockSpec((1, D_HEAD, D_HEAD), lambda b, h, n: ((b * N_HEADS + h) * nb + nb - 1 - n, 0, 0)),
                  blk],
        out_specs=[blk, blk, blk, pl.BlockSpec((1, 1, D_HEAD), lambda b, h, n: (b, 0, h))],
        out_shape=[jax.ShapeDtypeStruct((t, D_MODEL), F32)] * 3 + [jax.ShapeDtypeStruct((bsz, 1, D_MODEL), F32)],
        scratch_shapes=[pltpu.VMEM((D_HEAD, D_HEAD), F32), pltpu.VMEM((1, D_HEAD), F32),
                        pltpu.VMEM((BLOCK, D_HEAD), F32), pltpu.VMEM((BLOCK, D_HEAD), F32),
                        pltpu.VMEM((N_SUB, D_HEAD, D_HEAD), F32), pltpu.VMEM((BLOCK, D_HEAD), F32),
                        pltpu.VMEM((BLOCK, D_HEAD), F32)],
        compiler_params=_params(("arbitrary", "arbitrary", "arbitrary")),
    )(qs, fs, vs, lam, sst, do)


def _sb_tile(q16, k_blk, i, j):
    z = lax.dot_general(q16, k_blk.astype(_MXU), _NT, preferred_element_type=F32) * SB_SCALE
    qpos = i * BLOCK + _iota2((BLOCK, BLOCK), 0)
    kpos = j * BLOCK + _iota2((BLOCK, BLOCK), 1)
    valid = (kpos < qpos) & (kpos >= N_PAD)
    softplus = jnp.maximum(z, 0.0) + jnp.log(1.0 + jnp.exp(-jnp.abs(z)))
    return valid, jnp.where(valid, -softplus, 0.0), z - softplus


def _sb_fwd(qs, ks, vs, bsz, nb):
    t = qs.shape[0]
    lp = nb * BLOCK

    def body(q_ref, k_ref, v_ref, o_ref, c_ref):
        i = pl.program_id(2)
        q16 = q_ref[...].astype(_MXU)
        r, c = _iota2((BLOCK, BLOCK), 0), _iota2((BLOCK, BLOCK), 1)
        after = (r > c).astype(_MXU)

        def step(jj, carry):
            acc, later0 = carry
            j = i - jj
            ks_ = pl.ds(pl.multiple_of(j * BLOCK, BLOCK), BLOCK)
            valid, keep, log_beta = _sb_tile(q16, k_ref[ks_, :], i, j)
            later = later0 + _mm01_right(keep, after)
            a = jnp.where(valid, jnp.exp(log_beta + later), 0.0)
            acc = acc + _mm(a, v_ref[ks_, :])
            return acc, later0 + jnp.sum(keep, axis=1, keepdims=True)

        acc, tot = lax.fori_loop(0, i + 1, step, (jnp.zeros((BLOCK, D_HEAD), F32), jnp.zeros((BLOCK, 1), F32)))
        o_ref[...] = acc
        c_ref[...] = jnp.broadcast_to(tot, (BLOCK, D_HEAD))

    blk = pl.BlockSpec((BLOCK, D_HEAD), lambda b, h, i: (b * nb + i, h))
    seq = pl.BlockSpec((lp, D_HEAD), lambda b, h, i: (b, h))
    return pl.pallas_call(
        body, name="sb_fwd", grid=(bsz, N_HEADS, nb),
        in_specs=[blk, seq, seq], out_specs=[blk, blk],
        out_shape=[jax.ShapeDtypeStruct((t, D_MODEL), F32)] * 2,
        compiler_params=_params(("arbitrary", "arbitrary", "arbitrary")),
    )(qs, ks, vs)


def _sb_bwd(qs, ks, vs, o, ctot, do, bsz, nb):
    t = qs.shape[0]
    lp = nb * BLOCK

    def body(q_ref, k_ref, v_ref, o_ref, c_ref, do_ref, dq_ref, dk_ref, dv_ref):
        i = pl.program_id(2)

        @pl.when(i == 0)
        def _():
            dk_ref[...] = jnp.zeros_like(dk_ref)
            dv_ref[...] = jnp.zeros_like(dv_ref)

        q16 = q_ref[...].astype(_MXU)
        do16 = do_ref[...].astype(_MXU)
        ctot_col = c_ref[:, 0:1]
        r, c = _iota2((BLOCK, BLOCK), 0), _iota2((BLOCK, BLOCK), 1)
        upto = (r <= c).astype(_MXU)
        before = (r < c).astype(_MXU)

        def step(j, carry):
            dq, keep_pre, g_pre = carry
            ks_ = pl.ds(pl.multiple_of(j * BLOCK, BLOCK), BLOCK)
            k_blk, v_blk = k_ref[ks_, :], v_ref[ks_, :]
            valid, keep, log_beta = _sb_tile(q16, k_blk, i, j)
            later = ctot_col - keep_pre - _mm01_right(keep, upto)
            a = jnp.where(valid, jnp.exp(log_beta + later), 0.0)
            da = lax.dot_general(do16, v_blk.astype(_MXU), _NT, preferred_element_type=F32)
            g = a * da
            g_before = g_pre + _mm01_right(g, before)
            beta = jnp.exp(log_beta)
            dz = jnp.where(valid, g * (1.0 - beta) - beta * g_before, 0.0) * SB_SCALE
            dz16 = dz.astype(_MXU)
            dq = dq + jnp.dot(dz16, k_blk.astype(_MXU), preferred_element_type=F32)
            dk_ref[ks_, :] += lax.dot_general(dz16, q16, _TN, preferred_element_type=F32)
            dv_ref[ks_, :] += lax.dot_general(a.astype(_MXU), do16, _TN, preferred_element_type=F32)
            return (dq, keep_pre + jnp.sum(keep, axis=1, keepdims=True), g_pre + jnp.sum(g, axis=1, keepdims=True))

        zero_col = jnp.zeros((BLOCK, 1), F32)
        dq, _, _ = lax.fori_loop(0, i + 1, step, (jnp.zeros((BLOCK, D_HEAD), F32), zero_col, zero_col))
        dq_ref[...] = dq

    blk = pl.BlockSpec((BLOCK, D_HEAD), lambda b, h, i: (b * nb + i, h))
    seq = pl.BlockSpec((lp, D_HEAD), lambda b, h, i: (b, h))
    return pl.pallas_call(
        body, name="sb_bwd", grid=(bsz, N_HEADS, nb),
        in_specs=[blk, seq, seq, blk, blk, blk], out_specs=[blk, seq, seq],
        out_shape=[jax.ShapeDtypeStruct((t, D_MODEL), F32)] * 3,
        compiler_params=_params(("arbitrary", "arbitrary", "arbitrary")),
    )(qs, ks, vs, o, ctot, do)


def _head_norm(o, head_gain):
    outs, rs = [], []
    for h in range(N_HEADS):
        oh = o[:, h * D_HEAD:(h + 1) * D_HEAD]
        r = lax.rsqrt(jnp.mean(oh * oh, axis=-1, keepdims=True) + EPS)
        outs.append(oh * r)
        rs.append(r)
    return outs, rs


def _mix(o, gate, head_gain):
    if head_gain is None:
        on = o
    else:
        outs, _ = _head_norm(o, head_gain)
        on = jnp.concatenate([x * head_gain for x in outs], axis=1)
    return on, on * (gate * _sigmoid(gate))


def _out_fwd(o, gate, h_in, w_out, post_gain, head_gain, name):
    t = o.shape[0]
    tm = _row_tile(t, 256)
    has_head = head_gain is not None

    def body(*refs):
        if has_head:
            o_ref, g_ref, h_ref, w_ref, pg_ref, hg_ref, ho_ref, u_ref = refs
            hg = hg_ref[...]
        else:
            o_ref, g_ref, h_ref, w_ref, pg_ref, ho_ref, u_ref = refs
            hg = None
        _, mix = _mix(o_ref[...], g_ref[...], hg)
        u = jnp.dot(mix.astype(_MXU), w_ref[...], preferred_element_type=F32)
        u_ref[...] = u
        r = lax.rsqrt(jnp.mean(u * u, axis=-1, keepdims=True) + EPS)
        ho_ref[...] = h_ref[...] + u * r * pg_ref[...]

    row = pl.BlockSpec((tm, D_MODEL), lambda i: (i, 0))
    vec = pl.BlockSpec((1, D_MODEL), lambda i: (0, 0))
    in_specs = [row, row, row, pl.BlockSpec((D_MODEL, D_MODEL), lambda i: (0, 0)), vec]
    args = [o, gate, h_in, w_out, post_gain]
    if has_head:
        in_specs.append(pl.BlockSpec((1, D_HEAD), lambda i: (0, 0)))
        args.append(head_gain)
    return pl.pallas_call(
        body, name=name, grid=(t // tm,), in_specs=in_specs, out_specs=[row, row],
        out_shape=[jax.ShapeDtypeStruct((t, D_MODEL), F32)] * 2,
        compiler_params=_params(("arbitrary",)),
    )(*args)


def _out_bwd(dh, u, o, gate, w_out, post_gain, head_gain, name):
    t = o.shape[0]
    tm = _row_tile(t, 256)
    has_head = head_gain is not None

    def body(*refs):
        if has_head:
            dh_ref, u_ref, o_ref, g_ref, w_ref, pg_ref, hg_ref, do_ref, dg_ref, gw_ref, gp_ref, gh_ref = refs
            hg = hg_ref[...]
        else:
            dh_ref, u_ref, o_ref, g_ref, w_ref, pg_ref, do_ref, dg_ref, gw_ref, gp_ref = refs
            hg = None
        first = pl.program_id(0) == 0

        @pl.when(first)
        def _():
            gw_ref[...] = jnp.zeros_like(gw_ref)
            gp_ref[...] = jnp.zeros_like(gp_ref)
            if has_head:
                gh_ref[...] = jnp.zeros_like(gh_ref)

        dr, u, o, gate = dh_ref[...], u_ref[...], o_ref[...], g_ref[...]
        r = lax.rsqrt(jnp.mean(u * u, axis=-1, keepdims=True) + EPS)
        un = u * r
        gp_ref[...] += jnp.sum(dr * un, axis=0, keepdims=True)
        dun = dr * pg_ref[...]
        du = r * (dun - un * jnp.mean(dun * un, axis=-1, keepdims=True))
        on, mix = _mix(o, gate, hg)
        du16 = du.astype(_MXU)
        gw_ref[...] += lax.dot_general(mix.astype(_MXU), du16, _TN, preferred_element_type=F32)
        dmix = lax.dot_general(du16, w_ref[...], _NT, preferred_element_type=F32)
        sg = _sigmoid(gate)
        dg_ref[...] = dmix * on * (sg * (1.0 + gate * (1.0 - sg)))
        don = dmix * (gate * sg)
        if has_head:
            outs, rs = _head_norm(o, hg)
            gh = jnp.zeros((1, D_HEAD), F32)
            cols = []
            for h in range(N_HEADS):
                dn = don[:, h * D_HEAD:(h + 1) * D_HEAD]
                gh = gh + jnp.sum(dn * outs[h], axis=0, keepdims=True)
                dnn = dn * hg
                cols.append(rs[h] * (dnn - outs[h] * jnp.mean(dnn * outs[h], axis=-1, keepdims=True)))
            gh_ref[...] += gh
            do_ref[...] = jnp.concatenate(cols, axis=1)
        else:
            do_ref[...] = don

    row = pl.BlockSpec((tm, D_MODEL), lambda i: (i, 0))
    vec = pl.BlockSpec((1, D_MODEL), lambda i: (0, 0))
    mat = pl.BlockSpec((D_MODEL, D_MODEL), lambda i: (0, 0))
    in_specs = [row, row, row, row, mat, vec]
    args = [dh, u, o, gate, w_out, post_gain]
    out_specs = [row, row, mat, vec]
    out_shape = [jax.ShapeDtypeStruct((t, D_MODEL), F32)] * 2 + [jax.ShapeDtypeStruct((D_MODEL, D_MODEL), F32),
                                                                  jax.ShapeDtypeStruct((1, D_MODEL), F32)]
    if has_head:
        in_specs.append(pl.BlockSpec((1, D_HEAD), lambda i: (0, 0)))
        args.append(head_gain)
        out_specs.append(pl.BlockSpec((1, D_HEAD), lambda i: (0, 0)))
        out_shape.append(jax.ShapeDtypeStruct((1, D_HEAD), F32))
    return pl.pallas_call(
        body, name=name, grid=(t // tm,), in_specs=in_specs, out_specs=out_specs, out_shape=out_shape,
        compiler_params=_params(("arbitrary",)),
    )(*args)


def _proj_bwd(ds, w4, h_in, gain, dh_out, name):
    t = h_in.shape[0]
    tm = _row_tile(t, 256)

    def body(d0, d1, d2, d3, w_ref, h_ref, g_ref, dho_ref, dhi_ref, gg_ref):
        @pl.when(pl.program_id(0) == 0)
        def _():
            gg_ref[...] = jnp.zeros_like(gg_ref)

        dy = jnp.zeros((tm, D_MODEL), F32)
        for j, d in enumerate((d0, d1, d2, d3)):
            dy = dy + lax.dot_general(d[...].astype(_MXU), w_ref[j], _NT, preferred_element_type=F32)
        x = h_ref[...]
        r = lax.rsqrt(jnp.mean(x * x, axis=-1, keepdims=True) + EPS)
        xn = x * r
        gg_ref[...] += jnp.sum(dy * xn, axis=0, keepdims=True)
        dxn = dy * g_ref[...]
        dhi_ref[...] = dho_ref[...] + r * (dxn - xn * jnp.mean(dxn * xn, axis=-1, keepdims=True))

    row = pl.BlockSpec((tm, D_MODEL), lambda i: (i, 0))
    vec = pl.BlockSpec((1, D_MODEL), lambda i: (0, 0))
    return pl.pallas_call(
        body, name=name, grid=(t // tm,),
        in_specs=[row] * 4 + [pl.BlockSpec((4, D_MODEL, D_MODEL), lambda i: (0, 0, 0)), row, vec, row],
        out_specs=[row, vec],
        out_shape=[jax.ShapeDtypeStruct((t, D_MODEL), F32), jax.ShapeDtypeStruct((1, D_MODEL), F32)],
        compiler_params=_params(("arbitrary",)),
    )(*ds, w4, h_in, gain, dh_out)


def _weight_grad(y, d, name):
    t = y.shape[0]
    tk = _row_tile(t, 512)

    def body(y_ref, d_ref, g_ref):
        @pl.when(pl.program_id(0) == 0)
        def _():
            g_ref[...] = jnp.zeros_like(g_ref)

        g_ref[...] += lax.dot_general(y_ref[...], d_ref[...].astype(_MXU), _TN, preferred_element_type=F32)

    row = pl.BlockSpec((tk, D_MODEL), lambda i: (i, 0))
    return pl.pallas_call(
        body, name=name, grid=(t // tk,), in_specs=[row, row],
        out_specs=pl.BlockSpec((D_MODEL, D_MODEL), lambda i: (0, 0)),
        out_shape=jax.ShapeDtypeStruct((D_MODEL, D_MODEL), F32),
        compiler_params=_params(("arbitrary",)),
    )(y, d)


def _loss_head(h, target, bsz, nb):
    t = h.shape[0]

    def body(h_ref, t_ref, dh_ref, l_ref):
        i = pl.program_id(1)

        @pl.when((pl.program_id(0) == 0) & (i == 0))
        def _():
            l_ref[...] = jnp.zeros_like(l_ref)

        @pl.when(i == 0)
        def _():
            dh_ref[...] = jnp.zeros_like(dh_ref)

        @pl.when(i > 0)
        def _():
            e = h_ref[...] - t_ref[...]
            dh_ref[...] = e * (1.0 / D_MODEL)
            l_ref[...] += jnp.sum(e * e) * (0.5 / D_MODEL)

    return pl.pallas_call(
        body, name="loss_head", grid=(bsz, nb),
        in_specs=[pl.BlockSpec((BLOCK, D_MODEL), lambda b, i: (b * nb + i, 0)),
                  pl.BlockSpec((BLOCK, D_MODEL), lambda b, i: (b * (nb - 1) + jnp.maximum(i - 1, 0), 0))],
        out_specs=[pl.BlockSpec((BLOCK, D_MODEL), lambda b, i: (b * nb + i, 0)),
                   pl.BlockSpec((8, 128), lambda b, i: (0, 0))],
        out_shape=[jax.ShapeDtypeStruct((t, D_MODEL), F32), jax.ShapeDtypeStruct((8, 128), F32)],
        compiler_params=_params(("arbitrary", "arbitrary")),
    )(h, target)


def _local_step(x, target, meta, pre_norm, post_norm, lam, head_gain, hw_in, hw_out, sw_in, sw_out):
    bsz, seq, _ = x.shape
    nb = seq // BLOCK + 1
    lp = nb * BLOCK
    t = bsz * lp
    front = jnp.concatenate([jnp.zeros((N_PAD, D_MODEL), F32), meta], axis=0)
    h0 = jnp.concatenate([jnp.broadcast_to(front[None], (bsz, BLOCK, D_MODEL)), x], axis=1).reshape(t, D_MODEL)
    pre0, pre1, post0, post1 = pre_norm[0:1], pre_norm[1:2], post_norm[0:1], post_norm[1:2]

    y0, q0, f0, v0, g0 = _norm_proj(h0, pre0, hw_in, "norm_proj_hgrn")
    o0, sst = _hgrn_fwd(q0, f0, v0, lam, bsz, nb)
    h1, u0 = _out_fwd(o0, g0, h0, hw_out, post0, head_gain, "out_fwd_hgrn")
    y1, q1, k1, v1, g1 = _norm_proj(h1, pre1, sw_in, "norm_proj_sb")
    o1, ctot = _sb_fwd(q1, k1, v1, bsz, nb)
    h2, u1 = _out_fwd(o1, g1, h1, sw_out, post1, None, "out_fwd_sb")

    dh2, loss_blk = _loss_head(h2, target.reshape(bsz * seq, D_MODEL), bsz, nb)

    do1, dg1, g_sw_out, g_post1 = _out_bwd(dh2, u1, o1, g1, sw_out, post1, None, "out_bwd_sb")
    dq1, dk1, dv1 = _sb_bwd(q1, k1, v1, o1, ctot, do1, bsz, nb)
    ds1 = (dq1, dk1, dv1, dg1)
    dh1, g_pre1 = _proj_bwd(ds1, sw_in, h1, pre1, dh2, "proj_bwd_sb")
    g_sw_in = jnp.stack([_weight_grad(y1, d, "wgrad_sb_%d" % j) for j, d in enumerate(ds1)])

    do0, dg0, g_hw_out, g_post0, g_head = _out_bwd(dh1, u0, o0, g0, hw_out, post0, head_gain, "out_bwd_hgrn")
    dq0, df0, dv0, dlam = _hgrn_bwd(q0, f0, v0, lam, sst, do0, bsz, nb)
    ds0 = (dq0, df0, dv0, dg0)
    dh0, g_pre0 = _proj_bwd(ds0, hw_in, h0, pre0, dh1, "proj_bwd_hgrn")
    g_hw_in = jnp.stack([_weight_grad(y0, d, "wgrad_hgrn_%d" % j) for j, d in enumerate(ds0)])

    dh0 = dh0.reshape(bsz, lp, D_MODEL)
    grad_x = dh0[:, BLOCK:, :]
    g_meta = jnp.sum(dh0[:, N_PAD:BLOCK, :], axis=0)
    g_lam = jnp.sum(dlam, axis=0)
    small = jnp.concatenate([g_pre0, g_pre1, g_post0, g_post1, g_lam, g_lam,
                             jnp.pad(g_head, ((0, 0), (0, D_MODEL - D_HEAD))), g_meta,
                             jnp.zeros((SMALL_ROWS - 23, D_MODEL), F32)], axis=0)
    grads = dict(hw_in=g_hw_in, sw_in=g_sw_in, hw_out=g_hw_out.reshape(N_CHIPS, D_MODEL // N_CHIPS, D_MODEL),
                 sw_out=g_sw_out.reshape(N_CHIPS, D_MODEL // N_CHIPS, D_MODEL), small=small)
    return loss_blk, grad_x, grads


def _place():
    x, y, c = lax.axis_index("x"), lax.axis_index("y"), lax.axis_index("c")
    return x, y, c


def _gather_weights(hw_in, sw_in, hw_out, sw_out, meta):
    def body(hi_ref, si_ref, ho_ref, so_ref, m_ref, ghi, gsi, gho, gso, gm, send_sems, recv_sems):
        x, y, c = _place()
        me = 2 * x + y
        ghi[me] = hi_ref[0].astype(_MXU)
        gsi[me] = si_ref[0].astype(_MXU)
        gho[me] = ho_ref[0].astype(_MXU)
        gso[me] = so_ref[0].astype(_MXU)
        gm[me] = m_ref[...]
        outs = (ghi, gsi, gho, gso, gm)
        peers = [(1 - x, y), (x, 1 - y), (1 - x, 1 - y)]

        def copy(r, a, slot, to):
            return pltpu.make_async_remote_copy(
                src_ref=outs[a].at[slot], dst_ref=outs[a].at[slot], send_sem=send_sems.at[r * 5 + a],
                recv_sem=recv_sems.at[r * 5 + a], device_id=to, device_id_type=MESH)

        sends = [copy(r, a, me, (px, py, c)) for r, (px, py) in enumerate(peers) for a in range(5)]
        for cp in sends:
            cp.start()
        for r, (px, py) in enumerate(peers):
            for a in range(5):
                copy(r, a, 2 * px + py, (px, py, c)).wait_recv()
        for cp in sends:
            cp.wait_send()

    d4 = D_MODEL // N_CHIPS
    vm = pl.BlockSpec(memory_space=pltpu.VMEM)
    return pl.pallas_call(
        body, name="gather_weights",
        in_specs=[vm] * 5, out_specs=[vm] * 5,
        out_shape=[jax.ShapeDtypeStruct((N_CHIPS, D_MODEL, D_MODEL), _MXU)] * 2
        + [jax.ShapeDtypeStruct((N_CHIPS, d4, D_MODEL), _MXU)] * 2
        + [jax.ShapeDtypeStruct((N_CHIPS, N_META, d4), F32)],
        scratch_shapes=[pltpu.SemaphoreType.DMA((15,)), pltpu.SemaphoreType.DMA((15,))],
        compiler_params=pltpu.CompilerParams(vmem_limit_bytes=VMEM_LIMIT),
    )(hw_in, sw_in, hw_out, sw_out, meta)


def _scatter_grads(g):
    names = ("hw_in", "sw_in", "hw_out", "sw_out")

    def body(a0, a1, a2, a3, sm, l0, l1, l2, l3, lsm, send_sems, recv_sems, small_send, small_recv, local_sems):
        x, y, c = _place()
        me = 2 * x + y
        srcs, lands = (a0, a1, a2, a3), (l0, l1, l2, l3)
        peers = [(1 - x, y), (x, 1 - y), (1 - x, 1 - y)]
        local = [pltpu.make_async_copy(srcs[a].at[me], lands[a].at[me], local_sems.at[a]) for a in range(4)]
        local.append(pltpu.make_async_copy(sm, lsm.at[4 * x + 2 * y + c], local_sems.at[4]))
        for cp in local:
            cp.start()

        def copy(r, a, src_slot, dst_slot, to):
            return pltpu.make_async_remote_copy(
                src_ref=srcs[a].at[src_slot], dst_ref=lands[a].at[dst_slot], send_sem=send_sems.at[r * 4 + a],
                recv_sem=recv_sems.at[r * 4 + a], device_id=to, device_id_type=MESH)

        def small_copy(rel, src_dev, to):
            return pltpu.make_async_remote_copy(
                src_ref=sm, dst_ref=lsm.at[src_dev], send_sem=small_send.at[rel - 1], recv_sem=small_recv.at[rel - 1],
                device_id=to, device_id_type=MESH)

        sends = [copy(r, a, 2 * px + py, me, (px, py, c)) for r, (px, py) in enumerate(peers) for a in range(4)]
        flip = lambda bit, v: 1 - v if bit else v
        rels = [(rel, flip(rel & 4, x), flip(rel & 2, y), flip(rel & 1, c)) for rel in range(1, N_DEV)]
        sends += [small_copy(rel, 4 * x + 2 * y + c, (px, py, pc)) for rel, px, py, pc in rels]
        for cp in sends:
            cp.start()
        for r, (px, py) in enumerate(peers):
            for a in range(4):
                copy(r, a, me, 2 * px + py, (px, py, c)).wait_recv()
        for rel, px, py, pc in rels:
            small_copy(rel, 4 * px + 2 * py + pc, (px, py, pc)).wait_recv()
        for cp in sends:
            cp.wait_send()
        for cp in local:
            cp.wait()

    hbm = pl.BlockSpec(memory_space=pl.ANY)
    args = [g[n] for n in names] + [g["small"]]
    return pl.pallas_call(
        body, name="scatter_grads",
        in_specs=[hbm] * 5, out_specs=[hbm] * 5,
        out_shape=[jax.ShapeDtypeStruct(a.shape, F32) for a in args[:4]]
        + [jax.ShapeDtypeStruct((N_DEV, SMALL_ROWS, D_MODEL), F32)],
        scratch_shapes=[pltpu.SemaphoreType.DMA((12,)), pltpu.SemaphoreType.DMA((12,)),
                        pltpu.SemaphoreType.DMA((N_DEV - 1,)), pltpu.SemaphoreType.DMA((N_DEV - 1,)),
                        pltpu.SemaphoreType.DMA((5,))],
    )(*args)


def _sum_slots(land, name):
    n, rows, _ = land.shape
    tm = rows if rows < 256 else 256

    def body(l_ref, o_ref):
        acc = l_ref[0]
        for k in range(1, n):
            acc = acc + l_ref[k]
        o_ref[...] = acc

    return pl.pallas_call(
        body, name=name, grid=(rows // tm,),
        in_specs=[pl.BlockSpec((n, tm, D_MODEL), lambda i: (0, i, 0))],
        out_specs=pl.BlockSpec((tm, D_MODEL), lambda i: (i, 0)),
        out_shape=jax.ShapeDtypeStruct((rows, D_MODEL), F32),
        compiler_params=_params(("arbitrary",)),
    )(land)


def _swap_with_sibling(parts):
    def body(a0, a1, a2, a3, b0, b1, b2, b3, send_sems, recv_sems):
        x, y, c = _place()
        copies = [pltpu.make_async_remote_copy(src_ref=s, dst_ref=d, send_sem=send_sems.at[a], recv_sem=recv_sems.at[a],
                                               device_id=(x, y, 1 - c), device_id_type=MESH)
                  for a, (s, d) in enumerate(zip((a0, a1, a2, a3), (b0, b1, b2, b3)))]
        for cp in copies:
            cp.start()
        for cp in copies:
            cp.wait()

    hbm = pl.BlockSpec(memory_space=pl.ANY)
    return pl.pallas_call(
        body, name="swap_with_sibling", in_specs=[hbm] * 4, out_specs=[hbm] * 4,
        out_shape=[jax.ShapeDtypeStruct(p.shape, F32) for p in parts],
        scratch_shapes=[pltpu.SemaphoreType.DMA((4,)), pltpu.SemaphoreType.DMA((4,))],
    )(*parts)


def _adamw_math(w, g, m, v):
    m = ADAM_B1 * m + (1.0 - ADAM_B1) * g
    v = ADAM_B2 * v + (1.0 - ADAM_B2) * (g * g)
    m_hat = m / (1.0 - ADAM_B1 ** ADAM_STEP)
    v_hat = v / (1.0 - ADAM_B2 ** ADAM_STEP)
    delta = -ADAM_LR * (m_hat / (jnp.sqrt(v_hat) + ADAM_EPS) + ADAM_WD * w)
    return delta, m, v


def _adamw(w, g_parts, m, v, name):
    rows, cols = w.shape
    tm = rows if rows < 256 else 256
    n = len(g_parts)

    def body(*refs):
        w_ref, m_ref, v_ref = refs[n:n + 3]
        g_ref, d_ref, nm_ref, nv_ref = refs[n + 3:]
        g = refs[0][...]
        for p in refs[1:n]:
            g = g + p[...]
        g_ref[...] = g
        d_ref[...], nm_ref[...], nv_ref[...] = _adamw_math(w_ref[...], g, m_ref[...], v_ref[...])

    blk = pl.BlockSpec((tm, cols), lambda i: (i, 0))
    return pl.pallas_call(
        body, name=name, grid=(rows // tm,), in_specs=[blk] * (n + 3), out_specs=[blk] * 4,
        out_shape=[jax.ShapeDtypeStruct((rows, cols), F32)] * 4,
        compiler_params=_params(("arbitrary",)),
    )(*g_parts, w, m, v)


def _lam_of(hgrn_lb):
    def body(lb_ref, o_ref):
        lb = lb_ref[...]
        e = jnp.exp(lb - jnp.max(lb, axis=0, keepdims=True))
        o_ref[...] = e[0:1, :] / jnp.sum(e, axis=0, keepdims=True)

    return pl.pallas_call(body, name="lam_of", out_shape=jax.ShapeDtypeStruct((1, D_MODEL), F32))(hgrn_lb)


def _small_grads(land_small, lam):
    def body(l_ref, lam_ref, o_ref):
        acc = l_ref[0]
        for k in range(1, N_DEV):
            acc = acc + l_ref[k]
        p = lam_ref[...]
        slope = p * (1.0 - p)
        row = _iota2((SMALL_ROWS, D_MODEL), 0)
        o_ref[...] = acc * jnp.where(row == 4, slope, jnp.where(row == 5, -slope, 1.0))

    return pl.pallas_call(body, name="small_grads",
                          out_shape=jax.ShapeDtypeStruct((SMALL_ROWS, D_MODEL), F32))(land_small, lam)


def kernel(x, meta_tokens, pre_norm, post_norm, hgrn_w_in, hgrn_lb, hgrn_out_norm, hgrn_w_out, sb_w_in, sb_w_out, loss_target, m_meta_tokens, m_pre_norm, m_post_norm, m_hgrn_w_in, m_hgrn_lb, m_hgrn_out_norm, m_hgrn_w_out, m_sb_w_in, m_sb_w_out, v_meta_tokens, v_pre_norm, v_post_norm, v_hgrn_w_in, v_hgrn_lb, v_hgrn_out_norm, v_hgrn_w_out, v_sb_w_in, v_sb_w_out):
    d4 = D_MODEL // N_CHIPS
    chip = 2 * lax.axis_index("x") + lax.axis_index("y")
    hw_in, sw_in, hw_out, sw_out, meta4 = _gather_weights(hgrn_w_in, sb_w_in, hgrn_w_out, sb_w_out, meta_tokens)
    meta = meta4.transpose(1, 0, 2).reshape(N_META, D_MODEL)
    lam = _lam_of(hgrn_lb)
    loss_blk, grad_x, grads = _local_step(
        x, loss_target, meta, pre_norm, post_norm, lam, hgrn_out_norm,
        hw_in, hw_out.reshape(D_MODEL, D_MODEL), sw_in, sw_out.reshape(D_MODEL, D_MODEL))
    loss = lax.psum(loss_blk[0, 0], ("x", "y", "c"))

    l_hi, l_si, l_ho, l_so, l_small = _scatter_grads(grads)
    parts = [_sum_slots(l_hi, "sum_hw_in"), _sum_slots(l_si, "sum_sw_in"),
             _sum_slots(l_ho, "sum_hw_out"), _sum_slots(l_so, "sum_sw_out")]
    sib = _swap_with_sibling(parts)
    small = _small_grads(l_small, lam)

    res = {}
    res["hgrn_w_in"] = _adamw(hgrn_w_in[0], [parts[0], sib[0]], m_hgrn_w_in[0], v_hgrn_w_in[0], "adamw_hw_in")
    res["sb_w_in"] = _adamw(sb_w_in[0], [parts[1], sib[1]], m_sb_w_in[0], v_sb_w_in[0], "adamw_sw_in")
    res["hgrn_w_out"] = _adamw(hgrn_w_out[0], [parts[2], sib[2]], m_hgrn_w_out[0], v_hgrn_w_out[0], "adamw_hw_out")
    res["sb_w_out"] = _adamw(sb_w_out[0], [parts[3], sib[3]], m_sb_w_out[0], v_sb_w_out[0], "adamw_sw_out")
    res["pre_norm"] = _adamw(pre_norm, [small[0:2]], m_pre_norm, v_pre_norm, "adamw_pre")
    res["post_norm"] = _adamw(post_norm, [small[2:4]], m_post_norm, v_post_norm, "adamw_post")
    res["hgrn_lb"] = _adamw(hgrn_lb, [small[4:6]], m_hgrn_lb, v_hgrn_lb, "adamw_lb")
    res["hgrn_out_norm"] = _adamw(hgrn_out_norm, [small[6:7, :D_HEAD]], m_hgrn_out_norm, v_hgrn_out_norm, "adamw_head")
    g_meta = lax.dynamic_slice_in_dim(small[7:7 + N_META], chip * d4, d4, axis=1)
    res["meta_tokens"] = _adamw(meta_tokens, [g_meta], m_meta_tokens, v_meta_tokens, "adamw_meta")
    for n in ("hgrn_w_in", "hgrn_w_out", "sb_w_in", "sb_w_out"):
        res[n] = tuple(a[None] for a in res[n])
    order = ("meta_tokens", "pre_norm", "post_norm", "hgrn_w_in", "hgrn_lb", "hgrn_out_norm", "hgrn_w_out",
             "sb_w_in", "sb_w_out")
    return (loss, grad_x, *[res[n][0] for n in order], *[res[n][1] for n in order],
            *[res[n][2] for n in order], *[res[n][3] for n in order])
```

```python
import functools

import jax
import jax.numpy as jnp
from jax import lax
from jax.experimental import pallas as pl
from jax.experimental.pallas import tpu as pltpu

F32 = jnp.float32
_MXU = jnp.bfloat16

D_MODEL = 1024
N_HEADS = 8
D_HEAD = 128
BLOCK = 128
N_META = 16
TILE = 256
N_PAD = TILE - N_META
UNDERFLOW = -104.0
SUB = 16
N_SUB = BLOCK // SUB
EPS = 1e-6
SB_SCALE = D_HEAD ** -0.5
ADAM_LR, ADAM_B1, ADAM_B2, ADAM_EPS, ADAM_WD, ADAM_STEP = 0.001, 0.9, 0.999, 1e-08, 0.01, 10
N_CHIPS = 4
N_DEV = 8
SMALL_ROWS = 24
VMEM_LIMIT = 56 * 1024 * 1024
MESH = pl.DeviceIdType.MESH

_NT = (((1,), (1,)), ((), ()))
_TN = (((0,), (0,)), ((), ()))


def _mm(a, b):
    return jnp.dot(a.astype(_MXU), b.astype(_MXU), preferred_element_type=F32)


def _mm_nt(a, b):
    return lax.dot_general(a.astype(_MXU), b.astype(_MXU), _NT, preferred_element_type=F32)


def _mm_tn(a, b):
    return lax.dot_general(a.astype(_MXU), b.astype(_MXU), _TN, preferred_element_type=F32)


def _split2(x):
    hi = x.astype(_MXU)
    return hi, (x - hi.astype(F32)).astype(_MXU)


def _mm_s(a16, state):
    hi, lo = _split2(state)
    return jnp.dot(a16, hi, preferred_element_type=F32) + jnp.dot(a16, lo, preferred_element_type=F32)


def _mm_nt_s(a16, state):
    hi, lo = _split2(state)
    return (lax.dot_general(a16, hi, _NT, preferred_element_type=F32)
            + lax.dot_general(a16, lo, _NT, preferred_element_type=F32))


def _split3(x):
    hi = x.astype(_MXU)
    r1 = x - hi.astype(F32)
    mid = r1.astype(_MXU)
    lo = (r1 - mid.astype(F32)).astype(_MXU)
    return hi, mid, lo


def _mm01_right(x, m01):
    hi, lo = _split2(x)
    return jnp.dot(hi, m01, preferred_element_type=F32) + jnp.dot(lo, m01, preferred_element_type=F32)


def _mm01_left(m01, x):
    hi, mid, lo = _split3(x)
    d = functools.partial(jnp.dot, preferred_element_type=F32)
    return d(m01, hi) + d(m01, mid) + d(m01, lo)


def _iota2(shape, dim):
    return lax.broadcasted_iota(jnp.int32, shape, dim)


def _row_tile(total, pref):
    t = pref
    while total % t:
        t -= BLOCK
    return t


def _params(sem, limit=VMEM_LIMIT):
    return pltpu.CompilerParams(dimension_semantics=sem, vmem_limit_bytes=limit)


def _sigmoid(x):
    return 1.0 / (1.0 + jnp.exp(-x))


def _norm_proj(h, gain, w4, name):
    t = h.shape[0]
    tm = _row_tile(t, 256)

    def body(h_ref, g_ref, w_ref, y_ref, s0, s1, s2, s3):
        x = h_ref[...]
        r = lax.rsqrt(jnp.mean(x * x, axis=-1, keepdims=True) + EPS)
        y = (x * r * g_ref[...]).astype(_MXU)
        y_ref[...] = y
        for j, s in enumerate((s0, s1, s2, s3)):
            s[...] = jnp.dot(y, w_ref[j], preferred_element_type=F32)

    row = pl.BlockSpec((tm, D_MODEL), lambda i: (i, 0))
    return pl.pallas_call(
        body, name=name, grid=(t // tm,),
        in_specs=[row, pl.BlockSpec((1, D_MODEL), lambda i: (0, 0)),
                  pl.BlockSpec((4, D_MODEL, D_MODEL), lambda i: (0, 0, 0))],
        out_specs=[row] * 5,
        out_shape=[jax.ShapeDtypeStruct((t, D_MODEL), _MXU)] + [jax.ShapeDtypeStruct((t, D_MODEL), F32)] * 4,
        compiler_params=_params(("arbitrary",)),
    )(h, gain, w4)


def _hgrn_gates(fz, lam, chunk):
    pos = chunk * BLOCK + _iota2((BLOCK, D_HEAD), 0)
    live = pos >= N_PAD
    sg = _sigmoid(fz)
    f = lam + (1.0 - lam) * sg
    g = jnp.where(live, jnp.log(f), 0.0)
    k = jnp.where(live, (1.0 - lam) * (1.0 - sg), 0.0)
    return sg, f, g, k, live


def _sub_cumsum_matrix():
    r, c = _iota2((BLOCK, BLOCK), 0), _iota2((BLOCK, BLOCK), 1)
    return ((r // SUB == c // SUB) & (c <= r)).astype(_MXU)


def _hgrn_fwd(qs, fs, vs, lam, bsz, nb):
    t = qs.shape[0]

    def body(q_ref, f_ref, v_ref, lam_ref, o_ref, sst_ref, st_scr, b_scr, k_scr):
        n = pl.program_id(2)

        @pl.when(n == 0)
        def _():
            st_scr[...] = jnp.zeros_like(st_scr)

        sst_ref[0] = st_scr[...]
        _, _, g, k, _ = _hgrn_gates(f_ref[...], lam_ref[...], n)
        b_scr[...] = _mm01_left(_sub_cumsum_matrix(), g)
        k_scr[...] = k
        row = _iota2((SUB, D_HEAD), 0)
        st = st_scr[...]
        for i in range(N_SUB):
            rs = pl.ds(i * SUB, SUB)
            qi, vi, bi, ki = q_ref[rs, :], v_ref[rs, :], b_scr[rs, :], k_scr[rs, :]
            acc = _mm_nt_s((qi * jnp.exp(bi)).astype(_MXU), st)
            for s in range(SUB):
                w = jnp.where(row >= s, jnp.exp(bi - bi[s:s + 1, :]), 0.0)
                col = jnp.sum(qi * w * ki[s:s + 1, :], axis=1, keepdims=True)
                acc = acc + col * vi[s:s + 1, :]
            o_ref[rs, :] = acc
            b_end = bi[SUB - 1:SUB, :]
            st = st * jnp.exp(b_end) + _mm_tn(vi, ki * jnp.exp(b_end - bi))
        st_scr[...] = st

    blk = pl.BlockSpec((BLOCK, D_HEAD), lambda b, h, n: (b * nb + n, h))
    return pl.pallas_call(
        body, name="hgrn_fwd", grid=(bsz, N_HEADS, nb),
        in_specs=[blk, blk, blk, pl.BlockSpec((1, D_HEAD), lambda b, h, n: (0, h))],
        out_specs=[blk, pl.BlockSpec((1, D_HEAD, D_HEAD), lambda b, h, n: ((b * N_HEADS + h) * nb + n, 0, 0))],
        out_shape=[jax.ShapeDtypeStruct((t, D_MODEL), F32),
                   jax.ShapeDtypeStruct((bsz * N_HEADS * nb, D_HEAD, D_HEAD), F32)],
        scratch_shapes=[pltpu.VMEM((D_HEAD, D_HEAD), F32), pltpu.VMEM((BLOCK, D_HEAD), F32),
                        pltpu.VMEM((BLOCK, D_HEAD), F32)],
        compiler_params=_params(("arbitrary", "arbitrary", "arbitrary")),
    )(qs, fs, vs, lam)


def _hgrn_bwd(qs, fs, vs, lam, sst, do, bsz, nb):
    t = qs.shape[0]

    def body(q_ref, f_ref, v_ref, lam_ref, sst_ref, do_ref, dq_ref, df_ref, dv_ref, dlam_ref,
             dst_scr, gsum_scr, b_scr, k_scr, sub_scr, dk_scr, db_scr):
        n = pl.program_id(2)
        chunk = nb - 1 - n

        @pl.when(n == 0)
        def _():
            dst_scr[...] = jnp.zeros_like(dst_scr)
            gsum_scr[...] = jnp.zeros_like(gsum_scr)
            dlam_ref[...] = jnp.zeros_like(dlam_ref)

        lam = lam_ref[...]
        sg, f, g, k, live = _hgrn_gates(f_ref[...], lam, chunk)
        b_scr[...] = _mm01_left(_sub_cumsum_matrix(), g)
        k_scr[...] = k
        st = sst_ref[0]
        for i in range(N_SUB):
            rs = pl.ds(i * SUB, SUB)
            sub_scr[i] = st
            bi, ki = b_scr[rs, :], k_scr[rs, :]
            b_end = bi[SUB - 1:SUB, :]
            st = st * jnp.exp(b_end) + _mm_tn(v_ref[rs, :], ki * jnp.exp(b_end - bi))
        row = _iota2((SUB, D_HEAD), 0)
        dst = dst_scr[...]
        for i in reversed(range(N_SUB)):
            rs = pl.ds(i * SUB, SUB)
            qi, vi, bi, ki, doi = q_ref[rs, :], v_ref[rs, :], b_scr[rs, :], k_scr[rs, :], do_ref[rs, :]
            eb = jnp.exp(bi)
            b_end = bi[SUB - 1:SUB, :]
            dec = jnp.exp(b_end - bi)
            qh16, kt16, do16, v16 = (qi * eb).astype(_MXU), (ki * dec).astype(_MXU), doi.astype(_MXU), vi.astype(_MXU)
            dv = _mm_nt_s(kt16, dst)
            dq_st = _mm_s(do16, sub_scr[i])
            dk_st = _mm_s(v16, dst)
            dst = dst * jnp.exp(b_end) + lax.dot_general(do16, qh16, _TN, preferred_element_type=F32)
            dq = jnp.zeros((SUB, D_HEAD), F32)
            dk = jnp.zeros((SUB, D_HEAD), F32)
            for s in range(SUB):
                ks, vrow = ki[s:s + 1, :], vi[s:s + 1, :]
                w = jnp.where(row >= s, jnp.exp(bi - bi[s:s + 1, :]), 0.0)
                qw = qi * w
                a_col = jnp.sum(qw * ks, axis=1, keepdims=True)
                da_col = jnp.sum(doi * vrow, axis=1, keepdims=True)
                dq = dq + da_col * (ks * w)
                dk = jnp.where(row == s, dk + jnp.sum(da_col * qw, axis=0, keepdims=True), dk)
                dv = jnp.where(row == s, dv + jnp.sum(a_col * doi, axis=0, keepdims=True), dv)
            dq_ref[rs, :] = dq + eb * dq_st
            dv_ref[rs, :] = dv
            dk_scr[rs, :] = dk + dec * dk_st
            db_scr[rs, :] = (qi * dq - ki * dk) + (qh16.astype(F32) * dq_st - kt16.astype(F32) * dk_st)
        dst_scr[...] = dst
        r, c = _iota2((BLOCK, BLOCK), 0), _iota2((BLOCK, BLOCK), 1)
        db = db_scr[...]
        dg = _mm01_left((c >= r).astype(_MXU), db) + gsum_scr[...]
        gsum_scr[...] = gsum_scr[...] + jnp.sum(db, axis=0, keepdims=True)
        dk = dk_scr[...]
        slope = (1.0 - lam) * sg * (1.0 - sg)
        df_ref[...] = jnp.where(live, dg * slope / f - dk * slope, 0.0)
        dl = jnp.where(live, (dg / f - dk) * (1.0 - sg), 0.0)
        dlam_ref[0] = dlam_ref[0] + jnp.sum(dl, axis=0, keepdims=True)

    blk = pl.BlockSpec((BLOCK, D_HEAD), lambda b, h, n: (b * nb + nb - 1 - n, h))
    return pl.pallas_call(
        body, name="hgrn_bwd", grid=(bsz, N_HEADS, nb),
        in_specs=[blk, blk, blk, pl.BlockSpec((1, D_HEAD), lambda b, h, n: (0, h)),
                  pl.BlockSpec((1, D_HEAD, D_HEAD), lambda b, h, n: ((b * N_HEADS + h) * nb + nb - 1 - n, 0, 0)),
                  blk],
        out_specs=[blk, blk, blk, pl.BlockSpec((1, 1, D_HEAD), lambda b, h, n: (b, 0, h))],
        out_shape=[jax.ShapeDtypeStruct((t, D_MODEL), F32)] * 3 + [jax.ShapeDtypeStruct((bsz, 1, D_MODEL), F32)],
        scratch_shapes=[pltpu.VMEM((D_HEAD, D_HEAD), F32), pltpu.VMEM((1, D_HEAD), F32),
                        pltpu.VMEM((BLOCK, D_HEAD), F32), pltpu.VMEM((BLOCK, D_HEAD), F32),
                        pltpu.VMEM((N_SUB, D_HEAD, D_HEAD), F32), pltpu.VMEM((BLOCK, D_HEAD), F32),
                        pltpu.VMEM((BLOCK, D_HEAD), F32)],
        compiler_params=_params(("arbitrary", "arbitrary", "arbitrary")),
    )(qs, fs, vs, lam, sst, do)


def _sb_tile(q16, k_blk, i, j):
    z = lax.dot_general(q16, k_blk.astype(_MXU), _NT, preferred_element_type=F32) * SB_SCALE
    qpos = i * TILE + _iota2((TILE, TILE), 0)
    kpos = j * TILE + _iota2((TILE, TILE), 1)
    valid = (kpos < qpos) & (kpos >= N_PAD)
    softplus = jnp.maximum(z, 0.0) + jnp.log(1.0 + jnp.exp(-jnp.abs(z)))
    return valid, jnp.where(valid, -softplus, 0.0), z - softplus


def _sb_fwd(qs, ks, vs, bsz, nq):
    t = qs.shape[0]
    lp = nq * TILE

    def body(q_ref, k_ref, v_ref, o_ref, c_ref, n_ref):
        b, h, i = pl.program_id(0), pl.program_id(1), pl.program_id(2)
        q16 = q_ref[...].astype(_MXU)
        r, c = _iota2((TILE, TILE), 0), _iota2((TILE, TILE), 1)
        after = (r > c).astype(_MXU)

        def more(carry):
            jj, _, _, top = carry
            return (jj <= i) & (top > UNDERFLOW)

        def step(carry):
            jj, acc, later0, _ = carry
            j = i - jj
            ks_ = pl.ds(pl.multiple_of(j * TILE, TILE), TILE)
            valid, keep, log_beta = _sb_tile(q16, k_ref[ks_, :], i, j)
            later = later0 + _mm01_right(keep, after)
            a = jnp.where(valid, jnp.exp(log_beta + later), 0.0)
            acc = acc + _mm(a, v_ref[ks_, :])
            later0 = later0 + jnp.sum(keep, axis=1, keepdims=True)
            return jj + 1, acc, later0, jnp.max(later0)

        init = (jnp.int32(0), jnp.zeros((TILE, D_HEAD), F32), jnp.zeros((TILE, 1), F32), jnp.float32(0.0))
        visited, acc, tot, _ = lax.while_loop(more, step, init)
        o_ref[...] = acc
        c_ref[...] = jnp.broadcast_to(tot, (TILE, D_HEAD))
        n_ref[(b * N_HEADS + h) * nq + i] = visited.astype(F32)

    blk = pl.BlockSpec((TILE, D_HEAD), lambda b, h, i: (b * nq + i, h))
    seq = pl.BlockSpec((lp, D_HEAD), lambda b, h, i: (b, h))
    return pl.pallas_call(
        body, name="sb_fwd", grid=(bsz, N_HEADS, nq),
        in_specs=[blk, seq, seq], out_specs=[blk, blk, pl.BlockSpec(memory_space=pltpu.SMEM)],
        out_shape=[jax.ShapeDtypeStruct((t, D_MODEL), F32)] * 2 + [jax.ShapeDtypeStruct((bsz * N_HEADS * nq,), F32)],
        compiler_params=_params(("arbitrary", "arbitrary", "arbitrary")),
    )(qs, ks, vs)


def _sb_bwd(qs, ks, vs, ctot, visited, do, bsz, nq):
    t = qs.shape[0]
    lp = nq * TILE

    def body(n_ref, q_ref, k_ref, v_ref, c_ref, do_ref, dq_ref, dk_ref, dv_ref):
        b, h, i = pl.program_id(0), pl.program_id(1), pl.program_id(2)

        @pl.when(i == 0)
        def _():
            dk_ref[...] = jnp.zeros_like(dk_ref)
            dv_ref[...] = jnp.zeros_like(dv_ref)

        q16 = q_ref[...].astype(_MXU)
        do16 = do_ref[...].astype(_MXU)
        ctot_col = c_ref[:, 0:1]
        r, c = _iota2((TILE, TILE), 0), _iota2((TILE, TILE), 1)
        upto = (r <= c).astype(_MXU)
        before = (r < c).astype(_MXU)
        first = jnp.maximum(i + 1 - n_ref[(b * N_HEADS + h) * nq + i].astype(jnp.int32), 0)

        def step(j, carry):
            dq, keep_pre, g_pre = carry
            ks_ = pl.ds(pl.multiple_of(j * TILE, TILE), TILE)
            k_blk, v_blk = k_ref[ks_, :], v_ref[ks_, :]
            valid, keep, log_beta = _sb_tile(q16, k_blk, i, j)
            later = ctot_col - keep_pre - _mm01_right(keep, upto)
            a = jnp.where(valid, jnp.exp(log_beta + later), 0.0)
            da = lax.dot_general(do16, v_blk.astype(_MXU), _NT, preferred_element_type=F32)
            g = a * da
            g_before = g_pre + _mm01_right(g, before)
            beta = jnp.exp(log_beta)
            dz = jnp.where(valid, g * (1.0 - beta) - beta * g_before, 0.0) * SB_SCALE
            dz16 = dz.astype(_MXU)
            dq = dq + jnp.dot(dz16, k_blk.astype(_MXU), preferred_element_type=F32)
            dk_ref[ks_, :] += lax.dot_general(dz16, q16, _TN, preferred_element_type=F32)
            dv_ref[ks_, :] += lax.dot_general(a.astype(_MXU), do16, _TN, preferred_element_type=F32)
            return (dq, keep_pre + jnp.sum(keep, axis=1, keepdims=True), g_pre + jnp.sum(g, axis=1, keepdims=True))

        zero_col = jnp.zeros((TILE, 1), F32)
        dq, _, _ = lax.fori_loop(first, i + 1, step, (jnp.zeros((TILE, D_HEAD), F32), zero_col, zero_col))
        dq_ref[...] = dq

    blk = pl.BlockSpec((TILE, D_HEAD), lambda b, h, i: (b * nq + i, h))
    seq = pl.BlockSpec((lp, D_HEAD), lambda b, h, i: (b, h))
    return pl.pallas_call(
        body, name="sb_bwd", grid=(bsz, N_HEADS, nq),
        in_specs=[pl.BlockSpec(memory_space=pltpu.SMEM), blk, seq, seq, blk, blk], out_specs=[blk, seq, seq],
        out_shape=[jax.ShapeDtypeStruct((t, D_MODEL), F32)] * 3,
        compiler_params=_params(("arbitrary", "arbitrary", "arbitrary")),
    )(visited, qs, ks, vs, ctot, do)


def _head_norm(o, head_gain):
    outs, rs = [], []
    for h in range(N_HEADS):
        oh = o[:, h * D_HEAD:(h + 1) * D_HEAD]
        r = lax.rsqrt(jnp.mean(oh * oh, axis=-1, keepdims=True) + EPS)
        outs.append(oh * r)
        rs.append(r)
    return outs, rs


def _mix(o, gate, head_gain):
    if head_gain is None:
        on = o
    else:
        outs, _ = _head_norm(o, head_gain)
        on = jnp.concatenate([x * head_gain for x in outs], axis=1)
    return on, on * (gate * _sigmoid(gate))


def _out_fwd(o, gate, h_in, w_out, post_gain, head_gain, name):
    t = o.shape[0]
    tm = _row_tile(t, 256)
    has_head = head_gain is not None

    def body(*refs):
        if has_head:
            o_ref, g_ref, h_ref, w_ref, pg_ref, hg_ref, ho_ref, u_ref = refs
            hg = hg_ref[...]
        else:
            o_ref, g_ref, h_ref, w_ref, pg_ref, ho_ref, u_ref = refs
            hg = None
        _, mix = _mix(o_ref[...], g_ref[...], hg)
        u = jnp.dot(mix.astype(_MXU), w_ref[...], preferred_element_type=F32)
        u_ref[...] = u
        r = lax.rsqrt(jnp.mean(u * u, axis=-1, keepdims=True) + EPS)
        ho_ref[...] = h_ref[...] + u * r * pg_ref[...]

    row = pl.BlockSpec((tm, D_MODEL), lambda i: (i, 0))
    vec = pl.BlockSpec((1, D_MODEL), lambda i: (0, 0))
    in_specs = [row, row, row, pl.BlockSpec((D_MODEL, D_MODEL), lambda i: (0, 0)), vec]
    args = [o, gate, h_in, w_out, post_gain]
    if has_head:
        in_specs.append(pl.BlockSpec((1, D_HEAD), lambda i: (0, 0)))
        args.append(head_gain)
    return pl.pallas_call(
        body, name=name, grid=(t // tm,), in_specs=in_specs, out_specs=[row, row],
        out_shape=[jax.ShapeDtypeStruct((t, D_MODEL), F32)] * 2,
        compiler_params=_params(("arbitrary",)),
    )(*args)


def _out_bwd(dh, u, o, gate, w_out, post_gain, head_gain, name):
    t = o.shape[0]
    tm = _row_tile(t, 256)
    has_head = head_gain is not None

    def body(*refs):
        if has_head:
            dh_ref, u_ref, o_ref, g_ref, w_ref, pg_ref, hg_ref, do_ref, dg_ref, gw_ref, gp_ref, gh_ref = refs
            hg = hg_ref[...]
        else:
            dh_ref, u_ref, o_ref, g_ref, w_ref, pg_ref, do_ref, dg_ref, gw_ref, gp_ref = refs
            hg = None
        first = pl.program_id(0) == 0

        @pl.when(first)
        def _():
            gw_ref[...] = jnp.zeros_like(gw_ref)
            gp_ref[...] = jnp.zeros_like(gp_ref)
            if has_head:
                gh_ref[...] = jnp.zeros_like(gh_ref)

        dr, u, o, gate = dh_ref[...], u_ref[...], o_ref[...], g_ref[...]
        r = lax.rsqrt(jnp.mean(u * u, axis=-1, keepdims=True) + EPS)
        un = u * r
        gp_ref[...] += jnp.sum(dr * un, axis=0, keepdims=True)
        dun = dr * pg_ref[...]
        du = r * (dun - un * jnp.mean(dun * un, axis=-1, keepdims=True))
        on, mix = _mix(o, gate, hg)
        du16 = du.astype(_MXU)
        gw_ref[...] += lax.dot_general(mix.astype(_MXU), du16, _TN, preferred_element_type=F32)
        dmix = lax.dot_general(du16, w_ref[...], _NT, preferred_element_type=F32)
        sg = _sigmoid(gate)
        dg_ref[...] = dmix * on * (sg * (1.0 + gate * (1.0 - sg)))
        don = dmix * (gate * sg)
        if has_head:
            outs, rs = _head_norm(o, hg)
            gh = jnp.zeros((1, D_HEAD), F32)
            cols = []
            for h in range(N_HEADS):
                dn = don[:, h * D_HEAD:(h + 1) * D_HEAD]
                gh = gh + jnp.sum(dn * outs[h], axis=0, keepdims=True)
                dnn = dn * hg
                cols.append(rs[h] * (dnn - outs[h] * jnp.mean(dnn * outs[h], axis=-1, keepdims=True)))
            gh_ref[...] += gh
            do_ref[...] = jnp.concatenate(cols, axis=1)
        else:
            do_ref[...] = don

    row = pl.BlockSpec((tm, D_MODEL), lambda i: (i, 0))
    vec = pl.BlockSpec((1, D_MODEL), lambda i: (0, 0))
    mat = pl.BlockSpec((D_MODEL, D_MODEL), lambda i: (0, 0))
    in_specs = [row, row, row, row, mat, vec]
    args = [dh, u, o, gate, w_out, post_gain]
    out_specs = [row, row, mat, vec]
    out_shape = [jax.ShapeDtypeStruct((t, D_MODEL), F32)] * 2 + [jax.ShapeDtypeStruct((D_MODEL, D_MODEL), F32),
                                                                  jax.ShapeDtypeStruct((1, D_MODEL), F32)]
    if has_head:
        in_specs.append(pl.BlockSpec((1, D_HEAD), lambda i: (0, 0)))
        args.append(head_gain)
        out_specs.append(pl.BlockSpec((1, D_HEAD), lambda i: (0, 0)))
        out_shape.append(jax.ShapeDtypeStruct((1, D_HEAD), F32))
    return pl.pallas_call(
        body, name=name, grid=(t // tm,), in_specs=in_specs, out_specs=out_specs, out_shape=out_shape,
        compiler_params=_params(("arbitrary",)),
    )(*args)


def _proj_bwd(ds, w4, h_in, gain, dh_out, name):
    t = h_in.shape[0]
    tm = _row_tile(t, 256)

    def body(d0, d1, d2, d3, w_ref, h_ref, g_ref, dho_ref, dhi_ref, gg_ref):
        @pl.when(pl.program_id(0) == 0)
        def _():
            gg_ref[...] = jnp.zeros_like(gg_ref)

        dy = jnp.zeros((tm, D_MODEL), F32)
        for j, d in enumerate((d0, d1, d2, d3)):
            dy = dy + lax.dot_general(d[...].astype(_MXU), w_ref[j], _NT, preferred_element_type=F32)
        x = h_ref[...]
        r = lax.rsqrt(jnp.mean(x * x, axis=-1, keepdims=True) + EPS)
        xn = x * r
        gg_ref[...] += jnp.sum(dy * xn, axis=0, keepdims=True)
        dxn = dy * g_ref[...]
        dhi_ref[...] = dho_ref[...] + r * (dxn - xn * jnp.mean(dxn * xn, axis=-1, keepdims=True))

    row = pl.BlockSpec((tm, D_MODEL), lambda i: (i, 0))
    vec = pl.BlockSpec((1, D_MODEL), lambda i: (0, 0))
    return pl.pallas_call(
        body, name=name, grid=(t // tm,),
        in_specs=[row] * 4 + [pl.BlockSpec((4, D_MODEL, D_MODEL), lambda i: (0, 0, 0)), row, vec, row],
        out_specs=[row, vec],
        out_shape=[jax.ShapeDtypeStruct((t, D_MODEL), F32), jax.ShapeDtypeStruct((1, D_MODEL), F32)],
        compiler_params=_params(("arbitrary",)),
    )(*ds, w4, h_in, gain, dh_out)


def _weight_grad(y, d, name):
    t = y.shape[0]
    tk = _row_tile(t, 512)

    def body(y_ref, d_ref, g_ref):
        @pl.when(pl.program_id(0) == 0)
        def _():
            g_ref[...] = jnp.zeros_like(g_ref)

        g_ref[...] += lax.dot_general(y_ref[...], d_ref[...].astype(_MXU), _TN, preferred_element_type=F32)

    row = pl.BlockSpec((tk, D_MODEL), lambda i: (i, 0))
    return pl.pallas_call(
        body, name=name, grid=(t // tk,), in_specs=[row, row],
        out_specs=pl.BlockSpec((D_MODEL, D_MODEL), lambda i: (0, 0)),
        out_shape=jax.ShapeDtypeStruct((D_MODEL, D_MODEL), F32),
        compiler_params=_params(("arbitrary",)),
    )(y, d)


def _loss_head(h, target, bsz, nq):
    t = h.shape[0]

    def body(h_ref, t_ref, dh_ref, l_ref):
        i = pl.program_id(1)

        @pl.when((pl.program_id(0) == 0) & (i == 0))
        def _():
            l_ref[...] = jnp.zeros_like(l_ref)

        @pl.when(i == 0)
        def _():
            dh_ref[...] = jnp.zeros_like(dh_ref)

        @pl.when(i > 0)
        def _():
            e = h_ref[...] - t_ref[...]
            dh_ref[...] = e * (1.0 / D_MODEL)
            l_ref[...] += jnp.sum(e * e) * (0.5 / D_MODEL)

    return pl.pallas_call(
        body, name="loss_head", grid=(bsz, nq),
        in_specs=[pl.BlockSpec((TILE, D_MODEL), lambda b, i: (b * nq + i, 0)),
                  pl.BlockSpec((TILE, D_MODEL), lambda b, i: (b * (nq - 1) + jnp.maximum(i - 1, 0), 0))],
        out_specs=[pl.BlockSpec((TILE, D_MODEL), lambda b, i: (b * nq + i, 0)),
                   pl.BlockSpec((8, 128), lambda b, i: (0, 0))],
        out_shape=[jax.ShapeDtypeStruct((t, D_MODEL), F32), jax.ShapeDtypeStruct((8, 128), F32)],
        compiler_params=_params(("arbitrary", "arbitrary")),
    )(h, target)


def _local_step(x, target, meta, pre_norm, post_norm, lam, head_gain, hw_in, hw_out, sw_in, sw_out):
    bsz, seq, _ = x.shape
    nq = seq // TILE + 1
    nb = nq * (TILE // BLOCK)
    lp = nq * TILE
    t = bsz * lp
    front = jnp.concatenate([jnp.zeros((N_PAD, D_MODEL), F32), meta], axis=0)
    h0 = jnp.concatenate([jnp.broadcast_to(front[None], (bsz, TILE, D_MODEL)), x], axis=1).reshape(t, D_MODEL)
    pre0, pre1, post0, post1 = pre_norm[0:1], pre_norm[1:2], post_norm[0:1], post_norm[1:2]

    y0, q0, f0, v0, g0 = _norm_proj(h0, pre0, hw_in, "norm_proj_hgrn")
    o0, sst = _hgrn_fwd(q0, f0, v0, lam, bsz, nb)
    h1, u0 = _out_fwd(o0, g0, h0, hw_out, post0, head_gain, "out_fwd_hgrn")
    y1, q1, k1, v1, g1 = _norm_proj(h1, pre1, sw_in, "norm_proj_sb")
    o1, ctot, visited = _sb_fwd(q1, k1, v1, bsz, nq)
    h2, u1 = _out_fwd(o1, g1, h1, sw_out, post1, None, "out_fwd_sb")

    dh2, loss_blk = _loss_head(h2, target.reshape(bsz * seq, D_MODEL), bsz, nq)

    do1, dg1, g_sw_out, g_post1 = _out_bwd(dh2, u1, o1, g1, sw_out, post1, None, "out_bwd_sb")
    dq1, dk1, dv1 = _sb_bwd(q1, k1, v1, ctot, visited, do1, bsz, nq)
    ds1 = (dq1, dk1, dv1, dg1)
    dh1, g_pre1 = _proj_bwd(ds1, sw_in, h1, pre1, dh2, "proj_bwd_sb")
    g_sw_in = jnp.stack([_weight_grad(y1, d, "wgrad_sb_%d" % j) for j, d in enumerate(ds1)])

    do0, dg0, g_hw_out, g_post0, g_head = _out_bwd(dh1, u0, o0, g0, hw_out, post0, head_gain, "out_bwd_hgrn")
    dq0, df0, dv0, dlam = _hgrn_bwd(q0, f0, v0, lam, sst, do0, bsz, nb)
    ds0 = (dq0, df0, dv0, dg0)
    dh0, g_pre0 = _proj_bwd(ds0, hw_in, h0, pre0, dh1, "proj_bwd_hgrn")
    g_hw_in = jnp.stack([_weight_grad(y0, d, "wgrad_hgrn_%d" % j) for j, d in enumerate(ds0)])

    dh0 = dh0.reshape(bsz, lp, D_MODEL)
    grad_x = dh0[:, TILE:, :]
    g_meta = jnp.sum(dh0[:, N_PAD:TILE, :], axis=0)
    g_lam = jnp.sum(dlam, axis=0)
    small = jnp.concatenate([g_pre0, g_pre1, g_post0, g_post1, g_lam, g_lam,
                             jnp.pad(g_head, ((0, 0), (0, D_MODEL - D_HEAD))), g_meta,
                             jnp.zeros((SMALL_ROWS - 23, D_MODEL), F32)], axis=0)
    grads = dict(hw_in=g_hw_in, sw_in=g_sw_in, hw_out=g_hw_out.reshape(N_CHIPS, D_MODEL // N_CHIPS, D_MODEL),
                 sw_out=g_sw_out.reshape(N_CHIPS, D_MODEL // N_CHIPS, D_MODEL), small=small)
    return loss_blk, grad_x, grads


def _place():
    x, y, c = lax.axis_index("x"), lax.axis_index("y"), lax.axis_index("c")
    return x, y, c


def _gather_weights(hw_in, sw_in, hw_out, sw_out, meta):
    def body(hi_ref, si_ref, ho_ref, so_ref, m_ref, ghi, gsi, gho, gso, gm, send_sems, recv_sems):
        x, y, c = _place()
        me = 2 * x + y
        ghi[me] = hi_ref[0].astype(_MXU)
        gsi[me] = si_ref[0].astype(_MXU)
        gho[me] = ho_ref[0].astype(_MXU)
        gso[me] = so_ref[0].astype(_MXU)
        gm[me] = m_ref[...]
        outs = (ghi, gsi, gho, gso, gm)
        peers = [(1 - x, y), (x, 1 - y), (1 - x, 1 - y)]

        def copy(r, a, slot, to):
            return pltpu.make_async_remote_copy(
                src_ref=outs[a].at[slot], dst_ref=outs[a].at[slot], send_sem=send_sems.at[r * 5 + a],
                recv_sem=recv_sems.at[r * 5 + a], device_id=to, device_id_type=MESH)

        sends = [copy(r, a, me, (px, py, c)) for r, (px, py) in enumerate(peers) for a in range(5)]
        for cp in sends:
            cp.start()
        for r, (px, py) in enumerate(peers):
            for a in range(5):
                copy(r, a, 2 * px + py, (px, py, c)).wait_recv()
        for cp in sends:
            cp.wait_send()

    d4 = D_MODEL // N_CHIPS
    vm = pl.BlockSpec(memory_space=pltpu.VMEM)
    return pl.pallas_call(
        body, name="gather_weights",
        in_specs=[vm] * 5, out_specs=[vm] * 5,
        out_shape=[jax.ShapeDtypeStruct((N_CHIPS, D_MODEL, D_MODEL), _MXU)] * 2
        + [jax.ShapeDtypeStruct((N_CHIPS, d4, D_MODEL), _MXU)] * 2
        + [jax.ShapeDtypeStruct((N_CHIPS, N_META, d4), F32)],
        scratch_shapes=[pltpu.SemaphoreType.DMA((15,)), pltpu.SemaphoreType.DMA((15,))],
        compiler_params=pltpu.CompilerParams(vmem_limit_bytes=VMEM_LIMIT),
    )(hw_in, sw_in, hw_out, sw_out, meta)


def _scatter_grads(g):
    names = ("hw_in", "sw_in", "hw_out", "sw_out")

    def body(a0, a1, a2, a3, sm, l0, l1, l2, l3, lsm, send_sems, recv_sems, small_send, small_recv, local_sems):
        x, y, c = _place()
        me = 2 * x + y
        srcs, lands = (a0, a1, a2, a3), (l0, l1, l2, l3)
        peers = [(1 - x, y), (x, 1 - y), (1 - x, 1 - y)]
        local = [pltpu.make_async_copy(srcs[a].at[me], lands[a].at[me], local_sems.at[a]) for a in range(4)]
        local.append(pltpu.make_async_copy(sm, lsm.at[4 * x + 2 * y + c], local_sems.at[4]))
        for cp in local:
            cp.start()

        def copy(r, a, src_slot, dst_slot, to):
            return pltpu.make_async_remote_copy(
                src_ref=srcs[a].at[src_slot], dst_ref=lands[a].at[dst_slot], send_sem=send_sems.at[r * 4 + a],
                recv_sem=recv_sems.at[r * 4 + a], device_id=to, device_id_type=MESH)

        def small_copy(rel, src_dev, to):
            return pltpu.make_async_remote_copy(
                src_ref=sm, dst_ref=lsm.at[src_dev], send_sem=small_send.at[rel - 1], recv_sem=small_recv.at[rel - 1],
                device_id=to, device_id_type=MESH)

        sends = [copy(r, a, 2 * px + py, me, (px, py, c)) for r, (px, py) in enumerate(peers) for a in range(4)]
        flip = lambda bit, v: 1 - v if bit else v
        rels = [(rel, flip(rel & 4, x), flip(rel & 2, y), flip(rel & 1, c)) for rel in range(1, N_DEV)]
        sends += [small_copy(rel, 4 * x + 2 * y + c, (px, py, pc)) for rel, px, py, pc in rels]
        for cp in sends:
            cp.start()
        for r, (px, py) in enumerate(peers):
            for a in range(4):
                copy(r, a, me, 2 * px + py, (px, py, c)).wait_recv()
        for rel, px, py, pc in rels:
            small_copy(rel, 4 * px + 2 * py + pc, (px, py, pc)).wait_recv()
        for cp in sends:
            cp.wait_send()
        for cp in local:
            cp.wait()

    hbm = pl.BlockSpec(memory_space=pl.ANY)
    args = [g[n] for n in names] + [g["small"]]
    return pl.pallas_call(
        body, name="scatter_grads",
        in_specs=[hbm] * 5, out_specs=[hbm] * 5,
        out_shape=[jax.ShapeDtypeStruct(a.shape, F32) for a in args[:4]]
        + [jax.ShapeDtypeStruct((N_DEV, SMALL_ROWS, D_MODEL), F32)],
        scratch_shapes=[pltpu.SemaphoreType.DMA((12,)), pltpu.SemaphoreType.DMA((12,)),
                        pltpu.SemaphoreType.DMA((N_DEV - 1,)), pltpu.SemaphoreType.DMA((N_DEV - 1,)),
                        pltpu.SemaphoreType.DMA((5,))],
    )(*args)


def _sum_slots(land, name):
    n, rows, _ = land.shape
    tm = rows if rows < 256 else 256

    def body(l_ref, o_ref):
        acc = l_ref[0]
        for k in range(1, n):
            acc = acc + l_ref[k]
        o_ref[...] = acc

    return pl.pallas_call(
        body, name=name, grid=(rows // tm,),
        in_specs=[pl.BlockSpec((n, tm, D_MODEL), lambda i: (0, i, 0))],
        out_specs=pl.BlockSpec((tm, D_MODEL), lambda i: (i, 0)),
        out_shape=jax.ShapeDtypeStruct((rows, D_MODEL), F32),
        compiler_params=_params(("arbitrary",)),
    )(land)


def _swap_with_sibling(parts):
    def body(a0, a1, a2, a3, b0, b1, b2, b3, send_sems, recv_sems):
        x, y, c = _place()
        copies = [pltpu.make_async_remote_copy(src_ref=s, dst_ref=d, send_sem=send_sems.at[a], recv_sem=recv_sems.at[a],
                                               device_id=(x, y, 1 - c), device_id_type=MESH)
                  for a, (s, d) in enumerate(zip((a0, a1, a2, a3), (b0, b1, b2, b3)))]
        for cp in copies:
            cp.start()
        for cp in copies:
            cp.wait()

    hbm = pl.BlockSpec(memory_space=pl.ANY)
    return pl.pallas_call(
        body, name="swap_with_sibling", in_specs=[hbm] * 4, out_specs=[hbm] * 4,
        out_shape=[jax.ShapeDtypeStruct(p.shape, F32) for p in parts],
        scratch_shapes=[pltpu.SemaphoreType.DMA((4,)), pltpu.SemaphoreType.DMA((4,))],
    )(*parts)


def _adamw_math(w, g, m, v):
    m = ADAM_B1 * m + (1.0 - ADAM_B1) * g
    v = ADAM_B2 * v + (1.0 - ADAM_B2) * (g * g)
    m_hat = m / (1.0 - ADAM_B1 ** ADAM_STEP)
    v_hat = v / (1.0 - ADAM_B2 ** ADAM_STEP)
    delta = -ADAM_LR * (m_hat / (jnp.sqrt(v_hat) + ADAM_EPS) + ADAM_WD * w)
    return delta, m, v


def _adamw(w, g_parts, m, v, name):
    rows, cols = w.shape
    tm = rows if rows < 256 else 256
    n = len(g_parts)

    def body(*refs):
        w_ref, m_ref, v_ref = refs[n:n + 3]
        g_ref, d_ref, nm_ref, nv_ref = refs[n + 3:]
        g = refs[0][...]
        for p in refs[1:n]:
            g = g + p[...]
        g_ref[...] = g
        d_ref[...], nm_ref[...], nv_ref[...] = _adamw_math(w_ref[...], g, m_ref[...], v_ref[...])

    blk = pl.BlockSpec((tm, cols), lambda i: (i, 0))
    return pl.pallas_call(
        body, name=name, grid=(rows // tm,), in_specs=[blk] * (n + 3), out_specs=[blk] * 4,
        out_shape=[jax.ShapeDtypeStruct((rows, cols), F32)] * 4,
        compiler_params=_params(("arbitrary",)),
    )(*g_parts, w, m, v)


def _lam_of(hgrn_lb):
    def body(lb_ref, o_ref):
        lb = lb_ref[...]
        e = jnp.exp(lb - jnp.max(lb, axis=0, keepdims=True))
        o_ref[...] = e[0:1, :] / jnp.sum(e, axis=0, keepdims=True)

    return pl.pallas_call(body, name="lam_of", out_shape=jax.ShapeDtypeStruct((1, D_MODEL), F32))(hgrn_lb)


def _small_grads(land_small, lam):
    def body(l_ref, lam_ref, o_ref):
        acc = l_ref[0]
        for k in range(1, N_DEV):
            acc = acc + l_ref[k]
        p = lam_ref[...]
        slope = p * (1.0 - p)
        row = _iota2((SMALL_ROWS, D_MODEL), 0)
        o_ref[...] = acc * jnp.where(row == 4, slope, jnp.where(row == 5, -slope, 1.0))

    return pl.pallas_call(body, name="small_grads",
                          out_shape=jax.ShapeDtypeStruct((SMALL_ROWS, D_MODEL), F32))(land_small, lam)


def kernel(x, meta_tokens, pre_norm, post_norm, hgrn_w_in, hgrn_lb, hgrn_out_norm, hgrn_w_out, sb_w_in, sb_w_out, loss_target, m_meta_tokens, m_pre_norm, m_post_norm, m_hgrn_w_in, m_hgrn_lb, m_hgrn_out_norm, m_hgrn_w_out, m_sb_w_in, m_sb_w_out, v_meta_tokens, v_pre_norm, v_post_norm, v_hgrn_w_in, v_hgrn_lb, v_hgrn_out_norm, v_hgrn_w_out, v_sb_w_in, v_sb_w_out):
    d4 = D_MODEL // N_CHIPS
    chip = 2 * lax.axis_index("x") + lax.axis_index("y")
    hw_in, sw_in, hw_out, sw_out, meta4 = _gather_weights(hgrn_w_in, sb_w_in, hgrn_w_out, sb_w_out, meta_tokens)
    meta = meta4.transpose(1, 0, 2).reshape(N_META, D_MODEL)
    lam = _lam_of(hgrn_lb)
    loss_blk, grad_x, grads = _local_step(
        x, loss_target, meta, pre_norm, post_norm, lam, hgrn_out_norm,
        hw_in, hw_out.reshape(D_MODEL, D_MODEL), sw_in, sw_out.reshape(D_MODEL, D_MODEL))
    loss = lax.psum(loss_blk[0, 0], ("x", "y", "c"))

    l_hi, l_si, l_ho, l_so, l_small = _scatter_grads(grads)
    parts = [_sum_slots(l_hi, "sum_hw_in"), _sum_slots(l_si, "sum_sw_in"),
             _sum_slots(l_ho, "sum_hw_out"), _sum_slots(l_so, "sum_sw_out")]
    sib = _swap_with_sibling(parts)
    small = _small_grads(l_small, lam)

    res = {}
    res["hgrn_w_in"] = _adamw(hgrn_w_in[0], [parts[0], sib[0]], m_hgrn_w_in[0], v_hgrn_w_in[0], "adamw_hw_in")
    res["sb_w_in"] = _adamw(sb_w_in[0], [parts[1], sib[1]], m_sb_w_in[0], v_sb_w_in[0], "adamw_sw_in")
    res["hgrn_w_out"] = _adamw(hgrn_w_out[0], [parts[2], sib[2]], m_hgrn_w_out[0], v_hgrn_w_out[0], "adamw_hw_out")
    res["sb_w_out"] = _adamw(sb_w_out[0], [parts[3], sib[3]], m_sb_w_out[0], v_sb_w_out[0], "adamw_sw_out")
    res["pre_norm"] = _adamw(pre_norm, [small[0:2]], m_pre_norm, v_pre_norm, "adamw_pre")
    res["post_norm"] = _adamw(post_norm, [small[2:4]], m_post_norm, v_post_norm, "adamw_post")
    res["hgrn_lb"] = _adamw(hgrn_lb, [small[4:6]], m_hgrn_lb, v_hgrn_lb, "adamw_lb")
    res["hgrn_out_norm"] = _adamw(hgrn_out_norm, [small[6:7, :D_HEAD]], m_hgrn_out_norm, v_hgrn_out_norm, "adamw_head")
    g_meta = lax.dynamic_slice_in_dim(small[7:7 + N_META], chip * d4, d4, axis=1)
    res["meta_tokens"] = _adamw(meta_tokens, [g_meta], m_meta_tokens, v_meta_tokens, "adamw_meta")
    for n in ("hgrn_w_in", "hgrn_w_out", "sb_w_in", "sb_w_out"):
        res[n] = tuple(a[None] for a in res[n])
    order = ("meta_tokens", "pre_norm", "post_norm", "hgrn_w_in", "hgrn_lb", "hgrn_out_norm", "hgrn_w_out",
             "sb_w_in", "sb_w_out")
    return (loss, grad_x, *[res[n][0] for n in order], *[res[n][1] for n in order],
            *[res[n][2] for n in order], *[res[n][3] for n in order])
```

```python
import functools

import jax
import numpy as np
import jax.numpy as jnp
from jax import lax
from jax.experimental import pallas as pl
from jax.experimental.pallas import tpu as pltpu

F32 = jnp.float32
_MXU = jnp.bfloat16

D_MODEL = 1024
N_HEADS = 8
D_HEAD = 128
BLOCK = 128
N_META = 16
TILE = 256
N_PAD = TILE - N_META
UNDERFLOW = -104.0
EPS = 1e-6
SB_SCALE = D_HEAD ** -0.5
ADAM_LR, ADAM_B1, ADAM_B2, ADAM_EPS, ADAM_WD, ADAM_STEP = 0.001, 0.9, 0.999, 1e-08, 0.01, 10
N_CHIPS = 4
N_DEV = 8
SMALL_ROWS = 24
VMEM_LIMIT = 56 * 1024 * 1024
MESH = pl.DeviceIdType.MESH

_NT = (((1,), (1,)), ((), ()))
_TN = (((0,), (0,)), ((), ()))


def _mm(a, b):
    return jnp.dot(a.astype(_MXU), b.astype(_MXU), preferred_element_type=F32)


def _mm_nt(a, b):
    return lax.dot_general(a.astype(_MXU), b.astype(_MXU), _NT, preferred_element_type=F32)


def _mm_tn(a, b):
    return lax.dot_general(a.astype(_MXU), b.astype(_MXU), _TN, preferred_element_type=F32)


def _split2(x):
    hi = x.astype(_MXU)
    return hi, (x - hi.astype(F32)).astype(_MXU)


def _mm_s(a16, state):
    hi, lo = _split2(state)
    return jnp.dot(a16, hi, preferred_element_type=F32) + jnp.dot(a16, lo, preferred_element_type=F32)


def _mm_nt_s(a16, state):
    hi, lo = _split2(state)
    return (lax.dot_general(a16, hi, _NT, preferred_element_type=F32)
            + lax.dot_general(a16, lo, _NT, preferred_element_type=F32))


def _split3(x):
    hi = x.astype(_MXU)
    r1 = x - hi.astype(F32)
    mid = r1.astype(_MXU)
    lo = (r1 - mid.astype(F32)).astype(_MXU)
    return hi, mid, lo


def _mm01_right(x, m01):
    hi, lo = _split2(x)
    return jnp.dot(hi, m01, preferred_element_type=F32) + jnp.dot(lo, m01, preferred_element_type=F32)


def _mm01_left(m01, x):
    hi, mid, lo = _split3(x)
    d = functools.partial(jnp.dot, preferred_element_type=F32)
    return d(m01, hi) + d(m01, mid) + d(m01, lo)


def _iota2(shape, dim):
    return lax.broadcasted_iota(jnp.int32, shape, dim)


def _row_tile(total, pref):
    t = pref
    while total % t:
        t -= BLOCK
    return t


def _params(sem, limit=VMEM_LIMIT):
    return pltpu.CompilerParams(dimension_semantics=sem, vmem_limit_bytes=limit)


def _sigmoid(x):
    return 1.0 / (1.0 + jnp.exp(-x))


def _norm_proj(h, gain, w4, name):
    t = h.shape[0]
    tm = _row_tile(t, 256)

    def body(h_ref, g_ref, w_ref, y_ref, s0, s1, s2, s3):
        x = h_ref[...]
        r = lax.rsqrt(jnp.mean(x * x, axis=-1, keepdims=True) + EPS)
        y = (x * r * g_ref[...]).astype(_MXU)
        y_ref[...] = y
        for j, s in enumerate((s0, s1, s2, s3)):
            s[...] = jnp.dot(y, w_ref[j], preferred_element_type=F32)

    row = pl.BlockSpec((tm, D_MODEL), lambda i: (i, 0))
    return pl.pallas_call(
        body, name=name, grid=(t // tm,),
        in_specs=[row, pl.BlockSpec((1, D_MODEL), lambda i: (0, 0)),
                  pl.BlockSpec((4, D_MODEL, D_MODEL), lambda i: (0, 0, 0))],
        out_specs=[row] * 5,
        out_shape=[jax.ShapeDtypeStruct((t, D_MODEL), _MXU)] + [jax.ShapeDtypeStruct((t, D_MODEL), F32)] * 4,
        compiler_params=_params(("arbitrary",)),
    )(h, gain, w4)


LEVELS = (64, 32, 16, 8, 4, 2, 1)


def _hgrn_tables():
    r = np.arange(BLOCK)
    mats = []
    for m in LEVELS:
        same = r[:, None] // m == r[None, :] // m
        second = ((r // m) % 2 == 1)[:, None]
        mats.append(same & np.where(second, r[None, :] <= r[:, None], r[None, :] > r[:, None]))
    mats.append(r[None, :] <= r[:, None])
    x = r[:, None] ^ r[None, :]
    lv = np.full((BLOCK, BLOCK), len(LEVELS), np.int32)
    for i, m in enumerate(LEVELS):
        lv[(x >= m) & (x < 2 * m)] = i
    return jnp.asarray(np.concatenate(mats, 0).astype(np.float32), dtype=_MXU), jnp.asarray(lv)


def _hgrn_gates(fz, lam, chunk):
    pos = chunk * BLOCK + _iota2((BLOCK, D_HEAD), 0)
    live = pos >= N_PAD
    sg = _sigmoid(fz)
    f = lam + (1.0 - lam) * sg
    g = jnp.where(live, jnp.log(f), 0.0)
    k = jnp.where(live, (1.0 - lam) * (1.0 - sg), 0.0)
    return sg, f, g, k, live


def _level_operand(q, k, e_all, li, m):
    decay = jnp.exp(e_all[li * BLOCK:(li + 1) * BLOCK, :])
    is_q = (_iota2((BLOCK, D_HEAD), 0) & m) != 0
    return is_q, decay, (jnp.where(is_q, q, k) * decay).astype(_MXU)


def _hgrn_fwd(qs, fs, vs, lam, bsz, nb):
    t = qs.shape[0]
    sums, levels = _hgrn_tables()
    n_lv = len(LEVELS)

    def body(q_ref, f_ref, v_ref, lam_ref, sums_ref, lv_ref, o_ref, sst_ref, st_scr):
        n = pl.program_id(2)

        @pl.when(n == 0)
        def _():
            st_scr[...] = jnp.zeros_like(st_scr)

        st = st_scr[...]
        sst_ref[0] = st
        q, v = q_ref[...], v_ref[...]
        _, _, g, k, _ = _hgrn_gates(f_ref[...], lam_ref[...], n)
        e_all = _mm01_left(sums_ref[...], g)
        b = e_all[n_lv * BLOCK:, :]
        lv = lv_ref[...]
        sym = jnp.zeros((BLOCK, BLOCK), F32)
        for li, m in enumerate(LEVELS):
            _, _, x16 = _level_operand(q, k, e_all, li, m)
            sym = jnp.where(lv == li, lax.dot_general(x16, x16, _NT, preferred_element_type=F32), sym)
        r, c = _iota2((BLOCK, BLOCK), 0), _iota2((BLOCK, BLOCK), 1)
        a = jnp.where(c < r, sym, jnp.where(c == r, jnp.sum(q * k, axis=1, keepdims=True), 0.0))
        o_ref[...] = _mm_nt_s((q * jnp.exp(b)).astype(_MXU), st) + _mm(a, v)
        b_end = b[BLOCK - 1:BLOCK, :]
        st_scr[...] = st * jnp.exp(b_end) + _mm_tn(v, k * jnp.exp(b_end - b))

    blk = pl.BlockSpec((BLOCK, D_HEAD), lambda b, h, n: (b * nb + n, h))
    return pl.pallas_call(
        body, name="hgrn_fwd", grid=(bsz, N_HEADS, nb),
        in_specs=[blk, blk, blk, pl.BlockSpec((1, D_HEAD), lambda b, h, n: (0, h)),
                  pl.BlockSpec(sums.shape, lambda b, h, n: (0, 0)), pl.BlockSpec(levels.shape, lambda b, h, n: (0, 0))],
        out_specs=[blk, pl.BlockSpec((1, D_HEAD, D_HEAD), lambda b, h, n: ((b * N_HEADS + h) * nb + n, 0, 0))],
        out_shape=[jax.ShapeDtypeStruct((t, D_MODEL), F32),
                   jax.ShapeDtypeStruct((bsz * N_HEADS * nb, D_HEAD, D_HEAD), F32)],
        scratch_shapes=[pltpu.VMEM((D_HEAD, D_HEAD), F32)],
        compiler_params=_params(("arbitrary", "arbitrary", "arbitrary")),
    )(qs, fs, vs, lam, sums, levels)


def _hgrn_bwd(qs, fs, vs, lam, sst, do, bsz, nb):
    t = qs.shape[0]
    sums, levels = _hgrn_tables()
    n_lv = len(LEVELS)

    def body(q_ref, f_ref, v_ref, lam_ref, sst_ref, do_ref, sums_ref, lv_ref, dq_ref, df_ref, dv_ref, dlam_ref,
             dst_scr, gsum_scr):
        n = pl.program_id(2)
        chunk = nb - 1 - n

        @pl.when(n == 0)
        def _():
            dst_scr[...] = jnp.zeros_like(dst_scr)
            gsum_scr[...] = jnp.zeros_like(gsum_scr)
            dlam_ref[...] = jnp.zeros_like(dlam_ref)

        lam = lam_ref[...]
        q, v, do = q_ref[...], v_ref[...], do_ref[...]
        sg, f, g, k, live = _hgrn_gates(f_ref[...], lam, chunk)
        e_all = _mm01_left(sums_ref[...], g)
        b = e_all[n_lv * BLOCK:, :]
        lv = lv_ref[...]
        r, c = _iota2((BLOCK, BLOCK), 0), _iota2((BLOCK, BLOCK), 1)
        do16, v16 = do.astype(_MXU), v.astype(_MXU)
        da = lax.dot_general(do16, v16, _NT, preferred_element_type=F32)
        da_t = lax.dot_general(v16, do16, _NT, preferred_element_type=F32)
        da_sym = jnp.where(c < r, da, da_t)
        sym = jnp.zeros((BLOCK, BLOCK), F32)
        dq = jnp.zeros((BLOCK, D_HEAD), F32)
        dk = jnp.zeros((BLOCK, D_HEAD), F32)
        db = jnp.zeros((BLOCK, D_HEAD), F32)
        for li, m in enumerate(LEVELS):
            is_q, decay, x16 = _level_operand(q, k, e_all, li, m)
            here = lv == li
            sym = jnp.where(here, lax.dot_general(x16, x16, _NT, preferred_element_type=F32), sym)
            y = jnp.dot(jnp.where(here, da_sym, 0.0).astype(_MXU), x16, preferred_element_type=F32)
            dx = y * decay
            dxq = jnp.where(is_q, dx, 0.0)
            dq = dq + dxq
            dk = dk + (dx - dxq)
            p = x16.astype(F32) * y
            db = db + jnp.where(is_q, p, -p)
        on_diag = jnp.sum(q * k, axis=1, keepdims=True)
        a_t = jnp.where(c > r, sym, jnp.where(c == r, on_diag, 0.0))
        st, dst = sst_ref[0], dst_scr[...]
        eb = jnp.exp(b)
        b_end = b[BLOCK - 1:BLOCK, :]
        dec = jnp.exp(b_end - b)
        qh16, kt16 = (q * eb).astype(_MXU), (k * dec).astype(_MXU)
        dq_st = _mm_s(do16, st)
        dk_st = _mm_s(v16, dst)
        d_diag = jnp.sum(do * v, axis=1, keepdims=True)
        dq_ref[...] = dq + d_diag * k + eb * dq_st
        dk = dk + d_diag * q + dec * dk_st
        dv_ref[...] = jnp.dot(a_t.astype(_MXU), do16, preferred_element_type=F32) + _mm_nt_s(kt16, dst)
        dst_scr[...] = dst * jnp.exp(b_end) + lax.dot_general(do16, qh16, _TN, preferred_element_type=F32)
        db = db + (qh16.astype(F32) * dq_st - kt16.astype(F32) * dk_st)
        dg = _mm01_left((c >= r).astype(_MXU), db) + gsum_scr[...]
        gsum_scr[...] = gsum_scr[...] + jnp.sum(db, axis=0, keepdims=True)
        slope = (1.0 - lam) * sg * (1.0 - sg)
        df_ref[...] = jnp.where(live, dg * slope / f - dk * slope, 0.0)
        dl = jnp.where(live, (dg / f - dk) * (1.0 - sg), 0.0)
        dlam_ref[0] = dlam_ref[0] + jnp.sum(dl, axis=0, keepdims=True)

    blk = pl.BlockSpec((BLOCK, D_HEAD), lambda b, h, n: (b * nb + nb - 1 - n, h))
    return pl.pallas_call(
        body, name="hgrn_bwd", grid=(bsz, N_HEADS, nb),
        in_specs=[blk, blk, blk, pl.BlockSpec((1, D_HEAD), lambda b, h, n: (0, h)),
                  pl.BlockSpec((1, D_HEAD, D_HEAD), lambda b, h, n: ((b * N_HEADS + h) * nb + nb - 1 - n, 0, 0)),
                  blk, pl.BlockSpec(sums.shape, lambda b, h, n: (0, 0)), pl.BlockSpec(levels.shape, lambda b, h, n: (0, 0))],
        out_specs=[blk, blk, blk, pl.BlockSpec((1, 1, D_HEAD), lambda b, h, n: (b, 0, h))],
        out_shape=[jax.ShapeDtypeStruct((t, D_MODEL), F32)] * 3 + [jax.ShapeDtypeStruct((bsz, 1, D_MODEL), F32)],
        scratch_shapes=[pltpu.VMEM((D_HEAD, D_HEAD), F32), pltpu.VMEM((1, D_HEAD), F32)],
        compiler_params=_params(("arbitrary", "arbitrary", "arbitrary")),
    )(qs, fs, vs, lam, sst, do, sums, levels)


def _sb_tile(q16, k_blk, i, j):
    z = lax.dot_general(q16, k_blk.astype(_MXU), _NT, preferred_element_type=F32) * SB_SCALE
    qpos = i * TILE + _iota2((TILE, TILE), 0)
    kpos = j * TILE + _iota2((TILE, TILE), 1)
    valid = (kpos < qpos) & (kpos >= N_PAD)
    softplus = jnp.maximum(z, 0.0) + jnp.log(1.0 + jnp.exp(-jnp.abs(z)))
    return valid, jnp.where(valid, -softplus, 0.0), z - softplus


def _sb_fwd(qs, ks, vs, bsz, nq):
    t = qs.shape[0]
    lp = nq * TILE

    def body(q_ref, k_ref, v_ref, o_ref, c_ref, n_ref):
        b, h, i = pl.program_id(0), pl.program_id(1), pl.program_id(2)
        q16 = q_ref[...].astype(_MXU)
        r, c = _iota2((TILE, TILE), 0), _iota2((TILE, TILE), 1)
        after = (r > c).astype(_MXU)

        def more(carry):
            jj, _, _, top = carry
            return (jj <= i) & (top > UNDERFLOW)

        def step(carry):
            jj, acc, later0, _ = carry
            j = i - jj
            ks_ = pl.ds(pl.multiple_of(j * TILE, TILE), TILE)
            valid, keep, log_beta = _sb_tile(q16, k_ref[ks_, :], i, j)
            later = later0 + _mm01_right(keep, after)
            a = jnp.where(valid, jnp.exp(log_beta + later), 0.0)
            acc = acc + _mm(a, v_ref[ks_, :])
            later0 = later0 + jnp.sum(keep, axis=1, keepdims=True)
            return jj + 1, acc, later0, jnp.max(later0)

        init = (jnp.int32(0), jnp.zeros((TILE, D_HEAD), F32), jnp.zeros((TILE, 1), F32), jnp.float32(0.0))
        visited, acc, tot, _ = lax.while_loop(more, step, init)
        o_ref[...] = acc
        c_ref[...] = jnp.broadcast_to(tot, (TILE, D_HEAD))
        n_ref[(b * N_HEADS + h) * nq + i] = visited.astype(F32)

    blk = pl.BlockSpec((TILE, D_HEAD), lambda b, h, i: (b * nq + i, h))
    seq = pl.BlockSpec((lp, D_HEAD), lambda b, h, i: (b, h))
    return pl.pallas_call(
        body, name="sb_fwd", grid=(bsz, N_HEADS, nq),
        in_specs=[blk, seq, seq], out_specs=[blk, blk, pl.BlockSpec(memory_space=pltpu.SMEM)],
        out_shape=[jax.ShapeDtypeStruct((t, D_MODEL), F32)] * 2 + [jax.ShapeDtypeStruct((bsz * N_HEADS * nq,), F32)],
        compiler_params=_params(("arbitrary", "arbitrary", "arbitrary")),
    )(qs, ks, vs)


def _sb_bwd(qs, ks, vs, ctot, visited, do, bsz, nq):
    t = qs.shape[0]
    lp = nq * TILE

    def body(n_ref, q_ref, k_ref, v_ref, c_ref, do_ref, dq_ref, dk_ref, dv_ref):
        b, h, i = pl.program_id(0), pl.program_id(1), pl.program_id(2)

        @pl.when(i == 0)
        def _():
            dk_ref[...] = jnp.zeros_like(dk_ref)
            dv_ref[...] = jnp.zeros_like(dv_ref)

        q16 = q_ref[...].astype(_MXU)
        do16 = do_ref[...].astype(_MXU)
        ctot_col = c_ref[:, 0:1]
        r, c = _iota2((TILE, TILE), 0), _iota2((TILE, TILE), 1)
        upto = (r <= c).astype(_MXU)
        before = (r < c).astype(_MXU)
        first = jnp.maximum(i + 1 - n_ref[(b * N_HEADS + h) * nq + i].astype(jnp.int32), 0)

        def step(j, carry):
            dq, keep_pre, g_pre = carry
            ks_ = pl.ds(pl.multiple_of(j * TILE, TILE), TILE)
            k_blk, v_blk = k_ref[ks_, :], v_ref[ks_, :]
            valid, keep, log_beta = _sb_tile(q16, k_blk, i, j)
            later = ctot_col - keep_pre - _mm01_right(keep, upto)
            a = jnp.where(valid, jnp.exp(log_beta + later), 0.0)
            da = lax.dot_general(do16, v_blk.astype(_MXU), _NT, preferred_element_type=F32)
            g = a * da
            g_before = g_pre + _mm01_right(g, before)
            beta = jnp.exp(log_beta)
            dz = jnp.where(valid, g * (1.0 - beta) - beta * g_before, 0.0) * SB_SCALE
            dz16 = dz.astype(_MXU)
            dq = dq + jnp.dot(dz16, k_blk.astype(_MXU), preferred_element_type=F32)
            dk_ref[ks_, :] += lax.dot_general(dz16, q16, _TN, preferred_element_type=F32)
            dv_ref[ks_, :] += lax.dot_general(a.astype(_MXU), do16, _TN, preferred_element_type=F32)
            return (dq, keep_pre + jnp.sum(keep, axis=1, keepdims=True), g_pre + jnp.sum(g, axis=1, keepdims=True))

        zero_col = jnp.zeros((TILE, 1), F32)
        dq, _, _ = lax.fori_loop(first, i + 1, step, (jnp.zeros((TILE, D_HEAD), F32), zero_col, zero_col))
        dq_ref[...] = dq

    blk = pl.BlockSpec((TILE, D_HEAD), lambda b, h, i: (b * nq + i, h))
    seq = pl.BlockSpec((lp, D_HEAD), lambda b, h, i: (b, h))
    return pl.pallas_call(
        body, name="sb_bwd", grid=(bsz, N_HEADS, nq),
        in_specs=[pl.BlockSpec(memory_space=pltpu.SMEM), blk, seq, seq, blk, blk], out_specs=[blk, seq, seq],
        out_shape=[jax.ShapeDtypeStruct((t, D_MODEL), F32)] * 3,
        compiler_params=_params(("arbitrary", "arbitrary", "arbitrary")),
    )(visited, qs, ks, vs, ctot, do)


def _head_norm(o, head_gain):
    outs, rs = [], []
    for h in range(N_HEADS):
        oh = o[:, h * D_HEAD:(h + 1) * D_HEAD]
        r = lax.rsqrt(jnp.mean(oh * oh, axis=-1, keepdims=True) + EPS)
        outs.append(oh * r)
        rs.append(r)
    return outs, rs


def _mix(o, gate, head_gain):
    if head_gain is None:
        on = o
    else:
        outs, _ = _head_norm(o, head_gain)
        on = jnp.concatenate([x * head_gain for x in outs], axis=1)
    return on, on * (gate * _sigmoid(gate))


def _out_fwd(o, gate, h_in, w_out, post_gain, head_gain, name):
    t = o.shape[0]
    tm = _row_tile(t, 256)
    has_head = head_gain is not None

    def body(*refs):
        if has_head:
            o_ref, g_ref, h_ref, w_ref, pg_ref, hg_ref, ho_ref, u_ref = refs
            hg = hg_ref[...]
        else:
            o_ref, g_ref, h_ref, w_ref, pg_ref, ho_ref, u_ref = refs
            hg = None
        _, mix = _mix(o_ref[...], g_ref[...], hg)
        u = jnp.dot(mix.astype(_MXU), w_ref[...], preferred_element_type=F32)
        u_ref[...] = u
        r = lax.rsqrt(jnp.mean(u * u, axis=-1, keepdims=True) + EPS)
        ho_ref[...] = h_ref[...] + u * r * pg_ref[...]

    row = pl.BlockSpec((tm, D_MODEL), lambda i: (i, 0))
    vec = pl.BlockSpec((1, D_MODEL), lambda i: (0, 0))
    in_specs = [row, row, row, pl.BlockSpec((D_MODEL, D_MODEL), lambda i: (0, 0)), vec]
    args = [o, gate, h_in, w_out, post_gain]
    if has_head:
        in_specs.append(pl.BlockSpec((1, D_HEAD), lambda i: (0, 0)))
        args.append(head_gain)
    return pl.pallas_call(
        body, name=name, grid=(t // tm,), in_specs=in_specs, out_specs=[row, row],
        out_shape=[jax.ShapeDtypeStruct((t, D_MODEL), F32)] * 2,
        compiler_params=_params(("arbitrary",)),
    )(*args)


def _out_bwd(dh, u, o, gate, w_out, post_gain, head_gain, name):
    t = o.shape[0]
    tm = _row_tile(t, 256)
    has_head = head_gain is not None

    def body(*refs):
        if has_head:
            dh_ref, u_ref, o_ref, g_ref, w_ref, pg_ref, hg_ref, do_ref, dg_ref, gw_ref, gp_ref, gh_ref = refs
            hg = hg_ref[...]
        else:
            dh_ref, u_ref, o_ref, g_ref, w_ref, pg_ref, do_ref, dg_ref, gw_ref, gp_ref = refs
            hg = None
        first = pl.program_id(0) == 0

        @pl.when(first)
        def _():
            gw_ref[...] = jnp.zeros_like(gw_ref)
            gp_ref[...] = jnp.zeros_like(gp_ref)
            if has_head:
                gh_ref[...] = jnp.zeros_like(gh_ref)

        dr, u, o, gate = dh_ref[...], u_ref[...], o_ref[...], g_ref[...]
        r = lax.rsqrt(jnp.mean(u * u, axis=-1, keepdims=True) + EPS)
        un = u * r
        gp_ref[...] += jnp.sum(dr * un, axis=0, keepdims=True)
        dun = dr * pg_ref[...]
        du = r * (dun - un * jnp.mean(dun * un, axis=-1, keepdims=True))
        on, mix = _mix(o, gate, hg)
        du16 = du.astype(_MXU)
        gw_ref[...] += lax.dot_general(mix.astype(_MXU), du16, _TN, preferred_element_type=F32)
        dmix = lax.dot_general(du16, w_ref[...], _NT, preferred_element_type=F32)
        sg = _sigmoid(gate)
        dg_ref[...] = dmix * on * (sg * (1.0 + gate * (1.0 - sg)))
        don = dmix * (gate * sg)
        if has_head:
            outs, rs = _head_norm(o, hg)
            gh = jnp.zeros((1, D_HEAD), F32)
            cols = []
            for h in range(N_HEADS):
                dn = don[:, h * D_HEAD:(h + 1) * D_HEAD]
                gh = gh + jnp.sum(dn * outs[h], axis=0, keepdims=True)
                dnn = dn * hg
                cols.append(rs[h] * (dnn - outs[h] * jnp.mean(dnn * outs[h], axis=-1, keepdims=True)))
            gh_ref[...] += gh
            do_ref[...] = jnp.concatenate(cols, axis=1)
        else:
            do_ref[...] = don

    row = pl.BlockSpec((tm, D_MODEL), lambda i: (i, 0))
    vec = pl.BlockSpec((1, D_MODEL), lambda i: (0, 0))
    mat = pl.BlockSpec((D_MODEL, D_MODEL), lambda i: (0, 0))
    in_specs = [row, row, row, row, mat, vec]
    args = [dh, u, o, gate, w_out, post_gain]
    out_specs = [row, row, mat, vec]
    out_shape = [jax.ShapeDtypeStruct((t, D_MODEL), F32)] * 2 + [jax.ShapeDtypeStruct((D_MODEL, D_MODEL), F32),
                                                                  jax.ShapeDtypeStruct((1, D_MODEL), F32)]
    if has_head:
        in_specs.append(pl.BlockSpec((1, D_HEAD), lambda i: (0, 0)))
        args.append(head_gain)
        out_specs.append(pl.BlockSpec((1, D_HEAD), lambda i: (0, 0)))
        out_shape.append(jax.ShapeDtypeStruct((1, D_HEAD), F32))
    return pl.pallas_call(
        body, name=name, grid=(t // tm,), in_specs=in_specs, out_specs=out_specs, out_shape=out_shape,
        compiler_params=_params(("arbitrary",)),
    )(*args)


def _proj_bwd(ds, w4, h_in, gain, dh_out, name):
    t = h_in.shape[0]
    tm = _row_tile(t, 256)

    def body(d0, d1, d2, d3, w_ref, h_ref, g_ref, dho_ref, dhi_ref, gg_ref):
        @pl.when(pl.program_id(0) == 0)
        def _():
            gg_ref[...] = jnp.zeros_like(gg_ref)

        dy = jnp.zeros((tm, D_MODEL), F32)
        for j, d in enumerate((d0, d1, d2, d3)):
            dy = dy + lax.dot_general(d[...].astype(_MXU), w_ref[j], _NT, preferred_element_type=F32)
        x = h_ref[...]
        r = lax.rsqrt(jnp.mean(x * x, axis=-1, keepdims=True) + EPS)
        xn = x * r
        gg_ref[...] += jnp.sum(dy * xn, axis=0, keepdims=True)
        dxn = dy * g_ref[...]
        dhi_ref[...] = dho_ref[...] + r * (dxn - xn * jnp.mean(dxn * xn, axis=-1, keepdims=True))

    row = pl.BlockSpec((tm, D_MODEL), lambda i: (i, 0))
    vec = pl.BlockSpec((1, D_MODEL), lambda i: (0, 0))
    return pl.pallas_call(
        body, name=name, grid=(t // tm,),
        in_specs=[row] * 4 + [pl.BlockSpec((4, D_MODEL, D_MODEL), lambda i: (0, 0, 0)), row, vec, row],
        out_specs=[row, vec],
        out_shape=[jax.ShapeDtypeStruct((t, D_MODEL), F32), jax.ShapeDtypeStruct((1, D_MODEL), F32)],
        compiler_params=_params(("arbitrary",)),
    )(*ds, w4, h_in, gain, dh_out)


def _weight_grad(y, d, name):
    t = y.shape[0]
    tk = _row_tile(t, 512)

    def body(y_ref, d_ref, g_ref):
        @pl.when(pl.program_id(0) == 0)
        def _():
            g_ref[...] = jnp.zeros_like(g_ref)

        g_ref[...] += lax.dot_general(y_ref[...], d_ref[...].astype(_MXU), _TN, preferred_element_type=F32)

    row = pl.BlockSpec((tk, D_MODEL), lambda i: (i, 0))
    return pl.pallas_call(
        body, name=name, grid=(t // tk,), in_specs=[row, row],
        out_specs=pl.BlockSpec((D_MODEL, D_MODEL), lambda i: (0, 0)),
        out_shape=jax.ShapeDtypeStruct((D_MODEL, D_MODEL), F32),
        compiler_params=_params(("arbitrary",)),
    )(y, d)


def _loss_head(h, target, bsz, nq):
    t = h.shape[0]

    def body(h_ref, t_ref, dh_ref, l_ref):
        i = pl.program_id(1)

        @pl.when((pl.program_id(0) == 0) & (i == 0))
        def _():
            l_ref[...] = jnp.zeros_like(l_ref)

        @pl.when(i == 0)
        def _():
            dh_ref[...] = jnp.zeros_like(dh_ref)

        @pl.when(i > 0)
        def _():
            e = h_ref[...] - t_ref[...]
            dh_ref[...] = e * (1.0 / D_MODEL)
            l_ref[...] += jnp.sum(e * e) * (0.5 / D_MODEL)

    return pl.pallas_call(
        body, name="loss_head", grid=(bsz, nq),
        in_specs=[pl.BlockSpec((TILE, D_MODEL), lambda b, i: (b * nq + i, 0)),
                  pl.BlockSpec((TILE, D_MODEL), lambda b, i: (b * (nq - 1) + jnp.maximum(i - 1, 0), 0))],
        out_specs=[pl.BlockSpec((TILE, D_MODEL), lambda b, i: (b * nq + i, 0)),
                   pl.BlockSpec((8, 128), lambda b, i: (0, 0))],
        out_shape=[jax.ShapeDtypeStruct((t, D_MODEL), F32), jax.ShapeDtypeStruct((8, 128), F32)],
        compiler_params=_params(("arbitrary", "arbitrary")),
    )(h, target)


def _local_step(x, target, meta, pre_norm, post_norm, lam, head_gain, hw_in, hw_out, sw_in, sw_out):
    bsz, seq, _ = x.shape
    nq = seq // TILE + 1
    nb = nq * (TILE // BLOCK)
    lp = nq * TILE
    t = bsz * lp
    front = jnp.concatenate([jnp.zeros((N_PAD, D_MODEL), F32), meta], axis=0)
    h0 = jnp.concatenate([jnp.broadcast_to(front[None], (bsz, TILE, D_MODEL)), x], axis=1).reshape(t, D_MODEL)
    pre0, pre1, post0, post1 = pre_norm[0:1], pre_norm[1:2], post_norm[0:1], post_norm[1:2]

    y0, q0, f0, v0, g0 = _norm_proj(h0, pre0, hw_in, "norm_proj_hgrn")
    o0, sst = _hgrn_fwd(q0, f0, v0, lam, bsz, nb)
    h1, u0 = _out_fwd(o0, g0, h0, hw_out, post0, head_gain, "out_fwd_hgrn")
    y1, q1, k1, v1, g1 = _norm_proj(h1, pre1, sw_in, "norm_proj_sb")
    o1, ctot, visited = _sb_fwd(q1, k1, v1, bsz, nq)
    h2, u1 = _out_fwd(o1, g1, h1, sw_out, post1, None, "out_fwd_sb")

    dh2, loss_blk = _loss_head(h2, target.reshape(bsz * seq, D_MODEL), bsz, nq)

    do1, dg1, g_sw_out, g_post1 = _out_bwd(dh2, u1, o1, g1, sw_out, post1, None, "out_bwd_sb")
    dq1, dk1, dv1 = _sb_bwd(q1, k1, v1, ctot, visited, do1, bsz, nq)
    ds1 = (dq1, dk1, dv1, dg1)
    dh1, g_pre1 = _proj_bwd(ds1, sw_in, h1, pre1, dh2, "proj_bwd_sb")
    g_sw_in = jnp.stack([_weight_grad(y1, d, "wgrad_sb_%d" % j) for j, d in enumerate(ds1)])

    do0, dg0, g_hw_out, g_post0, g_head = _out_bwd(dh1, u0, o0, g0, hw_out, post0, head_gain, "out_bwd_hgrn")
    dq0, df0, dv0, dlam = _hgrn_bwd(q0, f0, v0, lam, sst, do0, bsz, nb)
    ds0 = (dq0, df0, dv0, dg0)
    dh0, g_pre0 = _proj_bwd(ds0, hw_in, h0, pre0, dh1, "proj_bwd_hgrn")
    g_hw_in = jnp.stack([_weight_grad(y0, d, "wgrad_hgrn_%d" % j) for j, d in enumerate(ds0)])

    dh0 = dh0.reshape(bsz, lp, D_MODEL)
    grad_x = dh0[:, TILE:, :]
    g_meta = jnp.sum(dh0[:, N_PAD:TILE, :], axis=0)
    g_lam = jnp.sum(dlam, axis=0)
    small = jnp.concatenate([g_pre0, g_pre1, g_post0, g_post1, g_lam, g_lam,
                             jnp.pad(g_head, ((0, 0), (0, D_MODEL - D_HEAD))), g_meta,
                             jnp.zeros((SMALL_ROWS - 23, D_MODEL), F32)], axis=0)
    grads = dict(hw_in=g_hw_in, sw_in=g_sw_in, hw_out=g_hw_out.reshape(N_CHIPS, D_MODEL // N_CHIPS, D_MODEL),
                 sw_out=g_sw_out.reshape(N_CHIPS, D_MODEL // N_CHIPS, D_MODEL), small=small)
    return loss_blk, grad_x, grads


def _place():
    x, y, c = lax.axis_index("x"), lax.axis_index("y"), lax.axis_index("c")
    return x, y, c


def _gather_weights(hw_in, sw_in, hw_out, sw_out, meta):
    def body(hi_ref, si_ref, ho_ref, so_ref, m_ref, ghi, gsi, gho, gso, gm, send_sems, recv_sems):
        x, y, c = _place()
        me = 2 * x + y
        ghi[me] = hi_ref[0].astype(_MXU)
        gsi[me] = si_ref[0].astype(_MXU)
        gho[me] = ho_ref[0].astype(_MXU)
        gso[me] = so_ref[0].astype(_MXU)
        gm[me] = m_ref[...]
        outs = (ghi, gsi, gho, gso, gm)
        peers = [(1 - x, y), (x, 1 - y), (1 - x, 1 - y)]

        def copy(r, a, slot, to):
            return pltpu.make_async_remote_copy(
                src_ref=outs[a].at[slot], dst_ref=outs[a].at[slot], send_sem=send_sems.at[r * 5 + a],
                recv_sem=recv_sems.at[r * 5 + a], device_id=to, device_id_type=MESH)

        sends = [copy(r, a, me, (px, py, c)) for r, (px, py) in enumerate(peers) for a in range(5)]
        for cp in sends:
            cp.start()
        for r, (px, py) in enumerate(peers):
            for a in range(5):
                copy(r, a, 2 * px + py, (px, py, c)).wait_recv()
        for cp in sends:
            cp.wait_send()

    d4 = D_MODEL // N_CHIPS
    vm = pl.BlockSpec(memory_space=pltpu.VMEM)
    return pl.pallas_call(
        body, name="gather_weights",
        in_specs=[vm] * 5, out_specs=[vm] * 5,
        out_shape=[jax.ShapeDtypeStruct((N_CHIPS, D_MODEL, D_MODEL), _MXU)] * 2
        + [jax.ShapeDtypeStruct((N_CHIPS, d4, D_MODEL), _MXU)] * 2
        + [jax.ShapeDtypeStruct((N_CHIPS, N_META, d4), F32)],
        scratch_shapes=[pltpu.SemaphoreType.DMA((15,)), pltpu.SemaphoreType.DMA((15,))],
        compiler_params=pltpu.CompilerParams(vmem_limit_bytes=VMEM_LIMIT),
    )(hw_in, sw_in, hw_out, sw_out, meta)


def _scatter_grads(g):
    names = ("hw_in", "sw_in", "hw_out", "sw_out")

    def body(a0, a1, a2, a3, sm, l0, l1, l2, l3, lsm, send_sems, recv_sems, small_send, small_recv, local_sems):
        x, y, c = _place()
        me = 2 * x + y
        srcs, lands = (a0, a1, a2, a3), (l0, l1, l2, l3)
        peers = [(1 - x, y), (x, 1 - y), (1 - x, 1 - y)]
        local = [pltpu.make_async_copy(srcs[a].at[me], lands[a].at[me], local_sems.at[a]) for a in range(4)]
        local.append(pltpu.make_async_copy(sm, lsm.at[4 * x + 2 * y + c], local_sems.at[4]))
        for cp in local:
            cp.start()

        def copy(r, a, src_slot, dst_slot, to):
            return pltpu.make_async_remote_copy(
                src_ref=srcs[a].at[src_slot], dst_ref=lands[a].at[dst_slot], send_sem=send_sems.at[r * 4 + a],
                recv_sem=recv_sems.at[r * 4 + a], device_id=to, device_id_type=MESH)

        def small_copy(rel, src_dev, to):
            return pltpu.make_async_remote_copy(
                src_ref=sm, dst_ref=lsm.at[src_dev], send_sem=small_send.at[rel - 1], recv_sem=small_recv.at[rel - 1],
                device_id=to, device_id_type=MESH)

        sends = [copy(r, a, 2 * px + py, me, (px, py, c)) for r, (px, py) in enumerate(peers) for a in range(4)]
        flip = lambda bit, v: 1 - v if bit else v
        rels = [(rel, flip(rel & 4, x), flip(rel & 2, y), flip(rel & 1, c)) for rel in range(1, N_DEV)]
        sends += [small_copy(rel, 4 * x + 2 * y + c, (px, py, pc)) for rel, px, py, pc in rels]
        for cp in sends:
            cp.start()
        for r, (px, py) in enumerate(peers):
            for a in range(4):
                copy(r, a, me, 2 * px + py, (px, py, c)).wait_recv()
        for rel, px, py, pc in rels:
            small_copy(rel, 4 * px + 2 * py + pc, (px, py, pc)).wait_recv()
        for cp in sends:
            cp.wait_send()
        for cp in local:
            cp.wait()

    hbm = pl.BlockSpec(memory_space=pl.ANY)
    args = [g[n] for n in names] + [g["small"]]
    return pl.pallas_call(
        body, name="scatter_grads",
        in_specs=[hbm] * 5, out_specs=[hbm] * 5,
        out_shape=[jax.ShapeDtypeStruct(a.shape, F32) for a in args[:4]]
        + [jax.ShapeDtypeStruct((N_DEV, SMALL_ROWS, D_MODEL), F32)],
        scratch_shapes=[pltpu.SemaphoreType.DMA((12,)), pltpu.SemaphoreType.DMA((12,)),
                        pltpu.SemaphoreType.DMA((N_DEV - 1,)), pltpu.SemaphoreType.DMA((N_DEV - 1,)),
                        pltpu.SemaphoreType.DMA((5,))],
    )(*args)


def _sum_slots(land, name):
    n, rows, _ = land.shape
    tm = rows if rows < 256 else 256

    def body(l_ref, o_ref):
        acc = l_ref[0]
        for k in range(1, n):
            acc = acc + l_ref[k]
        o_ref[...] = acc

    return pl.pallas_call(
        body, name=name, grid=(rows // tm,),
        in_specs=[pl.BlockSpec((n, tm, D_MODEL), lambda i: (0, i, 0))],
        out_specs=pl.BlockSpec((tm, D_MODEL), lambda i: (i, 0)),
        out_shape=jax.ShapeDtypeStruct((rows, D_MODEL), F32),
        compiler_params=_params(("arbitrary",)),
    )(land)


def _swap_with_sibling(parts):
    def body(a0, a1, a2, a3, b0, b1, b2, b3, send_sems, recv_sems):
        x, y, c = _place()
        copies = [pltpu.make_async_remote_copy(src_ref=s, dst_ref=d, send_sem=send_sems.at[a], recv_sem=recv_sems.at[a],
                                               device_id=(x, y, 1 - c), device_id_type=MESH)
                  for a, (s, d) in enumerate(zip((a0, a1, a2, a3), (b0, b1, b2, b3)))]
        for cp in copies:
            cp.start()
        for cp in copies:
            cp.wait()

    hbm = pl.BlockSpec(memory_space=pl.ANY)
    return pl.pallas_call(
        body, name="swap_with_sibling", in_specs=[hbm] * 4, out_specs=[hbm] * 4,
        out_shape=[jax.ShapeDtypeStruct(p.shape, F32) for p in parts],
        scratch_shapes=[pltpu.SemaphoreType.DMA((4,)), pltpu.SemaphoreType.DMA((4,))],
    )(*parts)


def _adamw_math(w, g, m, v):
    m = ADAM_B1 * m + (1.0 - ADAM_B1) * g
    v = ADAM_B2 * v + (1.0 - ADAM_B2) * (g * g)
    m_hat = m / (1.0 - ADAM_B1 ** ADAM_STEP)
    v_hat = v / (1.0 - ADAM_B2 ** ADAM_STEP)
    delta = -ADAM_LR * (m_hat / (jnp.sqrt(v_hat) + ADAM_EPS) + ADAM_WD * w)
    return delta, m, v


def _adamw(w, g_parts, m, v, name):
    rows, cols = w.shape
    tm = rows if rows < 256 else 256
    n = len(g_parts)

    def body(*refs):
        w_ref, m_ref, v_ref = refs[n:n + 3]
        g_ref, d_ref, nm_ref, nv_ref = refs[n + 3:]
        g = refs[0][...]
        for p in refs[1:n]:
            g = g + p[...]
        g_ref[...] = g
        d_ref[...], nm_ref[...], nv_ref[...] = _adamw_math(w_ref[...], g, m_ref[...], v_ref[...])

    blk = pl.BlockSpec((tm, cols), lambda i: (i, 0))
    return pl.pallas_call(
        body, name=name, grid=(rows // tm,), in_specs=[blk] * (n + 3), out_specs=[blk] * 4,
        out_shape=[jax.ShapeDtypeStruct((rows, cols), F32)] * 4,
        compiler_params=_params(("arbitrary",)),
    )(*g_parts, w, m, v)


def _lam_of(hgrn_lb):
    def body(lb_ref, o_ref):
        lb = lb_ref[...]
        e = jnp.exp(lb - jnp.max(lb, axis=0, keepdims=True))
        o_ref[...] = e[0:1, :] / jnp.sum(e, axis=0, keepdims=True)

    return pl.pallas_call(body, name="lam_of", out_shape=jax.ShapeDtypeStruct((1, D_MODEL), F32))(hgrn_lb)


def _small_grads(land_small, lam):
    def body(l_ref, lam_ref, o_ref):
        acc = l_ref[0]
        for k in range(1, N_DEV):
            acc = acc + l_ref[k]
        p = lam_ref[...]
        slope = p * (1.0 - p)
        row = _iota2((SMALL_ROWS, D_MODEL), 0)
        o_ref[...] = acc * jnp.where(row == 4, slope, jnp.where(row == 5, -slope, 1.0))

    return pl.pallas_call(body, name="small_grads",
                          out_shape=jax.ShapeDtypeStruct((SMALL_ROWS, D_MODEL), F32))(land_small, lam)


def kernel(x, meta_tokens, pre_norm, post_norm, hgrn_w_in, hgrn_lb, hgrn_out_norm, hgrn_w_out, sb_w_in, sb_w_out, loss_target, m_meta_tokens, m_pre_norm, m_post_norm, m_hgrn_w_in, m_hgrn_lb, m_hgrn_out_norm, m_hgrn_w_out, m_sb_w_in, m_sb_w_out, v_meta_tokens, v_pre_norm, v_post_norm, v_hgrn_w_in, v_hgrn_lb, v_hgrn_out_norm, v_hgrn_w_out, v_sb_w_in, v_sb_w_out):
    d4 = D_MODEL // N_CHIPS
    chip = 2 * lax.axis_index("x") + lax.axis_index("y")
    hw_in, sw_in, hw_out, sw_out, meta4 = _gather_weights(hgrn_w_in, sb_w_in, hgrn_w_out, sb_w_out, meta_tokens)
    meta = meta4.transpose(1, 0, 2).reshape(N_META, D_MODEL)
    lam = _lam_of(hgrn_lb)
    loss_blk, grad_x, grads = _local_step(
        x, loss_target, meta, pre_norm, post_norm, lam, hgrn_out_norm,
        hw_in, hw_out.reshape(D_MODEL, D_MODEL), sw_in, sw_out.reshape(D_MODEL, D_MODEL))
    loss = lax.psum(loss_blk[0, 0], ("x", "y", "c"))

    l_hi, l_si, l_ho, l_so, l_small = _scatter_grads(grads)
    parts = [_sum_slots(l_hi, "sum_hw_in"), _sum_slots(l_si, "sum_sw_in"),
             _sum_slots(l_ho, "sum_hw_out"), _sum_slots(l_so, "sum_sw_out")]
    sib = _swap_with_sibling(parts)
    small = _small_grads(l_small, lam)

    res = {}
    res["hgrn_w_in"] = _adamw(hgrn_w_in[0], [parts[0], sib[0]], m_hgrn_w_in[0], v_hgrn_w_in[0], "adamw_hw_in")
    res["sb_w_in"] = _adamw(sb_w_in[0], [parts[1], sib[1]], m_sb_w_in[0], v_sb_w_in[0], "adamw_sw_in")
    res["hgrn_w_out"] = _adamw(hgrn_w_out[0], [parts[2], sib[2]], m_hgrn_w_out[0], v_hgrn_w_out[0], "adamw_hw_out")
    res["sb_w_out"] = _adamw(sb_w_out[0], [parts[3], sib[3]], m_sb_w_out[0], v_sb_w_out[0], "adamw_sw_out")
    res["pre_norm"] = _adamw(pre_norm, [small[0:2]], m_pre_norm, v_pre_norm, "adamw_pre")
    res["post_norm"] = _adamw(post_norm, [small[2:4]], m_post_norm, v_post_norm, "adamw_post")
    res["hgrn_lb"] = _adamw(hgrn_lb, [small[4:6]], m_hgrn_lb, v_hgrn_lb, "adamw_lb")
    res["hgrn_out_norm"] = _adamw(hgrn_out_norm, [small[6:7, :D_HEAD]], m_hgrn_out_norm, v_hgrn_out_norm, "adamw_head")
    g_meta = lax.dynamic_slice_in_dim(small[7:7 + N_META], chip * d4, d4, axis=1)
    res["meta_tokens"] = _adamw(meta_tokens, [g_meta], m_meta_tokens, v_meta_tokens, "adamw_meta")
    for n in ("hgrn_w_in", "hgrn_w_out", "sb_w_in", "sb_w_out"):
        res[n] = tuple(a[None] for a in res[n])
    order = ("meta_tokens", "pre_norm", "post_norm", "hgrn_w_in", "hgrn_lb", "hgrn_out_norm", "hgrn_w_out",
             "sb_w_in", "sb_w_out")
    return (loss, grad_x, *[res[n][0] for n in order], *[res[n][1] for n in order],
            *[res[n][2] for n in order], *[res[n][3] for n in order])
```

```python
import functools

import jax
import numpy as np
import jax.numpy as jnp
from jax import lax
from jax.experimental import pallas as pl
from jax.experimental.pallas import tpu as pltpu

F32 = jnp.float32
_MXU = jnp.bfloat16

D_MODEL = 1024
N_HEADS = 8
D_HEAD = 128
BLOCK = 128
N_META = 16
TILE = 256
N_PAD = TILE - N_META
UNDERFLOW = -104.0
EPS = 1e-6
SB_SCALE = D_HEAD ** -0.5
ADAM_LR, ADAM_B1, ADAM_B2, ADAM_EPS, ADAM_WD, ADAM_STEP = 0.001, 0.9, 0.999, 1e-08, 0.01, 10
N_CHIPS = 4
N_DEV = 8
SMALL_ROWS = 24
VMEM_LIMIT = 56 * 1024 * 1024
MESH = pl.DeviceIdType.MESH

_NT = (((1,), (1,)), ((), ()))
_TN = (((0,), (0,)), ((), ()))


def _mm(a, b):
    return jnp.dot(a.astype(_MXU), b.astype(_MXU), preferred_element_type=F32)


def _mm_nt(a, b):
    return lax.dot_general(a.astype(_MXU), b.astype(_MXU), _NT, preferred_element_type=F32)


def _mm_tn(a, b):
    return lax.dot_general(a.astype(_MXU), b.astype(_MXU), _TN, preferred_element_type=F32)


def _split2(x):
    hi = x.astype(_MXU)
    return hi, (x - hi.astype(F32)).astype(_MXU)


def _mm_s(a16, state):
    hi, lo = _split2(state)
    return jnp.dot(a16, hi, preferred_element_type=F32) + jnp.dot(a16, lo, preferred_element_type=F32)


def _mm_nt_s(a16, state):
    hi, lo = _split2(state)
    return (lax.dot_general(a16, hi, _NT, preferred_element_type=F32)
            + lax.dot_general(a16, lo, _NT, preferred_element_type=F32))


def _split3(x):
    hi = x.astype(_MXU)
    r1 = x - hi.astype(F32)
    mid = r1.astype(_MXU)
    lo = (r1 - mid.astype(F32)).astype(_MXU)
    return hi, mid, lo


def _mm01_right(x, m01):
    hi, lo = _split2(x)
    return jnp.dot(hi, m01, preferred_element_type=F32) + jnp.dot(lo, m01, preferred_element_type=F32)


def _mm01_left(m01, x):
    hi, mid, lo = _split3(x)
    d = functools.partial(jnp.dot, preferred_element_type=F32)
    return d(m01, hi) + d(m01, mid) + d(m01, lo)


def _iota2(shape, dim):
    return lax.broadcasted_iota(jnp.int32, shape, dim)


def _row_tile(total, pref):
    t = pref
    while total % t:
        t -= BLOCK
    return t


def _params(sem, limit=VMEM_LIMIT):
    return pltpu.CompilerParams(dimension_semantics=sem, vmem_limit_bytes=limit)


def _sigmoid(x):
    return 1.0 / (1.0 + jnp.exp(-x))


def _norm_proj(h, gain, w4, name):
    t = h.shape[0]
    tm = _row_tile(t, 256)

    def body(h_ref, g_ref, w_ref, y_ref, s0, s1, s2, s3):
        x = h_ref[...]
        r = lax.rsqrt(jnp.mean(x * x, axis=-1, keepdims=True) + EPS)
        y = (x * r * g_ref[...]).astype(_MXU)
        y_ref[...] = y
        for j, s in enumerate((s0, s1, s2, s3)):
            s[...] = jnp.dot(y, w_ref[j], preferred_element_type=F32)

    row = pl.BlockSpec((tm, D_MODEL), lambda i: (i, 0))
    return pl.pallas_call(
        body, name=name, grid=(t // tm,),
        in_specs=[row, pl.BlockSpec((1, D_MODEL), lambda i: (0, 0)),
                  pl.BlockSpec((4, D_MODEL, D_MODEL), lambda i: (0, 0, 0))],
        out_specs=[row] * 5,
        out_shape=[jax.ShapeDtypeStruct((t, D_MODEL), _MXU)] + [jax.ShapeDtypeStruct((t, D_MODEL), F32)] * 4,
        compiler_params=_params(("arbitrary",)),
    )(h, gain, w4)


LEVELS = (64, 32, 16, 8, 4, 2, 1)
HEAD_GROUP = 2


def _hgrn_tables():
    r = np.arange(BLOCK)
    mats = []
    for m in (4, 2):
        same = r[:, None] // m == r[None, :] // m
        second = ((r // m) % 2 == 1)[:, None]
        mats.append(same & np.where(second, r[None, :] <= r[:, None], r[None, :] > r[:, None]))
    mats.append(r[None, :] <= r[:, None])
    x = r[:, None] ^ r[None, :]
    lv = np.full((BLOCK, BLOCK), len(LEVELS), np.int32)
    for i, m in enumerate(LEVELS):
        lv[(x >= m) & (x < 2 * m)] = i
    return jnp.asarray(np.concatenate(mats, 0).astype(np.float32), dtype=_MXU), jnp.asarray(lv)


def _hgrn_exponents(g, sums):
    e = _mm01_left(sums, g)
    b = e[2 * BLOCK:, :]
    row = _iota2((BLOCK, D_HEAD), 0)
    out = []
    for m in LEVELS:
        is_q = (row & m) != 0
        if m >= 8:
            grp = b.reshape(BLOCK // (2 * m), 2 * m, D_HEAD)
            ref = jnp.broadcast_to(grp[:, m - 1:m, :], grp.shape).reshape(BLOCK, D_HEAD)
            out.append(jnp.where(is_q, b - ref, ref - b))
        elif m == 4:
            out.append(e[:BLOCK, :])
        elif m == 2:
            out.append(e[BLOCK:2 * BLOCK, :])
        else:
            out.append(jnp.where(is_q, g, 0.0))
    return b, out


def _hgrn_gates(fz, lam, chunk):
    pos = chunk * BLOCK + _iota2((BLOCK, D_HEAD), 0)
    live = pos >= N_PAD
    sg = _sigmoid(fz)
    f = lam + (1.0 - lam) * sg
    g = jnp.where(live, jnp.log(f), 0.0)
    k = jnp.where(live, (1.0 - lam) * (1.0 - sg), 0.0)
    return sg, f, g, k, live


def _level_operand(q, k, exponent, m):
    decay = jnp.exp(exponent)
    is_q = (_iota2((BLOCK, D_HEAD), 0) & m) != 0
    return is_q, decay, (jnp.where(is_q, q, k) * decay).astype(_MXU)


def _hgrn_fwd(qs, fs, vs, lam, bsz, nb):
    t = qs.shape[0]
    sums, levels = _hgrn_tables()
    width = HEAD_GROUP * D_HEAD

    def body(q_ref, f_ref, v_ref, lam_ref, sums_ref, lv_ref, o_ref, sst_ref, st_scr):
        n = pl.program_id(2)

        @pl.when(n == 0)
        def _():
            st_scr[...] = jnp.zeros_like(st_scr)

        lv = lv_ref[...]
        r, c = _iota2((BLOCK, BLOCK), 0), _iota2((BLOCK, BLOCK), 1)
        for hh in range(HEAD_GROUP):
            ls = slice(hh * D_HEAD, (hh + 1) * D_HEAD)
            st = st_scr[hh]
            sst_ref[0, hh, 0] = st
            q, v = q_ref[:, ls], v_ref[:, ls]
            _, _, g, k, _ = _hgrn_gates(f_ref[:, ls], lam_ref[:, ls], n)
            b, exps = _hgrn_exponents(g, sums_ref[...])
            sym = jnp.zeros((BLOCK, BLOCK), F32)
            for li, m in enumerate(LEVELS):
                _, _, x16 = _level_operand(q, k, exps[li], m)
                sym = jnp.where(lv == li, lax.dot_general(x16, x16, _NT, preferred_element_type=F32), sym)
            a = jnp.where(c < r, sym, jnp.where(c == r, jnp.sum(q * k, axis=1, keepdims=True), 0.0))
            o_ref[:, ls] = _mm_nt_s((q * jnp.exp(b)).astype(_MXU), st) + _mm(a, v)
            b_end = b[BLOCK - 1:BLOCK, :]
            st_scr[hh] = st * jnp.exp(b_end) + _mm_tn(v, k * jnp.exp(b_end - b))

    blk = pl.BlockSpec((BLOCK, width), lambda b, h, n: (b * nb + n, h))
    return pl.pallas_call(
        body, name="hgrn_fwd", grid=(bsz, N_HEADS // HEAD_GROUP, nb),
        in_specs=[blk, blk, blk, pl.BlockSpec((1, width), lambda b, h, n: (0, h)),
                  pl.BlockSpec(sums.shape, lambda b, h, n: (0, 0)), pl.BlockSpec(levels.shape, lambda b, h, n: (0, 0))],
        out_specs=[blk, pl.BlockSpec((1, HEAD_GROUP, 1, D_HEAD, D_HEAD), lambda b, h, n: (b, h, n, 0, 0))],
        out_shape=[jax.ShapeDtypeStruct((t, D_MODEL), F32),
                   jax.ShapeDtypeStruct((bsz, N_HEADS, nb, D_HEAD, D_HEAD), F32)],
        scratch_shapes=[pltpu.VMEM((HEAD_GROUP, D_HEAD, D_HEAD), F32)],
        compiler_params=_params(("arbitrary", "arbitrary", "arbitrary")),
    )(qs, fs, vs, lam, sums, levels)


def _hgrn_bwd(qs, fs, vs, lam, sst, do, bsz, nb):
    t = qs.shape[0]
    sums, levels = _hgrn_tables()
    width = HEAD_GROUP * D_HEAD

    def body(q_ref, f_ref, v_ref, lam_ref, sst_ref, do_ref, sums_ref, lv_ref, dq_ref, df_ref, dv_ref, dlam_ref,
             dst_scr, gsum_scr):
        n = pl.program_id(2)
        chunk = nb - 1 - n

        @pl.when(n == 0)
        def _():
            dst_scr[...] = jnp.zeros_like(dst_scr)
            gsum_scr[...] = jnp.zeros_like(gsum_scr)
            dlam_ref[...] = jnp.zeros_like(dlam_ref)

        lv = lv_ref[...]
        r, c = _iota2((BLOCK, BLOCK), 0), _iota2((BLOCK, BLOCK), 1)
        for hh in range(HEAD_GROUP):
            ls = slice(hh * D_HEAD, (hh + 1) * D_HEAD)
            lam = lam_ref[:, ls]
            q, v, do = q_ref[:, ls], v_ref[:, ls], do_ref[:, ls]
            sg, f, g, k, live = _hgrn_gates(f_ref[:, ls], lam, chunk)
            b, exps = _hgrn_exponents(g, sums_ref[...])
            do16, v16 = do.astype(_MXU), v.astype(_MXU)
            da = lax.dot_general(do16, v16, _NT, preferred_element_type=F32)
            da_t = lax.dot_general(v16, do16, _NT, preferred_element_type=F32)
            da_sym = jnp.where(c < r, da, da_t)
            sym = jnp.zeros((BLOCK, BLOCK), F32)
            dq = jnp.zeros((BLOCK, D_HEAD), F32)
            dk = jnp.zeros((BLOCK, D_HEAD), F32)
            db = jnp.zeros((BLOCK, D_HEAD), F32)
            for li, m in enumerate(LEVELS):
                is_q, decay, x16 = _level_operand(q, k, exps[li], m)
                here = lv == li
                sym = jnp.where(here, lax.dot_general(x16, x16, _NT, preferred_element_type=F32), sym)
                y = jnp.dot(jnp.where(here, da_sym, 0.0).astype(_MXU), x16, preferred_element_type=F32)
                dx = y * decay
                dxq = jnp.where(is_q, dx, 0.0)
                dq = dq + dxq
                dk = dk + (dx - dxq)
                p = x16.astype(F32) * y
                db = db + jnp.where(is_q, p, -p)
            on_diag = jnp.sum(q * k, axis=1, keepdims=True)
            a_t = jnp.where(c > r, sym, jnp.where(c == r, on_diag, 0.0))
            st, dst = sst_ref[0, hh, 0], dst_scr[hh]
            eb = jnp.exp(b)
            b_end = b[BLOCK - 1:BLOCK, :]
            dec = jnp.exp(b_end - b)
            qh16, kt16 = (q * eb).astype(_MXU), (k * dec).astype(_MXU)
            dq_st = _mm_s(do16, st)
            dk_st = _mm_s(v16, dst)
            d_diag = jnp.sum(do * v, axis=1, keepdims=True)
            dq_ref[:, ls] = dq + d_diag * k + eb * dq_st
            dk = dk + d_diag * q + dec * dk_st
            dv_ref[:, ls] = jnp.dot(a_t.astype(_MXU), do16, preferred_element_type=F32) + _mm_nt_s(kt16, dst)
            dst_scr[hh] = dst * jnp.exp(b_end) + lax.dot_general(do16, qh16, _TN, preferred_element_type=F32)
            db = db + (qh16.astype(F32) * dq_st - kt16.astype(F32) * dk_st)
            dg = _mm01_left((c >= r).astype(_MXU), db) + gsum_scr[:, ls]
            gsum_scr[:, ls] = gsum_scr[:, ls] + jnp.sum(db, axis=0, keepdims=True)
            slope = (1.0 - lam) * sg * (1.0 - sg)
            df_ref[:, ls] = jnp.where(live, dg * slope / f - dk * slope, 0.0)
            dl = jnp.where(live, (dg / f - dk) * (1.0 - sg), 0.0)
            dlam_ref[0, :, ls] = dlam_ref[0, :, ls] + jnp.sum(dl, axis=0, keepdims=True)

    blk = pl.BlockSpec((BLOCK, width), lambda b, h, n: (b * nb + nb - 1 - n, h))
    return pl.pallas_call(
        body, name="hgrn_bwd", grid=(bsz, N_HEADS // HEAD_GROUP, nb),
        in_specs=[blk, blk, blk, pl.BlockSpec((1, width), lambda b, h, n: (0, h)),
                  pl.BlockSpec((1, HEAD_GROUP, 1, D_HEAD, D_HEAD), lambda b, h, n: (b, h, nb - 1 - n, 0, 0)),
                  blk, pl.BlockSpec(sums.shape, lambda b, h, n: (0, 0)), pl.BlockSpec(levels.shape, lambda b, h, n: (0, 0))],
        out_specs=[blk, blk, blk, pl.BlockSpec((1, 1, width), lambda b, h, n: (b, 0, h))],
        out_shape=[jax.ShapeDtypeStruct((t, D_MODEL), F32)] * 3 + [jax.ShapeDtypeStruct((bsz, 1, D_MODEL), F32)],
        scratch_shapes=[pltpu.VMEM((HEAD_GROUP, D_HEAD, D_HEAD), F32), pltpu.VMEM((1, width), F32)],
        compiler_params=_params(("arbitrary", "arbitrary", "arbitrary")),
    )(qs, fs, vs, lam, sst, do, sums, levels)


def _sb_valid(i, j):
    qpos = i * TILE + _iota2((TILE, TILE), 0)
    kpos = j * TILE + _iota2((TILE, TILE), 1)
    return (kpos < qpos) & (kpos >= N_PAD)


def _sb_logits(q16, k_blk, valid):
    z = lax.dot_general(q16, k_blk.astype(_MXU), _NT, preferred_element_type=F32) * SB_SCALE
    softplus = jnp.maximum(z, 0.0) + jnp.log(1.0 + jnp.exp(-jnp.abs(z)))
    return jnp.where(valid, -softplus, 0.0), z - softplus


def _sb_fwd(qs, ks, vs, bsz, nq):
    t = qs.shape[0]
    lp = nq * TILE
    width = HEAD_GROUP * D_HEAD
    groups = N_HEADS // HEAD_GROUP
    lanes = [slice(hh * D_HEAD, (hh + 1) * D_HEAD) for hh in range(HEAD_GROUP)]

    def body(q_ref, k_ref, v_ref, o_ref, c_ref, n_ref):
        b, h, i = pl.program_id(0), pl.program_id(1), pl.program_id(2)
        q16 = [q_ref[:, ls].astype(_MXU) for ls in lanes]
        r, c = _iota2((TILE, TILE), 0), _iota2((TILE, TILE), 1)
        after = (r > c).astype(_MXU)

        def more(carry):
            jj, _, _, top = carry
            return (jj <= i) & (top > UNDERFLOW)

        def step(carry):
            jj, accs, sums, _ = carry
            j = i - jj
            ks_ = pl.ds(pl.multiple_of(j * TILE, TILE), TILE)
            valid = _sb_valid(i, j)
            new_accs, new_sums = [], []
            for hh, ls in enumerate(lanes):
                keep, log_beta = _sb_logits(q16[hh], k_ref[ks_, ls], valid)
                later = sums[hh] + _mm01_right(keep, after)
                a = jnp.where(valid, jnp.exp(log_beta + later), 0.0)
                new_accs.append(accs[hh] + _mm(a, v_ref[ks_, ls]))
                new_sums.append(sums[hh] + jnp.sum(keep, axis=1, keepdims=True))
            top = functools.reduce(jnp.maximum, [jnp.max(x) for x in new_sums])
            return jj + 1, tuple(new_accs), tuple(new_sums), top

        init = (jnp.int32(0), tuple(jnp.zeros((TILE, D_HEAD), F32) for _ in lanes),
                tuple(jnp.zeros((TILE, 1), F32) for _ in lanes), jnp.float32(0.0))
        visited, accs, sums, _ = lax.while_loop(more, step, init)
        for hh, ls in enumerate(lanes):
            o_ref[:, ls] = accs[hh]
            c_ref[:, ls] = jnp.broadcast_to(sums[hh], (TILE, D_HEAD))
        n_ref[(b * groups + h) * nq + i] = visited.astype(F32)

    blk = pl.BlockSpec((TILE, width), lambda b, h, i: (b * nq + i, h))
    seq = pl.BlockSpec((lp, width), lambda b, h, i: (b, h))
    return pl.pallas_call(
        body, name="sb_fwd", grid=(bsz, groups, nq),
        in_specs=[blk, seq, seq], out_specs=[blk, blk, pl.BlockSpec(memory_space=pltpu.SMEM)],
        out_shape=[jax.ShapeDtypeStruct((t, D_MODEL), F32)] * 2 + [jax.ShapeDtypeStruct((bsz * groups * nq,), F32)],
        compiler_params=_params(("arbitrary", "arbitrary", "arbitrary")),
    )(qs, ks, vs)


def _sb_bwd(qs, ks, vs, ctot, visited, do, bsz, nq):
    t = qs.shape[0]
    lp = nq * TILE
    width = HEAD_GROUP * D_HEAD
    groups = N_HEADS // HEAD_GROUP
    lanes = [slice(hh * D_HEAD, (hh + 1) * D_HEAD) for hh in range(HEAD_GROUP)]

    def body(n_ref, q_ref, k_ref, v_ref, c_ref, do_ref, dq_ref, dk_ref, dv_ref):
        b, h, i = pl.program_id(0), pl.program_id(1), pl.program_id(2)

        @pl.when(i == 0)
        def _():
            dk_ref[...] = jnp.zeros_like(dk_ref)
            dv_ref[...] = jnp.zeros_like(dv_ref)

        q16 = [q_ref[:, ls].astype(_MXU) for ls in lanes]
        do16 = [do_ref[:, ls].astype(_MXU) for ls in lanes]
        totals = [c_ref[:, hh * D_HEAD:hh * D_HEAD + 1] for hh in range(HEAD_GROUP)]
        r, c = _iota2((TILE, TILE), 0), _iota2((TILE, TILE), 1)
        upto = (r <= c).astype(_MXU)
        before = (r < c).astype(_MXU)
        first = jnp.maximum(i + 1 - n_ref[(b * groups + h) * nq + i].astype(jnp.int32), 0)

        def step(j, carry):
            ks_ = pl.ds(pl.multiple_of(j * TILE, TILE), TILE)
            valid = _sb_valid(i, j)
            out = []
            for hh, ls in enumerate(lanes):
                dq, keep_pre, g_pre = carry[hh]
                k_blk, v_blk = k_ref[ks_, ls], v_ref[ks_, ls]
                keep, log_beta = _sb_logits(q16[hh], k_blk, valid)
                later = totals[hh] - keep_pre - _mm01_right(keep, upto)
                a = jnp.where(valid, jnp.exp(log_beta + later), 0.0)
                da = lax.dot_general(do16[hh], v_blk.astype(_MXU), _NT, preferred_element_type=F32)
                g = a * da
                g_before = g_pre + _mm01_right(g, before)
                beta = jnp.exp(log_beta)
                dz = jnp.where(valid, g * (1.0 - beta) - beta * g_before, 0.0) * SB_SCALE
                dz16 = dz.astype(_MXU)
                dq = dq + jnp.dot(dz16, k_blk.astype(_MXU), preferred_element_type=F32)
                dk_ref[ks_, ls] += lax.dot_general(dz16, q16[hh], _TN, preferred_element_type=F32)
                dv_ref[ks_, ls] += lax.dot_general(a.astype(_MXU), do16[hh], _TN, preferred_element_type=F32)
                out.append((dq, keep_pre + jnp.sum(keep, axis=1, keepdims=True),
                            g_pre + jnp.sum(g, axis=1, keepdims=True)))
            return tuple(out)

        zero_col = jnp.zeros((TILE, 1), F32)
        init = tuple((jnp.zeros((TILE, D_HEAD), F32), zero_col, zero_col) for _ in lanes)
        res = lax.fori_loop(first, i + 1, step, init)
        for hh, ls in enumerate(lanes):
            dq_ref[:, ls] = res[hh][0]

    blk = pl.BlockSpec((TILE, width), lambda b, h, i: (b * nq + i, h))
    seq = pl.BlockSpec((lp, width), lambda b, h, i: (b, h))
    return pl.pallas_call(
        body, name="sb_bwd", grid=(bsz, groups, nq),
        in_specs=[pl.BlockSpec(memory_space=pltpu.SMEM), blk, seq, seq, blk, blk], out_specs=[blk, seq, seq],
        out_shape=[jax.ShapeDtypeStruct((t, D_MODEL), F32)] * 3,
        compiler_params=_params(("arbitrary", "arbitrary", "arbitrary")),
    )(visited, qs, ks, vs, ctot, do)


def _head_norm(o, head_gain):
    outs, rs = [], []
    for h in range(N_HEADS):
        oh = o[:, h * D_HEAD:(h + 1) * D_HEAD]
        r = lax.rsqrt(jnp.mean(oh * oh, axis=-1, keepdims=True) + EPS)
        outs.append(oh * r)
        rs.append(r)
    return outs, rs


def _mix(o, gate, head_gain):
    if head_gain is None:
        on = o
    else:
        outs, _ = _head_norm(o, head_gain)
        on = jnp.concatenate([x * head_gain for x in outs], axis=1)
    return on, on * (gate * _sigmoid(gate))


def _out_fwd(o, gate, h_in, w_out, post_gain, head_gain, name):
    t = o.shape[0]
    tm = _row_tile(t, 256)
    has_head = head_gain is not None

    def body(*refs):
        if has_head:
            o_ref, g_ref, h_ref, w_ref, pg_ref, hg_ref, ho_ref, u_ref = refs
            hg = hg_ref[...]
        else:
            o_ref, g_ref, h_ref, w_ref, pg_ref, ho_ref, u_ref = refs
            hg = None
        _, mix = _mix(o_ref[...], g_ref[...], hg)
        u = jnp.dot(mix.astype(_MXU), w_ref[...], preferred_element_type=F32)
        u_ref[...] = u
        r = lax.rsqrt(jnp.mean(u * u, axis=-1, keepdims=True) + EPS)
        ho_ref[...] = h_ref[...] + u * r * pg_ref[...]

    row = pl.BlockSpec((tm, D_MODEL), lambda i: (i, 0))
    vec = pl.BlockSpec((1, D_MODEL), lambda i: (0, 0))
    in_specs = [row, row, row, pl.BlockSpec((D_MODEL, D_MODEL), lambda i: (0, 0)), vec]
    args = [o, gate, h_in, w_out, post_gain]
    if has_head:
        in_specs.append(pl.BlockSpec((1, D_HEAD), lambda i: (0, 0)))
        args.append(head_gain)
    return pl.pallas_call(
        body, name=name, grid=(t // tm,), in_specs=in_specs, out_specs=[row, row],
        out_shape=[jax.ShapeDtypeStruct((t, D_MODEL), F32)] * 2,
        compiler_params=_params(("arbitrary",)),
    )(*args)


def _out_bwd(dh, u, o, gate, w_out, post_gain, head_gain, name):
    t = o.shape[0]
    tm = _row_tile(t, 256)
    has_head = head_gain is not None

    def body(*refs):
        if has_head:
            dh_ref, u_ref, o_ref, g_ref, w_ref, pg_ref, hg_ref, do_ref, dg_ref, gw_ref, gp_ref, gh_ref = refs
            hg = hg_ref[...]
        else:
            dh_ref, u_ref, o_ref, g_ref, w_ref, pg_ref, do_ref, dg_ref, gw_ref, gp_ref = refs
            hg = None
        first = pl.program_id(0) == 0

        @pl.when(first)
        def _():
            gw_ref[...] = jnp.zeros_like(gw_ref)
            gp_ref[...] = jnp.zeros_like(gp_ref)
            if has_head:
                gh_ref[...] = jnp.zeros_like(gh_ref)

        dr, u, o, gate = dh_ref[...], u_ref[...], o_ref[...], g_ref[...]
        r = lax.rsqrt(jnp.mean(u * u, axis=-1, keepdims=True) + EPS)
        un = u * r
        gp_ref[...] += jnp.sum(dr * un, axis=0, keepdims=True)
        dun = dr * pg_ref[...]
        du = r * (dun - un * jnp.mean(dun * un, axis=-1, keepdims=True))
        on, mix = _mix(o, gate, hg)
        du16 = du.astype(_MXU)
        gw_ref[...] += lax.dot_general(mix.astype(_MXU), du16, _TN, preferred_element_type=F32)
        dmix = lax.dot_general(du16, w_ref[...], _NT, preferred_element_type=F32)
        sg = _sigmoid(gate)
        dg_ref[...] = dmix * on * (sg * (1.0 + gate * (1.0 - sg)))
        don = dmix * (gate * sg)
        if has_head:
            outs, rs = _head_norm(o, hg)
            gh = jnp.zeros((1, D_HEAD), F32)
            cols = []
            for h in range(N_HEADS):
                dn = don[:, h * D_HEAD:(h + 1) * D_HEAD]
                gh = gh + jnp.sum(dn * outs[h], axis=0, keepdims=True)
                dnn = dn * hg
                cols.append(rs[h] * (dnn - outs[h] * jnp.mean(dnn * outs[h], axis=-1, keepdims=True)))
            gh_ref[...] += gh
            do_ref[...] = jnp.concatenate(cols, axis=1)
        else:
            do_ref[...] = don

    row = pl.BlockSpec((tm, D_MODEL), lambda i: (i, 0))
    vec = pl.BlockSpec((1, D_MODEL), lambda i: (0, 0))
    mat = pl.BlockSpec((D_MODEL, D_MODEL), lambda i: (0, 0))
    in_specs = [row, row, row, row, mat, vec]
    args = [dh, u, o, gate, w_out, post_gain]
    out_specs = [row, row, mat, vec]
    out_shape = [jax.ShapeDtypeStruct((t, D_MODEL), F32)] * 2 + [jax.ShapeDtypeStruct((D_MODEL, D_MODEL), F32),
                                                                  jax.ShapeDtypeStruct((1, D_MODEL), F32)]
    if has_head:
        in_specs.append(pl.BlockSpec((1, D_HEAD), lambda i: (0, 0)))
        args.append(head_gain)
        out_specs.append(pl.BlockSpec((1, D_HEAD), lambda i: (0, 0)))
        out_shape.append(jax.ShapeDtypeStruct((1, D_HEAD), F32))
    return pl.pallas_call(
        body, name=name, grid=(t // tm,), in_specs=in_specs, out_specs=out_specs, out_shape=out_shape,
        compiler_params=_params(("arbitrary",)),
    )(*args)


def _proj_bwd(ds, w4, h_in, gain, dh_out, name):
    t = h_in.shape[0]
    tm = _row_tile(t, 256)

    def body(d0, d1, d2, d3, w_ref, h_ref, g_ref, dho_ref, dhi_ref, gg_ref):
        @pl.when(pl.program_id(0) == 0)
        def _():
            gg_ref[...] = jnp.zeros_like(gg_ref)

        dy = jnp.zeros((tm, D_MODEL), F32)
        for j, d in enumerate((d0, d1, d2, d3)):
            dy = dy + lax.dot_general(d[...].astype(_MXU), w_ref[j], _NT, preferred_element_type=F32)
        x = h_ref[...]
        r = lax.rsqrt(jnp.mean(x * x, axis=-1, keepdims=True) + EPS)
        xn = x * r
        gg_ref[...] += jnp.sum(dy * xn, axis=0, keepdims=True)
        dxn = dy * g_ref[...]
        dhi_ref[...] = dho_ref[...] + r * (dxn - xn * jnp.mean(dxn * xn, axis=-1, keepdims=True))

    row = pl.BlockSpec((tm, D_MODEL), lambda i: (i, 0))
    vec = pl.BlockSpec((1, D_MODEL), lambda i: (0, 0))
    return pl.pallas_call(
        body, name=name, grid=(t // tm,),
        in_specs=[row] * 4 + [pl.BlockSpec((4, D_MODEL, D_MODEL), lambda i: (0, 0, 0)), row, vec, row],
        out_specs=[row, vec],
        out_shape=[jax.ShapeDtypeStruct((t, D_MODEL), F32), jax.ShapeDtypeStruct((1, D_MODEL), F32)],
        compiler_params=_params(("arbitrary",)),
    )(*ds, w4, h_in, gain, dh_out)


def _weight_grad(y, d, name):
    t = y.shape[0]
    tk = _row_tile(t, 512)

    def body(y_ref, d_ref, g_ref):
        @pl.when(pl.program_id(0) == 0)
        def _():
            g_ref[...] = jnp.zeros_like(g_ref)

        g_ref[...] += lax.dot_general(y_ref[...], d_ref[...].astype(_MXU), _TN, preferred_element_type=F32)

    row = pl.BlockSpec((tk, D_MODEL), lambda i: (i, 0))
    return pl.pallas_call(
        body, name=name, grid=(t // tk,), in_specs=[row, row],
        out_specs=pl.BlockSpec((D_MODEL, D_MODEL), lambda i: (0, 0)),
        out_shape=jax.ShapeDtypeStruct((D_MODEL, D_MODEL), F32),
        compiler_params=_params(("arbitrary",)),
    )(y, d)


def _loss_head(h, target, bsz, nq):
    t = h.shape[0]

    def body(h_ref, t_ref, dh_ref, l_ref):
        i = pl.program_id(1)

        @pl.when((pl.program_id(0) == 0) & (i == 0))
        def _():
            l_ref[...] = jnp.zeros_like(l_ref)

        @pl.when(i == 0)
        def _():
            dh_ref[...] = jnp.zeros_like(dh_ref)

        @pl.when(i > 0)
        def _():
            e = h_ref[...] - t_ref[...]
            dh_ref[...] = e * (1.0 / D_MODEL)
            l_ref[...] += jnp.sum(e * e) * (0.5 / D_MODEL)

    return pl.pallas_call(
        body, name="loss_head", grid=(bsz, nq),
        in_specs=[pl.BlockSpec((TILE, D_MODEL), lambda b, i: (b * nq + i, 0)),
                  pl.BlockSpec((TILE, D_MODEL), lambda b, i: (b * (nq - 1) + jnp.maximum(i - 1, 0), 0))],
        out_specs=[pl.BlockSpec((TILE, D_MODEL), lambda b, i: (b * nq + i, 0)),
                   pl.BlockSpec((8, 128), lambda b, i: (0, 0))],
        out_shape=[jax.ShapeDtypeStruct((t, D_MODEL), F32), jax.ShapeDtypeStruct((8, 128), F32)],
        compiler_params=_params(("arbitrary", "arbitrary")),
    )(h, target)


def _local_step(x, target, meta, pre_norm, post_norm, lam, head_gain, hw_in, hw_out, sw_in, sw_out):
    bsz, seq, _ = x.shape
    nq = seq // TILE + 1
    nb = nq * (TILE // BLOCK)
    lp = nq * TILE
    t = bsz * lp
    front = jnp.concatenate([jnp.zeros((N_PAD, D_MODEL), F32), meta], axis=0)
    h0 = jnp.concatenate([jnp.broadcast_to(front[None], (bsz, TILE, D_MODEL)), x], axis=1).reshape(t, D_MODEL)
    pre0, pre1, post0, post1 = pre_norm[0:1], pre_norm[1:2], post_norm[0:1], post_norm[1:2]

    y0, q0, f0, v0, g0 = _norm_proj(h0, pre0, hw_in, "norm_proj_hgrn")
    o0, sst = _hgrn_fwd(q0, f0, v0, lam, bsz, nb)
    h1, u0 = _out_fwd(o0, g0, h0, hw_out, post0, head_gain, "out_fwd_hgrn")
    y1, q1, k1, v1, g1 = _norm_proj(h1, pre1, sw_in, "norm_proj_sb")
    o1, ctot, visited = _sb_fwd(q1, k1, v1, bsz, nq)
    h2, u1 = _out_fwd(o1, g1, h1, sw_out, post1, None, "out_fwd_sb")

    dh2, loss_blk = _loss_head(h2, target.reshape(bsz * seq, D_MODEL), bsz, nq)

    do1, dg1, g_sw_out, g_post1 = _out_bwd(dh2, u1, o1, g1, sw_out, post1, None, "out_bwd_sb")
    dq1, dk1, dv1 = _sb_bwd(q1, k1, v1, ctot, visited, do1, bsz, nq)
    ds1 = (dq1, dk1, dv1, dg1)
    dh1, g_pre1 = _proj_bwd(ds1, sw_in, h1, pre1, dh2, "proj_bwd_sb")
    g_sw_in = jnp.stack([_weight_grad(y1, d, "wgrad_sb_%d" % j) for j, d in enumerate(ds1)])

    do0, dg0, g_hw_out, g_post0, g_head = _out_bwd(dh1, u0, o0, g0, hw_out, post0, head_gain, "out_bwd_hgrn")
    dq0, df0, dv0, dlam = _hgrn_bwd(q0, f0, v0, lam, sst, do0, bsz, nb)
    ds0 = (dq0, df0, dv0, dg0)
    dh0, g_pre0 = _proj_bwd(ds0, hw_in, h0, pre0, dh1, "proj_bwd_hgrn")
    g_hw_in = jnp.stack([_weight_grad(y0, d, "wgrad_hgrn_%d" % j) for j, d in enumerate(ds0)])

    dh0 = dh0.reshape(bsz, lp, D_MODEL)
    grad_x = dh0[:, TILE:, :]
    g_meta = jnp.sum(dh0[:, N_PAD:TILE, :], axis=0)
    g_lam = jnp.sum(dlam, axis=0)
    small = jnp.concatenate([g_pre0, g_pre1, g_post0, g_post1, g_lam, g_lam,
                             jnp.pad(g_head, ((0, 0), (0, D_MODEL - D_HEAD))), g_meta,
                             jnp.zeros((SMALL_ROWS - 23, D_MODEL), F32)], axis=0)
    grads = dict(hw_in=g_hw_in, sw_in=g_sw_in, hw_out=g_hw_out.reshape(N_CHIPS, D_MODEL // N_CHIPS, D_MODEL),
                 sw_out=g_sw_out.reshape(N_CHIPS, D_MODEL // N_CHIPS, D_MODEL), small=small)
    return loss_blk, grad_x, grads


def _place():
    x, y, c = lax.axis_index("x"), lax.axis_index("y"), lax.axis_index("c")
    return x, y, c


def _gather_weights(hw_in, sw_in, hw_out, sw_out, meta):
    def body(hi_ref, si_ref, ho_ref, so_ref, m_ref, ghi, gsi, gho, gso, gm, send_sems, recv_sems):
        x, y, c = _place()
        me = 2 * x + y
        ghi[me] = hi_ref[0].astype(_MXU)
        gsi[me] = si_ref[0].astype(_MXU)
        gho[me] = ho_ref[0].astype(_MXU)
        gso[me] = so_ref[0].astype(_MXU)
        gm[me] = m_ref[...]
        outs = (ghi, gsi, gho, gso, gm)
        peers = [(1 - x, y), (x, 1 - y), (1 - x, 1 - y)]

        def copy(r, a, slot, to):
            return pltpu.make_async_remote_copy(
                src_ref=outs[a].at[slot], dst_ref=outs[a].at[slot], send_sem=send_sems.at[r * 5 + a],
                recv_sem=recv_sems.at[r * 5 + a], device_id=to, device_id_type=MESH)

        sends = [copy(r, a, me, (px, py, c)) for r, (px, py) in enumerate(peers) for a in range(5)]
        for cp in sends:
            cp.start()
        for r, (px, py) in enumerate(peers):
            for a in range(5):
                copy(r, a, 2 * px + py, (px, py, c)).wait_recv()
        for cp in sends:
            cp.wait_send()

    d4 = D_MODEL // N_CHIPS
    vm = pl.BlockSpec(memory_space=pltpu.VMEM)
    return pl.pallas_call(
        body, name="gather_weights",
        in_specs=[vm] * 5, out_specs=[vm] * 5,
        out_shape=[jax.ShapeDtypeStruct((N_CHIPS, D_MODEL, D_MODEL), _MXU)] * 2
        + [jax.ShapeDtypeStruct((N_CHIPS, d4, D_MODEL), _MXU)] * 2
        + [jax.ShapeDtypeStruct((N_CHIPS, N_META, d4), F32)],
        scratch_shapes=[pltpu.SemaphoreType.DMA((15,)), pltpu.SemaphoreType.DMA((15,))],
        compiler_params=pltpu.CompilerParams(vmem_limit_bytes=VMEM_LIMIT),
    )(hw_in, sw_in, hw_out, sw_out, meta)


def _scatter_grads(g):
    names = ("hw_in", "sw_in", "hw_out", "sw_out")

    def body(a0, a1, a2, a3, sm, l0, l1, l2, l3, lsm, send_sems, recv_sems, small_send, small_recv, local_sems):
        x, y, c = _place()
        me = 2 * x + y
        srcs, lands = (a0, a1, a2, a3), (l0, l1, l2, l3)
        peers = [(1 - x, y), (x, 1 - y), (1 - x, 1 - y)]
        local = [pltpu.make_async_copy(srcs[a].at[me], lands[a].at[me], local_sems.at[a]) for a in range(4)]
        local.append(pltpu.make_async_copy(sm, lsm.at[4 * x + 2 * y + c], local_sems.at[4]))
        for cp in local:
            cp.start()

        def copy(r, a, src_slot, dst_slot, to):
            return pltpu.make_async_remote_copy(
                src_ref=srcs[a].at[src_slot], dst_ref=lands[a].at[dst_slot], send_sem=send_sems.at[r * 4 + a],
                recv_sem=recv_sems.at[r * 4 + a], device_id=to, device_id_type=MESH)

        def small_copy(rel, src_dev, to):
            return pltpu.make_async_remote_copy(
                src_ref=sm, dst_ref=lsm.at[src_dev], send_sem=small_send.at[rel - 1], recv_sem=small_recv.at[rel - 1],
                device_id=to, device_id_type=MESH)

        sends = [copy(r, a, 2 * px + py, me, (px, py, c)) for r, (px, py) in enumerate(peers) for a in range(4)]
        flip = lambda bit, v: 1 - v if bit else v
        rels = [(rel, flip(rel & 4, x), flip(rel & 2, y), flip(rel & 1, c)) for rel in range(1, N_DEV)]
        sends += [small_copy(rel, 4 * x + 2 * y + c, (px, py, pc)) for rel, px, py, pc in rels]
        for cp in sends:
            cp.start()
        for r, (px, py) in enumerate(peers):
            for a in range(4):
                copy(r, a, me, 2 * px + py, (px, py, c)).wait_recv()
        for rel, px, py, pc in rels:
            small_copy(rel, 4 * px + 2 * py + pc, (px, py, pc)).wait_recv()
        for cp in sends:
            cp.wait_send()
        for cp in local:
            cp.wait()

    hbm = pl.BlockSpec(memory_space=pl.ANY)
    args = [g[n] for n in names] + [g["small"]]
    return pl.pallas_call(
        body, name="scatter_grads",
        in_specs=[hbm] * 5, out_specs=[hbm] * 5,
        out_shape=[jax.ShapeDtypeStruct(a.shape, F32) for a in args[:4]]
        + [jax.ShapeDtypeStruct((N_DEV, SMALL_ROWS, D_MODEL), F32)],
        scratch_shapes=[pltpu.SemaphoreType.DMA((12,)), pltpu.SemaphoreType.DMA((12,)),
                        pltpu.SemaphoreType.DMA((N_DEV - 1,)), pltpu.SemaphoreType.DMA((N_DEV - 1,)),
                        pltpu.SemaphoreType.DMA((5,))],
    )(*args)


def _sum_slots(land, name):
    n, rows, _ = land.shape
    tm = rows if rows < 256 else 256

    def body(l_ref, o_ref):
        acc = l_ref[0]
        for k in range(1, n):
            acc = acc + l_ref[k]
        o_ref[...] = acc

    return pl.pallas_call(
        body, name=name, grid=(rows // tm,),
        in_specs=[pl.BlockSpec((n, tm, D_MODEL), lambda i: (0, i, 0))],
        out_specs=pl.BlockSpec((tm, D_MODEL), lambda i: (i, 0)),
        out_shape=jax.ShapeDtypeStruct((rows, D_MODEL), F32),
        compiler_params=_params(("arbitrary",)),
    )(land)


def _swap_with_sibling(parts):
    def body(a0, a1, a2, a3, b0, b1, b2, b3, send_sems, recv_sems):
        x, y, c = _place()
        copies = [pltpu.make_async_remote_copy(src_ref=s, dst_ref=d, send_sem=send_sems.at[a], recv_sem=recv_sems.at[a],
                                               device_id=(x, y, 1 - c), device_id_type=MESH)
                  for a, (s, d) in enumerate(zip((a0, a1, a2, a3), (b0, b1, b2, b3)))]
        for cp in copies:
            cp.start()
        for cp in copies:
            cp.wait()

    hbm = pl.BlockSpec(memory_space=pl.ANY)
    return pl.pallas_call(
        body, name="swap_with_sibling", in_specs=[hbm] * 4, out_specs=[hbm] * 4,
        out_shape=[jax.ShapeDtypeStruct(p.shape, F32) for p in parts],
        scratch_shapes=[pltpu.SemaphoreType.DMA((4,)), pltpu.SemaphoreType.DMA((4,))],
    )(*parts)


def _adamw_math(w, g, m, v):
    m = ADAM_B1 * m + (1.0 - ADAM_B1) * g
    v = ADAM_B2 * v + (1.0 - ADAM_B2) * (g * g)
    m_hat = m / (1.0 - ADAM_B1 ** ADAM_STEP)
    v_hat = v / (1.0 - ADAM_B2 ** ADAM_STEP)
    delta = -ADAM_LR * (m_hat / (jnp.sqrt(v_hat) + ADAM_EPS) + ADAM_WD * w)
    return delta, m, v


def _adamw(w, g_parts, m, v, name):
    rows, cols = w.shape
    tm = rows if rows < 256 else 256
    n = len(g_parts)

    def body(*refs):
        w_ref, m_ref, v_ref = refs[n:n + 3]
        g_ref, d_ref, nm_ref, nv_ref = refs[n + 3:]
        g = refs[0][...]
        for p in refs[1:n]:
            g = g + p[...]
        g_ref[...] = g
        d_ref[...], nm_ref[...], nv_ref[...] = _adamw_math(w_ref[...], g, m_ref[...], v_ref[...])

    blk = pl.BlockSpec((tm, cols), lambda i: (i, 0))
    return pl.pallas_call(
        body, name=name, grid=(rows // tm,), in_specs=[blk] * (n + 3), out_specs=[blk] * 4,
        out_shape=[jax.ShapeDtypeStruct((rows, cols), F32)] * 4,
        compiler_params=_params(("arbitrary",)),
    )(*g_parts, w, m, v)


def _lam_of(hgrn_lb):
    def body(lb_ref, o_ref):
        lb = lb_ref[...]
        e = jnp.exp(lb - jnp.max(lb, axis=0, keepdims=True))
        o_ref[...] = e[0:1, :] / jnp.sum(e, axis=0, keepdims=True)

    return pl.pallas_call(body, name="lam_of", out_shape=jax.ShapeDtypeStruct((1, D_MODEL), F32))(hgrn_lb)


def _small_grads(land_small, lam):
    def body(l_ref, lam_ref, o_ref):
        acc = l_ref[0]
        for k in range(1, N_DEV):
            acc = acc + l_ref[k]
        p = lam_ref[...]
        slope = p * (1.0 - p)
        row = _iota2((SMALL_ROWS, D_MODEL), 0)
        o_ref[...] = acc * jnp.where(row == 4, slope, jnp.where(row == 5, -slope, 1.0))

    return pl.pallas_call(body, name="small_grads",
                          out_shape=jax.ShapeDtypeStruct((SMALL_ROWS, D_MODEL), F32))(land_small, lam)


def kernel(x, meta_tokens, pre_norm, post_norm, hgrn_w_in, hgrn_lb, hgrn_out_norm, hgrn_w_out, sb_w_in, sb_w_out, loss_target, m_meta_tokens, m_pre_norm, m_post_norm, m_hgrn_w_in, m_hgrn_lb, m_hgrn_out_norm, m_hgrn_w_out, m_sb_w_in, m_sb_w_out, v_meta_tokens, v_pre_norm, v_post_norm, v_hgrn_w_in, v_hgrn_lb, v_hgrn_out_norm, v_hgrn_w_out, v_sb_w_in, v_sb_w_out):
    d4 = D_MODEL // N_CHIPS
    chip = 2 * lax.axis_index("x") + lax.axis_index("y")
    hw_in, sw_in, hw_out, sw_out, meta4 = _gather_weights(hgrn_w_in, sb_w_in, hgrn_w_out, sb_w_out, meta_tokens)
    meta = meta4.transpose(1, 0, 2).reshape(N_META, D_MODEL)
    lam = _lam_of(hgrn_lb)
    loss_blk, grad_x, grads = _local_step(
        x, loss_target, meta, pre_norm, post_norm, lam, hgrn_out_norm,
        hw_in, hw_out.reshape(D_MODEL, D_MODEL), sw_in, sw_out.reshape(D_MODEL, D_MODEL))
    loss = lax.psum(loss_blk[0, 0], ("x", "y", "c"))

    l_hi, l_si, l_ho, l_so, l_small = _scatter_grads(grads)
    parts = [_sum_slots(l_hi, "sum_hw_in"), _sum_slots(l_si, "sum_sw_in"),
             _sum_slots(l_ho, "sum_hw_out"), _sum_slots(l_so, "sum_sw_out")]
    sib = _swap_with_sibling(parts)
    small = _small_grads(l_small, lam)

    res = {}
    res["hgrn_w_in"] = _adamw(hgrn_w_in[0], [parts[0], sib[0]], m_hgrn_w_in[0], v_hgrn_w_in[0], "adamw_hw_in")
    res["sb_w_in"] = _adamw(sb_w_in[0], [parts[1], sib[1]], m_sb_w_in[0], v_sb_w_in[0], "adamw_sw_in")
    res["hgrn_w_out"] = _adamw(hgrn_w_out[0], [parts[2], sib[2]], m_hgrn_w_out[0], v_hgrn_w_out[0], "adamw_hw_out")
    res["sb_w_out"] = _adamw(sb_w_out[0], [parts[3], sib[3]], m_sb_w_out[0], v_sb_w_out[0], "adamw_sw_out")
    res["pre_norm"] = _adamw(pre_norm, [small[0:2]], m_pre_norm, v_pre_norm, "adamw_pre")
    res["post_norm"] = _adamw(post_norm, [small[2:4]], m_post_norm, v_post_norm, "adamw_post")
    res["hgrn_lb"] = _adamw(hgrn_lb, [small[4:6]], m_hgrn_lb, v_hgrn_lb, "adamw_lb")
    res["hgrn_out_norm"] = _adamw(hgrn_out_norm, [small[6:7, :D_HEAD]], m_hgrn_out_norm, v_hgrn_out_norm, "adamw_head")
    g_meta = lax.dynamic_slice_in_dim(small[7:7 + N_META], chip * d4, d4, axis=1)
    res["meta_tokens"] = _adamw(meta_tokens, [g_meta], m_meta_tokens, v_meta_tokens, "adamw_meta")
    for n in ("hgrn_w_in", "hgrn_w_out", "sb_w_in", "sb_w_out"):
        res[n] = tuple(a[None] for a in res[n])
    order = ("meta_tokens", "pre_norm", "post_norm", "hgrn_w_in", "hgrn_lb", "hgrn_out_norm", "hgrn_w_out",
             "sb_w_in", "sb_w_out")
    return (loss, grad_x, *[res[n][0] for n in order], *[res[n][1] for n in order],
            *[res[n][2] for n in order], *[res[n][3] for n in order])
```

```python
import functools

import jax
import numpy as np
import jax.numpy as jnp
from jax import lax
from jax.experimental import pallas as pl
from jax.experimental.pallas import tpu as pltpu

F32 = jnp.float32
_MXU = jnp.bfloat16

D_MODEL = 1024
N_HEADS = 8
D_HEAD = 128
BLOCK = 128
N_META = 16
TILE = 256
N_PAD = TILE - N_META
UNDERFLOW = -104.0
EPS = 1e-6
SB_SCALE = D_HEAD ** -0.5
ADAM_LR, ADAM_B1, ADAM_B2, ADAM_EPS, ADAM_WD, ADAM_STEP = 0.001, 0.9, 0.999, 1e-08, 0.01, 10
N_CHIPS = 4
N_DEV = 8
SMALL_ROWS = 24
VMEM_LIMIT = 56 * 1024 * 1024
MESH = pl.DeviceIdType.MESH

_NT = (((1,), (1,)), ((), ()))
_TN = (((0,), (0,)), ((), ()))


def _mm(a, b):
    return jnp.dot(a.astype(_MXU), b.astype(_MXU), preferred_element_type=F32)


def _mm_nt(a, b):
    return lax.dot_general(a.astype(_MXU), b.astype(_MXU), _NT, preferred_element_type=F32)


def _mm_tn(a, b):
    return lax.dot_general(a.astype(_MXU), b.astype(_MXU), _TN, preferred_element_type=F32)


def _split2(x):
    hi = x.astype(_MXU)
    return hi, (x - hi.astype(F32)).astype(_MXU)


def _mm_s(a16, state):
    hi, lo = _split2(state)
    return jnp.dot(a16, hi, preferred_element_type=F32) + jnp.dot(a16, lo, preferred_element_type=F32)


def _mm_nt_s(a16, state):
    hi, lo = _split2(state)
    return (lax.dot_general(a16, hi, _NT, preferred_element_type=F32)
            + lax.dot_general(a16, lo, _NT, preferred_element_type=F32))


def _split3(x):
    hi = x.astype(_MXU)
    r1 = x - hi.astype(F32)
    mid = r1.astype(_MXU)
    lo = (r1 - mid.astype(F32)).astype(_MXU)
    return hi, mid, lo


def _mm01_right(x, m01):
    hi, lo = _split2(x)
    return jnp.dot(hi, m01, preferred_element_type=F32) + jnp.dot(lo, m01, preferred_element_type=F32)


def _mm01_left(m01, x):
    hi, mid, lo = _split3(x)
    d = functools.partial(jnp.dot, preferred_element_type=F32)
    return d(m01, hi) + d(m01, mid) + d(m01, lo)


def _iota2(shape, dim):
    return lax.broadcasted_iota(jnp.int32, shape, dim)


def _row_tile(total, pref):
    t = pref
    while total % t:
        t -= BLOCK
    return t


def _params(sem, limit=VMEM_LIMIT):
    return pltpu.CompilerParams(dimension_semantics=sem, vmem_limit_bytes=limit)


def _sigmoid(x):
    return 1.0 / (1.0 + jnp.exp(-x))


def _grid_ends(ndim):
    first, last = True, True
    for d in range(ndim):
        first = first & (pl.program_id(d) == 0)
        last = last & (pl.program_id(d) == pl.num_programs(d) - 1)
    return first, last


def _place():
    return lax.axis_index("x"), lax.axis_index("y"), lax.axis_index("c")


def _exchange_scratch(n):
    return [pltpu.SemaphoreType.DMA((3 * n,)), pltpu.SemaphoreType.DMA((3 * n,)), pltpu.SemaphoreType.DMA((n,))]


def _chip_exchange(srcs, dsts, sems, slotted):
    send_sems, recv_sems, local_sems = sems
    x, y, c = _place()
    me = 2 * x + y
    peers = [(1 - x, y), (x, 1 - y), (1 - x, 1 - y)]
    n = len(dsts)

    def remote(r, a, sending):
        px, py = peers[r]
        p = 2 * px + py
        return pltpu.make_async_remote_copy(
            src_ref=srcs[a].at[p] if slotted else srcs[a], dst_ref=dsts[a].at[me if sending else p],
            send_sem=send_sems.at[r * n + a], recv_sem=recv_sems.at[r * n + a],
            device_id=(px, py, c), device_id_type=MESH)

    def local(a):
        return pltpu.make_async_copy(srcs[a].at[me] if slotted else srcs[a], dsts[a].at[me], local_sems.at[a])

    def start():
        for a in range(n):
            local(a).start()
        for r in range(3):
            for a in range(n):
                remote(r, a, True).start()

    def finish():
        for r in range(3):
            for a in range(n):
                remote(r, a, False).wait_recv()
        for r in range(3):
            for a in range(n):
                remote(r, a, True).wait_send()
        for a in range(n):
            local(a).wait()

    return start, finish


def _norm_proj(h, gain, w4, name):
    t = h.shape[0]
    tm = _row_tile(t, 256)

    def body(h_ref, g_ref, w_ref, y_ref, s0, s1, s2, s3):
        x = h_ref[...]
        r = lax.rsqrt(jnp.mean(x * x, axis=-1, keepdims=True) + EPS)
        y = (x * r * g_ref[...]).astype(_MXU)
        y_ref[...] = y
        for j, s in enumerate((s0, s1, s2, s3)):
            s[...] = jnp.dot(y, w_ref[j], preferred_element_type=F32)

    row = pl.BlockSpec((tm, D_MODEL), lambda i: (i, 0))
    return pl.pallas_call(
        body, name=name, grid=(t // tm,),
        in_specs=[row, pl.BlockSpec((1, D_MODEL), lambda i: (0, 0)),
                  pl.BlockSpec((4, D_MODEL, D_MODEL), lambda i: (0, 0, 0))],
        out_specs=[row] * 5,
        out_shape=[jax.ShapeDtypeStruct((t, D_MODEL), _MXU)] + [jax.ShapeDtypeStruct((t, D_MODEL), F32)] * 4,
        compiler_params=_params(("arbitrary",)),
    )(h, gain, w4)


LEVELS = (64, 32, 16, 8, 4, 2, 1)
HEAD_GROUP = 2


def _hgrn_tables():
    r = np.arange(BLOCK)
    mats = []
    for m in (4, 2):
        same = r[:, None] // m == r[None, :] // m
        second = ((r // m) % 2 == 1)[:, None]
        mats.append(same & np.where(second, r[None, :] <= r[:, None], r[None, :] > r[:, None]))
    mats.append(r[None, :] <= r[:, None])
    x = r[:, None] ^ r[None, :]
    lv = np.full((BLOCK, BLOCK), len(LEVELS), np.int32)
    for i, m in enumerate(LEVELS):
        lv[(x >= m) & (x < 2 * m)] = i
    return jnp.asarray(np.concatenate(mats, 0).astype(np.float32), dtype=_MXU), jnp.asarray(lv)


def _hgrn_exponents(g, sums):
    e = _mm01_left(sums, g)
    b = e[2 * BLOCK:, :]
    row = _iota2((BLOCK, D_HEAD), 0)
    out = []
    for m in LEVELS:
        is_q = (row & m) != 0
        if m >= 8:
            grp = b.reshape(BLOCK // (2 * m), 2 * m, D_HEAD)
            ref = jnp.broadcast_to(grp[:, m - 1:m, :], grp.shape).reshape(BLOCK, D_HEAD)
            out.append(jnp.where(is_q, b - ref, ref - b))
        elif m == 4:
            out.append(e[:BLOCK, :])
        elif m == 2:
            out.append(e[BLOCK:2 * BLOCK, :])
        else:
            out.append(jnp.where(is_q, g, 0.0))
    return b, out


def _hgrn_gates(fz, lam, chunk):
    pos = chunk * BLOCK + _iota2((BLOCK, D_HEAD), 0)
    live = pos >= N_PAD
    sg = _sigmoid(fz)
    f = lam + (1.0 - lam) * sg
    g = jnp.where(live, jnp.log(f), 0.0)
    k = jnp.where(live, (1.0 - lam) * (1.0 - sg), 0.0)
    return sg, f, g, k, live


def _level_operand(q, k, exponent, m):
    decay = jnp.exp(exponent)
    is_q = (_iota2((BLOCK, D_HEAD), 0) & m) != 0
    return is_q, decay, (jnp.where(is_q, q, k) * decay).astype(_MXU)


def _hgrn_fwd(qs, fs, vs, lam, bsz, nb, shards):
    t = qs.shape[0]
    sums, levels = _hgrn_tables()
    width = HEAD_GROUP * D_HEAD

    n_sh = len(shards)

    def body(q_ref, f_ref, v_ref, lam_ref, sums_ref, lv_ref, *rest):
        own, (o_ref, sst_ref), rest = rest[:n_sh], rest[n_sh:n_sh + 2], rest[n_sh + 2:]
        gathered, st_scr, sems = rest[:n_sh], rest[n_sh], rest[n_sh + 1:]
        n = pl.program_id(2)
        first, last = _grid_ends(3)
        start, finish = _chip_exchange(own, gathered, sems, slotted=False)
        pl.when(first)(start)

        @pl.when(n == 0)
        def _():
            st_scr[...] = jnp.zeros_like(st_scr)

        lv = lv_ref[...]
        r, c = _iota2((BLOCK, BLOCK), 0), _iota2((BLOCK, BLOCK), 1)
        for hh in range(HEAD_GROUP):
            ls = slice(hh * D_HEAD, (hh + 1) * D_HEAD)
            st = st_scr[hh]
            sst_ref[0, hh, 0] = st
            q, v = q_ref[:, ls], v_ref[:, ls]
            _, _, g, k, _ = _hgrn_gates(f_ref[:, ls], lam_ref[:, ls], n)
            b, exps = _hgrn_exponents(g, sums_ref[...])
            sym = jnp.zeros((BLOCK, BLOCK), F32)
            for li, m in enumerate(LEVELS):
                _, _, x16 = _level_operand(q, k, exps[li], m)
                sym = jnp.where(lv == li, lax.dot_general(x16, x16, _NT, preferred_element_type=F32), sym)
            a = jnp.where(c < r, sym, jnp.where(c == r, jnp.sum(q * k, axis=1, keepdims=True), 0.0))
            o_ref[:, ls] = _mm_nt_s((q * jnp.exp(b)).astype(_MXU), st) + _mm(a, v)
            b_end = b[BLOCK - 1:BLOCK, :]
            st_scr[hh] = st * jnp.exp(b_end) + _mm_tn(v, k * jnp.exp(b_end - b))
        pl.when(last)(finish)

    blk = pl.BlockSpec((BLOCK, width), lambda b, h, n: (b * nb + n, h))
    hbm = pl.BlockSpec(memory_space=pl.ANY)
    return pl.pallas_call(
        body, name="hgrn_fwd", grid=(bsz, N_HEADS // HEAD_GROUP, nb),
        in_specs=[blk, blk, blk, pl.BlockSpec((1, width), lambda b, h, n: (0, h)),
                  pl.BlockSpec(sums.shape, lambda b, h, n: (0, 0)), pl.BlockSpec(levels.shape, lambda b, h, n: (0, 0))]
        + [hbm] * n_sh,
        out_specs=[blk, pl.BlockSpec((1, HEAD_GROUP, 1, D_HEAD, D_HEAD), lambda b, h, n: (b, h, n, 0, 0))] + [hbm] * n_sh,
        out_shape=[jax.ShapeDtypeStruct((t, D_MODEL), F32),
                   jax.ShapeDtypeStruct((bsz, N_HEADS, nb, D_HEAD, D_HEAD), F32)]
        + [jax.ShapeDtypeStruct((N_CHIPS,) + a.shape, a.dtype) for a in shards],
        scratch_shapes=[pltpu.VMEM((HEAD_GROUP, D_HEAD, D_HEAD), F32)] + _exchange_scratch(n_sh),
        compiler_params=_params(("arbitrary", "arbitrary", "arbitrary")),
    )(qs, fs, vs, lam, sums, levels, *shards)


def _hgrn_bwd(qs, fs, vs, lam, sst, do, bsz, nb, outgoing):
    t = qs.shape[0]
    sums, levels = _hgrn_tables()
    width = HEAD_GROUP * D_HEAD

    n_out = len(outgoing)

    def body(q_ref, f_ref, v_ref, lam_ref, sst_ref, do_ref, sums_ref, lv_ref, *rest):
        send, (dq_ref, df_ref, dv_ref, dlam_ref), rest = rest[:n_out], rest[n_out:n_out + 4], rest[n_out + 4:]
        landed, dst_scr, gsum_scr, sems = rest[:n_out], rest[n_out], rest[n_out + 1], rest[n_out + 2:]
        n = pl.program_id(2)
        chunk = nb - 1 - n
        first, last = _grid_ends(3)
        start, finish = _chip_exchange(send, landed, sems, slotted=True)
        pl.when(first)(start)

        @pl.when(n == 0)
        def _():
            dst_scr[...] = jnp.zeros_like(dst_scr)
            gsum_scr[...] = jnp.zeros_like(gsum_scr)
            dlam_ref[...] = jnp.zeros_like(dlam_ref)

        lv = lv_ref[...]
        r, c = _iota2((BLOCK, BLOCK), 0), _iota2((BLOCK, BLOCK), 1)
        for hh in range(HEAD_GROUP):
            ls = slice(hh * D_HEAD, (hh + 1) * D_HEAD)
            lam = lam_ref[:, ls]
            q, v, do = q_ref[:, ls], v_ref[:, ls], do_ref[:, ls]
            sg, f, g, k, live = _hgrn_gates(f_ref[:, ls], lam, chunk)
            b, exps = _hgrn_exponents(g, sums_ref[...])
            do16, v16 = do.astype(_MXU), v.astype(_MXU)
            da = lax.dot_general(do16, v16, _NT, preferred_element_type=F32)
            da_t = lax.dot_general(v16, do16, _NT, preferred_element_type=F32)
            da_sym = jnp.where(c < r, da, da_t)
            sym = jnp.zeros((BLOCK, BLOCK), F32)
            dq = jnp.zeros((BLOCK, D_HEAD), F32)
            dk = jnp.zeros((BLOCK, D_HEAD), F32)
            db = jnp.zeros((BLOCK, D_HEAD), F32)
            for li, m in enumerate(LEVELS):
                is_q, decay, x16 = _level_operand(q, k, exps[li], m)
                here = lv == li
                sym = jnp.where(here, lax.dot_general(x16, x16, _NT, preferred_element_type=F32), sym)
                y = jnp.dot(jnp.where(here, da_sym, 0.0).astype(_MXU), x16, preferred_element_type=F32)
                dx = y * decay
                dxq = jnp.where(is_q, dx, 0.0)
                dq = dq + dxq
                dk = dk + (dx - dxq)
                p = x16.astype(F32) * y
                db = db + jnp.where(is_q, p, -p)
            on_diag = jnp.sum(q * k, axis=1, keepdims=True)
            a_t = jnp.where(c > r, sym, jnp.where(c == r, on_diag, 0.0))
            st, dst = sst_ref[0, hh, 0], dst_scr[hh]
            eb = jnp.exp(b)
            b_end = b[BLOCK - 1:BLOCK, :]
            dec = jnp.exp(b_end - b)
            qh16, kt16 = (q * eb).astype(_MXU), (k * dec).astype(_MXU)
            dq_st = _mm_s(do16, st)
            dk_st = _mm_s(v16, dst)
            d_diag = jnp.sum(do * v, axis=1, keepdims=True)
            dq_ref[:, ls] = dq + d_diag * k + eb * dq_st
            dk = dk + d_diag * q + dec * dk_st
            dv_ref[:, ls] = jnp.dot(a_t.astype(_MXU), do16, preferred_element_type=F32) + _mm_nt_s(kt16, dst)
            dst_scr[hh] = dst * jnp.exp(b_end) + lax.dot_general(do16, qh16, _TN, preferred_element_type=F32)
            db = db + (qh16.astype(F32) * dq_st - kt16.astype(F32) * dk_st)
            dg = _mm01_left((c >= r).astype(_MXU), db) + gsum_scr[:, ls]
            gsum_scr[:, ls] = gsum_scr[:, ls] + jnp.sum(db, axis=0, keepdims=True)
            slope = (1.0 - lam) * sg * (1.0 - sg)
            df_ref[:, ls] = jnp.where(live, dg * slope / f - dk * slope, 0.0)
            dl = jnp.where(live, (dg / f - dk) * (1.0 - sg), 0.0)
            dlam_ref[0, :, ls] = dlam_ref[0, :, ls] + jnp.sum(dl, axis=0, keepdims=True)
        pl.when(last)(finish)

    blk = pl.BlockSpec((BLOCK, width), lambda b, h, n: (b * nb + nb - 1 - n, h))
    hbm = pl.BlockSpec(memory_space=pl.ANY)
    return pl.pallas_call(
        body, name="hgrn_bwd", grid=(bsz, N_HEADS // HEAD_GROUP, nb),
        in_specs=[blk, blk, blk, pl.BlockSpec((1, width), lambda b, h, n: (0, h)),
                  pl.BlockSpec((1, HEAD_GROUP, 1, D_HEAD, D_HEAD), lambda b, h, n: (b, h, nb - 1 - n, 0, 0)),
                  blk, pl.BlockSpec(sums.shape, lambda b, h, n: (0, 0)), pl.BlockSpec(levels.shape, lambda b, h, n: (0, 0))]
        + [hbm] * n_out,
        out_specs=[blk, blk, blk, pl.BlockSpec((1, 1, width), lambda b, h, n: (b, 0, h))] + [hbm] * n_out,
        out_shape=[jax.ShapeDtypeStruct((t, D_MODEL), F32)] * 3 + [jax.ShapeDtypeStruct((bsz, 1, D_MODEL), F32)]
        + [jax.ShapeDtypeStruct(a.shape, a.dtype) for a in outgoing],
        scratch_shapes=[pltpu.VMEM((HEAD_GROUP, D_HEAD, D_HEAD), F32), pltpu.VMEM((1, width), F32)]
        + _exchange_scratch(n_out),
        compiler_params=_params(("arbitrary", "arbitrary", "arbitrary")),
    )(qs, fs, vs, lam, sst, do, sums, levels, *outgoing)


def _sb_valid(i, j):
    qpos = i * TILE + _iota2((TILE, TILE), 0)
    kpos = j * TILE + _iota2((TILE, TILE), 1)
    return (kpos < qpos) & (kpos >= N_PAD)


def _sb_logits(q16, k_blk, valid):
    z = lax.dot_general(q16, k_blk.astype(_MXU), _NT, preferred_element_type=F32) * SB_SCALE
    softplus = jnp.maximum(z, 0.0) + jnp.log(1.0 + jnp.exp(-jnp.abs(z)))
    return jnp.where(valid, -softplus, 0.0), z - softplus


def _sb_fwd(qs, ks, vs, bsz, nq):
    t = qs.shape[0]
    lp = nq * TILE
    width = HEAD_GROUP * D_HEAD
    groups = N_HEADS // HEAD_GROUP
    lanes = [slice(hh * D_HEAD, (hh + 1) * D_HEAD) for hh in range(HEAD_GROUP)]

    def body(q_ref, k_ref, v_ref, o_ref, c_ref, n_ref):
        b, h, i = pl.program_id(0), pl.program_id(1), pl.program_id(2)
        q16 = [q_ref[:, ls].astype(_MXU) for ls in lanes]
        r, c = _iota2((TILE, TILE), 0), _iota2((TILE, TILE), 1)
        after = (r > c).astype(_MXU)

        def more(carry):
            jj, _, _, top = carry
            return (jj <= i) & (top > UNDERFLOW)

        def step(carry):
            jj, accs, sums, _ = carry
            j = i - jj
            ks_ = pl.ds(pl.multiple_of(j * TILE, TILE), TILE)
            valid = _sb_valid(i, j)
            new_accs, new_sums = [], []
            for hh, ls in enumerate(lanes):
                keep, log_beta = _sb_logits(q16[hh], k_ref[ks_, ls], valid)
                later = sums[hh] + _mm01_right(keep, after)
                a = jnp.where(valid, jnp.exp(log_beta + later), 0.0)
                new_accs.append(accs[hh] + _mm(a, v_ref[ks_, ls]))
                new_sums.append(sums[hh] + jnp.sum(keep, axis=1, keepdims=True))
            top = functools.reduce(jnp.maximum, [jnp.max(x) for x in new_sums])
            return jj + 1, tuple(new_accs), tuple(new_sums), top

        init = (jnp.int32(0), tuple(jnp.zeros((TILE, D_HEAD), F32) for _ in lanes),
                tuple(jnp.zeros((TILE, 1), F32) for _ in lanes), jnp.float32(0.0))
        visited, accs, sums, _ = lax.while_loop(more, step, init)
        for hh, ls in enumerate(lanes):
            o_ref[:, ls] = accs[hh]
            c_ref[:, ls] = jnp.broadcast_to(sums[hh], (TILE, D_HEAD))
        n_ref[(b * groups + h) * nq + i] = visited.astype(F32)

    blk = pl.BlockSpec((TILE, width), lambda b, h, i: (b * nq + i, h))
    seq = pl.BlockSpec((lp, width), lambda b, h, i: (b, h))
    return pl.pallas_call(
        body, name="sb_fwd", grid=(bsz, groups, nq),
        in_specs=[blk, seq, seq], out_specs=[blk, blk, pl.BlockSpec(memory_space=pltpu.SMEM)],
        out_shape=[jax.ShapeDtypeStruct((t, D_MODEL), F32)] * 2 + [jax.ShapeDtypeStruct((bsz * groups * nq,), F32)],
        compiler_params=_params(("arbitrary", "arbitrary", "arbitrary")),
    )(qs, ks, vs)


def _sb_bwd(qs, ks, vs, ctot, visited, do, bsz, nq):
    t = qs.shape[0]
    lp = nq * TILE
    width = HEAD_GROUP * D_HEAD
    groups = N_HEADS // HEAD_GROUP
    lanes = [slice(hh * D_HEAD, (hh + 1) * D_HEAD) for hh in range(HEAD_GROUP)]

    def body(n_ref, q_ref, k_ref, v_ref, c_ref, do_ref, dq_ref, dk_ref, dv_ref):
        b, h, i = pl.program_id(0), pl.program_id(1), pl.program_id(2)

        @pl.when(i == 0)
        def _():
            dk_ref[...] = jnp.zeros_like(dk_ref)
            dv_ref[...] = jnp.zeros_like(dv_ref)

        q16 = [q_ref[:, ls].astype(_MXU) for ls in lanes]
        do16 = [do_ref[:, ls].astype(_MXU) for ls in lanes]
        totals = [c_ref[:, hh * D_HEAD:hh * D_HEAD + 1] for hh in range(HEAD_GROUP)]
        r, c = _iota2((TILE, TILE), 0), _iota2((TILE, TILE), 1)
        upto = (r <= c).astype(_MXU)
        before = (r < c).astype(_MXU)
        first = jnp.maximum(i + 1 - n_ref[(b * groups + h) * nq + i].astype(jnp.int32), 0)

        def step(j, carry):
            ks_ = pl.ds(pl.multiple_of(j * TILE, TILE), TILE)
            valid = _sb_valid(i, j)
            out = []
            for hh, ls in enumerate(lanes):
                dq, keep_pre, g_pre = carry[hh]
                k_blk, v_blk = k_ref[ks_, ls], v_ref[ks_, ls]
                keep, log_beta = _sb_logits(q16[hh], k_blk, valid)
                later = totals[hh] - keep_pre - _mm01_right(keep, upto)
                a = jnp.where(valid, jnp.exp(log_beta + later), 0.0)
                da = lax.dot_general(do16[hh], v_blk.astype(_MXU), _NT, preferred_element_type=F32)
                g = a * da
                g_before = g_pre + _mm01_right(g, before)
                beta = jnp.exp(log_beta)
                dz = jnp.where(valid, g * (1.0 - beta) - beta * g_before, 0.0) * SB_SCALE
                dz16 = dz.astype(_MXU)
                dq = dq + jnp.dot(dz16, k_blk.astype(_MXU), preferred_element_type=F32)
                dk_ref[ks_, ls] += lax.dot_general(dz16, q16[hh], _TN, preferred_element_type=F32)
                dv_ref[ks_, ls] += lax.dot_general(a.astype(_MXU), do16[hh], _TN, preferred_element_type=F32)
                out.append((dq, keep_pre + jnp.sum(keep, axis=1, keepdims=True),
                            g_pre + jnp.sum(g, axis=1, keepdims=True)))
            return tuple(out)

        zero_col = jnp.zeros((TILE, 1), F32)
        init = tuple((jnp.zeros((TILE, D_HEAD), F32), zero_col, zero_col) for _ in lanes)
        res = lax.fori_loop(first, i + 1, step, init)
        for hh, ls in enumerate(lanes):
            dq_ref[:, ls] = res[hh][0]

    blk = pl.BlockSpec((TILE, width), lambda b, h, i: (b * nq + i, h))
    seq = pl.BlockSpec((lp, width), lambda b, h, i: (b, h))
    return pl.pallas_call(
        body, name="sb_bwd", grid=(bsz, groups, nq),
        in_specs=[pl.BlockSpec(memory_space=pltpu.SMEM), blk, seq, seq, blk, blk], out_specs=[blk, seq, seq],
        out_shape=[jax.ShapeDtypeStruct((t, D_MODEL), F32)] * 3,
        compiler_params=_params(("arbitrary", "arbitrary", "arbitrary")),
    )(visited, qs, ks, vs, ctot, do)


def _head_norm(o, head_gain):
    outs, rs = [], []
    for h in range(N_HEADS):
        oh = o[:, h * D_HEAD:(h + 1) * D_HEAD]
        r = lax.rsqrt(jnp.mean(oh * oh, axis=-1, keepdims=True) + EPS)
        outs.append(oh * r)
        rs.append(r)
    return outs, rs


def _mix(o, gate, head_gain):
    if head_gain is None:
        on = o
    else:
        outs, _ = _head_norm(o, head_gain)
        on = jnp.concatenate([x * head_gain for x in outs], axis=1)
    return on, on * (gate * _sigmoid(gate))


def _out_fwd(o, gate, h_in, w_out, post_gain, head_gain, name):
    t = o.shape[0]
    tm = _row_tile(t, 256)
    has_head = head_gain is not None

    def body(*refs):
        if has_head:
            o_ref, g_ref, h_ref, w_ref, pg_ref, hg_ref, ho_ref, u_ref = refs
            hg = hg_ref[...]
        else:
            o_ref, g_ref, h_ref, w_ref, pg_ref, ho_ref, u_ref = refs
            hg = None
        _, mix = _mix(o_ref[...], g_ref[...], hg)
        u = jnp.dot(mix.astype(_MXU), w_ref[...], preferred_element_type=F32)
        u_ref[...] = u
        r = lax.rsqrt(jnp.mean(u * u, axis=-1, keepdims=True) + EPS)
        ho_ref[...] = h_ref[...] + u * r * pg_ref[...]

    row = pl.BlockSpec((tm, D_MODEL), lambda i: (i, 0))
    vec = pl.BlockSpec((1, D_MODEL), lambda i: (0, 0))
    in_specs = [row, row, row, pl.BlockSpec((D_MODEL, D_MODEL), lambda i: (0, 0)), vec]
    args = [o, gate, h_in, w_out, post_gain]
    if has_head:
        in_specs.append(pl.BlockSpec((1, D_HEAD), lambda i: (0, 0)))
        args.append(head_gain)
    return pl.pallas_call(
        body, name=name, grid=(t // tm,), in_specs=in_specs, out_specs=[row, row],
        out_shape=[jax.ShapeDtypeStruct((t, D_MODEL), F32)] * 2,
        compiler_params=_params(("arbitrary",)),
    )(*args)


def _out_bwd(dh, u, o, gate, w_out, post_gain, head_gain, name):
    t = o.shape[0]
    tm = _row_tile(t, 256)
    has_head = head_gain is not None

    def body(*refs):
        if has_head:
            dh_ref, u_ref, o_ref, g_ref, w_ref, pg_ref, hg_ref, do_ref, dg_ref, gw_ref, gw16_ref, gp_ref, gh_ref = refs
            hg = hg_ref[...]
        else:
            dh_ref, u_ref, o_ref, g_ref, w_ref, pg_ref, do_ref, dg_ref, gw_ref, gw16_ref, gp_ref = refs
            hg = None
        first = pl.program_id(0) == 0

        @pl.when(first)
        def _():
            gw_ref[...] = jnp.zeros_like(gw_ref)
            gp_ref[...] = jnp.zeros_like(gp_ref)
            if has_head:
                gh_ref[...] = jnp.zeros_like(gh_ref)

        dr, u, o, gate = dh_ref[...], u_ref[...], o_ref[...], g_ref[...]
        r = lax.rsqrt(jnp.mean(u * u, axis=-1, keepdims=True) + EPS)
        un = u * r
        gp_ref[...] += jnp.sum(dr * un, axis=0, keepdims=True)
        dun = dr * pg_ref[...]
        du = r * (dun - un * jnp.mean(dun * un, axis=-1, keepdims=True))
        on, mix = _mix(o, gate, hg)
        du16 = du.astype(_MXU)
        gw_ref[...] += lax.dot_general(mix.astype(_MXU), du16, _TN, preferred_element_type=F32)
        dmix = lax.dot_general(du16, w_ref[...], _NT, preferred_element_type=F32)
        sg = _sigmoid(gate)
        dg_ref[...] = dmix * on * (sg * (1.0 + gate * (1.0 - sg)))
        don = dmix * (gate * sg)
        if has_head:
            outs, rs = _head_norm(o, hg)
            gh = jnp.zeros((1, D_HEAD), F32)
            cols = []
            for h in range(N_HEADS):
                dn = don[:, h * D_HEAD:(h + 1) * D_HEAD]
                gh = gh + jnp.sum(dn * outs[h], axis=0, keepdims=True)
                dnn = dn * hg
                cols.append(rs[h] * (dnn - outs[h] * jnp.mean(dnn * outs[h], axis=-1, keepdims=True)))
            gh_ref[...] += gh
            do_ref[...] = jnp.concatenate(cols, axis=1)
        else:
            do_ref[...] = don

        @pl.when(pl.program_id(0) == pl.num_programs(0) - 1)
        def _():
            gw16_ref[...] = gw_ref[...].astype(_MXU)

    row = pl.BlockSpec((tm, D_MODEL), lambda i: (i, 0))
    vec = pl.BlockSpec((1, D_MODEL), lambda i: (0, 0))
    mat = pl.BlockSpec((D_MODEL, D_MODEL), lambda i: (0, 0))
    in_specs = [row, row, row, row, mat, vec]
    args = [dh, u, o, gate, w_out, post_gain]
    out_specs = [row, row, mat, mat, vec]
    out_shape = [jax.ShapeDtypeStruct((t, D_MODEL), F32)] * 2 + [jax.ShapeDtypeStruct((D_MODEL, D_MODEL), F32),
                                                                  jax.ShapeDtypeStruct((D_MODEL, D_MODEL), _MXU),
                                                                  jax.ShapeDtypeStruct((1, D_MODEL), F32)]
    if has_head:
        in_specs.append(pl.BlockSpec((1, D_HEAD), lambda i: (0, 0)))
        args.append(head_gain)
        out_specs.append(pl.BlockSpec((1, D_HEAD), lambda i: (0, 0)))
        out_shape.append(jax.ShapeDtypeStruct((1, D_HEAD), F32))
    return pl.pallas_call(
        body, name=name, grid=(t // tm,), in_specs=in_specs, out_specs=out_specs, out_shape=out_shape,
        compiler_params=_params(("arbitrary",)),
    )(*args)


def _proj_bwd(ds, w4, h_in, gain, dh_out, name, outgoing=()):
    t = h_in.shape[0]
    tm = _row_tile(t, 256)
    n_out = len(outgoing)

    def body(d0, d1, d2, d3, w_ref, h_ref, g_ref, dho_ref, *rest):
        send, (dhi_ref, gg_ref), rest = rest[:n_out], rest[n_out:n_out + 2], rest[n_out + 2:]
        landed, sems = rest[:n_out], rest[n_out:]
        if n_out:
            first, last = _grid_ends(1)
            start, finish = _chip_exchange(send, landed, sems, slotted=True)
            pl.when(first)(start)

        @pl.when(pl.program_id(0) == 0)
        def _():
            gg_ref[...] = jnp.zeros_like(gg_ref)

        dy = jnp.zeros((tm, D_MODEL), F32)
        for j, d in enumerate((d0, d1, d2, d3)):
            dy = dy + lax.dot_general(d[...].astype(_MXU), w_ref[j], _NT, preferred_element_type=F32)
        x = h_ref[...]
        r = lax.rsqrt(jnp.mean(x * x, axis=-1, keepdims=True) + EPS)
        xn = x * r
        gg_ref[...] += jnp.sum(dy * xn, axis=0, keepdims=True)
        dxn = dy * g_ref[...]
        dhi_ref[...] = dho_ref[...] + r * (dxn - xn * jnp.mean(dxn * xn, axis=-1, keepdims=True))
        if n_out:
            pl.when(last)(finish)

    row = pl.BlockSpec((tm, D_MODEL), lambda i: (i, 0))
    vec = pl.BlockSpec((1, D_MODEL), lambda i: (0, 0))
    hbm = pl.BlockSpec(memory_space=pl.ANY)
    return pl.pallas_call(
        body, name=name, grid=(t // tm,),
        in_specs=[row] * 4 + [pl.BlockSpec((4, D_MODEL, D_MODEL), lambda i: (0, 0, 0)), row, vec, row] + [hbm] * n_out,
        out_specs=[row, vec] + [hbm] * n_out,
        out_shape=[jax.ShapeDtypeStruct((t, D_MODEL), F32), jax.ShapeDtypeStruct((1, D_MODEL), F32)]
        + [jax.ShapeDtypeStruct(a.shape, a.dtype) for a in outgoing],
        scratch_shapes=_exchange_scratch(n_out) if n_out else [],
        compiler_params=_params(("arbitrary",)),
    )(*ds, w4, h_in, gain, dh_out, *outgoing)


def _weight_grad(y, d, name):
    t = y.shape[0]
    tk = _row_tile(t, 512)

    def body(y_ref, d_ref, g_ref, g16_ref):
        @pl.when(pl.program_id(0) == 0)
        def _():
            g_ref[...] = jnp.zeros_like(g_ref)

        g_ref[...] += lax.dot_general(y_ref[...], d_ref[...].astype(_MXU), _TN, preferred_element_type=F32)

        @pl.when(pl.program_id(0) == pl.num_programs(0) - 1)
        def _():
            g16_ref[...] = g_ref[...].astype(_MXU)

    row = pl.BlockSpec((tk, D_MODEL), lambda i: (i, 0))
    mat = pl.BlockSpec((D_MODEL, D_MODEL), lambda i: (0, 0))
    return pl.pallas_call(
        body, name=name, grid=(t // tk,), in_specs=[row, row], out_specs=[mat, mat],
        out_shape=[jax.ShapeDtypeStruct((D_MODEL, D_MODEL), F32), jax.ShapeDtypeStruct((D_MODEL, D_MODEL), _MXU)],
        compiler_params=_params(("arbitrary",)),
    )(y, d)


def _loss_head(h, target, bsz, nq):
    t = h.shape[0]

    def body(h_ref, t_ref, dh_ref, l_ref):
        i = pl.program_id(1)

        @pl.when((pl.program_id(0) == 0) & (i == 0))
        def _():
            l_ref[...] = jnp.zeros_like(l_ref)

        @pl.when(i == 0)
        def _():
            dh_ref[...] = jnp.zeros_like(dh_ref)

        @pl.when(i > 0)
        def _():
            e = h_ref[...] - t_ref[...]
            dh_ref[...] = e * (1.0 / D_MODEL)
            l_ref[...] += jnp.sum(e * e) * (0.5 / D_MODEL)

    return pl.pallas_call(
        body, name="loss_head", grid=(bsz, nq),
        in_specs=[pl.BlockSpec((TILE, D_MODEL), lambda b, i: (b * nq + i, 0)),
                  pl.BlockSpec((TILE, D_MODEL), lambda b, i: (b * (nq - 1) + jnp.maximum(i - 1, 0), 0))],
        out_specs=[pl.BlockSpec((TILE, D_MODEL), lambda b, i: (b * nq + i, 0)),
                   pl.BlockSpec((8, 128), lambda b, i: (0, 0))],
        out_shape=[jax.ShapeDtypeStruct((t, D_MODEL), F32), jax.ShapeDtypeStruct((8, 128), F32)],
        compiler_params=_params(("arbitrary", "arbitrary")),
    )(h, target)


def _local_step(x, target, meta, pre_norm, post_norm, lam, head_gain, hw_in, hw_out, sb_shards):
    bsz, seq, _ = x.shape
    nq = seq // TILE + 1
    nb = nq * (TILE // BLOCK)
    lp = nq * TILE
    t = bsz * lp
    d4 = D_MODEL // N_CHIPS
    front = jnp.concatenate([jnp.zeros((N_PAD, D_MODEL), F32), meta], axis=0)
    h0 = jnp.concatenate([jnp.broadcast_to(front[None], (bsz, TILE, D_MODEL)), x], axis=1).reshape(t, D_MODEL)
    pre0, pre1, post0, post1 = pre_norm[0:1], pre_norm[1:2], post_norm[0:1], post_norm[1:2]

    y0, q0, f0, v0, g0 = _norm_proj(h0, pre0, hw_in, "norm_proj_hgrn")
    o0, sst, sw_in, sw_out = _hgrn_fwd(q0, f0, v0, lam, bsz, nb, sb_shards)
    sw_out = sw_out.reshape(D_MODEL, D_MODEL)
    h1, u0 = _out_fwd(o0, g0, h0, hw_out, post0, head_gain, "out_fwd_hgrn")
    y1, q1, k1, v1, g1 = _norm_proj(h1, pre1, sw_in, "norm_proj_sb")
    o1, ctot, visited = _sb_fwd(q1, k1, v1, bsz, nq)
    h2, u1 = _out_fwd(o1, g1, h1, sw_out, post1, None, "out_fwd_sb")

    dh2, loss_blk = _loss_head(h2, target.reshape(bsz * seq, D_MODEL), bsz, nq)

    do1, dg1, g_sw_out, g_sw_out16, g_post1 = _out_bwd(dh2, u1, o1, g1, sw_out, post1, None, "out_bwd_sb")
    dq1, dk1, dv1 = _sb_bwd(q1, k1, v1, ctot, visited, do1, bsz, nq)
    ds1 = (dq1, dk1, dv1, dg1)
    dh1, g_pre1 = _proj_bwd(ds1, sw_in, h1, pre1, dh2, "proj_bwd_sb")
    g_sw_in = [_weight_grad(y1, d, "wgrad_sb_%d" % j) for j, d in enumerate(ds1)]

    do0, dg0, g_hw_out, g_hw_out16, g_post0, g_head = _out_bwd(dh1, u0, o0, g0, hw_out, post0, head_gain, "out_bwd_hgrn")
    sb_out = (jnp.stack([g16 for _, g16 in g_sw_in]), g_sw_out16.reshape(N_CHIPS, d4, D_MODEL))
    dq0, df0, dv0, dlam, land_sw_in, land_sw_out = _hgrn_bwd(q0, f0, v0, lam, sst, do0, bsz, nb, sb_out)
    ds0 = (dq0, df0, dv0, dg0)
    g_hw_in = [_weight_grad(y0, d, "wgrad_hgrn_%d" % j) for j, d in enumerate(ds0)]
    hg_out = (jnp.stack([g16 for _, g16 in g_hw_in]), g_hw_out16.reshape(N_CHIPS, d4, D_MODEL))
    dh0, g_pre0, land_hw_in, land_hw_out = _proj_bwd(ds0, hw_in, h0, pre0, dh1, "proj_bwd_hgrn", hg_out)

    dh0 = dh0.reshape(bsz, lp, D_MODEL)
    grad_x = dh0[:, TILE:, :]
    g_meta = jnp.sum(dh0[:, N_PAD:TILE, :], axis=0)
    g_lam = jnp.sum(dlam, axis=0)
    small = jnp.concatenate([g_pre0, g_pre1, g_post0, g_post1, g_lam, g_lam,
                             jnp.pad(g_head, ((0, 0), (0, D_MODEL - D_HEAD))), g_meta,
                             jnp.zeros((SMALL_ROWS - 23, D_MODEL), F32)], axis=0)
    rows4 = lambda g: [g[j * d4:(j + 1) * d4] for j in range(N_CHIPS)]
    large = dict(hw_in=(land_hw_in, [g for g, _ in g_hw_in]), sw_in=(land_sw_in, [g for g, _ in g_sw_in]),
                 hw_out=(land_hw_out, rows4(g_hw_out)), sw_out=(land_sw_out, rows4(g_sw_out)))
    return loss_blk, grad_x, small, large


def _prep_weights(hw_in, sw_in, hw_out, sw_out, meta):
    def body(hi_ref, si_ref, ho_ref, so_ref, m_ref, ghi, gho, gm, si16, so16, send_sems, recv_sems):
        x, y, c = _place()
        me = 2 * x + y
        ghi[me] = hi_ref[0].astype(_MXU)
        gho[me] = ho_ref[0].astype(_MXU)
        gm[me] = m_ref[...]
        si16[...] = si_ref[0].astype(_MXU)
        so16[...] = so_ref[0].astype(_MXU)
        outs = (ghi, gho, gm)
        peers = [(1 - x, y), (x, 1 - y), (1 - x, 1 - y)]

        def copy(r, a, slot, to):
            return pltpu.make_async_remote_copy(
                src_ref=outs[a].at[slot], dst_ref=outs[a].at[slot], send_sem=send_sems.at[r * 3 + a],
                recv_sem=recv_sems.at[r * 3 + a], device_id=to, device_id_type=MESH)

        sends = [copy(r, a, me, (px, py, c)) for r, (px, py) in enumerate(peers) for a in range(3)]
        for cp in sends:
            cp.start()
        for r, (px, py) in enumerate(peers):
            for a in range(3):
                copy(r, a, 2 * px + py, (px, py, c)).wait_recv()
        for cp in sends:
            cp.wait_send()

    d4 = D_MODEL // N_CHIPS
    vm = pl.BlockSpec(memory_space=pltpu.VMEM)
    return pl.pallas_call(
        body, name="prep_weights",
        in_specs=[vm] * 5, out_specs=[vm] * 5,
        out_shape=[jax.ShapeDtypeStruct((N_CHIPS, D_MODEL, D_MODEL), _MXU), jax.ShapeDtypeStruct((N_CHIPS, d4, D_MODEL), _MXU),
                   jax.ShapeDtypeStruct((N_CHIPS, N_META, d4), F32),
                   jax.ShapeDtypeStruct((D_MODEL, D_MODEL), _MXU), jax.ShapeDtypeStruct((d4, D_MODEL), _MXU)],
        scratch_shapes=[pltpu.SemaphoreType.DMA((9,)), pltpu.SemaphoreType.DMA((9,))],
        compiler_params=pltpu.CompilerParams(vmem_limit_bytes=VMEM_LIMIT),
    )(hw_in, sw_in, hw_out, sw_out, meta)


def _scatter_small(small):
    def body(sm, lsm, send_sems, recv_sems, local_sem):
        x, y, c = _place()
        mine = 4 * x + 2 * y + c
        local = pltpu.make_async_copy(sm, lsm.at[mine], local_sem)
        local.start()

        def copy(rel, src_dev, to):
            return pltpu.make_async_remote_copy(
                src_ref=sm, dst_ref=lsm.at[src_dev], send_sem=send_sems.at[rel - 1], recv_sem=recv_sems.at[rel - 1],
                device_id=to, device_id_type=MESH)

        flip = lambda bit, v: 1 - v if bit else v
        rels = [(rel, flip(rel & 4, x), flip(rel & 2, y), flip(rel & 1, c)) for rel in range(1, N_DEV)]
        sends = [copy(rel, mine, (px, py, pc)) for rel, px, py, pc in rels]
        for cp in sends:
            cp.start()
        for rel, px, py, pc in rels:
            copy(rel, 4 * px + 2 * py + pc, (px, py, pc)).wait_recv()
        for cp in sends:
            cp.wait_send()
        local.wait()

    hbm = pl.BlockSpec(memory_space=pl.ANY)
    return pl.pallas_call(
        body, name="scatter_small", in_specs=[hbm], out_specs=hbm,
        out_shape=jax.ShapeDtypeStruct((N_DEV, SMALL_ROWS, D_MODEL), F32),
        scratch_shapes=[pltpu.SemaphoreType.DMA((N_DEV - 1,)), pltpu.SemaphoreType.DMA((N_DEV - 1,)),
                        pltpu.SemaphoreType.DMA(())],
    )(small)


def _sum_slots(landed, own, me, name):
    n, rows, _ = landed.shape
    tm = rows if rows < 256 else 256

    def body(me_ref, l_ref, o0, o1, o2, o3, out_ref):
        acc = None
        for k, o in enumerate((o0, o1, o2, o3)):
            term = jnp.where(me_ref[0] == k, o[...], l_ref[k].astype(F32))
            acc = term if acc is None else acc + term
        out_ref[...] = acc

    blk = pl.BlockSpec((tm, D_MODEL), lambda i: (i, 0))
    return pl.pallas_call(
        body, name=name, grid=(rows // tm,),
        in_specs=[pl.BlockSpec(memory_space=pltpu.SMEM), pl.BlockSpec((n, tm, D_MODEL), lambda i: (0, i, 0))] + [blk] * 4,
        out_specs=blk, out_shape=jax.ShapeDtypeStruct((rows, D_MODEL), F32),
        compiler_params=_params(("arbitrary",)),
    )(me, landed, *own)


def _swap_with_sibling(parts):
    def body(a0, a1, a2, a3, b0, b1, b2, b3, send_sems, recv_sems):
        x, y, c = _place()
        copies = [pltpu.make_async_remote_copy(src_ref=s, dst_ref=d, send_sem=send_sems.at[a], recv_sem=recv_sems.at[a],
                                               device_id=(x, y, 1 - c), device_id_type=MESH)
                  for a, (s, d) in enumerate(zip((a0, a1, a2, a3), (b0, b1, b2, b3)))]
        for cp in copies:
            cp.start()
        for cp in copies:
            cp.wait()

    hbm = pl.BlockSpec(memory_space=pl.ANY)
    return pl.pallas_call(
        body, name="swap_with_sibling", in_specs=[hbm] * 4, out_specs=[hbm] * 4,
        out_shape=[jax.ShapeDtypeStruct(p.shape, F32) for p in parts],
        scratch_shapes=[pltpu.SemaphoreType.DMA((4,)), pltpu.SemaphoreType.DMA((4,))],
    )(*parts)


def _adamw_math(w, g, m, v):
    m = ADAM_B1 * m + (1.0 - ADAM_B1) * g
    v = ADAM_B2 * v + (1.0 - ADAM_B2) * (g * g)
    m_hat = m / (1.0 - ADAM_B1 ** ADAM_STEP)
    v_hat = v / (1.0 - ADAM_B2 ** ADAM_STEP)
    delta = -ADAM_LR * (m_hat / (jnp.sqrt(v_hat) + ADAM_EPS) + ADAM_WD * w)
    return delta, m, v


def _adamw(w, g_parts, m, v, name):
    rows, cols = w.shape
    tm = rows if rows < 256 else 256
    n = len(g_parts)

    def body(*refs):
        w_ref, m_ref, v_ref = refs[n:n + 3]
        g_ref, d_ref, nm_ref, nv_ref = refs[n + 3:]
        g = refs[0][...]
        for p in refs[1:n]:
            g = g + p[...]
        g_ref[...] = g
        d_ref[...], nm_ref[...], nv_ref[...] = _adamw_math(w_ref[...], g, m_ref[...], v_ref[...])

    blk = pl.BlockSpec((tm, cols), lambda i: (i, 0))
    return pl.pallas_call(
        body, name=name, grid=(rows // tm,), in_specs=[blk] * (n + 3), out_specs=[blk] * 4,
        out_shape=[jax.ShapeDtypeStruct((rows, cols), F32)] * 4,
        compiler_params=_params(("arbitrary",)),
    )(*g_parts, w, m, v)


def _lam_of(hgrn_lb):
    def body(lb_ref, o_ref):
        lb = lb_ref[...]
        e = jnp.exp(lb - jnp.max(lb, axis=0, keepdims=True))
        o_ref[...] = e[0:1, :] / jnp.sum(e, axis=0, keepdims=True)

    return pl.pallas_call(body, name="lam_of", out_shape=jax.ShapeDtypeStruct((1, D_MODEL), F32))(hgrn_lb)


def _small_grads(land_small, lam):
    def body(l_ref, lam_ref, o_ref):
        acc = l_ref[0]
        for k in range(1, N_DEV):
            acc = acc + l_ref[k]
        p = lam_ref[...]
        slope = p * (1.0 - p)
        row = _iota2((SMALL_ROWS, D_MODEL), 0)
        o_ref[...] = acc * jnp.where(row == 4, slope, jnp.where(row == 5, -slope, 1.0))

    return pl.pallas_call(body, name="small_grads",
                          out_shape=jax.ShapeDtypeStruct((SMALL_ROWS, D_MODEL), F32))(land_small, lam)


def kernel(x, meta_tokens, pre_norm, post_norm, hgrn_w_in, hgrn_lb, hgrn_out_norm, hgrn_w_out, sb_w_in, sb_w_out, loss_target, m_meta_tokens, m_pre_norm, m_post_norm, m_hgrn_w_in, m_hgrn_lb, m_hgrn_out_norm, m_hgrn_w_out, m_sb_w_in, m_sb_w_out, v_meta_tokens, v_pre_norm, v_post_norm, v_hgrn_w_in, v_hgrn_lb, v_hgrn_out_norm, v_hgrn_w_out, v_sb_w_in, v_sb_w_out):
    d4 = D_MODEL // N_CHIPS
    chip = 2 * lax.axis_index("x") + lax.axis_index("y")
    hw_in, hw_out, meta4, sw_in16, sw_out16 = _prep_weights(hgrn_w_in, sb_w_in, hgrn_w_out, sb_w_out, meta_tokens)
    meta = meta4.transpose(1, 0, 2).reshape(N_META, D_MODEL)
    lam = _lam_of(hgrn_lb)
    loss_blk, grad_x, small, large = _local_step(
        x, loss_target, meta, pre_norm, post_norm, lam, hgrn_out_norm,
        hw_in, hw_out.reshape(D_MODEL, D_MODEL), (sw_in16, sw_out16))
    loss = lax.psum(loss_blk[0, 0], ("x", "y", "c"))

    me = jnp.reshape(chip, (1,)).astype(jnp.int32)
    parts = [_sum_slots(*large[n], me, "sum_" + n) for n in ("hw_in", "sw_in", "hw_out", "sw_out")]
    sib = _swap_with_sibling(parts)
    small = _small_grads(_scatter_small(small), lam)

    res = {}
    res["hgrn_w_in"] = _adamw(hgrn_w_in[0], [parts[0], sib[0]], m_hgrn_w_in[0], v_hgrn_w_in[0], "adamw_hw_in")
    res["sb_w_in"] = _adamw(sb_w_in[0], [parts[1], sib[1]], m_sb_w_in[0], v_sb_w_in[0], "adamw_sw_in")
    res["hgrn_w_out"] = _adamw(hgrn_w_out[0], [parts[2], sib[2]], m_hgrn_w_out[0], v_hgrn_w_out[0], "adamw_hw_out")
    res["sb_w_out"] = _adamw(sb_w_out[0], [parts[3], sib[3]], m_sb_w_out[0], v_sb_w_out[0], "adamw_sw_out")
    res["pre_norm"] = _adamw(pre_norm, [small[0:2]], m_pre_norm, v_pre_norm, "adamw_pre")
    res["post_norm"] = _adamw(post_norm, [small[2:4]], m_post_norm, v_post_norm, "adamw_post")
    res["hgrn_lb"] = _adamw(hgrn_lb, [small[4:6]], m_hgrn_lb, v_hgrn_lb, "adamw_lb")
    res["hgrn_out_norm"] = _adamw(hgrn_out_norm, [small[6:7, :D_HEAD]], m_hgrn_out_norm, v_hgrn_out_norm, "adamw_head")
    g_meta = lax.dynamic_slice_in_dim(small[7:7 + N_META], chip * d4, d4, axis=1)
    res["meta_tokens"] = _adamw(meta_tokens, [g_meta], m_meta_tokens, v_meta_tokens, "adamw_meta")
    for n in ("hgrn_w_in", "hgrn_w_out", "sb_w_in", "sb_w_out"):
        res[n] = tuple(a[None] for a in res[n])
    order = ("meta_tokens", "pre_norm", "post_norm", "hgrn_w_in", "hgrn_lb", "hgrn_out_norm", "hgrn_w_out",
             "sb_w_in", "sb_w_out")
    return (loss, grad_x, *[res[n][0] for n in order], *[res[n][1] for n in order],
            *[res[n][2] for n in order], *[res[n][3] for n in order])
```

```python
import functools

import jax
import numpy as np
import jax.numpy as jnp
from jax import lax
from jax.experimental import pallas as pl
from jax.experimental.pallas import tpu as pltpu

F32 = jnp.float32
_MXU = jnp.bfloat16

D_MODEL = 1024
N_HEADS = 8
D_HEAD = 128
BLOCK = 128
N_META = 16
TILE = 256
N_PAD = TILE - N_META
UNDERFLOW = -104.0
EPS = 1e-6
SB_SCALE = D_HEAD ** -0.5
ADAM_LR, ADAM_B1, ADAM_B2, ADAM_EPS, ADAM_WD, ADAM_STEP = 0.001, 0.9, 0.999, 1e-08, 0.01, 10
N_CHIPS = 4
N_DEV = 8
SMALL_ROWS = 24
VMEM_LIMIT = 56 * 1024 * 1024
MESH = pl.DeviceIdType.MESH

_NT = (((1,), (1,)), ((), ()))
_TN = (((0,), (0,)), ((), ()))


def _mm(a, b):
    return jnp.dot(a.astype(_MXU), b.astype(_MXU), preferred_element_type=F32)


def _mm_nt(a, b):
    return lax.dot_general(a.astype(_MXU), b.astype(_MXU), _NT, preferred_element_type=F32)


def _mm_tn(a, b):
    return lax.dot_general(a.astype(_MXU), b.astype(_MXU), _TN, preferred_element_type=F32)


def _split2(x):
    hi = x.astype(_MXU)
    return hi, (x - hi.astype(F32)).astype(_MXU)


def _mm_s(a16, state):
    hi, lo = _split2(state)
    return jnp.dot(a16, hi, preferred_element_type=F32) + jnp.dot(a16, lo, preferred_element_type=F32)


def _mm_nt_s(a16, state):
    hi, lo = _split2(state)
    return (lax.dot_general(a16, hi, _NT, preferred_element_type=F32)
            + lax.dot_general(a16, lo, _NT, preferred_element_type=F32))


def _mm01_right(x, m01):
    hi, lo = _split2(x)
    return jnp.dot(hi, m01, preferred_element_type=F32) + jnp.dot(lo, m01, preferred_element_type=F32)


def _mm01_left(m01, x):
    hi, lo = _split2(x)
    return jnp.dot(m01, hi, preferred_element_type=F32) + jnp.dot(m01, lo, preferred_element_type=F32)


def _iota2(shape, dim):
    return lax.broadcasted_iota(jnp.int32, shape, dim)


def _row_tile(total, pref):
    t = pref
    while total % t:
        t -= BLOCK
    return t


def _params(sem, limit=VMEM_LIMIT):
    return pltpu.CompilerParams(dimension_semantics=sem, vmem_limit_bytes=limit)


def _sigmoid(x):
    return 1.0 / (1.0 + jnp.exp(-x))


def _grid_ends(ndim):
    first, last = True, True
    for d in range(ndim):
        first = first & (pl.program_id(d) == 0)
        last = last & (pl.program_id(d) == pl.num_programs(d) - 1)
    return first, last


def _place():
    return lax.axis_index("x"), lax.axis_index("y"), lax.axis_index("c")


def _exchange_scratch(n):
    return [pltpu.SemaphoreType.DMA((3 * n,)), pltpu.SemaphoreType.DMA((3 * n,)), pltpu.SemaphoreType.DMA((n,))]


def _chip_exchange(srcs, dsts, sems, slotted):
    send_sems, recv_sems, local_sems = sems
    x, y, c = _place()
    me = 2 * x + y
    peers = [(1 - x, y), (x, 1 - y), (1 - x, 1 - y)]
    n = len(dsts)

    def remote(r, a, sending):
        px, py = peers[r]
        p = 2 * px + py
        return pltpu.make_async_remote_copy(
            src_ref=srcs[a].at[p] if slotted else srcs[a], dst_ref=dsts[a].at[me if sending else p],
            send_sem=send_sems.at[r * n + a], recv_sem=recv_sems.at[r * n + a],
            device_id=(px, py, c), device_id_type=MESH)

    def local(a):
        return pltpu.make_async_copy(srcs[a].at[me] if slotted else srcs[a], dsts[a].at[me], local_sems.at[a])

    def start():
        for a in range(n):
            local(a).start()
        for r in range(3):
            for a in range(n):
                remote(r, a, True).start()

    def finish():
        for r in range(3):
            for a in range(n):
                remote(r, a, False).wait_recv()
        for r in range(3):
            for a in range(n):
                remote(r, a, True).wait_send()
        for a in range(n):
            local(a).wait()

    return start, finish


def _norm_proj(h, gain, w4, name, narrow):
    t = h.shape[0]
    tm = _row_tile(t, 256)

    def body(h_ref, g_ref, w_ref, y_ref, s0, s1, s2, s3):
        x = h_ref[...]
        r = lax.rsqrt(jnp.mean(x * x, axis=-1, keepdims=True) + EPS)
        y = (x * r * g_ref[...]).astype(_MXU)
        y_ref[...] = y
        for j, s in enumerate((s0, s1, s2, s3)):
            s[...] = jnp.dot(y, w_ref[j], preferred_element_type=F32).astype(s.dtype)

    row = pl.BlockSpec((tm, D_MODEL), lambda i: (i, 0))
    return pl.pallas_call(
        body, name=name, grid=(t // tm,),
        in_specs=[row, pl.BlockSpec((1, D_MODEL), lambda i: (0, 0)),
                  pl.BlockSpec((4, D_MODEL, D_MODEL), lambda i: (0, 0, 0))],
        out_specs=[row] * 5,
        out_shape=[jax.ShapeDtypeStruct((t, D_MODEL), _MXU)]
        + [jax.ShapeDtypeStruct((t, D_MODEL), _MXU if n else F32) for n in narrow],
        compiler_params=_params(("arbitrary",)),
    )(h, gain, w4)


LEVELS = (64, 32, 16, 8, 4, 2, 1)
HEAD_GROUP = 2


def _hgrn_tables():
    r = np.arange(BLOCK)
    mats = []
    for m in (4, 2):
        same = r[:, None] // m == r[None, :] // m
        second = ((r // m) % 2 == 1)[:, None]
        mats.append(same & np.where(second, r[None, :] <= r[:, None], r[None, :] > r[:, None]))
    mats.append(r[None, :] <= r[:, None])
    x = r[:, None] ^ r[None, :]
    lv = np.full((BLOCK, BLOCK), len(LEVELS), np.int32)
    for i, m in enumerate(LEVELS):
        lv[(x >= m) & (x < 2 * m)] = i
    return jnp.asarray(np.concatenate(mats, 0).astype(np.float32), dtype=_MXU), jnp.asarray(lv)


def _hgrn_exponents(g, sums):
    e = _mm01_left(sums, g)
    b = e[2 * BLOCK:, :]
    row = _iota2((BLOCK, D_HEAD), 0)
    out = []
    for m in LEVELS:
        is_q = (row & m) != 0
        if m >= 8:
            grp = b.reshape(BLOCK // (2 * m), 2 * m, D_HEAD)
            ref = jnp.broadcast_to(grp[:, m - 1:m, :], grp.shape).reshape(BLOCK, D_HEAD)
            out.append(jnp.where(is_q, b - ref, ref - b))
        elif m == 4:
            out.append(e[:BLOCK, :])
        elif m == 2:
            out.append(e[BLOCK:2 * BLOCK, :])
        else:
            out.append(jnp.where(is_q, g, 0.0))
    return b, out


def _hgrn_gates(fz, lam, chunk):
    pos = chunk * BLOCK + _iota2((BLOCK, D_HEAD), 0)
    live = pos >= N_PAD
    sg = _sigmoid(fz)
    f = lam + (1.0 - lam) * sg
    g = jnp.where(live, jnp.log(f), 0.0)
    k = jnp.where(live, (1.0 - lam) * (1.0 - sg), 0.0)
    return sg, f, g, k, live


def _level_operand(q, k, exponent, m):
    decay = jnp.exp(exponent)
    is_q = (_iota2((BLOCK, D_HEAD), 0) & m) != 0
    return is_q, decay, (jnp.where(is_q, q, k) * decay).astype(_MXU)


def _hgrn_fwd(qs, fs, vs, lam, bsz, nb, shards):
    t = qs.shape[0]
    sums, levels = _hgrn_tables()
    width = HEAD_GROUP * D_HEAD

    n_sh = len(shards)

    def body(q_ref, f_ref, v_ref, lam_ref, sums_ref, lv_ref, *rest):
        own, (o_ref, sst_ref, sym_ref), rest = rest[:n_sh], rest[n_sh:n_sh + 3], rest[n_sh + 3:]
        gathered, st_scr, sems = rest[:n_sh], rest[n_sh], rest[n_sh + 1:]
        n = pl.program_id(2)
        first, last = _grid_ends(3)
        start, finish = _chip_exchange(own, gathered, sems, slotted=False)
        pl.when(first)(start)

        @pl.when(n == 0)
        def _():
            st_scr[...] = jnp.zeros_like(st_scr)

        lv = lv_ref[...]
        r, c = _iota2((BLOCK, BLOCK), 0), _iota2((BLOCK, BLOCK), 1)
        for hh in range(HEAD_GROUP):
            ls = slice(hh * D_HEAD, (hh + 1) * D_HEAD)
            st = st_scr[hh]
            sst_ref[0, hh, 0] = st
            q, v = q_ref[:, ls], v_ref[:, ls]
            _, _, g, k, _ = _hgrn_gates(f_ref[:, ls], lam_ref[:, ls], n)
            b, exps = _hgrn_exponents(g, sums_ref[...])
            sym = jnp.zeros((BLOCK, BLOCK), F32)
            for li, m in enumerate(LEVELS):
                _, _, x16 = _level_operand(q, k, exps[li], m)
                sym = jnp.where(lv == li, lax.dot_general(x16, x16, _NT, preferred_element_type=F32), sym)
            sym = jnp.where(c == r, jnp.sum(q * k, axis=1, keepdims=True), sym).astype(_MXU)
            sym_ref[0, hh, 0] = sym
            o_ref[:, ls] = _mm_nt_s((q * jnp.exp(b)).astype(_MXU), st) + _mm(jnp.where(c <= r, sym, 0), v)
            b_end = b[BLOCK - 1:BLOCK, :]
            st_scr[hh] = st * jnp.exp(b_end) + _mm_tn(v, k * jnp.exp(b_end - b))
        pl.when(last)(finish)

    blk = pl.BlockSpec((BLOCK, width), lambda b, h, n: (b * nb + n, h))
    hbm = pl.BlockSpec(memory_space=pl.ANY)
    return pl.pallas_call(
        body, name="hgrn_fwd", grid=(bsz, N_HEADS // HEAD_GROUP, nb),
        in_specs=[blk, blk, blk, pl.BlockSpec((1, width), lambda b, h, n: (0, h)),
                  pl.BlockSpec(sums.shape, lambda b, h, n: (0, 0)), pl.BlockSpec(levels.shape, lambda b, h, n: (0, 0))]
        + [hbm] * n_sh,
        out_specs=[blk] + [pl.BlockSpec((1, HEAD_GROUP, 1, D_HEAD, D_HEAD), lambda b, h, n: (b, h, n, 0, 0))] * 2
        + [hbm] * n_sh,
        out_shape=[jax.ShapeDtypeStruct((t, D_MODEL), F32),
                   jax.ShapeDtypeStruct((bsz, N_HEADS, nb, D_HEAD, D_HEAD), F32),
                   jax.ShapeDtypeStruct((bsz, N_HEADS, nb, D_HEAD, D_HEAD), _MXU)]
        + [jax.ShapeDtypeStruct((N_CHIPS,) + a.shape, a.dtype) for a in shards],
        scratch_shapes=[pltpu.VMEM((HEAD_GROUP, D_HEAD, D_HEAD), F32)] + _exchange_scratch(n_sh),
        compiler_params=_params(("arbitrary", "arbitrary", "arbitrary")),
    )(qs, fs, vs, lam, sums, levels, *shards)


def _hgrn_bwd(qs, fs, vs, lam, sst, sym, do, bsz, nb, outgoing):
    t = qs.shape[0]
    sums, levels = _hgrn_tables()
    width = HEAD_GROUP * D_HEAD

    n_out = len(outgoing)

    def body(q_ref, f_ref, v_ref, lam_ref, sst_ref, sym_ref, do_ref, sums_ref, lv_ref, *rest):
        send, (dq_ref, df_ref, dv_ref, dlam_ref), rest = rest[:n_out], rest[n_out:n_out + 4], rest[n_out + 4:]
        landed, dst_scr, gsum_scr, sems = rest[:n_out], rest[n_out], rest[n_out + 1], rest[n_out + 2:]
        n = pl.program_id(2)
        chunk = nb - 1 - n
        first, last = _grid_ends(3)
        start, finish = _chip_exchange(send, landed, sems, slotted=True)
        pl.when(first)(start)

        @pl.when(n == 0)
        def _():
            dst_scr[...] = jnp.zeros_like(dst_scr)
            gsum_scr[...] = jnp.zeros_like(gsum_scr)
            dlam_ref[...] = jnp.zeros_like(dlam_ref)

        lv = lv_ref[...]
        r, c = _iota2((BLOCK, BLOCK), 0), _iota2((BLOCK, BLOCK), 1)
        for hh in range(HEAD_GROUP):
            ls = slice(hh * D_HEAD, (hh + 1) * D_HEAD)
            lam = lam_ref[:, ls]
            q, v, do = q_ref[:, ls], v_ref[:, ls], do_ref[:, ls]
            sg, f, g, k, live = _hgrn_gates(f_ref[:, ls], lam, chunk)
            b, exps = _hgrn_exponents(g, sums_ref[...])
            do16, v16 = do.astype(_MXU), v.astype(_MXU)
            da = lax.dot_general(do16, v16, _NT, preferred_element_type=F32)
            da_t = lax.dot_general(v16, do16, _NT, preferred_element_type=F32)
            da_sym = jnp.where(c < r, da, da_t)
            dq = jnp.zeros((BLOCK, D_HEAD), F32)
            dk = jnp.zeros((BLOCK, D_HEAD), F32)
            db = jnp.zeros((BLOCK, D_HEAD), F32)
            for li, m in enumerate(LEVELS):
                is_q, decay, x16 = _level_operand(q, k, exps[li], m)
                y = jnp.dot(jnp.where(lv == li, da_sym, 0.0).astype(_MXU), x16, preferred_element_type=F32)
                dx = y * decay
                dxq = jnp.where(is_q, dx, 0.0)
                dq = dq + dxq
                dk = dk + (dx - dxq)
                p = x16.astype(F32) * y
                db = db + jnp.where(is_q, p, -p)
            a_t = jnp.where(c >= r, sym_ref[0, hh, 0], 0)
            st, dst = sst_ref[0, hh, 0], dst_scr[hh]
            eb = jnp.exp(b)
            b_end = b[BLOCK - 1:BLOCK, :]
            dec = jnp.exp(b_end - b)
            qh16, kt16 = (q * eb).astype(_MXU), (k * dec).astype(_MXU)
            dq_st = _mm_s(do16, st)
            dk_st = _mm_s(v16, dst)
            d_diag = jnp.sum(do * v, axis=1, keepdims=True)
            dq_ref[:, ls] = (dq + d_diag * k + eb * dq_st).astype(dq_ref.dtype)
            dk = dk + d_diag * q + dec * dk_st
            dv_ref[:, ls] = (jnp.dot(a_t, do16, preferred_element_type=F32) + _mm_nt_s(kt16, dst)).astype(dv_ref.dtype)
            dst_scr[hh] = dst * jnp.exp(b_end) + lax.dot_general(do16, qh16, _TN, preferred_element_type=F32)
            db = db + (qh16.astype(F32) * dq_st - kt16.astype(F32) * dk_st)
            dg = _mm01_left((c >= r).astype(_MXU), db) + gsum_scr[:, ls]
            gsum_scr[:, ls] = gsum_scr[:, ls] + jnp.sum(db, axis=0, keepdims=True)
            slope = (1.0 - lam) * sg * (1.0 - sg)
            df_ref[:, ls] = jnp.where(live, dg * slope / f - dk * slope, 0.0).astype(df_ref.dtype)
            dl = jnp.where(live, (dg / f - dk) * (1.0 - sg), 0.0)
            dlam_ref[0, :, ls] = dlam_ref[0, :, ls] + jnp.sum(dl, axis=0, keepdims=True)
        pl.when(last)(finish)

    blk = pl.BlockSpec((BLOCK, width), lambda b, h, n: (b * nb + nb - 1 - n, h))
    hbm = pl.BlockSpec(memory_space=pl.ANY)
    return pl.pallas_call(
        body, name="hgrn_bwd", grid=(bsz, N_HEADS // HEAD_GROUP, nb),
        in_specs=[blk, blk, blk, pl.BlockSpec((1, width), lambda b, h, n: (0, h)),
                  pl.BlockSpec((1, HEAD_GROUP, 1, D_HEAD, D_HEAD), lambda b, h, n: (b, h, nb - 1 - n, 0, 0)),
                  pl.BlockSpec((1, HEAD_GROUP, 1, D_HEAD, D_HEAD), lambda b, h, n: (b, h, nb - 1 - n, 0, 0)),
                  blk, pl.BlockSpec(sums.shape, lambda b, h, n: (0, 0)), pl.BlockSpec(levels.shape, lambda b, h, n: (0, 0))]
        + [hbm] * n_out,
        out_specs=[blk, blk, blk, pl.BlockSpec((1, 1, width), lambda b, h, n: (b, 0, h))] + [hbm] * n_out,
        out_shape=[jax.ShapeDtypeStruct((t, D_MODEL), _MXU)] * 3 + [jax.ShapeDtypeStruct((bsz, 1, D_MODEL), F32)]
        + [jax.ShapeDtypeStruct(a.shape, a.dtype) for a in outgoing],
        scratch_shapes=[pltpu.VMEM((HEAD_GROUP, D_HEAD, D_HEAD), F32), pltpu.VMEM((1, width), F32)]
        + _exchange_scratch(n_out),
        compiler_params=_params(("arbitrary", "arbitrary", "arbitrary")),
    )(qs, fs, vs, lam, sst, sym, do, sums, levels, *outgoing)


def _sb_valid(i, j):
    qpos = i * TILE + _iota2((TILE, TILE), 0)
    kpos = j * TILE + _iota2((TILE, TILE), 1)
    return (kpos < qpos) & (kpos >= N_PAD)


def _sb_logits(q16, k_blk, valid):
    z = lax.dot_general(q16, k_blk.astype(_MXU), _NT, preferred_element_type=F32) * SB_SCALE
    softplus = jnp.maximum(z, 0.0) + jnp.log(1.0 + jnp.exp(-jnp.abs(z)))
    return jnp.where(valid, -softplus, 0.0), z - softplus


def _sb_fwd(qs, ks, vs, bsz, nq):
    t = qs.shape[0]
    lp = nq * TILE
    width = HEAD_GROUP * D_HEAD
    groups = N_HEADS // HEAD_GROUP
    lanes = [slice(hh * D_HEAD, (hh + 1) * D_HEAD) for hh in range(HEAD_GROUP)]

    def body(q_ref, k_ref, v_ref, o_ref, c_ref, n_ref):
        b, h, i = pl.program_id(0), pl.program_id(1), pl.program_id(2)
        q16 = [q_ref[:, ls].astype(_MXU) for ls in lanes]
        r, c = _iota2((TILE, TILE), 0), _iota2((TILE, TILE), 1)
        after = (r > c).astype(_MXU)

        def more(carry):
            jj, _, _, top = carry
            return (jj <= i) & (top > UNDERFLOW)

        def step(carry):
            jj, accs, sums, _ = carry
            j = i - jj
            ks_ = pl.ds(pl.multiple_of(j * TILE, TILE), TILE)
            valid = _sb_valid(i, j)
            new_accs, new_sums = [], []
            for hh, ls in enumerate(lanes):
                keep, log_beta = _sb_logits(q16[hh], k_ref[ks_, ls], valid)
                after_s = _mm01_right(keep, after)
                a = jnp.where(valid, jnp.exp(log_beta + (sums[hh] + after_s)), 0.0)
                new_accs.append(accs[hh] + _mm(a, v_ref[ks_, ls]))
                new_sums.append(sums[hh] + (after_s[:, 0:1] + keep[:, 0:1]))
            top = functools.reduce(jnp.maximum, [jnp.max(x) for x in new_sums])
            return jj + 1, tuple(new_accs), tuple(new_sums), top

        init = (jnp.int32(0), tuple(jnp.zeros((TILE, D_HEAD), F32) for _ in lanes),
                tuple(jnp.zeros((TILE, 1), F32) for _ in lanes), jnp.float32(0.0))
        visited, accs, sums, _ = lax.while_loop(more, step, init)
        for hh, ls in enumerate(lanes):
            o_ref[:, ls] = accs[hh]
            c_ref[:, ls] = jnp.broadcast_to(sums[hh], (TILE, D_HEAD))
        n_ref[(b * groups + h) * nq + i] = visited.astype(F32)

    blk = pl.BlockSpec((TILE, width), lambda b, h, i: (b * nq + i, h))
    seq = pl.BlockSpec((lp, width), lambda b, h, i: (b, h))
    return pl.pallas_call(
        body, name="sb_fwd", grid=(bsz, groups, nq),
        in_specs=[blk, seq, seq], out_specs=[blk, blk, pl.BlockSpec(memory_space=pltpu.SMEM)],
        out_shape=[jax.ShapeDtypeStruct((t, D_MODEL), F32)] * 2 + [jax.ShapeDtypeStruct((bsz * groups * nq,), F32)],
        compiler_params=_params(("arbitrary", "arbitrary", "arbitrary")),
    )(qs, ks, vs)


def _sb_bwd(qs, ks, vs, ctot, visited, do, bsz, nq):
    t = qs.shape[0]
    lp = nq * TILE
    width = HEAD_GROUP * D_HEAD
    groups = N_HEADS // HEAD_GROUP
    lanes = [slice(hh * D_HEAD, (hh + 1) * D_HEAD) for hh in range(HEAD_GROUP)]

    def body(n_ref, q_ref, k_ref, v_ref, c_ref, do_ref, dq_ref, dk_ref, dv_ref, dk_acc, dv_acc):
        b, h, i = pl.program_id(0), pl.program_id(1), pl.program_id(2)

        @pl.when(i == 0)
        def _():
            dk_acc[...] = jnp.zeros_like(dk_acc)
            dv_acc[...] = jnp.zeros_like(dv_acc)

        q16 = [q_ref[:, ls].astype(_MXU) for ls in lanes]
        do16 = [do_ref[:, ls].astype(_MXU) for ls in lanes]
        totals = [c_ref[:, hh * D_HEAD:hh * D_HEAD + 1] for hh in range(HEAD_GROUP)]
        r, c = _iota2((TILE, TILE), 0), _iota2((TILE, TILE), 1)
        upto = (r <= c).astype(_MXU)
        before = (r < c).astype(_MXU)
        first = jnp.maximum(i + 1 - n_ref[(b * groups + h) * nq + i].astype(jnp.int32), 0)

        def step(j, carry):
            ks_ = pl.ds(pl.multiple_of(j * TILE, TILE), TILE)
            valid = _sb_valid(i, j)
            out = []
            for hh, ls in enumerate(lanes):
                dq, keep_pre, g_pre = carry[hh]
                k_blk, v_blk = k_ref[ks_, ls], v_ref[ks_, ls]
                keep, log_beta = _sb_logits(q16[hh], k_blk, valid)
                keep_upto = _mm01_right(keep, upto)
                a = jnp.where(valid, jnp.exp(log_beta + (totals[hh] - keep_pre - keep_upto)), 0.0)
                da = lax.dot_general(do16[hh], v_blk.astype(_MXU), _NT, preferred_element_type=F32)
                g = a * da
                g_inside = _mm01_right(g, before)
                g_before = g_pre + g_inside
                beta = jnp.exp(log_beta)
                dz = jnp.where(valid, g * (1.0 - beta) - beta * g_before, 0.0) * SB_SCALE
                dz16 = dz.astype(_MXU)
                dq = dq + jnp.dot(dz16, k_blk.astype(_MXU), preferred_element_type=F32)
                dk_acc[ks_, ls] += lax.dot_general(dz16, q16[hh], _TN, preferred_element_type=F32)
                dv_acc[ks_, ls] += lax.dot_general(a.astype(_MXU), do16[hh], _TN, preferred_element_type=F32)
                out.append((dq, keep_pre + keep_upto[:, TILE - 1:TILE],
                            g_pre + (g_inside[:, TILE - 1:TILE] + g[:, TILE - 1:TILE])))
            return tuple(out)

        zero_col = jnp.zeros((TILE, 1), F32)
        init = tuple((jnp.zeros((TILE, D_HEAD), F32), zero_col, zero_col) for _ in lanes)
        res = lax.fori_loop(first, i + 1, step, init)
        for hh, ls in enumerate(lanes):
            dq_ref[:, ls] = res[hh][0].astype(dq_ref.dtype)

        @pl.when(i == nq - 1)
        def _():
            dk_ref[...] = dk_acc[...].astype(dk_ref.dtype)
            dv_ref[...] = dv_acc[...].astype(dv_ref.dtype)

    blk = pl.BlockSpec((TILE, width), lambda b, h, i: (b * nq + i, h))
    seq = pl.BlockSpec((lp, width), lambda b, h, i: (b, h))
    return pl.pallas_call(
        body, name="sb_bwd", grid=(bsz, groups, nq),
        in_specs=[pl.BlockSpec(memory_space=pltpu.SMEM), blk, seq, seq, blk, blk], out_specs=[blk, seq, seq],
        out_shape=[jax.ShapeDtypeStruct((t, D_MODEL), _MXU)] * 3,
        scratch_shapes=[pltpu.VMEM((lp, width), F32)] * 2,
        compiler_params=_params(("arbitrary", "arbitrary", "arbitrary")),
    )(visited, qs, ks, vs, ctot, do)


def _head_norm(o, head_gain):
    outs, rs = [], []
    for h in range(N_HEADS):
        oh = o[:, h * D_HEAD:(h + 1) * D_HEAD]
        r = lax.rsqrt(jnp.mean(oh * oh, axis=-1, keepdims=True) + EPS)
        outs.append(oh * r)
        rs.append(r)
    return outs, rs


def _mix(o, gate, head_gain):
    if head_gain is None:
        on = o
    else:
        outs, _ = _head_norm(o, head_gain)
        on = jnp.concatenate([x * head_gain for x in outs], axis=1)
    return on, on * (gate * _sigmoid(gate))


def _out_fwd(o, gate, h_in, w_out, post_gain, head_gain, name):
    t = o.shape[0]
    tm = _row_tile(t, 256)

    def body(o_ref, g_ref, h_ref, w_ref, pg_ref, hg_ref, ho_ref, u_ref):
        _, mix = _mix(o_ref[...], g_ref[...], hg_ref[...])
        u = jnp.dot(mix.astype(_MXU), w_ref[...], preferred_element_type=F32)
        u_ref[...] = u
        r = lax.rsqrt(jnp.mean(u * u, axis=-1, keepdims=True) + EPS)
        ho_ref[...] = h_ref[...] + u * r * pg_ref[...]

    row = pl.BlockSpec((tm, D_MODEL), lambda i: (i, 0))
    vec = pl.BlockSpec((1, D_MODEL), lambda i: (0, 0))
    return pl.pallas_call(
        body, name=name, grid=(t // tm,),
        in_specs=[row, row, row, pl.BlockSpec((D_MODEL, D_MODEL), lambda i: (0, 0)), vec,
                  pl.BlockSpec((1, D_HEAD), lambda i: (0, 0))],
        out_specs=[row, row], out_shape=[jax.ShapeDtypeStruct((t, D_MODEL), F32)] * 2,
        compiler_params=_params(("arbitrary",)),
    )(o, gate, h_in, w_out, post_gain, head_gain)


def _out_fwd_loss(o, gate, h_in, w_out, post_gain, target, nq, name):
    t = o.shape[0]

    def body(o_ref, g_ref, h_ref, w_ref, pg_ref, t_ref, dh_ref, u_ref, l_ref):
        i = pl.program_id(0)

        @pl.when(i == 0)
        def _():
            l_ref[...] = jnp.zeros_like(l_ref)

        _, mix = _mix(o_ref[...], g_ref[...], None)
        u = jnp.dot(mix.astype(_MXU), w_ref[...], preferred_element_type=F32)
        u_ref[...] = u

        @pl.when(i % nq == 0)
        def _():
            dh_ref[...] = jnp.zeros_like(dh_ref)

        @pl.when(i % nq != 0)
        def _():
            r = lax.rsqrt(jnp.mean(u * u, axis=-1, keepdims=True) + EPS)
            e = h_ref[...] + u * r * pg_ref[...] - t_ref[...]
            dh_ref[...] = e * (1.0 / D_MODEL)
            l_ref[...] += jnp.sum(e * e) * (0.5 / D_MODEL)

    row = pl.BlockSpec((TILE, D_MODEL), lambda i: (i, 0))
    vec = pl.BlockSpec((1, D_MODEL), lambda i: (0, 0))
    return pl.pallas_call(
        body, name=name, grid=(t // TILE,),
        in_specs=[row, row, row, pl.BlockSpec((D_MODEL, D_MODEL), lambda i: (0, 0)), vec,
                  pl.BlockSpec((TILE, D_MODEL), lambda i: ((i // nq) * (nq - 1) + jnp.maximum(i % nq - 1, 0), 0))],
        out_specs=[row, row, pl.BlockSpec((8, 128), lambda i: (0, 0))],
        out_shape=[jax.ShapeDtypeStruct((t, D_MODEL), F32)] * 2 + [jax.ShapeDtypeStruct((8, 128), F32)],
        compiler_params=_params(("arbitrary",)),
    )(o, gate, h_in, w_out, post_gain, target)


def _out_bwd(dh, u, o, gate, w_out, post_gain, head_gain, name, narrow_do):
    t = o.shape[0]
    tm = _row_tile(t, 256)
    has_head = head_gain is not None

    def body(*refs):
        if has_head:
            dh_ref, u_ref, o_ref, g_ref, w_ref, pg_ref, hg_ref, do_ref, dg_ref, gw_ref, gw16_ref, gp_ref, gh_ref = refs
            hg = hg_ref[...]
        else:
            dh_ref, u_ref, o_ref, g_ref, w_ref, pg_ref, do_ref, dg_ref, gw_ref, gw16_ref, gp_ref = refs
            hg = None
        first = pl.program_id(0) == 0

        @pl.when(first)
        def _():
            gw_ref[...] = jnp.zeros_like(gw_ref)
            gp_ref[...] = jnp.zeros_like(gp_ref)
            if has_head:
                gh_ref[...] = jnp.zeros_like(gh_ref)

        dr, u, o, gate = dh_ref[...], u_ref[...], o_ref[...], g_ref[...]
        r = lax.rsqrt(jnp.mean(u * u, axis=-1, keepdims=True) + EPS)
        un = u * r
        gp_ref[...] += jnp.sum(dr * un, axis=0, keepdims=True)
        dun = dr * pg_ref[...]
        du = r * (dun - un * jnp.mean(dun * un, axis=-1, keepdims=True))
        on, mix = _mix(o, gate, hg)
        du16 = du.astype(_MXU)
        gw_ref[...] += lax.dot_general(mix.astype(_MXU), du16, _TN, preferred_element_type=F32)
        dmix = lax.dot_general(du16, w_ref[...], _NT, preferred_element_type=F32)
        sg = _sigmoid(gate)
        dg_ref[...] = (dmix * on * (sg * (1.0 + gate * (1.0 - sg)))).astype(dg_ref.dtype)
        don = dmix * (gate * sg)
        if has_head:
            outs, rs = _head_norm(o, hg)
            gh = jnp.zeros((1, D_HEAD), F32)
            cols = []
            for h in range(N_HEADS):
                dn = don[:, h * D_HEAD:(h + 1) * D_HEAD]
                gh = gh + jnp.sum(dn * outs[h], axis=0, keepdims=True)
                dnn = dn * hg
                cols.append(rs[h] * (dnn - outs[h] * jnp.mean(dnn * outs[h], axis=-1, keepdims=True)))
            gh_ref[...] += gh
            do_ref[...] = jnp.concatenate(cols, axis=1)
        else:
            do_ref[...] = don.astype(do_ref.dtype)

        @pl.when(pl.program_id(0) == pl.num_programs(0) - 1)
        def _():
            gw16_ref[...] = gw_ref[...].astype(_MXU)

    row = pl.BlockSpec((tm, D_MODEL), lambda i: (i, 0))
    vec = pl.BlockSpec((1, D_MODEL), lambda i: (0, 0))
    mat = pl.BlockSpec((D_MODEL, D_MODEL), lambda i: (0, 0))
    in_specs = [row, row, row, row, mat, vec]
    args = [dh, u, o, gate, w_out, post_gain]
    out_specs = [row, row, mat, mat, vec]
    out_shape = [jax.ShapeDtypeStruct((t, D_MODEL), _MXU if narrow_do else F32),
                 jax.ShapeDtypeStruct((t, D_MODEL), _MXU)] + [jax.ShapeDtypeStruct((D_MODEL, D_MODEL), F32),
                                                                  jax.ShapeDtypeStruct((D_MODEL, D_MODEL), _MXU),
                                                                  jax.ShapeDtypeStruct((1, D_MODEL), F32)]
    if has_head:
        in_specs.append(pl.BlockSpec((1, D_HEAD), lambda i: (0, 0)))
        args.append(head_gain)
        out_specs.append(pl.BlockSpec((1, D_HEAD), lambda i: (0, 0)))
        out_shape.append(jax.ShapeDtypeStruct((1, D_HEAD), F32))
    return pl.pallas_call(
        body, name=name, grid=(t // tm,), in_specs=in_specs, out_specs=out_specs, out_shape=out_shape,
        compiler_params=_params(("arbitrary",)),
    )(*args)


def _proj_bwd(ds, w4, h_in, gain, dh_out, name, outgoing=()):
    t = h_in.shape[0]
    tm = _row_tile(t, 256)
    n_out = len(outgoing)

    def body(d0, d1, d2, d3, w_ref, h_ref, g_ref, dho_ref, *rest):
        send, (dhi_ref, gg_ref), rest = rest[:n_out], rest[n_out:n_out + 2], rest[n_out + 2:]
        landed, sems = rest[:n_out], rest[n_out:]
        if n_out:
            first, last = _grid_ends(1)
            start, finish = _chip_exchange(send, landed, sems, slotted=True)
            pl.when(first)(start)

        @pl.when(pl.program_id(0) == 0)
        def _():
            gg_ref[...] = jnp.zeros_like(gg_ref)

        dy = jnp.zeros((tm, D_MODEL), F32)
        for j, d in enumerate((d0, d1, d2, d3)):
            dy = dy + lax.dot_general(d[...].astype(_MXU), w_ref[j], _NT, preferred_element_type=F32)
        x = h_ref[...]
        r = lax.rsqrt(jnp.mean(x * x, axis=-1, keepdims=True) + EPS)
        xn = x * r
        gg_ref[...] += jnp.sum(dy * xn, axis=0, keepdims=True)
        dxn = dy * g_ref[...]
        dhi_ref[...] = dho_ref[...] + r * (dxn - xn * jnp.mean(dxn * xn, axis=-1, keepdims=True))
        if n_out:
            pl.when(last)(finish)

    row = pl.BlockSpec((tm, D_MODEL), lambda i: (i, 0))
    vec = pl.BlockSpec((1, D_MODEL), lambda i: (0, 0))
    hbm = pl.BlockSpec(memory_space=pl.ANY)
    return pl.pallas_call(
        body, name=name, grid=(t // tm,),
        in_specs=[row] * 4 + [pl.BlockSpec((4, D_MODEL, D_MODEL), lambda i: (0, 0, 0)), row, vec, row] + [hbm] * n_out,
        out_specs=[row, vec] + [hbm] * n_out,
        out_shape=[jax.ShapeDtypeStruct((t, D_MODEL), F32), jax.ShapeDtypeStruct((1, D_MODEL), F32)]
        + [jax.ShapeDtypeStruct(a.shape, a.dtype) for a in outgoing],
        scratch_shapes=_exchange_scratch(n_out) if n_out else [],
        compiler_params=_params(("arbitrary",)),
    )(*ds, w4, h_in, gain, dh_out, *outgoing)


def _weight_grad(y, d, name):
    t = y.shape[0]
    tk = _row_tile(t, 512)

    def body(y_ref, d_ref, g_ref, g16_ref):
        @pl.when(pl.program_id(0) == 0)
        def _():
            g_ref[...] = jnp.zeros_like(g_ref)

        g_ref[...] += lax.dot_general(y_ref[...], d_ref[...].astype(_MXU), _TN, preferred_element_type=F32)

        @pl.when(pl.program_id(0) == pl.num_programs(0) - 1)
        def _():
            g16_ref[...] = g_ref[...].astype(_MXU)

    row = pl.BlockSpec((tk, D_MODEL), lambda i: (i, 0))
    mat = pl.BlockSpec((D_MODEL, D_MODEL), lambda i: (0, 0))
    return pl.pallas_call(
        body, name=name, grid=(t // tk,), in_specs=[row, row], out_specs=[mat, mat],
        out_shape=[jax.ShapeDtypeStruct((D_MODEL, D_MODEL), F32), jax.ShapeDtypeStruct((D_MODEL, D_MODEL), _MXU)],
        compiler_params=_params(("arbitrary",)),
    )(y, d)


def _local_step(x, target, meta, pre_norm, post_norm, lam, head_gain, hw_in, hw_out, sb_shards):
    bsz, seq, _ = x.shape
    nq = seq // TILE + 1
    nb = nq * (TILE // BLOCK)
    lp = nq * TILE
    t = bsz * lp
    d4 = D_MODEL // N_CHIPS
    front = jnp.concatenate([jnp.zeros((N_PAD, D_MODEL), F32), meta], axis=0)
    h0 = jnp.concatenate([jnp.broadcast_to(front[None], (bsz, TILE, D_MODEL)), x], axis=1).reshape(t, D_MODEL)
    pre0, pre1, post0, post1 = pre_norm[0:1], pre_norm[1:2], post_norm[0:1], post_norm[1:2]

    y0, q0, f0, v0, g0 = _norm_proj(h0, pre0, hw_in, "norm_proj_hgrn", (False,) * 4)
    o0, sst, sym, sw_in, sw_out = _hgrn_fwd(q0, f0, v0, lam, bsz, nb, sb_shards)
    sw_out = sw_out.reshape(D_MODEL, D_MODEL)
    h1, u0 = _out_fwd(o0, g0, h0, hw_out, post0, head_gain, "out_fwd_hgrn")
    y1, q1, k1, v1, g1 = _norm_proj(h1, pre1, sw_in, "norm_proj_sb", (True, True, True, False))
    o1, ctot, visited = _sb_fwd(q1, k1, v1, bsz, nq)
    dh2, u1, loss_blk = _out_fwd_loss(o1, g1, h1, sw_out, post1, target.reshape(bsz * seq, D_MODEL), nq, "out_fwd_sb")

    do1, dg1, g_sw_out, g_sw_out16, g_post1 = _out_bwd(dh2, u1, o1, g1, sw_out, post1, None, "out_bwd_sb", True)
    dq1, dk1, dv1 = _sb_bwd(q1, k1, v1, ctot, visited, do1, bsz, nq)
    ds1 = (dq1, dk1, dv1, dg1)
    dh1, g_pre1 = _proj_bwd(ds1, sw_in, h1, pre1, dh2, "proj_bwd_sb")
    g_sw_in = [_weight_grad(y1, d, "wgrad_sb_%d" % j) for j, d in enumerate(ds1)]

    do0, dg0, g_hw_out, g_hw_out16, g_post0, g_head = _out_bwd(dh1, u0, o0, g0, hw_out, post0, head_gain, "out_bwd_hgrn",
                                                               False)
    sb_out = (jnp.stack([g16 for _, g16 in g_sw_in]), g_sw_out16.reshape(N_CHIPS, d4, D_MODEL))
    dq0, df0, dv0, dlam, land_sw_in, land_sw_out = _hgrn_bwd(q0, f0, v0, lam, sst, sym, do0, bsz, nb, sb_out)
    ds0 = (dq0, df0, dv0, dg0)
    g_hw_in = [_weight_grad(y0, d, "wgrad_hgrn_%d" % j) for j, d in enumerate(ds0)]
    hg_out = (jnp.stack([g16 for _, g16 in g_hw_in]), g_hw_out16.reshape(N_CHIPS, d4, D_MODEL))
    dh0, g_pre0, land_hw_in, land_hw_out = _proj_bwd(ds0, hw_in, h0, pre0, dh1, "proj_bwd_hgrn", hg_out)

    dh0 = dh0.reshape(bsz, lp, D_MODEL)
    grad_x = dh0[:, TILE:, :]
    g_meta = jnp.sum(dh0[:, N_PAD:TILE, :], axis=0)
    g_lam = jnp.sum(dlam, axis=0)
    small = jnp.concatenate([g_pre0, g_pre1, g_post0, g_post1, g_lam, g_lam,
                             jnp.pad(g_head, ((0, 0), (0, D_MODEL - D_HEAD))), g_meta,
                             jnp.zeros((SMALL_ROWS - 23, D_MODEL), F32)], axis=0)
    rows4 = lambda g: [g[j * d4:(j + 1) * d4] for j in range(N_CHIPS)]
    large = dict(hw_in=(land_hw_in, [g for g, _ in g_hw_in]), sw_in=(land_sw_in, [g for g, _ in g_sw_in]),
                 hw_out=(land_hw_out, rows4(g_hw_out)), sw_out=(land_sw_out, rows4(g_sw_out)))
    return loss_blk, grad_x, small, large


def _prep_weights(hw_in, sw_in, hw_out, sw_out, meta):
    def body(hi_ref, si_ref, ho_ref, so_ref, m_ref, ghi, gho, gm, si16, so16, send_sems, recv_sems):
        x, y, c = _place()
        me = 2 * x + y
        ghi[me] = hi_ref[0].astype(_MXU)
        gho[me] = ho_ref[0].astype(_MXU)
        gm[me] = m_ref[...]
        si16[...] = si_ref[0].astype(_MXU)
        so16[...] = so_ref[0].astype(_MXU)
        outs = (ghi, gho, gm)
        peers = [(1 - x, y), (x, 1 - y), (1 - x, 1 - y)]

        def copy(r, a, slot, to):
            return pltpu.make_async_remote_copy(
                src_ref=outs[a].at[slot], dst_ref=outs[a].at[slot], send_sem=send_sems.at[r * 3 + a],
                recv_sem=recv_sems.at[r * 3 + a], device_id=to, device_id_type=MESH)

        sends = [copy(r, a, me, (px, py, c)) for r, (px, py) in enumerate(peers) for a in range(3)]
        for cp in sends:
            cp.start()
        for r, (px, py) in enumerate(peers):
            for a in range(3):
                copy(r, a, 2 * px + py, (px, py, c)).wait_recv()
        for cp in sends:
            cp.wait_send()

    d4 = D_MODEL // N_CHIPS
    vm = pl.BlockSpec(memory_space=pltpu.VMEM)
    return pl.pallas_call(
        body, name="prep_weights",
        in_specs=[vm] * 5, out_specs=[vm] * 5,
        out_shape=[jax.ShapeDtypeStruct((N_CHIPS, D_MODEL, D_MODEL), _MXU), jax.ShapeDtypeStruct((N_CHIPS, d4, D_MODEL), _MXU),
                   jax.ShapeDtypeStruct((N_CHIPS, N_META, d4), F32),
                   jax.ShapeDtypeStruct((D_MODEL, D_MODEL), _MXU), jax.ShapeDtypeStruct((d4, D_MODEL), _MXU)],
        scratch_shapes=[pltpu.SemaphoreType.DMA((9,)), pltpu.SemaphoreType.DMA((9,))],
        compiler_params=pltpu.CompilerParams(vmem_limit_bytes=VMEM_LIMIT),
    )(hw_in, sw_in, hw_out, sw_out, meta)


def _scatter_small(small):
    def body(sm, lsm, send_sems, recv_sems, local_sem):
        x, y, c = _place()
        mine = 4 * x + 2 * y + c
        local = pltpu.make_async_copy(sm, lsm.at[mine], local_sem)
        local.start()

        def copy(rel, src_dev, to):
            return pltpu.make_async_remote_copy(
                src_ref=sm, dst_ref=lsm.at[src_dev], send_sem=send_sems.at[rel - 1], recv_sem=recv_sems.at[rel - 1],
                device_id=to, device_id_type=MESH)

        flip = lambda bit, v: 1 - v if bit else v
        rels = [(rel, flip(rel & 4, x), flip(rel & 2, y), flip(rel & 1, c)) for rel in range(1, N_DEV)]
        sends = [copy(rel, mine, (px, py, pc)) for rel, px, py, pc in rels]
        for cp in sends:
            cp.start()
        for rel, px, py, pc in rels:
            copy(rel, 4 * px + 2 * py + pc, (px, py, pc)).wait_recv()
        for cp in sends:
            cp.wait_send()
        local.wait()

    hbm = pl.BlockSpec(memory_space=pl.ANY)
    return pl.pallas_call(
        body, name="scatter_small", in_specs=[hbm], out_specs=hbm,
        out_shape=jax.ShapeDtypeStruct((N_DEV, SMALL_ROWS, D_MODEL), F32),
        scratch_shapes=[pltpu.SemaphoreType.DMA((N_DEV - 1,)), pltpu.SemaphoreType.DMA((N_DEV - 1,)),
                        pltpu.SemaphoreType.DMA(())],
    )(small)


def _sum_slots(landed, own, me, name):
    n, rows, _ = landed.shape
    tm = rows if rows < 256 else 256

    def body(me_ref, l_ref, o0, o1, o2, o3, out_ref):
        acc = None
        for k, o in enumerate((o0, o1, o2, o3)):
            term = jnp.where(me_ref[0] == k, o[...], l_ref[k].astype(F32))
            acc = term if acc is None else acc + term
        out_ref[...] = acc

    blk = pl.BlockSpec((tm, D_MODEL), lambda i: (i, 0))
    return pl.pallas_call(
        body, name=name, grid=(rows // tm,),
        in_specs=[pl.BlockSpec(memory_space=pltpu.SMEM), pl.BlockSpec((n, tm, D_MODEL), lambda i: (0, i, 0))] + [blk] * 4,
        out_specs=blk, out_shape=jax.ShapeDtypeStruct((rows, D_MODEL), F32),
        compiler_params=_params(("arbitrary",)),
    )(me, landed, *own)


def _swap_with_sibling(parts):
    def body(a0, a1, a2, a3, b0, b1, b2, b3, send_sems, recv_sems):
        x, y, c = _place()
        copies = [pltpu.make_async_remote_copy(src_ref=s, dst_ref=d, send_sem=send_sems.at[a], recv_sem=recv_sems.at[a],
                                               device_id=(x, y, 1 - c), device_id_type=MESH)
                  for a, (s, d) in enumerate(zip((a0, a1, a2, a3), (b0, b1, b2, b3)))]
        for cp in copies:
            cp.start()
        for cp in copies:
            cp.wait()

    hbm = pl.BlockSpec(memory_space=pl.ANY)
    return pl.pallas_call(
        body, name="swap_with_sibling", in_specs=[hbm] * 4, out_specs=[hbm] * 4,
        out_shape=[jax.ShapeDtypeStruct(p.shape, F32) for p in parts],
        scratch_shapes=[pltpu.SemaphoreType.DMA((4,)), pltpu.SemaphoreType.DMA((4,))],
    )(*parts)


def _adamw_math(w, g, m, v):
    m = ADAM_B1 * m + (1.0 - ADAM_B1) * g
    v = ADAM_B2 * v + (1.0 - ADAM_B2) * (g * g)
    m_hat = m / (1.0 - ADAM_B1 ** ADAM_STEP)
    v_hat = v / (1.0 - ADAM_B2 ** ADAM_STEP)
    delta = -ADAM_LR * (m_hat / (jnp.sqrt(v_hat) + ADAM_EPS) + ADAM_WD * w)
    return delta, m, v


def _adamw(w, g_parts, m, v, name):
    rows, cols = w.shape
    tm = rows if rows < 256 else 256
    n = len(g_parts)

    def body(*refs):
        w_ref, m_ref, v_ref = refs[n:n + 3]
        g_ref, d_ref, nm_ref, nv_ref = refs[n + 3:]
        g = refs[0][...]
        for p in refs[1:n]:
            g = g + p[...]
        g_ref[...] = g
        d_ref[...], nm_ref[...], nv_ref[...] = _adamw_math(w_ref[...], g, m_ref[...], v_ref[...])

    blk = pl.BlockSpec((tm, cols), lambda i: (i, 0))
    return pl.pallas_call(
        body, name=name, grid=(rows // tm,), in_specs=[blk] * (n + 3), out_specs=[blk] * 4,
        out_shape=[jax.ShapeDtypeStruct((rows, cols), F32)] * 4,
        compiler_params=_params(("arbitrary",)),
    )(*g_parts, w, m, v)


def _lam_of(hgrn_lb):
    def body(lb_ref, o_ref):
        lb = lb_ref[...]
        e = jnp.exp(lb - jnp.max(lb, axis=0, keepdims=True))
        o_ref[...] = e[0:1, :] / jnp.sum(e, axis=0, keepdims=True)

    return pl.pallas_call(body, name="lam_of", out_shape=jax.ShapeDtypeStruct((1, D_MODEL), F32))(hgrn_lb)


def _small_grads(land_small, lam):
    def body(l_ref, lam_ref, o_ref):
        acc = l_ref[0]
        for k in range(1, N_DEV):
            acc = acc + l_ref[k]
        p = lam_ref[...]
        slope = p * (1.0 - p)
        row = _iota2((SMALL_ROWS, D_MODEL), 0)
        o_ref[...] = acc * jnp.where(row == 4, slope, jnp.where(row == 5, -slope, 1.0))

    return pl.pallas_call(body, name="small_grads",
                          out_shape=jax.ShapeDtypeStruct((SMALL_ROWS, D_MODEL), F32))(land_small, lam)


def kernel(x, meta_tokens, pre_norm, post_norm, hgrn_w_in, hgrn_lb, hgrn_out_norm, hgrn_w_out, sb_w_in, sb_w_out, loss_target, m_meta_tokens, m_pre_norm, m_post_norm, m_hgrn_w_in, m_hgrn_lb, m_hgrn_out_norm, m_hgrn_w_out, m_sb_w_in, m_sb_w_out, v_meta_tokens, v_pre_norm, v_post_norm, v_hgrn_w_in, v_hgrn_lb, v_hgrn_out_norm, v_hgrn_w_out, v_sb_w_in, v_sb_w_out):
    d4 = D_MODEL // N_CHIPS
    chip = 2 * lax.axis_index("x") + lax.axis_index("y")
    hw_in, hw_out, meta4, sw_in16, sw_out16 = _prep_weights(hgrn_w_in, sb_w_in, hgrn_w_out, sb_w_out, meta_tokens)
    meta = meta4.transpose(1, 0, 2).reshape(N_META, D_MODEL)
    lam = _lam_of(hgrn_lb)
    loss_blk, grad_x, small, large = _local_step(
        x, loss_target, meta, pre_norm, post_norm, lam, hgrn_out_norm,
        hw_in, hw_out.reshape(D_MODEL, D_MODEL), (sw_in16, sw_out16))
    loss = lax.psum(loss_blk[0, 0], ("x", "y", "c"))

    me = jnp.reshape(chip, (1,)).astype(jnp.int32)
    parts = [_sum_slots(*large[n], me, "sum_" + n) for n in ("hw_in", "sw_in", "hw_out", "sw_out")]
    sib = _swap_with_sibling(parts)
    small = _small_grads(_scatter_small(small), lam)

    res = {}
    res["hgrn_w_in"] = _adamw(hgrn_w_in[0], [parts[0], sib[0]], m_hgrn_w_in[0], v_hgrn_w_in[0], "adamw_hw_in")
    res["sb_w_in"] = _adamw(sb_w_in[0], [parts[1], sib[1]], m_sb_w_in[0], v_sb_w_in[0], "adamw_sw_in")
    res["hgrn_w_out"] = _adamw(hgrn_w_out[0], [parts[2], sib[2]], m_hgrn_w_out[0], v_hgrn_w_out[0], "adamw_hw_out")
    res["sb_w_out"] = _adamw(sb_w_out[0], [parts[3], sib[3]], m_sb_w_out[0], v_sb_w_out[0], "adamw_sw_out")
    res["pre_norm"] = _adamw(pre_norm, [small[0:2]], m_pre_norm, v_pre_norm, "adamw_pre")
    res["post_norm"] = _adamw(post_norm, [small[2:4]], m_post_norm, v_post_norm, "adamw_post")
    res["hgrn_lb"] = _adamw(hgrn_lb, [small[4:6]], m_hgrn_lb, v_hgrn_lb, "adamw_lb")
    res["hgrn_out_norm"] = _adamw(hgrn_out_norm, [small[6:7, :D_HEAD]], m_hgrn_out_norm, v_hgrn_out_norm, "adamw_head")
    g_meta = lax.dynamic_slice_in_dim(small[7:7 + N_META], chip * d4, d4, axis=1)
    res["meta_tokens"] = _adamw(meta_tokens, [g_meta], m_meta_tokens, v_meta_tokens, "adamw_meta")
    for n in ("hgrn_w_in", "hgrn_w_out", "sb_w_in", "sb_w_out"):
        res[n] = tuple(a[None] for a in res[n])
    order = ("meta_tokens", "pre_norm", "post_norm", "hgrn_w_in", "hgrn_lb", "hgrn_out_norm", "hgrn_w_out",
             "sb_w_in", "sb_w_out")
    return (loss, grad_x, *[res[n][0] for n in order], *[res[n][1] for n in order],
            *[res[n][2] for n in order], *[res[n][3] for n in order])
```

```python
import functools

import jax
import numpy as np
import jax.numpy as jnp
from jax import lax
from jax.experimental import pallas as pl
from jax.experimental.pallas import tpu as pltpu

F32 = jnp.float32
_MXU = jnp.bfloat16

D_MODEL = 1024
N_HEADS = 8
D_HEAD = 128
BLOCK = 128
N_META = 16
TILE = 256
N_PAD = TILE - N_META
UNDERFLOW = -104.0
EPS = 1e-6
SB_SCALE = D_HEAD ** -0.5
ADAM_LR, ADAM_B1, ADAM_B2, ADAM_EPS, ADAM_WD, ADAM_STEP = 0.001, 0.9, 0.999, 1e-08, 0.01, 10
N_CHIPS = 4
N_DEV = 8
SMALL_ROWS = 24
VMEM_LIMIT = 56 * 1024 * 1024
MESH = pl.DeviceIdType.MESH

_NT = (((1,), (1,)), ((), ()))
_TN = (((0,), (0,)), ((), ()))


def _mm(a, b):
    return jnp.dot(a.astype(_MXU), b.astype(_MXU), preferred_element_type=F32)


def _mm_nt(a, b):
    return lax.dot_general(a.astype(_MXU), b.astype(_MXU), _NT, preferred_element_type=F32)


def _mm_tn(a, b):
    return lax.dot_general(a.astype(_MXU), b.astype(_MXU), _TN, preferred_element_type=F32)


def _split2(x):
    hi = x.astype(_MXU)
    return hi, (x - hi.astype(F32)).astype(_MXU)


def _mm_s(a16, state):
    hi, lo = _split2(state)
    return jnp.dot(a16, hi, preferred_element_type=F32) + jnp.dot(a16, lo, preferred_element_type=F32)


def _mm_nt_s(a16, state):
    hi, lo = _split2(state)
    return (lax.dot_general(a16, hi, _NT, preferred_element_type=F32)
            + lax.dot_general(a16, lo, _NT, preferred_element_type=F32))


def _mm01_right(x, m01):
    hi, lo = _split2(x)
    return jnp.dot(hi, m01, preferred_element_type=F32) + jnp.dot(lo, m01, preferred_element_type=F32)


def _mm01_left(m01, x):
    hi, lo = _split2(x)
    return jnp.dot(m01, hi, preferred_element_type=F32) + jnp.dot(m01, lo, preferred_element_type=F32)


def _iota2(shape, dim):
    return lax.broadcasted_iota(jnp.int32, shape, dim)


def _row_tile(total, pref):
    t = pref
    while total % t:
        t -= BLOCK
    return t


def _params(sem, limit=VMEM_LIMIT):
    return pltpu.CompilerParams(dimension_semantics=sem, vmem_limit_bytes=limit)


def _sigmoid(x):
    return 1.0 / (1.0 + jnp.exp(-x))


def _grid_ends(ndim):
    first, last = True, True
    for d in range(ndim):
        first = first & (pl.program_id(d) == 0)
        last = last & (pl.program_id(d) == pl.num_programs(d) - 1)
    return first, last


def _place():
    return lax.axis_index("x"), lax.axis_index("y"), lax.axis_index("c")


def _exchange_scratch(n):
    return [pltpu.SemaphoreType.DMA((3 * n,)), pltpu.SemaphoreType.DMA((3 * n,)), pltpu.SemaphoreType.DMA((n,))]


def _chip_exchange(srcs, dsts, sems, slotted):
    send_sems, recv_sems, local_sems = sems
    x, y, c = _place()
    me = 2 * x + y
    peers = [(1 - x, y), (x, 1 - y), (1 - x, 1 - y)]
    n = len(dsts)

    def remote(r, a, sending):
        px, py = peers[r]
        p = 2 * px + py
        return pltpu.make_async_remote_copy(
            src_ref=srcs[a].at[p] if slotted else srcs[a], dst_ref=dsts[a].at[me if sending else p],
            send_sem=send_sems.at[r * n + a], recv_sem=recv_sems.at[r * n + a],
            device_id=(px, py, c), device_id_type=MESH)

    def local(a):
        return pltpu.make_async_copy(srcs[a].at[me] if slotted else srcs[a], dsts[a].at[me], local_sems.at[a])

    def start():
        for a in range(n):
            local(a).start()
        for r in range(3):
            for a in range(n):
                remote(r, a, True).start()

    def finish():
        for r in range(3):
            for a in range(n):
                remote(r, a, False).wait_recv()
        for r in range(3):
            for a in range(n):
                remote(r, a, True).wait_send()
        for a in range(n):
            local(a).wait()

    return start, finish


def _norm_proj(h, gain, w4, name, narrow):
    t = h.shape[0]
    tm = _row_tile(t, 256)

    def body(h_ref, g_ref, w_ref, y_ref, s0, s1, s2, s3):
        x = h_ref[...]
        r = lax.rsqrt(jnp.mean(x * x, axis=-1, keepdims=True) + EPS)
        y = (x * r * g_ref[...]).astype(_MXU)
        y_ref[...] = y
        for j, s in enumerate((s0, s1, s2, s3)):
            s[...] = jnp.dot(y, w_ref[j], preferred_element_type=F32).astype(s.dtype)

    row = pl.BlockSpec((tm, D_MODEL), lambda i: (i, 0))
    return pl.pallas_call(
        body, name=name, grid=(t // tm,),
        in_specs=[row, pl.BlockSpec((1, D_MODEL), lambda i: (0, 0)),
                  pl.BlockSpec((4, D_MODEL, D_MODEL), lambda i: (0, 0, 0))],
        out_specs=[row] * 5,
        out_shape=[jax.ShapeDtypeStruct((t, D_MODEL), _MXU)]
        + [jax.ShapeDtypeStruct((t, D_MODEL), _MXU if n else F32) for n in narrow],
        compiler_params=_params(("arbitrary",)),
    )(h, gain, w4)


LEVELS = (64, 32, 16, 8, 4, 2, 1)
HEAD_GROUP = 2


def _hgrn_tables():
    r = np.arange(BLOCK)
    mats = [r[None, :] <= r[:, None]]
    x = r[:, None] ^ r[None, :]
    lv = np.full((BLOCK, BLOCK), len(LEVELS), np.int32)
    for i, m in enumerate(LEVELS):
        lv[(x >= m) & (x < 2 * m)] = i
    return jnp.asarray(np.concatenate(mats, 0).astype(np.float32), dtype=_MXU), jnp.asarray(lv)


def _hgrn_exponents(g, sums):
    b = _mm01_left(sums, g)
    row = _iota2((BLOCK, D_HEAD), 0)
    out = []
    for m in LEVELS:
        is_q = (row & m) != 0
        if m >= 4:
            grp = b.reshape(BLOCK // (2 * m), 2 * m, D_HEAD)
            ref = jnp.broadcast_to(grp[:, m - 1:m, :], grp.shape).reshape(BLOCK, D_HEAD)
            d = b - ref
            out.append(jnp.where(is_q, d, -d))
        elif m == 2:
            below, above = pltpu.roll(g, 1, axis=0), pltpu.roll(g, BLOCK - 1, axis=0)
            low = row & 3
            out.append(jnp.where(low == 3, g + below, jnp.where(low == 2, g, jnp.where(low == 0, above, 0.0))))
        else:
            out.append(jnp.where(is_q, g, 0.0))
    return b, out


def _hgrn_gates(fz, lam, chunk):
    pos = chunk * BLOCK + _iota2((BLOCK, D_HEAD), 0)
    live = pos >= N_PAD
    sg = _sigmoid(fz)
    f = lam + (1.0 - lam) * sg
    g = jnp.where(live, jnp.log(f), 0.0)
    k = jnp.where(live, (1.0 - lam) * (1.0 - sg), 0.0)
    return sg, f, g, k, live


def _level_operand(q, k, exponent, m):
    decay = jnp.exp(exponent)
    is_q = (_iota2((BLOCK, D_HEAD), 0) & m) != 0
    return is_q, decay, (jnp.where(is_q, q, k) * decay).astype(_MXU)


def _hgrn_fwd(qs, fs, vs, lam, bsz, nb, shards):
    t = qs.shape[0]
    sums, levels = _hgrn_tables()
    width = HEAD_GROUP * D_HEAD

    n_sh = len(shards)

    def body(q_ref, f_ref, v_ref, lam_ref, sums_ref, lv_ref, *rest):
        own, (o_ref, sst_ref, sym_ref), rest = rest[:n_sh], rest[n_sh:n_sh + 3], rest[n_sh + 3:]
        gathered, st_scr, sems = rest[:n_sh], rest[n_sh], rest[n_sh + 1:]
        n = pl.program_id(2)
        first, last = _grid_ends(3)
        start, finish = _chip_exchange(own, gathered, sems, slotted=False)
        pl.when(first)(start)

        @pl.when(n == 0)
        def _():
            st_scr[...] = jnp.zeros_like(st_scr)

        lv = lv_ref[...]
        r, c = _iota2((BLOCK, BLOCK), 0), _iota2((BLOCK, BLOCK), 1)
        for hh in range(HEAD_GROUP):
            ls = slice(hh * D_HEAD, (hh + 1) * D_HEAD)
            st = st_scr[hh]
            sst_ref[0, hh, 0] = st
            q, v = q_ref[:, ls], v_ref[:, ls]
            _, _, g, k, _ = _hgrn_gates(f_ref[:, ls], lam_ref[:, ls], n)
            b, exps = _hgrn_exponents(g, sums_ref[...])
            sym = jnp.zeros((BLOCK, BLOCK), F32)
            for li, m in enumerate(LEVELS):
                _, _, x16 = _level_operand(q, k, exps[li], m)
                sym = jnp.where(lv == li, lax.dot_general(x16, x16, _NT, preferred_element_type=F32), sym)
            sym = jnp.where(c == r, jnp.sum(q * k, axis=1, keepdims=True), sym).astype(_MXU)
            sym_ref[0, hh, 0] = sym
            o_ref[:, ls] = _mm_nt(q * jnp.exp(b), st) + _mm(jnp.where(c <= r, sym, 0), v)
            b_end = b[BLOCK - 1:BLOCK, :]
            st_scr[hh] = st * jnp.exp(b_end) + _mm_tn(v, k * jnp.exp(b_end - b))
        pl.when(last)(finish)

    blk = pl.BlockSpec((BLOCK, width), lambda b, h, n: (b * nb + n, h))
    hbm = pl.BlockSpec(memory_space=pl.ANY)
    return pl.pallas_call(
        body, name="hgrn_fwd", grid=(bsz, N_HEADS // HEAD_GROUP, nb),
        in_specs=[blk, blk, blk, pl.BlockSpec((1, width), lambda b, h, n: (0, h)),
                  pl.BlockSpec(sums.shape, lambda b, h, n: (0, 0)), pl.BlockSpec(levels.shape, lambda b, h, n: (0, 0))]
        + [hbm] * n_sh,
        out_specs=[blk] + [pl.BlockSpec((1, HEAD_GROUP, 1, D_HEAD, D_HEAD), lambda b, h, n: (b, h, n, 0, 0))] * 2
        + [hbm] * n_sh,
        out_shape=[jax.ShapeDtypeStruct((t, D_MODEL), F32),
                   jax.ShapeDtypeStruct((bsz, N_HEADS, nb, D_HEAD, D_HEAD), F32),
                   jax.ShapeDtypeStruct((bsz, N_HEADS, nb, D_HEAD, D_HEAD), _MXU)]
        + [jax.ShapeDtypeStruct((N_CHIPS,) + a.shape, a.dtype) for a in shards],
        scratch_shapes=[pltpu.VMEM((HEAD_GROUP, D_HEAD, D_HEAD), F32)] + _exchange_scratch(n_sh),
        compiler_params=_params(("arbitrary", "arbitrary", "arbitrary")),
    )(qs, fs, vs, lam, sums, levels, *shards)


def _hgrn_bwd(qs, fs, vs, lam, sst, sym, do, bsz, nb, outgoing):
    t = qs.shape[0]
    sums, levels = _hgrn_tables()
    width = HEAD_GROUP * D_HEAD

    n_out = len(outgoing)

    def body(q_ref, f_ref, v_ref, lam_ref, sst_ref, sym_ref, do_ref, sums_ref, lv_ref, *rest):
        send, (dq_ref, df_ref, dv_ref, dlam_ref), rest = rest[:n_out], rest[n_out:n_out + 4], rest[n_out + 4:]
        landed, dst_scr, gsum_scr, sems = rest[:n_out], rest[n_out], rest[n_out + 1], rest[n_out + 2:]
        n = pl.program_id(2)
        chunk = nb - 1 - n
        first, last = _grid_ends(3)
        start, finish = _chip_exchange(send, landed, sems, slotted=True)
        pl.when(first)(start)

        @pl.when(n == 0)
        def _():
            dst_scr[...] = jnp.zeros_like(dst_scr)
            gsum_scr[...] = jnp.zeros_like(gsum_scr)
            dlam_ref[...] = jnp.zeros_like(dlam_ref)

        lv = lv_ref[...]
        r, c = _iota2((BLOCK, BLOCK), 0), _iota2((BLOCK, BLOCK), 1)
        for hh in range(HEAD_GROUP):
            ls = slice(hh * D_HEAD, (hh + 1) * D_HEAD)
            lam = lam_ref[:, ls]
            q, v, do = q_ref[:, ls], v_ref[:, ls], do_ref[:, ls]
            sg, f, g, k, live = _hgrn_gates(f_ref[:, ls], lam, chunk)
            b, exps = _hgrn_exponents(g, sums_ref[...])
            do16, v16 = do.astype(_MXU), v.astype(_MXU)
            da = lax.dot_general(do16, v16, _NT, preferred_element_type=F32)
            da_sym = jnp.where(c < r, da, da.T)
            dq = jnp.zeros((BLOCK, D_HEAD), F32)
            dqk = jnp.zeros((BLOCK, D_HEAD), F32)
            db_q = jnp.zeros((BLOCK, D_HEAD), F32)
            db_qk = jnp.zeros((BLOCK, D_HEAD), F32)
            for li, m in enumerate(LEVELS):
                is_q, decay, x16 = _level_operand(q, k, exps[li], m)
                y = jnp.dot(jnp.where(lv == li, da_sym, 0.0).astype(_MXU), x16, preferred_element_type=F32)
                dx = y * decay
                dq = dq + jnp.where(is_q, dx, 0.0)
                dqk = dqk + dx
                p = x16.astype(F32) * y
                db_q = db_q + jnp.where(is_q, p, 0.0)
                db_qk = db_qk + p
            dk = dqk - dq
            db = 2.0 * db_q - db_qk
            a_t = jnp.where(c >= r, sym_ref[0, hh, 0], 0)
            st, dst = sst_ref[0, hh, 0], dst_scr[hh]
            eb = jnp.exp(b)
            b_end = b[BLOCK - 1:BLOCK, :]
            dec = jnp.exp(b_end - b)
            qh16, kt16 = (q * eb).astype(_MXU), (k * dec).astype(_MXU)
            dq_st = _mm_s(do16, st)
            dk_st = _mm_s(v16, dst)
            d_diag = jnp.sum(do * v, axis=1, keepdims=True)
            dq_ref[:, ls] = (dq + d_diag * k + eb * dq_st).astype(dq_ref.dtype)
            dk = dk + d_diag * q + dec * dk_st
            dv_ref[:, ls] = (jnp.dot(a_t, do16, preferred_element_type=F32) + _mm_nt_s(kt16, dst)).astype(dv_ref.dtype)
            dst_scr[hh] = dst * jnp.exp(b_end) + lax.dot_general(do16, qh16, _TN, preferred_element_type=F32)
            db = db + (qh16.astype(F32) * dq_st - kt16.astype(F32) * dk_st)
            dg = _mm01_left((c >= r).astype(_MXU), db) + gsum_scr[:, ls]
            gsum_scr[:, ls] = gsum_scr[:, ls] + jnp.sum(db, axis=0, keepdims=True)
            slope = (1.0 - lam) * sg * (1.0 - sg)
            df_ref[:, ls] = jnp.where(live, dg * slope / f - dk * slope, 0.0).astype(df_ref.dtype)
            dl = jnp.where(live, (dg / f - dk) * (1.0 - sg), 0.0)
            dlam_ref[0, :, ls] = dlam_ref[0, :, ls] + jnp.sum(dl, axis=0, keepdims=True)
        pl.when(last)(finish)

    blk = pl.BlockSpec((BLOCK, width), lambda b, h, n: (b * nb + nb - 1 - n, h))
    hbm = pl.BlockSpec(memory_space=pl.ANY)
    return pl.pallas_call(
        body, name="hgrn_bwd", grid=(bsz, N_HEADS // HEAD_GROUP, nb),
        in_specs=[blk, blk, blk, pl.BlockSpec((1, width), lambda b, h, n: (0, h)),
                  pl.BlockSpec((1, HEAD_GROUP, 1, D_HEAD, D_HEAD), lambda b, h, n: (b, h, nb - 1 - n, 0, 0)),
                  pl.BlockSpec((1, HEAD_GROUP, 1, D_HEAD, D_HEAD), lambda b, h, n: (b, h, nb - 1 - n, 0, 0)),
                  blk, pl.BlockSpec(sums.shape, lambda b, h, n: (0, 0)), pl.BlockSpec(levels.shape, lambda b, h, n: (0, 0))]
        + [hbm] * n_out,
        out_specs=[blk, blk, blk, pl.BlockSpec((1, 1, width), lambda b, h, n: (b, 0, h))] + [hbm] * n_out,
        out_shape=[jax.ShapeDtypeStruct((t, D_MODEL), _MXU)] * 3 + [jax.ShapeDtypeStruct((bsz, 1, D_MODEL), F32)]
        + [jax.ShapeDtypeStruct(a.shape, a.dtype) for a in outgoing],
        scratch_shapes=[pltpu.VMEM((HEAD_GROUP, D_HEAD, D_HEAD), F32), pltpu.VMEM((1, width), F32)]
        + _exchange_scratch(n_out),
        compiler_params=_params(("arbitrary", "arbitrary", "arbitrary")),
    )(qs, fs, vs, lam, sst, sym, do, sums, levels, *outgoing)


def _sb_valid(i, j):
    qpos = i * TILE + _iota2((TILE, TILE), 0)
    kpos = j * TILE + _iota2((TILE, TILE), 1)
    return (kpos < qpos) & (kpos >= N_PAD)


def _sb_logits(q16, k_blk, valid):
    z = lax.dot_general(q16, k_blk.astype(_MXU), _NT, preferred_element_type=F32) * SB_SCALE
    softplus = jnp.maximum(z, 0.0) + jnp.log(1.0 + jnp.exp(-jnp.abs(z)))
    return jnp.where(valid, -softplus, 0.0), z - softplus


def _sb_fwd(qs, ks, vs, bsz, nq):
    t = qs.shape[0]
    lp = nq * TILE
    width = HEAD_GROUP * D_HEAD
    groups = N_HEADS // HEAD_GROUP
    lanes = [slice(hh * D_HEAD, (hh + 1) * D_HEAD) for hh in range(HEAD_GROUP)]

    def body(q_ref, k_ref, v_ref, o_ref, c_ref, n_ref):
        b, h, i = pl.program_id(0), pl.program_id(1), pl.program_id(2)
        q16 = [q_ref[:, ls].astype(_MXU) for ls in lanes]
        r, c = _iota2((TILE, TILE), 0), _iota2((TILE, TILE), 1)
        after = (r > c).astype(_MXU)

        def more(carry):
            jj, _, _, top = carry
            return (jj <= i) & (top > UNDERFLOW)

        def step(carry):
            jj, accs, sums, _ = carry
            j = i - jj
            ks_ = pl.ds(pl.multiple_of(j * TILE, TILE), TILE)
            valid = _sb_valid(i, j)
            new_accs, new_sums = [], []
            for hh, ls in enumerate(lanes):
                keep, log_beta = _sb_logits(q16[hh], k_ref[ks_, ls], valid)
                after_s = _mm01_right(keep, after)
                a = jnp.where(valid, jnp.exp(log_beta + (sums[hh] + after_s)), 0.0)
                new_accs.append(accs[hh] + _mm(a, v_ref[ks_, ls]))
                new_sums.append(sums[hh] + (after_s[:, 0:1] + keep[:, 0:1]))
            top = functools.reduce(jnp.maximum, [jnp.max(x) for x in new_sums])
            return jj + 1, tuple(new_accs), tuple(new_sums), top

        init = (jnp.int32(0), tuple(jnp.zeros((TILE, D_HEAD), F32) for _ in lanes),
                tuple(jnp.zeros((TILE, 1), F32) for _ in lanes), jnp.float32(0.0))
        visited, accs, sums, _ = lax.while_loop(more, step, init)
        for hh, ls in enumerate(lanes):
            o_ref[:, ls] = accs[hh]
            c_ref[:, ls] = jnp.broadcast_to(sums[hh], (TILE, D_HEAD))
        n_ref[(b * groups + h) * nq + i] = visited.astype(F32)

    blk = pl.BlockSpec((TILE, width), lambda b, h, i: (b * nq + i, h))
    seq = pl.BlockSpec((lp, width), lambda b, h, i: (b, h))
    return pl.pallas_call(
        body, name="sb_fwd", grid=(bsz, groups, nq),
        in_specs=[blk, seq, seq], out_specs=[blk, blk, pl.BlockSpec(memory_space=pltpu.SMEM)],
        out_shape=[jax.ShapeDtypeStruct((t, D_MODEL), F32)] * 2 + [jax.ShapeDtypeStruct((bsz * groups * nq,), F32)],
        compiler_params=_params(("arbitrary", "arbitrary", "arbitrary")),
    )(qs, ks, vs)


def _sb_bwd(qs, ks, vs, ctot, visited, do, bsz, nq):
    t = qs.shape[0]
    lp = nq * TILE
    width = HEAD_GROUP * D_HEAD
    groups = N_HEADS // HEAD_GROUP
    lanes = [slice(hh * D_HEAD, (hh + 1) * D_HEAD) for hh in range(HEAD_GROUP)]

    def body(n_ref, q_ref, k_ref, v_ref, c_ref, do_ref, dq_ref, dk_ref, dv_ref, dk_acc, dv_acc):
        b, h, i = pl.program_id(0), pl.program_id(1), pl.program_id(2)

        @pl.when(i == 0)
        def _():
            dk_acc[...] = jnp.zeros_like(dk_acc)
            dv_acc[...] = jnp.zeros_like(dv_acc)

        q16 = [q_ref[:, ls].astype(_MXU) for ls in lanes]
        do16 = [do_ref[:, ls].astype(_MXU) for ls in lanes]
        totals = [c_ref[:, hh * D_HEAD:hh * D_HEAD + 1] for hh in range(HEAD_GROUP)]
        r, c = _iota2((TILE, TILE), 0), _iota2((TILE, TILE), 1)
        upto = (r <= c).astype(_MXU)
        before = (r < c).astype(_MXU)
        first = jnp.maximum(i + 1 - n_ref[(b * groups + h) * nq + i].astype(jnp.int32), 0)

        def step(j, carry):
            ks_ = pl.ds(pl.multiple_of(j * TILE, TILE), TILE)
            valid = _sb_valid(i, j)
            out = []
            for hh, ls in enumerate(lanes):
                dq, keep_pre, g_pre = carry[hh]
                k_blk, v_blk = k_ref[ks_, ls], v_ref[ks_, ls]
                keep, log_beta = _sb_logits(q16[hh], k_blk, valid)
                keep_upto = _mm01_right(keep, upto)
                a = jnp.where(valid, jnp.exp(log_beta + (totals[hh] - keep_pre - keep_upto)), 0.0)
                da = lax.dot_general(do16[hh], v_blk.astype(_MXU), _NT, preferred_element_type=F32)
                g = a * da
                g_inside = _mm01_right(g, before)
                g_before = g_pre + g_inside
                beta = jnp.exp(log_beta)
                dz = jnp.where(valid, g * (1.0 - beta) - beta * g_before, 0.0) * SB_SCALE
                dz16 = dz.astype(_MXU)
                dq = dq + jnp.dot(dz16, k_blk.astype(_MXU), preferred_element_type=F32)
                dk_acc[ks_, ls] += lax.dot_general(dz16, q16[hh], _TN, preferred_element_type=F32)
                dv_acc[ks_, ls] += lax.dot_general(a.astype(_MXU), do16[hh], _TN, preferred_element_type=F32)
                out.append((dq, keep_pre + keep_upto[:, TILE - 1:TILE],
                            g_pre + (g_inside[:, TILE - 1:TILE] + g[:, TILE - 1:TILE])))
            return tuple(out)

        zero_col = jnp.zeros((TILE, 1), F32)
        init = tuple((jnp.zeros((TILE, D_HEAD), F32), zero_col, zero_col) for _ in lanes)
        res = lax.fori_loop(first, i + 1, step, init)
        for hh, ls in enumerate(lanes):
            dq_ref[:, ls] = res[hh][0].astype(dq_ref.dtype)

        @pl.when(i == nq - 1)
        def _():
            dk_ref[...] = dk_acc[...].astype(dk_ref.dtype)
            dv_ref[...] = dv_acc[...].astype(dv_ref.dtype)

    blk = pl.BlockSpec((TILE, width), lambda b, h, i: (b * nq + i, h))
    seq = pl.BlockSpec((lp, width), lambda b, h, i: (b, h))
    return pl.pallas_call(
        body, name="sb_bwd", grid=(bsz, groups, nq),
        in_specs=[pl.BlockSpec(memory_space=pltpu.SMEM), blk, seq, seq, blk, blk], out_specs=[blk, seq, seq],
        out_shape=[jax.ShapeDtypeStruct((t, D_MODEL), _MXU)] * 3,
        scratch_shapes=[pltpu.VMEM((lp, width), F32)] * 2,
        compiler_params=_params(("arbitrary", "arbitrary", "arbitrary")),
    )(visited, qs, ks, vs, ctot, do)


def _head_norm(o, head_gain):
    outs, rs = [], []
    for h in range(N_HEADS):
        oh = o[:, h * D_HEAD:(h + 1) * D_HEAD]
        r = lax.rsqrt(jnp.mean(oh * oh, axis=-1, keepdims=True) + EPS)
        outs.append(oh * r)
        rs.append(r)
    return outs, rs


def _mix(o, gate, head_gain):
    if head_gain is None:
        on = o
    else:
        outs, _ = _head_norm(o, head_gain)
        on = jnp.concatenate([x * head_gain for x in outs], axis=1)
    return on, on * (gate * _sigmoid(gate))


def _out_fwd(o, gate, h_in, w_out, post_gain, head_gain, name):
    t = o.shape[0]
    tm = _row_tile(t, 256)

    def body(o_ref, g_ref, h_ref, w_ref, pg_ref, hg_ref, ho_ref, u_ref):
        _, mix = _mix(o_ref[...], g_ref[...], hg_ref[...])
        u = jnp.dot(mix.astype(_MXU), w_ref[...], preferred_element_type=F32)
        u_ref[...] = u
        r = lax.rsqrt(jnp.mean(u * u, axis=-1, keepdims=True) + EPS)
        ho_ref[...] = h_ref[...] + u * r * pg_ref[...]

    row = pl.BlockSpec((tm, D_MODEL), lambda i: (i, 0))
    vec = pl.BlockSpec((1, D_MODEL), lambda i: (0, 0))
    return pl.pallas_call(
        body, name=name, grid=(t // tm,),
        in_specs=[row, row, row, pl.BlockSpec((D_MODEL, D_MODEL), lambda i: (0, 0)), vec,
                  pl.BlockSpec((1, D_HEAD), lambda i: (0, 0))],
        out_specs=[row, row], out_shape=[jax.ShapeDtypeStruct((t, D_MODEL), F32)] * 2,
        compiler_params=_params(("arbitrary",)),
    )(o, gate, h_in, w_out, post_gain, head_gain)


def _out_fwd_loss(o, gate, h_in, w_out, post_gain, target, nq, name):
    t = o.shape[0]

    def body(o_ref, g_ref, h_ref, w_ref, pg_ref, t_ref, dh_ref, u_ref, l_ref):
        i = pl.program_id(0)

        @pl.when(i == 0)
        def _():
            l_ref[...] = jnp.zeros_like(l_ref)

        _, mix = _mix(o_ref[...], g_ref[...], None)
        u = jnp.dot(mix.astype(_MXU), w_ref[...], preferred_element_type=F32)
        u_ref[...] = u

        @pl.when(i % nq == 0)
        def _():
            dh_ref[...] = jnp.zeros_like(dh_ref)

        @pl.when(i % nq != 0)
        def _():
            r = lax.rsqrt(jnp.mean(u * u, axis=-1, keepdims=True) + EPS)
            e = h_ref[...] + u * r * pg_ref[...] - t_ref[...]
            dh_ref[...] = e * (1.0 / D_MODEL)
            l_ref[...] += jnp.sum(e * e) * (0.5 / D_MODEL)

    row = pl.BlockSpec((TILE, D_MODEL), lambda i: (i, 0))
    vec = pl.BlockSpec((1, D_MODEL), lambda i: (0, 0))
    return pl.pallas_call(
        body, name=name, grid=(t // TILE,),
        in_specs=[row, row, row, pl.BlockSpec((D_MODEL, D_MODEL), lambda i: (0, 0)), vec,
                  pl.BlockSpec((TILE, D_MODEL), lambda i: ((i // nq) * (nq - 1) + jnp.maximum(i % nq - 1, 0), 0))],
        out_specs=[row, row, pl.BlockSpec((8, 128), lambda i: (0, 0))],
        out_shape=[jax.ShapeDtypeStruct((t, D_MODEL), F32)] * 2 + [jax.ShapeDtypeStruct((8, 128), F32)],
        compiler_params=_params(("arbitrary",)),
    )(o, gate, h_in, w_out, post_gain, target)


def _out_bwd(dh, u, o, gate, w_out, post_gain, head_gain, name, narrow_do):
    t = o.shape[0]
    tm = _row_tile(t, 256)
    has_head = head_gain is not None

    def body(*refs):
        if has_head:
            dh_ref, u_ref, o_ref, g_ref, w_ref, pg_ref, hg_ref, do_ref, dg_ref, gw_ref, gw16_ref, gp_ref, gh_ref = refs
            hg = hg_ref[...]
        else:
            dh_ref, u_ref, o_ref, g_ref, w_ref, pg_ref, do_ref, dg_ref, gw_ref, gw16_ref, gp_ref = refs
            hg = None
        first = pl.program_id(0) == 0

        @pl.when(first)
        def _():
            gw_ref[...] = jnp.zeros_like(gw_ref)
            gp_ref[...] = jnp.zeros_like(gp_ref)
            if has_head:
                gh_ref[...] = jnp.zeros_like(gh_ref)

        dr, u, o, gate = dh_ref[...], u_ref[...], o_ref[...], g_ref[...]
        r = lax.rsqrt(jnp.mean(u * u, axis=-1, keepdims=True) + EPS)
        un = u * r
        gp_ref[...] += jnp.sum(dr * un, axis=0, keepdims=True)
        dun = dr * pg_ref[...]
        du = r * (dun - un * jnp.mean(dun * un, axis=-1, keepdims=True))
        on, mix = _mix(o, gate, hg)
        du16 = du.astype(_MXU)
        gw_ref[...] += lax.dot_general(mix.astype(_MXU), du16, _TN, preferred_element_type=F32)
        dmix = lax.dot_general(du16, w_ref[...], _NT, preferred_element_type=F32)
        sg = _sigmoid(gate)
        dg_ref[...] = (dmix * on * (sg * (1.0 + gate * (1.0 - sg)))).astype(dg_ref.dtype)
        don = dmix * (gate * sg)
        if has_head:
            outs, rs = _head_norm(o, hg)
            gh = jnp.zeros((1, D_HEAD), F32)
            cols = []
            for h in range(N_HEADS):
                dn = don[:, h * D_HEAD:(h + 1) * D_HEAD]
                gh = gh + jnp.sum(dn * outs[h], axis=0, keepdims=True)
                dnn = dn * hg
                cols.append(rs[h] * (dnn - outs[h] * jnp.mean(dnn * outs[h], axis=-1, keepdims=True)))
            gh_ref[...] += gh
            do_ref[...] = jnp.concatenate(cols, axis=1)
        else:
            do_ref[...] = don.astype(do_ref.dtype)

        @pl.when(pl.program_id(0) == pl.num_programs(0) - 1)
        def _():
            gw16_ref[...] = gw_ref[...].astype(_MXU)

    row = pl.BlockSpec((tm, D_MODEL), lambda i: (i, 0))
    vec = pl.BlockSpec((1, D_MODEL), lambda i: (0, 0))
    mat = pl.BlockSpec((D_MODEL, D_MODEL), lambda i: (0, 0))
    in_specs = [row, row, row, row, mat, vec]
    args = [dh, u, o, gate, w_out, post_gain]
    out_specs = [row, row, mat, mat, vec]
    out_shape = [jax.ShapeDtypeStruct((t, D_MODEL), _MXU if narrow_do else F32),
                 jax.ShapeDtypeStruct((t, D_MODEL), _MXU)] + [jax.ShapeDtypeStruct((D_MODEL, D_MODEL), F32),
                                                                  jax.ShapeDtypeStruct((D_MODEL, D_MODEL), _MXU),
                                                                  jax.ShapeDtypeStruct((1, D_MODEL), F32)]
    if has_head:
        in_specs.append(pl.BlockSpec((1, D_HEAD), lambda i: (0, 0)))
        args.append(head_gain)
        out_specs.append(pl.BlockSpec((1, D_HEAD), lambda i: (0, 0)))
        out_shape.append(jax.ShapeDtypeStruct((1, D_HEAD), F32))
    return pl.pallas_call(
        body, name=name, grid=(t // tm,), in_specs=in_specs, out_specs=out_specs, out_shape=out_shape,
        compiler_params=_params(("arbitrary",)),
    )(*args)


def _proj_bwd(ds, w4, h_in, gain, dh_out, name, outgoing=()):
    t = h_in.shape[0]
    tm = _row_tile(t, 256)
    n_out = len(outgoing)

    def body(d0, d1, d2, d3, w_ref, h_ref, g_ref, dho_ref, *rest):
        send, (dhi_ref, gg_ref), rest = rest[:n_out], rest[n_out:n_out + 2], rest[n_out + 2:]
        landed, sems = rest[:n_out], rest[n_out:]
        if n_out:
            first, last = _grid_ends(1)
            start, finish = _chip_exchange(send, landed, sems, slotted=True)
            pl.when(first)(start)

        @pl.when(pl.program_id(0) == 0)
        def _():
            gg_ref[...] = jnp.zeros_like(gg_ref)

        dy = jnp.zeros((tm, D_MODEL), F32)
        for j, d in enumerate((d0, d1, d2, d3)):
            dy = dy + lax.dot_general(d[...].astype(_MXU), w_ref[j], _NT, preferred_element_type=F32)
        x = h_ref[...]
        r = lax.rsqrt(jnp.mean(x * x, axis=-1, keepdims=True) + EPS)
        xn = x * r
        gg_ref[...] += jnp.sum(dy * xn, axis=0, keepdims=True)
        dxn = dy * g_ref[...]
        dhi_ref[...] = dho_ref[...] + r * (dxn - xn * jnp.mean(dxn * xn, axis=-1, keepdims=True))
        if n_out:
            pl.when(last)(finish)

    row = pl.BlockSpec((tm, D_MODEL), lambda i: (i, 0))
    vec = pl.BlockSpec((1, D_MODEL), lambda i: (0, 0))
    hbm = pl.BlockSpec(memory_space=pl.ANY)
    return pl.pallas_call(
        body, name=name, grid=(t // tm,),
        in_specs=[row] * 4 + [pl.BlockSpec((4, D_MODEL, D_MODEL), lambda i: (0, 0, 0)), row, vec, row] + [hbm] * n_out,
        out_specs=[row, vec] + [hbm] * n_out,
        out_shape=[jax.ShapeDtypeStruct((t, D_MODEL), F32), jax.ShapeDtypeStruct((1, D_MODEL), F32)]
        + [jax.ShapeDtypeStruct(a.shape, a.dtype) for a in outgoing],
        scratch_shapes=_exchange_scratch(n_out) if n_out else [],
        compiler_params=_params(("arbitrary",)),
    )(*ds, w4, h_in, gain, dh_out, *outgoing)


def _weight_grad(y, d, name):
    t = y.shape[0]
    tk = _row_tile(t, 512)

    def body(y_ref, d_ref, g_ref, g16_ref):
        @pl.when(pl.program_id(0) == 0)
        def _():
            g_ref[...] = jnp.zeros_like(g_ref)

        g_ref[...] += lax.dot_general(y_ref[...], d_ref[...].astype(_MXU), _TN, preferred_element_type=F32)

        @pl.when(pl.program_id(0) == pl.num_programs(0) - 1)
        def _():
            g16_ref[...] = g_ref[...].astype(_MXU)

    row = pl.BlockSpec((tk, D_MODEL), lambda i: (i, 0))
    mat = pl.BlockSpec((D_MODEL, D_MODEL), lambda i: (0, 0))
    return pl.pallas_call(
        body, name=name, grid=(t // tk,), in_specs=[row, row], out_specs=[mat, mat],
        out_shape=[jax.ShapeDtypeStruct((D_MODEL, D_MODEL), F32), jax.ShapeDtypeStruct((D_MODEL, D_MODEL), _MXU)],
        compiler_params=_params(("arbitrary",)),
    )(y, d)


def _local_step(x, target, meta, pre_norm, post_norm, lam, head_gain, hw_in, hw_out, sb_shards):
    bsz, seq, _ = x.shape
    nq = seq // TILE + 1
    nb = nq * (TILE // BLOCK)
    lp = nq * TILE
    t = bsz * lp
    d4 = D_MODEL // N_CHIPS
    front = jnp.concatenate([jnp.zeros((N_PAD, D_MODEL), F32), meta], axis=0)
    h0 = jnp.concatenate([jnp.broadcast_to(front[None], (bsz, TILE, D_MODEL)), x], axis=1).reshape(t, D_MODEL)
    pre0, pre1, post0, post1 = pre_norm[0:1], pre_norm[1:2], post_norm[0:1], post_norm[1:2]

    y0, q0, f0, v0, g0 = _norm_proj(h0, pre0, hw_in, "norm_proj_hgrn", (False,) * 4)
    o0, sst, sym, sw_in, sw_out = _hgrn_fwd(q0, f0, v0, lam, bsz, nb, sb_shards)
    sw_out = sw_out.reshape(D_MODEL, D_MODEL)
    h1, u0 = _out_fwd(o0, g0, h0, hw_out, post0, head_gain, "out_fwd_hgrn")
    y1, q1, k1, v1, g1 = _norm_proj(h1, pre1, sw_in, "norm_proj_sb", (True, True, True, False))
    o1, ctot, visited = _sb_fwd(q1, k1, v1, bsz, nq)
    dh2, u1, loss_blk = _out_fwd_loss(o1, g1, h1, sw_out, post1, target.reshape(bsz * seq, D_MODEL), nq, "out_fwd_sb")

    do1, dg1, g_sw_out, g_sw_out16, g_post1 = _out_bwd(dh2, u1, o1, g1, sw_out, post1, None, "out_bwd_sb", True)
    dq1, dk1, dv1 = _sb_bwd(q1, k1, v1, ctot, visited, do1, bsz, nq)
    ds1 = (dq1, dk1, dv1, dg1)
    dh1, g_pre1 = _proj_bwd(ds1, sw_in, h1, pre1, dh2, "proj_bwd_sb")
    g_sw_in = [_weight_grad(y1, d, "wgrad_sb_%d" % j) for j, d in enumerate(ds1)]

    do0, dg0, g_hw_out, g_hw_out16, g_post0, g_head = _out_bwd(dh1, u0, o0, g0, hw_out, post0, head_gain, "out_bwd_hgrn",
                                                               False)
    sb_out = (jnp.stack([g16 for _, g16 in g_sw_in]), g_sw_out16.reshape(N_CHIPS, d4, D_MODEL))
    dq0, df0, dv0, dlam, land_sw_in, land_sw_out = _hgrn_bwd(q0, f0, v0, lam, sst, sym, do0, bsz, nb, sb_out)
    ds0 = (dq0, df0, dv0, dg0)
    g_hw_in = [_weight_grad(y0, d, "wgrad_hgrn_%d" % j) for j, d in enumerate(ds0)]
    hg_out = (jnp.stack([g16 for _, g16 in g_hw_in]), g_hw_out16.reshape(N_CHIPS, d4, D_MODEL))
    dh0, g_pre0, land_hw_in, land_hw_out = _proj_bwd(ds0, hw_in, h0, pre0, dh1, "proj_bwd_hgrn", hg_out)

    dh0 = dh0.reshape(bsz, lp, D_MODEL)
    grad_x = dh0[:, TILE:, :]
    g_meta = jnp.sum(dh0[:, N_PAD:TILE, :], axis=0)
    g_lam = jnp.sum(dlam, axis=0)
    small = jnp.concatenate([g_pre0, g_pre1, g_post0, g_post1, g_lam, g_lam,
                             jnp.pad(g_head, ((0, 0), (0, D_MODEL - D_HEAD))), g_meta,
                             jnp.zeros((SMALL_ROWS - 23, D_MODEL), F32)], axis=0)
    rows4 = lambda g: [g[j * d4:(j + 1) * d4] for j in range(N_CHIPS)]
    large = dict(hw_in=(land_hw_in, [g for g, _ in g_hw_in]), sw_in=(land_sw_in, [g for g, _ in g_sw_in]),
                 hw_out=(land_hw_out, rows4(g_hw_out)), sw_out=(land_sw_out, rows4(g_sw_out)))
    return loss_blk, grad_x, small, large


def _prep_weights(hw_in, sw_in, hw_out, sw_out, meta):
    def body(hi_ref, si_ref, ho_ref, so_ref, m_ref, ghi, gho, gm, si16, so16, far_send, far_recv, near_send, near_recv):
        x, y, c = _place()
        me = 2 * x + y
        ghi[me] = hi_ref[0].astype(_MXU)
        gho[me] = ho_ref[0].astype(_MXU)
        gm[me] = m_ref[...]
        si16[...] = si_ref[0].astype(_MXU)
        so16[...] = so_ref[0].astype(_MXU)
        outs = (ghi, gho, gm)
        peers = [(1 - x, y), (x, 1 - y), (1 - x, 1 - y)]

        def half(a, slot, which):
            rows = outs[a].shape[1] // 2
            return outs[a].at[slot, pl.ds(which * rows, rows), :]

        def far(r, a, slot):
            px, py = peers[r]
            return pltpu.make_async_remote_copy(
                src_ref=half(a, slot, c), dst_ref=half(a, slot, c), send_sem=far_send.at[r * 3 + a],
                recv_sem=far_recv.at[r * 3 + a], device_id=(px, py, c), device_id_type=MESH)

        def near(r, a, which):
            px, py = peers[r]
            return pltpu.make_async_remote_copy(
                src_ref=half(a, 2 * px + py, which), dst_ref=half(a, 2 * px + py, which),
                send_sem=near_send.at[r * 3 + a], recv_sem=near_recv.at[r * 3 + a],
                device_id=(x, y, 1 - c), device_id_type=MESH)

        for r in range(3):
            for a in range(3):
                far(r, a, me).start()
        for r, (px, py) in enumerate(peers):
            for a in range(3):
                far(r, a, 2 * px + py).wait_recv()
                near(r, a, c).start()
        for r in range(3):
            for a in range(3):
                near(r, a, 1 - c).wait_recv()
        for r in range(3):
            for a in range(3):
                far(r, a, me).wait_send()
                near(r, a, c).wait_send()

    d4 = D_MODEL // N_CHIPS
    vm = pl.BlockSpec(memory_space=pltpu.VMEM)
    return pl.pallas_call(
        body, name="prep_weights",
        in_specs=[vm] * 5, out_specs=[vm] * 5,
        out_shape=[jax.ShapeDtypeStruct((N_CHIPS, D_MODEL, D_MODEL), _MXU), jax.ShapeDtypeStruct((N_CHIPS, d4, D_MODEL), _MXU),
                   jax.ShapeDtypeStruct((N_CHIPS, N_META, d4), F32),
                   jax.ShapeDtypeStruct((D_MODEL, D_MODEL), _MXU), jax.ShapeDtypeStruct((d4, D_MODEL), _MXU)],
        scratch_shapes=[pltpu.SemaphoreType.DMA((9,))] * 4,
        compiler_params=pltpu.CompilerParams(vmem_limit_bytes=VMEM_LIMIT),
    )(hw_in, sw_in, hw_out, sw_out, meta)


def _scatter_small(small):
    def body(sm, lsm, send_sems, recv_sems, local_sem):
        x, y, c = _place()
        mine = 4 * x + 2 * y + c
        local = pltpu.make_async_copy(sm, lsm.at[mine], local_sem)
        local.start()

        def copy(rel, src_dev, to):
            return pltpu.make_async_remote_copy(
                src_ref=sm, dst_ref=lsm.at[src_dev], send_sem=send_sems.at[rel - 1], recv_sem=recv_sems.at[rel - 1],
                device_id=to, device_id_type=MESH)

        flip = lambda bit, v: 1 - v if bit else v
        rels = [(rel, flip(rel & 4, x), flip(rel & 2, y), flip(rel & 1, c)) for rel in range(1, N_DEV)]
        sends = [copy(rel, mine, (px, py, pc)) for rel, px, py, pc in rels]
        for cp in sends:
            cp.start()
        for rel, px, py, pc in rels:
            copy(rel, 4 * px + 2 * py + pc, (px, py, pc)).wait_recv()
        for cp in sends:
            cp.wait_send()
        local.wait()

    hbm = pl.BlockSpec(memory_space=pl.ANY)
    return pl.pallas_call(
        body, name="scatter_small", in_specs=[hbm], out_specs=hbm,
        out_shape=jax.ShapeDtypeStruct((N_DEV, SMALL_ROWS, D_MODEL), F32),
        scratch_shapes=[pltpu.SemaphoreType.DMA((N_DEV - 1,)), pltpu.SemaphoreType.DMA((N_DEV - 1,)),
                        pltpu.SemaphoreType.DMA(())],
    )(small)


def _sum_slots(landed, own, me, name):
    n, rows, _ = landed.shape
    tm = rows if rows < 256 else 256

    def body(me_ref, l_ref, o0, o1, o2, o3, out_ref):
        acc = None
        for k, o in enumerate((o0, o1, o2, o3)):
            term = jnp.where(me_ref[0] == k, o[...], l_ref[k].astype(F32))
            acc = term if acc is None else acc + term
        out_ref[...] = acc

    blk = pl.BlockSpec((tm, D_MODEL), lambda i: (i, 0))
    return pl.pallas_call(
        body, name=name, grid=(rows // tm,),
        in_specs=[pl.BlockSpec(memory_space=pltpu.SMEM), pl.BlockSpec((n, tm, D_MODEL), lambda i: (0, i, 0))] + [blk] * 4,
        out_specs=blk, out_shape=jax.ShapeDtypeStruct((rows, D_MODEL), F32),
        compiler_params=_params(("arbitrary",)),
    )(me, landed, *own)


def _swap_with_sibling(parts):
    def body(a0, a1, a2, a3, b0, b1, b2, b3, send_sems, recv_sems):
        x, y, c = _place()
        copies = [pltpu.make_async_remote_copy(src_ref=s, dst_ref=d, send_sem=send_sems.at[a], recv_sem=recv_sems.at[a],
                                               device_id=(x, y, 1 - c), device_id_type=MESH)
                  for a, (s, d) in enumerate(zip((a0, a1, a2, a3), (b0, b1, b2, b3)))]
        for cp in copies:
            cp.start()
        for cp in copies:
            cp.wait()

    hbm = pl.BlockSpec(memory_space=pl.ANY)
    return pl.pallas_call(
        body, name="swap_with_sibling", in_specs=[hbm] * 4, out_specs=[hbm] * 4,
        out_shape=[jax.ShapeDtypeStruct(p.shape, F32) for p in parts],
        scratch_shapes=[pltpu.SemaphoreType.DMA((4,)), pltpu.SemaphoreType.DMA((4,))],
    )(*parts)


def _adamw_math(w, g, m, v):
    m = ADAM_B1 * m + (1.0 - ADAM_B1) * g
    v = ADAM_B2 * v + (1.0 - ADAM_B2) * (g * g)
    m_hat = m / (1.0 - ADAM_B1 ** ADAM_STEP)
    v_hat = v / (1.0 - ADAM_B2 ** ADAM_STEP)
    delta = -ADAM_LR * (m_hat / (jnp.sqrt(v_hat) + ADAM_EPS) + ADAM_WD * w)
    return delta, m, v


def _adamw(w, g_parts, m, v, name):
    rows, cols = w.shape
    tm = rows if rows < 256 else 256
    n = len(g_parts)

    def body(*refs):
        w_ref, m_ref, v_ref = refs[n:n + 3]
        g_ref, d_ref, nm_ref, nv_ref = refs[n + 3:]
        g = refs[0][...]
        for p in refs[1:n]:
            g = g + p[...]
        g_ref[...] = g
        d_ref[...], nm_ref[...], nv_ref[...] = _adamw_math(w_ref[...], g, m_ref[...], v_ref[...])

    blk = pl.BlockSpec((tm, cols), lambda i: (i, 0))
    return pl.pallas_call(
        body, name=name, grid=(rows // tm,), in_specs=[blk] * (n + 3), out_specs=[blk] * 4,
        out_shape=[jax.ShapeDtypeStruct((rows, cols), F32)] * 4,
        compiler_params=_params(("arbitrary",)),
    )(*g_parts, w, m, v)


def _lam_of(hgrn_lb):
    def body(lb_ref, o_ref):
        lb = lb_ref[...]
        e = jnp.exp(lb - jnp.max(lb, axis=0, keepdims=True))
        o_ref[...] = e[0:1, :] / jnp.sum(e, axis=0, keepdims=True)

    return pl.pallas_call(body, name="lam_of", out_shape=jax.ShapeDtypeStruct((1, D_MODEL), F32))(hgrn_lb)


def _small_grads(land_small, lam):
    def body(l_ref, lam_ref, o_ref):
        acc = l_ref[0]
        for k in range(1, N_DEV):
            acc = acc + l_ref[k]
        p = lam_ref[...]
        slope = p * (1.0 - p)
        row = _iota2((SMALL_ROWS, D_MODEL), 0)
        o_ref[...] = acc * jnp.where(row == 4, slope, jnp.where(row == 5, -slope, 1.0))

    return pl.pallas_call(body, name="small_grads",
                          out_shape=jax.ShapeDtypeStruct((SMALL_ROWS, D_MODEL), F32))(land_small, lam)


def kernel(x, meta_tokens, pre_norm, post_norm, hgrn_w_in, hgrn_lb, hgrn_out_norm, hgrn_w_out, sb_w_in, sb_w_out, loss_target, m_meta_tokens, m_pre_norm, m_post_norm, m_hgrn_w_in, m_hgrn_lb, m_hgrn_out_norm, m_hgrn_w_out, m_sb_w_in, m_sb_w_out, v_meta_tokens, v_pre_norm, v_post_norm, v_hgrn_w_in, v_hgrn_lb, v_hgrn_out_norm, v_hgrn_w_out, v_sb_w_in, v_sb_w_out):
    d4 = D_MODEL // N_CHIPS
    chip = 2 * lax.axis_index("x") + lax.axis_index("y")
    hw_in, hw_out, meta4, sw_in16, sw_out16 = _prep_weights(hgrn_w_in, sb_w_in, hgrn_w_out, sb_w_out, meta_tokens)
    meta = meta4.transpose(1, 0, 2).reshape(N_META, D_MODEL)
    lam = _lam_of(hgrn_lb)
    loss_blk, grad_x, small, large = _local_step(
        x, loss_target, meta, pre_norm, post_norm, lam, hgrn_out_norm,
        hw_in, hw_out.reshape(D_MODEL, D_MODEL), (sw_in16, sw_out16))
    loss = lax.psum(loss_blk[0, 0], ("x", "y", "c"))

    me = jnp.reshape(chip, (1,)).astype(jnp.int32)
    parts = [_sum_slots(*large[n], me, "sum_" + n) for n in ("hw_in", "sw_in", "hw_out", "sw_out")]
    sib = _swap_with_sibling(parts)
    small = _small_grads(_scatter_small(small), lam)

    res = {}
    res["hgrn_w_in"] = _adamw(hgrn_w_in[0], [parts[0], sib[0]], m_hgrn_w_in[0], v_hgrn_w_in[0], "adamw_hw_in")
    res["sb_w_in"] = _adamw(sb_w_in[0], [parts[1], sib[1]], m_sb_w_in[0], v_sb_w_in[0], "adamw_sw_in")
    res["hgrn_w_out"] = _adamw(hgrn_w_out[0], [parts[2], sib[2]], m_hgrn_w_out[0], v_hgrn_w_out[0], "adamw_hw_out")
    res["sb_w_out"] = _adamw(sb_w_out[0], [parts[3], sib[3]], m_sb_w_out[0], v_sb_w_out[0], "adamw_sw_out")
    res["pre_norm"] = _adamw(pre_norm, [small[0:2]], m_pre_norm, v_pre_norm, "adamw_pre")
    res["post_norm"] = _adamw(post_norm, [small[2:4]], m_post_norm, v_post_norm, "adamw_post")
    res["hgrn_lb"] = _adamw(hgrn_lb, [small[4:6]], m_hgrn_lb, v_hgrn_lb, "adamw_lb")
    res["hgrn_out_norm"] = _adamw(hgrn_out_norm, [small[6:7, :D_HEAD]], m_hgrn_out_norm, v_hgrn_out_norm, "adamw_head")
    g_meta = lax.dynamic_slice_in_dim(small[7:7 + N_META], chip * d4, d4, axis=1)
    res["meta_tokens"] = _adamw(meta_tokens, [g_meta], m_meta_tokens, v_meta_tokens, "adamw_meta")
    for n in ("hgrn_w_in", "hgrn_w_out", "sb_w_in", "sb_w_out"):
        res[n] = tuple(a[None] for a in res[n])
    order = ("meta_tokens", "pre_norm", "post_norm", "hgrn_w_in", "hgrn_lb", "hgrn_out_norm", "hgrn_w_out",
             "sb_w_in", "sb_w_out")
    return (loss, grad_x, *[res[n][0] for n in order], *[res[n][1] for n in order],
            *[res[n][2] for n in order], *[res[n][3] for n in order])
```

```python
import functools

import jax
import numpy as np
import jax.numpy as jnp
from jax import lax
from jax.experimental import pallas as pl
from jax.experimental.pallas import tpu as pltpu

F32 = jnp.float32
_MXU = jnp.bfloat16

D_MODEL = 1024
N_HEADS = 8
D_HEAD = 128
BLOCK = 128
N_META = 16
TILE = 256
N_PAD = TILE - N_META
UNDERFLOW = -105.0
EPS = 1e-6
SB_SCALE = D_HEAD ** -0.5
ADAM_LR, ADAM_B1, ADAM_B2, ADAM_EPS, ADAM_WD, ADAM_STEP = 0.001, 0.9, 0.999, 1e-08, 0.01, 10
N_CHIPS = 4
N_DEV = 8
SMALL_ROWS = 24
VMEM_LIMIT = 56 * 1024 * 1024
MESH = pl.DeviceIdType.MESH

_NT = (((1,), (1,)), ((), ()))
_TN = (((0,), (0,)), ((), ()))


def _mm(a, b):
    return jnp.dot(a.astype(_MXU), b.astype(_MXU), preferred_element_type=F32)


def _mm_nt(a, b):
    return lax.dot_general(a.astype(_MXU), b.astype(_MXU), _NT, preferred_element_type=F32)


def _mm_tn(a, b):
    return lax.dot_general(a.astype(_MXU), b.astype(_MXU), _TN, preferred_element_type=F32)


def _split2(x):
    hi = x.astype(_MXU)
    return hi, (x - hi.astype(F32)).astype(_MXU)


def _mm_s(a16, state):
    hi, lo = _split2(state)
    return jnp.dot(a16, hi, preferred_element_type=F32) + jnp.dot(a16, lo, preferred_element_type=F32)


def _mm_nt_s(a16, state):
    hi, lo = _split2(state)
    return (lax.dot_general(a16, hi, _NT, preferred_element_type=F32)
            + lax.dot_general(a16, lo, _NT, preferred_element_type=F32))


def _mm01_right(x, m01):
    return jnp.dot(x.astype(_MXU), m01, preferred_element_type=F32)


def _mm01_left(m01, x):
    hi, lo = _split2(x)
    return jnp.dot(m01, hi, preferred_element_type=F32) + jnp.dot(m01, lo, preferred_element_type=F32)


def _iota2(shape, dim):
    return lax.broadcasted_iota(jnp.int32, shape, dim)


def _row_tile(total, pref):
    t = pref
    while total % t:
        t -= BLOCK
    return t


def _params(sem, limit=VMEM_LIMIT):
    return pltpu.CompilerParams(dimension_semantics=sem, vmem_limit_bytes=limit)


def _sigmoid(x):
    return 1.0 / (1.0 + jnp.exp(-x))


def _grid_ends(ndim):
    first, last = True, True
    for d in range(ndim):
        first = first & (pl.program_id(d) == 0)
        last = last & (pl.program_id(d) == pl.num_programs(d) - 1)
    return first, last


def _place():
    return lax.axis_index("x"), lax.axis_index("y"), lax.axis_index("c")


def _exchange_scratch(n):
    return [pltpu.SemaphoreType.DMA((3 * n,)), pltpu.SemaphoreType.DMA((3 * n,)), pltpu.SemaphoreType.DMA((n,))]


def _chip_exchange(srcs, dsts, sems, slotted):
    send_sems, recv_sems, local_sems = sems
    x, y, c = _place()
    me = 2 * x + y
    peers = [(1 - x, y), (x, 1 - y), (1 - x, 1 - y)]
    n = len(dsts)

    def remote(r, a, sending):
        px, py = peers[r]
        p = 2 * px + py
        return pltpu.make_async_remote_copy(
            src_ref=srcs[a].at[p] if slotted else srcs[a], dst_ref=dsts[a].at[me if sending else p],
            send_sem=send_sems.at[r * n + a], recv_sem=recv_sems.at[r * n + a],
            device_id=(px, py, c), device_id_type=MESH)

    def local(a):
        return pltpu.make_async_copy(srcs[a].at[me] if slotted else srcs[a], dsts[a].at[me], local_sems.at[a])

    def start():
        for a in range(n):
            local(a).start()
        for r in range(3):
            for a in range(n):
                remote(r, a, True).start()

    def finish():
        for r in range(3):
            for a in range(n):
                remote(r, a, False).wait_recv()
        for r in range(3):
            for a in range(n):
                remote(r, a, True).wait_send()
        for a in range(n):
            local(a).wait()

    return start, finish


def _norm_proj(h, gain, w4, name, narrow):
    t = h.shape[0]
    tm = _row_tile(t, 256)

    def body(h_ref, g_ref, w_ref, y_ref, s0, s1, s2, s3):
        x = h_ref[...]
        r = lax.rsqrt(jnp.mean(x * x, axis=-1, keepdims=True) + EPS)
        y = (x * r * g_ref[...]).astype(_MXU)
        y_ref[...] = y
        for j, s in enumerate((s0, s1, s2, s3)):
            s[...] = jnp.dot(y, w_ref[j], preferred_element_type=F32).astype(s.dtype)

    row = pl.BlockSpec((tm, D_MODEL), lambda i: (i, 0))
    return pl.pallas_call(
        body, name=name, grid=(t // tm,),
        in_specs=[row, pl.BlockSpec((1, D_MODEL), lambda i: (0, 0)),
                  pl.BlockSpec((4, D_MODEL, D_MODEL), lambda i: (0, 0, 0))],
        out_specs=[row] * 5,
        out_shape=[jax.ShapeDtypeStruct((t, D_MODEL), _MXU)]
        + [jax.ShapeDtypeStruct((t, D_MODEL), _MXU if n else F32) for n in narrow],
        compiler_params=_params(("arbitrary",)),
    )(h, gain, w4)


LEVELS = (64, 32, 16, 8, 4, 2, 1)
HEAD_GROUP = 2


def _hgrn_tables():
    r = np.arange(BLOCK)
    mats = [r[None, :] <= r[:, None]]
    x = r[:, None] ^ r[None, :]
    lv = np.full((BLOCK, BLOCK), len(LEVELS), np.int32)
    for i, m in enumerate(LEVELS):
        lv[(x >= m) & (x < 2 * m)] = i
    return jnp.asarray(np.concatenate(mats, 0).astype(np.float32), dtype=_MXU), jnp.asarray(lv)


def _hgrn_exponents(g, sums):
    b = _mm01_left(sums, g)
    row = _iota2((BLOCK, D_HEAD), 0)
    out = []
    for m in LEVELS:
        is_q = (row & m) != 0
        if m >= 4:
            grp = b.reshape(BLOCK // (2 * m), 2 * m, D_HEAD)
            ref = jnp.broadcast_to(grp[:, m - 1:m, :], grp.shape).reshape(BLOCK, D_HEAD)
            d = b - ref
            out.append(jnp.where(is_q, d, -d))
        elif m == 2:
            below, above = pltpu.roll(g, 1, axis=0), pltpu.roll(g, BLOCK - 1, axis=0)
            low = row & 3
            out.append(jnp.where(low == 3, g + below, jnp.where(low == 2, g, jnp.where(low == 0, above, 0.0))))
        else:
            out.append(jnp.where(is_q, g, 0.0))
    return b, out


def _hgrn_gates(fz, lam, chunk):
    pos = chunk * BLOCK + _iota2((BLOCK, D_HEAD), 0)
    live = pos >= N_PAD
    sg = _sigmoid(fz)
    f = lam + (1.0 - lam) * sg
    g = jnp.where(live, jnp.log(f), 0.0)
    k = jnp.where(live, (1.0 - lam) * (1.0 - sg), 0.0)
    return sg, f, g, k, live


def _level_operand(q, k, exponent, m):
    decay = jnp.exp(exponent)
    is_q = (_iota2((BLOCK, D_HEAD), 0) & m) != 0
    return is_q, decay, (jnp.where(is_q, q, k) * decay).astype(_MXU)


def _hgrn_fwd(qs, fs, vs, lam, bsz, nb, shards):
    t = qs.shape[0]
    sums, levels = _hgrn_tables()
    width = HEAD_GROUP * D_HEAD

    n_sh = len(shards)

    def body(q_ref, f_ref, v_ref, lam_ref, sums_ref, lv_ref, *rest):
        own, (o_ref, sst_ref, sym_ref), rest = rest[:n_sh], rest[n_sh:n_sh + 3], rest[n_sh + 3:]
        gathered, st_scr, sems = rest[:n_sh], rest[n_sh], rest[n_sh + 1:]
        n = pl.program_id(2)
        first, last = _grid_ends(3)
        start, finish = _chip_exchange(own, gathered, sems, slotted=False)
        pl.when(first)(start)

        @pl.when(n == 0)
        def _():
            st_scr[...] = jnp.zeros_like(st_scr)

        lv = lv_ref[...]
        r, c = _iota2((BLOCK, BLOCK), 0), _iota2((BLOCK, BLOCK), 1)
        for hh in range(HEAD_GROUP):
            ls = slice(hh * D_HEAD, (hh + 1) * D_HEAD)
            st = st_scr[hh]
            sst_ref[0, hh, 0] = st
            q, v = q_ref[:, ls], v_ref[:, ls]
            _, _, g, k, _ = _hgrn_gates(f_ref[:, ls], lam_ref[:, ls], n)
            b, exps = _hgrn_exponents(g, sums_ref[...])
            sym = jnp.zeros((BLOCK, BLOCK), F32)
            for li, m in enumerate(LEVELS):
                _, _, x16 = _level_operand(q, k, exps[li], m)
                sym = jnp.where(lv == li, lax.dot_general(x16, x16, _NT, preferred_element_type=F32), sym)
            sym = jnp.where(c == r, jnp.sum(q * k, axis=1, keepdims=True), sym).astype(_MXU)
            sym_ref[0, hh, 0] = sym
            o_ref[:, ls] = _mm_nt(q * jnp.exp(b), st) + _mm(jnp.where(c <= r, sym, 0), v)
            b_end = b[BLOCK - 1:BLOCK, :]
            st_scr[hh] = st * jnp.exp(b_end) + _mm_tn(v, k * jnp.exp(b_end - b))
        pl.when(last)(finish)

    blk = pl.BlockSpec((BLOCK, width), lambda b, h, n: (b * nb + n, h))
    hbm = pl.BlockSpec(memory_space=pl.ANY)
    return pl.pallas_call(
        body, name="hgrn_fwd", grid=(bsz, N_HEADS // HEAD_GROUP, nb),
        in_specs=[blk, blk, blk, pl.BlockSpec((1, width), lambda b, h, n: (0, h)),
                  pl.BlockSpec(sums.shape, lambda b, h, n: (0, 0)), pl.BlockSpec(levels.shape, lambda b, h, n: (0, 0))]
        + [hbm] * n_sh,
        out_specs=[blk] + [pl.BlockSpec((1, HEAD_GROUP, 1, D_HEAD, D_HEAD), lambda b, h, n: (b, h, n, 0, 0))] * 2
        + [hbm] * n_sh,
        out_shape=[jax.ShapeDtypeStruct((t, D_MODEL), F32),
                   jax.ShapeDtypeStruct((bsz, N_HEADS, nb, D_HEAD, D_HEAD), F32),
                   jax.ShapeDtypeStruct((bsz, N_HEADS, nb, D_HEAD, D_HEAD), _MXU)]
        + [jax.ShapeDtypeStruct((N_CHIPS,) + a.shape, a.dtype) for a in shards],
        scratch_shapes=[pltpu.VMEM((HEAD_GROUP, D_HEAD, D_HEAD), F32)] + _exchange_scratch(n_sh),
        compiler_params=_params(("arbitrary", "arbitrary", "arbitrary")),
    )(qs, fs, vs, lam, sums, levels, *shards)


def _hgrn_bwd(qs, fs, vs, lam, sst, sym, do, bsz, nb, outgoing):
    t = qs.shape[0]
    sums, levels = _hgrn_tables()
    width = HEAD_GROUP * D_HEAD

    n_out = len(outgoing)

    def body(q_ref, f_ref, v_ref, lam_ref, sst_ref, sym_ref, do_ref, sums_ref, lv_ref, *rest):
        send, (dq_ref, df_ref, dv_ref, dlam_ref), rest = rest[:n_out], rest[n_out:n_out + 4], rest[n_out + 4:]
        landed, dst_scr, gsum_scr, sems = rest[:n_out], rest[n_out], rest[n_out + 1], rest[n_out + 2:]
        n = pl.program_id(2)
        chunk = nb - 1 - n
        first, last = _grid_ends(3)
        start, finish = _chip_exchange(send, landed, sems, slotted=True)
        pl.when(first)(start)

        @pl.when(n == 0)
        def _():
            dst_scr[...] = jnp.zeros_like(dst_scr)
            gsum_scr[...] = jnp.zeros_like(gsum_scr)
            dlam_ref[...] = jnp.zeros_like(dlam_ref)

        lv = lv_ref[...]
        r, c = _iota2((BLOCK, BLOCK), 0), _iota2((BLOCK, BLOCK), 1)
        for hh in range(HEAD_GROUP):
            ls = slice(hh * D_HEAD, (hh + 1) * D_HEAD)
            lam = lam_ref[:, ls]
            q, v, do = q_ref[:, ls], v_ref[:, ls], do_ref[:, ls]
            sg, f, g, k, live = _hgrn_gates(f_ref[:, ls], lam, chunk)
            b, exps = _hgrn_exponents(g, sums_ref[...])
            do16, v16 = do.astype(_MXU), v.astype(_MXU)
            da = lax.dot_general(do16, v16, _NT, preferred_element_type=F32)
            da_sym = jnp.where(c < r, da, da.T)
            dq = jnp.zeros((BLOCK, D_HEAD), F32)
            dqk = jnp.zeros((BLOCK, D_HEAD), F32)
            db_q = jnp.zeros((BLOCK, D_HEAD), F32)
            db_qk = jnp.zeros((BLOCK, D_HEAD), F32)
            for li, m in enumerate(LEVELS):
                is_q, decay, x16 = _level_operand(q, k, exps[li], m)
                y = jnp.dot(jnp.where(lv == li, da_sym, 0.0).astype(_MXU), x16, preferred_element_type=F32)
                dx = y * decay
                dq = dq + jnp.where(is_q, dx, 0.0)
                dqk = dqk + dx
                p = x16.astype(F32) * y
                db_q = db_q + jnp.where(is_q, p, 0.0)
                db_qk = db_qk + p
            dk = dqk - dq
            db = 2.0 * db_q - db_qk
            a_t = jnp.where(c >= r, sym_ref[0, hh, 0], 0)
            st, dst = sst_ref[0, hh, 0], dst_scr[hh]
            eb = jnp.exp(b)
            b_end = b[BLOCK - 1:BLOCK, :]
            dec = jnp.exp(b_end - b)
            qh16, kt16 = (q * eb).astype(_MXU), (k * dec).astype(_MXU)
            dq_st = _mm_s(do16, st)
            dk_st = _mm_s(v16, dst)
            d_diag = jnp.sum(do * v, axis=1, keepdims=True)
            dq_ref[:, ls] = (dq + d_diag * k + eb * dq_st).astype(dq_ref.dtype)
            dk = dk + d_diag * q + dec * dk_st
            dv_ref[:, ls] = (jnp.dot(a_t, do16, preferred_element_type=F32) + _mm_nt_s(kt16, dst)).astype(dv_ref.dtype)
            dst_scr[hh] = dst * jnp.exp(b_end) + lax.dot_general(do16, qh16, _TN, preferred_element_type=F32)
            db = db + (qh16.astype(F32) * dq_st - kt16.astype(F32) * dk_st)
            dg = _mm01_left((c >= r).astype(_MXU), db) + gsum_scr[:, ls]
            gsum_scr[:, ls] = gsum_scr[:, ls] + jnp.sum(db, axis=0, keepdims=True)
            slope = (1.0 - lam) * sg * (1.0 - sg)
            df_ref[:, ls] = jnp.where(live, dg * slope / f - dk * slope, 0.0).astype(df_ref.dtype)
            dl = jnp.where(live, (dg / f - dk) * (1.0 - sg), 0.0)
            dlam_ref[0, :, ls] = dlam_ref[0, :, ls] + jnp.sum(dl, axis=0, keepdims=True)
        pl.when(last)(finish)

    blk = pl.BlockSpec((BLOCK, width), lambda b, h, n: (b * nb + nb - 1 - n, h))
    hbm = pl.BlockSpec(memory_space=pl.ANY)
    return pl.pallas_call(
        body, name="hgrn_bwd", grid=(bsz, N_HEADS // HEAD_GROUP, nb),
        in_specs=[blk, blk, blk, pl.BlockSpec((1, width), lambda b, h, n: (0, h)),
                  pl.BlockSpec((1, HEAD_GROUP, 1, D_HEAD, D_HEAD), lambda b, h, n: (b, h, nb - 1 - n, 0, 0)),
                  pl.BlockSpec((1, HEAD_GROUP, 1, D_HEAD, D_HEAD), lambda b, h, n: (b, h, nb - 1 - n, 0, 0)),
                  blk, pl.BlockSpec(sums.shape, lambda b, h, n: (0, 0)), pl.BlockSpec(levels.shape, lambda b, h, n: (0, 0))]
        + [hbm] * n_out,
        out_specs=[blk, blk, blk, pl.BlockSpec((1, 1, width), lambda b, h, n: (b, 0, h))] + [hbm] * n_out,
        out_shape=[jax.ShapeDtypeStruct((t, D_MODEL), _MXU)] * 3 + [jax.ShapeDtypeStruct((bsz, 1, D_MODEL), F32)]
        + [jax.ShapeDtypeStruct(a.shape, a.dtype) for a in outgoing],
        scratch_shapes=[pltpu.VMEM((HEAD_GROUP, D_HEAD, D_HEAD), F32), pltpu.VMEM((1, width), F32)]
        + _exchange_scratch(n_out),
        compiler_params=_params(("arbitrary", "arbitrary", "arbitrary")),
    )(qs, fs, vs, lam, sst, sym, do, sums, levels, *outgoing)


def _sb_valid(ahead, col, i, j):
    return (ahead < (i - j) * TILE) & (col >= N_PAD - j * TILE)


def _sb_logits(q16, k_blk, valid):
    z = lax.dot_general(q16, k_blk.astype(_MXU), _NT, preferred_element_type=F32) * SB_SCALE
    softplus = jnp.maximum(z, 0.0) + jnp.log(1.0 + jnp.exp(-jnp.abs(z)))
    return jnp.where(valid, -softplus, 0.0), z - softplus


def _sb_fwd(qs, ks, vs, bsz, nq):
    t = qs.shape[0]
    lp = nq * TILE
    width = HEAD_GROUP * D_HEAD
    groups = N_HEADS // HEAD_GROUP
    lanes = [slice(hh * D_HEAD, (hh + 1) * D_HEAD) for hh in range(HEAD_GROUP)]

    def body(q_ref, k_ref, v_ref, o_ref, c_ref, n_ref):
        b, h, i = pl.program_id(0), pl.program_id(1), pl.program_id(2)
        q16 = [q_ref[:, ls].astype(_MXU) for ls in lanes]
        r, c = _iota2((TILE, TILE), 0), _iota2((TILE, TILE), 1)
        after = (r > c).astype(_MXU)

        def more(carry):
            jj, _, _, top = carry
            return (jj <= i) & (top > UNDERFLOW)

        def step(carry):
            jj, accs, sums, _ = carry
            j = i - jj
            ks_ = pl.ds(pl.multiple_of(j * TILE, TILE), TILE)
            valid = _sb_valid(c - r, c, i, j)
            new_accs, new_sums = [], []
            for hh, ls in enumerate(lanes):
                keep, log_beta = _sb_logits(q16[hh], k_ref[ks_, ls], valid)
                after_s = _mm01_right(keep, after)
                a = jnp.where(valid, jnp.exp(log_beta + (sums[hh] + after_s)), 0.0)
                new_accs.append(accs[hh] + _mm(a, v_ref[ks_, ls]))
                new_sums.append(sums[hh] + (after_s[:, 0:1] + keep[:, 0:1]))
            top = functools.reduce(jnp.maximum, [jnp.max(x) for x in new_sums])
            return jj + 1, tuple(new_accs), tuple(new_sums), top

        init = (jnp.int32(0), tuple(jnp.zeros((TILE, D_HEAD), F32) for _ in lanes),
                tuple(jnp.zeros((TILE, 1), F32) for _ in lanes), jnp.float32(0.0))
        visited, accs, sums, _ = lax.while_loop(more, step, init)
        for hh, ls in enumerate(lanes):
            o_ref[:, ls] = accs[hh]
            c_ref[:, ls] = jnp.broadcast_to(sums[hh], (TILE, D_HEAD))
        n_ref[(b * groups + h) * nq + i] = visited.astype(F32)

    blk = pl.BlockSpec((TILE, width), lambda b, h, i: (b * nq + i, h))
    seq = pl.BlockSpec((lp, width), lambda b, h, i: (b, h))
    return pl.pallas_call(
        body, name="sb_fwd", grid=(bsz, groups, nq),
        in_specs=[blk, seq, seq], out_specs=[blk, blk, pl.BlockSpec(memory_space=pltpu.SMEM)],
        out_shape=[jax.ShapeDtypeStruct((t, D_MODEL), F32)] * 2 + [jax.ShapeDtypeStruct((bsz * groups * nq,), F32)],
        compiler_params=_params(("arbitrary", "arbitrary", "arbitrary")),
    )(qs, ks, vs)


def _sb_bwd(qs, ks, vs, ctot, visited, do, bsz, nq):
    t = qs.shape[0]
    lp = nq * TILE
    width = HEAD_GROUP * D_HEAD
    groups = N_HEADS // HEAD_GROUP
    lanes = [slice(hh * D_HEAD, (hh + 1) * D_HEAD) for hh in range(HEAD_GROUP)]

    def body(n_ref, q_ref, k_ref, v_ref, c_ref, do_ref, dq_ref, dk_ref, dv_ref, dk_acc, dv_acc):
        b, h, i = pl.program_id(0), pl.program_id(1), pl.program_id(2)

        @pl.when(i == 0)
        def _():
            dk_acc[...] = jnp.zeros_like(dk_acc)
            dv_acc[...] = jnp.zeros_like(dv_acc)

        q16 = [q_ref[:, ls].astype(_MXU) for ls in lanes]
        do16 = [do_ref[:, ls].astype(_MXU) for ls in lanes]
        totals = [c_ref[:, hh * D_HEAD:hh * D_HEAD + 1] for hh in range(HEAD_GROUP)]
        r, c = _iota2((TILE, TILE), 0), _iota2((TILE, TILE), 1)
        upto = (r <= c).astype(_MXU)
        before = (r < c).astype(_MXU)
        first = jnp.maximum(i + 1 - n_ref[(b * groups + h) * nq + i].astype(jnp.int32), 0)

        def step(j, carry):
            ks_ = pl.ds(pl.multiple_of(j * TILE, TILE), TILE)
            valid = _sb_valid(c - r, c, i, j)
            out = []
            for hh, ls in enumerate(lanes):
                dq, keep_pre, g_pre = carry[hh]
                k_blk, v_blk = k_ref[ks_, ls], v_ref[ks_, ls]
                keep, log_beta = _sb_logits(q16[hh], k_blk, valid)
                keep_upto = _mm01_right(keep, upto)
                a = jnp.where(valid, jnp.exp(log_beta + (totals[hh] - keep_pre - keep_upto)), 0.0)
                da = lax.dot_general(do16[hh], v_blk.astype(_MXU), _NT, preferred_element_type=F32)
                g = a * da
                g_inside = _mm01_right(g, before)
                g_before = g_pre + g_inside
                beta = jnp.exp(log_beta)
                dz = jnp.where(valid, g * (1.0 - beta) - beta * g_before, 0.0) * SB_SCALE
                dz16 = dz.astype(_MXU)
                dq = dq + jnp.dot(dz16, k_blk.astype(_MXU), preferred_element_type=F32)
                dk_acc[ks_, ls] += lax.dot_general(dz16, q16[hh], _TN, preferred_element_type=F32)
                dv_acc[ks_, ls] += lax.dot_general(a.astype(_MXU), do16[hh], _TN, preferred_element_type=F32)
                out.append((dq, keep_pre + keep_upto[:, TILE - 1:TILE],
                            g_pre + (g_inside[:, TILE - 1:TILE] + g[:, TILE - 1:TILE])))
            return tuple(out)

        zero_col = jnp.zeros((TILE, 1), F32)
        init = tuple((jnp.zeros((TILE, D_HEAD), F32), zero_col, zero_col) for _ in lanes)
        res = lax.fori_loop(first, i + 1, step, init)
        for hh, ls in enumerate(lanes):
            dq_ref[:, ls] = res[hh][0].astype(dq_ref.dtype)

        @pl.when(i == nq - 1)
        def _():
            dk_ref[...] = dk_acc[...].astype(dk_ref.dtype)
            dv_ref[...] = dv_acc[...].astype(dv_ref.dtype)

    blk = pl.BlockSpec((TILE, width), lambda b, h, i: (b * nq + i, h))
    seq = pl.BlockSpec((lp, width), lambda b, h, i: (b, h))
    return pl.pallas_call(
        body, name="sb_bwd", grid=(bsz, groups, nq),
        in_specs=[pl.BlockSpec(memory_space=pltpu.SMEM), blk, seq, seq, blk, blk], out_specs=[blk, seq, seq],
        out_shape=[jax.ShapeDtypeStruct((t, D_MODEL), _MXU)] * 3,
        scratch_shapes=[pltpu.VMEM((lp, width), F32)] * 2,
        compiler_params=_params(("arbitrary", "arbitrary", "arbitrary")),
    )(visited, qs, ks, vs, ctot, do)


def _head_norm(o, head_gain):
    outs, rs = [], []
    for h in range(N_HEADS):
        oh = o[:, h * D_HEAD:(h + 1) * D_HEAD]
        r = lax.rsqrt(jnp.mean(oh * oh, axis=-1, keepdims=True) + EPS)
        outs.append(oh * r)
        rs.append(r)
    return outs, rs


def _mix(o, gate, head_gain):
    if head_gain is None:
        on = o
    else:
        outs, _ = _head_norm(o, head_gain)
        on = jnp.concatenate([x * head_gain for x in outs], axis=1)
    return on, on * (gate * _sigmoid(gate))


def _out_fwd(o, gate, h_in, w_out, post_gain, head_gain, name):
    t = o.shape[0]
    tm = _row_tile(t, 256)

    def body(o_ref, g_ref, h_ref, w_ref, pg_ref, hg_ref, ho_ref, u_ref):
        _, mix = _mix(o_ref[...], g_ref[...], hg_ref[...])
        u = jnp.dot(mix.astype(_MXU), w_ref[...], preferred_element_type=F32)
        u_ref[...] = u
        r = lax.rsqrt(jnp.mean(u * u, axis=-1, keepdims=True) + EPS)
        ho_ref[...] = h_ref[...] + u * r * pg_ref[...]

    row = pl.BlockSpec((tm, D_MODEL), lambda i: (i, 0))
    vec = pl.BlockSpec((1, D_MODEL), lambda i: (0, 0))
    return pl.pallas_call(
        body, name=name, grid=(t // tm,),
        in_specs=[row, row, row, pl.BlockSpec((D_MODEL, D_MODEL), lambda i: (0, 0)), vec,
                  pl.BlockSpec((1, D_HEAD), lambda i: (0, 0))],
        out_specs=[row, row], out_shape=[jax.ShapeDtypeStruct((t, D_MODEL), F32)] * 2,
        compiler_params=_params(("arbitrary",)),
    )(o, gate, h_in, w_out, post_gain, head_gain)


def _out_fwd_loss(o, gate, h_in, w_out, post_gain, target, nq, name):
    t = o.shape[0]

    def body(o_ref, g_ref, h_ref, w_ref, pg_ref, t_ref, dh_ref, u_ref, l_ref):
        i = pl.program_id(0)

        @pl.when(i == 0)
        def _():
            l_ref[...] = jnp.zeros_like(l_ref)

        _, mix = _mix(o_ref[...], g_ref[...], None)
        u = jnp.dot(mix.astype(_MXU), w_ref[...], preferred_element_type=F32)
        u_ref[...] = u

        @pl.when(i % nq == 0)
        def _():
            dh_ref[...] = jnp.zeros_like(dh_ref)

        @pl.when(i % nq != 0)
        def _():
            r = lax.rsqrt(jnp.mean(u * u, axis=-1, keepdims=True) + EPS)
            e = h_ref[...] + u * r * pg_ref[...] - t_ref[...]
            dh_ref[...] = e * (1.0 / D_MODEL)
            l_ref[...] += jnp.sum(e * e) * (0.5 / D_MODEL)

    row = pl.BlockSpec((TILE, D_MODEL), lambda i: (i, 0))
    vec = pl.BlockSpec((1, D_MODEL), lambda i: (0, 0))
    return pl.pallas_call(
        body, name=name, grid=(t // TILE,),
        in_specs=[row, row, row, pl.BlockSpec((D_MODEL, D_MODEL), lambda i: (0, 0)), vec,
                  pl.BlockSpec((TILE, D_MODEL), lambda i: ((i // nq) * (nq - 1) + jnp.maximum(i % nq - 1, 0), 0))],
        out_specs=[row, row, pl.BlockSpec((8, 128), lambda i: (0, 0))],
        out_shape=[jax.ShapeDtypeStruct((t, D_MODEL), F32)] * 2 + [jax.ShapeDtypeStruct((8, 128), F32)],
        compiler_params=_params(("arbitrary",)),
    )(o, gate, h_in, w_out, post_gain, target)


def _out_bwd(dh, u, o, gate, w_out, post_gain, head_gain, name, narrow_do):
    t = o.shape[0]
    tm = _row_tile(t, 256)
    has_head = head_gain is not None

    def body(*refs):
        if has_head:
            dh_ref, u_ref, o_ref, g_ref, w_ref, pg_ref, hg_ref, do_ref, dg_ref, gw_ref, gw16_ref, gp_ref, gh_ref = refs
            hg = hg_ref[...]
        else:
            dh_ref, u_ref, o_ref, g_ref, w_ref, pg_ref, do_ref, dg_ref, gw_ref, gw16_ref, gp_ref = refs
            hg = None
        first = pl.program_id(0) == 0

        @pl.when(first)
        def _():
            gw_ref[...] = jnp.zeros_like(gw_ref)
            gp_ref[...] = jnp.zeros_like(gp_ref)
            if has_head:
                gh_ref[...] = jnp.zeros_like(gh_ref)

        dr, u, o, gate = dh_ref[...], u_ref[...], o_ref[...], g_ref[...]
        r = lax.rsqrt(jnp.mean(u * u, axis=-1, keepdims=True) + EPS)
        un = u * r
        gp_ref[...] += jnp.sum(dr * un, axis=0, keepdims=True)
        dun = dr * pg_ref[...]
        du = r * (dun - un * jnp.mean(dun * un, axis=-1, keepdims=True))
        on, mix = _mix(o, gate, hg)
        du16 = du.astype(_MXU)
        gw_ref[...] += lax.dot_general(mix.astype(_MXU), du16, _TN, preferred_element_type=F32)
        dmix = lax.dot_general(du16, w_ref[...], _NT, preferred_element_type=F32)
        sg = _sigmoid(gate)
        dg_ref[...] = (dmix * on * (sg * (1.0 + gate * (1.0 - sg)))).astype(dg_ref.dtype)
        don = dmix * (gate * sg)
        if has_head:
            outs, rs = _head_norm(o, hg)
            gh = jnp.zeros((1, D_HEAD), F32)
            cols = []
            for h in range(N_HEADS):
                dn = don[:, h * D_HEAD:(h + 1) * D_HEAD]
                gh = gh + jnp.sum(dn * outs[h], axis=0, keepdims=True)
                dnn = dn * hg
                cols.append(rs[h] * (dnn - outs[h] * jnp.mean(dnn * outs[h], axis=-1, keepdims=True)))
            gh_ref[...] += gh
            do_ref[...] = jnp.concatenate(cols, axis=1)
        else:
            do_ref[...] = don.astype(do_ref.dtype)

        @pl.when(pl.program_id(0) == pl.num_programs(0) - 1)
        def _():
            gw16_ref[...] = gw_ref[...].astype(_MXU)

    row = pl.BlockSpec((tm, D_MODEL), lambda i: (i, 0))
    vec = pl.BlockSpec((1, D_MODEL), lambda i: (0, 0))
    mat = pl.BlockSpec((D_MODEL, D_MODEL), lambda i: (0, 0))
    in_specs = [row, row, row, row, mat, vec]
    args = [dh, u, o, gate, w_out, post_gain]
    out_specs = [row, row, mat, mat, vec]
    out_shape = [jax.ShapeDtypeStruct((t, D_MODEL), _MXU if narrow_do else F32),
                 jax.ShapeDtypeStruct((t, D_MODEL), _MXU)] + [jax.ShapeDtypeStruct((D_MODEL, D_MODEL), F32),
                                                                  jax.ShapeDtypeStruct((D_MODEL, D_MODEL), _MXU),
                                                                  jax.ShapeDtypeStruct((1, D_MODEL), F32)]
    if has_head:
        in_specs.append(pl.BlockSpec((1, D_HEAD), lambda i: (0, 0)))
        args.append(head_gain)
        out_specs.append(pl.BlockSpec((1, D_HEAD), lambda i: (0, 0)))
        out_shape.append(jax.ShapeDtypeStruct((1, D_HEAD), F32))
    return pl.pallas_call(
        body, name=name, grid=(t // tm,), in_specs=in_specs, out_specs=out_specs, out_shape=out_shape,
        compiler_params=_params(("arbitrary",)),
    )(*args)


def _proj_bwd(ds, w4, h_in, gain, dh_out, name, outgoing=()):
    t = h_in.shape[0]
    tm = _row_tile(t, 256)
    n_out = len(outgoing)

    def body(d0, d1, d2, d3, w_ref, h_ref, g_ref, dho_ref, *rest):
        send, (dhi_ref, gg_ref), rest = rest[:n_out], rest[n_out:n_out + 2], rest[n_out + 2:]
        landed, sems = rest[:n_out], rest[n_out:]
        if n_out:
            first, last = _grid_ends(1)
            start, finish = _chip_exchange(send, landed, sems, slotted=True)
            pl.when(first)(start)

        @pl.when(pl.program_id(0) == 0)
        def _():
            gg_ref[...] = jnp.zeros_like(gg_ref)

        dy = jnp.zeros((tm, D_MODEL), F32)
        for j, d in enumerate((d0, d1, d2, d3)):
            dy = dy + lax.dot_general(d[...].astype(_MXU), w_ref[j], _NT, preferred_element_type=F32)
        x = h_ref[...]
        r = lax.rsqrt(jnp.mean(x * x, axis=-1, keepdims=True) + EPS)
        xn = x * r
        gg_ref[...] += jnp.sum(dy * xn, axis=0, keepdims=True)
        dxn = dy * g_ref[...]
        dhi_ref[...] = dho_ref[...] + r * (dxn - xn * jnp.mean(dxn * xn, axis=-1, keepdims=True))
        if n_out:
            pl.when(last)(finish)

    row = pl.BlockSpec((tm, D_MODEL), lambda i: (i, 0))
    vec = pl.BlockSpec((1, D_MODEL), lambda i: (0, 0))
    hbm = pl.BlockSpec(memory_space=pl.ANY)
    return pl.pallas_call(
        body, name=name, grid=(t // tm,),
        in_specs=[row] * 4 + [pl.BlockSpec((4, D_MODEL, D_MODEL), lambda i: (0, 0, 0)), row, vec, row] + [hbm] * n_out,
        out_specs=[row, vec] + [hbm] * n_out,
        out_shape=[jax.ShapeDtypeStruct((t, D_MODEL), F32), jax.ShapeDtypeStruct((1, D_MODEL), F32)]
        + [jax.ShapeDtypeStruct(a.shape, a.dtype) for a in outgoing],
        scratch_shapes=_exchange_scratch(n_out) if n_out else [],
        compiler_params=_params(("arbitrary",)),
    )(*ds, w4, h_in, gain, dh_out, *outgoing)


def _weight_grad(y, d, name):
    t = y.shape[0]
    tk = _row_tile(t, 512)

    def body(y_ref, d_ref, g_ref, g16_ref):
        @pl.when(pl.program_id(0) == 0)
        def _():
            g_ref[...] = jnp.zeros_like(g_ref)

        g_ref[...] += lax.dot_general(y_ref[...], d_ref[...].astype(_MXU), _TN, preferred_element_type=F32)

        @pl.when(pl.program_id(0) == pl.num_programs(0) - 1)
        def _():
            g16_ref[...] = g_ref[...].astype(_MXU)

    row = pl.BlockSpec((tk, D_MODEL), lambda i: (i, 0))
    mat = pl.BlockSpec((D_MODEL, D_MODEL), lambda i: (0, 0))
    return pl.pallas_call(
        body, name=name, grid=(t // tk,), in_specs=[row, row], out_specs=[mat, mat],
        out_shape=[jax.ShapeDtypeStruct((D_MODEL, D_MODEL), F32), jax.ShapeDtypeStruct((D_MODEL, D_MODEL), _MXU)],
        compiler_params=_params(("arbitrary",)),
    )(y, d)


def _local_step(x, target, meta, pre_norm, post_norm, lam, head_gain, hw_in, hw_out, sb_shards):
    bsz, seq, _ = x.shape
    nq = seq // TILE + 1
    nb = nq * (TILE // BLOCK)
    lp = nq * TILE
    t = bsz * lp
    d4 = D_MODEL // N_CHIPS
    front = jnp.concatenate([jnp.zeros((N_PAD, D_MODEL), F32), meta], axis=0)
    h0 = jnp.concatenate([jnp.broadcast_to(front[None], (bsz, TILE, D_MODEL)), x], axis=1).reshape(t, D_MODEL)
    pre0, pre1, post0, post1 = pre_norm[0:1], pre_norm[1:2], post_norm[0:1], post_norm[1:2]

    y0, q0, f0, v0, g0 = _norm_proj(h0, pre0, hw_in, "norm_proj_hgrn", (False,) * 4)
    o0, sst, sym, sw_in, sw_out = _hgrn_fwd(q0, f0, v0, lam, bsz, nb, sb_shards)
    sw_out = sw_out.reshape(D_MODEL, D_MODEL)
    h1, u0 = _out_fwd(o0, g0, h0, hw_out, post0, head_gain, "out_fwd_hgrn")
    y1, q1, k1, v1, g1 = _norm_proj(h1, pre1, sw_in, "norm_proj_sb", (True, True, True, False))
    o1, ctot, visited = _sb_fwd(q1, k1, v1, bsz, nq)
    dh2, u1, loss_blk = _out_fwd_loss(o1, g1, h1, sw_out, post1, target.reshape(bsz * seq, D_MODEL), nq, "out_fwd_sb")

    do1, dg1, g_sw_out, g_sw_out16, g_post1 = _out_bwd(dh2, u1, o1, g1, sw_out, post1, None, "out_bwd_sb", True)
    dq1, dk1, dv1 = _sb_bwd(q1, k1, v1, ctot, visited, do1, bsz, nq)
    ds1 = (dq1, dk1, dv1, dg1)
    dh1, g_pre1 = _proj_bwd(ds1, sw_in, h1, pre1, dh2, "proj_bwd_sb")
    g_sw_in = [_weight_grad(y1, d, "wgrad_sb_%d" % j) for j, d in enumerate(ds1)]

    do0, dg0, g_hw_out, g_hw_out16, g_post0, g_head = _out_bwd(dh1, u0, o0, g0, hw_out, post0, head_gain, "out_bwd_hgrn",
                                                               False)
    sb_out = (jnp.stack([g16 for _, g16 in g_sw_in]), g_sw_out16.reshape(N_CHIPS, d4, D_MODEL))
    dq0, df0, dv0, dlam, land_sw_in, land_sw_out = _hgrn_bwd(q0, f0, v0, lam, sst, sym, do0, bsz, nb, sb_out)
    ds0 = (dq0, df0, dv0, dg0)
    g_hw_in = [_weight_grad(y0, d, "wgrad_hgrn_%d" % j) for j, d in enumerate(ds0)]
    hg_out = (jnp.stack([g16 for _, g16 in g_hw_in]), g_hw_out16.reshape(N_CHIPS, d4, D_MODEL))
    dh0, g_pre0, land_hw_in, land_hw_out = _proj_bwd(ds0, hw_in, h0, pre0, dh1, "proj_bwd_hgrn", hg_out)

    dh0 = dh0.reshape(bsz, lp, D_MODEL)
    grad_x = dh0[:, TILE:, :]
    g_meta = jnp.sum(dh0[:, N_PAD:TILE, :], axis=0)
    g_lam = jnp.sum(dlam, axis=0)
    small = jnp.concatenate([g_pre0, g_pre1, g_post0, g_post1, g_lam, g_lam,
                             jnp.pad(g_head, ((0, 0), (0, D_MODEL - D_HEAD))), g_meta,
                             jnp.zeros((SMALL_ROWS - 23, D_MODEL), F32)], axis=0)
    rows4 = lambda g: [g[j * d4:(j + 1) * d4] for j in range(N_CHIPS)]
    large = dict(hw_in=(land_hw_in, [g for g, _ in g_hw_in]), sw_in=(land_sw_in, [g for g, _ in g_sw_in]),
                 hw_out=(land_hw_out, rows4(g_hw_out)), sw_out=(land_sw_out, rows4(g_sw_out)))
    return loss_blk, grad_x, small, large


def _prep_weights(hw_in, sw_in, hw_out, sw_out, meta):
    def body(hi_ref, si_ref, ho_ref, so_ref, m_ref, ghi, gho, gm, si16, so16, far_send, far_recv, near_send, near_recv):
        x, y, c = _place()
        me = 2 * x + y
        ghi[me] = hi_ref[0].astype(_MXU)
        gho[me] = ho_ref[0].astype(_MXU)
        gm[me] = m_ref[...]
        si16[...] = si_ref[0].astype(_MXU)
        so16[...] = so_ref[0].astype(_MXU)
        outs = (ghi, gho, gm)
        peers = [(1 - x, y), (x, 1 - y), (1 - x, 1 - y)]

        def half(a, slot, which):
            rows = outs[a].shape[1] // 2
            return outs[a].at[slot, pl.ds(which * rows, rows), :]

        def far(r, a, slot):
            px, py = peers[r]
            return pltpu.make_async_remote_copy(
                src_ref=half(a, slot, c), dst_ref=half(a, slot, c), send_sem=far_send.at[r * 3 + a],
                recv_sem=far_recv.at[r * 3 + a], device_id=(px, py, c), device_id_type=MESH)

        def near(r, a, which):
            px, py = peers[r]
            return pltpu.make_async_remote_copy(
                src_ref=half(a, 2 * px + py, which), dst_ref=half(a, 2 * px + py, which),
                send_sem=near_send.at[r * 3 + a], recv_sem=near_recv.at[r * 3 + a],
                device_id=(x, y, 1 - c), device_id_type=MESH)

        for r in range(3):
            for a in range(3):
                far(r, a, me).start()
        for r, (px, py) in enumerate(peers):
            for a in range(3):
                far(r, a, 2 * px + py).wait_recv()
                near(r, a, c).start()
        for r in range(3):
            for a in range(3):
                near(r, a, 1 - c).wait_recv()
        for r in range(3):
            for a in range(3):
                far(r, a, me).wait_send()
                near(r, a, c).wait_send()

    d4 = D_MODEL // N_CHIPS
    vm = pl.BlockSpec(memory_space=pltpu.VMEM)
    return pl.pallas_call(
        body, name="prep_weights",
        in_specs=[vm] * 5, out_specs=[vm] * 5,
        out_shape=[jax.ShapeDtypeStruct((N_CHIPS, D_MODEL, D_MODEL), _MXU), jax.ShapeDtypeStruct((N_CHIPS, d4, D_MODEL), _MXU),
                   jax.ShapeDtypeStruct((N_CHIPS, N_META, d4), F32),
                   jax.ShapeDtypeStruct((D_MODEL, D_MODEL), _MXU), jax.ShapeDtypeStruct((d4, D_MODEL), _MXU)],
        scratch_shapes=[pltpu.SemaphoreType.DMA((9,))] * 4,
        compiler_params=pltpu.CompilerParams(vmem_limit_bytes=VMEM_LIMIT),
    )(hw_in, sw_in, hw_out, sw_out, meta)


def _scatter_small(small):
    def body(sm, lsm, send_sems, recv_sems, local_sem):
        x, y, c = _place()
        mine = 4 * x + 2 * y + c
        local = pltpu.make_async_copy(sm, lsm.at[mine], local_sem)
        local.start()

        def copy(rel, src_dev, to):
            return pltpu.make_async_remote_copy(
                src_ref=sm, dst_ref=lsm.at[src_dev], send_sem=send_sems.at[rel - 1], recv_sem=recv_sems.at[rel - 1],
                device_id=to, device_id_type=MESH)

        flip = lambda bit, v: 1 - v if bit else v
        rels = [(rel, flip(rel & 4, x), flip(rel & 2, y), flip(rel & 1, c)) for rel in range(1, N_DEV)]
        sends = [copy(rel, mine, (px, py, pc)) for rel, px, py, pc in rels]
        for cp in sends:
            cp.start()
        for rel, px, py, pc in rels:
            copy(rel, 4 * px + 2 * py + pc, (px, py, pc)).wait_recv()
        for cp in sends:
            cp.wait_send()
        local.wait()

    hbm = pl.BlockSpec(memory_space=pl.ANY)
    return pl.pallas_call(
        body, name="scatter_small", in_specs=[hbm], out_specs=hbm,
        out_shape=jax.ShapeDtypeStruct((N_DEV, SMALL_ROWS, D_MODEL), F32),
        scratch_shapes=[pltpu.SemaphoreType.DMA((N_DEV - 1,)), pltpu.SemaphoreType.DMA((N_DEV - 1,)),
                        pltpu.SemaphoreType.DMA(())],
    )(small)


def _sum_slots(landed, own, me, name):
    n, rows, _ = landed.shape
    tm = rows if rows < 256 else 256

    def body(me_ref, l_ref, o0, o1, o2, o3, out_ref):
        acc = None
        for k, o in enumerate((o0, o1, o2, o3)):
            term = jnp.where(me_ref[0] == k, o[...], l_ref[k].astype(F32))
            acc = term if acc is None else acc + term
        out_ref[...] = acc

    blk = pl.BlockSpec((tm, D_MODEL), lambda i: (i, 0))
    return pl.pallas_call(
        body, name=name, grid=(rows // tm,),
        in_specs=[pl.BlockSpec(memory_space=pltpu.SMEM), pl.BlockSpec((n, tm, D_MODEL), lambda i: (0, i, 0))] + [blk] * 4,
        out_specs=blk, out_shape=jax.ShapeDtypeStruct((rows, D_MODEL), F32),
        compiler_params=_params(("arbitrary",)),
    )(me, landed, *own)


def _swap_with_sibling(parts):
    def body(a0, a1, a2, a3, b0, b1, b2, b3, send_sems, recv_sems):
        x, y, c = _place()
        copies = [pltpu.make_async_remote_copy(src_ref=s, dst_ref=d, send_sem=send_sems.at[a], recv_sem=recv_sems.at[a],
                                               device_id=(x, y, 1 - c), device_id_type=MESH)
                  for a, (s, d) in enumerate(zip((a0, a1, a2, a3), (b0, b1, b2, b3)))]
        for cp in copies:
            cp.start()
        for cp in copies:
            cp.wait()

    hbm = pl.BlockSpec(memory_space=pl.ANY)
    return pl.pallas_call(
        body, name="swap_with_sibling", in_specs=[hbm] * 4, out_specs=[hbm] * 4,
        out_shape=[jax.ShapeDtypeStruct(p.shape, F32) for p in parts],
        scratch_shapes=[pltpu.SemaphoreType.DMA((4,)), pltpu.SemaphoreType.DMA((4,))],
    )(*parts)


def _adamw_math(w, g, m, v):
    m = ADAM_B1 * m + (1.0 - ADAM_B1) * g
    v = ADAM_B2 * v + (1.0 - ADAM_B2) * (g * g)
    m_hat = m / (1.0 - ADAM_B1 ** ADAM_STEP)
    v_hat = v / (1.0 - ADAM_B2 ** ADAM_STEP)
    delta = -ADAM_LR * (m_hat / (jnp.sqrt(v_hat) + ADAM_EPS) + ADAM_WD * w)
    return delta, m, v


def _adamw(w, g_parts, m, v, name):
    rows, cols = w.shape
    tm = rows if rows < 256 else 256
    n = len(g_parts)

    def body(*refs):
        w_ref, m_ref, v_ref = refs[n:n + 3]
        g_ref, d_ref, nm_ref, nv_ref = refs[n + 3:]
        g = refs[0][...]
        for p in refs[1:n]:
            g = g + p[...]
        g_ref[...] = g
        d_ref[...], nm_ref[...], nv_ref[...] = _adamw_math(w_ref[...], g, m_ref[...], v_ref[...])

    blk = pl.BlockSpec((tm, cols), lambda i: (i, 0))
    return pl.pallas_call(
        body, name=name, grid=(rows // tm,), in_specs=[blk] * (n + 3), out_specs=[blk] * 4,
        out_shape=[jax.ShapeDtypeStruct((rows, cols), F32)] * 4,
        compiler_params=_params(("arbitrary",)),
    )(*g_parts, w, m, v)


def _lam_of(hgrn_lb):
    def body(lb_ref, o_ref):
        lb = lb_ref[...]
        e = jnp.exp(lb - jnp.max(lb, axis=0, keepdims=True))
        o_ref[...] = e[0:1, :] / jnp.sum(e, axis=0, keepdims=True)

    return pl.pallas_call(body, name="lam_of", out_shape=jax.ShapeDtypeStruct((1, D_MODEL), F32))(hgrn_lb)


def _small_grads(land_small, lam):
    def body(l_ref, lam_ref, o_ref):
        acc = l_ref[0]
        for k in range(1, N_DEV):
            acc = acc + l_ref[k]
        p = lam_ref[...]
        slope = p * (1.0 - p)
        row = _iota2((SMALL_ROWS, D_MODEL), 0)
        o_ref[...] = acc * jnp.where(row == 4, slope, jnp.where(row == 5, -slope, 1.0))

    return pl.pallas_call(body, name="small_grads",
                          out_shape=jax.ShapeDtypeStruct((SMALL_ROWS, D_MODEL), F32))(land_small, lam)


def kernel(x, meta_tokens, pre_norm, post_norm, hgrn_w_in, hgrn_lb, hgrn_out_norm, hgrn_w_out, sb_w_in, sb_w_out, loss_target, m_meta_tokens, m_pre_norm, m_post_norm, m_hgrn_w_in, m_hgrn_lb, m_hgrn_out_norm, m_hgrn_w_out, m_sb_w_in, m_sb_w_out, v_meta_tokens, v_pre_norm, v_post_norm, v_hgrn_w_in, v_hgrn_lb, v_hgrn_out_norm, v_hgrn_w_out, v_sb_w_in, v_sb_w_out):
    d4 = D_MODEL // N_CHIPS
    chip = 2 * lax.axis_index("x") + lax.axis_index("y")
    hw_in, hw_out, meta4, sw_in16, sw_out16 = _prep_weights(hgrn_w_in, sb_w_in, hgrn_w_out, sb_w_out, meta_tokens)
    meta = meta4.transpose(1, 0, 2).reshape(N_META, D_MODEL)
    lam = _lam_of(hgrn_lb)
    loss_blk, grad_x, small, large = _local_step(
        x, loss_target, meta, pre_norm, post_norm, lam, hgrn_out_norm,
        hw_in, hw_out.reshape(D_MODEL, D_MODEL), (sw_in16, sw_out16))
    loss = lax.psum(loss_blk[0, 0], ("x", "y", "c"))

    me = jnp.reshape(chip, (1,)).astype(jnp.int32)
    parts = [_sum_slots(*large[n], me, "sum_" + n) for n in ("hw_in", "sw_in", "hw_out", "sw_out")]
    sib = _swap_with_sibling(parts)
    small = _small_grads(_scatter_small(small), lam)

    res = {}
    res["hgrn_w_in"] = _adamw(hgrn_w_in[0], [parts[0], sib[0]], m_hgrn_w_in[0], v_hgrn_w_in[0], "adamw_hw_in")
    res["sb_w_in"] = _adamw(sb_w_in[0], [parts[1], sib[1]], m_sb_w_in[0], v_sb_w_in[0], "adamw_sw_in")
    res["hgrn_w_out"] = _adamw(hgrn_w_out[0], [parts[2], sib[2]], m_hgrn_w_out[0], v_hgrn_w_out[0], "adamw_hw_out")
    res["sb_w_out"] = _adamw(sb_w_out[0], [parts[3], sib[3]], m_sb_w_out[0], v_sb_w_out[0], "adamw_sw_out")
    res["pre_norm"] = _adamw(pre_norm, [small[0:2]], m_pre_norm, v_pre_norm, "adamw_pre")
    res["post_norm"] = _adamw(post_norm, [small[2:4]], m_post_norm, v_post_norm, "adamw_post")
    res["hgrn_lb"] = _adamw(hgrn_lb, [small[4:6]], m_hgrn_lb, v_hgrn_lb, "adamw_lb")
    res["hgrn_out_norm"] = _adamw(hgrn_out_norm, [small[6:7, :D_HEAD]], m_hgrn_out_norm, v_hgrn_out_norm, "adamw_head")
    g_meta = lax.dynamic_slice_in_dim(small[7:7 + N_META], chip * d4, d4, axis=1)
    res["meta_tokens"] = _adamw(meta_tokens, [g_meta], m_meta_tokens, v_meta_tokens, "adamw_meta")
    for n in ("hgrn_w_in", "hgrn_w_out", "sb_w_in", "sb_w_out"):
        res[n] = tuple(a[None] for a in res[n])
    order = ("meta_tokens", "pre_norm", "post_norm", "hgrn_w_in", "hgrn_lb", "hgrn_out_norm", "hgrn_w_out",
             "sb_w_in", "sb_w_out")
    return (loss, grad_x, *[res[n][0] for n in order], *[res[n][1] for n in order],
            *[res[n][2] for n in order], *[res[n][3] for n in order])
```

```python
import functools

import jax
import numpy as np
import jax.numpy as jnp
from jax import lax
from jax.experimental import pallas as pl
from jax.experimental.pallas import tpu as pltpu

F32 = jnp.float32
_MXU = jnp.bfloat16

D_MODEL = 1024
N_HEADS = 8
D_HEAD = 128
BLOCK = 128
N_META = 16
TILE = 256
N_PAD = TILE - N_META
UNDERFLOW = -105.0
EPS = 1e-6
SB_SCALE = D_HEAD ** -0.5
SOFTPLUS_LINEAR = 20.0
ADAM_LR, ADAM_B1, ADAM_B2, ADAM_EPS, ADAM_WD, ADAM_STEP = 0.001, 0.9, 0.999, 1e-08, 0.01, 10
N_CHIPS = 4
N_DEV = 8
SMALL_ROWS = 24
VMEM_LIMIT = 56 * 1024 * 1024
MESH = pl.DeviceIdType.MESH

_NT = (((1,), (1,)), ((), ()))
_TN = (((0,), (0,)), ((), ()))


def _mm(a, b):
    return jnp.dot(a.astype(_MXU), b.astype(_MXU), preferred_element_type=F32)


def _mm_nt(a, b):
    return lax.dot_general(a.astype(_MXU), b.astype(_MXU), _NT, preferred_element_type=F32)


def _mm_tn(a, b):
    return lax.dot_general(a.astype(_MXU), b.astype(_MXU), _TN, preferred_element_type=F32)


def _split2(x):
    hi = x.astype(_MXU)
    return hi, (x - hi.astype(F32)).astype(_MXU)


def _mm_s(a16, state):
    hi, lo = _split2(state)
    return jnp.dot(a16, hi, preferred_element_type=F32) + jnp.dot(a16, lo, preferred_element_type=F32)


def _mm_nt_s(a16, state):
    hi, lo = _split2(state)
    return (lax.dot_general(a16, hi, _NT, preferred_element_type=F32)
            + lax.dot_general(a16, lo, _NT, preferred_element_type=F32))


def _mm01_right(x, m01):
    return jnp.dot(x.astype(_MXU), m01, preferred_element_type=F32)


def _mm01_left(m01, x):
    hi, lo = _split2(x)
    return jnp.dot(m01, hi, preferred_element_type=F32) + jnp.dot(m01, lo, preferred_element_type=F32)


def _iota2(shape, dim):
    return lax.broadcasted_iota(jnp.int32, shape, dim)


def _row_tile(total, pref):
    t = pref
    while total % t:
        t -= BLOCK
    return t


def _params(sem, limit=VMEM_LIMIT):
    return pltpu.CompilerParams(dimension_semantics=sem, vmem_limit_bytes=limit)


def _sigmoid(x):
    return 1.0 / (1.0 + jnp.exp(-x))


def _grid_ends(ndim):
    first, last = True, True
    for d in range(ndim):
        first = first & (pl.program_id(d) == 0)
        last = last & (pl.program_id(d) == pl.num_programs(d) - 1)
    return first, last


def _place():
    return lax.axis_index("x"), lax.axis_index("y"), lax.axis_index("c")


def _exchange_scratch(n):
    return [pltpu.SemaphoreType.DMA((3 * n,)), pltpu.SemaphoreType.DMA((3 * n,)), pltpu.SemaphoreType.DMA((n,))]


def _chip_exchange(srcs, dsts, sems, slotted):
    send_sems, recv_sems, local_sems = sems
    x, y, c = _place()
    me = 2 * x + y
    peers = [(1 - x, y), (x, 1 - y), (1 - x, 1 - y)]
    n = len(dsts)

    def remote(r, a, sending):
        px, py = peers[r]
        p = 2 * px + py
        return pltpu.make_async_remote_copy(
            src_ref=srcs[a].at[p] if slotted else srcs[a], dst_ref=dsts[a].at[me if sending else p],
            send_sem=send_sems.at[r * n + a], recv_sem=recv_sems.at[r * n + a],
            device_id=(px, py, c), device_id_type=MESH)

    def local(a):
        return pltpu.make_async_copy(srcs[a].at[me] if slotted else srcs[a], dsts[a].at[me], local_sems.at[a])

    def start():
        for a in range(n):
            local(a).start()
        for r in range(3):
            for a in range(n):
                remote(r, a, True).start()

    def finish():
        for r in range(3):
            for a in range(n):
                remote(r, a, False).wait_recv()
        for r in range(3):
            for a in range(n):
                remote(r, a, True).wait_send()
        for a in range(n):
            local(a).wait()

    return start, finish


def _norm_proj(h, gain, w4, name, narrow):
    t = h.shape[0]
    tm = _row_tile(t, 256)

    def body(h_ref, g_ref, w_ref, y_ref, s0, s1, s2, s3):
        x = h_ref[...]
        r = lax.rsqrt(jnp.mean(x * x, axis=-1, keepdims=True) + EPS)
        y = (x * r * g_ref[...]).astype(_MXU)
        y_ref[...] = y
        for j, s in enumerate((s0, s1, s2, s3)):
            s[...] = jnp.dot(y, w_ref[j], preferred_element_type=F32).astype(s.dtype)

    row = pl.BlockSpec((tm, D_MODEL), lambda i: (i, 0))
    return pl.pallas_call(
        body, name=name, grid=(t // tm,),
        in_specs=[row, pl.BlockSpec((1, D_MODEL), lambda i: (0, 0)),
                  pl.BlockSpec((4, D_MODEL, D_MODEL), lambda i: (0, 0, 0))],
        out_specs=[row] * 5,
        out_shape=[jax.ShapeDtypeStruct((t, D_MODEL), _MXU)]
        + [jax.ShapeDtypeStruct((t, D_MODEL), _MXU if n else F32) for n in narrow],
        compiler_params=_params(("arbitrary",)),
    )(h, gain, w4)


LEVELS = (64, 32, 16, 8, 4, 2, 1)
HEAD_GROUP = 2


def _hgrn_tables():
    r = np.arange(BLOCK)
    mats = [r[None, :] <= r[:, None]]
    x = r[:, None] ^ r[None, :]
    lv = np.full((BLOCK, BLOCK), len(LEVELS), np.int32)
    for i, m in enumerate(LEVELS):
        lv[(x >= m) & (x < 2 * m)] = i
    return jnp.asarray(np.concatenate(mats, 0).astype(np.float32), dtype=_MXU), jnp.asarray(lv)


def _hgrn_exponents(g, sums):
    b = _mm01_left(sums, g)
    row = _iota2((BLOCK, D_HEAD), 0)
    out = []
    for m in LEVELS:
        is_q = (row & m) != 0
        if m >= 4:
            grp = b.reshape(BLOCK // (2 * m), 2 * m, D_HEAD)
            ref = jnp.broadcast_to(grp[:, m - 1:m, :], grp.shape).reshape(BLOCK, D_HEAD)
            d = b - ref
            out.append(jnp.where(is_q, d, -d))
        elif m == 2:
            below, above = pltpu.roll(g, 1, axis=0), pltpu.roll(g, BLOCK - 1, axis=0)
            low = row & 3
            out.append(jnp.where(low == 3, g + below, jnp.where(low == 2, g, jnp.where(low == 0, above, 0.0))))
        else:
            out.append(jnp.where(is_q, g, 0.0))
    return b, out


def _hgrn_gates(fz, lam, chunk):
    pos = chunk * BLOCK + _iota2((BLOCK, D_HEAD), 0)
    live = pos >= N_PAD
    sg = _sigmoid(fz)
    f = lam + (1.0 - lam) * sg
    g = jnp.where(live, jnp.log(f), 0.0)
    k = jnp.where(live, (1.0 - lam) * (1.0 - sg), 0.0)
    return sg, f, g, k, live


def _level_operand(q, k, exponent, m):
    decay = jnp.exp(exponent)
    is_q = (_iota2((BLOCK, D_HEAD), 0) & m) != 0
    return is_q, decay, (jnp.where(is_q, q, k) * decay).astype(_MXU)


def _hgrn_fwd(qs, fs, vs, lam, bsz, nb, shards):
    t = qs.shape[0]
    sums, levels = _hgrn_tables()
    width = HEAD_GROUP * D_HEAD

    n_sh = len(shards)

    def body(q_ref, f_ref, v_ref, lam_ref, sums_ref, lv_ref, *rest):
        own, (o_ref, sst_ref, sym_ref), rest = rest[:n_sh], rest[n_sh:n_sh + 3], rest[n_sh + 3:]
        gathered, st_scr, sems = rest[:n_sh], rest[n_sh], rest[n_sh + 1:]
        n = pl.program_id(2)
        first, last = _grid_ends(3)
        start, finish = _chip_exchange(own, gathered, sems, slotted=False)
        pl.when(first)(start)

        @pl.when(n == 0)
        def _():
            st_scr[...] = jnp.zeros_like(st_scr)

        lv = lv_ref[...]
        r, c = _iota2((BLOCK, BLOCK), 0), _iota2((BLOCK, BLOCK), 1)
        for hh in range(HEAD_GROUP):
            ls = slice(hh * D_HEAD, (hh + 1) * D_HEAD)
            st = st_scr[hh]
            sst_ref[0, hh, 0] = st
            q, v = q_ref[:, ls], v_ref[:, ls]
            _, _, g, k, _ = _hgrn_gates(f_ref[:, ls], lam_ref[:, ls], n)
            b, exps = _hgrn_exponents(g, sums_ref[...])
            sym = jnp.zeros((BLOCK, BLOCK), F32)
            for li, m in enumerate(LEVELS):
                _, _, x16 = _level_operand(q, k, exps[li], m)
                sym = jnp.where(lv == li, lax.dot_general(x16, x16, _NT, preferred_element_type=F32), sym)
            sym = jnp.where(c == r, jnp.sum(q * k, axis=1, keepdims=True), sym).astype(_MXU)
            sym_ref[0, hh, 0] = sym
            o_ref[:, ls] = _mm_nt(q * jnp.exp(b), st) + _mm(jnp.where(c <= r, sym, 0), v)
            b_end = b[BLOCK - 1:BLOCK, :]
            st_scr[hh] = st * jnp.exp(b_end) + _mm_tn(v, k * jnp.exp(b_end - b))
        pl.when(last)(finish)

    blk = pl.BlockSpec((BLOCK, width), lambda b, h, n: (b * nb + n, h))
    hbm = pl.BlockSpec(memory_space=pl.ANY)
    return pl.pallas_call(
        body, name="hgrn_fwd", grid=(bsz, N_HEADS // HEAD_GROUP, nb),
        in_specs=[blk, blk, blk, pl.BlockSpec((1, width), lambda b, h, n: (0, h)),
                  pl.BlockSpec(sums.shape, lambda b, h, n: (0, 0)), pl.BlockSpec(levels.shape, lambda b, h, n: (0, 0))]
        + [hbm] * n_sh,
        out_specs=[blk] + [pl.BlockSpec((1, HEAD_GROUP, 1, D_HEAD, D_HEAD), lambda b, h, n: (b, h, n, 0, 0))] * 2
        + [hbm] * n_sh,
        out_shape=[jax.ShapeDtypeStruct((t, D_MODEL), F32),
                   jax.ShapeDtypeStruct((bsz, N_HEADS, nb, D_HEAD, D_HEAD), F32),
                   jax.ShapeDtypeStruct((bsz, N_HEADS, nb, D_HEAD, D_HEAD), _MXU)]
        + [jax.ShapeDtypeStruct((N_CHIPS,) + a.shape, a.dtype) for a in shards],
        scratch_shapes=[pltpu.VMEM((HEAD_GROUP, D_HEAD, D_HEAD), F32)] + _exchange_scratch(n_sh),
        compiler_params=_params(("arbitrary", "arbitrary", "arbitrary")),
    )(qs, fs, vs, lam, sums, levels, *shards)


def _hgrn_bwd(qs, fs, vs, lam, sst, sym, do, bsz, nb, outgoing):
    t = qs.shape[0]
    sums, levels = _hgrn_tables()
    width = HEAD_GROUP * D_HEAD

    n_out = len(outgoing)

    def body(q_ref, f_ref, v_ref, lam_ref, sst_ref, sym_ref, do_ref, sums_ref, lv_ref, *rest):
        send, (dq_ref, df_ref, dv_ref, dlam_ref), rest = rest[:n_out], rest[n_out:n_out + 4], rest[n_out + 4:]
        landed, dst_scr, gsum_scr, sems = rest[:n_out], rest[n_out], rest[n_out + 1], rest[n_out + 2:]
        n = pl.program_id(2)
        chunk = nb - 1 - n
        first, last = _grid_ends(3)
        start, finish = _chip_exchange(send, landed, sems, slotted=True)
        pl.when(first)(start)

        @pl.when(n == 0)
        def _():
            dst_scr[...] = jnp.zeros_like(dst_scr)
            gsum_scr[...] = jnp.zeros_like(gsum_scr)
            dlam_ref[...] = jnp.zeros_like(dlam_ref)

        lv = lv_ref[...]
        r, c = _iota2((BLOCK, BLOCK), 0), _iota2((BLOCK, BLOCK), 1)
        for hh in range(HEAD_GROUP):
            ls = slice(hh * D_HEAD, (hh + 1) * D_HEAD)
            lam = lam_ref[:, ls]
            q, v, do = q_ref[:, ls], v_ref[:, ls], do_ref[:, ls]
            sg, f, g, k, live = _hgrn_gates(f_ref[:, ls], lam, chunk)
            b, exps = _hgrn_exponents(g, sums_ref[...])
            do16, v16 = do.astype(_MXU), v.astype(_MXU)
            da = lax.dot_general(do16, v16, _NT, preferred_element_type=F32)
            da_sym = jnp.where(c < r, da, da.T)
            dq = jnp.zeros((BLOCK, D_HEAD), F32)
            dqk = jnp.zeros((BLOCK, D_HEAD), F32)
            db_q = jnp.zeros((BLOCK, D_HEAD), F32)
            db_qk = jnp.zeros((BLOCK, D_HEAD), F32)
            for li, m in enumerate(LEVELS):
                is_q, decay, x16 = _level_operand(q, k, exps[li], m)
                y = jnp.dot(jnp.where(lv == li, da_sym, 0.0).astype(_MXU), x16, preferred_element_type=F32)
                dx = y * decay
                dq = dq + jnp.where(is_q, dx, 0.0)
                dqk = dqk + dx
                p = x16.astype(F32) * y
                db_q = db_q + jnp.where(is_q, p, 0.0)
                db_qk = db_qk + p
            dk = dqk - dq
            db = 2.0 * db_q - db_qk
            a_t = jnp.where(c >= r, sym_ref[0, hh, 0], 0)
            st, dst = sst_ref[0, hh, 0], dst_scr[hh]
            eb = jnp.exp(b)
            b_end = b[BLOCK - 1:BLOCK, :]
            dec = jnp.exp(b_end - b)
            qh16, kt16 = (q * eb).astype(_MXU), (k * dec).astype(_MXU)
            dq_st = _mm_s(do16, st)
            dk_st = _mm_s(v16, dst)
            d_diag = jnp.sum(do * v, axis=1, keepdims=True)
            dq_ref[:, ls] = (dq + d_diag * k + eb * dq_st).astype(dq_ref.dtype)
            dk = dk + d_diag * q + dec * dk_st
            dv_ref[:, ls] = (jnp.dot(a_t, do16, preferred_element_type=F32) + _mm_nt_s(kt16, dst)).astype(dv_ref.dtype)
            dst_scr[hh] = dst * jnp.exp(b_end) + lax.dot_general(do16, qh16, _TN, preferred_element_type=F32)
            db = db + (qh16.astype(F32) * dq_st - kt16.astype(F32) * dk_st)
            dg = _mm01_left((c >= r).astype(_MXU), db) + gsum_scr[:, ls]
            gsum_scr[:, ls] = gsum_scr[:, ls] + jnp.sum(db, axis=0, keepdims=True)
            slope = (1.0 - lam) * sg * (1.0 - sg)
            df_ref[:, ls] = jnp.where(live, dg * slope / f - dk * slope, 0.0).astype(df_ref.dtype)
            dl = jnp.where(live, (dg / f - dk) * (1.0 - sg), 0.0)
            dlam_ref[0, :, ls] = dlam_ref[0, :, ls] + jnp.sum(dl, axis=0, keepdims=True)
        pl.when(last)(finish)

    blk = pl.BlockSpec((BLOCK, width), lambda b, h, n: (b * nb + nb - 1 - n, h))
    hbm = pl.BlockSpec(memory_space=pl.ANY)
    return pl.pallas_call(
        body, name="hgrn_bwd", grid=(bsz, N_HEADS // HEAD_GROUP, nb),
        in_specs=[blk, blk, blk, pl.BlockSpec((1, width), lambda b, h, n: (0, h)),
                  pl.BlockSpec((1, HEAD_GROUP, 1, D_HEAD, D_HEAD), lambda b, h, n: (b, h, nb - 1 - n, 0, 0)),
                  pl.BlockSpec((1, HEAD_GROUP, 1, D_HEAD, D_HEAD), lambda b, h, n: (b, h, nb - 1 - n, 0, 0)),
                  blk, pl.BlockSpec(sums.shape, lambda b, h, n: (0, 0)), pl.BlockSpec(levels.shape, lambda b, h, n: (0, 0))]
        + [hbm] * n_out,
        out_specs=[blk, blk, blk, pl.BlockSpec((1, 1, width), lambda b, h, n: (b, 0, h))] + [hbm] * n_out,
        out_shape=[jax.ShapeDtypeStruct((t, D_MODEL), _MXU)] * 3 + [jax.ShapeDtypeStruct((bsz, 1, D_MODEL), F32)]
        + [jax.ShapeDtypeStruct(a.shape, a.dtype) for a in outgoing],
        scratch_shapes=[pltpu.VMEM((HEAD_GROUP, D_HEAD, D_HEAD), F32), pltpu.VMEM((1, width), F32)]
        + _exchange_scratch(n_out),
        compiler_params=_params(("arbitrary", "arbitrary", "arbitrary")),
    )(qs, fs, vs, lam, sst, sym, do, sums, levels, *outgoing)


def _sb_valid(ahead, col, i, j):
    return (ahead < (i - j) * TILE) & (col >= N_PAD - j * TILE)


def _sb_logits(q16, k_blk, valid):
    z = lax.dot_general(q16, k_blk.astype(_MXU), _NT, preferred_element_type=F32) * SB_SCALE
    softplus = jnp.where(z > SOFTPLUS_LINEAR, z, jnp.log(1.0 + jnp.exp(jnp.minimum(z, SOFTPLUS_LINEAR))))
    return jnp.where(valid, -softplus, 0.0), z - softplus


def _sb_fwd(qs, ks, vs, bsz, nq):
    t = qs.shape[0]
    lp = nq * TILE
    width = HEAD_GROUP * D_HEAD
    groups = N_HEADS // HEAD_GROUP
    lanes = [slice(hh * D_HEAD, (hh + 1) * D_HEAD) for hh in range(HEAD_GROUP)]

    def body(q_ref, k_ref, v_ref, o_ref, c_ref, n_ref):
        b, h, i = pl.program_id(0), pl.program_id(1), pl.program_id(2)
        q16 = [q_ref[:, ls].astype(_MXU) for ls in lanes]
        r, c = _iota2((TILE, TILE), 0), _iota2((TILE, TILE), 1)
        after = (r > c).astype(_MXU)

        def more(carry):
            jj, _, _, top = carry
            return (jj <= i) & (top > UNDERFLOW)

        def step(carry):
            jj, accs, sums, _ = carry
            j = i - jj
            ks_ = pl.ds(pl.multiple_of(j * TILE, TILE), TILE)
            valid = _sb_valid(c - r, c, i, j)
            new_accs, new_sums = [], []
            for hh, ls in enumerate(lanes):
                keep, log_beta = _sb_logits(q16[hh], k_ref[ks_, ls], valid)
                after_s = _mm01_right(keep, after)
                a = jnp.where(valid, jnp.exp(log_beta + (sums[hh] + after_s)), 0.0)
                new_accs.append(accs[hh] + _mm(a, v_ref[ks_, ls]))
                new_sums.append(sums[hh] + (after_s[:, 0:1] + keep[:, 0:1]))
            top = functools.reduce(jnp.maximum, [jnp.max(x) for x in new_sums])
            return jj + 1, tuple(new_accs), tuple(new_sums), top

        init = (jnp.int32(0), tuple(jnp.zeros((TILE, D_HEAD), F32) for _ in lanes),
                tuple(jnp.zeros((TILE, 1), F32) for _ in lanes), jnp.float32(0.0))
        visited, accs, sums, _ = lax.while_loop(more, step, init)
        for hh, ls in enumerate(lanes):
            o_ref[:, ls] = accs[hh]
            c_ref[:, ls] = jnp.broadcast_to(sums[hh], (TILE, D_HEAD))
        n_ref[(b * groups + h) * nq + i] = visited.astype(F32)

    blk = pl.BlockSpec((TILE, width), lambda b, h, i: (b * nq + i, h))
    seq = pl.BlockSpec((lp, width), lambda b, h, i: (b, h))
    return pl.pallas_call(
        body, name="sb_fwd", grid=(bsz, groups, nq),
        in_specs=[blk, seq, seq], out_specs=[blk, blk, pl.BlockSpec(memory_space=pltpu.SMEM)],
        out_shape=[jax.ShapeDtypeStruct((t, D_MODEL), F32)] * 2 + [jax.ShapeDtypeStruct((bsz * groups * nq,), F32)],
        compiler_params=_params(("arbitrary", "arbitrary", "arbitrary")),
    )(qs, ks, vs)


def _sb_bwd(qs, ks, vs, ctot, visited, do, bsz, nq):
    t = qs.shape[0]
    lp = nq * TILE
    width = HEAD_GROUP * D_HEAD
    groups = N_HEADS // HEAD_GROUP
    lanes = [slice(hh * D_HEAD, (hh + 1) * D_HEAD) for hh in range(HEAD_GROUP)]

    def body(n_ref, q_ref, k_ref, v_ref, c_ref, do_ref, dq_ref, dk_ref, dv_ref, dk_acc, dv_acc):
        b, h, i = pl.program_id(0), pl.program_id(1), pl.program_id(2)

        @pl.when(i == 0)
        def _():
            dk_acc[...] = jnp.zeros_like(dk_acc)
            dv_acc[...] = jnp.zeros_like(dv_acc)

        q16 = [q_ref[:, ls].astype(_MXU) for ls in lanes]
        do16 = [do_ref[:, ls].astype(_MXU) for ls in lanes]
        totals = [c_ref[:, hh * D_HEAD:hh * D_HEAD + 1] for hh in range(HEAD_GROUP)]
        r, c = _iota2((TILE, TILE), 0), _iota2((TILE, TILE), 1)
        upto = (r <= c).astype(_MXU)
        before = (r < c).astype(_MXU)
        first = jnp.maximum(i + 1 - n_ref[(b * groups + h) * nq + i].astype(jnp.int32), 0)

        def step(j, carry):
            ks_ = pl.ds(pl.multiple_of(j * TILE, TILE), TILE)
            valid = _sb_valid(c - r, c, i, j)
            out = []
            for hh, ls in enumerate(lanes):
                dq, keep_pre, g_pre = carry[hh]
                k_blk, v_blk = k_ref[ks_, ls], v_ref[ks_, ls]
                keep, log_beta = _sb_logits(q16[hh], k_blk, valid)
                keep_upto = _mm01_right(keep, upto)
                a = jnp.where(valid, jnp.exp(log_beta + (totals[hh] - keep_pre - keep_upto)), 0.0)
                da = lax.dot_general(do16[hh], v_blk.astype(_MXU), _NT, preferred_element_type=F32)
                g = a * da
                g_inside = _mm01_right(g, before)
                g_before = g_pre + g_inside
                beta = jnp.exp(log_beta)
                dz = jnp.where(valid, g * (1.0 - beta) - beta * g_before, 0.0) * SB_SCALE
                dz16 = dz.astype(_MXU)
                dq = dq + jnp.dot(dz16, k_blk.astype(_MXU), preferred_element_type=F32)
                dk_acc[ks_, ls] += lax.dot_general(dz16, q16[hh], _TN, preferred_element_type=F32)
                dv_acc[ks_, ls] += lax.dot_general(a.astype(_MXU), do16[hh], _TN, preferred_element_type=F32)
                out.append((dq, keep_pre + keep_upto[:, TILE - 1:TILE],
                            g_pre + (g_inside[:, TILE - 1:TILE] + g[:, TILE - 1:TILE])))
            return tuple(out)

        zero_col = jnp.zeros((TILE, 1), F32)
        init = tuple((jnp.zeros((TILE, D_HEAD), F32), zero_col, zero_col) for _ in lanes)
        res = lax.fori_loop(first, i + 1, step, init)
        for hh, ls in enumerate(lanes):
            dq_ref[:, ls] = res[hh][0].astype(dq_ref.dtype)

        @pl.when(i == nq - 1)
        def _():
            dk_ref[...] = dk_acc[...].astype(dk_ref.dtype)
            dv_ref[...] = dv_acc[...].astype(dv_ref.dtype)

    blk = pl.BlockSpec((TILE, width), lambda b, h, i: (b * nq + i, h))
    seq = pl.BlockSpec((lp, width), lambda b, h, i: (b, h))
    return pl.pallas_call(
        body, name="sb_bwd", grid=(bsz, groups, nq),
        in_specs=[pl.BlockSpec(memory_space=pltpu.SMEM), blk, seq, seq, blk, blk], out_specs=[blk, seq, seq],
        out_shape=[jax.ShapeDtypeStruct((t, D_MODEL), _MXU)] * 3,
        scratch_shapes=[pltpu.VMEM((lp, width), F32)] * 2,
        compiler_params=_params(("arbitrary", "arbitrary", "arbitrary")),
    )(visited, qs, ks, vs, ctot, do)


def _head_norm(o, head_gain):
    outs, rs = [], []
    for h in range(N_HEADS):
        oh = o[:, h * D_HEAD:(h + 1) * D_HEAD]
        r = lax.rsqrt(jnp.mean(oh * oh, axis=-1, keepdims=True) + EPS)
        outs.append(oh * r)
        rs.append(r)
    return outs, rs


def _mix(o, gate, head_gain):
    if head_gain is None:
        on = o
    else:
        outs, _ = _head_norm(o, head_gain)
        on = jnp.concatenate([x * head_gain for x in outs], axis=1)
    return on, on * (gate * _sigmoid(gate))


def _out_fwd(o, gate, h_in, w_out, post_gain, head_gain, name):
    t = o.shape[0]
    tm = _row_tile(t, 256)

    def body(o_ref, g_ref, h_ref, w_ref, pg_ref, hg_ref, ho_ref, u_ref):
        _, mix = _mix(o_ref[...], g_ref[...], hg_ref[...])
        u = jnp.dot(mix.astype(_MXU), w_ref[...], preferred_element_type=F32)
        u_ref[...] = u
        r = lax.rsqrt(jnp.mean(u * u, axis=-1, keepdims=True) + EPS)
        ho_ref[...] = h_ref[...] + u * r * pg_ref[...]

    row = pl.BlockSpec((tm, D_MODEL), lambda i: (i, 0))
    vec = pl.BlockSpec((1, D_MODEL), lambda i: (0, 0))
    return pl.pallas_call(
        body, name=name, grid=(t // tm,),
        in_specs=[row, row, row, pl.BlockSpec((D_MODEL, D_MODEL), lambda i: (0, 0)), vec,
                  pl.BlockSpec((1, D_HEAD), lambda i: (0, 0))],
        out_specs=[row, row], out_shape=[jax.ShapeDtypeStruct((t, D_MODEL), F32)] * 2,
        compiler_params=_params(("arbitrary",)),
    )(o, gate, h_in, w_out, post_gain, head_gain)


def _out_fwd_loss(o, gate, h_in, w_out, post_gain, target, nq, name):
    t = o.shape[0]

    def body(o_ref, g_ref, h_ref, w_ref, pg_ref, t_ref, dh_ref, u_ref, l_ref):
        i = pl.program_id(0)

        @pl.when(i == 0)
        def _():
            l_ref[...] = jnp.zeros_like(l_ref)

        _, mix = _mix(o_ref[...], g_ref[...], None)
        u = jnp.dot(mix.astype(_MXU), w_ref[...], preferred_element_type=F32)
        u_ref[...] = u

        @pl.when(i % nq == 0)
        def _():
            dh_ref[...] = jnp.zeros_like(dh_ref)

        @pl.when(i % nq != 0)
        def _():
            r = lax.rsqrt(jnp.mean(u * u, axis=-1, keepdims=True) + EPS)
            e = h_ref[...] + u * r * pg_ref[...] - t_ref[...]
            dh_ref[...] = e * (1.0 / D_MODEL)
            l_ref[...] += jnp.sum(e * e) * (0.5 / D_MODEL)

    row = pl.BlockSpec((TILE, D_MODEL), lambda i: (i, 0))
    vec = pl.BlockSpec((1, D_MODEL), lambda i: (0, 0))
    return pl.pallas_call(
        body, name=name, grid=(t // TILE,),
        in_specs=[row, row, row, pl.BlockSpec((D_MODEL, D_MODEL), lambda i: (0, 0)), vec,
                  pl.BlockSpec((TILE, D_MODEL), lambda i: ((i // nq) * (nq - 1) + jnp.maximum(i % nq - 1, 0), 0))],
        out_specs=[row, row, pl.BlockSpec((8, 128), lambda i: (0, 0))],
        out_shape=[jax.ShapeDtypeStruct((t, D_MODEL), F32)] * 2 + [jax.ShapeDtypeStruct((8, 128), F32)],
        compiler_params=_params(("arbitrary",)),
    )(o, gate, h_in, w_out, post_gain, target)


def _out_bwd(dh, u, o, gate, w_out, post_gain, head_gain, name, narrow_do):
    t = o.shape[0]
    tm = _row_tile(t, 512)
    has_head = head_gain is not None

    def body(*refs):
        if has_head:
            dh_ref, u_ref, o_ref, g_ref, w_ref, pg_ref, hg_ref, do_ref, dg_ref, gw_ref, gw16_ref, gp_ref, gh_ref = refs
            hg = hg_ref[...]
        else:
            dh_ref, u_ref, o_ref, g_ref, w_ref, pg_ref, do_ref, dg_ref, gw_ref, gw16_ref, gp_ref = refs
            hg = None
        first = pl.program_id(0) == 0

        @pl.when(first)
        def _():
            gw_ref[...] = jnp.zeros_like(gw_ref)
            gp_ref[...] = jnp.zeros_like(gp_ref)
            if has_head:
                gh_ref[...] = jnp.zeros_like(gh_ref)

        dr, u, o, gate = dh_ref[...], u_ref[...], o_ref[...], g_ref[...]
        r = lax.rsqrt(jnp.mean(u * u, axis=-1, keepdims=True) + EPS)
        un = u * r
        gp_ref[...] += jnp.sum(dr * un, axis=0, keepdims=True)
        dun = dr * pg_ref[...]
        du = r * (dun - un * jnp.mean(dun * un, axis=-1, keepdims=True))
        on, mix = _mix(o, gate, hg)
        du16 = du.astype(_MXU)
        gw_ref[...] += lax.dot_general(mix.astype(_MXU), du16, _TN, preferred_element_type=F32)
        dmix = lax.dot_general(du16, w_ref[...], _NT, preferred_element_type=F32)
        sg = _sigmoid(gate)
        dg_ref[...] = (dmix * on * (sg * (1.0 + gate * (1.0 - sg)))).astype(dg_ref.dtype)
        don = dmix * (gate * sg)
        if has_head:
            outs, rs = _head_norm(o, hg)
            gh = jnp.zeros((1, D_HEAD), F32)
            cols = []
            for h in range(N_HEADS):
                dn = don[:, h * D_HEAD:(h + 1) * D_HEAD]
                gh = gh + jnp.sum(dn * outs[h], axis=0, keepdims=True)
                dnn = dn * hg
                cols.append(rs[h] * (dnn - outs[h] * jnp.mean(dnn * outs[h], axis=-1, keepdims=True)))
            gh_ref[...] += gh
            do_ref[...] = jnp.concatenate(cols, axis=1)
        else:
            do_ref[...] = don.astype(do_ref.dtype)

        @pl.when(pl.program_id(0) == pl.num_programs(0) - 1)
        def _():
            gw16_ref[...] = gw_ref[...].astype(_MXU)

    row = pl.BlockSpec((tm, D_MODEL), lambda i: (i, 0))
    vec = pl.BlockSpec((1, D_MODEL), lambda i: (0, 0))
    mat = pl.BlockSpec((D_MODEL, D_MODEL), lambda i: (0, 0))
    in_specs = [row, row, row, row, mat, vec]
    args = [dh, u, o, gate, w_out, post_gain]
    out_specs = [row, row, mat, mat, vec]
    out_shape = [jax.ShapeDtypeStruct((t, D_MODEL), _MXU if narrow_do else F32),
                 jax.ShapeDtypeStruct((t, D_MODEL), _MXU)] + [jax.ShapeDtypeStruct((D_MODEL, D_MODEL), F32),
                                                                  jax.ShapeDtypeStruct((D_MODEL, D_MODEL), _MXU),
                                                                  jax.ShapeDtypeStruct((1, D_MODEL), F32)]
    if has_head:
        in_specs.append(pl.BlockSpec((1, D_HEAD), lambda i: (0, 0)))
        args.append(head_gain)
        out_specs.append(pl.BlockSpec((1, D_HEAD), lambda i: (0, 0)))
        out_shape.append(jax.ShapeDtypeStruct((1, D_HEAD), F32))
    return pl.pallas_call(
        body, name=name, grid=(t // tm,), in_specs=in_specs, out_specs=out_specs, out_shape=out_shape,
        compiler_params=_params(("arbitrary",)),
    )(*args)


def _proj_bwd(ds, w4, h_in, gain, dh_out, name, outgoing=()):
    t = h_in.shape[0]
    tm = _row_tile(t, 256)
    n_out = len(outgoing)

    def body(d0, d1, d2, d3, w_ref, h_ref, g_ref, dho_ref, *rest):
        send, (dhi_ref, gg_ref), rest = rest[:n_out], rest[n_out:n_out + 2], rest[n_out + 2:]
        landed, sems = rest[:n_out], rest[n_out:]
        if n_out:
            first, last = _grid_ends(1)
            start, finish = _chip_exchange(send, landed, sems, slotted=True)
            pl.when(first)(start)

        @pl.when(pl.program_id(0) == 0)
        def _():
            gg_ref[...] = jnp.zeros_like(gg_ref)

        dy = jnp.zeros((tm, D_MODEL), F32)
        for j, d in enumerate((d0, d1, d2, d3)):
            dy = dy + lax.dot_general(d[...].astype(_MXU), w_ref[j], _NT, preferred_element_type=F32)
        x = h_ref[...]
        r = lax.rsqrt(jnp.mean(x * x, axis=-1, keepdims=True) + EPS)
        xn = x * r
        gg_ref[...] += jnp.sum(dy * xn, axis=0, keepdims=True)
        dxn = dy * g_ref[...]
        dhi_ref[...] = dho_ref[...] + r * (dxn - xn * jnp.mean(dxn * xn, axis=-1, keepdims=True))
        if n_out:
            pl.when(last)(finish)

    row = pl.BlockSpec((tm, D_MODEL), lambda i: (i, 0))
    vec = pl.BlockSpec((1, D_MODEL), lambda i: (0, 0))
    hbm = pl.BlockSpec(memory_space=pl.ANY)
    return pl.pallas_call(
        body, name=name, grid=(t // tm,),
        in_specs=[row] * 4 + [pl.BlockSpec((4, D_MODEL, D_MODEL), lambda i: (0, 0, 0)), row, vec, row] + [hbm] * n_out,
        out_specs=[row, vec] + [hbm] * n_out,
        out_shape=[jax.ShapeDtypeStruct((t, D_MODEL), F32), jax.ShapeDtypeStruct((1, D_MODEL), F32)]
        + [jax.ShapeDtypeStruct(a.shape, a.dtype) for a in outgoing],
        scratch_shapes=_exchange_scratch(n_out) if n_out else [],
        compiler_params=_params(("arbitrary",)),
    )(*ds, w4, h_in, gain, dh_out, *outgoing)


def _weight_grad(y, d, name):
    t = y.shape[0]
    tk = _row_tile(t, 2176)

    def body(y_ref, d_ref, g_ref, g16_ref):
        @pl.when(pl.program_id(0) == 0)
        def _():
            g_ref[...] = jnp.zeros_like(g_ref)

        g_ref[...] += lax.dot_general(y_ref[...], d_ref[...].astype(_MXU), _TN, preferred_element_type=F32)

        @pl.when(pl.program_id(0) == pl.num_programs(0) - 1)
        def _():
            g16_ref[...] = g_ref[...].astype(_MXU)

    row = pl.BlockSpec((tk, D_MODEL), lambda i: (i, 0))
    mat = pl.BlockSpec((D_MODEL, D_MODEL), lambda i: (0, 0))
    return pl.pallas_call(
        body, name=name, grid=(t // tk,), in_specs=[row, row], out_specs=[mat, mat],
        out_shape=[jax.ShapeDtypeStruct((D_MODEL, D_MODEL), F32), jax.ShapeDtypeStruct((D_MODEL, D_MODEL), _MXU)],
        compiler_params=_params(("arbitrary",)),
    )(y, d)


def _local_step(x, target, meta, pre_norm, post_norm, lam, head_gain, hw_in, hw_out, sb_shards):
    bsz, seq, _ = x.shape
    nq = seq // TILE + 1
    nb = nq * (TILE // BLOCK)
    lp = nq * TILE
    t = bsz * lp
    d4 = D_MODEL // N_CHIPS
    front = jnp.concatenate([jnp.zeros((N_PAD, D_MODEL), F32), meta], axis=0)
    h0 = jnp.concatenate([jnp.broadcast_to(front[None], (bsz, TILE, D_MODEL)), x], axis=1).reshape(t, D_MODEL)
    pre0, pre1, post0, post1 = pre_norm[0:1], pre_norm[1:2], post_norm[0:1], post_norm[1:2]

    y0, q0, f0, v0, g0 = _norm_proj(h0, pre0, hw_in, "norm_proj_hgrn", (False,) * 4)
    o0, sst, sym, sw_in, sw_out = _hgrn_fwd(q0, f0, v0, lam, bsz, nb, sb_shards)
    sw_out = sw_out.reshape(D_MODEL, D_MODEL)
    h1, u0 = _out_fwd(o0, g0, h0, hw_out, post0, head_gain, "out_fwd_hgrn")
    y1, q1, k1, v1, g1 = _norm_proj(h1, pre1, sw_in, "norm_proj_sb", (True, True, True, False))
    o1, ctot, visited = _sb_fwd(q1, k1, v1, bsz, nq)
    dh2, u1, loss_blk = _out_fwd_loss(o1, g1, h1, sw_out, post1, target.reshape(bsz * seq, D_MODEL), nq, "out_fwd_sb")

    do1, dg1, g_sw_out, g_sw_out16, g_post1 = _out_bwd(dh2, u1, o1, g1, sw_out, post1, None, "out_bwd_sb", True)
    dq1, dk1, dv1 = _sb_bwd(q1, k1, v1, ctot, visited, do1, bsz, nq)
    ds1 = (dq1, dk1, dv1, dg1)
    dh1, g_pre1 = _proj_bwd(ds1, sw_in, h1, pre1, dh2, "proj_bwd_sb")
    g_sw_in = [_weight_grad(y1, d, "wgrad_sb_%d" % j) for j, d in enumerate(ds1)]

    do0, dg0, g_hw_out, g_hw_out16, g_post0, g_head = _out_bwd(dh1, u0, o0, g0, hw_out, post0, head_gain, "out_bwd_hgrn",
                                                               False)
    sb_out = (jnp.stack([g16 for _, g16 in g_sw_in]), g_sw_out16.reshape(N_CHIPS, d4, D_MODEL))
    dq0, df0, dv0, dlam, land_sw_in, land_sw_out = _hgrn_bwd(q0, f0, v0, lam, sst, sym, do0, bsz, nb, sb_out)
    ds0 = (dq0, df0, dv0, dg0)
    g_hw_in = [_weight_grad(y0, d, "wgrad_hgrn_%d" % j) for j, d in enumerate(ds0)]
    hg_out = (jnp.stack([g16 for _, g16 in g_hw_in]), g_hw_out16.reshape(N_CHIPS, d4, D_MODEL))
    dh0, g_pre0, land_hw_in, land_hw_out = _proj_bwd(ds0, hw_in, h0, pre0, dh1, "proj_bwd_hgrn", hg_out)

    dh0 = dh0.reshape(bsz, lp, D_MODEL)
    grad_x = dh0[:, TILE:, :]
    g_meta = jnp.sum(dh0[:, N_PAD:TILE, :], axis=0)
    g_lam = jnp.sum(dlam, axis=0)
    small = jnp.concatenate([g_pre0, g_pre1, g_post0, g_post1, g_lam, g_lam,
                             jnp.pad(g_head, ((0, 0), (0, D_MODEL - D_HEAD))), g_meta,
                             jnp.zeros((SMALL_ROWS - 23, D_MODEL), F32)], axis=0)
    rows4 = lambda g: [g[j * d4:(j + 1) * d4] for j in range(N_CHIPS)]
    large = dict(hw_in=(land_hw_in, [g for g, _ in g_hw_in]), sw_in=(land_sw_in, [g for g, _ in g_sw_in]),
                 hw_out=(land_hw_out, rows4(g_hw_out)), sw_out=(land_sw_out, rows4(g_sw_out)))
    return loss_blk, grad_x, small, large


def _prep_weights(hw_in, sw_in, hw_out, sw_out, meta):
    def body(hi_ref, si_ref, ho_ref, so_ref, m_ref, ghi, gho, gm, si16, so16, far_send, far_recv, near_send, near_recv):
        x, y, c = _place()
        me = 2 * x + y
        ghi[me] = hi_ref[0].astype(_MXU)
        gho[me] = ho_ref[0].astype(_MXU)
        gm[me] = m_ref[...]
        si16[...] = si_ref[0].astype(_MXU)
        so16[...] = so_ref[0].astype(_MXU)
        outs = (ghi, gho, gm)
        peers = [(1 - x, y), (x, 1 - y), (1 - x, 1 - y)]

        def half(a, slot, which):
            rows = outs[a].shape[1] // 2
            return outs[a].at[slot, pl.ds(which * rows, rows), :]

        def far(r, a, slot):
            px, py = peers[r]
            return pltpu.make_async_remote_copy(
                src_ref=half(a, slot, c), dst_ref=half(a, slot, c), send_sem=far_send.at[r * 3 + a],
                recv_sem=far_recv.at[r * 3 + a], device_id=(px, py, c), device_id_type=MESH)

        def near(r, a, which):
            px, py = peers[r]
            return pltpu.make_async_remote_copy(
                src_ref=half(a, 2 * px + py, which), dst_ref=half(a, 2 * px + py, which),
                send_sem=near_send.at[r * 3 + a], recv_sem=near_recv.at[r * 3 + a],
                device_id=(x, y, 1 - c), device_id_type=MESH)

        for r in range(3):
            for a in range(3):
                far(r, a, me).start()
        for r, (px, py) in enumerate(peers):
            for a in range(3):
                far(r, a, 2 * px + py).wait_recv()
                near(r, a, c).start()
        for r in range(3):
            for a in range(3):
                near(r, a, 1 - c).wait_recv()
        for r in range(3):
            for a in range(3):
                far(r, a, me).wait_send()
                near(r, a, c).wait_send()

    d4 = D_MODEL // N_CHIPS
    vm = pl.BlockSpec(memory_space=pltpu.VMEM)
    return pl.pallas_call(
        body, name="prep_weights",
        in_specs=[vm] * 5, out_specs=[vm] * 5,
        out_shape=[jax.ShapeDtypeStruct((N_CHIPS, D_MODEL, D_MODEL), _MXU), jax.ShapeDtypeStruct((N_CHIPS, d4, D_MODEL), _MXU),
                   jax.ShapeDtypeStruct((N_CHIPS, N_META, d4), F32),
                   jax.ShapeDtypeStruct((D_MODEL, D_MODEL), _MXU), jax.ShapeDtypeStruct((d4, D_MODEL), _MXU)],
        scratch_shapes=[pltpu.SemaphoreType.DMA((9,))] * 4,
        compiler_params=pltpu.CompilerParams(vmem_limit_bytes=VMEM_LIMIT),
    )(hw_in, sw_in, hw_out, sw_out, meta)


def _scatter_small(small):
    def body(sm, lsm, send_sems, recv_sems, local_sem):
        x, y, c = _place()
        mine = 4 * x + 2 * y + c
        local = pltpu.make_async_copy(sm, lsm.at[mine], local_sem)
        local.start()

        def copy(rel, src_dev, to):
            return pltpu.make_async_remote_copy(
                src_ref=sm, dst_ref=lsm.at[src_dev], send_sem=send_sems.at[rel - 1], recv_sem=recv_sems.at[rel - 1],
                device_id=to, device_id_type=MESH)

        flip = lambda bit, v: 1 - v if bit else v
        rels = [(rel, flip(rel & 4, x), flip(rel & 2, y), flip(rel & 1, c)) for rel in range(1, N_DEV)]
        sends = [copy(rel, mine, (px, py, pc)) for rel, px, py, pc in rels]
        for cp in sends:
            cp.start()
        for rel, px, py, pc in rels:
            copy(rel, 4 * px + 2 * py + pc, (px, py, pc)).wait_recv()
        for cp in sends:
            cp.wait_send()
        local.wait()

    hbm = pl.BlockSpec(memory_space=pl.ANY)
    return pl.pallas_call(
        body, name="scatter_small", in_specs=[hbm], out_specs=hbm,
        out_shape=jax.ShapeDtypeStruct((N_DEV, SMALL_ROWS, D_MODEL), F32),
        scratch_shapes=[pltpu.SemaphoreType.DMA((N_DEV - 1,)), pltpu.SemaphoreType.DMA((N_DEV - 1,)),
                        pltpu.SemaphoreType.DMA(())],
    )(small)


def _sum_slots(landed, own, me, name):
    n, rows, _ = landed.shape
    tm = rows if rows < 256 else 256

    def body(me_ref, l_ref, o0, o1, o2, o3, out_ref):
        acc = None
        for k, o in enumerate((o0, o1, o2, o3)):
            term = jnp.where(me_ref[0] == k, o[...], l_ref[k].astype(F32))
            acc = term if acc is None else acc + term
        out_ref[...] = acc

    blk = pl.BlockSpec((tm, D_MODEL), lambda i: (i, 0))
    return pl.pallas_call(
        body, name=name, grid=(rows // tm,),
        in_specs=[pl.BlockSpec(memory_space=pltpu.SMEM), pl.BlockSpec((n, tm, D_MODEL), lambda i: (0, i, 0))] + [blk] * 4,
        out_specs=blk, out_shape=jax.ShapeDtypeStruct((rows, D_MODEL), F32),
        compiler_params=_params(("arbitrary",)),
    )(me, landed, *own)


def _swap_with_sibling(parts):
    def body(a0, a1, a2, a3, b0, b1, b2, b3, send_sems, recv_sems):
        x, y, c = _place()
        copies = [pltpu.make_async_remote_copy(src_ref=s, dst_ref=d, send_sem=send_sems.at[a], recv_sem=recv_sems.at[a],
                                               device_id=(x, y, 1 - c), device_id_type=MESH)
                  for a, (s, d) in enumerate(zip((a0, a1, a2, a3), (b0, b1, b2, b3)))]
        for cp in copies:
            cp.start()
        for cp in copies:
            cp.wait()

    hbm = pl.BlockSpec(memory_space=pl.ANY)
    return pl.pallas_call(
        body, name="swap_with_sibling", in_specs=[hbm] * 4, out_specs=[hbm] * 4,
        out_shape=[jax.ShapeDtypeStruct(p.shape, F32) for p in parts],
        scratch_shapes=[pltpu.SemaphoreType.DMA((4,)), pltpu.SemaphoreType.DMA((4,))],
    )(*parts)


def _adamw_math(w, g, m, v):
    m = ADAM_B1 * m + (1.0 - ADAM_B1) * g
    v = ADAM_B2 * v + (1.0 - ADAM_B2) * (g * g)
    m_hat = m / (1.0 - ADAM_B1 ** ADAM_STEP)
    v_hat = v / (1.0 - ADAM_B2 ** ADAM_STEP)
    delta = -ADAM_LR * (m_hat / (jnp.sqrt(v_hat) + ADAM_EPS) + ADAM_WD * w)
    return delta, m, v


def _adamw(w, g_parts, m, v, name):
    rows, cols = w.shape
    tm = rows if rows < 256 else 256
    n = len(g_parts)

    def body(*refs):
        w_ref, m_ref, v_ref = refs[n:n + 3]
        g_ref, d_ref, nm_ref, nv_ref = refs[n + 3:]
        g = refs[0][...]
        for p in refs[1:n]:
            g = g + p[...]
        g_ref[...] = g
        d_ref[...], nm_ref[...], nv_ref[...] = _adamw_math(w_ref[...], g, m_ref[...], v_ref[...])

    blk = pl.BlockSpec((tm, cols), lambda i: (i, 0))
    return pl.pallas_call(
        body, name=name, grid=(rows // tm,), in_specs=[blk] * (n + 3), out_specs=[blk] * 4,
        out_shape=[jax.ShapeDtypeStruct((rows, cols), F32)] * 4,
        compiler_params=_params(("arbitrary",)),
    )(*g_parts, w, m, v)


def _lam_of(hgrn_lb):
    def body(lb_ref, o_ref):
        lb = lb_ref[...]
        e = jnp.exp(lb - jnp.max(lb, axis=0, keepdims=True))
        o_ref[...] = e[0:1, :] / jnp.sum(e, axis=0, keepdims=True)

    return pl.pallas_call(body, name="lam_of", out_shape=jax.ShapeDtypeStruct((1, D_MODEL), F32))(hgrn_lb)


def _small_grads(land_small, lam):
    def body(l_ref, lam_ref, o_ref):
        acc = l_ref[0]
        for k in range(1, N_DEV):
            acc = acc + l_ref[k]
        p = lam_ref[...]
        slope = p * (1.0 - p)
        row = _iota2((SMALL_ROWS, D_MODEL), 0)
        o_ref[...] = acc * jnp.where(row == 4, slope, jnp.where(row == 5, -slope, 1.0))

    return pl.pallas_call(body, name="small_grads",
                          out_shape=jax.ShapeDtypeStruct((SMALL_ROWS, D_MODEL), F32))(land_small, lam)


def kernel(x, meta_tokens, pre_norm, post_norm, hgrn_w_in, hgrn_lb, hgrn_out_norm, hgrn_w_out, sb_w_in, sb_w_out, loss_target, m_meta_tokens, m_pre_norm, m_post_norm, m_hgrn_w_in, m_hgrn_lb, m_hgrn_out_norm, m_hgrn_w_out, m_sb_w_in, m_sb_w_out, v_meta_tokens, v_pre_norm, v_post_norm, v_hgrn_w_in, v_hgrn_lb, v_hgrn_out_norm, v_hgrn_w_out, v_sb_w_in, v_sb_w_out):
    d4 = D_MODEL // N_CHIPS
    chip = 2 * lax.axis_index("x") + lax.axis_index("y")
    hw_in, hw_out, meta4, sw_in16, sw_out16 = _prep_weights(hgrn_w_in, sb_w_in, hgrn_w_out, sb_w_out, meta_tokens)
    meta = meta4.transpose(1, 0, 2).reshape(N_META, D_MODEL)
    lam = _lam_of(hgrn_lb)
    loss_blk, grad_x, small, large = _local_step(
        x, loss_target, meta, pre_norm, post_norm, lam, hgrn_out_norm,
        hw_in, hw_out.reshape(D_MODEL, D_MODEL), (sw_in16, sw_out16))
    loss = lax.psum(loss_blk[0, 0], ("x", "y", "c"))

    me = jnp.reshape(chip, (1,)).astype(jnp.int32)
    parts = [_sum_slots(*large[n], me, "sum_" + n) for n in ("hw_in", "sw_in", "hw_out", "sw_out")]
    sib = _swap_with_sibling(parts)
    small = _small_grads(_scatter_small(small), lam)

    res = {}
    res["hgrn_w_in"] = _adamw(hgrn_w_in[0], [parts[0], sib[0]], m_hgrn_w_in[0], v_hgrn_w_in[0], "adamw_hw_in")
    res["sb_w_in"] = _adamw(sb_w_in[0], [parts[1], sib[1]], m_sb_w_in[0], v_sb_w_in[0], "adamw_sw_in")
    res["hgrn_w_out"] = _adamw(hgrn_w_out[0], [parts[2], sib[2]], m_hgrn_w_out[0], v_hgrn_w_out[0], "adamw_hw_out")
    res["sb_w_out"] = _adamw(sb_w_out[0], [parts[3], sib[3]], m_sb_w_out[0], v_sb_w_out[0], "adamw_sw_out")
    res["pre_norm"] = _adamw(pre_norm, [small[0:2]], m_pre_norm, v_pre_norm, "adamw_pre")
    res["post_norm"] = _adamw(post_norm, [small[2:4]], m_post_norm, v_post_norm, "adamw_post")
    res["hgrn_lb"] = _adamw(hgrn_lb, [small[4:6]], m_hgrn_lb, v_hgrn_lb, "adamw_lb")
    res["hgrn_out_norm"] = _adamw(hgrn_out_norm, [small[6:7, :D_HEAD]], m_hgrn_out_norm, v_hgrn_out_norm, "adamw_head")
    g_meta = lax.dynamic_slice_in_dim(small[7:7 + N_META], chip * d4, d4, axis=1)
    res["meta_tokens"] = _adamw(meta_tokens, [g_meta], m_meta_tokens, v_meta_tokens, "adamw_meta")
    for n in ("hgrn_w_in", "hgrn_w_out", "sb_w_in", "sb_w_out"):
        res[n] = tuple(a[None] for a in res[n])
    order = ("meta_tokens", "pre_norm", "post_norm", "hgrn_w_in", "hgrn_lb", "hgrn_out_norm", "hgrn_w_out",
             "sb_w_in", "sb_w_out")
    return (loss, grad_x, *[res[n][0] for n in order], *[res[n][1] for n in order],
            *[res[n][2] for n in order], *[res[n][3] for n in order])
```

```python
import functools

import jax
import numpy as np
import jax.numpy as jnp
from jax import lax
from jax.experimental import pallas as pl
from jax.experimental.pallas import tpu as pltpu

F32 = jnp.float32
_MXU = jnp.bfloat16

D_MODEL = 1024
N_HEADS = 8
D_HEAD = 128
BLOCK = 128
N_META = 16
TILE = 256
N_PAD = TILE - N_META
UNDERFLOW = -105.0
EPS = 1e-6
SB_SCALE = D_HEAD ** -0.5
SOFTPLUS_LINEAR = 20.0
ADAM_LR, ADAM_B1, ADAM_B2, ADAM_EPS, ADAM_WD, ADAM_STEP = 0.001, 0.9, 0.999, 1e-08, 0.01, 10
N_CHIPS = 4
N_DEV = 8
SMALL_ROWS = 24
VMEM_LIMIT = 56 * 1024 * 1024
MESH = pl.DeviceIdType.MESH

_NT = (((1,), (1,)), ((), ()))
_TN = (((0,), (0,)), ((), ()))


def _mm(a, b):
    return jnp.dot(a.astype(_MXU), b.astype(_MXU), preferred_element_type=F32)


def _mm_nt(a, b):
    return lax.dot_general(a.astype(_MXU), b.astype(_MXU), _NT, preferred_element_type=F32)


def _mm_tn(a, b):
    return lax.dot_general(a.astype(_MXU), b.astype(_MXU), _TN, preferred_element_type=F32)


def _split2(x):
    hi = x.astype(_MXU)
    return hi, (x - hi.astype(F32)).astype(_MXU)


def _mm_s(a16, state):
    hi, lo = _split2(state)
    return jnp.dot(a16, hi, preferred_element_type=F32) + jnp.dot(a16, lo, preferred_element_type=F32)


def _mm_nt_s(a16, state):
    hi, lo = _split2(state)
    return (lax.dot_general(a16, hi, _NT, preferred_element_type=F32)
            + lax.dot_general(a16, lo, _NT, preferred_element_type=F32))


def _mm01_right(x, m01):
    return jnp.dot(x.astype(_MXU), m01, preferred_element_type=F32)


def _mm01_left(m01, x):
    hi, lo = _split2(x)
    return jnp.dot(m01, hi, preferred_element_type=F32) + jnp.dot(m01, lo, preferred_element_type=F32)


def _iota2(shape, dim):
    return lax.broadcasted_iota(jnp.int32, shape, dim)


def _row_tile(total, pref):
    t = pref
    while total % t:
        t -= BLOCK
    return t


def _params(sem, limit=VMEM_LIMIT):
    return pltpu.CompilerParams(dimension_semantics=sem, vmem_limit_bytes=limit)


def _sigmoid(x):
    return 1.0 / (1.0 + jnp.exp(-x))


def _grid_ends(ndim):
    first, last = True, True
    for d in range(ndim):
        first = first & (pl.program_id(d) == 0)
        last = last & (pl.program_id(d) == pl.num_programs(d) - 1)
    return first, last


def _place():
    return lax.axis_index("x"), lax.axis_index("y"), lax.axis_index("c")


def _exchange_scratch(n):
    return [pltpu.SemaphoreType.DMA((3 * n,)), pltpu.SemaphoreType.DMA((3 * n,)), pltpu.SemaphoreType.DMA((n,))]


def _chip_exchange(srcs, dsts, sems, slotted):
    send_sems, recv_sems, local_sems = sems
    x, y, c = _place()
    me = 2 * x + y
    peers = [(1 - x, y), (x, 1 - y), (1 - x, 1 - y)]
    n = len(dsts)

    def remote(r, a, sending):
        px, py = peers[r]
        p = 2 * px + py
        return pltpu.make_async_remote_copy(
            src_ref=srcs[a].at[p] if slotted else srcs[a], dst_ref=dsts[a].at[me if sending else p],
            send_sem=send_sems.at[r * n + a], recv_sem=recv_sems.at[r * n + a],
            device_id=(px, py, c), device_id_type=MESH)

    def local(a):
        return pltpu.make_async_copy(srcs[a].at[me] if slotted else srcs[a], dsts[a].at[me], local_sems.at[a])

    def start():
        for a in range(n):
            local(a).start()
        for r in range(3):
            for a in range(n):
                remote(r, a, True).start()

    def finish():
        for r in range(3):
            for a in range(n):
                remote(r, a, False).wait_recv()
        for r in range(3):
            for a in range(n):
                remote(r, a, True).wait_send()
        for a in range(n):
            local(a).wait()

    return start, finish


def _norm_proj(h, gain, w4, name, narrow):
    t = h.shape[0]
    tm = _row_tile(t, 256)

    def body(h_ref, g_ref, w_ref, y_ref, s0, s1, s2, s3):
        x = h_ref[...]
        r = lax.rsqrt(jnp.mean(x * x, axis=-1, keepdims=True) + EPS)
        y = (x * r * g_ref[...]).astype(_MXU)
        y_ref[...] = y
        for j, s in enumerate((s0, s1, s2, s3)):
            s[...] = jnp.dot(y, w_ref[j], preferred_element_type=F32).astype(s.dtype)

    row = pl.BlockSpec((tm, D_MODEL), lambda i: (i, 0))
    return pl.pallas_call(
        body, name=name, grid=(t // tm,),
        in_specs=[row, pl.BlockSpec((1, D_MODEL), lambda i: (0, 0)),
                  pl.BlockSpec((4, D_MODEL, D_MODEL), lambda i: (0, 0, 0))],
        out_specs=[row] * 5,
        out_shape=[jax.ShapeDtypeStruct((t, D_MODEL), _MXU)]
        + [jax.ShapeDtypeStruct((t, D_MODEL), _MXU if n else F32) for n in narrow],
        compiler_params=_params(("arbitrary",)),
    )(h, gain, w4)


LEVELS = (64, 32, 16, 8, 4, 2, 1)
HEAD_GROUP = 2


def _hgrn_tables():
    r = np.arange(BLOCK)
    mats = [r[None, :] <= r[:, None]]
    x = r[:, None] ^ r[None, :]
    lv = np.full((BLOCK, BLOCK), len(LEVELS), np.int32)
    for i, m in enumerate(LEVELS):
        lv[(x >= m) & (x < 2 * m)] = i
    return jnp.asarray(np.concatenate(mats, 0).astype(np.float32), dtype=_MXU), jnp.asarray(lv)


def _hgrn_exponents(g, sums):
    b = _mm01_left(sums, g)
    row = _iota2((BLOCK, D_HEAD), 0)
    out = []
    for m in LEVELS:
        is_q = (row & m) != 0
        if m >= 4:
            grp = b.reshape(BLOCK // (2 * m), 2 * m, D_HEAD)
            ref = jnp.broadcast_to(grp[:, m - 1:m, :], grp.shape).reshape(BLOCK, D_HEAD)
            d = b - ref
            out.append(jnp.where(is_q, d, -d))
        elif m == 2:
            below, above = pltpu.roll(g, 1, axis=0), pltpu.roll(g, BLOCK - 1, axis=0)
            low = row & 3
            out.append(jnp.where(low == 3, g + below, jnp.where(low == 2, g, jnp.where(low == 0, above, 0.0))))
        else:
            out.append(jnp.where(is_q, g, 0.0))
    return b, out


def _hgrn_gates(fz, lam, chunk):
    pos = chunk * BLOCK + _iota2((BLOCK, D_HEAD), 0)
    live = pos >= N_PAD
    sg = _sigmoid(fz)
    f = lam + (1.0 - lam) * sg
    g = jnp.where(live, jnp.log(f), 0.0)
    k = jnp.where(live, (1.0 - lam) * (1.0 - sg), 0.0)
    return sg, f, g, k, live


def _level_operand(q, k, exponent, m):
    decay = jnp.exp(exponent)
    is_q = (_iota2((BLOCK, D_HEAD), 0) & m) != 0
    return is_q, decay, (jnp.where(is_q, q, k) * decay).astype(_MXU)


def _hgrn_fwd(qs, fs, vs, lam, bsz, nb, shards):
    t = qs.shape[0]
    sums, levels = _hgrn_tables()
    width = HEAD_GROUP * D_HEAD

    n_sh = len(shards)

    def body(q_ref, f_ref, v_ref, lam_ref, sums_ref, lv_ref, *rest):
        own, (o_ref, sst_ref, sym_ref), rest = rest[:n_sh], rest[n_sh:n_sh + 3], rest[n_sh + 3:]
        gathered, st_scr, sems = rest[:n_sh], rest[n_sh], rest[n_sh + 1:]
        n = pl.program_id(2)
        first, last = _grid_ends(3)
        start, finish = _chip_exchange(own, gathered, sems, slotted=False)
        pl.when(first)(start)

        @pl.when(n == 0)
        def _():
            st_scr[...] = jnp.zeros_like(st_scr)

        lv = lv_ref[...]
        r, c = _iota2((BLOCK, BLOCK), 0), _iota2((BLOCK, BLOCK), 1)
        for hh in range(HEAD_GROUP):
            ls = slice(hh * D_HEAD, (hh + 1) * D_HEAD)
            st = st_scr[hh]
            sst_ref[0, hh, 0] = st
            q, v = q_ref[:, ls], v_ref[:, ls]
            _, _, g, k, _ = _hgrn_gates(f_ref[:, ls], lam_ref[:, ls], n)
            b, exps = _hgrn_exponents(g, sums_ref[...])
            sym = jnp.zeros((BLOCK, BLOCK), F32)
            for li, m in enumerate(LEVELS):
                _, _, x16 = _level_operand(q, k, exps[li], m)
                sym = jnp.where(lv == li, lax.dot_general(x16, x16, _NT, preferred_element_type=F32), sym)
            sym = jnp.where(c == r, jnp.sum(q * k, axis=1, keepdims=True), sym).astype(_MXU)
            sym_ref[0, hh, 0] = sym
            o_ref[:, ls] = _mm_nt(q * jnp.exp(b), st) + _mm(jnp.where(c <= r, sym, 0), v)
            b_end = b[BLOCK - 1:BLOCK, :]
            st_scr[hh] = st * jnp.exp(b_end) + _mm_tn(v, k * jnp.exp(b_end - b))
        pl.when(last)(finish)

    blk = pl.BlockSpec((BLOCK, width), lambda b, h, n: (b * nb + n, h))
    hbm = pl.BlockSpec(memory_space=pl.ANY)
    return pl.pallas_call(
        body, name="hgrn_fwd", grid=(bsz, N_HEADS // HEAD_GROUP, nb),
        in_specs=[blk, blk, blk, pl.BlockSpec((1, width), lambda b, h, n: (0, h)),
                  pl.BlockSpec(sums.shape, lambda b, h, n: (0, 0)), pl.BlockSpec(levels.shape, lambda b, h, n: (0, 0))]
        + [hbm] * n_sh,
        out_specs=[blk] + [pl.BlockSpec((1, HEAD_GROUP, 1, D_HEAD, D_HEAD), lambda b, h, n: (b, h, n, 0, 0))] * 2
        + [hbm] * n_sh,
        out_shape=[jax.ShapeDtypeStruct((t, D_MODEL), F32),
                   jax.ShapeDtypeStruct((bsz, N_HEADS, nb, D_HEAD, D_HEAD), F32),
                   jax.ShapeDtypeStruct((bsz, N_HEADS, nb, D_HEAD, D_HEAD), _MXU)]
        + [jax.ShapeDtypeStruct((N_CHIPS,) + a.shape, a.dtype) for a in shards],
        scratch_shapes=[pltpu.VMEM((HEAD_GROUP, D_HEAD, D_HEAD), F32)] + _exchange_scratch(n_sh),
        compiler_params=_params(("arbitrary", "arbitrary", "arbitrary")),
    )(qs, fs, vs, lam, sums, levels, *shards)


def _hgrn_bwd(qs, fs, vs, lam, sst, sym, do, bsz, nb, outgoing):
    t = qs.shape[0]
    sums, levels = _hgrn_tables()
    width = HEAD_GROUP * D_HEAD

    n_out = len(outgoing)

    def body(q_ref, f_ref, v_ref, lam_ref, sst_ref, sym_ref, do_ref, sums_ref, lv_ref, *rest):
        send, (dq_ref, df_ref, dv_ref, dlam_ref), rest = rest[:n_out], rest[n_out:n_out + 4], rest[n_out + 4:]
        landed, dst_scr, gsum_scr, sems = rest[:n_out], rest[n_out], rest[n_out + 1], rest[n_out + 2:]
        n = pl.program_id(2)
        chunk = nb - 1 - n
        first, last = _grid_ends(3)
        start, finish = _chip_exchange(send, landed, sems, slotted=True)
        pl.when(first)(start)

        @pl.when(n == 0)
        def _():
            dst_scr[...] = jnp.zeros_like(dst_scr)
            gsum_scr[...] = jnp.zeros_like(gsum_scr)
            dlam_ref[...] = jnp.zeros_like(dlam_ref)

        lv = lv_ref[...]
        r, c = _iota2((BLOCK, BLOCK), 0), _iota2((BLOCK, BLOCK), 1)
        for hh in range(HEAD_GROUP):
            ls = slice(hh * D_HEAD, (hh + 1) * D_HEAD)
            lam = lam_ref[:, ls]
            q, v, do = q_ref[:, ls], v_ref[:, ls], do_ref[:, ls]
            sg, f, g, k, live = _hgrn_gates(f_ref[:, ls], lam, chunk)
            b, exps = _hgrn_exponents(g, sums_ref[...])
            do16, v16 = do.astype(_MXU), v.astype(_MXU)
            da = lax.dot_general(do16, v16, _NT, preferred_element_type=F32)
            da_sym = jnp.where(c < r, da, da.T)
            dq = jnp.zeros((BLOCK, D_HEAD), F32)
            dqk = jnp.zeros((BLOCK, D_HEAD), F32)
            db_q = jnp.zeros((BLOCK, D_HEAD), F32)
            db_qk = jnp.zeros((BLOCK, D_HEAD), F32)
            for li, m in enumerate(LEVELS):
                is_q, decay, x16 = _level_operand(q, k, exps[li], m)
                y = jnp.dot(jnp.where(lv == li, da_sym, 0.0).astype(_MXU), x16, preferred_element_type=F32)
                dx = y * decay
                dq = dq + jnp.where(is_q, dx, 0.0)
                dqk = dqk + dx
                p = x16.astype(F32) * y
                db_q = db_q + jnp.where(is_q, p, 0.0)
                db_qk = db_qk + p
            dk = dqk - dq
            db = 2.0 * db_q - db_qk
            a_t = jnp.where(c >= r, sym_ref[0, hh, 0], 0)
            st, dst = sst_ref[0, hh, 0], dst_scr[hh]
            eb = jnp.exp(b)
            b_end = b[BLOCK - 1:BLOCK, :]
            dec = jnp.exp(b_end - b)
            qh16, kt16 = (q * eb).astype(_MXU), (k * dec).astype(_MXU)
            dq_st = _mm_s(do16, st)
            dk_st = _mm_s(v16, dst)
            d_diag = jnp.sum(do * v, axis=1, keepdims=True)
            dq_ref[:, ls] = (dq + d_diag * k + eb * dq_st).astype(dq_ref.dtype)
            dk = dk + d_diag * q + dec * dk_st
            dv_ref[:, ls] = (jnp.dot(a_t, do16, preferred_element_type=F32) + _mm_nt_s(kt16, dst)).astype(dv_ref.dtype)
            dst_scr[hh] = dst * jnp.exp(b_end) + lax.dot_general(do16, qh16, _TN, preferred_element_type=F32)
            db = db + (qh16.astype(F32) * dq_st - kt16.astype(F32) * dk_st)
            dg = _mm01_left((c >= r).astype(_MXU), db) + gsum_scr[:, ls]
            gsum_scr[:, ls] = gsum_scr[:, ls] + jnp.sum(db, axis=0, keepdims=True)
            slope = (1.0 - lam) * sg * (1.0 - sg)
            df_ref[:, ls] = jnp.where(live, dg * slope / f - dk * slope, 0.0).astype(df_ref.dtype)
            dl = jnp.where(live, (dg / f - dk) * (1.0 - sg), 0.0)
            dlam_ref[0, :, ls] = dlam_ref[0, :, ls] + jnp.sum(dl, axis=0, keepdims=True)
        pl.when(last)(finish)

    blk = pl.BlockSpec((BLOCK, width), lambda b, h, n: (b * nb + nb - 1 - n, h))
    hbm = pl.BlockSpec(memory_space=pl.ANY)
    return pl.pallas_call(
        body, name="hgrn_bwd", grid=(bsz, N_HEADS // HEAD_GROUP, nb),
        in_specs=[blk, blk, blk, pl.BlockSpec((1, width), lambda b, h, n: (0, h)),
                  pl.BlockSpec((1, HEAD_GROUP, 1, D_HEAD, D_HEAD), lambda b, h, n: (b, h, nb - 1 - n, 0, 0)),
                  pl.BlockSpec((1, HEAD_GROUP, 1, D_HEAD, D_HEAD), lambda b, h, n: (b, h, nb - 1 - n, 0, 0)),
                  blk, pl.BlockSpec(sums.shape, lambda b, h, n: (0, 0)), pl.BlockSpec(levels.shape, lambda b, h, n: (0, 0))]
        + [hbm] * n_out,
        out_specs=[blk, blk, blk, pl.BlockSpec((1, 1, width), lambda b, h, n: (b, 0, h))] + [hbm] * n_out,
        out_shape=[jax.ShapeDtypeStruct((t, D_MODEL), _MXU)] * 3 + [jax.ShapeDtypeStruct((bsz, 1, D_MODEL), F32)]
        + [jax.ShapeDtypeStruct(a.shape, a.dtype) for a in outgoing],
        scratch_shapes=[pltpu.VMEM((HEAD_GROUP, D_HEAD, D_HEAD), F32), pltpu.VMEM((1, width), F32)]
        + _exchange_scratch(n_out),
        compiler_params=_params(("arbitrary", "arbitrary", "arbitrary")),
    )(qs, fs, vs, lam, sst, sym, do, sums, levels, *outgoing)


def _sb_valid(ahead, col, i, j):
    return (ahead < (i - j) * TILE) & (col >= N_PAD - j * TILE)


def _sb_logits(q16, k_blk, valid):
    z = lax.dot_general(q16, k_blk.astype(_MXU), _NT, preferred_element_type=F32) * SB_SCALE
    softplus = jnp.where(z > SOFTPLUS_LINEAR, z, jnp.log(1.0 + jnp.exp(jnp.minimum(z, SOFTPLUS_LINEAR))))
    return jnp.where(valid, -softplus, 0.0), z - softplus


def _sb_fwd(qs, ks, vs, bsz, nq):
    t = qs.shape[0]
    lp = nq * TILE
    width = HEAD_GROUP * D_HEAD
    groups = N_HEADS // HEAD_GROUP
    lanes = [slice(hh * D_HEAD, (hh + 1) * D_HEAD) for hh in range(HEAD_GROUP)]

    def body(q_ref, k_ref, v_ref, o_ref, c_ref, n_ref):
        b, h, i = pl.program_id(0), pl.program_id(1), pl.program_id(2)
        q16 = [q_ref[:, ls].astype(_MXU) for ls in lanes]
        r, c = _iota2((TILE, TILE), 0), _iota2((TILE, TILE), 1)
        after = (r > c).astype(_MXU)

        def more(carry):
            jj, _, _, top = carry
            return (jj <= i) & (top > UNDERFLOW)

        def step(carry):
            jj, accs, sums, _ = carry
            j = i - jj
            ks_ = pl.ds(pl.multiple_of(j * TILE, TILE), TILE)
            valid = _sb_valid(c - r, c, i, j)
            new_accs, new_sums = [], []
            for hh, ls in enumerate(lanes):
                keep, log_beta = _sb_logits(q16[hh], k_ref[ks_, ls], valid)
                after_s = _mm01_right(keep, after)
                a = jnp.where(valid, jnp.exp(log_beta + (sums[hh] + after_s)), 0.0)
                new_accs.append(accs[hh] + _mm(a, v_ref[ks_, ls]))
                new_sums.append(sums[hh] + (after_s[:, 0:1] + keep[:, 0:1]))
            top = functools.reduce(jnp.maximum, [jnp.max(x) for x in new_sums])
            return jj + 1, tuple(new_accs), tuple(new_sums), top

        init = (jnp.int32(0), tuple(jnp.zeros((TILE, D_HEAD), F32) for _ in lanes),
                tuple(jnp.zeros((TILE, 1), F32) for _ in lanes), jnp.float32(0.0))
        visited, accs, sums, _ = lax.while_loop(more, step, init)
        for hh, ls in enumerate(lanes):
            o_ref[:, ls] = accs[hh]
            c_ref[:, ls] = jnp.broadcast_to(sums[hh], (TILE, D_HEAD))
        n_ref[(b * groups + h) * nq + i] = visited.astype(F32)

    blk = pl.BlockSpec((TILE, width), lambda b, h, i: (b * nq + i, h))
    seq = pl.BlockSpec((lp, width), lambda b, h, i: (b, h))
    return pl.pallas_call(
        body, name="sb_fwd", grid=(bsz, groups, nq),
        in_specs=[blk, seq, seq], out_specs=[blk, blk, pl.BlockSpec(memory_space=pltpu.SMEM)],
        out_shape=[jax.ShapeDtypeStruct((t, D_MODEL), F32)] * 2 + [jax.ShapeDtypeStruct((bsz * groups * nq,), F32)],
        compiler_params=_params(("arbitrary", "arbitrary", "arbitrary")),
    )(qs, ks, vs)


def _sb_bwd(qs, ks, vs, ctot, visited, do, bsz, nq):
    t = qs.shape[0]
    lp = nq * TILE
    width = HEAD_GROUP * D_HEAD
    groups = N_HEADS // HEAD_GROUP
    lanes = [slice(hh * D_HEAD, (hh + 1) * D_HEAD) for hh in range(HEAD_GROUP)]

    def body(n_ref, q_ref, k_ref, v_ref, c_ref, do_ref, dq_ref, dk_ref, dv_ref, dk_acc, dv_acc):
        b, h, i = pl.program_id(0), pl.program_id(1), pl.program_id(2)

        @pl.when(i == 0)
        def _():
            dk_acc[...] = jnp.zeros_like(dk_acc)
            dv_acc[...] = jnp.zeros_like(dv_acc)

        q16 = [q_ref[:, ls].astype(_MXU) for ls in lanes]
        do16 = [do_ref[:, ls].astype(_MXU) for ls in lanes]
        totals = [c_ref[:, hh * D_HEAD:hh * D_HEAD + 1] for hh in range(HEAD_GROUP)]
        r, c = _iota2((TILE, TILE), 0), _iota2((TILE, TILE), 1)
        upto = (r <= c).astype(_MXU)
        before = (r < c).astype(_MXU)
        first = jnp.maximum(i + 1 - n_ref[(b * groups + h) * nq + i].astype(jnp.int32), 0)

        def step(j, carry):
            ks_ = pl.ds(pl.multiple_of(j * TILE, TILE), TILE)
            valid = _sb_valid(c - r, c, i, j)
            out = []
            for hh, ls in enumerate(lanes):
                dq, keep_pre, g_pre = carry[hh]
                k_blk, v_blk = k_ref[ks_, ls], v_ref[ks_, ls]
                keep, log_beta = _sb_logits(q16[hh], k_blk, valid)
                keep_upto = _mm01_right(keep, upto)
                a = jnp.where(valid, jnp.exp(log_beta + (totals[hh] - keep_pre - keep_upto)), 0.0)
                da = lax.dot_general(do16[hh], v_blk.astype(_MXU), _NT, preferred_element_type=F32)
                g = a * da
                g_inside = _mm01_right(g, before)
                g_before = g_pre + g_inside
                beta = jnp.exp(log_beta)
                dz = jnp.where(valid, g * (1.0 - beta) - beta * g_before, 0.0) * SB_SCALE
                dz16 = dz.astype(_MXU)
                dq = dq + jnp.dot(dz16, k_blk.astype(_MXU), preferred_element_type=F32)
                dk_acc[ks_, ls] += lax.dot_general(dz16, q16[hh], _TN, preferred_element_type=F32)
                dv_acc[ks_, ls] += lax.dot_general(a.astype(_MXU), do16[hh], _TN, preferred_element_type=F32)
                out.append((dq, keep_pre + keep_upto[:, TILE - 1:TILE],
                            g_pre + (g_inside[:, TILE - 1:TILE] + g[:, TILE - 1:TILE])))
            return tuple(out)

        zero_col = jnp.zeros((TILE, 1), F32)
        init = tuple((jnp.zeros((TILE, D_HEAD), F32), zero_col, zero_col) for _ in lanes)
        res = lax.fori_loop(first, i + 1, step, init)
        for hh, ls in enumerate(lanes):
            dq_ref[:, ls] = res[hh][0].astype(dq_ref.dtype)

        @pl.when(i == nq - 1)
        def _():
            dk_ref[...] = dk_acc[...].astype(dk_ref.dtype)
            dv_ref[...] = dv_acc[...].astype(dv_ref.dtype)

    blk = pl.BlockSpec((TILE, width), lambda b, h, i: (b * nq + i, h))
    seq = pl.BlockSpec((lp, width), lambda b, h, i: (b, h))
    return pl.pallas_call(
        body, name="sb_bwd", grid=(bsz, groups, nq),
        in_specs=[pl.BlockSpec(memory_space=pltpu.SMEM), blk, seq, seq, blk, blk], out_specs=[blk, seq, seq],
        out_shape=[jax.ShapeDtypeStruct((t, D_MODEL), _MXU)] * 3,
        scratch_shapes=[pltpu.VMEM((lp, width), F32)] * 2,
        compiler_params=_params(("arbitrary", "arbitrary", "arbitrary")),
    )(visited, qs, ks, vs, ctot, do)


def _head_norm(o, head_gain):
    outs, rs = [], []
    for h in range(N_HEADS):
        oh = o[:, h * D_HEAD:(h + 1) * D_HEAD]
        r = lax.rsqrt(jnp.mean(oh * oh, axis=-1, keepdims=True) + EPS)
        outs.append(oh * r)
        rs.append(r)
    return outs, rs


def _mix(o, gate, head_gain):
    if head_gain is None:
        on = o
    else:
        outs, _ = _head_norm(o, head_gain)
        on = jnp.concatenate([x * head_gain for x in outs], axis=1)
    return on, on * (gate * _sigmoid(gate))


def _out_fwd(o, gate, h_in, w_out, post_gain, head_gain, name):
    t = o.shape[0]
    tm = _row_tile(t, 256)

    def body(o_ref, g_ref, h_ref, w_ref, pg_ref, hg_ref, ho_ref, u_ref):
        _, mix = _mix(o_ref[...], g_ref[...], hg_ref[...])
        u = jnp.dot(mix.astype(_MXU), w_ref[...], preferred_element_type=F32)
        u_ref[...] = u
        r = lax.rsqrt(jnp.mean(u * u, axis=-1, keepdims=True) + EPS)
        ho_ref[...] = h_ref[...] + u * r * pg_ref[...]

    row = pl.BlockSpec((tm, D_MODEL), lambda i: (i, 0))
    vec = pl.BlockSpec((1, D_MODEL), lambda i: (0, 0))
    return pl.pallas_call(
        body, name=name, grid=(t // tm,),
        in_specs=[row, row, row, pl.BlockSpec((D_MODEL, D_MODEL), lambda i: (0, 0)), vec,
                  pl.BlockSpec((1, D_HEAD), lambda i: (0, 0))],
        out_specs=[row, row], out_shape=[jax.ShapeDtypeStruct((t, D_MODEL), F32)] * 2,
        compiler_params=_params(("arbitrary",)),
    )(o, gate, h_in, w_out, post_gain, head_gain)


def _out_fwd_loss(o, gate, h_in, w_out, post_gain, target, nq, name):
    t = o.shape[0]

    def body(o_ref, g_ref, h_ref, w_ref, pg_ref, t_ref, dh_ref, u_ref, l_ref):
        i = pl.program_id(0)

        @pl.when(i == 0)
        def _():
            l_ref[...] = jnp.zeros_like(l_ref)

        _, mix = _mix(o_ref[...], g_ref[...], None)
        u = jnp.dot(mix.astype(_MXU), w_ref[...], preferred_element_type=F32)
        u_ref[...] = u

        @pl.when(i % nq == 0)
        def _():
            dh_ref[...] = jnp.zeros_like(dh_ref)

        @pl.when(i % nq != 0)
        def _():
            r = lax.rsqrt(jnp.mean(u * u, axis=-1, keepdims=True) + EPS)
            e = h_ref[...] + u * r * pg_ref[...] - t_ref[...]
            dh_ref[...] = e * (1.0 / D_MODEL)
            l_ref[...] += jnp.sum(e * e) * (0.5 / D_MODEL)

    row = pl.BlockSpec((TILE, D_MODEL), lambda i: (i, 0))
    vec = pl.BlockSpec((1, D_MODEL), lambda i: (0, 0))
    return pl.pallas_call(
        body, name=name, grid=(t // TILE,),
        in_specs=[row, row, row, pl.BlockSpec((D_MODEL, D_MODEL), lambda i: (0, 0)), vec,
                  pl.BlockSpec((TILE, D_MODEL), lambda i: ((i // nq) * (nq - 1) + jnp.maximum(i % nq - 1, 0), 0))],
        out_specs=[row, row, pl.BlockSpec((8, 128), lambda i: (0, 0))],
        out_shape=[jax.ShapeDtypeStruct((t, D_MODEL), F32)] * 2 + [jax.ShapeDtypeStruct((8, 128), F32)],
        compiler_params=_params(("arbitrary",)),
    )(o, gate, h_in, w_out, post_gain, target)


def _out_bwd(dh, u, o, gate, w_out, post_gain, head_gain, name, narrow_do):
    t = o.shape[0]
    tm = _row_tile(t, 512)
    has_head = head_gain is not None

    def body(*refs):
        if has_head:
            dh_ref, u_ref, o_ref, g_ref, w_ref, pg_ref, hg_ref, do_ref, dg_ref, gw_ref, gw16_ref, gp_ref, gh_ref = refs
            hg = hg_ref[...]
        else:
            dh_ref, u_ref, o_ref, g_ref, w_ref, pg_ref, do_ref, dg_ref, gw_ref, gw16_ref, gp_ref = refs
            hg = None
        first = pl.program_id(0) == 0

        @pl.when(first)
        def _():
            gw_ref[...] = jnp.zeros_like(gw_ref)
            gp_ref[...] = jnp.zeros_like(gp_ref)
            if has_head:
                gh_ref[...] = jnp.zeros_like(gh_ref)

        dr, u, o, gate = dh_ref[...], u_ref[...], o_ref[...], g_ref[...]
        r = lax.rsqrt(jnp.mean(u * u, axis=-1, keepdims=True) + EPS)
        un = u * r
        gp_ref[...] += jnp.sum(dr * un, axis=0, keepdims=True)
        dun = dr * pg_ref[...]
        du = r * (dun - un * jnp.mean(dun * un, axis=-1, keepdims=True))
        on, mix = _mix(o, gate, hg)
        du16 = du.astype(_MXU)
        gw_ref[...] += lax.dot_general(mix.astype(_MXU), du16, _TN, preferred_element_type=F32)
        dmix = lax.dot_general(du16, w_ref[...], _NT, preferred_element_type=F32)
        sg = _sigmoid(gate)
        dg_ref[...] = (dmix * on * (sg * (1.0 + gate * (1.0 - sg)))).astype(dg_ref.dtype)
        don = dmix * (gate * sg)
        if has_head:
            outs, rs = _head_norm(o, hg)
            gh = jnp.zeros((1, D_HEAD), F32)
            cols = []
            for h in range(N_HEADS):
                dn = don[:, h * D_HEAD:(h + 1) * D_HEAD]
                gh = gh + jnp.sum(dn * outs[h], axis=0, keepdims=True)
                dnn = dn * hg
                cols.append(rs[h] * (dnn - outs[h] * jnp.mean(dnn * outs[h], axis=-1, keepdims=True)))
            gh_ref[...] += gh
            do_ref[...] = jnp.concatenate(cols, axis=1)
        else:
            do_ref[...] = don.astype(do_ref.dtype)

        @pl.when(pl.program_id(0) == pl.num_programs(0) - 1)
        def _():
            gw16_ref[...] = gw_ref[...].astype(_MXU)

    row = pl.BlockSpec((tm, D_MODEL), lambda i: (i, 0))
    vec = pl.BlockSpec((1, D_MODEL), lambda i: (0, 0))
    mat = pl.BlockSpec((D_MODEL, D_MODEL), lambda i: (0, 0))
    in_specs = [row, row, row, row, mat, vec]
    args = [dh, u, o, gate, w_out, post_gain]
    out_specs = [row, row, mat, mat, vec]
    out_shape = [jax.ShapeDtypeStruct((t, D_MODEL), _MXU if narrow_do else F32),
                 jax.ShapeDtypeStruct((t, D_MODEL), _MXU)] + [jax.ShapeDtypeStruct((D_MODEL, D_MODEL), F32),
                                                                  jax.ShapeDtypeStruct((D_MODEL, D_MODEL), _MXU),
                                                                  jax.ShapeDtypeStruct((1, D_MODEL), F32)]
    if has_head:
        in_specs.append(pl.BlockSpec((1, D_HEAD), lambda i: (0, 0)))
        args.append(head_gain)
        out_specs.append(pl.BlockSpec((1, D_HEAD), lambda i: (0, 0)))
        out_shape.append(jax.ShapeDtypeStruct((1, D_HEAD), F32))
    return pl.pallas_call(
        body, name=name, grid=(t // tm,), in_specs=in_specs, out_specs=out_specs, out_shape=out_shape,
        compiler_params=_params(("arbitrary",)),
    )(*args)


def _proj_bwd(ds, w4, h_in, gain, dh_out, name, outgoing=(), nq=None):
    t = h_in.shape[0]
    tm = _row_tile(t, 256) if nq is None else TILE
    n_out = len(outgoing)
    n_dh = 1 if nq is None else 2

    def body(d0, d1, d2, d3, w_ref, h_ref, g_ref, dho_ref, *rest):
        send, dh_refs, gg_ref, rest = rest[:n_out], rest[n_out:n_out + n_dh], rest[n_out + n_dh], rest[n_out + n_dh + 1:]
        landed, sems = rest[:n_out], rest[n_out:]
        if n_out:
            first, last = _grid_ends(1)
            start, finish = _chip_exchange(send, landed, sems, slotted=True)
            pl.when(first)(start)

        @pl.when(pl.program_id(0) == 0)
        def _():
            gg_ref[...] = jnp.zeros_like(gg_ref)

        dy = jnp.zeros((tm, D_MODEL), F32)
        for j, d in enumerate((d0, d1, d2, d3)):
            dy = dy + lax.dot_general(d[...].astype(_MXU), w_ref[j], _NT, preferred_element_type=F32)
        x = h_ref[...]
        r = lax.rsqrt(jnp.mean(x * x, axis=-1, keepdims=True) + EPS)
        xn = x * r
        gg_ref[...] += jnp.sum(dy * xn, axis=0, keepdims=True)
        dxn = dy * g_ref[...]
        dh = dho_ref[...] + r * (dxn - xn * jnp.mean(dxn * xn, axis=-1, keepdims=True))
        if nq is None:
            dh_refs[0][...] = dh
        else:
            in_front = pl.program_id(0) % nq == 0

            @pl.when(in_front)
            def _():
                dh_refs[1][...] = dh

            @pl.when(jnp.logical_not(in_front))
            def _():
                dh_refs[0][...] = dh
        if n_out:
            pl.when(last)(finish)

    row = pl.BlockSpec((tm, D_MODEL), lambda i: (i, 0))
    vec = pl.BlockSpec((1, D_MODEL), lambda i: (0, 0))
    hbm = pl.BlockSpec(memory_space=pl.ANY)
    if nq is None:
        dh_specs, dh_shapes = [row], [jax.ShapeDtypeStruct((t, D_MODEL), F32)]
    else:
        dh_specs = [pl.BlockSpec((TILE, D_MODEL), lambda i: ((i // nq) * (nq - 1) + jnp.maximum(i % nq - 1, 0), 0)),
                    pl.BlockSpec((TILE, D_MODEL), lambda i: (i // nq, 0))]
        dh_shapes = [jax.ShapeDtypeStruct((t // nq * (nq - 1), D_MODEL), F32),
                     jax.ShapeDtypeStruct((t // nq, D_MODEL), F32)]
    return pl.pallas_call(
        body, name=name, grid=(t // tm,),
        in_specs=[row] * 4 + [pl.BlockSpec((4, D_MODEL, D_MODEL), lambda i: (0, 0, 0)), row, vec, row] + [hbm] * n_out,
        out_specs=dh_specs + [vec] + [hbm] * n_out,
        out_shape=dh_shapes + [jax.ShapeDtypeStruct((1, D_MODEL), F32)]
        + [jax.ShapeDtypeStruct(a.shape, a.dtype) for a in outgoing],
        scratch_shapes=_exchange_scratch(n_out) if n_out else [],
        compiler_params=_params(("arbitrary",)),
    )(*ds, w4, h_in, gain, dh_out, *outgoing)


def _weight_grad(y, d, name):
    t = y.shape[0]
    tk = _row_tile(t, 2176)

    def body(y_ref, d_ref, g_ref, g16_ref):
        @pl.when(pl.program_id(0) == 0)
        def _():
            g_ref[...] = jnp.zeros_like(g_ref)

        g_ref[...] += lax.dot_general(y_ref[...], d_ref[...].astype(_MXU), _TN, preferred_element_type=F32)

        @pl.when(pl.program_id(0) == pl.num_programs(0) - 1)
        def _():
            g16_ref[...] = g_ref[...].astype(_MXU)

    row = pl.BlockSpec((tk, D_MODEL), lambda i: (i, 0))
    mat = pl.BlockSpec((D_MODEL, D_MODEL), lambda i: (0, 0))
    return pl.pallas_call(
        body, name=name, grid=(t // tk,), in_specs=[row, row], out_specs=[mat, mat],
        out_shape=[jax.ShapeDtypeStruct((D_MODEL, D_MODEL), F32), jax.ShapeDtypeStruct((D_MODEL, D_MODEL), _MXU)],
        compiler_params=_params(("arbitrary",)),
    )(y, d)


def _local_step(x, target, meta, pre_norm, post_norm, lam, head_gain, hw_in, shards):
    bsz, seq, _ = x.shape
    nq = seq // TILE + 1
    nb = nq * (TILE // BLOCK)
    lp = nq * TILE
    t = bsz * lp
    d4 = D_MODEL // N_CHIPS
    front = jnp.concatenate([jnp.zeros((N_PAD, D_MODEL), F32), meta], axis=0)
    h0 = jnp.concatenate([jnp.broadcast_to(front[None], (bsz, TILE, D_MODEL)), x], axis=1).reshape(t, D_MODEL)
    pre0, pre1, post0, post1 = pre_norm[0:1], pre_norm[1:2], post_norm[0:1], post_norm[1:2]

    y0, q0, f0, v0, g0 = _norm_proj(h0, pre0, hw_in, "norm_proj_hgrn", (False,) * 4)
    o0, sst, sym, sw_in, sw_out, hw_out = _hgrn_fwd(q0, f0, v0, lam, bsz, nb, shards)
    sw_out, hw_out = sw_out.reshape(D_MODEL, D_MODEL), hw_out.reshape(D_MODEL, D_MODEL)
    h1, u0 = _out_fwd(o0, g0, h0, hw_out, post0, head_gain, "out_fwd_hgrn")
    y1, q1, k1, v1, g1 = _norm_proj(h1, pre1, sw_in, "norm_proj_sb", (True, True, True, False))
    o1, ctot, visited = _sb_fwd(q1, k1, v1, bsz, nq)
    dh2, u1, loss_blk = _out_fwd_loss(o1, g1, h1, sw_out, post1, target.reshape(bsz * seq, D_MODEL), nq, "out_fwd_sb")

    do1, dg1, g_sw_out, g_sw_out16, g_post1 = _out_bwd(dh2, u1, o1, g1, sw_out, post1, None, "out_bwd_sb", True)
    dq1, dk1, dv1 = _sb_bwd(q1, k1, v1, ctot, visited, do1, bsz, nq)
    ds1 = (dq1, dk1, dv1, dg1)
    dh1, g_pre1 = _proj_bwd(ds1, sw_in, h1, pre1, dh2, "proj_bwd_sb")
    g_sw_in = [_weight_grad(y1, d, "wgrad_sb_%d" % j) for j, d in enumerate(ds1)]

    do0, dg0, g_hw_out, g_hw_out16, g_post0, g_head = _out_bwd(dh1, u0, o0, g0, hw_out, post0, head_gain, "out_bwd_hgrn",
                                                               False)
    ready = (jnp.stack([g16 for _, g16 in g_sw_in]), g_sw_out16.reshape(N_CHIPS, d4, D_MODEL),
             g_hw_out16.reshape(N_CHIPS, d4, D_MODEL))
    dq0, df0, dv0, dlam, land_sw_in, land_sw_out, land_hw_out = _hgrn_bwd(q0, f0, v0, lam, sst, sym, do0, bsz, nb, ready)
    ds0 = (dq0, df0, dv0, dg0)
    g_hw_in = [_weight_grad(y0, d, "wgrad_hgrn_%d" % j) for j, d in enumerate(ds0)]
    last = (jnp.stack([g16 for _, g16 in g_hw_in]),)
    grad_x, dh_front, g_pre0, land_hw_in = _proj_bwd(ds0, hw_in, h0, pre0, dh1, "proj_bwd_hgrn", last, nq)

    grad_x = grad_x.reshape(bsz, seq, D_MODEL)
    g_meta = jnp.sum(dh_front.reshape(bsz, TILE, D_MODEL)[:, N_PAD:, :], axis=0)
    g_lam = jnp.sum(dlam, axis=0)
    small = jnp.concatenate([g_pre0, g_pre1, g_post0, g_post1, g_lam, g_lam,
                             jnp.pad(g_head, ((0, 0), (0, D_MODEL - D_HEAD))), g_meta,
                             jnp.zeros((SMALL_ROWS - 23, D_MODEL), F32)], axis=0)
    rows4 = lambda g: [g[j * d4:(j + 1) * d4] for j in range(N_CHIPS)]
    large = dict(hw_in=(land_hw_in, [g for g, _ in g_hw_in]), sw_in=(land_sw_in, [g for g, _ in g_sw_in]),
                 hw_out=(land_hw_out, rows4(g_hw_out)), sw_out=(land_sw_out, rows4(g_sw_out)))
    return loss_blk, grad_x, small, large


def _prep_weights(hw_in, sw_in, hw_out, sw_out, meta):
    def body(hi_ref, si_ref, ho_ref, so_ref, m_ref, ghi, gm, si16, so16, ho16, far_send, far_recv, near_send, near_recv):
        x, y, c = _place()
        me = 2 * x + y
        ghi[me] = hi_ref[0].astype(_MXU)
        gm[me] = m_ref[...]
        si16[...] = si_ref[0].astype(_MXU)
        so16[...] = so_ref[0].astype(_MXU)
        ho16[...] = ho_ref[0].astype(_MXU)
        outs = (ghi, gm)
        n = len(outs)
        peers = [(1 - x, y), (x, 1 - y), (1 - x, 1 - y)]

        def half(a, slot, which):
            rows = outs[a].shape[1] // 2
            return outs[a].at[slot, pl.ds(which * rows, rows), :]

        def far(r, a, slot):
            px, py = peers[r]
            return pltpu.make_async_remote_copy(
                src_ref=half(a, slot, c), dst_ref=half(a, slot, c), send_sem=far_send.at[r * n + a],
                recv_sem=far_recv.at[r * n + a], device_id=(px, py, c), device_id_type=MESH)

        def near(r, a, which):
            px, py = peers[r]
            return pltpu.make_async_remote_copy(
                src_ref=half(a, 2 * px + py, which), dst_ref=half(a, 2 * px + py, which),
                send_sem=near_send.at[r * n + a], recv_sem=near_recv.at[r * n + a],
                device_id=(x, y, 1 - c), device_id_type=MESH)

        for r in range(3):
            for a in range(n):
                far(r, a, me).start()
        for r, (px, py) in enumerate(peers):
            for a in range(n):
                far(r, a, 2 * px + py).wait_recv()
                near(r, a, c).start()
        for r in range(3):
            for a in range(n):
                near(r, a, 1 - c).wait_recv()
        for r in range(3):
            for a in range(n):
                far(r, a, me).wait_send()
                near(r, a, c).wait_send()

    d4 = D_MODEL // N_CHIPS
    vm = pl.BlockSpec(memory_space=pltpu.VMEM)
    return pl.pallas_call(
        body, name="prep_weights",
        in_specs=[vm] * 5, out_specs=[vm] * 5,
        out_shape=[jax.ShapeDtypeStruct((N_CHIPS, D_MODEL, D_MODEL), _MXU), jax.ShapeDtypeStruct((N_CHIPS, N_META, d4), F32),
                   jax.ShapeDtypeStruct((D_MODEL, D_MODEL), _MXU), jax.ShapeDtypeStruct((d4, D_MODEL), _MXU),
                   jax.ShapeDtypeStruct((d4, D_MODEL), _MXU)],
        scratch_shapes=[pltpu.SemaphoreType.DMA((6,))] * 4,
        compiler_params=pltpu.CompilerParams(vmem_limit_bytes=VMEM_LIMIT),
    )(hw_in, sw_in, hw_out, sw_out, meta)


def _scatter_small(small):
    def body(sm, lsm, send_sems, recv_sems, local_sem):
        x, y, c = _place()
        mine = 4 * x + 2 * y + c
        local = pltpu.make_async_copy(sm, lsm.at[mine], local_sem)
        local.start()

        def copy(rel, src_dev, to):
            return pltpu.make_async_remote_copy(
                src_ref=sm, dst_ref=lsm.at[src_dev], send_sem=send_sems.at[rel - 1], recv_sem=recv_sems.at[rel - 1],
                device_id=to, device_id_type=MESH)

        flip = lambda bit, v: 1 - v if bit else v
        rels = [(rel, flip(rel & 4, x), flip(rel & 2, y), flip(rel & 1, c)) for rel in range(1, N_DEV)]
        sends = [copy(rel, mine, (px, py, pc)) for rel, px, py, pc in rels]
        for cp in sends:
            cp.start()
        for rel, px, py, pc in rels:
            copy(rel, 4 * px + 2 * py + pc, (px, py, pc)).wait_recv()
        for cp in sends:
            cp.wait_send()
        local.wait()

    hbm = pl.BlockSpec(memory_space=pl.ANY)
    return pl.pallas_call(
        body, name="scatter_small", in_specs=[hbm], out_specs=hbm,
        out_shape=jax.ShapeDtypeStruct((N_DEV, SMALL_ROWS, D_MODEL), F32),
        scratch_shapes=[pltpu.SemaphoreType.DMA((N_DEV - 1,)), pltpu.SemaphoreType.DMA((N_DEV - 1,)),
                        pltpu.SemaphoreType.DMA(())],
    )(small)


def _sum_slots(landed, own, me, name):
    n, rows, _ = landed.shape
    tm = rows if rows < 256 else 256

    def body(me_ref, l_ref, o0, o1, o2, o3, out_ref):
        acc = None
        for k, o in enumerate((o0, o1, o2, o3)):
            term = jnp.where(me_ref[0] == k, o[...], l_ref[k].astype(F32))
            acc = term if acc is None else acc + term
        out_ref[...] = acc

    blk = pl.BlockSpec((tm, D_MODEL), lambda i: (i, 0))
    return pl.pallas_call(
        body, name=name, grid=(rows // tm,),
        in_specs=[pl.BlockSpec(memory_space=pltpu.SMEM), pl.BlockSpec((n, tm, D_MODEL), lambda i: (0, i, 0))] + [blk] * 4,
        out_specs=blk, out_shape=jax.ShapeDtypeStruct((rows, D_MODEL), F32),
        compiler_params=_params(("arbitrary",)),
    )(me, landed, *own)


def _swap_with_sibling(parts):
    def body(a0, a1, a2, a3, b0, b1, b2, b3, send_sems, recv_sems):
        x, y, c = _place()
        copies = [pltpu.make_async_remote_copy(src_ref=s, dst_ref=d, send_sem=send_sems.at[a], recv_sem=recv_sems.at[a],
                                               device_id=(x, y, 1 - c), device_id_type=MESH)
                  for a, (s, d) in enumerate(zip((a0, a1, a2, a3), (b0, b1, b2, b3)))]
        for cp in copies:
            cp.start()
        for cp in copies:
            cp.wait()

    hbm = pl.BlockSpec(memory_space=pl.ANY)
    return pl.pallas_call(
        body, name="swap_with_sibling", in_specs=[hbm] * 4, out_specs=[hbm] * 4,
        out_shape=[jax.ShapeDtypeStruct(p.shape, F32) for p in parts],
        scratch_shapes=[pltpu.SemaphoreType.DMA((4,)), pltpu.SemaphoreType.DMA((4,))],
    )(*parts)


def _adamw_math(w, g, m, v):
    m = ADAM_B1 * m + (1.0 - ADAM_B1) * g
    v = ADAM_B2 * v + (1.0 - ADAM_B2) * (g * g)
    m_hat = m / (1.0 - ADAM_B1 ** ADAM_STEP)
    v_hat = v / (1.0 - ADAM_B2 ** ADAM_STEP)
    delta = -ADAM_LR * (m_hat / (jnp.sqrt(v_hat) + ADAM_EPS) + ADAM_WD * w)
    return delta, m, v


def _adamw(w, g_parts, m, v, name):
    rows, cols = w.shape
    tm = rows if rows < 256 else 256
    n = len(g_parts)

    def body(*refs):
        w_ref, m_ref, v_ref = refs[n:n + 3]
        g_ref, d_ref, nm_ref, nv_ref = refs[n + 3:]
        g = refs[0][...]
        for p in refs[1:n]:
            g = g + p[...]
        g_ref[...] = g
        d_ref[...], nm_ref[...], nv_ref[...] = _adamw_math(w_ref[...], g, m_ref[...], v_ref[...])

    blk = pl.BlockSpec((tm, cols), lambda i: (i, 0))
    return pl.pallas_call(
        body, name=name, grid=(rows // tm,), in_specs=[blk] * (n + 3), out_specs=[blk] * 4,
        out_shape=[jax.ShapeDtypeStruct((rows, cols), F32)] * 4,
        compiler_params=_params(("arbitrary",)),
    )(*g_parts, w, m, v)


def _lam_of(hgrn_lb):
    def body(lb_ref, o_ref):
        lb = lb_ref[...]
        e = jnp.exp(lb - jnp.max(lb, axis=0, keepdims=True))
        o_ref[...] = e[0:1, :] / jnp.sum(e, axis=0, keepdims=True)

    return pl.pallas_call(body, name="lam_of", out_shape=jax.ShapeDtypeStruct((1, D_MODEL), F32))(hgrn_lb)


def _small_grads(land_small, lam):
    def body(l_ref, lam_ref, o_ref):
        acc = l_ref[0]
        for k in range(1, N_DEV):
            acc = acc + l_ref[k]
        p = lam_ref[...]
        slope = p * (1.0 - p)
        row = _iota2((SMALL_ROWS, D_MODEL), 0)
        o_ref[...] = acc * jnp.where(row == 4, slope, jnp.where(row == 5, -slope, 1.0))

    return pl.pallas_call(body, name="small_grads",
                          out_shape=jax.ShapeDtypeStruct((SMALL_ROWS, D_MODEL), F32))(land_small, lam)


def kernel(x, meta_tokens, pre_norm, post_norm, hgrn_w_in, hgrn_lb, hgrn_out_norm, hgrn_w_out, sb_w_in, sb_w_out, loss_target, m_meta_tokens, m_pre_norm, m_post_norm, m_hgrn_w_in, m_hgrn_lb, m_hgrn_out_norm, m_hgrn_w_out, m_sb_w_in, m_sb_w_out, v_meta_tokens, v_pre_norm, v_post_norm, v_hgrn_w_in, v_hgrn_lb, v_hgrn_out_norm, v_hgrn_w_out, v_sb_w_in, v_sb_w_out):
    d4 = D_MODEL // N_CHIPS
    chip = 2 * lax.axis_index("x") + lax.axis_index("y")
    hw_in, meta4, sw_in16, sw_out16, hw_out16 = _prep_weights(hgrn_w_in, sb_w_in, hgrn_w_out, sb_w_out, meta_tokens)
    meta = meta4.transpose(1, 0, 2).reshape(N_META, D_MODEL)
    lam = _lam_of(hgrn_lb)
    loss_blk, grad_x, small, large = _local_step(
        x, loss_target, meta, pre_norm, post_norm, lam, hgrn_out_norm,
        hw_in, (sw_in16, sw_out16, hw_out16))
    loss = lax.psum(loss_blk[0, 0], ("x", "y", "c"))

    me = jnp.reshape(chip, (1,)).astype(jnp.int32)
    parts = [_sum_slots(*large[n], me, "sum_" + n) for n in ("hw_in", "sw_in", "hw_out", "sw_out")]
    sib = _swap_with_sibling(parts)
    small = _small_grads(_scatter_small(small), lam)

    res = {}
    res["hgrn_w_in"] = _adamw(hgrn_w_in[0], [parts[0], sib[0]], m_hgrn_w_in[0], v_hgrn_w_in[0], "adamw_hw_in")
    res["sb_w_in"] = _adamw(sb_w_in[0], [parts[1], sib[1]], m_sb_w_in[0], v_sb_w_in[0], "adamw_sw_in")
    res["hgrn_w_out"] = _adamw(hgrn_w_out[0], [parts[2], sib[2]], m_hgrn_w_out[0], v_hgrn_w_out[0], "adamw_hw_out")
    res["sb_w_out"] = _adamw(sb_w_out[0], [parts[3], sib[3]], m_sb_w_out[0], v_sb_w_out[0], "adamw_sw_out")
    res["pre_norm"] = _adamw(pre_norm, [small[0:2]], m_pre_norm, v_pre_norm, "adamw_pre")
    res["post_norm"] = _adamw(post_norm, [small[2:4]], m_post_norm, v_post_norm, "adamw_post")
    res["hgrn_lb"] = _adamw(hgrn_lb, [small[4:6]], m_hgrn_lb, v_hgrn_lb, "adamw_lb")
    res["hgrn_out_norm"] = _adamw(hgrn_out_norm, [small[6:7, :D_HEAD]], m_hgrn_out_norm, v_hgrn_out_norm, "adamw_head")
    g_meta = lax.dynamic_slice_in_dim(small[7:7 + N_META], chip * d4, d4, axis=1)
    res["meta_tokens"] = _adamw(meta_tokens, [g_meta], m_meta_tokens, v_meta_tokens, "adamw_meta")
    for n in ("hgrn_w_in", "hgrn_w_out", "sb_w_in", "sb_w_out"):
        res[n] = tuple(a[None] for a in res[n])
    order = ("meta_tokens", "pre_norm", "post_norm", "hgrn_w_in", "hgrn_lb", "hgrn_out_norm", "hgrn_w_out",
             "sb_w_in", "sb_w_out")
    return (loss, grad_x, *[res[n][0] for n in order], *[res[n][1] for n in order],
            *[res[n][2] for n in order], *[res[n][3] for n in order])
```

```python
import functools

import jax
import numpy as np
import jax.numpy as jnp
from jax import lax
from jax.experimental import pallas as pl
from jax.experimental.pallas import tpu as pltpu

F32 = jnp.float32
_MXU = jnp.bfloat16

D_MODEL = 1024
N_HEADS = 8
D_HEAD = 128
BLOCK = 128
N_META = 16
TILE = 256
N_PAD = TILE - N_META
UNDERFLOW = -105.0
EPS = 1e-6
SB_SCALE = D_HEAD ** -0.5
SOFTPLUS_LINEAR = 20.0
ADAM_LR, ADAM_B1, ADAM_B2, ADAM_EPS, ADAM_WD, ADAM_STEP = 0.001, 0.9, 0.999, 1e-08, 0.01, 10
N_CHIPS = 4
N_DEV = 8
SMALL_ROWS = 24
VMEM_LIMIT = 56 * 1024 * 1024
MESH = pl.DeviceIdType.MESH

_NT = (((1,), (1,)), ((), ()))
_TN = (((0,), (0,)), ((), ()))


def _mm(a, b):
    return jnp.dot(a.astype(_MXU), b.astype(_MXU), preferred_element_type=F32)


def _mm_nt(a, b):
    return lax.dot_general(a.astype(_MXU), b.astype(_MXU), _NT, preferred_element_type=F32)


def _mm_tn(a, b):
    return lax.dot_general(a.astype(_MXU), b.astype(_MXU), _TN, preferred_element_type=F32)


def _split2(x):
    hi = x.astype(_MXU)
    return hi, (x - hi.astype(F32)).astype(_MXU)


def _mm_s(a16, state):
    hi, lo = _split2(state)
    return jnp.dot(a16, hi, preferred_element_type=F32) + jnp.dot(a16, lo, preferred_element_type=F32)


def _mm_nt_s(a16, state):
    hi, lo = _split2(state)
    return (lax.dot_general(a16, hi, _NT, preferred_element_type=F32)
            + lax.dot_general(a16, lo, _NT, preferred_element_type=F32))


def _mm01_right(x, m01):
    return jnp.dot(x.astype(_MXU), m01, preferred_element_type=F32)


def _mm01_left(m01, x):
    hi, lo = _split2(x)
    return jnp.dot(m01, hi, preferred_element_type=F32) + jnp.dot(m01, lo, preferred_element_type=F32)


def _iota2(shape, dim):
    return lax.broadcasted_iota(jnp.int32, shape, dim)


def _row_tile(total, pref):
    t = pref
    while total % t:
        t -= BLOCK
    return t


def _params(sem, limit=VMEM_LIMIT):
    return pltpu.CompilerParams(dimension_semantics=sem, vmem_limit_bytes=limit)


def _sigmoid(x):
    return 1.0 / (1.0 + jnp.exp(-x))


def _grid_ends(ndim):
    first, last = True, True
    for d in range(ndim):
        first = first & (pl.program_id(d) == 0)
        last = last & (pl.program_id(d) == pl.num_programs(d) - 1)
    return first, last


def _place():
    return lax.axis_index("x"), lax.axis_index("y"), lax.axis_index("c")


def _exchange_scratch(n):
    return [pltpu.SemaphoreType.DMA((3 * n,)), pltpu.SemaphoreType.DMA((3 * n,)), pltpu.SemaphoreType.DMA((n,))]


def _chip_exchange(srcs, dsts, sems, slotted):
    send_sems, recv_sems, local_sems = sems
    x, y, c = _place()
    me = 2 * x + y
    peers = [(1 - x, y), (x, 1 - y), (1 - x, 1 - y)]
    n = len(dsts)

    def remote(r, a, sending):
        px, py = peers[r]
        p = 2 * px + py
        return pltpu.make_async_remote_copy(
            src_ref=srcs[a].at[p] if slotted else srcs[a], dst_ref=dsts[a].at[me if sending else p],
            send_sem=send_sems.at[r * n + a], recv_sem=recv_sems.at[r * n + a],
            device_id=(px, py, c), device_id_type=MESH)

    def local(a):
        return pltpu.make_async_copy(srcs[a].at[me] if slotted else srcs[a], dsts[a].at[me], local_sems.at[a])

    def start():
        for a in range(n):
            local(a).start()
        for r in range(3):
            for a in range(n):
                remote(r, a, True).start()

    def finish():
        for r in range(3):
            for a in range(n):
                remote(r, a, False).wait_recv()
        for r in range(3):
            for a in range(n):
                remote(r, a, True).wait_send()
        for a in range(n):
            local(a).wait()

    return start, finish


def _norm_proj(h, gain, w4, name, narrow):
    t = h.shape[0]
    tm = _row_tile(t, 256)

    def body(h_ref, g_ref, w_ref, y_ref, s0, s1, s2, s3):
        x = h_ref[...]
        r = lax.rsqrt(jnp.mean(x * x, axis=-1, keepdims=True) + EPS)
        y = (x * r * g_ref[...]).astype(_MXU)
        y_ref[...] = y
        for j, s in enumerate((s0, s1, s2, s3)):
            s[...] = jnp.dot(y, w_ref[j], preferred_element_type=F32).astype(s.dtype)

    row = pl.BlockSpec((tm, D_MODEL), lambda i: (i, 0))
    return pl.pallas_call(
        body, name=name, grid=(t // tm,),
        in_specs=[row, pl.BlockSpec((1, D_MODEL), lambda i: (0, 0)),
                  pl.BlockSpec((4, D_MODEL, D_MODEL), lambda i: (0, 0, 0))],
        out_specs=[row] * 5,
        out_shape=[jax.ShapeDtypeStruct((t, D_MODEL), _MXU)]
        + [jax.ShapeDtypeStruct((t, D_MODEL), _MXU if n else F32) for n in narrow],
        compiler_params=_params(("arbitrary",)),
    )(h, gain, w4)


LEVELS = (64, 32, 16, 8, 4, 2, 1)
HEAD_GROUP = 2


def _hgrn_tables():
    r = np.arange(BLOCK)
    mats = [r[None, :] <= r[:, None]]
    x = r[:, None] ^ r[None, :]
    lv = np.full((BLOCK, BLOCK), len(LEVELS), np.int32)
    for i, m in enumerate(LEVELS):
        lv[(x >= m) & (x < 2 * m)] = i
    return jnp.asarray(np.concatenate(mats, 0).astype(np.float32), dtype=_MXU), jnp.asarray(lv)


def _hgrn_exponents(g, sums):
    b = _mm01_left(sums, g)
    row = _iota2((BLOCK, D_HEAD), 0)
    out = []
    for m in LEVELS:
        is_q = (row & m) != 0
        if m >= 4:
            grp = b.reshape(BLOCK // (2 * m), 2 * m, D_HEAD)
            ref = jnp.broadcast_to(grp[:, m - 1:m, :], grp.shape).reshape(BLOCK, D_HEAD)
            d = b - ref
            out.append(jnp.where(is_q, d, -d))
        elif m == 2:
            below, above = pltpu.roll(g, 1, axis=0), pltpu.roll(g, BLOCK - 1, axis=0)
            low = row & 3
            out.append(jnp.where(low == 3, g + below, jnp.where(low == 2, g, jnp.where(low == 0, above, 0.0))))
        else:
            out.append(jnp.where(is_q, g, 0.0))
    return b, out


def _hgrn_gates(fz, lam, chunk):
    pos = chunk * BLOCK + _iota2((BLOCK, D_HEAD), 0)
    live = pos >= N_PAD
    sg = _sigmoid(fz)
    f = lam + (1.0 - lam) * sg
    g = jnp.where(live, jnp.log(f), 0.0)
    k = jnp.where(live, (1.0 - lam) * (1.0 - sg), 0.0)
    return sg, f, g, k, live


def _level_operand(q, k, exponent, m):
    decay = jnp.exp(exponent)
    is_q = (_iota2((BLOCK, D_HEAD), 0) & m) != 0
    return is_q, decay, (jnp.where(is_q, q, k) * decay).astype(_MXU)


def _hgrn_fwd(qs, fs, vs, lam, bsz, nb, shards):
    t = qs.shape[0]
    sums, levels = _hgrn_tables()
    width = HEAD_GROUP * D_HEAD

    n_sh = len(shards)

    def body(q_ref, f_ref, v_ref, lam_ref, sums_ref, lv_ref, *rest):
        own, (o_ref, sst_ref, sym_ref), rest = rest[:n_sh], rest[n_sh:n_sh + 3], rest[n_sh + 3:]
        gathered, st_scr, sems = rest[:n_sh], rest[n_sh], rest[n_sh + 1:]
        n = pl.program_id(2)
        first, last = _grid_ends(3)
        start, finish = _chip_exchange(own, gathered, sems, slotted=False)
        pl.when(first)(start)

        @pl.when(n == 0)
        def _():
            st_scr[...] = jnp.zeros_like(st_scr)

        lv = lv_ref[...]
        r, c = _iota2((BLOCK, BLOCK), 0), _iota2((BLOCK, BLOCK), 1)
        for hh in range(HEAD_GROUP):
            ls = slice(hh * D_HEAD, (hh + 1) * D_HEAD)
            st = st_scr[hh]
            sst_ref[0, hh, 0] = st
            q, v = q_ref[:, ls], v_ref[:, ls]
            _, _, g, k, _ = _hgrn_gates(f_ref[:, ls], lam_ref[:, ls], n)
            b, exps = _hgrn_exponents(g, sums_ref[...])
            sym = jnp.zeros((BLOCK, BLOCK), F32)
            for li, m in enumerate(LEVELS):
                _, _, x16 = _level_operand(q, k, exps[li], m)
                sym = jnp.where(lv == li, lax.dot_general(x16, x16, _NT, preferred_element_type=F32), sym)
            sym = jnp.where(c == r, jnp.sum(q * k, axis=1, keepdims=True), sym).astype(_MXU)
            sym_ref[0, hh, 0] = sym
            o_ref[:, ls] = (_mm_nt(q * jnp.exp(b), st) + _mm(jnp.where(c <= r, sym, 0), v)).astype(o_ref.dtype)
            b_end = b[BLOCK - 1:BLOCK, :]
            st_scr[hh] = st * jnp.exp(b_end) + _mm_tn(v, k * jnp.exp(b_end - b))
        pl.when(last)(finish)

    blk = pl.BlockSpec((BLOCK, width), lambda b, h, n: (b * nb + n, h))
    hbm = pl.BlockSpec(memory_space=pl.ANY)
    return pl.pallas_call(
        body, name="hgrn_fwd", grid=(bsz, N_HEADS // HEAD_GROUP, nb),
        in_specs=[blk, blk, blk, pl.BlockSpec((1, width), lambda b, h, n: (0, h)),
                  pl.BlockSpec(sums.shape, lambda b, h, n: (0, 0)), pl.BlockSpec(levels.shape, lambda b, h, n: (0, 0))]
        + [hbm] * n_sh,
        out_specs=[blk] + [pl.BlockSpec((1, HEAD_GROUP, 1, D_HEAD, D_HEAD), lambda b, h, n: (b, h, n, 0, 0))] * 2
        + [hbm] * n_sh,
        out_shape=[jax.ShapeDtypeStruct((t, D_MODEL), _MXU),
                   jax.ShapeDtypeStruct((bsz, N_HEADS, nb, D_HEAD, D_HEAD), F32),
                   jax.ShapeDtypeStruct((bsz, N_HEADS, nb, D_HEAD, D_HEAD), _MXU)]
        + [jax.ShapeDtypeStruct((N_CHIPS,) + a.shape, a.dtype) for a in shards],
        scratch_shapes=[pltpu.VMEM((HEAD_GROUP, D_HEAD, D_HEAD), F32)] + _exchange_scratch(n_sh),
        compiler_params=_params(("arbitrary", "arbitrary", "arbitrary")),
    )(qs, fs, vs, lam, sums, levels, *shards)


def _hgrn_bwd(qs, fs, vs, lam, sst, sym, do, bsz, nb, outgoing):
    t = qs.shape[0]
    sums, levels = _hgrn_tables()
    width = HEAD_GROUP * D_HEAD

    n_out = len(outgoing)

    def body(q_ref, f_ref, v_ref, lam_ref, sst_ref, sym_ref, do_ref, sums_ref, lv_ref, *rest):
        send, (dq_ref, df_ref, dv_ref, dlam_ref), rest = rest[:n_out], rest[n_out:n_out + 4], rest[n_out + 4:]
        landed, dst_scr, gsum_scr, sems = rest[:n_out], rest[n_out], rest[n_out + 1], rest[n_out + 2:]
        n = pl.program_id(2)
        chunk = nb - 1 - n
        first, last = _grid_ends(3)
        start, finish = _chip_exchange(send, landed, sems, slotted=True)
        pl.when(first)(start)

        @pl.when(n == 0)
        def _():
            dst_scr[...] = jnp.zeros_like(dst_scr)
            gsum_scr[...] = jnp.zeros_like(gsum_scr)
            dlam_ref[...] = jnp.zeros_like(dlam_ref)

        lv = lv_ref[...]
        r, c = _iota2((BLOCK, BLOCK), 0), _iota2((BLOCK, BLOCK), 1)
        for hh in range(HEAD_GROUP):
            ls = slice(hh * D_HEAD, (hh + 1) * D_HEAD)
            lam = lam_ref[:, ls]
            q, v, do = q_ref[:, ls], v_ref[:, ls], do_ref[:, ls]
            sg, f, g, k, live = _hgrn_gates(f_ref[:, ls], lam, chunk)
            b, exps = _hgrn_exponents(g, sums_ref[...])
            do16, v16 = do.astype(_MXU), v.astype(_MXU)
            da = lax.dot_general(do16, v16, _NT, preferred_element_type=F32)
            da_sym = jnp.where(c < r, da, da.T)
            dq = jnp.zeros((BLOCK, D_HEAD), F32)
            dqk = jnp.zeros((BLOCK, D_HEAD), F32)
            db_q = jnp.zeros((BLOCK, D_HEAD), F32)
            db_qk = jnp.zeros((BLOCK, D_HEAD), F32)
            for li, m in enumerate(LEVELS):
                is_q, decay, x16 = _level_operand(q, k, exps[li], m)
                y = jnp.dot(jnp.where(lv == li, da_sym, 0.0).astype(_MXU), x16, preferred_element_type=F32)
                dx = y * decay
                dq = dq + jnp.where(is_q, dx, 0.0)
                dqk = dqk + dx
                p = x16.astype(F32) * y
                db_q = db_q + jnp.where(is_q, p, 0.0)
                db_qk = db_qk + p
            dk = dqk - dq
            db = 2.0 * db_q - db_qk
            a_t = jnp.where(c >= r, sym_ref[0, hh, 0], 0)
            st, dst = sst_ref[0, hh, 0], dst_scr[hh]
            eb = jnp.exp(b)
            b_end = b[BLOCK - 1:BLOCK, :]
            dec = jnp.exp(b_end - b)
            qh16, kt16 = (q * eb).astype(_MXU), (k * dec).astype(_MXU)
            dq_st = _mm_s(do16, st)
            dk_st = _mm_s(v16, dst)
            d_diag = jnp.sum(do * v, axis=1, keepdims=True)
            dq_ref[:, ls] = (dq + d_diag * k + eb * dq_st).astype(dq_ref.dtype)
            dk = dk + d_diag * q + dec * dk_st
            dv_ref[:, ls] = (jnp.dot(a_t, do16, preferred_element_type=F32) + _mm_nt_s(kt16, dst)).astype(dv_ref.dtype)
            dst_scr[hh] = dst * jnp.exp(b_end) + lax.dot_general(do16, qh16, _TN, preferred_element_type=F32)
            db = db + (qh16.astype(F32) * dq_st - kt16.astype(F32) * dk_st)
            dg = _mm01_left((c >= r).astype(_MXU), db) + gsum_scr[:, ls]
            gsum_scr[:, ls] = gsum_scr[:, ls] + jnp.sum(db, axis=0, keepdims=True)
            slope = (1.0 - lam) * sg * (1.0 - sg)
            df_ref[:, ls] = jnp.where(live, dg * slope / f - dk * slope, 0.0).astype(df_ref.dtype)
            dl = jnp.where(live, (dg / f - dk) * (1.0 - sg), 0.0)
            dlam_ref[0, :, ls] = dlam_ref[0, :, ls] + jnp.sum(dl, axis=0, keepdims=True)
        pl.when(last)(finish)

    blk = pl.BlockSpec((BLOCK, width), lambda b, h, n: (b * nb + nb - 1 - n, h))
    hbm = pl.BlockSpec(memory_space=pl.ANY)
    return pl.pallas_call(
        body, name="hgrn_bwd", grid=(bsz, N_HEADS // HEAD_GROUP, nb),
        in_specs=[blk, blk, blk, pl.BlockSpec((1, width), lambda b, h, n: (0, h)),
                  pl.BlockSpec((1, HEAD_GROUP, 1, D_HEAD, D_HEAD), lambda b, h, n: (b, h, nb - 1 - n, 0, 0)),
                  pl.BlockSpec((1, HEAD_GROUP, 1, D_HEAD, D_HEAD), lambda b, h, n: (b, h, nb - 1 - n, 0, 0)),
                  blk, pl.BlockSpec(sums.shape, lambda b, h, n: (0, 0)), pl.BlockSpec(levels.shape, lambda b, h, n: (0, 0))]
        + [hbm] * n_out,
        out_specs=[blk, blk, blk, pl.BlockSpec((1, 1, width), lambda b, h, n: (b, 0, h))] + [hbm] * n_out,
        out_shape=[jax.ShapeDtypeStruct((t, D_MODEL), _MXU)] * 3 + [jax.ShapeDtypeStruct((bsz, 1, D_MODEL), F32)]
        + [jax.ShapeDtypeStruct(a.shape, a.dtype) for a in outgoing],
        scratch_shapes=[pltpu.VMEM((HEAD_GROUP, D_HEAD, D_HEAD), F32), pltpu.VMEM((1, width), F32)]
        + _exchange_scratch(n_out),
        compiler_params=_params(("arbitrary", "arbitrary", "arbitrary")),
    )(qs, fs, vs, lam, sst, sym, do, sums, levels, *outgoing)


def _sb_valid(ahead, col, i, j):
    return (ahead < (i - j) * TILE) & (col >= N_PAD - j * TILE)


def _sb_logits(q16, k_blk, valid):
    z = lax.dot_general(q16, k_blk.astype(_MXU), _NT, preferred_element_type=F32) * SB_SCALE
    softplus = jnp.where(z > SOFTPLUS_LINEAR, z, jnp.log(1.0 + jnp.exp(jnp.minimum(z, SOFTPLUS_LINEAR))))
    return jnp.where(valid, -softplus, 0.0), z - softplus


def _sb_fwd(qs, ks, vs, bsz, nq):
    t = qs.shape[0]
    lp = nq * TILE
    width = HEAD_GROUP * D_HEAD
    groups = N_HEADS // HEAD_GROUP
    lanes = [slice(hh * D_HEAD, (hh + 1) * D_HEAD) for hh in range(HEAD_GROUP)]

    def body(q_ref, k_ref, v_ref, o_ref, c_ref, n_ref):
        b, h, i = pl.program_id(0), pl.program_id(1), pl.program_id(2)
        q16 = [q_ref[:, ls].astype(_MXU) for ls in lanes]
        r, c = _iota2((TILE, TILE), 0), _iota2((TILE, TILE), 1)
        after = (r > c).astype(_MXU)

        def more(carry):
            jj, _, _, top = carry
            return (jj <= i) & (top > UNDERFLOW)

        def step(carry):
            jj, accs, sums, _ = carry
            j = i - jj
            ks_ = pl.ds(pl.multiple_of(j * TILE, TILE), TILE)
            valid = _sb_valid(c - r, c, i, j)
            new_accs, new_sums = [], []
            for hh, ls in enumerate(lanes):
                keep, log_beta = _sb_logits(q16[hh], k_ref[ks_, ls], valid)
                after_s = _mm01_right(keep, after)
                a = jnp.where(valid, jnp.exp(log_beta + (sums[hh] + after_s)), 0.0)
                new_accs.append(accs[hh] + _mm(a, v_ref[ks_, ls]))
                new_sums.append(sums[hh] + (after_s[:, 0:1] + keep[:, 0:1]))
            top = functools.reduce(jnp.maximum, [jnp.max(x) for x in new_sums])
            return jj + 1, tuple(new_accs), tuple(new_sums), top

        init = (jnp.int32(0), tuple(jnp.zeros((TILE, D_HEAD), F32) for _ in lanes),
                tuple(jnp.zeros((TILE, 1), F32) for _ in lanes), jnp.float32(0.0))
        visited, accs, sums, _ = lax.while_loop(more, step, init)
        for hh, ls in enumerate(lanes):
            o_ref[:, ls] = accs[hh].astype(o_ref.dtype)
            c_ref[:, ls] = jnp.broadcast_to(sums[hh], (TILE, D_HEAD))
        n_ref[(b * groups + h) * nq + i] = visited.astype(F32)

    blk = pl.BlockSpec((TILE, width), lambda b, h, i: (b * nq + i, h))
    seq = pl.BlockSpec((lp, width), lambda b, h, i: (b, h))
    return pl.pallas_call(
        body, name="sb_fwd", grid=(bsz, groups, nq),
        in_specs=[blk, seq, seq], out_specs=[blk, blk, pl.BlockSpec(memory_space=pltpu.SMEM)],
        out_shape=[jax.ShapeDtypeStruct((t, D_MODEL), _MXU), jax.ShapeDtypeStruct((t, D_MODEL), F32),
                   jax.ShapeDtypeStruct((bsz * groups * nq,), F32)],
        compiler_params=_params(("arbitrary", "arbitrary", "arbitrary")),
    )(qs, ks, vs)


def _sb_bwd(qs, ks, vs, ctot, visited, do, bsz, nq):
    t = qs.shape[0]
    lp = nq * TILE
    width = HEAD_GROUP * D_HEAD
    groups = N_HEADS // HEAD_GROUP
    lanes = [slice(hh * D_HEAD, (hh + 1) * D_HEAD) for hh in range(HEAD_GROUP)]

    def body(n_ref, q_ref, k_ref, v_ref, c_ref, do_ref, dq_ref, dk_ref, dv_ref, dk_acc, dv_acc):
        b, h, i = pl.program_id(0), pl.program_id(1), pl.program_id(2)

        @pl.when(i == 0)
        def _():
            dk_acc[...] = jnp.zeros_like(dk_acc)
            dv_acc[...] = jnp.zeros_like(dv_acc)

        q16 = [q_ref[:, ls].astype(_MXU) for ls in lanes]
        do16 = [do_ref[:, ls].astype(_MXU) for ls in lanes]
        totals = [c_ref[:, hh * D_HEAD:hh * D_HEAD + 1] for hh in range(HEAD_GROUP)]
        r, c = _iota2((TILE, TILE), 0), _iota2((TILE, TILE), 1)
        upto = (r <= c).astype(_MXU)
        before = (r < c).astype(_MXU)
        first = jnp.maximum(i + 1 - n_ref[(b * groups + h) * nq + i].astype(jnp.int32), 0)

        def step(j, carry):
            ks_ = pl.ds(pl.multiple_of(j * TILE, TILE), TILE)
            valid = _sb_valid(c - r, c, i, j)
            out = []
            for hh, ls in enumerate(lanes):
                dq, keep_pre, g_pre = carry[hh]
                k_blk, v_blk = k_ref[ks_, ls], v_ref[ks_, ls]
                keep, log_beta = _sb_logits(q16[hh], k_blk, valid)
                keep_upto = _mm01_right(keep, upto)
                a = jnp.where(valid, jnp.exp(log_beta + (totals[hh] - keep_pre - keep_upto)), 0.0)
                da = lax.dot_general(do16[hh], v_blk.astype(_MXU), _NT, preferred_element_type=F32)
                g = a * da
                g_inside = _mm01_right(g, before)
                g_before = g_pre + g_inside
                beta = jnp.exp(log_beta)
                dz = jnp.where(valid, g * (1.0 - beta) - beta * g_before, 0.0) * SB_SCALE
                dz16 = dz.astype(_MXU)
                dq = dq + jnp.dot(dz16, k_blk.astype(_MXU), preferred_element_type=F32)
                dk_acc[ks_, ls] += lax.dot_general(dz16, q16[hh], _TN, preferred_element_type=F32)
                dv_acc[ks_, ls] += lax.dot_general(a.astype(_MXU), do16[hh], _TN, preferred_element_type=F32)
                out.append((dq, keep_pre + keep_upto[:, TILE - 1:TILE],
                            g_pre + (g_inside[:, TILE - 1:TILE] + g[:, TILE - 1:TILE])))
            return tuple(out)

        zero_col = jnp.zeros((TILE, 1), F32)
        init = tuple((jnp.zeros((TILE, D_HEAD), F32), zero_col, zero_col) for _ in lanes)
        res = lax.fori_loop(first, i + 1, step, init)
        for hh, ls in enumerate(lanes):
            dq_ref[:, ls] = res[hh][0].astype(dq_ref.dtype)

        @pl.when(i == nq - 1)
        def _():
            dk_ref[...] = dk_acc[...].astype(dk_ref.dtype)
            dv_ref[...] = dv_acc[...].astype(dv_ref.dtype)

    blk = pl.BlockSpec((TILE, width), lambda b, h, i: (b * nq + i, h))
    seq = pl.BlockSpec((lp, width), lambda b, h, i: (b, h))
    return pl.pallas_call(
        body, name="sb_bwd", grid=(bsz, groups, nq),
        in_specs=[pl.BlockSpec(memory_space=pltpu.SMEM), blk, seq, seq, blk, blk], out_specs=[blk, seq, seq],
        out_shape=[jax.ShapeDtypeStruct((t, D_MODEL), _MXU)] * 3,
        scratch_shapes=[pltpu.VMEM((lp, width), F32)] * 2,
        compiler_params=_params(("arbitrary", "arbitrary", "arbitrary")),
    )(visited, qs, ks, vs, ctot, do)


def _head_norm(o, head_gain):
    outs, rs = [], []
    for h in range(N_HEADS):
        oh = o[:, h * D_HEAD:(h + 1) * D_HEAD]
        r = lax.rsqrt(jnp.mean(oh * oh, axis=-1, keepdims=True) + EPS)
        outs.append(oh * r)
        rs.append(r)
    return outs, rs


def _mix(o, gate, head_gain):
    if head_gain is None:
        on = o
    else:
        outs, _ = _head_norm(o, head_gain)
        on = jnp.concatenate([x * head_gain for x in outs], axis=1)
    return on, on * (gate * _sigmoid(gate))


def _out_fwd(o, gate, h_in, w_out, post_gain, head_gain, name):
    t = o.shape[0]
    tm = _row_tile(t, 256)

    def body(o_ref, g_ref, h_ref, w_ref, pg_ref, hg_ref, ho_ref, u_ref):
        _, mix = _mix(o_ref[...].astype(F32), g_ref[...].astype(F32), hg_ref[...])
        u = jnp.dot(mix.astype(_MXU), w_ref[...], preferred_element_type=F32)
        u_ref[...] = u
        r = lax.rsqrt(jnp.mean(u * u, axis=-1, keepdims=True) + EPS)
        ho_ref[...] = h_ref[...] + u * r * pg_ref[...]

    row = pl.BlockSpec((tm, D_MODEL), lambda i: (i, 0))
    vec = pl.BlockSpec((1, D_MODEL), lambda i: (0, 0))
    return pl.pallas_call(
        body, name=name, grid=(t // tm,),
        in_specs=[row, row, row, pl.BlockSpec((D_MODEL, D_MODEL), lambda i: (0, 0)), vec,
                  pl.BlockSpec((1, D_HEAD), lambda i: (0, 0))],
        out_specs=[row, row], out_shape=[jax.ShapeDtypeStruct((t, D_MODEL), F32)] * 2,
        compiler_params=_params(("arbitrary",)),
    )(o, gate, h_in, w_out, post_gain, head_gain)


def _out_fwd_loss(o, gate, h_in, w_out, post_gain, target, nq, name):
    t = o.shape[0]

    def body(o_ref, g_ref, h_ref, w_ref, pg_ref, t_ref, dh_ref, u_ref, l_ref):
        i = pl.program_id(0)

        @pl.when(i == 0)
        def _():
            l_ref[...] = jnp.zeros_like(l_ref)

        _, mix = _mix(o_ref[...].astype(F32), g_ref[...].astype(F32), None)
        u = jnp.dot(mix.astype(_MXU), w_ref[...], preferred_element_type=F32)
        u_ref[...] = u

        @pl.when(i % nq == 0)
        def _():
            dh_ref[...] = jnp.zeros_like(dh_ref)

        @pl.when(i % nq != 0)
        def _():
            r = lax.rsqrt(jnp.mean(u * u, axis=-1, keepdims=True) + EPS)
            e = h_ref[...] + u * r * pg_ref[...] - t_ref[...]
            dh_ref[...] = e * (1.0 / D_MODEL)
            l_ref[...] += jnp.sum(e * e) * (0.5 / D_MODEL)

    row = pl.BlockSpec((TILE, D_MODEL), lambda i: (i, 0))
    vec = pl.BlockSpec((1, D_MODEL), lambda i: (0, 0))
    return pl.pallas_call(
        body, name=name, grid=(t // TILE,),
        in_specs=[row, row, row, pl.BlockSpec((D_MODEL, D_MODEL), lambda i: (0, 0)), vec,
                  pl.BlockSpec((TILE, D_MODEL), lambda i: ((i // nq) * (nq - 1) + jnp.maximum(i % nq - 1, 0), 0))],
        out_specs=[row, row, pl.BlockSpec((8, 128), lambda i: (0, 0))],
        out_shape=[jax.ShapeDtypeStruct((t, D_MODEL), F32)] * 2 + [jax.ShapeDtypeStruct((8, 128), F32)],
        compiler_params=_params(("arbitrary",)),
    )(o, gate, h_in, w_out, post_gain, target)


def _out_bwd(dh, u, o, gate, w_out, post_gain, head_gain, name, narrow_do):
    t = o.shape[0]
    tm = _row_tile(t, 512)
    has_head = head_gain is not None

    def body(*refs):
        if has_head:
            dh_ref, u_ref, o_ref, g_ref, w_ref, pg_ref, hg_ref, do_ref, dg_ref, gw_ref, gw16_ref, gp_ref, gh_ref = refs
            hg = hg_ref[...]
        else:
            dh_ref, u_ref, o_ref, g_ref, w_ref, pg_ref, do_ref, dg_ref, gw_ref, gw16_ref, gp_ref = refs
            hg = None
        first = pl.program_id(0) == 0

        @pl.when(first)
        def _():
            gw_ref[...] = jnp.zeros_like(gw_ref)
            gp_ref[...] = jnp.zeros_like(gp_ref)
            if has_head:
                gh_ref[...] = jnp.zeros_like(gh_ref)

        dr, u, o, gate = dh_ref[...], u_ref[...], o_ref[...].astype(F32), g_ref[...].astype(F32)
        r = lax.rsqrt(jnp.mean(u * u, axis=-1, keepdims=True) + EPS)
        un = u * r
        gp_ref[...] += jnp.sum(dr * un, axis=0, keepdims=True)
        dun = dr * pg_ref[...]
        du = r * (dun - un * jnp.mean(dun * un, axis=-1, keepdims=True))
        on, mix = _mix(o, gate, hg)
        du16 = du.astype(_MXU)
        gw_ref[...] += lax.dot_general(mix.astype(_MXU), du16, _TN, preferred_element_type=F32)
        dmix = lax.dot_general(du16, w_ref[...], _NT, preferred_element_type=F32)
        sg = _sigmoid(gate)
        dg_ref[...] = (dmix * on * (sg * (1.0 + gate * (1.0 - sg)))).astype(dg_ref.dtype)
        don = dmix * (gate * sg)
        if has_head:
            outs, rs = _head_norm(o, hg)
            gh = jnp.zeros((1, D_HEAD), F32)
            cols = []
            for h in range(N_HEADS):
                dn = don[:, h * D_HEAD:(h + 1) * D_HEAD]
                gh = gh + jnp.sum(dn * outs[h], axis=0, keepdims=True)
                dnn = dn * hg
                cols.append(rs[h] * (dnn - outs[h] * jnp.mean(dnn * outs[h], axis=-1, keepdims=True)))
            gh_ref[...] += gh
            do_ref[...] = jnp.concatenate(cols, axis=1)
        else:
            do_ref[...] = don.astype(do_ref.dtype)

        @pl.when(pl.program_id(0) == pl.num_programs(0) - 1)
        def _():
            gw16_ref[...] = gw_ref[...].astype(_MXU)

    row = pl.BlockSpec((tm, D_MODEL), lambda i: (i, 0))
    vec = pl.BlockSpec((1, D_MODEL), lambda i: (0, 0))
    mat = pl.BlockSpec((D_MODEL, D_MODEL), lambda i: (0, 0))
    in_specs = [row, row, row, row, mat, vec]
    args = [dh, u, o, gate, w_out, post_gain]
    out_specs = [row, row, mat, mat, vec]
    out_shape = [jax.ShapeDtypeStruct((t, D_MODEL), _MXU if narrow_do else F32),
                 jax.ShapeDtypeStruct((t, D_MODEL), _MXU)] + [jax.ShapeDtypeStruct((D_MODEL, D_MODEL), F32),
                                                                  jax.ShapeDtypeStruct((D_MODEL, D_MODEL), _MXU),
                                                                  jax.ShapeDtypeStruct((1, D_MODEL), F32)]
    if has_head:
        in_specs.append(pl.BlockSpec((1, D_HEAD), lambda i: (0, 0)))
        args.append(head_gain)
        out_specs.append(pl.BlockSpec((1, D_HEAD), lambda i: (0, 0)))
        out_shape.append(jax.ShapeDtypeStruct((1, D_HEAD), F32))
    return pl.pallas_call(
        body, name=name, grid=(t // tm,), in_specs=in_specs, out_specs=out_specs, out_shape=out_shape,
        compiler_params=_params(("arbitrary",)),
    )(*args)


def _proj_bwd(ds, w4, h_in, gain, dh_out, name, outgoing=(), nq=None):
    t = h_in.shape[0]
    tm = _row_tile(t, 256) if nq is None else TILE
    n_out = len(outgoing)
    n_dh = 1 if nq is None else 2

    def body(d0, d1, d2, d3, w_ref, h_ref, g_ref, dho_ref, *rest):
        send, dh_refs, gg_ref, rest = rest[:n_out], rest[n_out:n_out + n_dh], rest[n_out + n_dh], rest[n_out + n_dh + 1:]
        landed, sems = rest[:n_out], rest[n_out:]
        if n_out:
            first, last = _grid_ends(1)
            start, finish = _chip_exchange(send, landed, sems, slotted=True)
            pl.when(first)(start)

        @pl.when(pl.program_id(0) == 0)
        def _():
            gg_ref[...] = jnp.zeros_like(gg_ref)

        dy = jnp.zeros((tm, D_MODEL), F32)
        for j, d in enumerate((d0, d1, d2, d3)):
            dy = dy + lax.dot_general(d[...].astype(_MXU), w_ref[j], _NT, preferred_element_type=F32)
        x = h_ref[...]
        r = lax.rsqrt(jnp.mean(x * x, axis=-1, keepdims=True) + EPS)
        xn = x * r
        gg_ref[...] += jnp.sum(dy * xn, axis=0, keepdims=True)
        dxn = dy * g_ref[...]
        dh = dho_ref[...] + r * (dxn - xn * jnp.mean(dxn * xn, axis=-1, keepdims=True))
        if nq is None:
            dh_refs[0][...] = dh
        else:
            in_front = pl.program_id(0) % nq == 0

            @pl.when(in_front)
            def _():
                dh_refs[1][...] = dh

            @pl.when(jnp.logical_not(in_front))
            def _():
                dh_refs[0][...] = dh
        if n_out:
            pl.when(last)(finish)

    row = pl.BlockSpec((tm, D_MODEL), lambda i: (i, 0))
    vec = pl.BlockSpec((1, D_MODEL), lambda i: (0, 0))
    hbm = pl.BlockSpec(memory_space=pl.ANY)
    if nq is None:
        dh_specs, dh_shapes = [row], [jax.ShapeDtypeStruct((t, D_MODEL), F32)]
    else:
        dh_specs = [pl.BlockSpec((TILE, D_MODEL), lambda i: ((i // nq) * (nq - 1) + jnp.maximum(i % nq - 1, 0), 0)),
                    pl.BlockSpec((TILE, D_MODEL), lambda i: (i // nq, 0))]
        dh_shapes = [jax.ShapeDtypeStruct((t // nq * (nq - 1), D_MODEL), F32),
                     jax.ShapeDtypeStruct((t // nq, D_MODEL), F32)]
    return pl.pallas_call(
        body, name=name, grid=(t // tm,),
        in_specs=[row] * 4 + [pl.BlockSpec((4, D_MODEL, D_MODEL), lambda i: (0, 0, 0)), row, vec, row] + [hbm] * n_out,
        out_specs=dh_specs + [vec] + [hbm] * n_out,
        out_shape=dh_shapes + [jax.ShapeDtypeStruct((1, D_MODEL), F32)]
        + [jax.ShapeDtypeStruct(a.shape, a.dtype) for a in outgoing],
        scratch_shapes=_exchange_scratch(n_out) if n_out else [],
        compiler_params=_params(("arbitrary",)),
    )(*ds, w4, h_in, gain, dh_out, *outgoing)


def _weight_grad(y, d, name):
    t = y.shape[0]
    tk = _row_tile(t, 2176)

    def body(y_ref, d_ref, g_ref, g16_ref):
        @pl.when(pl.program_id(0) == 0)
        def _():
            g_ref[...] = jnp.zeros_like(g_ref)

        g_ref[...] += lax.dot_general(y_ref[...], d_ref[...].astype(_MXU), _TN, preferred_element_type=F32)

        @pl.when(pl.program_id(0) == pl.num_programs(0) - 1)
        def _():
            g16_ref[...] = g_ref[...].astype(_MXU)

    row = pl.BlockSpec((tk, D_MODEL), lambda i: (i, 0))
    mat = pl.BlockSpec((D_MODEL, D_MODEL), lambda i: (0, 0))
    return pl.pallas_call(
        body, name=name, grid=(t // tk,), in_specs=[row, row], out_specs=[mat, mat],
        out_shape=[jax.ShapeDtypeStruct((D_MODEL, D_MODEL), F32), jax.ShapeDtypeStruct((D_MODEL, D_MODEL), _MXU)],
        compiler_params=_params(("arbitrary",)),
    )(y, d)


def _local_step(x, target, meta, pre_norm, post_norm, lam, head_gain, hw_in, shards):
    bsz, seq, _ = x.shape
    nq = seq // TILE + 1
    nb = nq * (TILE // BLOCK)
    lp = nq * TILE
    t = bsz * lp
    d4 = D_MODEL // N_CHIPS
    front = jnp.concatenate([jnp.zeros((N_PAD, D_MODEL), F32), meta], axis=0)
    h0 = jnp.concatenate([jnp.broadcast_to(front[None], (bsz, TILE, D_MODEL)), x], axis=1).reshape(t, D_MODEL)
    pre0, pre1, post0, post1 = pre_norm[0:1], pre_norm[1:2], post_norm[0:1], post_norm[1:2]

    y0, q0, f0, v0, g0 = _norm_proj(h0, pre0, hw_in, "norm_proj_hgrn", (False, False, False, True))
    o0, sst, sym, sw_in, sw_out, hw_out = _hgrn_fwd(q0, f0, v0, lam, bsz, nb, shards)
    sw_out, hw_out = sw_out.reshape(D_MODEL, D_MODEL), hw_out.reshape(D_MODEL, D_MODEL)
    h1, u0 = _out_fwd(o0, g0, h0, hw_out, post0, head_gain, "out_fwd_hgrn")
    y1, q1, k1, v1, g1 = _norm_proj(h1, pre1, sw_in, "norm_proj_sb", (True,) * 4)
    o1, ctot, visited = _sb_fwd(q1, k1, v1, bsz, nq)
    dh2, u1, loss_blk = _out_fwd_loss(o1, g1, h1, sw_out, post1, target.reshape(bsz * seq, D_MODEL), nq, "out_fwd_sb")

    do1, dg1, g_sw_out, g_sw_out16, g_post1 = _out_bwd(dh2, u1, o1, g1, sw_out, post1, None, "out_bwd_sb", True)
    dq1, dk1, dv1 = _sb_bwd(q1, k1, v1, ctot, visited, do1, bsz, nq)
    ds1 = (dq1, dk1, dv1, dg1)
    dh1, g_pre1 = _proj_bwd(ds1, sw_in, h1, pre1, dh2, "proj_bwd_sb")
    g_sw_in = [_weight_grad(y1, d, "wgrad_sb_%d" % j) for j, d in enumerate(ds1)]

    do0, dg0, g_hw_out, g_hw_out16, g_post0, g_head = _out_bwd(dh1, u0, o0, g0, hw_out, post0, head_gain, "out_bwd_hgrn",
                                                               False)
    ready = (jnp.stack([g16 for _, g16 in g_sw_in]), g_sw_out16.reshape(N_CHIPS, d4, D_MODEL),
             g_hw_out16.reshape(N_CHIPS, d4, D_MODEL))
    dq0, df0, dv0, dlam, land_sw_in, land_sw_out, land_hw_out = _hgrn_bwd(q0, f0, v0, lam, sst, sym, do0, bsz, nb, ready)
    ds0 = (dq0, df0, dv0, dg0)
    g_hw_in = [_weight_grad(y0, d, "wgrad_hgrn_%d" % j) for j, d in enumerate(ds0)]
    last = (jnp.stack([g16 for _, g16 in g_hw_in]),)
    grad_x, dh_front, g_pre0, land_hw_in = _proj_bwd(ds0, hw_in, h0, pre0, dh1, "proj_bwd_hgrn", last, nq)

    grad_x = grad_x.reshape(bsz, seq, D_MODEL)
    g_meta = jnp.sum(dh_front.reshape(bsz, TILE, D_MODEL)[:, N_PAD:, :], axis=0)
    g_lam = jnp.sum(dlam, axis=0)
    small = jnp.concatenate([g_pre0, g_pre1, g_post0, g_post1, g_lam, g_lam,
                             jnp.pad(g_head, ((0, 0), (0, D_MODEL - D_HEAD))), g_meta,
                             jnp.zeros((SMALL_ROWS - 23, D_MODEL), F32)], axis=0)
    rows4 = lambda g: [g[j * d4:(j + 1) * d4] for j in range(N_CHIPS)]
    large = dict(hw_in=(land_hw_in, [g for g, _ in g_hw_in]), sw_in=(land_sw_in, [g for g, _ in g_sw_in]),
                 hw_out=(land_hw_out, rows4(g_hw_out)), sw_out=(land_sw_out, rows4(g_sw_out)))
    return loss_blk, grad_x, small, large


def _prep_weights(hw_in, sw_in, hw_out, sw_out, meta):
    def body(hi_ref, si_ref, ho_ref, so_ref, m_ref, ghi, gm, si16, so16, ho16, far_send, far_recv, near_send, near_recv):
        x, y, c = _place()
        me = 2 * x + y
        ghi[me] = hi_ref[0].astype(_MXU)
        gm[me] = m_ref[...]
        si16[...] = si_ref[0].astype(_MXU)
        so16[...] = so_ref[0].astype(_MXU)
        ho16[...] = ho_ref[0].astype(_MXU)
        outs = (ghi, gm)
        n = len(outs)
        peers = [(1 - x, y), (x, 1 - y), (1 - x, 1 - y)]

        def half(a, slot, which):
            rows = outs[a].shape[1] // 2
            return outs[a].at[slot, pl.ds(which * rows, rows), :]

        def far(r, a, slot):
            px, py = peers[r]
            return pltpu.make_async_remote_copy(
                src_ref=half(a, slot, c), dst_ref=half(a, slot, c), send_sem=far_send.at[r * n + a],
                recv_sem=far_recv.at[r * n + a], device_id=(px, py, c), device_id_type=MESH)

        def near(r, a, which):
            px, py = peers[r]
            return pltpu.make_async_remote_copy(
                src_ref=half(a, 2 * px + py, which), dst_ref=half(a, 2 * px + py, which),
                send_sem=near_send.at[r * n + a], recv_sem=near_recv.at[r * n + a],
                device_id=(x, y, 1 - c), device_id_type=MESH)

        for r in range(3):
            for a in range(n):
                far(r, a, me).start()
        for r, (px, py) in enumerate(peers):
            for a in range(n):
                far(r, a, 2 * px + py).wait_recv()
                near(r, a, c).start()
        for r in range(3):
            for a in range(n):
                near(r, a, 1 - c).wait_recv()
        for r in range(3):
            for a in range(n):
                far(r, a, me).wait_send()
                near(r, a, c).wait_send()

    d4 = D_MODEL // N_CHIPS
    vm = pl.BlockSpec(memory_space=pltpu.VMEM)
    return pl.pallas_call(
        body, name="prep_weights",
        in_specs=[vm] * 5, out_specs=[vm] * 5,
        out_shape=[jax.ShapeDtypeStruct((N_CHIPS, D_MODEL, D_MODEL), _MXU), jax.ShapeDtypeStruct((N_CHIPS, N_META, d4), F32),
                   jax.ShapeDtypeStruct((D_MODEL, D_MODEL), _MXU), jax.ShapeDtypeStruct((d4, D_MODEL), _MXU),
                   jax.ShapeDtypeStruct((d4, D_MODEL), _MXU)],
        scratch_shapes=[pltpu.SemaphoreType.DMA((6,))] * 4,
        compiler_params=pltpu.CompilerParams(vmem_limit_bytes=VMEM_LIMIT),
    )(hw_in, sw_in, hw_out, sw_out, meta)


def _scatter_small(small):
    def body(sm, lsm, send_sems, recv_sems, local_sem):
        x, y, c = _place()
        mine = 4 * x + 2 * y + c
        local = pltpu.make_async_copy(sm, lsm.at[mine], local_sem)
        local.start()

        def copy(rel, src_dev, to):
            return pltpu.make_async_remote_copy(
                src_ref=sm, dst_ref=lsm.at[src_dev], send_sem=send_sems.at[rel - 1], recv_sem=recv_sems.at[rel - 1],
                device_id=to, device_id_type=MESH)

        flip = lambda bit, v: 1 - v if bit else v
        rels = [(rel, flip(rel & 4, x), flip(rel & 2, y), flip(rel & 1, c)) for rel in range(1, N_DEV)]
        sends = [copy(rel, mine, (px, py, pc)) for rel, px, py, pc in rels]
        for cp in sends:
            cp.start()
        for rel, px, py, pc in rels:
            copy(rel, 4 * px + 2 * py + pc, (px, py, pc)).wait_recv()
        for cp in sends:
            cp.wait_send()
        local.wait()

    hbm = pl.BlockSpec(memory_space=pl.ANY)
    return pl.pallas_call(
        body, name="scatter_small", in_specs=[hbm], out_specs=hbm,
        out_shape=jax.ShapeDtypeStruct((N_DEV, SMALL_ROWS, D_MODEL), F32),
        scratch_shapes=[pltpu.SemaphoreType.DMA((N_DEV - 1,)), pltpu.SemaphoreType.DMA((N_DEV - 1,)),
                        pltpu.SemaphoreType.DMA(())],
    )(small)


def _sum_slots(landed, own, me, name):
    n, rows, _ = landed.shape
    tm = rows if rows < 256 else 256

    def body(me_ref, l_ref, o0, o1, o2, o3, out_ref):
        acc = None
        for k, o in enumerate((o0, o1, o2, o3)):
            term = jnp.where(me_ref[0] == k, o[...], l_ref[k].astype(F32))
            acc = term if acc is None else acc + term
        out_ref[...] = acc

    blk = pl.BlockSpec((tm, D_MODEL), lambda i: (i, 0))
    return pl.pallas_call(
        body, name=name, grid=(rows // tm,),
        in_specs=[pl.BlockSpec(memory_space=pltpu.SMEM), pl.BlockSpec((n, tm, D_MODEL), lambda i: (0, i, 0))] + [blk] * 4,
        out_specs=blk, out_shape=jax.ShapeDtypeStruct((rows, D_MODEL), F32),
        compiler_params=_params(("arbitrary",)),
    )(me, landed, *own)


def _swap_with_sibling(parts):
    def body(a0, a1, a2, a3, b0, b1, b2, b3, send_sems, recv_sems):
        x, y, c = _place()
        copies = [pltpu.make_async_remote_copy(src_ref=s, dst_ref=d, send_sem=send_sems.at[a], recv_sem=recv_sems.at[a],
                                               device_id=(x, y, 1 - c), device_id_type=MESH)
                  for a, (s, d) in enumerate(zip((a0, a1, a2, a3), (b0, b1, b2, b3)))]
        for cp in copies:
            cp.start()
        for cp in copies:
            cp.wait()

    hbm = pl.BlockSpec(memory_space=pl.ANY)
    return pl.pallas_call(
        body, name="swap_with_sibling", in_specs=[hbm] * 4, out_specs=[hbm] * 4,
        out_shape=[jax.ShapeDtypeStruct(p.shape, F32) for p in parts],
        scratch_shapes=[pltpu.SemaphoreType.DMA((4,)), pltpu.SemaphoreType.DMA((4,))],
    )(*parts)


def _adamw_math(w, g, m, v):
    m = ADAM_B1 * m + (1.0 - ADAM_B1) * g
    v = ADAM_B2 * v + (1.0 - ADAM_B2) * (g * g)
    m_hat = m / (1.0 - ADAM_B1 ** ADAM_STEP)
    v_hat = v / (1.0 - ADAM_B2 ** ADAM_STEP)
    delta = -ADAM_LR * (m_hat / (jnp.sqrt(v_hat) + ADAM_EPS) + ADAM_WD * w)
    return delta, m, v


def _adamw(w, g_parts, m, v, name):
    rows, cols = w.shape
    tm = rows if rows < 256 else 256
    n = len(g_parts)

    def body(*refs):
        w_ref, m_ref, v_ref = refs[n:n + 3]
        g_ref, d_ref, nm_ref, nv_ref = refs[n + 3:]
        g = refs[0][...]
        for p in refs[1:n]:
            g = g + p[...]
        g_ref[...] = g
        d_ref[...], nm_ref[...], nv_ref[...] = _adamw_math(w_ref[...], g, m_ref[...], v_ref[...])

    blk = pl.BlockSpec((tm, cols), lambda i: (i, 0))
    return pl.pallas_call(
        body, name=name, grid=(rows // tm,), in_specs=[blk] * (n + 3), out_specs=[blk] * 4,
        out_shape=[jax.ShapeDtypeStruct((rows, cols), F32)] * 4,
        compiler_params=_params(("arbitrary",)),
    )(*g_parts, w, m, v)


def _lam_of(hgrn_lb):
    def body(lb_ref, o_ref):
        lb = lb_ref[...]
        e = jnp.exp(lb - jnp.max(lb, axis=0, keepdims=True))
        o_ref[...] = e[0:1, :] / jnp.sum(e, axis=0, keepdims=True)

    return pl.pallas_call(body, name="lam_of", out_shape=jax.ShapeDtypeStruct((1, D_MODEL), F32))(hgrn_lb)


def _small_grads(land_small, lam):
    def body(l_ref, lam_ref, o_ref):
        acc = l_ref[0]
        for k in range(1, N_DEV):
            acc = acc + l_ref[k]
        p = lam_ref[...]
        slope = p * (1.0 - p)
        row = _iota2((SMALL_ROWS, D_MODEL), 0)
        o_ref[...] = acc * jnp.where(row == 4, slope, jnp.where(row == 5, -slope, 1.0))

    return pl.pallas_call(body, name="small_grads",
                          out_shape=jax.ShapeDtypeStruct((SMALL_ROWS, D_MODEL), F32))(land_small, lam)


def kernel(x, meta_tokens, pre_norm, post_norm, hgrn_w_in, hgrn_lb, hgrn_out_norm, hgrn_w_out, sb_w_in, sb_w_out, loss_target, m_meta_tokens, m_pre_norm, m_post_norm, m_hgrn_w_in, m_hgrn_lb, m_hgrn_out_norm, m_hgrn_w_out, m_sb_w_in, m_sb_w_out, v_meta_tokens, v_pre_norm, v_post_norm, v_hgrn_w_in, v_hgrn_lb, v_hgrn_out_norm, v_hgrn_w_out, v_sb_w_in, v_sb_w_out):
    d4 = D_MODEL // N_CHIPS
    chip = 2 * lax.axis_index("x") + lax.axis_index("y")
    hw_in, meta4, sw_in16, sw_out16, hw_out16 = _prep_weights(hgrn_w_in, sb_w_in, hgrn_w_out, sb_w_out, meta_tokens)
    meta = meta4.transpose(1, 0, 2).reshape(N_META, D_MODEL)
    lam = _lam_of(hgrn_lb)
    loss_blk, grad_x, small, large = _local_step(
        x, loss_target, meta, pre_norm, post_norm, lam, hgrn_out_norm,
        hw_in, (sw_in16, sw_out16, hw_out16))
    loss = lax.psum(loss_blk[0, 0], ("x", "y", "c"))

    me = jnp.reshape(chip, (1,)).astype(jnp.int32)
    parts = [_sum_slots(*large[n], me, "sum_" + n) for n in ("hw_in", "sw_in", "hw_out", "sw_out")]
    sib = _swap_with_sibling(parts)
    small = _small_grads(_scatter_small(small), lam)

    res = {}
    res["hgrn_w_in"] = _adamw(hgrn_w_in[0], [parts[0], sib[0]], m_hgrn_w_in[0], v_hgrn_w_in[0], "adamw_hw_in")
    res["sb_w_in"] = _adamw(sb_w_in[0], [parts[1], sib[1]], m_sb_w_in[0], v_sb_w_in[0], "adamw_sw_in")
    res["hgrn_w_out"] = _adamw(hgrn_w_out[0], [parts[2], sib[2]], m_hgrn_w_out[0], v_hgrn_w_out[0], "adamw_hw_out")
    res["sb_w_out"] = _adamw(sb_w_out[0], [parts[3], sib[3]], m_sb_w_out[0], v_sb_w_out[0], "adamw_sw_out")
    res["pre_norm"] = _adamw(pre_norm, [small[0:2]], m_pre_norm, v_pre_norm, "adamw_pre")
    res["post_norm"] = _adamw(post_norm, [small[2:4]], m_post_norm, v_post_norm, "adamw_post")
    res["hgrn_lb"] = _adamw(hgrn_lb, [small[4:6]], m_hgrn_lb, v_hgrn_lb, "adamw_lb")
    res["hgrn_out_norm"] = _adamw(hgrn_out_norm, [small[6:7, :D_HEAD]], m_hgrn_out_norm, v_hgrn_out_norm, "adamw_head")
    g_meta = lax.dynamic_slice_in_dim(small[7:7 + N_META], chip * d4, d4, axis=1)
    res["meta_tokens"] = _adamw(meta_tokens, [g_meta], m_meta_tokens, v_meta_tokens, "adamw_meta")
    for n in ("hgrn_w_in", "hgrn_w_out", "sb_w_in", "sb_w_out"):
        res[n] = tuple(a[None] for a in res[n])
    order = ("meta_tokens", "pre_norm", "post_norm", "hgrn_w_in", "hgrn_lb", "hgrn_out_norm", "hgrn_w_out",
             "sb_w_in", "sb_w_out")
    return (loss, grad_x, *[res[n][0] for n in order], *[res[n][1] for n in order],
            *[res[n][2] for n in order], *[res[n][3] for n in order])
```

```python
import functools

import jax
import numpy as np
import jax.numpy as jnp
from jax import lax
from jax.experimental import pallas as pl
from jax.experimental.pallas import tpu as pltpu

F32 = jnp.float32
_MXU = jnp.bfloat16

D_MODEL = 1024
N_HEADS = 8
D_HEAD = 128
BLOCK = 128
N_META = 16
TILE = 256
N_PAD = TILE - N_META
UNDERFLOW = -105.0
EPS = 1e-6
SB_SCALE = D_HEAD ** -0.5
SOFTPLUS_LINEAR = 20.0
ADAM_LR, ADAM_B1, ADAM_B2, ADAM_EPS, ADAM_WD, ADAM_STEP = 0.001, 0.9, 0.999, 1e-08, 0.01, 10
N_CHIPS = 4
N_DEV = 8
SMALL_ROWS = 24
VMEM_LIMIT = 56 * 1024 * 1024
MESH = pl.DeviceIdType.MESH

_NT = (((1,), (1,)), ((), ()))
_TN = (((0,), (0,)), ((), ()))


def _mm(a, b):
    return jnp.dot(a.astype(_MXU), b.astype(_MXU), preferred_element_type=F32)


def _mm_nt(a, b):
    return lax.dot_general(a.astype(_MXU), b.astype(_MXU), _NT, preferred_element_type=F32)


def _mm_tn(a, b):
    return lax.dot_general(a.astype(_MXU), b.astype(_MXU), _TN, preferred_element_type=F32)


def _split2(x):
    hi = x.astype(_MXU)
    return hi, (x - hi.astype(F32)).astype(_MXU)


def _mm_s(a16, state):
    hi, lo = _split2(state)
    return jnp.dot(a16, hi, preferred_element_type=F32) + jnp.dot(a16, lo, preferred_element_type=F32)


def _mm_nt_s(a16, state):
    hi, lo = _split2(state)
    return (lax.dot_general(a16, hi, _NT, preferred_element_type=F32)
            + lax.dot_general(a16, lo, _NT, preferred_element_type=F32))


def _mm01_right(x, m01):
    return jnp.dot(x.astype(_MXU), m01, preferred_element_type=F32)


def _mm01_left(m01, x):
    hi, lo = _split2(x)
    return jnp.dot(m01, hi, preferred_element_type=F32) + jnp.dot(m01, lo, preferred_element_type=F32)


def _iota2(shape, dim):
    return lax.broadcasted_iota(jnp.int32, shape, dim)


def _row_tile(total, pref):
    t = pref
    while total % t:
        t -= BLOCK
    return t


def _params(sem, limit=VMEM_LIMIT):
    return pltpu.CompilerParams(dimension_semantics=sem, vmem_limit_bytes=limit)


def _sigmoid(x):
    return 1.0 / (1.0 + jnp.exp(-x))


def _grid_ends(ndim):
    first, last = True, True
    for d in range(ndim):
        first = first & (pl.program_id(d) == 0)
        last = last & (pl.program_id(d) == pl.num_programs(d) - 1)
    return first, last


def _place():
    return lax.axis_index("x"), lax.axis_index("y"), lax.axis_index("c")


def _exchange_scratch(n):
    return [pltpu.SemaphoreType.DMA((3 * n,)), pltpu.SemaphoreType.DMA((3 * n,)), pltpu.SemaphoreType.DMA((n,))]


def _chip_exchange(srcs, dsts, sems, slotted):
    send_sems, recv_sems, local_sems = sems
    x, y, c = _place()
    me = 2 * x + y
    peers = [(1 - x, y), (x, 1 - y), (1 - x, 1 - y)]
    n = len(dsts)

    def remote(r, a, sending):
        px, py = peers[r]
        p = 2 * px + py
        return pltpu.make_async_remote_copy(
            src_ref=srcs[a].at[p] if slotted else srcs[a], dst_ref=dsts[a].at[me if sending else p],
            send_sem=send_sems.at[r * n + a], recv_sem=recv_sems.at[r * n + a],
            device_id=(px, py, c), device_id_type=MESH)

    def local(a):
        return pltpu.make_async_copy(srcs[a].at[me] if slotted else srcs[a], dsts[a].at[me], local_sems.at[a])

    def start():
        for a in range(n):
            local(a).start()
        for r in range(3):
            for a in range(n):
                remote(r, a, True).start()

    def finish():
        for r in range(3):
            for a in range(n):
                remote(r, a, False).wait_recv()
        for r in range(3):
            for a in range(n):
                remote(r, a, True).wait_send()
        for a in range(n):
            local(a).wait()

    return start, finish


def _norm_proj(h, gain, w4, name, narrow):
    t = h.shape[0]
    tm = _row_tile(t, 256)

    def body(h_ref, g_ref, w_ref, y_ref, s0, s1, s2, s3):
        x = h_ref[...]
        r = lax.rsqrt(jnp.mean(x * x, axis=-1, keepdims=True) + EPS)
        y = (x * r * g_ref[...]).astype(_MXU)
        y_ref[...] = y
        for j, s in enumerate((s0, s1, s2, s3)):
            s[...] = jnp.dot(y, w_ref[j], preferred_element_type=F32).astype(s.dtype)

    row = pl.BlockSpec((tm, D_MODEL), lambda i: (i, 0))
    return pl.pallas_call(
        body, name=name, grid=(t // tm,),
        in_specs=[row, pl.BlockSpec((1, D_MODEL), lambda i: (0, 0)),
                  pl.BlockSpec((4, D_MODEL, D_MODEL), lambda i: (0, 0, 0))],
        out_specs=[row] * 5,
        out_shape=[jax.ShapeDtypeStruct((t, D_MODEL), _MXU)]
        + [jax.ShapeDtypeStruct((t, D_MODEL), _MXU if n else F32) for n in narrow],
        compiler_params=_params(("arbitrary",)),
    )(h, gain, w4)


LEVELS = (64, 32, 16, 8, 4, 2, 1)
HEAD_GROUP = 2


def _hgrn_tables():
    r = np.arange(BLOCK)
    mats = [r[None, :] <= r[:, None]]
    x = r[:, None] ^ r[None, :]
    lv = np.full((BLOCK, BLOCK), len(LEVELS), np.int32)
    for i, m in enumerate(LEVELS):
        lv[(x >= m) & (x < 2 * m)] = i
    return jnp.asarray(np.concatenate(mats, 0).astype(np.float32), dtype=_MXU), jnp.asarray(lv)


def _hgrn_exponents(g, sums):
    b = _mm01_left(sums, g)
    row = _iota2((BLOCK, D_HEAD), 0)
    out = []
    for m in LEVELS:
        is_q = (row & m) != 0
        if m >= 4:
            grp = b.reshape(BLOCK // (2 * m), 2 * m, D_HEAD)
            ref = jnp.broadcast_to(grp[:, m - 1:m, :], grp.shape).reshape(BLOCK, D_HEAD)
            d = b - ref
            out.append(jnp.where(is_q, d, -d))
        elif m == 2:
            below, above = pltpu.roll(g, 1, axis=0), pltpu.roll(g, BLOCK - 1, axis=0)
            low = row & 3
            out.append(jnp.where(low == 3, g + below, jnp.where(low == 2, g, jnp.where(low == 0, above, 0.0))))
        else:
            out.append(jnp.where(is_q, g, 0.0))
    return b, out


def _hgrn_gates(fz, lam, chunk):
    pos = chunk * BLOCK + _iota2((BLOCK, D_HEAD), 0)
    live = pos >= N_PAD
    sg = _sigmoid(fz)
    f = lam + (1.0 - lam) * sg
    g = jnp.where(live, jnp.log(f), 0.0)
    k = jnp.where(live, (1.0 - lam) * (1.0 - sg), 0.0)
    return sg, f, g, k, live


def _level_operand(q, k, exponent, m):
    decay = jnp.exp(exponent)
    is_q = (_iota2((BLOCK, D_HEAD), 0) & m) != 0
    return is_q, decay, (jnp.where(is_q, q, k) * decay).astype(_MXU)


def _hgrn_fwd(qs, fs, vs, lam, bsz, nb, shards):
    t = qs.shape[0]
    sums, levels = _hgrn_tables()
    width = HEAD_GROUP * D_HEAD

    n_sh = len(shards)

    def body(q_ref, f_ref, v_ref, lam_ref, sums_ref, lv_ref, *rest):
        own, (o_ref, sst_ref, sym_ref), rest = rest[:n_sh], rest[n_sh:n_sh + 3], rest[n_sh + 3:]
        gathered, st_scr, sems = rest[:n_sh], rest[n_sh], rest[n_sh + 1:]
        n = pl.program_id(2)
        first, last = _grid_ends(3)
        start, finish = _chip_exchange(own, gathered, sems, slotted=False)
        pl.when(first)(start)

        @pl.when(n == 0)
        def _():
            st_scr[...] = jnp.zeros_like(st_scr)

        lv = lv_ref[...]
        r, c = _iota2((BLOCK, BLOCK), 0), _iota2((BLOCK, BLOCK), 1)
        for hh in range(HEAD_GROUP):
            ls = slice(hh * D_HEAD, (hh + 1) * D_HEAD)
            st = st_scr[hh]
            sst_ref[0, hh, 0] = st
            q, v = q_ref[:, ls], v_ref[:, ls]
            _, _, g, k, _ = _hgrn_gates(f_ref[:, ls], lam_ref[:, ls], n)
            b, exps = _hgrn_exponents(g, sums_ref[...])
            sym = jnp.zeros((BLOCK, BLOCK), F32)
            for li, m in enumerate(LEVELS):
                _, _, x16 = _level_operand(q, k, exps[li], m)
                sym = jnp.where(lv == li, lax.dot_general(x16, x16, _NT, preferred_element_type=F32), sym)
            sym = jnp.where(c == r, jnp.sum(q * k, axis=1, keepdims=True), sym).astype(_MXU)
            sym_ref[0, hh, 0] = sym
            o_ref[:, ls] = _mm_nt(q * jnp.exp(b), st) + _mm(jnp.where(c <= r, sym, 0), v)
            b_end = b[BLOCK - 1:BLOCK, :]
            st_scr[hh] = st * jnp.exp(b_end) + _mm_tn(v, k * jnp.exp(b_end - b))
        pl.when(last)(finish)

    blk = pl.BlockSpec((BLOCK, width), lambda b, h, n: (b * nb + n, h))
    hbm = pl.BlockSpec(memory_space=pl.ANY)
    return pl.pallas_call(
        body, name="hgrn_fwd", grid=(bsz, N_HEADS // HEAD_GROUP, nb),
        in_specs=[blk, blk, blk, pl.BlockSpec((1, width), lambda b, h, n: (0, h)),
                  pl.BlockSpec(sums.shape, lambda b, h, n: (0, 0)), pl.BlockSpec(levels.shape, lambda b, h, n: (0, 0))]
        + [hbm] * n_sh,
        out_specs=[blk] + [pl.BlockSpec((1, HEAD_GROUP, 1, D_HEAD, D_HEAD), lambda b, h, n: (b, h, n, 0, 0))] * 2
        + [hbm] * n_sh,
        out_shape=[jax.ShapeDtypeStruct((t, D_MODEL), F32),
                   jax.ShapeDtypeStruct((bsz, N_HEADS, nb, D_HEAD, D_HEAD), F32),
                   jax.ShapeDtypeStruct((bsz, N_HEADS, nb, D_HEAD, D_HEAD), _MXU)]
        + [jax.ShapeDtypeStruct((N_CHIPS,) + a.shape, a.dtype) for a in shards],
        scratch_shapes=[pltpu.VMEM((HEAD_GROUP, D_HEAD, D_HEAD), F32)] + _exchange_scratch(n_sh),
        compiler_params=_params(("arbitrary", "arbitrary", "arbitrary")),
    )(qs, fs, vs, lam, sums, levels, *shards)


def _hgrn_bwd(qs, fs, vs, lam, sst, sym, do, bsz, nb, outgoing):
    t = qs.shape[0]
    sums, levels = _hgrn_tables()
    width = HEAD_GROUP * D_HEAD

    n_out = len(outgoing)

    def body(q_ref, f_ref, v_ref, lam_ref, sst_ref, sym_ref, do_ref, sums_ref, lv_ref, *rest):
        send, (dq_ref, df_ref, dv_ref, dlam_ref), rest = rest[:n_out], rest[n_out:n_out + 4], rest[n_out + 4:]
        landed, dst_scr, gsum_scr, sems = rest[:n_out], rest[n_out], rest[n_out + 1], rest[n_out + 2:]
        n = pl.program_id(2)
        chunk = nb - 1 - n
        first, last = _grid_ends(3)
        start, finish = _chip_exchange(send, landed, sems, slotted=True)
        pl.when(first)(start)

        @pl.when(n == 0)
        def _():
            dst_scr[...] = jnp.zeros_like(dst_scr)
            gsum_scr[...] = jnp.zeros_like(gsum_scr)
            dlam_ref[...] = jnp.zeros_like(dlam_ref)

        lv = lv_ref[...]
        r, c = _iota2((BLOCK, BLOCK), 0), _iota2((BLOCK, BLOCK), 1)
        for hh in range(HEAD_GROUP):
            ls = slice(hh * D_HEAD, (hh + 1) * D_HEAD)
            lam = lam_ref[:, ls]
            q, v, do = q_ref[:, ls], v_ref[:, ls], do_ref[:, ls]
            sg, f, g, k, live = _hgrn_gates(f_ref[:, ls], lam, chunk)
            b, exps = _hgrn_exponents(g, sums_ref[...])
            do16, v16 = do.astype(_MXU), v.astype(_MXU)
            da = lax.dot_general(do16, v16, _NT, preferred_element_type=F32)
            da_sym = jnp.where(c < r, da, da.T)
            dq = jnp.zeros((BLOCK, D_HEAD), F32)
            dqk = jnp.zeros((BLOCK, D_HEAD), F32)
            db_q = jnp.zeros((BLOCK, D_HEAD), F32)
            db_qk = jnp.zeros((BLOCK, D_HEAD), F32)
            for li, m in enumerate(LEVELS):
                is_q, decay, x16 = _level_operand(q, k, exps[li], m)
                y = jnp.dot(jnp.where(lv == li, da_sym, 0.0).astype(_MXU), x16, preferred_element_type=F32)
                dx = y * decay
                dq = dq + jnp.where(is_q, dx, 0.0)
                dqk = dqk + dx
                p = x16.astype(F32) * y
                db_q = db_q + jnp.where(is_q, p, 0.0)
                db_qk = db_qk + p
            dk = dqk - dq
            db = 2.0 * db_q - db_qk
            a_t = jnp.where(c >= r, sym_ref[0, hh, 0], 0)
            st, dst = sst_ref[0, hh, 0], dst_scr[hh]
            eb = jnp.exp(b)
            b_end = b[BLOCK - 1:BLOCK, :]
            dec = jnp.exp(b_end - b)
            qh16, kt16 = (q * eb).astype(_MXU), (k * dec).astype(_MXU)
            dq_st = _mm_s(do16, st)
            dk_st = _mm_s(v16, dst)
            d_diag = jnp.sum(do * v, axis=1, keepdims=True)
            dq_ref[:, ls] = (dq + d_diag * k + eb * dq_st).astype(dq_ref.dtype)
            dk = dk + d_diag * q + dec * dk_st
            dv_ref[:, ls] = (jnp.dot(a_t, do16, preferred_element_type=F32) + _mm_nt_s(kt16, dst)).astype(dv_ref.dtype)
            dst_scr[hh] = dst * jnp.exp(b_end) + lax.dot_general(do16, qh16, _TN, preferred_element_type=F32)
            db = db + (qh16.astype(F32) * dq_st - kt16.astype(F32) * dk_st)
            dg = _mm01_left((c >= r).astype(_MXU), db) + gsum_scr[:, ls]
            gsum_scr[:, ls] = gsum_scr[:, ls] + jnp.sum(db, axis=0, keepdims=True)
            slope = (1.0 - lam) * sg * (1.0 - sg)
            df_ref[:, ls] = jnp.where(live, dg * slope / f - dk * slope, 0.0).astype(df_ref.dtype)
            dl = jnp.where(live, (dg / f - dk) * (1.0 - sg), 0.0)
            dlam_ref[0, :, ls] = dlam_ref[0, :, ls] + jnp.sum(dl, axis=0, keepdims=True)
        pl.when(last)(finish)

    blk = pl.BlockSpec((BLOCK, width), lambda b, h, n: (b * nb + nb - 1 - n, h))
    hbm = pl.BlockSpec(memory_space=pl.ANY)
    return pl.pallas_call(
        body, name="hgrn_bwd", grid=(bsz, N_HEADS // HEAD_GROUP, nb),
        in_specs=[blk, blk, blk, pl.BlockSpec((1, width), lambda b, h, n: (0, h)),
                  pl.BlockSpec((1, HEAD_GROUP, 1, D_HEAD, D_HEAD), lambda b, h, n: (b, h, nb - 1 - n, 0, 0)),
                  pl.BlockSpec((1, HEAD_GROUP, 1, D_HEAD, D_HEAD), lambda b, h, n: (b, h, nb - 1 - n, 0, 0)),
                  blk, pl.BlockSpec(sums.shape, lambda b, h, n: (0, 0)), pl.BlockSpec(levels.shape, lambda b, h, n: (0, 0))]
        + [hbm] * n_out,
        out_specs=[blk, blk, blk, pl.BlockSpec((1, 1, width), lambda b, h, n: (b, 0, h))] + [hbm] * n_out,
        out_shape=[jax.ShapeDtypeStruct((t, D_MODEL), _MXU)] * 3 + [jax.ShapeDtypeStruct((bsz, 1, D_MODEL), F32)]
        + [jax.ShapeDtypeStruct(a.shape, a.dtype) for a in outgoing],
        scratch_shapes=[pltpu.VMEM((HEAD_GROUP, D_HEAD, D_HEAD), F32), pltpu.VMEM((1, width), F32)]
        + _exchange_scratch(n_out),
        compiler_params=_params(("arbitrary", "arbitrary", "arbitrary")),
    )(qs, fs, vs, lam, sst, sym, do, sums, levels, *outgoing)


def _sb_valid(ahead, col, i, j):
    return (ahead < (i - j) * TILE) & (col >= N_PAD - j * TILE)


def _sb_logits(q16, k_blk, valid):
    z = lax.dot_general(q16, k_blk.astype(_MXU), _NT, preferred_element_type=F32) * SB_SCALE
    softplus = jnp.where(z > SOFTPLUS_LINEAR, z, jnp.log(1.0 + jnp.exp(jnp.minimum(z, SOFTPLUS_LINEAR))))
    return jnp.where(valid, -softplus, 0.0), z - softplus


def _sb_fwd(qs, ks, vs, bsz, nq):
    t = qs.shape[0]
    lp = nq * TILE
    width = HEAD_GROUP * D_HEAD
    groups = N_HEADS // HEAD_GROUP
    lanes = [slice(hh * D_HEAD, (hh + 1) * D_HEAD) for hh in range(HEAD_GROUP)]

    def body(q_ref, k_ref, v_ref, o_ref, c_ref, n_ref):
        b, h, i = pl.program_id(0), pl.program_id(1), pl.program_id(2)
        q16 = [q_ref[:, ls].astype(_MXU) for ls in lanes]
        r, c = _iota2((TILE, TILE), 0), _iota2((TILE, TILE), 1)
        after = (r > c).astype(_MXU)

        def more(carry):
            jj, _, _, top = carry
            return (jj <= i) & (top > UNDERFLOW)

        def step(carry):
            jj, accs, sums, _ = carry
            j = i - jj
            ks_ = pl.ds(pl.multiple_of(j * TILE, TILE), TILE)
            valid = _sb_valid(c - r, c, i, j)
            new_accs, new_sums = [], []
            for hh, ls in enumerate(lanes):
                keep, log_beta = _sb_logits(q16[hh], k_ref[ks_, ls], valid)
                after_s = _mm01_right(keep, after)
                a = jnp.where(valid, jnp.exp(log_beta + (sums[hh] + after_s)), 0.0)
                new_accs.append(accs[hh] + _mm(a, v_ref[ks_, ls]))
                new_sums.append(sums[hh] + (after_s[:, 0:1] + keep[:, 0:1]))
            top = functools.reduce(jnp.maximum, [jnp.max(x) for x in new_sums])
            return jj + 1, tuple(new_accs), tuple(new_sums), top

        init = (jnp.int32(0), tuple(jnp.zeros((TILE, D_HEAD), F32) for _ in lanes),
                tuple(jnp.zeros((TILE, 1), F32) for _ in lanes), jnp.float32(0.0))
        visited, accs, sums, _ = lax.while_loop(more, step, init)
        for hh, ls in enumerate(lanes):
            o_ref[:, ls] = accs[hh]
            c_ref[:, ls] = jnp.broadcast_to(sums[hh], (TILE, D_HEAD))
        n_ref[(b * groups + h) * nq + i] = visited.astype(F32)

    blk = pl.BlockSpec((TILE, width), lambda b, h, i: (b * nq + i, h))
    seq = pl.BlockSpec((lp, width), lambda b, h, i: (b, h))
    return pl.pallas_call(
        body, name="sb_fwd", grid=(bsz, groups, nq),
        in_specs=[blk, seq, seq], out_specs=[blk, blk, pl.BlockSpec(memory_space=pltpu.SMEM)],
        out_shape=[jax.ShapeDtypeStruct((t, D_MODEL), F32)] * 2 + [jax.ShapeDtypeStruct((bsz * groups * nq,), F32)],
        compiler_params=_params(("arbitrary", "arbitrary", "arbitrary")),
    )(qs, ks, vs)


def _sb_bwd(qs, ks, vs, ctot, visited, do, bsz, nq):
    t = qs.shape[0]
    lp = nq * TILE
    width = HEAD_GROUP * D_HEAD
    groups = N_HEADS // HEAD_GROUP
    lanes = [slice(hh * D_HEAD, (hh + 1) * D_HEAD) for hh in range(HEAD_GROUP)]

    def body(n_ref, q_ref, k_ref, v_ref, c_ref, do_ref, dq_ref, dk_ref, dv_ref, dk_acc, dv_acc):
        b, h, i = pl.program_id(0), pl.program_id(1), pl.program_id(2)

        @pl.when(i == 0)
        def _():
            dk_acc[...] = jnp.zeros_like(dk_acc)
            dv_acc[...] = jnp.zeros_like(dv_acc)

        q16 = [q_ref[:, ls].astype(_MXU) for ls in lanes]
        do16 = [do_ref[:, ls].astype(_MXU) for ls in lanes]
        totals = [c_ref[:, hh * D_HEAD:hh * D_HEAD + 1] for hh in range(HEAD_GROUP)]
        r, c = _iota2((TILE, TILE), 0), _iota2((TILE, TILE), 1)
        upto = (r <= c).astype(_MXU)
        before = (r < c).astype(_MXU)
        first = jnp.maximum(i + 1 - n_ref[(b * groups + h) * nq + i].astype(jnp.int32), 0)

        def step(j, carry):
            ks_ = pl.ds(pl.multiple_of(j * TILE, TILE), TILE)
            valid = _sb_valid(c - r, c, i, j)
            out = []
            for hh, ls in enumerate(lanes):
                dq, keep_pre, g_pre = carry[hh]
                k_blk, v_blk = k_ref[ks_, ls], v_ref[ks_, ls]
                keep, log_beta = _sb_logits(q16[hh], k_blk, valid)
                keep_upto = _mm01_right(keep, upto)
                a = jnp.where(valid, jnp.exp(log_beta + (totals[hh] - keep_pre - keep_upto)), 0.0)
                da = lax.dot_general(do16[hh], v_blk.astype(_MXU), _NT, preferred_element_type=F32)
                g = a * da
                g_inside = _mm01_right(g, before)
                g_before = g_pre + g_inside
                beta = jnp.exp(log_beta)
                dz = jnp.where(valid, g * (1.0 - beta) - beta * g_before, 0.0) * SB_SCALE
                dz16 = dz.astype(_MXU)
                dq = dq + jnp.dot(dz16, k_blk.astype(_MXU), preferred_element_type=F32)
                dk_acc[ks_, ls] += lax.dot_general(dz16, q16[hh], _TN, preferred_element_type=F32)
                dv_acc[ks_, ls] += lax.dot_general(a.astype(_MXU), do16[hh], _TN, preferred_element_type=F32)
                out.append((dq, keep_pre + keep_upto[:, TILE - 1:TILE],
                            g_pre + (g_inside[:, TILE - 1:TILE] + g[:, TILE - 1:TILE])))
            return tuple(out)

        zero_col = jnp.zeros((TILE, 1), F32)
        init = tuple((jnp.zeros((TILE, D_HEAD), F32), zero_col, zero_col) for _ in lanes)
        res = lax.fori_loop(first, i + 1, step, init)
        for hh, ls in enumerate(lanes):
            dq_ref[:, ls] = res[hh][0].astype(dq_ref.dtype)

        @pl.when(i == nq - 1)
        def _():
            dk_ref[...] = dk_acc[...].astype(dk_ref.dtype)
            dv_ref[...] = dv_acc[...].astype(dv_ref.dtype)

    blk = pl.BlockSpec((TILE, width), lambda b, h, i: (b * nq + i, h))
    seq = pl.BlockSpec((lp, width), lambda b, h, i: (b, h))
    return pl.pallas_call(
        body, name="sb_bwd", grid=(bsz, groups, nq),
        in_specs=[pl.BlockSpec(memory_space=pltpu.SMEM), blk, seq, seq, blk, blk], out_specs=[blk, seq, seq],
        out_shape=[jax.ShapeDtypeStruct((t, D_MODEL), _MXU)] * 3,
        scratch_shapes=[pltpu.VMEM((lp, width), F32)] * 2,
        compiler_params=_params(("arbitrary", "arbitrary", "arbitrary")),
    )(visited, qs, ks, vs, ctot, do)


def _head_norm(o, head_gain):
    outs, rs = [], []
    for h in range(N_HEADS):
        oh = o[:, h * D_HEAD:(h + 1) * D_HEAD]
        r = lax.rsqrt(jnp.mean(oh * oh, axis=-1, keepdims=True) + EPS)
        outs.append(oh * r)
        rs.append(r)
    return outs, rs


def _mix(o, gate, head_gain):
    if head_gain is None:
        on = o
    else:
        outs, _ = _head_norm(o, head_gain)
        on = jnp.concatenate([x * head_gain for x in outs], axis=1)
    return on, on * (gate * _sigmoid(gate))


def _out_fwd(o, gate, h_in, w_out, post_gain, head_gain, name):
    t = o.shape[0]
    tm = _row_tile(t, 256)

    def body(o_ref, g_ref, h_ref, w_ref, pg_ref, hg_ref, ho_ref, u_ref):
        _, mix = _mix(o_ref[...], g_ref[...], hg_ref[...])
        u = jnp.dot(mix.astype(_MXU), w_ref[...], preferred_element_type=F32)
        u_ref[...] = u
        r = lax.rsqrt(jnp.mean(u * u, axis=-1, keepdims=True) + EPS)
        ho_ref[...] = h_ref[...] + u * r * pg_ref[...]

    row = pl.BlockSpec((tm, D_MODEL), lambda i: (i, 0))
    vec = pl.BlockSpec((1, D_MODEL), lambda i: (0, 0))
    return pl.pallas_call(
        body, name=name, grid=(t // tm,),
        in_specs=[row, row, row, pl.BlockSpec((D_MODEL, D_MODEL), lambda i: (0, 0)), vec,
                  pl.BlockSpec((1, D_HEAD), lambda i: (0, 0))],
        out_specs=[row, row], out_shape=[jax.ShapeDtypeStruct((t, D_MODEL), F32)] * 2,
        compiler_params=_params(("arbitrary",)),
    )(o, gate, h_in, w_out, post_gain, head_gain)


def _out_fwd_loss(o, gate, h_in, w_out, post_gain, target, nq, name):
    t = o.shape[0]

    def body(o_ref, g_ref, h_ref, w_ref, pg_ref, t_ref, dh_ref, u_ref, l_ref):
        i = pl.program_id(0)

        @pl.when(i == 0)
        def _():
            l_ref[...] = jnp.zeros_like(l_ref)

        _, mix = _mix(o_ref[...], g_ref[...], None)
        u = jnp.dot(mix.astype(_MXU), w_ref[...], preferred_element_type=F32)
        u_ref[...] = u

        @pl.when(i % nq == 0)
        def _():
            dh_ref[...] = jnp.zeros_like(dh_ref)

        @pl.when(i % nq != 0)
        def _():
            r = lax.rsqrt(jnp.mean(u * u, axis=-1, keepdims=True) + EPS)
            e = h_ref[...] + u * r * pg_ref[...] - t_ref[...]
            dh_ref[...] = e * (1.0 / D_MODEL)
            l_ref[...] += jnp.sum(e * e) * (0.5 / D_MODEL)

    row = pl.BlockSpec((TILE, D_MODEL), lambda i: (i, 0))
    vec = pl.BlockSpec((1, D_MODEL), lambda i: (0, 0))
    return pl.pallas_call(
        body, name=name, grid=(t // TILE,),
        in_specs=[row, row, row, pl.BlockSpec((D_MODEL, D_MODEL), lambda i: (0, 0)), vec,
                  pl.BlockSpec((TILE, D_MODEL), lambda i: ((i // nq) * (nq - 1) + jnp.maximum(i % nq - 1, 0), 0))],
        out_specs=[row, row, pl.BlockSpec((8, 128), lambda i: (0, 0))],
        out_shape=[jax.ShapeDtypeStruct((t, D_MODEL), F32)] * 2 + [jax.ShapeDtypeStruct((8, 128), F32)],
        compiler_params=_params(("arbitrary",)),
    )(o, gate, h_in, w_out, post_gain, target)


def _out_bwd(dh, u, o, gate, w_out, post_gain, head_gain, name, narrow_do):
    t = o.shape[0]
    tm = _row_tile(t, 2 * TILE)
    has_head = head_gain is not None

    def body(*refs):
        if has_head:
            dh_ref, u_ref, o_ref, g_ref, w_ref, pg_ref, hg_ref, do_ref, dg_ref, gw_ref, gw16_ref, gp_ref, gh_ref = refs
            hg = hg_ref[...]
        else:
            dh_ref, u_ref, o_ref, g_ref, w_ref, pg_ref, do_ref, dg_ref, gw_ref, gw16_ref, gp_ref = refs
            hg = None
        first = pl.program_id(0) == 0

        @pl.when(first)
        def _():
            gw_ref[...] = jnp.zeros_like(gw_ref)
            gp_ref[...] = jnp.zeros_like(gp_ref)
            if has_head:
                gh_ref[...] = jnp.zeros_like(gh_ref)

        dr, u, o, gate = dh_ref[...], u_ref[...], o_ref[...], g_ref[...]
        r = lax.rsqrt(jnp.mean(u * u, axis=-1, keepdims=True) + EPS)
        un = u * r
        gp_ref[...] += jnp.sum(dr * un, axis=0, keepdims=True)
        dun = dr * pg_ref[...]
        du = r * (dun - un * jnp.mean(dun * un, axis=-1, keepdims=True))
        on, mix = _mix(o, gate, hg)
        du16 = du.astype(_MXU)
        gw_ref[...] += lax.dot_general(mix.astype(_MXU), du16, _TN, preferred_element_type=F32)
        dmix = lax.dot_general(du16, w_ref[...], _NT, preferred_element_type=F32)
        sg = _sigmoid(gate)
        dg_ref[...] = (dmix * on * (sg * (1.0 + gate * (1.0 - sg)))).astype(dg_ref.dtype)
        don = dmix * (gate * sg)
        if has_head:
            outs, rs = _head_norm(o, hg)
            gh = jnp.zeros((1, D_HEAD), F32)
            cols = []
            for h in range(N_HEADS):
                dn = don[:, h * D_HEAD:(h + 1) * D_HEAD]
                gh = gh + jnp.sum(dn * outs[h], axis=0, keepdims=True)
                dnn = dn * hg
                cols.append(rs[h] * (dnn - outs[h] * jnp.mean(dnn * outs[h], axis=-1, keepdims=True)))
            gh_ref[...] += gh
            do_ref[...] = jnp.concatenate(cols, axis=1)
        else:
            do_ref[...] = don.astype(do_ref.dtype)

        @pl.when(pl.program_id(0) == pl.num_programs(0) - 1)
        def _():
            gw16_ref[...] = gw_ref[...].astype(_MXU)

    row = pl.BlockSpec((tm, D_MODEL), lambda i: (i, 0))
    vec = pl.BlockSpec((1, D_MODEL), lambda i: (0, 0))
    mat = pl.BlockSpec((D_MODEL, D_MODEL), lambda i: (0, 0))
    in_specs = [row, row, row, row, mat, vec]
    args = [dh, u, o, gate, w_out, post_gain]
    out_specs = [row, row, mat, mat, vec]
    out_shape = [jax.ShapeDtypeStruct((t, D_MODEL), _MXU if narrow_do else F32),
                 jax.ShapeDtypeStruct((t, D_MODEL), _MXU)] + [jax.ShapeDtypeStruct((D_MODEL, D_MODEL), F32),
                                                                  jax.ShapeDtypeStruct((D_MODEL, D_MODEL), _MXU),
                                                                  jax.ShapeDtypeStruct((1, D_MODEL), F32)]
    if has_head:
        in_specs.append(pl.BlockSpec((1, D_HEAD), lambda i: (0, 0)))
        args.append(head_gain)
        out_specs.append(pl.BlockSpec((1, D_HEAD), lambda i: (0, 0)))
        out_shape.append(jax.ShapeDtypeStruct((1, D_HEAD), F32))
    return pl.pallas_call(
        body, name=name, grid=(t // tm,), in_specs=in_specs, out_specs=out_specs, out_shape=out_shape,
        compiler_params=_params(("arbitrary",)),
    )(*args)


def _proj_bwd(ds, w4, h_in, gain, dh_out, name, outgoing=(), nq=None):
    t = h_in.shape[0]
    tm = _row_tile(t, 256) if nq is None else TILE
    n_out = len(outgoing)
    n_dh = 1 if nq is None else 2

    def body(d0, d1, d2, d3, w_ref, h_ref, g_ref, dho_ref, *rest):
        send, dh_refs, gg_ref, rest = rest[:n_out], rest[n_out:n_out + n_dh], rest[n_out + n_dh], rest[n_out + n_dh + 1:]
        landed, sems = rest[:n_out], rest[n_out:]
        if n_out:
            first, last = _grid_ends(1)
            start, finish = _chip_exchange(send, landed, sems, slotted=True)
            pl.when(first)(start)

        @pl.when(pl.program_id(0) == 0)
        def _():
            gg_ref[...] = jnp.zeros_like(gg_ref)

        dy = jnp.zeros((tm, D_MODEL), F32)
        for j, d in enumerate((d0, d1, d2, d3)):
            dy = dy + lax.dot_general(d[...].astype(_MXU), w_ref[j], _NT, preferred_element_type=F32)
        x = h_ref[...]
        r = lax.rsqrt(jnp.mean(x * x, axis=-1, keepdims=True) + EPS)
        xn = x * r
        gg_ref[...] += jnp.sum(dy * xn, axis=0, keepdims=True)
        dxn = dy * g_ref[...]
        dh = dho_ref[...] + r * (dxn - xn * jnp.mean(dxn * xn, axis=-1, keepdims=True))
        if nq is None:
            dh_refs[0][...] = dh
        else:
            in_front = pl.program_id(0) % nq == 0

            @pl.when(in_front)
            def _():
                dh_refs[1][...] = dh

            @pl.when(jnp.logical_not(in_front))
            def _():
                dh_refs[0][...] = dh
        if n_out:
            pl.when(last)(finish)

    row = pl.BlockSpec((tm, D_MODEL), lambda i: (i, 0))
    vec = pl.BlockSpec((1, D_MODEL), lambda i: (0, 0))
    hbm = pl.BlockSpec(memory_space=pl.ANY)
    if nq is None:
        dh_specs, dh_shapes = [row], [jax.ShapeDtypeStruct((t, D_MODEL), F32)]
    else:
        dh_specs = [pl.BlockSpec((TILE, D_MODEL), lambda i: ((i // nq) * (nq - 1) + jnp.maximum(i % nq - 1, 0), 0)),
                    pl.BlockSpec((TILE, D_MODEL), lambda i: (i // nq, 0))]
        dh_shapes = [jax.ShapeDtypeStruct((t // nq * (nq - 1), D_MODEL), F32),
                     jax.ShapeDtypeStruct((t // nq, D_MODEL), F32)]
    return pl.pallas_call(
        body, name=name, grid=(t // tm,),
        in_specs=[row] * 4 + [pl.BlockSpec((4, D_MODEL, D_MODEL), lambda i: (0, 0, 0)), row, vec, row] + [hbm] * n_out,
        out_specs=dh_specs + [vec] + [hbm] * n_out,
        out_shape=dh_shapes + [jax.ShapeDtypeStruct((1, D_MODEL), F32)]
        + [jax.ShapeDtypeStruct(a.shape, a.dtype) for a in outgoing],
        scratch_shapes=_exchange_scratch(n_out) if n_out else [],
        compiler_params=_params(("arbitrary",)),
    )(*ds, w4, h_in, gain, dh_out, *outgoing)


def _weight_grad(y, d, name):
    t = y.shape[0]
    tk = _row_tile(t, t // 4)

    def body(y_ref, d_ref, g_ref, g16_ref):
        @pl.when(pl.program_id(0) == 0)
        def _():
            g_ref[...] = jnp.zeros_like(g_ref)

        g_ref[...] += lax.dot_general(y_ref[...], d_ref[...].astype(_MXU), _TN, preferred_element_type=F32)

        @pl.when(pl.program_id(0) == pl.num_programs(0) - 1)
        def _():
            g16_ref[...] = g_ref[...].astype(_MXU)

    row = pl.BlockSpec((tk, D_MODEL), lambda i: (i, 0))
    mat = pl.BlockSpec((D_MODEL, D_MODEL), lambda i: (0, 0))
    return pl.pallas_call(
        body, name=name, grid=(t // tk,), in_specs=[row, row], out_specs=[mat, mat],
        out_shape=[jax.ShapeDtypeStruct((D_MODEL, D_MODEL), F32), jax.ShapeDtypeStruct((D_MODEL, D_MODEL), _MXU)],
        compiler_params=_params(("arbitrary",)),
    )(y, d)


def _local_step(x, target, meta, pre_norm, post_norm, lam, head_gain, hw_in, shards):
    bsz, seq, _ = x.shape
    nq = seq // TILE + 1
    nb = nq * (TILE // BLOCK)
    lp = nq * TILE
    t = bsz * lp
    d4 = D_MODEL // N_CHIPS
    front = jnp.concatenate([jnp.zeros((N_PAD, D_MODEL), F32), meta], axis=0)
    h0 = jnp.concatenate([jnp.broadcast_to(front[None], (bsz, TILE, D_MODEL)), x], axis=1).reshape(t, D_MODEL)
    pre0, pre1, post0, post1 = pre_norm[0:1], pre_norm[1:2], post_norm[0:1], post_norm[1:2]

    y0, q0, f0, v0, g0 = _norm_proj(h0, pre0, hw_in, "norm_proj_hgrn", (False,) * 4)
    o0, sst, sym, sw_in, sw_out, hw_out = _hgrn_fwd(q0, f0, v0, lam, bsz, nb, shards)
    sw_out, hw_out = sw_out.reshape(D_MODEL, D_MODEL), hw_out.reshape(D_MODEL, D_MODEL)
    h1, u0 = _out_fwd(o0, g0, h0, hw_out, post0, head_gain, "out_fwd_hgrn")
    y1, q1, k1, v1, g1 = _norm_proj(h1, pre1, sw_in, "norm_proj_sb", (True, True, True, False))
    o1, ctot, visited = _sb_fwd(q1, k1, v1, bsz, nq)
    dh2, u1, loss_blk = _out_fwd_loss(o1, g1, h1, sw_out, post1, target.reshape(bsz * seq, D_MODEL), nq, "out_fwd_sb")

    do1, dg1, g_sw_out, g_sw_out16, g_post1 = _out_bwd(dh2, u1, o1, g1, sw_out, post1, None, "out_bwd_sb", True)
    dq1, dk1, dv1 = _sb_bwd(q1, k1, v1, ctot, visited, do1, bsz, nq)
    ds1 = (dq1, dk1, dv1, dg1)
    dh1, g_pre1 = _proj_bwd(ds1, sw_in, h1, pre1, dh2, "proj_bwd_sb")
    g_sw_in = [_weight_grad(y1, d, "wgrad_sb_%d" % j) for j, d in enumerate(ds1)]

    do0, dg0, g_hw_out, g_hw_out16, g_post0, g_head = _out_bwd(dh1, u0, o0, g0, hw_out, post0, head_gain, "out_bwd_hgrn",
                                                               False)
    ready = (jnp.stack([g16 for _, g16 in g_sw_in]), g_sw_out16.reshape(N_CHIPS, d4, D_MODEL),
             g_hw_out16.reshape(N_CHIPS, d4, D_MODEL))
    dq0, df0, dv0, dlam, land_sw_in, land_sw_out, land_hw_out = _hgrn_bwd(q0, f0, v0, lam, sst, sym, do0, bsz, nb, ready)
    ds0 = (dq0, df0, dv0, dg0)
    g_hw_in = [_weight_grad(y0, d, "wgrad_hgrn_%d" % j) for j, d in enumerate(ds0)]
    last = (jnp.stack([g16 for _, g16 in g_hw_in]),)
    grad_x, dh_front, g_pre0, land_hw_in = _proj_bwd(ds0, hw_in, h0, pre0, dh1, "proj_bwd_hgrn", last, nq)

    grad_x = grad_x.reshape(bsz, seq, D_MODEL)
    g_meta = jnp.sum(dh_front.reshape(bsz, TILE, D_MODEL)[:, N_PAD:, :], axis=0)
    g_lam = jnp.sum(dlam, axis=0)
    small = jnp.concatenate([g_pre0, g_pre1, g_post0, g_post1, g_lam, g_lam,
                             jnp.pad(g_head, ((0, 0), (0, D_MODEL - D_HEAD))), g_meta,
                             jnp.pad(loss_blk[0:1], ((0, 0), (0, D_MODEL - loss_blk.shape[1])))], axis=0)
    rows4 = lambda g: [g[j * d4:(j + 1) * d4] for j in range(N_CHIPS)]
    large = dict(hw_in=(land_hw_in, [g for g, _ in g_hw_in]), sw_in=(land_sw_in, [g for g, _ in g_sw_in]),
                 hw_out=(land_hw_out, rows4(g_hw_out)), sw_out=(land_sw_out, rows4(g_sw_out)))
    return grad_x, small, large


def _prep_weights(hw_in, sw_in, hw_out, sw_out, meta):
    def body(hi_ref, si_ref, ho_ref, so_ref, m_ref, ghi, gm, si16, so16, ho16, far_send, far_recv, near_send, near_recv):
        x, y, c = _place()
        me = 2 * x + y
        ghi[me] = hi_ref[0].astype(_MXU)
        gm[me] = m_ref[...]
        si16[...] = si_ref[0].astype(_MXU)
        so16[...] = so_ref[0].astype(_MXU)
        ho16[...] = ho_ref[0].astype(_MXU)
        outs = (ghi, gm)
        n = len(outs)
        peers = [(1 - x, y), (x, 1 - y), (1 - x, 1 - y)]

        def half(a, slot, which):
            rows = outs[a].shape[1] // 2
            return outs[a].at[slot, pl.ds(which * rows, rows), :]

        def far(r, a, slot):
            px, py = peers[r]
            return pltpu.make_async_remote_copy(
                src_ref=half(a, slot, c), dst_ref=half(a, slot, c), send_sem=far_send.at[r * n + a],
                recv_sem=far_recv.at[r * n + a], device_id=(px, py, c), device_id_type=MESH)

        def near(r, a, which):
            px, py = peers[r]
            return pltpu.make_async_remote_copy(
                src_ref=half(a, 2 * px + py, which), dst_ref=half(a, 2 * px + py, which),
                send_sem=near_send.at[r * n + a], recv_sem=near_recv.at[r * n + a],
                device_id=(x, y, 1 - c), device_id_type=MESH)

        for r in range(3):
            for a in range(n):
                far(r, a, me).start()
        for r, (px, py) in enumerate(peers):
            for a in range(n):
                far(r, a, 2 * px + py).wait_recv()
                near(r, a, c).start()
        for r in range(3):
            for a in range(n):
                near(r, a, 1 - c).wait_recv()
        for r in range(3):
            for a in range(n):
                far(r, a, me).wait_send()
                near(r, a, c).wait_send()

    d4 = D_MODEL // N_CHIPS
    vm = pl.BlockSpec(memory_space=pltpu.VMEM)
    return pl.pallas_call(
        body, name="prep_weights",
        in_specs=[vm] * 5, out_specs=[vm] * 5,
        out_shape=[jax.ShapeDtypeStruct((N_CHIPS, D_MODEL, D_MODEL), _MXU), jax.ShapeDtypeStruct((N_CHIPS, N_META, d4), F32),
                   jax.ShapeDtypeStruct((D_MODEL, D_MODEL), _MXU), jax.ShapeDtypeStruct((d4, D_MODEL), _MXU),
                   jax.ShapeDtypeStruct((d4, D_MODEL), _MXU)],
        scratch_shapes=[pltpu.SemaphoreType.DMA((6,))] * 4,
        compiler_params=pltpu.CompilerParams(vmem_limit_bytes=VMEM_LIMIT),
    )(hw_in, sw_in, hw_out, sw_out, meta)


def _scatter_small(small):
    def body(sm, lsm, send_sems, recv_sems, local_sem):
        x, y, c = _place()
        mine = 4 * x + 2 * y + c
        local = pltpu.make_async_copy(sm, lsm.at[mine], local_sem)
        local.start()

        def copy(rel, src_dev, to):
            return pltpu.make_async_remote_copy(
                src_ref=sm, dst_ref=lsm.at[src_dev], send_sem=send_sems.at[rel - 1], recv_sem=recv_sems.at[rel - 1],
                device_id=to, device_id_type=MESH)

        flip = lambda bit, v: 1 - v if bit else v
        rels = [(rel, flip(rel & 4, x), flip(rel & 2, y), flip(rel & 1, c)) for rel in range(1, N_DEV)]
        sends = [copy(rel, mine, (px, py, pc)) for rel, px, py, pc in rels]
        for cp in sends:
            cp.start()
        for rel, px, py, pc in rels:
            copy(rel, 4 * px + 2 * py + pc, (px, py, pc)).wait_recv()
        for cp in sends:
            cp.wait_send()
        local.wait()

    hbm = pl.BlockSpec(memory_space=pl.ANY)
    return pl.pallas_call(
        body, name="scatter_small", in_specs=[hbm], out_specs=hbm,
        out_shape=jax.ShapeDtypeStruct((N_DEV, SMALL_ROWS, D_MODEL), F32),
        scratch_shapes=[pltpu.SemaphoreType.DMA((N_DEV - 1,)), pltpu.SemaphoreType.DMA((N_DEV - 1,)),
                        pltpu.SemaphoreType.DMA(())],
    )(small)


def _sum_slots(landed, own, me, name):
    n, rows, _ = landed.shape
    tm = rows if rows < 256 else 256

    def body(me_ref, l_ref, o0, o1, o2, o3, out_ref):
        acc = None
        for k, o in enumerate((o0, o1, o2, o3)):
            term = jnp.where(me_ref[0] == k, o[...], l_ref[k].astype(F32))
            acc = term if acc is None else acc + term
        out_ref[...] = acc

    blk = pl.BlockSpec((tm, D_MODEL), lambda i: (i, 0))
    return pl.pallas_call(
        body, name=name, grid=(rows // tm,),
        in_specs=[pl.BlockSpec(memory_space=pltpu.SMEM), pl.BlockSpec((n, tm, D_MODEL), lambda i: (0, i, 0))] + [blk] * 4,
        out_specs=blk, out_shape=jax.ShapeDtypeStruct((rows, D_MODEL), F32),
        compiler_params=_params(("arbitrary",)),
    )(me, landed, *own)


def _swap_with_sibling(parts):
    def body(a0, a1, a2, a3, b0, b1, b2, b3, send_sems, recv_sems):
        x, y, c = _place()
        copies = [pltpu.make_async_remote_copy(src_ref=s, dst_ref=d, send_sem=send_sems.at[a], recv_sem=recv_sems.at[a],
                                               device_id=(x, y, 1 - c), device_id_type=MESH)
                  for a, (s, d) in enumerate(zip((a0, a1, a2, a3), (b0, b1, b2, b3)))]
        for cp in copies:
            cp.start()
        for cp in copies:
            cp.wait()

    hbm = pl.BlockSpec(memory_space=pl.ANY)
    return pl.pallas_call(
        body, name="swap_with_sibling", in_specs=[hbm] * 4, out_specs=[hbm] * 4,
        out_shape=[jax.ShapeDtypeStruct(p.shape, F32) for p in parts],
        scratch_shapes=[pltpu.SemaphoreType.DMA((4,)), pltpu.SemaphoreType.DMA((4,))],
    )(*parts)


def _adamw_math(w, g, m, v):
    m = ADAM_B1 * m + (1.0 - ADAM_B1) * g
    v = ADAM_B2 * v + (1.0 - ADAM_B2) * (g * g)
    m_hat = m / (1.0 - ADAM_B1 ** ADAM_STEP)
    v_hat = v / (1.0 - ADAM_B2 ** ADAM_STEP)
    delta = -ADAM_LR * (m_hat / (jnp.sqrt(v_hat) + ADAM_EPS) + ADAM_WD * w)
    return delta, m, v


def _adamw(w, g_parts, m, v, name):
    rows, cols = w.shape
    tm = rows if rows < 256 else 256
    n = len(g_parts)

    def body(*refs):
        w_ref, m_ref, v_ref = refs[n:n + 3]
        g_ref, d_ref, nm_ref, nv_ref = refs[n + 3:]
        g = refs[0][...]
        for p in refs[1:n]:
            g = g + p[...]
        g_ref[...] = g
        d_ref[...], nm_ref[...], nv_ref[...] = _adamw_math(w_ref[...], g, m_ref[...], v_ref[...])

    blk = pl.BlockSpec((tm, cols), lambda i: (i, 0))
    return pl.pallas_call(
        body, name=name, grid=(rows // tm,), in_specs=[blk] * (n + 3), out_specs=[blk] * 4,
        out_shape=[jax.ShapeDtypeStruct((rows, cols), F32)] * 4,
        compiler_params=_params(("arbitrary",)),
    )(*g_parts, w, m, v)


def _lam_of(hgrn_lb):
    def body(lb_ref, o_ref):
        lb = lb_ref[...]
        e = jnp.exp(lb - jnp.max(lb, axis=0, keepdims=True))
        o_ref[...] = e[0:1, :] / jnp.sum(e, axis=0, keepdims=True)

    return pl.pallas_call(body, name="lam_of", out_shape=jax.ShapeDtypeStruct((1, D_MODEL), F32))(hgrn_lb)


def _small_grads(land_small, lam):
    def body(l_ref, lam_ref, o_ref):
        acc = l_ref[0]
        for k in range(1, N_DEV):
            acc = acc + l_ref[k]
        p = lam_ref[...]
        slope = p * (1.0 - p)
        row = _iota2((SMALL_ROWS, D_MODEL), 0)
        o_ref[...] = acc * jnp.where(row == 4, slope, jnp.where(row == 5, -slope, 1.0))

    return pl.pallas_call(body, name="small_grads",
                          out_shape=jax.ShapeDtypeStruct((SMALL_ROWS, D_MODEL), F32))(land_small, lam)


def kernel(x, meta_tokens, pre_norm, post_norm, hgrn_w_in, hgrn_lb, hgrn_out_norm, hgrn_w_out, sb_w_in, sb_w_out, loss_target, m_meta_tokens, m_pre_norm, m_post_norm, m_hgrn_w_in, m_hgrn_lb, m_hgrn_out_norm, m_hgrn_w_out, m_sb_w_in, m_sb_w_out, v_meta_tokens, v_pre_norm, v_post_norm, v_hgrn_w_in, v_hgrn_lb, v_hgrn_out_norm, v_hgrn_w_out, v_sb_w_in, v_sb_w_out):
    d4 = D_MODEL // N_CHIPS
    chip = 2 * lax.axis_index("x") + lax.axis_index("y")
    hw_in, meta4, sw_in16, sw_out16, hw_out16 = _prep_weights(hgrn_w_in, sb_w_in, hgrn_w_out, sb_w_out, meta_tokens)
    meta = meta4.transpose(1, 0, 2).reshape(N_META, D_MODEL)
    lam = _lam_of(hgrn_lb)
    grad_x, small, large = _local_step(
        x, loss_target, meta, pre_norm, post_norm, lam, hgrn_out_norm,
        hw_in, (sw_in16, sw_out16, hw_out16))

    me = jnp.reshape(chip, (1,)).astype(jnp.int32)
    parts = [_sum_slots(*large[n], me, "sum_" + n) for n in ("hw_in", "sw_in", "hw_out", "sw_out")]
    sib = _swap_with_sibling(parts)
    small = _small_grads(_scatter_small(small), lam)
    loss = small[SMALL_ROWS - 1, 0]

    res = {}
    res["hgrn_w_in"] = _adamw(hgrn_w_in[0], [parts[0], sib[0]], m_hgrn_w_in[0], v_hgrn_w_in[0], "adamw_hw_in")
    res["sb_w_in"] = _adamw(sb_w_in[0], [parts[1], sib[1]], m_sb_w_in[0], v_sb_w_in[0], "adamw_sw_in")
    res["hgrn_w_out"] = _adamw(hgrn_w_out[0], [parts[2], sib[2]], m_hgrn_w_out[0], v_hgrn_w_out[0], "adamw_hw_out")
    res["sb_w_out"] = _adamw(sb_w_out[0], [parts[3], sib[3]], m_sb_w_out[0], v_sb_w_out[0], "adamw_sw_out")
    res["pre_norm"] = _adamw(pre_norm, [small[0:2]], m_pre_norm, v_pre_norm, "adamw_pre")
    res["post_norm"] = _adamw(post_norm, [small[2:4]], m_post_norm, v_post_norm, "adamw_post")
    res["hgrn_lb"] = _adamw(hgrn_lb, [small[4:6]], m_hgrn_lb, v_hgrn_lb, "adamw_lb")
    res["hgrn_out_norm"] = _adamw(hgrn_out_norm, [small[6:7, :D_HEAD]], m_hgrn_out_norm, v_hgrn_out_norm, "adamw_head")
    g_meta = lax.dynamic_slice_in_dim(small[7:7 + N_META], chip * d4, d4, axis=1)
    res["meta_tokens"] = _adamw(meta_tokens, [g_meta], m_meta_tokens, v_meta_tokens, "adamw_meta")
    for n in ("hgrn_w_in", "hgrn_w_out", "sb_w_in", "sb_w_out"):
        res[n] = tuple(a[None] for a in res[n])
    order = ("meta_tokens", "pre_norm", "post_norm", "hgrn_w_in", "hgrn_lb", "hgrn_out_norm", "hgrn_w_out",
             "sb_w_in", "sb_w_out")
    return (loss, grad_x, *[res[n][0] for n in order], *[res[n][1] for n in order],
            *[res[n][2] for n in order], *[res[n][3] for n in order])
```

```python
import functools

import jax
import numpy as np
import jax.numpy as jnp
from jax import lax
from jax.experimental import pallas as pl
from jax.experimental.pallas import tpu as pltpu

F32 = jnp.float32
_MXU = jnp.bfloat16

D_MODEL = 1024
N_HEADS = 8
D_HEAD = 128
BLOCK = 128
N_META = 16
TILE = 256
N_PAD = TILE - N_META
UNDERFLOW = -105.0
EPS = 1e-6
SB_SCALE = D_HEAD ** -0.5
SOFTPLUS_LINEAR = 20.0
MASKED = -1e30
ADAM_LR, ADAM_B1, ADAM_B2, ADAM_EPS, ADAM_WD, ADAM_STEP = 0.001, 0.9, 0.999, 1e-08, 0.01, 10
N_CHIPS = 4
N_DEV = 8
SMALL_ROWS = 24
VMEM_LIMIT = 56 * 1024 * 1024
MESH = pl.DeviceIdType.MESH

_NT = (((1,), (1,)), ((), ()))
_TN = (((0,), (0,)), ((), ()))


def _mm(a, b):
    return jnp.dot(a.astype(_MXU), b.astype(_MXU), preferred_element_type=F32)


def _mm_nt(a, b):
    return lax.dot_general(a.astype(_MXU), b.astype(_MXU), _NT, preferred_element_type=F32)


def _mm_tn(a, b):
    return lax.dot_general(a.astype(_MXU), b.astype(_MXU), _TN, preferred_element_type=F32)


def _split2(x):
    hi = x.astype(_MXU)
    return hi, (x - hi.astype(F32)).astype(_MXU)


def _mm_s(a16, state):
    hi, lo = _split2(state)
    return jnp.dot(a16, hi, preferred_element_type=F32) + jnp.dot(a16, lo, preferred_element_type=F32)


def _mm_nt_s(a16, state):
    hi, lo = _split2(state)
    return (lax.dot_general(a16, hi, _NT, preferred_element_type=F32)
            + lax.dot_general(a16, lo, _NT, preferred_element_type=F32))


def _mm01_right(x, m01):
    return jnp.dot(x.astype(_MXU), m01, preferred_element_type=F32)


def _mm01_left(m01, x):
    hi, lo = _split2(x)
    return jnp.dot(m01, hi, preferred_element_type=F32) + jnp.dot(m01, lo, preferred_element_type=F32)


def _iota2(shape, dim):
    return lax.broadcasted_iota(jnp.int32, shape, dim)


def _row_tile(total, pref):
    t = pref
    while total % t:
        t -= BLOCK
    return t


def _params(sem, limit=VMEM_LIMIT):
    return pltpu.CompilerParams(dimension_semantics=sem, vmem_limit_bytes=limit)


def _sigmoid(x):
    return 1.0 / (1.0 + jnp.exp(-x))


def _grid_ends(ndim):
    first, last = True, True
    for d in range(ndim):
        first = first & (pl.program_id(d) == 0)
        last = last & (pl.program_id(d) == pl.num_programs(d) - 1)
    return first, last


def _place():
    return lax.axis_index("x"), lax.axis_index("y"), lax.axis_index("c")


def _exchange_scratch(n):
    return [pltpu.SemaphoreType.DMA((3 * n,)), pltpu.SemaphoreType.DMA((3 * n,)), pltpu.SemaphoreType.DMA((n,))]


def _chip_exchange(srcs, dsts, sems, slotted):
    send_sems, recv_sems, local_sems = sems
    x, y, c = _place()
    me = 2 * x + y
    peers = [(1 - x, y), (x, 1 - y), (1 - x, 1 - y)]
    n = len(dsts)

    def remote(r, a, sending):
        px, py = peers[r]
        p = 2 * px + py
        return pltpu.make_async_remote_copy(
            src_ref=srcs[a].at[p] if slotted else srcs[a], dst_ref=dsts[a].at[me if sending else p],
            send_sem=send_sems.at[r * n + a], recv_sem=recv_sems.at[r * n + a],
            device_id=(px, py, c), device_id_type=MESH)

    def local(a):
        return pltpu.make_async_copy(srcs[a].at[me] if slotted else srcs[a], dsts[a].at[me], local_sems.at[a])

    def start():
        for a in range(n):
            local(a).start()
        for r in range(3):
            for a in range(n):
                remote(r, a, True).start()

    def finish():
        for r in range(3):
            for a in range(n):
                remote(r, a, False).wait_recv()
        for r in range(3):
            for a in range(n):
                remote(r, a, True).wait_send()
        for a in range(n):
            local(a).wait()

    return start, finish


def _norm_proj(h, gain, w4, name, narrow):
    t = h.shape[0]
    tm = _row_tile(t, 256)

    def body(h_ref, g_ref, w_ref, y_ref, s0, s1, s2, s3):
        x = h_ref[...]
        r = lax.rsqrt(jnp.mean(x * x, axis=-1, keepdims=True) + EPS)
        y = (x * r * g_ref[...]).astype(_MXU)
        y_ref[...] = y
        for j, s in enumerate((s0, s1, s2, s3)):
            s[...] = jnp.dot(y, w_ref[j], preferred_element_type=F32).astype(s.dtype)

    row = pl.BlockSpec((tm, D_MODEL), lambda i: (i, 0))
    return pl.pallas_call(
        body, name=name, grid=(t // tm,),
        in_specs=[row, pl.BlockSpec((1, D_MODEL), lambda i: (0, 0)),
                  pl.BlockSpec((4, D_MODEL, D_MODEL), lambda i: (0, 0, 0))],
        out_specs=[row] * 5,
        out_shape=[jax.ShapeDtypeStruct((t, D_MODEL), _MXU)]
        + [jax.ShapeDtypeStruct((t, D_MODEL), _MXU if n else F32) for n in narrow],
        compiler_params=_params(("arbitrary",)),
    )(h, gain, w4)


LEVELS = (64, 32, 16, 8, 4, 2, 1)
HEAD_GROUP = 2


def _hgrn_tables():
    r = np.arange(BLOCK)
    mats = [r[None, :] <= r[:, None]]
    x = r[:, None] ^ r[None, :]
    lv = np.full((BLOCK, BLOCK), len(LEVELS), np.int32)
    for i, m in enumerate(LEVELS):
        lv[(x >= m) & (x < 2 * m)] = i
    return jnp.asarray(np.concatenate(mats, 0).astype(np.float32), dtype=_MXU), jnp.asarray(lv)


def _hgrn_exponents(g, sums):
    b = _mm01_left(sums, g)
    row = _iota2((BLOCK, D_HEAD), 0)
    out = []
    for m in LEVELS:
        is_q = (row & m) != 0
        if m >= 4:
            grp = b.reshape(BLOCK // (2 * m), 2 * m, D_HEAD)
            ref = jnp.broadcast_to(grp[:, m - 1:m, :], grp.shape).reshape(BLOCK, D_HEAD)
            d = b - ref
            out.append(jnp.where(is_q, d, -d))
        elif m == 2:
            below, above = pltpu.roll(g, 1, axis=0), pltpu.roll(g, BLOCK - 1, axis=0)
            low = row & 3
            out.append(jnp.where(low == 3, g + below, jnp.where(low == 2, g, jnp.where(low == 0, above, 0.0))))
        else:
            out.append(jnp.where(is_q, g, 0.0))
    return b, out


def _hgrn_gates(fz, lam, chunk):
    pos = chunk * BLOCK + _iota2((BLOCK, D_HEAD), 0)
    live = pos >= N_PAD
    sg = _sigmoid(fz)
    f = lam + (1.0 - lam) * sg
    g = jnp.where(live, jnp.log(f), 0.0)
    k = jnp.where(live, (1.0 - lam) * (1.0 - sg), 0.0)
    return sg, f, g, k, live


def _level_operand(q, k, exponent, m):
    decay = jnp.exp(exponent)
    is_q = (_iota2((BLOCK, D_HEAD), 0) & m) != 0
    return is_q, decay, (jnp.where(is_q, q, k) * decay).astype(_MXU)


def _hgrn_fwd(qs, fs, vs, lam, bsz, nb, shards):
    t = qs.shape[0]
    sums, levels = _hgrn_tables()
    width = HEAD_GROUP * D_HEAD

    n_sh = len(shards)

    def body(q_ref, f_ref, v_ref, lam_ref, sums_ref, lv_ref, *rest):
        own, (o_ref, sst_ref, sym_ref), rest = rest[:n_sh], rest[n_sh:n_sh + 3], rest[n_sh + 3:]
        gathered, st_scr, sems = rest[:n_sh], rest[n_sh], rest[n_sh + 1:]
        n = pl.program_id(2)
        first, last = _grid_ends(3)
        start, finish = _chip_exchange(own, gathered, sems, slotted=False)
        pl.when(first)(start)

        @pl.when(n == 0)
        def _():
            st_scr[...] = jnp.zeros_like(st_scr)

        lv = lv_ref[...]
        r, c = _iota2((BLOCK, BLOCK), 0), _iota2((BLOCK, BLOCK), 1)
        for hh in range(HEAD_GROUP):
            ls = slice(hh * D_HEAD, (hh + 1) * D_HEAD)
            st = st_scr[hh]
            sst_ref[0, hh, 0] = st
            q, v = q_ref[:, ls], v_ref[:, ls]
            _, _, g, k, _ = _hgrn_gates(f_ref[:, ls], lam_ref[:, ls], n)
            b, exps = _hgrn_exponents(g, sums_ref[...])
            sym = jnp.zeros((BLOCK, BLOCK), F32)
            for li, m in enumerate(LEVELS):
                _, _, x16 = _level_operand(q, k, exps[li], m)
                sym = jnp.where(lv == li, lax.dot_general(x16, x16, _NT, preferred_element_type=F32), sym)
            sym = jnp.where(c == r, jnp.sum(q * k, axis=1, keepdims=True), sym).astype(_MXU)
            sym_ref[0, hh, 0] = sym
            o_ref[:, ls] = _mm_nt(q * jnp.exp(b), st) + _mm(jnp.where(c <= r, sym, 0), v)
            b_end = b[BLOCK - 1:BLOCK, :]
            st_scr[hh] = st * jnp.exp(b_end) + _mm_tn(v, k * jnp.exp(b_end - b))
        pl.when(last)(finish)

    blk = pl.BlockSpec((BLOCK, width), lambda b, h, n: (b * nb + n, h))
    hbm = pl.BlockSpec(memory_space=pl.ANY)
    return pl.pallas_call(
        body, name="hgrn_fwd", grid=(bsz, N_HEADS // HEAD_GROUP, nb),
        in_specs=[blk, blk, blk, pl.BlockSpec((1, width), lambda b, h, n: (0, h)),
                  pl.BlockSpec(sums.shape, lambda b, h, n: (0, 0)), pl.BlockSpec(levels.shape, lambda b, h, n: (0, 0))]
        + [hbm] * n_sh,
        out_specs=[blk] + [pl.BlockSpec((1, HEAD_GROUP, 1, D_HEAD, D_HEAD), lambda b, h, n: (b, h, n, 0, 0))] * 2
        + [hbm] * n_sh,
        out_shape=[jax.ShapeDtypeStruct((t, D_MODEL), F32),
                   jax.ShapeDtypeStruct((bsz, N_HEADS, nb, D_HEAD, D_HEAD), F32),
                   jax.ShapeDtypeStruct((bsz, N_HEADS, nb, D_HEAD, D_HEAD), _MXU)]
        + [jax.ShapeDtypeStruct((N_CHIPS,) + a.shape, a.dtype) for a in shards],
        scratch_shapes=[pltpu.VMEM((HEAD_GROUP, D_HEAD, D_HEAD), F32)] + _exchange_scratch(n_sh),
        compiler_params=_params(("arbitrary", "arbitrary", "arbitrary")),
    )(qs, fs, vs, lam, sums, levels, *shards)


def _hgrn_bwd(qs, fs, vs, lam, sst, sym, do, bsz, nb, outgoing):
    t = qs.shape[0]
    sums, levels = _hgrn_tables()
    width = HEAD_GROUP * D_HEAD

    n_out = len(outgoing)

    def body(q_ref, f_ref, v_ref, lam_ref, sst_ref, sym_ref, do_ref, sums_ref, lv_ref, *rest):
        send, (dq_ref, df_ref, dv_ref, dlam_ref), rest = rest[:n_out], rest[n_out:n_out + 4], rest[n_out + 4:]
        landed, dst_scr, gsum_scr, sems = rest[:n_out], rest[n_out], rest[n_out + 1], rest[n_out + 2:]
        n = pl.program_id(2)
        chunk = nb - 1 - n
        first, last = _grid_ends(3)
        start, finish = _chip_exchange(send, landed, sems, slotted=True)
        pl.when(first)(start)

        @pl.when(n == 0)
        def _():
            dst_scr[...] = jnp.zeros_like(dst_scr)
            gsum_scr[...] = jnp.zeros_like(gsum_scr)
            dlam_ref[...] = jnp.zeros_like(dlam_ref)

        lv = lv_ref[...]
        r, c = _iota2((BLOCK, BLOCK), 0), _iota2((BLOCK, BLOCK), 1)
        for hh in range(HEAD_GROUP):
            ls = slice(hh * D_HEAD, (hh + 1) * D_HEAD)
            lam = lam_ref[:, ls]
            q, v, do = q_ref[:, ls], v_ref[:, ls], do_ref[:, ls]
            sg, f, g, k, live = _hgrn_gates(f_ref[:, ls], lam, chunk)
            b, exps = _hgrn_exponents(g, sums_ref[...])
            do16, v16 = do.astype(_MXU), v.astype(_MXU)
            da = lax.dot_general(do16, v16, _NT, preferred_element_type=F32)
            da_sym = jnp.where(c < r, da, da.T)
            dq = jnp.zeros((BLOCK, D_HEAD), F32)
            dqk = jnp.zeros((BLOCK, D_HEAD), F32)
            db_q = jnp.zeros((BLOCK, D_HEAD), F32)
            db_qk = jnp.zeros((BLOCK, D_HEAD), F32)
            for li, m in enumerate(LEVELS):
                is_q, decay, x16 = _level_operand(q, k, exps[li], m)
                y = jnp.dot(jnp.where(lv == li, da_sym, 0.0).astype(_MXU), x16, preferred_element_type=F32)
                dx = y * decay
                dq = dq + jnp.where(is_q, dx, 0.0)
                dqk = dqk + dx
                p = x16.astype(F32) * y
                db_q = db_q + jnp.where(is_q, p, 0.0)
                db_qk = db_qk + p
            dk = dqk - dq
            db = 2.0 * db_q - db_qk
            a_t = jnp.where(c >= r, sym_ref[0, hh, 0], 0)
            st, dst = sst_ref[0, hh, 0], dst_scr[hh]
            eb = jnp.exp(b)
            b_end = b[BLOCK - 1:BLOCK, :]
            dec = jnp.exp(b_end - b)
            qh16, kt16 = (q * eb).astype(_MXU), (k * dec).astype(_MXU)
            dq_st = _mm_s(do16, st)
            dk_st = _mm_s(v16, dst)
            d_diag = jnp.sum(do * v, axis=1, keepdims=True)
            dq_ref[:, ls] = (dq + d_diag * k + eb * dq_st).astype(dq_ref.dtype)
            dk = dk + d_diag * q + dec * dk_st
            dv_ref[:, ls] = (jnp.dot(a_t, do16, preferred_element_type=F32) + _mm_nt_s(kt16, dst)).astype(dv_ref.dtype)
            dst_scr[hh] = dst * jnp.exp(b_end) + lax.dot_general(do16, qh16, _TN, preferred_element_type=F32)
            db = db + (qh16.astype(F32) * dq_st - kt16.astype(F32) * dk_st)
            dg = _mm01_left((c >= r).astype(_MXU), db) + gsum_scr[:, ls]
            gsum_scr[:, ls] = gsum_scr[:, ls] + jnp.sum(db, axis=0, keepdims=True)
            slope = (1.0 - lam) * sg * (1.0 - sg)
            df_ref[:, ls] = jnp.where(live, dg * slope / f - dk * slope, 0.0).astype(df_ref.dtype)
            dl = jnp.where(live, (dg / f - dk) * (1.0 - sg), 0.0)
            dlam_ref[0, :, ls] = dlam_ref[0, :, ls] + jnp.sum(dl, axis=0, keepdims=True)
        pl.when(last)(finish)

    blk = pl.BlockSpec((BLOCK, width), lambda b, h, n: (b * nb + nb - 1 - n, h))
    hbm = pl.BlockSpec(memory_space=pl.ANY)
    return pl.pallas_call(
        body, name="hgrn_bwd", grid=(bsz, N_HEADS // HEAD_GROUP, nb),
        in_specs=[blk, blk, blk, pl.BlockSpec((1, width), lambda b, h, n: (0, h)),
                  pl.BlockSpec((1, HEAD_GROUP, 1, D_HEAD, D_HEAD), lambda b, h, n: (b, h, nb - 1 - n, 0, 0)),
                  pl.BlockSpec((1, HEAD_GROUP, 1, D_HEAD, D_HEAD), lambda b, h, n: (b, h, nb - 1 - n, 0, 0)),
                  blk, pl.BlockSpec(sums.shape, lambda b, h, n: (0, 0)), pl.BlockSpec(levels.shape, lambda b, h, n: (0, 0))]
        + [hbm] * n_out,
        out_specs=[blk, blk, blk, pl.BlockSpec((1, 1, width), lambda b, h, n: (b, 0, h))] + [hbm] * n_out,
        out_shape=[jax.ShapeDtypeStruct((t, D_MODEL), _MXU)] * 3 + [jax.ShapeDtypeStruct((bsz, 1, D_MODEL), F32)]
        + [jax.ShapeDtypeStruct(a.shape, a.dtype) for a in outgoing],
        scratch_shapes=[pltpu.VMEM((HEAD_GROUP, D_HEAD, D_HEAD), F32), pltpu.VMEM((1, width), F32)]
        + _exchange_scratch(n_out),
        compiler_params=_params(("arbitrary", "arbitrary", "arbitrary")),
    )(qs, fs, vs, lam, sst, sym, do, sums, levels, *outgoing)


def _sb_valid(ahead, col, i, j):
    return (ahead < (i - j) * TILE) & (col >= N_PAD - j * TILE)


def _sb_logits(q16, k_blk, valid):
    z = jnp.where(valid, lax.dot_general(q16, k_blk.astype(_MXU), _NT, preferred_element_type=F32) * SB_SCALE, MASKED)
    softplus = jnp.where(z > SOFTPLUS_LINEAR, z, jnp.log(1.0 + jnp.exp(jnp.minimum(z, SOFTPLUS_LINEAR))))
    return -softplus, z - softplus


def _sb_fwd(qs, ks, vs, bsz, nq):
    t = qs.shape[0]
    lp = nq * TILE
    width = HEAD_GROUP * D_HEAD
    groups = N_HEADS // HEAD_GROUP
    lanes = [slice(hh * D_HEAD, (hh + 1) * D_HEAD) for hh in range(HEAD_GROUP)]

    def body(q_ref, k_ref, v_ref, o_ref, c_ref, n_ref):
        b, h, i = pl.program_id(0), pl.program_id(1), pl.program_id(2)
        q16 = [q_ref[:, ls].astype(_MXU) for ls in lanes]
        r, c = _iota2((TILE, TILE), 0), _iota2((TILE, TILE), 1)
        after = (r > c).astype(_MXU)

        def more(carry):
            jj, _, _, top = carry
            return (jj <= i) & (top > UNDERFLOW)

        def step(carry):
            jj, accs, sums, _ = carry
            j = i - jj
            ks_ = pl.ds(pl.multiple_of(j * TILE, TILE), TILE)
            valid = _sb_valid(c - r, c, i, j)
            new_accs, new_sums = [], []
            for hh, ls in enumerate(lanes):
                keep, log_beta = _sb_logits(q16[hh], k_ref[ks_, ls], valid)
                after_s = _mm01_right(keep, after)
                a = jnp.exp(log_beta + (sums[hh] + after_s))
                new_accs.append(accs[hh] + _mm(a, v_ref[ks_, ls]))
                new_sums.append(sums[hh] + (after_s[:, 0:1] + keep[:, 0:1]))
            top = functools.reduce(jnp.maximum, [jnp.max(x) for x in new_sums])
            return jj + 1, tuple(new_accs), tuple(new_sums), top

        init = (jnp.int32(0), tuple(jnp.zeros((TILE, D_HEAD), F32) for _ in lanes),
                tuple(jnp.zeros((TILE, 1), F32) for _ in lanes), jnp.float32(0.0))
        visited, accs, sums, _ = lax.while_loop(more, step, init)
        for hh, ls in enumerate(lanes):
            o_ref[:, ls] = accs[hh]
            c_ref[:, ls] = jnp.broadcast_to(sums[hh], (TILE, D_HEAD))
        n_ref[(b * groups + h) * nq + i] = visited.astype(F32)

    blk = pl.BlockSpec((TILE, width), lambda b, h, i: (b * nq + i, h))
    seq = pl.BlockSpec((lp, width), lambda b, h, i: (b, h))
    return pl.pallas_call(
        body, name="sb_fwd", grid=(bsz, groups, nq),
        in_specs=[blk, seq, seq], out_specs=[blk, blk, pl.BlockSpec(memory_space=pltpu.SMEM)],
        out_shape=[jax.ShapeDtypeStruct((t, D_MODEL), F32)] * 2 + [jax.ShapeDtypeStruct((bsz * groups * nq,), F32)],
        compiler_params=_params(("arbitrary", "arbitrary", "arbitrary")),
    )(qs, ks, vs)


def _sb_bwd(qs, ks, vs, ctot, visited, do, bsz, nq):
    t = qs.shape[0]
    lp = nq * TILE
    width = HEAD_GROUP * D_HEAD
    groups = N_HEADS // HEAD_GROUP
    lanes = [slice(hh * D_HEAD, (hh + 1) * D_HEAD) for hh in range(HEAD_GROUP)]

    def body(n_ref, q_ref, k_ref, v_ref, c_ref, do_ref, dq_ref, dk_ref, dv_ref, dk_acc, dv_acc):
        b, h, i = pl.program_id(0), pl.program_id(1), pl.program_id(2)

        @pl.when(i == 0)
        def _():
            dk_acc[...] = jnp.zeros_like(dk_acc)
            dv_acc[...] = jnp.zeros_like(dv_acc)

        q16 = [q_ref[:, ls].astype(_MXU) for ls in lanes]
        do16 = [do_ref[:, ls].astype(_MXU) for ls in lanes]
        totals = [c_ref[:, hh * D_HEAD:hh * D_HEAD + 1] for hh in range(HEAD_GROUP)]
        r, c = _iota2((TILE, TILE), 0), _iota2((TILE, TILE), 1)
        upto = (r <= c).astype(_MXU)
        before = (r < c).astype(_MXU)
        first = jnp.maximum(i + 1 - n_ref[(b * groups + h) * nq + i].astype(jnp.int32), 0)

        def step(j, carry):
            ks_ = pl.ds(pl.multiple_of(j * TILE, TILE), TILE)
            valid = _sb_valid(c - r, c, i, j)
            out = []
            for hh, ls in enumerate(lanes):
                dq, keep_pre, g_pre = carry[hh]
                k_blk, v_blk = k_ref[ks_, ls], v_ref[ks_, ls]
                keep, log_beta = _sb_logits(q16[hh], k_blk, valid)
                keep_upto = _mm01_right(keep, upto)
                a = jnp.exp(log_beta + (totals[hh] - keep_pre - keep_upto))
                da = lax.dot_general(do16[hh], v_blk.astype(_MXU), _NT, preferred_element_type=F32)
                g = a * da
                g_inside = _mm01_right(g, before)
                g_before = g_pre + g_inside
                beta = jnp.exp(log_beta)
                dz16 = (g * (1.0 - beta) - beta * g_before).astype(_MXU)
                dq = dq + jnp.dot(dz16, k_blk.astype(_MXU), preferred_element_type=F32)
                dk_acc[ks_, ls] += SB_SCALE * lax.dot_general(dz16, q16[hh], _TN, preferred_element_type=F32)
                dv_acc[ks_, ls] += lax.dot_general(a.astype(_MXU), do16[hh], _TN, preferred_element_type=F32)
                out.append((dq, keep_pre + keep_upto[:, TILE - 1:TILE],
                            g_pre + (g_inside[:, TILE - 1:TILE] + g[:, TILE - 1:TILE])))
            return tuple(out)

        zero_col = jnp.zeros((TILE, 1), F32)
        init = tuple((jnp.zeros((TILE, D_HEAD), F32), zero_col, zero_col) for _ in lanes)
        res = lax.fori_loop(first, i + 1, step, init)
        for hh, ls in enumerate(lanes):
            dq_ref[:, ls] = (SB_SCALE * res[hh][0]).astype(dq_ref.dtype)

        @pl.when(i == nq - 1)
        def _():
            dk_ref[...] = dk_acc[...].astype(dk_ref.dtype)
            dv_ref[...] = dv_acc[...].astype(dv_ref.dtype)

    blk = pl.BlockSpec((TILE, width), lambda b, h, i: (b * nq + i, h))
    seq = pl.BlockSpec((lp, width), lambda b, h, i: (b, h))
    return pl.pallas_call(
        body, name="sb_bwd", grid=(bsz, groups, nq),
        in_specs=[pl.BlockSpec(memory_space=pltpu.SMEM), blk, seq, seq, blk, blk], out_specs=[blk, seq, seq],
        out_shape=[jax.ShapeDtypeStruct((t, D_MODEL), _MXU)] * 3,
        scratch_shapes=[pltpu.VMEM((lp, width), F32)] * 2,
        compiler_params=_params(("arbitrary", "arbitrary", "arbitrary")),
    )(visited, qs, ks, vs, ctot, do)


def _head_norm(o, head_gain):
    outs, rs = [], []
    for h in range(N_HEADS):
        oh = o[:, h * D_HEAD:(h + 1) * D_HEAD]
        r = lax.rsqrt(jnp.mean(oh * oh, axis=-1, keepdims=True) + EPS)
        outs.append(oh * r)
        rs.append(r)
    return outs, rs


def _mix(o, gate, head_gain):
    if head_gain is None:
        on = o
    else:
        outs, _ = _head_norm(o, head_gain)
        on = jnp.concatenate([x * head_gain for x in outs], axis=1)
    return on, on * (gate * _sigmoid(gate))


def _out_fwd(o, gate, h_in, w_out, post_gain, head_gain, name):
    t = o.shape[0]
    tm = _row_tile(t, 256)

    def body(o_ref, g_ref, h_ref, w_ref, pg_ref, hg_ref, ho_ref, u_ref):
        _, mix = _mix(o_ref[...], g_ref[...], hg_ref[...])
        u = jnp.dot(mix.astype(_MXU), w_ref[...], preferred_element_type=F32)
        u_ref[...] = u
        r = lax.rsqrt(jnp.mean(u * u, axis=-1, keepdims=True) + EPS)
        ho_ref[...] = h_ref[...] + u * r * pg_ref[...]

    row = pl.BlockSpec((tm, D_MODEL), lambda i: (i, 0))
    vec = pl.BlockSpec((1, D_MODEL), lambda i: (0, 0))
    return pl.pallas_call(
        body, name=name, grid=(t // tm,),
        in_specs=[row, row, row, pl.BlockSpec((D_MODEL, D_MODEL), lambda i: (0, 0)), vec,
                  pl.BlockSpec((1, D_HEAD), lambda i: (0, 0))],
        out_specs=[row, row], out_shape=[jax.ShapeDtypeStruct((t, D_MODEL), F32)] * 2,
        compiler_params=_params(("arbitrary",)),
    )(o, gate, h_in, w_out, post_gain, head_gain)


def _out_fwd_loss(o, gate, h_in, w_out, post_gain, target, nq, name):
    t = o.shape[0]

    def body(o_ref, g_ref, h_ref, w_ref, pg_ref, t_ref, dh_ref, u_ref, l_ref):
        i = pl.program_id(0)

        @pl.when(i == 0)
        def _():
            l_ref[...] = jnp.zeros_like(l_ref)

        _, mix = _mix(o_ref[...], g_ref[...], None)
        u = jnp.dot(mix.astype(_MXU), w_ref[...], preferred_element_type=F32)
        u_ref[...] = u

        @pl.when(i % nq == 0)
        def _():
            dh_ref[...] = jnp.zeros_like(dh_ref)

        @pl.when(i % nq != 0)
        def _():
            r = lax.rsqrt(jnp.mean(u * u, axis=-1, keepdims=True) + EPS)
            e = h_ref[...] + u * r * pg_ref[...] - t_ref[...]
            dh_ref[...] = e * (1.0 / D_MODEL)
            l_ref[...] += jnp.sum(e * e) * (0.5 / D_MODEL)

    row = pl.BlockSpec((TILE, D_MODEL), lambda i: (i, 0))
    vec = pl.BlockSpec((1, D_MODEL), lambda i: (0, 0))
    return pl.pallas_call(
        body, name=name, grid=(t // TILE,),
        in_specs=[row, row, row, pl.BlockSpec((D_MODEL, D_MODEL), lambda i: (0, 0)), vec,
                  pl.BlockSpec((TILE, D_MODEL), lambda i: ((i // nq) * (nq - 1) + jnp.maximum(i % nq - 1, 0), 0))],
        out_specs=[row, row, pl.BlockSpec((8, 128), lambda i: (0, 0))],
        out_shape=[jax.ShapeDtypeStruct((t, D_MODEL), F32)] * 2 + [jax.ShapeDtypeStruct((8, 128), F32)],
        compiler_params=_params(("arbitrary",)),
    )(o, gate, h_in, w_out, post_gain, target)


def _out_bwd(dh, u, o, gate, w_out, post_gain, head_gain, name, narrow_do):
    t = o.shape[0]
    tm = _row_tile(t, 2 * TILE)
    has_head = head_gain is not None

    def body(*refs):
        if has_head:
            dh_ref, u_ref, o_ref, g_ref, w_ref, pg_ref, hg_ref, do_ref, dg_ref, gw_ref, gw16_ref, gp_ref, gh_ref = refs
            hg = hg_ref[...]
        else:
            dh_ref, u_ref, o_ref, g_ref, w_ref, pg_ref, do_ref, dg_ref, gw_ref, gw16_ref, gp_ref = refs
            hg = None
        first = pl.program_id(0) == 0

        @pl.when(first)
        def _():
            gw_ref[...] = jnp.zeros_like(gw_ref)
            gp_ref[...] = jnp.zeros_like(gp_ref)
            if has_head:
                gh_ref[...] = jnp.zeros_like(gh_ref)

        dr, u, o, gate = dh_ref[...], u_ref[...], o_ref[...], g_ref[...]
        r = lax.rsqrt(jnp.mean(u * u, axis=-1, keepdims=True) + EPS)
        un = u * r
        gp_ref[...] += jnp.sum(dr * un, axis=0, keepdims=True)
        dun = dr * pg_ref[...]
        du = r * (dun - un * jnp.mean(dun * un, axis=-1, keepdims=True))
        on, mix = _mix(o, gate, hg)
        du16 = du.astype(_MXU)
        gw_ref[...] += lax.dot_general(mix.astype(_MXU), du16, _TN, preferred_element_type=F32)
        dmix = lax.dot_general(du16, w_ref[...], _NT, preferred_element_type=F32)
        sg = _sigmoid(gate)
        dg_ref[...] = (dmix * on * (sg * (1.0 + gate * (1.0 - sg)))).astype(dg_ref.dtype)
        don = dmix * (gate * sg)
        if has_head:
            outs, rs = _head_norm(o, hg)
            gh = jnp.zeros((1, D_HEAD), F32)
            cols = []
            for h in range(N_HEADS):
                dn = don[:, h * D_HEAD:(h + 1) * D_HEAD]
                gh = gh + jnp.sum(dn * outs[h], axis=0, keepdims=True)
                dnn = dn * hg
                cols.append(rs[h] * (dnn - outs[h] * jnp.mean(dnn * outs[h], axis=-1, keepdims=True)))
            gh_ref[...] += gh
            do_ref[...] = jnp.concatenate(cols, axis=1)
        else:
            do_ref[...] = don.astype(do_ref.dtype)

        @pl.when(pl.program_id(0) == pl.num_programs(0) - 1)
        def _():
            gw16_ref[...] = gw_ref[...].astype(_MXU)

    row = pl.BlockSpec((tm, D_MODEL), lambda i: (i, 0))
    vec = pl.BlockSpec((1, D_MODEL), lambda i: (0, 0))
    mat = pl.BlockSpec((D_MODEL, D_MODEL), lambda i: (0, 0))
    in_specs = [row, row, row, row, mat, vec]
    args = [dh, u, o, gate, w_out, post_gain]
    out_specs = [row, row, mat, mat, vec]
    out_shape = [jax.ShapeDtypeStruct((t, D_MODEL), _MXU if narrow_do else F32),
                 jax.ShapeDtypeStruct((t, D_MODEL), _MXU)] + [jax.ShapeDtypeStruct((D_MODEL, D_MODEL), F32),
                                                                  jax.ShapeDtypeStruct((D_MODEL, D_MODEL), _MXU),
                                                                  jax.ShapeDtypeStruct((1, D_MODEL), F32)]
    if has_head:
        in_specs.append(pl.BlockSpec((1, D_HEAD), lambda i: (0, 0)))
        args.append(head_gain)
        out_specs.append(pl.BlockSpec((1, D_HEAD), lambda i: (0, 0)))
        out_shape.append(jax.ShapeDtypeStruct((1, D_HEAD), F32))
    return pl.pallas_call(
        body, name=name, grid=(t // tm,), in_specs=in_specs, out_specs=out_specs, out_shape=out_shape,
        compiler_params=_params(("arbitrary",)),
    )(*args)


def _proj_bwd(ds, w4, h_in, gain, dh_out, name, outgoing=(), nq=None):
    t = h_in.shape[0]
    tm = _row_tile(t, 256) if nq is None else TILE
    n_out = len(outgoing)
    n_dh = 1 if nq is None else 2

    def body(d0, d1, d2, d3, w_ref, h_ref, g_ref, dho_ref, *rest):
        send, dh_refs, gg_ref, rest = rest[:n_out], rest[n_out:n_out + n_dh], rest[n_out + n_dh], rest[n_out + n_dh + 1:]
        landed, sems = rest[:n_out], rest[n_out:]
        if n_out:
            first, last = _grid_ends(1)
            start, finish = _chip_exchange(send, landed, sems, slotted=True)
            pl.when(first)(start)

        @pl.when(pl.program_id(0) == 0)
        def _():
            gg_ref[...] = jnp.zeros_like(gg_ref)

        dy = jnp.zeros((tm, D_MODEL), F32)
        for j, d in enumerate((d0, d1, d2, d3)):
            dy = dy + lax.dot_general(d[...].astype(_MXU), w_ref[j], _NT, preferred_element_type=F32)
        x = h_ref[...]
        r = lax.rsqrt(jnp.mean(x * x, axis=-1, keepdims=True) + EPS)
        xn = x * r
        gg_ref[...] += jnp.sum(dy * xn, axis=0, keepdims=True)
        dxn = dy * g_ref[...]
        dh = dho_ref[...] + r * (dxn - xn * jnp.mean(dxn * xn, axis=-1, keepdims=True))
        if nq is None:
            dh_refs[0][...] = dh
        else:
            in_front = pl.program_id(0) % nq == 0

            @pl.when(in_front)
            def _():
                dh_refs[1][...] = dh

            @pl.when(jnp.logical_not(in_front))
            def _():
                dh_refs[0][...] = dh
        if n_out:
            pl.when(last)(finish)

    row = pl.BlockSpec((tm, D_MODEL), lambda i: (i, 0))
    vec = pl.BlockSpec((1, D_MODEL), lambda i: (0, 0))
    hbm = pl.BlockSpec(memory_space=pl.ANY)
    if nq is None:
        dh_specs, dh_shapes = [row], [jax.ShapeDtypeStruct((t, D_MODEL), F32)]
    else:
        dh_specs = [pl.BlockSpec((TILE, D_MODEL), lambda i: ((i // nq) * (nq - 1) + jnp.maximum(i % nq - 1, 0), 0)),
                    pl.BlockSpec((TILE, D_MODEL), lambda i: (i // nq, 0))]
        dh_shapes = [jax.ShapeDtypeStruct((t // nq * (nq - 1), D_MODEL), F32),
                     jax.ShapeDtypeStruct((t // nq, D_MODEL), F32)]
    return pl.pallas_call(
        body, name=name, grid=(t // tm,),
        in_specs=[row] * 4 + [pl.BlockSpec((4, D_MODEL, D_MODEL), lambda i: (0, 0, 0)), row, vec, row] + [hbm] * n_out,
        out_specs=dh_specs + [vec] + [hbm] * n_out,
        out_shape=dh_shapes + [jax.ShapeDtypeStruct((1, D_MODEL), F32)]
        + [jax.ShapeDtypeStruct(a.shape, a.dtype) for a in outgoing],
        scratch_shapes=_exchange_scratch(n_out) if n_out else [],
        compiler_params=_params(("arbitrary",)),
    )(*ds, w4, h_in, gain, dh_out, *outgoing)


def _weight_grad(y, d, name):
    t = y.shape[0]
    tk = _row_tile(t, t // 4)

    def body(y_ref, d_ref, g_ref, g16_ref):
        @pl.when(pl.program_id(0) == 0)
        def _():
            g_ref[...] = jnp.zeros_like(g_ref)

        g_ref[...] += lax.dot_general(y_ref[...], d_ref[...].astype(_MXU), _TN, preferred_element_type=F32)

        @pl.when(pl.program_id(0) == pl.num_programs(0) - 1)
        def _():
            g16_ref[...] = g_ref[...].astype(_MXU)

    row = pl.BlockSpec((tk, D_MODEL), lambda i: (i, 0))
    mat = pl.BlockSpec((D_MODEL, D_MODEL), lambda i: (0, 0))
    return pl.pallas_call(
        body, name=name, grid=(t // tk,), in_specs=[row, row], out_specs=[mat, mat],
        out_shape=[jax.ShapeDtypeStruct((D_MODEL, D_MODEL), F32), jax.ShapeDtypeStruct((D_MODEL, D_MODEL), _MXU)],
        compiler_params=_params(("arbitrary",)),
    )(y, d)


def _local_step(x, target, meta, pre_norm, post_norm, lam, head_gain, hw_in, shards):
    bsz, seq, _ = x.shape
    nq = seq // TILE + 1
    nb = nq * (TILE // BLOCK)
    lp = nq * TILE
    t = bsz * lp
    d4 = D_MODEL // N_CHIPS
    front = jnp.concatenate([jnp.zeros((N_PAD, D_MODEL), F32), meta], axis=0)
    h0 = jnp.concatenate([jnp.broadcast_to(front[None], (bsz, TILE, D_MODEL)), x], axis=1).reshape(t, D_MODEL)
    pre0, pre1, post0, post1 = pre_norm[0:1], pre_norm[1:2], post_norm[0:1], post_norm[1:2]

    y0, q0, f0, v0, g0 = _norm_proj(h0, pre0, hw_in, "norm_proj_hgrn", (False,) * 4)
    o0, sst, sym, sw_in, sw_out, hw_out = _hgrn_fwd(q0, f0, v0, lam, bsz, nb, shards)
    sw_out, hw_out = sw_out.reshape(D_MODEL, D_MODEL), hw_out.reshape(D_MODEL, D_MODEL)
    h1, u0 = _out_fwd(o0, g0, h0, hw_out, post0, head_gain, "out_fwd_hgrn")
    y1, q1, k1, v1, g1 = _norm_proj(h1, pre1, sw_in, "norm_proj_sb", (True, True, True, False))
    o1, ctot, visited = _sb_fwd(q1, k1, v1, bsz, nq)
    dh2, u1, loss_blk = _out_fwd_loss(o1, g1, h1, sw_out, post1, target.reshape(bsz * seq, D_MODEL), nq, "out_fwd_sb")

    do1, dg1, g_sw_out, g_sw_out16, g_post1 = _out_bwd(dh2, u1, o1, g1, sw_out, post1, None, "out_bwd_sb", True)
    dq1, dk1, dv1 = _sb_bwd(q1, k1, v1, ctot, visited, do1, bsz, nq)
    ds1 = (dq1, dk1, dv1, dg1)
    dh1, g_pre1 = _proj_bwd(ds1, sw_in, h1, pre1, dh2, "proj_bwd_sb")
    g_sw_in = [_weight_grad(y1, d, "wgrad_sb_%d" % j) for j, d in enumerate(ds1)]

    do0, dg0, g_hw_out, g_hw_out16, g_post0, g_head = _out_bwd(dh1, u0, o0, g0, hw_out, post0, head_gain, "out_bwd_hgrn",
                                                               False)
    ready = (jnp.stack([g16 for _, g16 in g_sw_in]), g_sw_out16.reshape(N_CHIPS, d4, D_MODEL),
             g_hw_out16.reshape(N_CHIPS, d4, D_MODEL))
    dq0, df0, dv0, dlam, land_sw_in, land_sw_out, land_hw_out = _hgrn_bwd(q0, f0, v0, lam, sst, sym, do0, bsz, nb, ready)
    ds0 = (dq0, df0, dv0, dg0)
    g_hw_in = [_weight_grad(y0, d, "wgrad_hgrn_%d" % j) for j, d in enumerate(ds0)]
    last = (jnp.stack([g16 for _, g16 in g_hw_in]),)
    grad_x, dh_front, g_pre0, land_hw_in = _proj_bwd(ds0, hw_in, h0, pre0, dh1, "proj_bwd_hgrn", last, nq)

    grad_x = grad_x.reshape(bsz, seq, D_MODEL)
    g_meta = jnp.sum(dh_front.reshape(bsz, TILE, D_MODEL)[:, N_PAD:, :], axis=0)
    g_lam = jnp.sum(dlam, axis=0)
    small = jnp.concatenate([g_pre0, g_pre1, g_post0, g_post1, g_lam, g_lam,
                             jnp.pad(g_head, ((0, 0), (0, D_MODEL - D_HEAD))), g_meta,
                             jnp.pad(loss_blk[0:1], ((0, 0), (0, D_MODEL - loss_blk.shape[1])))], axis=0)
    rows4 = lambda g: [g[j * d4:(j + 1) * d4] for j in range(N_CHIPS)]
    large = dict(hw_in=(land_hw_in, [g for g, _ in g_hw_in]), sw_in=(land_sw_in, [g for g, _ in g_sw_in]),
                 hw_out=(land_hw_out, rows4(g_hw_out)), sw_out=(land_sw_out, rows4(g_sw_out)))
    return grad_x, small, large


def _prep_weights(hw_in, sw_in, hw_out, sw_out, meta):
    def body(hi_ref, si_ref, ho_ref, so_ref, m_ref, ghi, gm, si16, so16, ho16, far_send, far_recv, near_send, near_recv):
        x, y, c = _place()
        me = 2 * x + y
        ghi[me] = hi_ref[0].astype(_MXU)
        gm[me] = m_ref[...]
        si16[...] = si_ref[0].astype(_MXU)
        so16[...] = so_ref[0].astype(_MXU)
        ho16[...] = ho_ref[0].astype(_MXU)
        outs = (ghi, gm)
        n = len(outs)
        peers = [(1 - x, y), (x, 1 - y), (1 - x, 1 - y)]

        def half(a, slot, which):
            rows = outs[a].shape[1] // 2
            return outs[a].at[slot, pl.ds(which * rows, rows), :]

        def far(r, a, slot):
            px, py = peers[r]
            return pltpu.make_async_remote_copy(
                src_ref=half(a, slot, c), dst_ref=half(a, slot, c), send_sem=far_send.at[r * n + a],
                recv_sem=far_recv.at[r * n + a], device_id=(px, py, c), device_id_type=MESH)

        def near(r, a, which):
            px, py = peers[r]
            return pltpu.make_async_remote_copy(
                src_ref=half(a, 2 * px + py, which), dst_ref=half(a, 2 * px + py, which),
                send_sem=near_send.at[r * n + a], recv_sem=near_recv.at[r * n + a],
                device_id=(x, y, 1 - c), device_id_type=MESH)

        for r in range(3):
            for a in range(n):
                far(r, a, me).start()
        for r, (px, py) in enumerate(peers):
            for a in range(n):
                far(r, a, 2 * px + py).wait_recv()
                near(r, a, c).start()
        for r in range(3):
            for a in range(n):
                near(r, a, 1 - c).wait_recv()
        for r in range(3):
            for a in range(n):
                far(r, a, me).wait_send()
                near(r, a, c).wait_send()

    d4 = D_MODEL // N_CHIPS
    vm = pl.BlockSpec(memory_space=pltpu.VMEM)
    return pl.pallas_call(
        body, name="prep_weights",
        in_specs=[vm] * 5, out_specs=[vm] * 5,
        out_shape=[jax.ShapeDtypeStruct((N_CHIPS, D_MODEL, D_MODEL), _MXU), jax.ShapeDtypeStruct((N_CHIPS, N_META, d4), F32),
                   jax.ShapeDtypeStruct((D_MODEL, D_MODEL), _MXU), jax.ShapeDtypeStruct((d4, D_MODEL), _MXU),
                   jax.ShapeDtypeStruct((d4, D_MODEL), _MXU)],
        scratch_shapes=[pltpu.SemaphoreType.DMA((6,))] * 4,
        compiler_params=pltpu.CompilerParams(vmem_limit_bytes=VMEM_LIMIT),
    )(hw_in, sw_in, hw_out, sw_out, meta)


def _scatter_small(small):
    def body(sm, lsm, send_sems, recv_sems, local_sem):
        x, y, c = _place()
        mine = 4 * x + 2 * y + c
        local = pltpu.make_async_copy(sm, lsm.at[mine], local_sem)
        local.start()

        def copy(rel, src_dev, to):
            return pltpu.make_async_remote_copy(
                src_ref=sm, dst_ref=lsm.at[src_dev], send_sem=send_sems.at[rel - 1], recv_sem=recv_sems.at[rel - 1],
                device_id=to, device_id_type=MESH)

        flip = lambda bit, v: 1 - v if bit else v
        rels = [(rel, flip(rel & 4, x), flip(rel & 2, y), flip(rel & 1, c)) for rel in range(1, N_DEV)]
        sends = [copy(rel, mine, (px, py, pc)) for rel, px, py, pc in rels]
        for cp in sends:
            cp.start()
        for rel, px, py, pc in rels:
            copy(rel, 4 * px + 2 * py + pc, (px, py, pc)).wait_recv()
        for cp in sends:
            cp.wait_send()
        local.wait()

    hbm = pl.BlockSpec(memory_space=pl.ANY)
    return pl.pallas_call(
        body, name="scatter_small", in_specs=[hbm], out_specs=hbm,
        out_shape=jax.ShapeDtypeStruct((N_DEV, SMALL_ROWS, D_MODEL), F32),
        scratch_shapes=[pltpu.SemaphoreType.DMA((N_DEV - 1,)), pltpu.SemaphoreType.DMA((N_DEV - 1,)),
                        pltpu.SemaphoreType.DMA(())],
    )(small)


def _sum_slots(landed, own, me, name):
    n, rows, _ = landed.shape
    tm = rows if rows < 256 else 256

    def body(me_ref, l_ref, o0, o1, o2, o3, out_ref):
        acc = None
        for k, o in enumerate((o0, o1, o2, o3)):
            term = jnp.where(me_ref[0] == k, o[...], l_ref[k].astype(F32))
            acc = term if acc is None else acc + term
        out_ref[...] = acc

    blk = pl.BlockSpec((tm, D_MODEL), lambda i: (i, 0))
    return pl.pallas_call(
        body, name=name, grid=(rows // tm,),
        in_specs=[pl.BlockSpec(memory_space=pltpu.SMEM), pl.BlockSpec((n, tm, D_MODEL), lambda i: (0, i, 0))] + [blk] * 4,
        out_specs=blk, out_shape=jax.ShapeDtypeStruct((rows, D_MODEL), F32),
        compiler_params=_params(("arbitrary",)),
    )(me, landed, *own)


def _swap_with_sibling(parts):
    def body(a0, a1, a2, a3, b0, b1, b2, b3, send_sems, recv_sems):
        x, y, c = _place()
        copies = [pltpu.make_async_remote_copy(src_ref=s, dst_ref=d, send_sem=send_sems.at[a], recv_sem=recv_sems.at[a],
                                               device_id=(x, y, 1 - c), device_id_type=MESH)
                  for a, (s, d) in enumerate(zip((a0, a1, a2, a3), (b0, b1, b2, b3)))]
        for cp in copies:
            cp.start()
        for cp in copies:
            cp.wait()

    hbm = pl.BlockSpec(memory_space=pl.ANY)
    return pl.pallas_call(
        body, name="swap_with_sibling", in_specs=[hbm] * 4, out_specs=[hbm] * 4,
        out_shape=[jax.ShapeDtypeStruct(p.shape, F32) for p in parts],
        scratch_shapes=[pltpu.SemaphoreType.DMA((4,)), pltpu.SemaphoreType.DMA((4,))],
    )(*parts)


def _adamw_math(w, g, m, v):
    m = ADAM_B1 * m + (1.0 - ADAM_B1) * g
    v = ADAM_B2 * v + (1.0 - ADAM_B2) * (g * g)
    m_hat = m / (1.0 - ADAM_B1 ** ADAM_STEP)
    v_hat = v / (1.0 - ADAM_B2 ** ADAM_STEP)
    delta = -ADAM_LR * (m_hat / (jnp.sqrt(v_hat) + ADAM_EPS) + ADAM_WD * w)
    return delta, m, v


def _adamw(w, g_parts, m, v, name):
    rows, cols = w.shape
    tm = rows if rows < 256 else 256
    n = len(g_parts)

    def body(*refs):
        w_ref, m_ref, v_ref = refs[n:n + 3]
        g_ref, d_ref, nm_ref, nv_ref = refs[n + 3:]
        g = refs[0][...]
        for p in refs[1:n]:
            g = g + p[...]
        g_ref[...] = g
        d_ref[...], nm_ref[...], nv_ref[...] = _adamw_math(w_ref[...], g, m_ref[...], v_ref[...])

    blk = pl.BlockSpec((tm, cols), lambda i: (i, 0))
    return pl.pallas_call(
        body, name=name, grid=(rows // tm,), in_specs=[blk] * (n + 3), out_specs=[blk] * 4,
        out_shape=[jax.ShapeDtypeStruct((rows, cols), F32)] * 4,
        compiler_params=_params(("arbitrary",)),
    )(*g_parts, w, m, v)


def _lam_of(hgrn_lb):
    def body(lb_ref, o_ref):
        lb = lb_ref[...]
        e = jnp.exp(lb - jnp.max(lb, axis=0, keepdims=True))
        o_ref[...] = e[0:1, :] / jnp.sum(e, axis=0, keepdims=True)

    return pl.pallas_call(body, name="lam_of", out_shape=jax.ShapeDtypeStruct((1, D_MODEL), F32))(hgrn_lb)


def _small_grads(land_small, lam):
    def body(l_ref, lam_ref, o_ref):
        acc = l_ref[0]
        for k in range(1, N_DEV):
            acc = acc + l_ref[k]
        p = lam_ref[...]
        slope = p * (1.0 - p)
        row = _iota2((SMALL_ROWS, D_MODEL), 0)
        o_ref[...] = acc * jnp.where(row == 4, slope, jnp.where(row == 5, -slope, 1.0))

    return pl.pallas_call(body, name="small_grads",
                          out_shape=jax.ShapeDtypeStruct((SMALL_ROWS, D_MODEL), F32))(land_small, lam)


def kernel(x, meta_tokens, pre_norm, post_norm, hgrn_w_in, hgrn_lb, hgrn_out_norm, hgrn_w_out, sb_w_in, sb_w_out, loss_target, m_meta_tokens, m_pre_norm, m_post_norm, m_hgrn_w_in, m_hgrn_lb, m_hgrn_out_norm, m_hgrn_w_out, m_sb_w_in, m_sb_w_out, v_meta_tokens, v_pre_norm, v_post_norm, v_hgrn_w_in, v_hgrn_lb, v_hgrn_out_norm, v_hgrn_w_out, v_sb_w_in, v_sb_w_out):
    d4 = D_MODEL // N_CHIPS
    chip = 2 * lax.axis_index("x") + lax.axis_index("y")
    hw_in, meta4, sw_in16, sw_out16, hw_out16 = _prep_weights(hgrn_w_in, sb_w_in, hgrn_w_out, sb_w_out, meta_tokens)
    meta = meta4.transpose(1, 0, 2).reshape(N_META, D_MODEL)
    lam = _lam_of(hgrn_lb)
    grad_x, small, large = _local_step(
        x, loss_target, meta, pre_norm, post_norm, lam, hgrn_out_norm,
        hw_in, (sw_in16, sw_out16, hw_out16))

    me = jnp.reshape(chip, (1,)).astype(jnp.int32)
    parts = [_sum_slots(*large[n], me, "sum_" + n) for n in ("hw_in", "sw_in", "hw_out", "sw_out")]
    sib = _swap_with_sibling(parts)
    small = _small_grads(_scatter_small(small), lam)
    loss = small[SMALL_ROWS - 1, 0]

    res = {}
    res["hgrn_w_in"] = _adamw(hgrn_w_in[0], [parts[0], sib[0]], m_hgrn_w_in[0], v_hgrn_w_in[0], "adamw_hw_in")
    res["sb_w_in"] = _adamw(sb_w_in[0], [parts[1], sib[1]], m_sb_w_in[0], v_sb_w_in[0], "adamw_sw_in")
    res["hgrn_w_out"] = _adamw(hgrn_w_out[0], [parts[2], sib[2]], m_hgrn_w_out[0], v_hgrn_w_out[0], "adamw_hw_out")
    res["sb_w_out"] = _adamw(sb_w_out[0], [parts[3], sib[3]], m_sb_w_out[0], v_sb_w_out[0], "adamw_sw_out")
    res["pre_norm"] = _adamw(pre_norm, [small[0:2]], m_pre_norm, v_pre_norm, "adamw_pre")
    res["post_norm"] = _adamw(post_norm, [small[2:4]], m_post_norm, v_post_norm, "adamw_post")
    res["hgrn_lb"] = _adamw(hgrn_lb, [small[4:6]], m_hgrn_lb, v_hgrn_lb, "adamw_lb")
    res["hgrn_out_norm"] = _adamw(hgrn_out_norm, [small[6:7, :D_HEAD]], m_hgrn_out_norm, v_hgrn_out_norm, "adamw_head")
    g_meta = lax.dynamic_slice_in_dim(small[7:7 + N_META], chip * d4, d4, axis=1)
    res["meta_tokens"] = _adamw(meta_tokens, [g_meta], m_meta_tokens, v_meta_tokens, "adamw_meta")
    for n in ("hgrn_w_in", "hgrn_w_out", "sb_w_in", "sb_w_out"):
        res[n] = tuple(a[None] for a in res[n])
    order = ("meta_tokens", "pre_norm", "post_norm", "hgrn_w_in", "hgrn_lb", "hgrn_out_norm", "hgrn_w_out",
             "sb_w_in", "sb_w_out")
    return (loss, grad_x, *[res[n][0] for n in order], *[res[n][1] for n in order],
            *[res[n][2] for n in order], *[res[n][3] for n in order])
```

```python
import functools

import jax
import numpy as np
import jax.numpy as jnp
from jax import lax
from jax.experimental import pallas as pl
from jax.experimental.pallas import tpu as pltpu

F32 = jnp.float32
_MXU = jnp.bfloat16

D_MODEL = 1024
N_HEADS = 8
D_HEAD = 128
BLOCK = 128
N_META = 16
TILE = 256
N_PAD = TILE - N_META
UNDERFLOW = -105.0
EPS = 1e-6
SB_SCALE = D_HEAD ** -0.5
SOFTPLUS_LINEAR = 20.0
MASKED = -1e30
ADAM_LR, ADAM_B1, ADAM_B2, ADAM_EPS, ADAM_WD, ADAM_STEP = 0.001, 0.9, 0.999, 1e-08, 0.01, 10
N_CHIPS = 4
N_DEV = 8
SMALL_ROWS = 24
VMEM_LIMIT = 56 * 1024 * 1024
MESH = pl.DeviceIdType.MESH

_NT = (((1,), (1,)), ((), ()))
_TN = (((0,), (0,)), ((), ()))


def _mm(a, b):
    return jnp.dot(a.astype(_MXU), b.astype(_MXU), preferred_element_type=F32)


def _mm_nt(a, b):
    return lax.dot_general(a.astype(_MXU), b.astype(_MXU), _NT, preferred_element_type=F32)


def _mm_tn(a, b):
    return lax.dot_general(a.astype(_MXU), b.astype(_MXU), _TN, preferred_element_type=F32)


def _split2(x):
    hi = x.astype(_MXU)
    return hi, (x - hi.astype(F32)).astype(_MXU)


def _mm_s(a16, state):
    hi, lo = _split2(state)
    return jnp.dot(a16, hi, preferred_element_type=F32) + jnp.dot(a16, lo, preferred_element_type=F32)


def _mm_nt_s(a16, state):
    hi, lo = _split2(state)
    return (lax.dot_general(a16, hi, _NT, preferred_element_type=F32)
            + lax.dot_general(a16, lo, _NT, preferred_element_type=F32))


def _mm01_right(x, m01):
    return jnp.dot(x.astype(_MXU), m01, preferred_element_type=F32)


def _mm01_left(m01, x):
    hi, lo = _split2(x)
    return jnp.dot(m01, hi, preferred_element_type=F32) + jnp.dot(m01, lo, preferred_element_type=F32)


def _iota2(shape, dim):
    return lax.broadcasted_iota(jnp.int32, shape, dim)


def _row_tile(total, pref):
    t = pref
    while total % t:
        t -= BLOCK
    return t


def _params(sem, limit=VMEM_LIMIT):
    return pltpu.CompilerParams(dimension_semantics=sem, vmem_limit_bytes=limit)


def _sigmoid(x):
    return 1.0 / (1.0 + jnp.exp(-x))


def _grid_ends(ndim):
    first, last = True, True
    for d in range(ndim):
        first = first & (pl.program_id(d) == 0)
        last = last & (pl.program_id(d) == pl.num_programs(d) - 1)
    return first, last


def _place():
    return lax.axis_index("x"), lax.axis_index("y"), lax.axis_index("c")


def _exchange_scratch(n):
    return [pltpu.SemaphoreType.DMA((3 * n,)), pltpu.SemaphoreType.DMA((3 * n,)), pltpu.SemaphoreType.DMA((n,))]


def _chip_exchange(srcs, dsts, sems, slotted):
    send_sems, recv_sems, local_sems = sems
    x, y, c = _place()
    me = 2 * x + y
    peers = [(1 - x, y), (x, 1 - y), (1 - x, 1 - y)]
    n = len(dsts)

    def remote(r, a, sending):
        px, py = peers[r]
        p = 2 * px + py
        return pltpu.make_async_remote_copy(
            src_ref=srcs[a].at[p] if slotted else srcs[a], dst_ref=dsts[a].at[me if sending else p],
            send_sem=send_sems.at[r * n + a], recv_sem=recv_sems.at[r * n + a],
            device_id=(px, py, c), device_id_type=MESH)

    def local(a):
        return pltpu.make_async_copy(srcs[a].at[me] if slotted else srcs[a], dsts[a].at[me], local_sems.at[a])

    def start():
        for a in range(n):
            local(a).start()
        for r in range(3):
            for a in range(n):
                remote(r, a, True).start()

    def finish():
        for r in range(3):
            for a in range(n):
                remote(r, a, False).wait_recv()
        for r in range(3):
            for a in range(n):
                remote(r, a, True).wait_send()
        for a in range(n):
            local(a).wait()

    return start, finish


def _norm_proj(h, gain, w4, name, narrow):
    t = h.shape[0]
    tm = _row_tile(t, 256)

    def body(h_ref, g_ref, w_ref, y_ref, s0, s1, s2, s3):
        x = h_ref[...]
        r = lax.rsqrt(jnp.mean(x * x, axis=-1, keepdims=True) + EPS)
        y = (x * r * g_ref[...]).astype(_MXU)
        y_ref[...] = y
        for j, s in enumerate((s0, s1, s2, s3)):
            s[...] = jnp.dot(y, w_ref[j], preferred_element_type=F32).astype(s.dtype)

    row = pl.BlockSpec((tm, D_MODEL), lambda i: (i, 0))
    return pl.pallas_call(
        body, name=name, grid=(t // tm,),
        in_specs=[row, pl.BlockSpec((1, D_MODEL), lambda i: (0, 0)),
                  pl.BlockSpec((4, D_MODEL, D_MODEL), lambda i: (0, 0, 0))],
        out_specs=[row] * 5,
        out_shape=[jax.ShapeDtypeStruct((t, D_MODEL), _MXU)]
        + [jax.ShapeDtypeStruct((t, D_MODEL), _MXU if n else F32) for n in narrow],
        compiler_params=_params(("arbitrary",)),
    )(h, gain, w4)


LEVELS = (64, 32, 16, 8, 4, 2, 1)
HEAD_GROUP = 2


def _hgrn_tables():
    r = np.arange(BLOCK)
    mats = [r[None, :] <= r[:, None]]
    x = r[:, None] ^ r[None, :]
    lv = np.full((BLOCK, BLOCK), len(LEVELS), np.int32)
    for i, m in enumerate(LEVELS):
        lv[(x >= m) & (x < 2 * m)] = i
    return jnp.asarray(np.concatenate(mats, 0).astype(np.float32), dtype=_MXU), jnp.asarray(lv)


def _hgrn_exponents(g, sums):
    b = _mm01_left(sums, g)
    row = _iota2((BLOCK, D_HEAD), 0)
    out = []
    for m in LEVELS:
        is_q = (row & m) != 0
        if m >= 4:
            grp = b.reshape(BLOCK // (2 * m), 2 * m, D_HEAD)
            ref = jnp.broadcast_to(grp[:, m - 1:m, :], grp.shape).reshape(BLOCK, D_HEAD)
            d = b - ref
            out.append(jnp.where(is_q, d, -d))
        elif m == 2:
            below, above = pltpu.roll(g, 1, axis=0), pltpu.roll(g, BLOCK - 1, axis=0)
            low = row & 3
            out.append(jnp.where(low == 3, g + below, jnp.where(low == 2, g, jnp.where(low == 0, above, 0.0))))
        else:
            out.append(jnp.where(is_q, g, 0.0))
    return b, out


def _hgrn_gates(fz, lam, chunk):
    pos = chunk * BLOCK + _iota2((BLOCK, D_HEAD), 0)
    live = pos >= N_PAD
    sg = _sigmoid(fz)
    f = lam + (1.0 - lam) * sg
    g = jnp.where(live, jnp.log(f), 0.0)
    k = jnp.where(live, (1.0 - lam) * (1.0 - sg), 0.0)
    return sg, f, g, k, live


def _level_operand(q, k, exponent, m):
    decay = jnp.exp(exponent)
    is_q = (_iota2((BLOCK, D_HEAD), 0) & m) != 0
    return is_q, decay, (jnp.where(is_q, q, k) * decay).astype(_MXU)


def _hgrn_fwd(qs, fs, vs, lam, bsz, nb, shards):
    t = qs.shape[0]
    sums, levels = _hgrn_tables()
    width = HEAD_GROUP * D_HEAD

    n_sh = len(shards)

    def body(q_ref, f_ref, v_ref, lam_ref, sums_ref, lv_ref, *rest):
        own, (o_ref, sst_ref, sym_ref), rest = rest[:n_sh], rest[n_sh:n_sh + 3], rest[n_sh + 3:]
        gathered, st_scr, sems = rest[:n_sh], rest[n_sh], rest[n_sh + 1:]
        n = pl.program_id(2)
        first, last = _grid_ends(3)
        start, finish = _chip_exchange(own, gathered, sems, slotted=False)
        pl.when(first)(start)

        @pl.when(n == 0)
        def _():
            st_scr[...] = jnp.zeros_like(st_scr)

        lv = lv_ref[...]
        r, c = _iota2((BLOCK, BLOCK), 0), _iota2((BLOCK, BLOCK), 1)
        for hh in range(HEAD_GROUP):
            ls = slice(hh * D_HEAD, (hh + 1) * D_HEAD)
            st = st_scr[hh]
            sst_ref[0, hh, 0] = st
            q, v = q_ref[:, ls], v_ref[:, ls]
            _, _, g, k, _ = _hgrn_gates(f_ref[:, ls], lam_ref[:, ls], n)
            b, exps = _hgrn_exponents(g, sums_ref[...])
            sym = jnp.zeros((BLOCK, BLOCK), F32)
            for li, m in enumerate(LEVELS):
                _, _, x16 = _level_operand(q, k, exps[li], m)
                sym = jnp.where(lv == li, lax.dot_general(x16, x16, _NT, preferred_element_type=F32), sym)
            sym = jnp.where(c == r, jnp.sum(q * k, axis=1, keepdims=True), sym).astype(_MXU)
            sym_ref[0, hh, 0] = sym
            o_ref[:, ls] = _mm_nt(q * jnp.exp(b), st) + _mm(jnp.where(c <= r, sym, 0), v)
            b_end = b[BLOCK - 1:BLOCK, :]
            st_scr[hh] = st * jnp.exp(b_end) + _mm_tn(v, k * jnp.exp(b_end - b))
        pl.when(last)(finish)

    blk = pl.BlockSpec((BLOCK, width), lambda b, h, n: (b * nb + n, h))
    hbm = pl.BlockSpec(memory_space=pl.ANY)
    return pl.pallas_call(
        body, name="hgrn_fwd", grid=(bsz, N_HEADS // HEAD_GROUP, nb),
        in_specs=[blk, blk, blk, pl.BlockSpec((1, width), lambda b, h, n: (0, h)),
                  pl.BlockSpec(sums.shape, lambda b, h, n: (0, 0)), pl.BlockSpec(levels.shape, lambda b, h, n: (0, 0))]
        + [hbm] * n_sh,
        out_specs=[blk] + [pl.BlockSpec((1, HEAD_GROUP, 1, D_HEAD, D_HEAD), lambda b, h, n: (b, h, n, 0, 0))] * 2
        + [hbm] * n_sh,
        out_shape=[jax.ShapeDtypeStruct((t, D_MODEL), F32),
                   jax.ShapeDtypeStruct((bsz, N_HEADS, nb, D_HEAD, D_HEAD), F32),
                   jax.ShapeDtypeStruct((bsz, N_HEADS, nb, D_HEAD, D_HEAD), _MXU)]
        + [jax.ShapeDtypeStruct((N_CHIPS,) + a.shape, a.dtype) for a in shards],
        scratch_shapes=[pltpu.VMEM((HEAD_GROUP, D_HEAD, D_HEAD), F32)] + _exchange_scratch(n_sh),
        compiler_params=_params(("arbitrary", "arbitrary", "arbitrary")),
    )(qs, fs, vs, lam, sums, levels, *shards)


def _hgrn_bwd(qs, fs, vs, lam, sst, sym, do, bsz, nb, outgoing):
    t = qs.shape[0]
    sums, levels = _hgrn_tables()
    width = HEAD_GROUP * D_HEAD

    n_out = len(outgoing)

    def body(q_ref, f_ref, v_ref, lam_ref, sst_ref, sym_ref, do_ref, sums_ref, lv_ref, *rest):
        send, (dq_ref, df_ref, dv_ref, dlam_ref), rest = rest[:n_out], rest[n_out:n_out + 4], rest[n_out + 4:]
        landed, dst_scr, gsum_scr, sems = rest[:n_out], rest[n_out], rest[n_out + 1], rest[n_out + 2:]
        n = pl.program_id(2)
        chunk = nb - 1 - n
        first, last = _grid_ends(3)
        start, finish = _chip_exchange(send, landed, sems, slotted=True)
        pl.when(first)(start)

        @pl.when(n == 0)
        def _():
            dst_scr[...] = jnp.zeros_like(dst_scr)
            gsum_scr[...] = jnp.zeros_like(gsum_scr)
            dlam_ref[...] = jnp.zeros_like(dlam_ref)

        lv = lv_ref[...]
        r, c = _iota2((BLOCK, BLOCK), 0), _iota2((BLOCK, BLOCK), 1)
        for hh in range(HEAD_GROUP):
            ls = slice(hh * D_HEAD, (hh + 1) * D_HEAD)
            lam = lam_ref[:, ls]
            q, v, do = q_ref[:, ls], v_ref[:, ls], do_ref[:, ls]
            sg, f, g, k, live = _hgrn_gates(f_ref[:, ls], lam, chunk)
            b, exps = _hgrn_exponents(g, sums_ref[...])
            do16, v16 = do.astype(_MXU), v.astype(_MXU)
            da = lax.dot_general(do16, v16, _NT, preferred_element_type=F32)
            da_sym = jnp.where(c < r, da, da.T)
            dq = jnp.zeros((BLOCK, D_HEAD), F32)
            dqk = jnp.zeros((BLOCK, D_HEAD), F32)
            db_q = jnp.zeros((BLOCK, D_HEAD), F32)
            db_qk = jnp.zeros((BLOCK, D_HEAD), F32)
            for li, m in enumerate(LEVELS):
                is_q, decay, x16 = _level_operand(q, k, exps[li], m)
                y = jnp.dot(jnp.where(lv == li, da_sym, 0.0).astype(_MXU), x16, preferred_element_type=F32)
                dx = y * decay
                dq = dq + jnp.where(is_q, dx, 0.0)
                dqk = dqk + dx
                p = x16.astype(F32) * y
                db_q = db_q + jnp.where(is_q, p, 0.0)
                db_qk = db_qk + p
            dk = dqk - dq
            db = 2.0 * db_q - db_qk
            a_t = jnp.where(c >= r, sym_ref[0, hh, 0], 0)
            st, dst = sst_ref[0, hh, 0], dst_scr[hh]
            eb = jnp.exp(b)
            b_end = b[BLOCK - 1:BLOCK, :]
            dec = jnp.exp(b_end - b)
            qh16, kt16 = (q * eb).astype(_MXU), (k * dec).astype(_MXU)
            dq_st = _mm_s(do16, st)
            dk_st = _mm_s(v16, dst)
            d_diag = jnp.sum(do * v, axis=1, keepdims=True)
            dq_ref[:, ls] = (dq + d_diag * k + eb * dq_st).astype(dq_ref.dtype)
            dk = dk + d_diag * q + dec * dk_st
            dv_ref[:, ls] = (jnp.dot(a_t, do16, preferred_element_type=F32) + _mm_nt_s(kt16, dst)).astype(dv_ref.dtype)
            dst_scr[hh] = dst * jnp.exp(b_end) + lax.dot_general(do16, qh16, _TN, preferred_element_type=F32)
            db = db + (qh16.astype(F32) * dq_st - kt16.astype(F32) * dk_st)
            dg = _mm01_left((c >= r).astype(_MXU), db) + gsum_scr[:, ls]
            gsum_scr[:, ls] = gsum_scr[:, ls] + jnp.sum(db, axis=0, keepdims=True)
            slope = (1.0 - lam) * sg * (1.0 - sg)
            df_ref[:, ls] = jnp.where(live, dg * slope / f - dk * slope, 0.0).astype(df_ref.dtype)
            dl = jnp.where(live, (dg / f - dk) * (1.0 - sg), 0.0)
            dlam_ref[0, :, ls] = dlam_ref[0, :, ls] + jnp.sum(dl, axis=0, keepdims=True)
        pl.when(last)(finish)

    blk = pl.BlockSpec((BLOCK, width), lambda b, h, n: (b * nb + nb - 1 - n, h))
    hbm = pl.BlockSpec(memory_space=pl.ANY)
    return pl.pallas_call(
        body, name="hgrn_bwd", grid=(bsz, N_HEADS // HEAD_GROUP, nb),
        in_specs=[blk, blk, blk, pl.BlockSpec((1, width), lambda b, h, n: (0, h)),
                  pl.BlockSpec((1, HEAD_GROUP, 1, D_HEAD, D_HEAD), lambda b, h, n: (b, h, nb - 1 - n, 0, 0)),
                  pl.BlockSpec((1, HEAD_GROUP, 1, D_HEAD, D_HEAD), lambda b, h, n: (b, h, nb - 1 - n, 0, 0)),
                  blk, pl.BlockSpec(sums.shape, lambda b, h, n: (0, 0)), pl.BlockSpec(levels.shape, lambda b, h, n: (0, 0))]
        + [hbm] * n_out,
        out_specs=[blk, blk, blk, pl.BlockSpec((1, 1, width), lambda b, h, n: (b, 0, h))] + [hbm] * n_out,
        out_shape=[jax.ShapeDtypeStruct((t, D_MODEL), _MXU)] * 3 + [jax.ShapeDtypeStruct((bsz, 1, D_MODEL), F32)]
        + [jax.ShapeDtypeStruct(a.shape, a.dtype) for a in outgoing],
        scratch_shapes=[pltpu.VMEM((HEAD_GROUP, D_HEAD, D_HEAD), F32), pltpu.VMEM((1, width), F32)]
        + _exchange_scratch(n_out),
        compiler_params=_params(("arbitrary", "arbitrary", "arbitrary")),
    )(qs, fs, vs, lam, sst, sym, do, sums, levels, *outgoing)


def _sb_valid(ahead, col, i, j):
    return (ahead < (i - j) * TILE) & (col >= N_PAD - j * TILE)


def _sb_logits(q16, k_blk, valid):
    z = jnp.where(valid, lax.dot_general(q16, k_blk.astype(_MXU), _NT, preferred_element_type=F32) * SB_SCALE, MASKED)
    softplus = jnp.where(z > SOFTPLUS_LINEAR, z, jnp.log(1.0 + jnp.exp(jnp.minimum(z, SOFTPLUS_LINEAR))))
    return -softplus, z - softplus


def _sb_fwd(qs, ks, vs, bsz, nq):
    t = qs.shape[0]
    lp = nq * TILE
    width = HEAD_GROUP * D_HEAD
    groups = N_HEADS // HEAD_GROUP
    lanes = [slice(hh * D_HEAD, (hh + 1) * D_HEAD) for hh in range(HEAD_GROUP)]

    def body(q_ref, k_ref, v_ref, o_ref, c_ref, n_ref):
        b, h, i = pl.program_id(0), pl.program_id(1), pl.program_id(2)
        q16 = [q_ref[:, ls].astype(_MXU) for ls in lanes]
        r, c = _iota2((TILE, TILE), 0), _iota2((TILE, TILE), 1)
        after = (r > c).astype(_MXU)

        def more(carry):
            jj, _, _, top = carry
            return (jj <= i) & (top > UNDERFLOW)

        def step(carry):
            jj, accs, sums, _ = carry
            j = i - jj
            ks_ = pl.ds(pl.multiple_of(j * TILE, TILE), TILE)
            valid = _sb_valid(c - r, c, i, j)
            new_accs, new_sums = [], []
            for hh, ls in enumerate(lanes):
                keep, log_beta = _sb_logits(q16[hh], k_ref[ks_, ls], valid)
                after_s = _mm01_right(keep, after)
                a = jnp.exp(log_beta + (sums[hh] + after_s))
                new_accs.append(accs[hh] + _mm(a, v_ref[ks_, ls]))
                new_sums.append(sums[hh] + (after_s[:, 0:1] + keep[:, 0:1]))
            top = functools.reduce(jnp.maximum, [jnp.max(x) for x in new_sums])
            return jj + 1, tuple(new_accs), tuple(new_sums), top

        init = (jnp.int32(0), tuple(jnp.zeros((TILE, D_HEAD), F32) for _ in lanes),
                tuple(jnp.zeros((TILE, 1), F32) for _ in lanes), jnp.float32(0.0))
        visited, accs, sums, _ = lax.while_loop(more, step, init)
        for hh, ls in enumerate(lanes):
            o_ref[:, ls] = accs[hh]
            c_ref[:, ls] = jnp.broadcast_to(sums[hh], (TILE, D_HEAD))
        n_ref[(b * groups + h) * nq + i] = visited.astype(F32)

    blk = pl.BlockSpec((TILE, width), lambda b, h, i: (b * nq + i, h))
    seq = pl.BlockSpec((lp, width), lambda b, h, i: (b, h))
    return pl.pallas_call(
        body, name="sb_fwd", grid=(bsz, groups, nq),
        in_specs=[blk, seq, seq], out_specs=[blk, blk, pl.BlockSpec(memory_space=pltpu.SMEM)],
        out_shape=[jax.ShapeDtypeStruct((t, D_MODEL), F32)] * 2 + [jax.ShapeDtypeStruct((bsz * groups * nq,), F32)],
        compiler_params=_params(("arbitrary", "arbitrary", "arbitrary")),
    )(qs, ks, vs)


def _sb_bwd(qs, ks, vs, ctot, visited, do, bsz, nq):
    t = qs.shape[0]
    lp = nq * TILE
    width = HEAD_GROUP * D_HEAD
    groups = N_HEADS // HEAD_GROUP
    lanes = [slice(hh * D_HEAD, (hh + 1) * D_HEAD) for hh in range(HEAD_GROUP)]

    def body(n_ref, q_ref, k_ref, v_ref, c_ref, do_ref, dq_ref, dk_ref, dv_ref, dk_acc, dv_acc):
        b, h, i = pl.program_id(0), pl.program_id(1), pl.program_id(2)

        @pl.when(i == 0)
        def _():
            dk_acc[...] = jnp.zeros_like(dk_acc)
            dv_acc[...] = jnp.zeros_like(dv_acc)

        q16 = [q_ref[:, ls].astype(_MXU) for ls in lanes]
        do16 = [do_ref[:, ls].astype(_MXU) for ls in lanes]
        q16_t = [q_ref[:, ls].astype(F32).T.astype(_MXU) for ls in lanes]
        do16_t = [do_ref[:, ls].astype(F32).T.astype(_MXU) for ls in lanes]
        totals = [c_ref[:, hh * D_HEAD:hh * D_HEAD + 1] for hh in range(HEAD_GROUP)]
        r, c = _iota2((TILE, TILE), 0), _iota2((TILE, TILE), 1)
        upto = (r <= c).astype(_MXU)
        before = (r < c).astype(_MXU)
        first = jnp.maximum(i + 1 - n_ref[(b * groups + h) * nq + i].astype(jnp.int32), 0)

        def step(j, carry):
            ks_ = pl.ds(pl.multiple_of(j * TILE, TILE), TILE)
            valid = _sb_valid(c - r, c, i, j)
            out = []
            for hh, ls in enumerate(lanes):
                dq, keep_pre, g_pre = carry[hh]
                k_blk, v_blk = k_ref[ks_, ls], v_ref[ks_, ls]
                keep, log_beta = _sb_logits(q16[hh], k_blk, valid)
                keep_upto = _mm01_right(keep, upto)
                a = jnp.exp(log_beta + (totals[hh] - keep_pre - keep_upto))
                da = lax.dot_general(do16[hh], v_blk.astype(_MXU), _NT, preferred_element_type=F32)
                g = a * da
                g_inside = _mm01_right(g, before)
                g_before = g_pre + g_inside
                beta = jnp.exp(log_beta)
                dz16 = (g * (1.0 - beta) - beta * g_before).astype(_MXU)
                dq = dq + jnp.dot(dz16, k_blk.astype(_MXU), preferred_element_type=F32)
                dk_acc[ls, ks_] += SB_SCALE * jnp.dot(q16_t[hh], dz16, preferred_element_type=F32)
                dv_acc[ls, ks_] += jnp.dot(do16_t[hh], a.astype(_MXU), preferred_element_type=F32)
                out.append((dq, keep_pre + keep_upto[:, TILE - 1:TILE],
                            g_pre + (g_inside[:, TILE - 1:TILE] + g[:, TILE - 1:TILE])))
            return tuple(out)

        zero_col = jnp.zeros((TILE, 1), F32)
        init = tuple((jnp.zeros((TILE, D_HEAD), F32), zero_col, zero_col) for _ in lanes)
        res = lax.fori_loop(first, i + 1, step, init)
        for hh, ls in enumerate(lanes):
            dq_ref[:, ls] = (SB_SCALE * res[hh][0]).astype(dq_ref.dtype)

        @pl.when(i == nq - 1)
        def _():
            dk_ref[...] = dk_acc[...].T.astype(dk_ref.dtype)
            dv_ref[...] = dv_acc[...].T.astype(dv_ref.dtype)

    blk = pl.BlockSpec((TILE, width), lambda b, h, i: (b * nq + i, h))
    seq = pl.BlockSpec((lp, width), lambda b, h, i: (b, h))
    return pl.pallas_call(
        body, name="sb_bwd", grid=(bsz, groups, nq),
        in_specs=[pl.BlockSpec(memory_space=pltpu.SMEM), blk, seq, seq, blk, blk], out_specs=[blk, seq, seq],
        out_shape=[jax.ShapeDtypeStruct((t, D_MODEL), _MXU)] * 3,
        scratch_shapes=[pltpu.VMEM((width, lp), F32)] * 2,
        compiler_params=_params(("arbitrary", "arbitrary", "arbitrary")),
    )(visited, qs, ks, vs, ctot, do)


def _head_norm(o, head_gain):
    outs, rs = [], []
    for h in range(N_HEADS):
        oh = o[:, h * D_HEAD:(h + 1) * D_HEAD]
        r = lax.rsqrt(jnp.mean(oh * oh, axis=-1, keepdims=True) + EPS)
        outs.append(oh * r)
        rs.append(r)
    return outs, rs


def _mix(o, gate, head_gain):
    if head_gain is None:
        on = o
    else:
        outs, _ = _head_norm(o, head_gain)
        on = jnp.concatenate([x * head_gain for x in outs], axis=1)
    return on, on * (gate * _sigmoid(gate))


def _out_fwd(o, gate, h_in, w_out, post_gain, head_gain, name):
    t = o.shape[0]
    tm = _row_tile(t, 256)

    def body(o_ref, g_ref, h_ref, w_ref, pg_ref, hg_ref, ho_ref, u_ref):
        _, mix = _mix(o_ref[...], g_ref[...], hg_ref[...])
        u = jnp.dot(mix.astype(_MXU), w_ref[...], preferred_element_type=F32)
        u_ref[...] = u
        r = lax.rsqrt(jnp.mean(u * u, axis=-1, keepdims=True) + EPS)
        ho_ref[...] = h_ref[...] + u * r * pg_ref[...]

    row = pl.BlockSpec((tm, D_MODEL), lambda i: (i, 0))
    vec = pl.BlockSpec((1, D_MODEL), lambda i: (0, 0))
    return pl.pallas_call(
        body, name=name, grid=(t // tm,),
        in_specs=[row, row, row, pl.BlockSpec((D_MODEL, D_MODEL), lambda i: (0, 0)), vec,
                  pl.BlockSpec((1, D_HEAD), lambda i: (0, 0))],
        out_specs=[row, row], out_shape=[jax.ShapeDtypeStruct((t, D_MODEL), F32)] * 2,
        compiler_params=_params(("arbitrary",)),
    )(o, gate, h_in, w_out, post_gain, head_gain)


def _out_fwd_loss(o, gate, h_in, w_out, post_gain, target, nq, name):
    t = o.shape[0]

    def body(o_ref, g_ref, h_ref, w_ref, pg_ref, t_ref, dh_ref, u_ref, l_ref):
        i = pl.program_id(0)

        @pl.when(i == 0)
        def _():
            l_ref[...] = jnp.zeros_like(l_ref)

        _, mix = _mix(o_ref[...], g_ref[...], None)
        u = jnp.dot(mix.astype(_MXU), w_ref[...], preferred_element_type=F32)
        u_ref[...] = u

        @pl.when(i % nq == 0)
        def _():
            dh_ref[...] = jnp.zeros_like(dh_ref)

        @pl.when(i % nq != 0)
        def _():
            r = lax.rsqrt(jnp.mean(u * u, axis=-1, keepdims=True) + EPS)
            e = h_ref[...] + u * r * pg_ref[...] - t_ref[...]
            dh_ref[...] = e * (1.0 / D_MODEL)
            l_ref[...] += jnp.sum(e * e) * (0.5 / D_MODEL)

    row = pl.BlockSpec((TILE, D_MODEL), lambda i: (i, 0))
    vec = pl.BlockSpec((1, D_MODEL), lambda i: (0, 0))
    return pl.pallas_call(
        body, name=name, grid=(t // TILE,),
        in_specs=[row, row, row, pl.BlockSpec((D_MODEL, D_MODEL), lambda i: (0, 0)), vec,
                  pl.BlockSpec((TILE, D_MODEL), lambda i: ((i // nq) * (nq - 1) + jnp.maximum(i % nq - 1, 0), 0))],
        out_specs=[row, row, pl.BlockSpec((8, 128), lambda i: (0, 0))],
        out_shape=[jax.ShapeDtypeStruct((t, D_MODEL), F32)] * 2 + [jax.ShapeDtypeStruct((8, 128), F32)],
        compiler_params=_params(("arbitrary",)),
    )(o, gate, h_in, w_out, post_gain, target)


def _out_bwd(dh, u, o, gate, w_out, post_gain, head_gain, name, narrow_do):
    t = o.shape[0]
    tm = _row_tile(t, 2 * TILE)
    has_head = head_gain is not None

    def body(*refs):
        if has_head:
            dh_ref, u_ref, o_ref, g_ref, w_ref, pg_ref, hg_ref, do_ref, dg_ref, gw_ref, gw16_ref, gp_ref, gh_ref = refs
            hg = hg_ref[...]
        else:
            dh_ref, u_ref, o_ref, g_ref, w_ref, pg_ref, do_ref, dg_ref, gw_ref, gw16_ref, gp_ref = refs
            hg = None
        first = pl.program_id(0) == 0

        @pl.when(first)
        def _():
            gw_ref[...] = jnp.zeros_like(gw_ref)
            gp_ref[...] = jnp.zeros_like(gp_ref)
            if has_head:
                gh_ref[...] = jnp.zeros_like(gh_ref)

        dr, u, o, gate = dh_ref[...], u_ref[...], o_ref[...], g_ref[...]
        r = lax.rsqrt(jnp.mean(u * u, axis=-1, keepdims=True) + EPS)
        un = u * r
        gp_ref[...] += jnp.sum(dr * un, axis=0, keepdims=True)
        dun = dr * pg_ref[...]
        du = r * (dun - un * jnp.mean(dun * un, axis=-1, keepdims=True))
        on, mix = _mix(o, gate, hg)
        du16 = du.astype(_MXU)
        gw_ref[...] += lax.dot_general(mix.astype(_MXU), du16, _TN, preferred_element_type=F32)
        dmix = lax.dot_general(du16, w_ref[...], _NT, preferred_element_type=F32)
        sg = _sigmoid(gate)
        dg_ref[...] = (dmix * on * (sg * (1.0 + gate * (1.0 - sg)))).astype(dg_ref.dtype)
        don = dmix * (gate * sg)
        if has_head:
            outs, rs = _head_norm(o, hg)
            gh = jnp.zeros((1, D_HEAD), F32)
            cols = []
            for h in range(N_HEADS):
                dn = don[:, h * D_HEAD:(h + 1) * D_HEAD]
                gh = gh + jnp.sum(dn * outs[h], axis=0, keepdims=True)
                dnn = dn * hg
                cols.append(rs[h] * (dnn - outs[h] * jnp.mean(dnn * outs[h], axis=-1, keepdims=True)))
            gh_ref[...] += gh
            do_ref[...] = jnp.concatenate(cols, axis=1)
        else:
            do_ref[...] = don.astype(do_ref.dtype)

        @pl.when(pl.program_id(0) == pl.num_programs(0) - 1)
        def _():
            gw16_ref[...] = gw_ref[...].astype(_MXU)

    row = pl.BlockSpec((tm, D_MODEL), lambda i: (i, 0))
    vec = pl.BlockSpec((1, D_MODEL), lambda i: (0, 0))
    mat = pl.BlockSpec((D_MODEL, D_MODEL), lambda i: (0, 0))
    in_specs = [row, row, row, row, mat, vec]
    args = [dh, u, o, gate, w_out, post_gain]
    out_specs = [row, row, mat, mat, vec]
    out_shape = [jax.ShapeDtypeStruct((t, D_MODEL), _MXU if narrow_do else F32),
                 jax.ShapeDtypeStruct((t, D_MODEL), _MXU)] + [jax.ShapeDtypeStruct((D_MODEL, D_MODEL), F32),
                                                                  jax.ShapeDtypeStruct((D_MODEL, D_MODEL), _MXU),
                                                                  jax.ShapeDtypeStruct((1, D_MODEL), F32)]
    if has_head:
        in_specs.append(pl.BlockSpec((1, D_HEAD), lambda i: (0, 0)))
        args.append(head_gain)
        out_specs.append(pl.BlockSpec((1, D_HEAD), lambda i: (0, 0)))
        out_shape.append(jax.ShapeDtypeStruct((1, D_HEAD), F32))
    return pl.pallas_call(
        body, name=name, grid=(t // tm,), in_specs=in_specs, out_specs=out_specs, out_shape=out_shape,
        compiler_params=_params(("arbitrary",)),
    )(*args)


def _proj_bwd(ds, w4, h_in, gain, dh_out, name, outgoing=(), nq=None):
    t = h_in.shape[0]
    tm = _row_tile(t, 256) if nq is None else TILE
    n_out = len(outgoing)
    n_dh = 1 if nq is None else 2

    def body(d0, d1, d2, d3, w_ref, h_ref, g_ref, dho_ref, *rest):
        send, dh_refs, gg_ref, rest = rest[:n_out], rest[n_out:n_out + n_dh], rest[n_out + n_dh], rest[n_out + n_dh + 1:]
        landed, sems = rest[:n_out], rest[n_out:]
        if n_out:
            first, last = _grid_ends(1)
            start, finish = _chip_exchange(send, landed, sems, slotted=True)
            pl.when(first)(start)

        @pl.when(pl.program_id(0) == 0)
        def _():
            gg_ref[...] = jnp.zeros_like(gg_ref)

        dy = jnp.zeros((tm, D_MODEL), F32)
        for j, d in enumerate((d0, d1, d2, d3)):
            dy = dy + lax.dot_general(d[...].astype(_MXU), w_ref[j], _NT, preferred_element_type=F32)
        x = h_ref[...]
        r = lax.rsqrt(jnp.mean(x * x, axis=-1, keepdims=True) + EPS)
        xn = x * r
        gg_ref[...] += jnp.sum(dy * xn, axis=0, keepdims=True)
        dxn = dy * g_ref[...]
        dh = dho_ref[...] + r * (dxn - xn * jnp.mean(dxn * xn, axis=-1, keepdims=True))
        if nq is None:
            dh_refs[0][...] = dh
        else:
            in_front = pl.program_id(0) % nq == 0

            @pl.when(in_front)
            def _():
                dh_refs[1][...] = dh

            @pl.when(jnp.logical_not(in_front))
            def _():
                dh_refs[0][...] = dh
        if n_out:
            pl.when(last)(finish)

    row = pl.BlockSpec((tm, D_MODEL), lambda i: (i, 0))
    vec = pl.BlockSpec((1, D_MODEL), lambda i: (0, 0))
    hbm = pl.BlockSpec(memory_space=pl.ANY)
    if nq is None:
        dh_specs, dh_shapes = [row], [jax.ShapeDtypeStruct((t, D_MODEL), F32)]
    else:
        dh_specs = [pl.BlockSpec((TILE, D_MODEL), lambda i: ((i // nq) * (nq - 1) + jnp.maximum(i % nq - 1, 0), 0)),
                    pl.BlockSpec((TILE, D_MODEL), lambda i: (i // nq, 0))]
        dh_shapes = [jax.ShapeDtypeStruct((t // nq * (nq - 1), D_MODEL), F32),
                     jax.ShapeDtypeStruct((t // nq, D_MODEL), F32)]
    return pl.pallas_call(
        body, name=name, grid=(t // tm,),
        in_specs=[row] * 4 + [pl.BlockSpec((4, D_MODEL, D_MODEL), lambda i: (0, 0, 0)), row, vec, row] + [hbm] * n_out,
        out_specs=dh_specs + [vec] + [hbm] * n_out,
        out_shape=dh_shapes + [jax.ShapeDtypeStruct((1, D_MODEL), F32)]
        + [jax.ShapeDtypeStruct(a.shape, a.dtype) for a in outgoing],
        scratch_shapes=_exchange_scratch(n_out) if n_out else [],
        compiler_params=_params(("arbitrary",)),
    )(*ds, w4, h_in, gain, dh_out, *outgoing)


def _weight_grad(y, d, name):
    t = y.shape[0]
    tk = _row_tile(t, t // 4)

    def body(y_ref, d_ref, g_ref, g16_ref):
        @pl.when(pl.program_id(0) == 0)
        def _():
            g_ref[...] = jnp.zeros_like(g_ref)

        g_ref[...] += lax.dot_general(y_ref[...], d_ref[...].astype(_MXU), _TN, preferred_element_type=F32)

        @pl.when(pl.program_id(0) == pl.num_programs(0) - 1)
        def _():
            g16_ref[...] = g_ref[...].astype(_MXU)

    row = pl.BlockSpec((tk, D_MODEL), lambda i: (i, 0))
    mat = pl.BlockSpec((D_MODEL, D_MODEL), lambda i: (0, 0))
    return pl.pallas_call(
        body, name=name, grid=(t // tk,), in_specs=[row, row], out_specs=[mat, mat],
        out_shape=[jax.ShapeDtypeStruct((D_MODEL, D_MODEL), F32), jax.ShapeDtypeStruct((D_MODEL, D_MODEL), _MXU)],
        compiler_params=_params(("arbitrary",)),
    )(y, d)


def _local_step(x, target, meta, pre_norm, post_norm, lam, head_gain, hw_in, shards):
    bsz, seq, _ = x.shape
    nq = seq // TILE + 1
    nb = nq * (TILE // BLOCK)
    lp = nq * TILE
    t = bsz * lp
    d4 = D_MODEL // N_CHIPS
    front = jnp.concatenate([jnp.zeros((N_PAD, D_MODEL), F32), meta], axis=0)
    h0 = jnp.concatenate([jnp.broadcast_to(front[None], (bsz, TILE, D_MODEL)), x], axis=1).reshape(t, D_MODEL)
    pre0, pre1, post0, post1 = pre_norm[0:1], pre_norm[1:2], post_norm[0:1], post_norm[1:2]

    y0, q0, f0, v0, g0 = _norm_proj(h0, pre0, hw_in, "norm_proj_hgrn", (False,) * 4)
    o0, sst, sym, sw_in, sw_out, hw_out = _hgrn_fwd(q0, f0, v0, lam, bsz, nb, shards)
    sw_out, hw_out = sw_out.reshape(D_MODEL, D_MODEL), hw_out.reshape(D_MODEL, D_MODEL)
    h1, u0 = _out_fwd(o0, g0, h0, hw_out, post0, head_gain, "out_fwd_hgrn")
    y1, q1, k1, v1, g1 = _norm_proj(h1, pre1, sw_in, "norm_proj_sb", (True, True, True, False))
    o1, ctot, visited = _sb_fwd(q1, k1, v1, bsz, nq)
    dh2, u1, loss_blk = _out_fwd_loss(o1, g1, h1, sw_out, post1, target.reshape(bsz * seq, D_MODEL), nq, "out_fwd_sb")

    do1, dg1, g_sw_out, g_sw_out16, g_post1 = _out_bwd(dh2, u1, o1, g1, sw_out, post1, None, "out_bwd_sb", True)
    dq1, dk1, dv1 = _sb_bwd(q1, k1, v1, ctot, visited, do1, bsz, nq)
    ds1 = (dq1, dk1, dv1, dg1)
    dh1, g_pre1 = _proj_bwd(ds1, sw_in, h1, pre1, dh2, "proj_bwd_sb")
    g_sw_in = [_weight_grad(y1, d, "wgrad_sb_%d" % j) for j, d in enumerate(ds1)]

    do0, dg0, g_hw_out, g_hw_out16, g_post0, g_head = _out_bwd(dh1, u0, o0, g0, hw_out, post0, head_gain, "out_bwd_hgrn",
                                                               False)
    ready = (jnp.stack([g16 for _, g16 in g_sw_in]), g_sw_out16.reshape(N_CHIPS, d4, D_MODEL),
             g_hw_out16.reshape(N_CHIPS, d4, D_MODEL))
    dq0, df0, dv0, dlam, land_sw_in, land_sw_out, land_hw_out = _hgrn_bwd(q0, f0, v0, lam, sst, sym, do0, bsz, nb, ready)
    ds0 = (dq0, df0, dv0, dg0)
    g_hw_in = [_weight_grad(y0, d, "wgrad_hgrn_%d" % j) for j, d in enumerate(ds0)]
    last = (jnp.stack([g16 for _, g16 in g_hw_in]),)
    grad_x, dh_front, g_pre0, land_hw_in = _proj_bwd(ds0, hw_in, h0, pre0, dh1, "proj_bwd_hgrn", last, nq)

    grad_x = grad_x.reshape(bsz, seq, D_MODEL)
    g_meta = jnp.sum(dh_front.reshape(bsz, TILE, D_MODEL)[:, N_PAD:, :], axis=0)
    g_lam = jnp.sum(dlam, axis=0)
    small = jnp.concatenate([g_pre0, g_pre1, g_post0, g_post1, g_lam, g_lam,
                             jnp.pad(g_head, ((0, 0), (0, D_MODEL - D_HEAD))), g_meta,
                             jnp.pad(loss_blk[0:1], ((0, 0), (0, D_MODEL - loss_blk.shape[1])))], axis=0)
    rows4 = lambda g: [g[j * d4:(j + 1) * d4] for j in range(N_CHIPS)]
    large = dict(hw_in=(land_hw_in, [g for g, _ in g_hw_in]), sw_in=(land_sw_in, [g for g, _ in g_sw_in]),
                 hw_out=(land_hw_out, rows4(g_hw_out)), sw_out=(land_sw_out, rows4(g_sw_out)))
    return grad_x, small, large


def _prep_weights(hw_in, sw_in, hw_out, sw_out, meta):
    def body(hi_ref, si_ref, ho_ref, so_ref, m_ref, ghi, gm, si16, so16, ho16, far_send, far_recv, near_send, near_recv):
        x, y, c = _place()
        me = 2 * x + y
        ghi[me] = hi_ref[0].astype(_MXU)
        gm[me] = m_ref[...]
        si16[...] = si_ref[0].astype(_MXU)
        so16[...] = so_ref[0].astype(_MXU)
        ho16[...] = ho_ref[0].astype(_MXU)
        outs = (ghi, gm)
        n = len(outs)
        peers = [(1 - x, y), (x, 1 - y), (1 - x, 1 - y)]

        def half(a, slot, which):
            rows = outs[a].shape[1] // 2
            return outs[a].at[slot, pl.ds(which * rows, rows), :]

        def far(r, a, slot):
            px, py = peers[r]
            return pltpu.make_async_remote_copy(
                src_ref=half(a, slot, c), dst_ref=half(a, slot, c), send_sem=far_send.at[r * n + a],
                recv_sem=far_recv.at[r * n + a], device_id=(px, py, c), device_id_type=MESH)

        def near(r, a, which):
            px, py = peers[r]
            return pltpu.make_async_remote_copy(
                src_ref=half(a, 2 * px + py, which), dst_ref=half(a, 2 * px + py, which),
                send_sem=near_send.at[r * n + a], recv_sem=near_recv.at[r * n + a],
                device_id=(x, y, 1 - c), device_id_type=MESH)

        for r in range(3):
            for a in range(n):
                far(r, a, me).start()
        for r, (px, py) in enumerate(peers):
            for a in range(n):
                far(r, a, 2 * px + py).wait_recv()
                near(r, a, c).start()
        for r in range(3):
            for a in range(n):
                near(r, a, 1 - c).wait_recv()
        for r in range(3):
            for a in range(n):
                far(r, a, me).wait_send()
                near(r, a, c).wait_send()

    d4 = D_MODEL // N_CHIPS
    vm = pl.BlockSpec(memory_space=pltpu.VMEM)
    return pl.pallas_call(
        body, name="prep_weights",
        in_specs=[vm] * 5, out_specs=[vm] * 5,
        out_shape=[jax.ShapeDtypeStruct((N_CHIPS, D_MODEL, D_MODEL), _MXU), jax.ShapeDtypeStruct((N_CHIPS, N_META, d4), F32),
                   jax.ShapeDtypeStruct((D_MODEL, D_MODEL), _MXU), jax.ShapeDtypeStruct((d4, D_MODEL), _MXU),
                   jax.ShapeDtypeStruct((d4, D_MODEL), _MXU)],
        scratch_shapes=[pltpu.SemaphoreType.DMA((6,))] * 4,
        compiler_params=pltpu.CompilerParams(vmem_limit_bytes=VMEM_LIMIT),
    )(hw_in, sw_in, hw_out, sw_out, meta)


def _scatter_small(small):
    def body(sm, lsm, send_sems, recv_sems, local_sem):
        x, y, c = _place()
        mine = 4 * x + 2 * y + c
        local = pltpu.make_async_copy(sm, lsm.at[mine], local_sem)
        local.start()

        def copy(rel, src_dev, to):
            return pltpu.make_async_remote_copy(
                src_ref=sm, dst_ref=lsm.at[src_dev], send_sem=send_sems.at[rel - 1], recv_sem=recv_sems.at[rel - 1],
                device_id=to, device_id_type=MESH)

        flip = lambda bit, v: 1 - v if bit else v
        rels = [(rel, flip(rel & 4, x), flip(rel & 2, y), flip(rel & 1, c)) for rel in range(1, N_DEV)]
        sends = [copy(rel, mine, (px, py, pc)) for rel, px, py, pc in rels]
        for cp in sends:
            cp.start()
        for rel, px, py, pc in rels:
            copy(rel, 4 * px + 2 * py + pc, (px, py, pc)).wait_recv()
        for cp in sends:
            cp.wait_send()
        local.wait()

    hbm = pl.BlockSpec(memory_space=pl.ANY)
    return pl.pallas_call(
        body, name="scatter_small", in_specs=[hbm], out_specs=hbm,
        out_shape=jax.ShapeDtypeStruct((N_DEV, SMALL_ROWS, D_MODEL), F32),
        scratch_shapes=[pltpu.SemaphoreType.DMA((N_DEV - 1,)), pltpu.SemaphoreType.DMA((N_DEV - 1,)),
                        pltpu.SemaphoreType.DMA(())],
    )(small)


def _sum_slots(landed, own, me, name):
    n, rows, _ = landed.shape
    tm = rows if rows < 256 else 256

    def body(me_ref, l_ref, o0, o1, o2, o3, out_ref):
        acc = None
        for k, o in enumerate((o0, o1, o2, o3)):
            term = jnp.where(me_ref[0] == k, o[...], l_ref[k].astype(F32))
            acc = term if acc is None else acc + term
        out_ref[...] = acc

    blk = pl.BlockSpec((tm, D_MODEL), lambda i: (i, 0))
    return pl.pallas_call(
        body, name=name, grid=(rows // tm,),
        in_specs=[pl.BlockSpec(memory_space=pltpu.SMEM), pl.BlockSpec((n, tm, D_MODEL), lambda i: (0, i, 0))] + [blk] * 4,
        out_specs=blk, out_shape=jax.ShapeDtypeStruct((rows, D_MODEL), F32),
        compiler_params=_params(("arbitrary",)),
    )(me, landed, *own)


def _swap_with_sibling(parts):
    def body(a0, a1, a2, a3, b0, b1, b2, b3, send_sems, recv_sems):
        x, y, c = _place()
        copies = [pltpu.make_async_remote_copy(src_ref=s, dst_ref=d, send_sem=send_sems.at[a], recv_sem=recv_sems.at[a],
                                               device_id=(x, y, 1 - c), device_id_type=MESH)
                  for a, (s, d) in enumerate(zip((a0, a1, a2, a3), (b0, b1, b2, b3)))]
        for cp in copies:
            cp.start()
        for cp in copies:
            cp.wait()

    hbm = pl.BlockSpec(memory_space=pl.ANY)
    return pl.pallas_call(
        body, name="swap_with_sibling", in_specs=[hbm] * 4, out_specs=[hbm] * 4,
        out_shape=[jax.ShapeDtypeStruct(p.shape, F32) for p in parts],
        scratch_shapes=[pltpu.SemaphoreType.DMA((4,)), pltpu.SemaphoreType.DMA((4,))],
    )(*parts)


def _adamw_math(w, g, m, v):
    m = ADAM_B1 * m + (1.0 - ADAM_B1) * g
    v = ADAM_B2 * v + (1.0 - ADAM_B2) * (g * g)
    m_hat = m / (1.0 - ADAM_B1 ** ADAM_STEP)
    v_hat = v / (1.0 - ADAM_B2 ** ADAM_STEP)
    delta = -ADAM_LR * (m_hat / (jnp.sqrt(v_hat) + ADAM_EPS) + ADAM_WD * w)
    return delta, m, v


def _adamw(w, g_parts, m, v, name):
    rows, cols = w.shape
    tm = rows if rows < 256 else 256
    n = len(g_parts)

    def body(*refs):
        w_ref, m_ref, v_ref = refs[n:n + 3]
        g_ref, d_ref, nm_ref, nv_ref = refs[n + 3:]
        g = refs[0][...]
        for p in refs[1:n]:
            g = g + p[...]
        g_ref[...] = g
        d_ref[...], nm_ref[...], nv_ref[...] = _adamw_math(w_ref[...], g, m_ref[...], v_ref[...])

    blk = pl.BlockSpec((tm, cols), lambda i: (i, 0))
    return pl.pallas_call(
        body, name=name, grid=(rows // tm,), in_specs=[blk] * (n + 3), out_specs=[blk] * 4,
        out_shape=[jax.ShapeDtypeStruct((rows, cols), F32)] * 4,
        compiler_params=_params(("arbitrary",)),
    )(*g_parts, w, m, v)


def _lam_of(hgrn_lb):
    def body(lb_ref, o_ref):
        lb = lb_ref[...]
        e = jnp.exp(lb - jnp.max(lb, axis=0, keepdims=True))
        o_ref[...] = e[0:1, :] / jnp.sum(e, axis=0, keepdims=True)

    return pl.pallas_call(body, name="lam_of", out_shape=jax.ShapeDtypeStruct((1, D_MODEL), F32))(hgrn_lb)


def _small_grads(land_small, lam):
    def body(l_ref, lam_ref, o_ref):
        acc = l_ref[0]
        for k in range(1, N_DEV):
            acc = acc + l_ref[k]
        p = lam_ref[...]
        slope = p * (1.0 - p)
        row = _iota2((SMALL_ROWS, D_MODEL), 0)
        o_ref[...] = acc * jnp.where(row == 4, slope, jnp.where(row == 5, -slope, 1.0))

    return pl.pallas_call(body, name="small_grads",
                          out_shape=jax.ShapeDtypeStruct((SMALL_ROWS, D_MODEL), F32))(land_small, lam)


def kernel(x, meta_tokens, pre_norm, post_norm, hgrn_w_in, hgrn_lb, hgrn_out_norm, hgrn_w_out, sb_w_in, sb_w_out, loss_target, m_meta_tokens, m_pre_norm, m_post_norm, m_hgrn_w_in, m_hgrn_lb, m_hgrn_out_norm, m_hgrn_w_out, m_sb_w_in, m_sb_w_out, v_meta_tokens, v_pre_norm, v_post_norm, v_hgrn_w_in, v_hgrn_lb, v_hgrn_out_norm, v_hgrn_w_out, v_sb_w_in, v_sb_w_out):
    d4 = D_MODEL // N_CHIPS
    chip = 2 * lax.axis_index("x") + lax.axis_index("y")
    hw_in, meta4, sw_in16, sw_out16, hw_out16 = _prep_weights(hgrn_w_in, sb_w_in, hgrn_w_out, sb_w_out, meta_tokens)
    meta = meta4.transpose(1, 0, 2).reshape(N_META, D_MODEL)
    lam = _lam_of(hgrn_lb)
    grad_x, small, large = _local_step(
        x, loss_target, meta, pre_norm, post_norm, lam, hgrn_out_norm,
        hw_in, (sw_in16, sw_out16, hw_out16))

    me = jnp.reshape(chip, (1,)).astype(jnp.int32)
    parts = [_sum_slots(*large[n], me, "sum_" + n) for n in ("hw_in", "sw_in", "hw_out", "sw_out")]
    sib = _swap_with_sibling(parts)
    small = _small_grads(_scatter_small(small), lam)
    loss = small[SMALL_ROWS - 1, 0]

    res = {}
    res["hgrn_w_in"] = _adamw(hgrn_w_in[0], [parts[0], sib[0]], m_hgrn_w_in[0], v_hgrn_w_in[0], "adamw_hw_in")
    res["sb_w_in"] = _adamw(sb_w_in[0], [parts[1], sib[1]], m_sb_w_in[0], v_sb_w_in[0], "adamw_sw_in")
    res["hgrn_w_out"] = _adamw(hgrn_w_out[0], [parts[2], sib[2]], m_hgrn_w_out[0], v_hgrn_w_out[0], "adamw_hw_out")
    res["sb_w_out"] = _adamw(sb_w_out[0], [parts[3], sib[3]], m_sb_w_out[0], v_sb_w_out[0], "adamw_sw_out")
    res["pre_norm"] = _adamw(pre_norm, [small[0:2]], m_pre_norm, v_pre_norm, "adamw_pre")
    res["post_norm"] = _adamw(post_norm, [small[2:4]], m_post_norm, v_post_norm, "adamw_post")
    res["hgrn_lb"] = _adamw(hgrn_lb, [small[4:6]], m_hgrn_lb, v_hgrn_lb, "adamw_lb")
    res["hgrn_out_norm"] = _adamw(hgrn_out_norm, [small[6:7, :D_HEAD]], m_hgrn_out_norm, v_hgrn_out_norm, "adamw_head")
    g_meta = lax.dynamic_slice_in_dim(small[7:7 + N_META], chip * d4, d4, axis=1)
    res["meta_tokens"] = _adamw(meta_tokens, [g_meta], m_meta_tokens, v_meta_tokens, "adamw_meta")
    for n in ("hgrn_w_in", "hgrn_w_out", "sb_w_in", "sb_w_out"):
        res[n] = tuple(a[None] for a in res[n])
    order = ("meta_tokens", "pre_norm", "post_norm", "hgrn_w_in", "hgrn_lb", "hgrn_out_norm", "hgrn_w_out",
             "sb_w_in", "sb_w_out")
    return (loss, grad_x, *[res[n][0] for n in order], *[res[n][1] for n in order],
            *[res[n][2] for n in order], *[res[n][3] for n in order])
```

```python
import functools

import jax
import numpy as np
import jax.numpy as jnp
from jax import lax
from jax.experimental import pallas as pl
from jax.experimental.pallas import tpu as pltpu

F32 = jnp.float32
_MXU = jnp.bfloat16

D_MODEL = 1024
N_HEADS = 8
D_HEAD = 128
BLOCK = 128
N_META = 16
TILE = 256
N_PAD = TILE - N_META
UNDERFLOW = -105.0
EPS = 1e-6
SB_SCALE = D_HEAD ** -0.5
SOFTPLUS_LINEAR = 20.0
MASKED = -1e30
ADAM_LR, ADAM_B1, ADAM_B2, ADAM_EPS, ADAM_WD, ADAM_STEP = 0.001, 0.9, 0.999, 1e-08, 0.01, 10
N_CHIPS = 4
N_DEV = 8
SMALL_ROWS = 24
VMEM_LIMIT = 56 * 1024 * 1024
MESH = pl.DeviceIdType.MESH

_NT = (((1,), (1,)), ((), ()))
_TN = (((0,), (0,)), ((), ()))


def _mm(a, b):
    return jnp.dot(a.astype(_MXU), b.astype(_MXU), preferred_element_type=F32)


def _mm_nt(a, b):
    return lax.dot_general(a.astype(_MXU), b.astype(_MXU), _NT, preferred_element_type=F32)


def _mm_tn(a, b):
    return lax.dot_general(a.astype(_MXU), b.astype(_MXU), _TN, preferred_element_type=F32)


def _split2(x):
    hi = x.astype(_MXU)
    return hi, (x - hi.astype(F32)).astype(_MXU)


def _mm_s(a16, state):
    hi, lo = _split2(state)
    return jnp.dot(a16, hi, preferred_element_type=F32) + jnp.dot(a16, lo, preferred_element_type=F32)


def _mm_nt_s(a16, state):
    hi, lo = _split2(state)
    return (lax.dot_general(a16, hi, _NT, preferred_element_type=F32)
            + lax.dot_general(a16, lo, _NT, preferred_element_type=F32))


def _mm01_right(x, m01):
    return jnp.dot(x.astype(_MXU), m01, preferred_element_type=F32)


def _mm01_left(m01, x):
    hi, lo = _split2(x)
    return jnp.dot(m01, hi, preferred_element_type=F32) + jnp.dot(m01, lo, preferred_element_type=F32)


def _iota2(shape, dim):
    return lax.broadcasted_iota(jnp.int32, shape, dim)


def _row_tile(total, pref):
    t = pref
    while total % t:
        t -= BLOCK
    return t


def _params(sem, limit=VMEM_LIMIT):
    return pltpu.CompilerParams(dimension_semantics=sem, vmem_limit_bytes=limit)


def _sigmoid(x):
    return 1.0 / (1.0 + jnp.exp(-x))


def _grid_ends(ndim):
    first, last = True, True
    for d in range(ndim):
        first = first & (pl.program_id(d) == 0)
        last = last & (pl.program_id(d) == pl.num_programs(d) - 1)
    return first, last


def _place():
    return lax.axis_index("x"), lax.axis_index("y"), lax.axis_index("c")


def _exchange_scratch(n):
    return [pltpu.SemaphoreType.DMA((3 * n,)), pltpu.SemaphoreType.DMA((3 * n,)), pltpu.SemaphoreType.DMA((n,))]


def _chip_exchange(srcs, dsts, sems, slotted):
    send_sems, recv_sems, local_sems = sems
    x, y, c = _place()
    me = 2 * x + y
    peers = [(1 - x, y), (x, 1 - y), (1 - x, 1 - y)]
    n = len(dsts)

    def remote(r, a, sending):
        px, py = peers[r]
        p = 2 * px + py
        return pltpu.make_async_remote_copy(
            src_ref=srcs[a].at[p] if slotted else srcs[a], dst_ref=dsts[a].at[me if sending else p],
            send_sem=send_sems.at[r * n + a], recv_sem=recv_sems.at[r * n + a],
            device_id=(px, py, c), device_id_type=MESH)

    def local(a):
        return pltpu.make_async_copy(srcs[a].at[me] if slotted else srcs[a], dsts[a].at[me], local_sems.at[a])

    def start():
        for a in range(n):
            local(a).start()
        for r in range(3):
            for a in range(n):
                remote(r, a, True).start()

    def finish():
        for r in range(3):
            for a in range(n):
                remote(r, a, False).wait_recv()
        for r in range(3):
            for a in range(n):
                remote(r, a, True).wait_send()
        for a in range(n):
            local(a).wait()

    return start, finish


def _norm_proj(h, gain, w4, name, narrow):
    t = h.shape[0]
    tm = _row_tile(t, 256)

    def body(h_ref, g_ref, w_ref, y_ref, s0, s1, s2, s3):
        x = h_ref[...]
        r = lax.rsqrt(jnp.mean(x * x, axis=-1, keepdims=True) + EPS)
        y = (x * r * g_ref[...]).astype(_MXU)
        y_ref[...] = y
        for j, s in enumerate((s0, s1, s2, s3)):
            s[...] = jnp.dot(y, w_ref[j], preferred_element_type=F32).astype(s.dtype)

    row = pl.BlockSpec((tm, D_MODEL), lambda i: (i, 0))
    return pl.pallas_call(
        body, name=name, grid=(t // tm,),
        in_specs=[row, pl.BlockSpec((1, D_MODEL), lambda i: (0, 0)),
                  pl.BlockSpec((4, D_MODEL, D_MODEL), lambda i: (0, 0, 0))],
        out_specs=[row] * 5,
        out_shape=[jax.ShapeDtypeStruct((t, D_MODEL), _MXU)]
        + [jax.ShapeDtypeStruct((t, D_MODEL), _MXU if n else F32) for n in narrow],
        compiler_params=_params(("arbitrary",)),
    )(h, gain, w4)


LEVELS = (64, 32, 16, 8, 4, 2, 1)
HEAD_GROUP = 2


def _hgrn_tables():
    r = np.arange(BLOCK)
    mats = [r[None, :] <= r[:, None]]
    x = r[:, None] ^ r[None, :]
    lv = np.full((BLOCK, BLOCK), len(LEVELS), np.int32)
    for i, m in enumerate(LEVELS):
        lv[(x >= m) & (x < 2 * m)] = i
    return jnp.asarray(np.concatenate(mats, 0).astype(np.float32), dtype=_MXU), jnp.asarray(lv)


def _hgrn_exponents(g, sums):
    b = _mm01_left(sums, g)
    row = _iota2((BLOCK, D_HEAD), 0)
    out = []
    for m in LEVELS:
        is_q = (row & m) != 0
        if m >= 4:
            grp = b.reshape(BLOCK // (2 * m), 2 * m, D_HEAD)
            ref = jnp.broadcast_to(grp[:, m - 1:m, :], grp.shape).reshape(BLOCK, D_HEAD)
            d = b - ref
            out.append(jnp.where(is_q, d, -d))
        elif m == 2:
            below, above = pltpu.roll(g, 1, axis=0), pltpu.roll(g, BLOCK - 1, axis=0)
            low = row & 3
            out.append(jnp.where(low == 3, g + below, jnp.where(low == 2, g, jnp.where(low == 0, above, 0.0))))
        else:
            out.append(jnp.where(is_q, g, 0.0))
    return b, out


def _hgrn_gates(fz, lam, chunk):
    pos = chunk * BLOCK + _iota2((BLOCK, D_HEAD), 0)
    live = pos >= N_PAD
    sg = _sigmoid(fz)
    f = lam + (1.0 - lam) * sg
    g = jnp.where(live, jnp.log(f), 0.0)
    k = jnp.where(live, (1.0 - lam) * (1.0 - sg), 0.0)
    return sg, f, g, k, live


def _level_operand(q, k, exponent, m):
    decay = jnp.exp(exponent)
    is_q = (_iota2((BLOCK, D_HEAD), 0) & m) != 0
    return is_q, decay, (jnp.where(is_q, q, k) * decay).astype(_MXU)


def _hgrn_fwd(qs, fs, vs, lam, bsz, nb, shards):
    t = qs.shape[0]
    sums, levels = _hgrn_tables()
    width = HEAD_GROUP * D_HEAD

    n_sh = len(shards)

    def body(q_ref, f_ref, v_ref, lam_ref, sums_ref, lv_ref, *rest):
        own, (o_ref, sst_ref, sym_ref), rest = rest[:n_sh], rest[n_sh:n_sh + 3], rest[n_sh + 3:]
        gathered, st_scr, sems = rest[:n_sh], rest[n_sh], rest[n_sh + 1:]
        n = pl.program_id(2)
        first, last = _grid_ends(3)
        start, finish = _chip_exchange(own, gathered, sems, slotted=False)
        pl.when(first)(start)

        @pl.when(n == 0)
        def _():
            st_scr[...] = jnp.zeros_like(st_scr)

        lv = lv_ref[...]
        r, c = _iota2((BLOCK, BLOCK), 0), _iota2((BLOCK, BLOCK), 1)
        for hh in range(HEAD_GROUP):
            ls = slice(hh * D_HEAD, (hh + 1) * D_HEAD)
            st = st_scr[hh]
            sst_ref[0, hh, 0] = st
            q, v = q_ref[:, ls], v_ref[:, ls]
            _, _, g, k, _ = _hgrn_gates(f_ref[:, ls], lam_ref[:, ls], n)
            b, exps = _hgrn_exponents(g, sums_ref[...])
            sym = jnp.zeros((BLOCK, BLOCK), F32)
            for li, m in enumerate(LEVELS):
                _, _, x16 = _level_operand(q, k, exps[li], m)
                sym = jnp.where(lv == li, lax.dot_general(x16, x16, _NT, preferred_element_type=F32), sym)
            sym = jnp.where(c == r, jnp.sum(q * k, axis=1, keepdims=True), sym).astype(_MXU)
            sym_ref[0, hh, 0] = sym
            o_ref[:, ls] = _mm_nt(q * jnp.exp(b), st) + _mm(jnp.where(c <= r, sym, 0), v)
            b_end = b[BLOCK - 1:BLOCK, :]
            st_scr[hh] = st * jnp.exp(b_end) + _mm_tn(v, k * jnp.exp(b_end - b))
        pl.when(last)(finish)

    blk = pl.BlockSpec((BLOCK, width), lambda b, h, n: (b * nb + n, h))
    hbm = pl.BlockSpec(memory_space=pl.ANY)
    return pl.pallas_call(
        body, name="hgrn_fwd", grid=(bsz, N_HEADS // HEAD_GROUP, nb),
        in_specs=[blk, blk, blk, pl.BlockSpec((1, width), lambda b, h, n: (0, h)),
                  pl.BlockSpec(sums.shape, lambda b, h, n: (0, 0)), pl.BlockSpec(levels.shape, lambda b, h, n: (0, 0))]
        + [hbm] * n_sh,
        out_specs=[blk] + [pl.BlockSpec((1, HEAD_GROUP, 1, D_HEAD, D_HEAD), lambda b, h, n: (b, h, n, 0, 0))] * 2
        + [hbm] * n_sh,
        out_shape=[jax.ShapeDtypeStruct((t, D_MODEL), F32),
                   jax.ShapeDtypeStruct((bsz, N_HEADS, nb, D_HEAD, D_HEAD), F32),
                   jax.ShapeDtypeStruct((bsz, N_HEADS, nb, D_HEAD, D_HEAD), _MXU)]
        + [jax.ShapeDtypeStruct((N_CHIPS,) + a.shape, a.dtype) for a in shards],
        scratch_shapes=[pltpu.VMEM((HEAD_GROUP, D_HEAD, D_HEAD), F32)] + _exchange_scratch(n_sh),
        compiler_params=_params(("arbitrary", "arbitrary", "arbitrary")),
    )(qs, fs, vs, lam, sums, levels, *shards)


def _hgrn_bwd(qs, fs, vs, lam, sst, sym, do, bsz, nb, outgoing):
    t = qs.shape[0]
    sums, levels = _hgrn_tables()
    width = HEAD_GROUP * D_HEAD

    n_out = len(outgoing)

    def body(q_ref, f_ref, v_ref, lam_ref, sst_ref, sym_ref, do_ref, sums_ref, lv_ref, *rest):
        send, (dq_ref, df_ref, dv_ref, dlam_ref), rest = rest[:n_out], rest[n_out:n_out + 4], rest[n_out + 4:]
        landed, dst_scr, gsum_scr, sems = rest[:n_out], rest[n_out], rest[n_out + 1], rest[n_out + 2:]
        n = pl.program_id(2)
        chunk = nb - 1 - n
        first, last = _grid_ends(3)
        start, finish = _chip_exchange(send, landed, sems, slotted=True)
        pl.when(first)(start)

        @pl.when(n == 0)
        def _():
            dst_scr[...] = jnp.zeros_like(dst_scr)
            gsum_scr[...] = jnp.zeros_like(gsum_scr)
            dlam_ref[...] = jnp.zeros_like(dlam_ref)

        lv = lv_ref[...]
        r, c = _iota2((BLOCK, BLOCK), 0), _iota2((BLOCK, BLOCK), 1)
        for hh in range(HEAD_GROUP):
            ls = slice(hh * D_HEAD, (hh + 1) * D_HEAD)
            lam = lam_ref[:, ls]
            q, v, do = q_ref[:, ls], v_ref[:, ls], do_ref[:, ls]
            sg, f, g, k, live = _hgrn_gates(f_ref[:, ls], lam, chunk)
            b, exps = _hgrn_exponents(g, sums_ref[...])
            do16, v16 = do.astype(_MXU), v.astype(_MXU)
            da = lax.dot_general(do16, v16, _NT, preferred_element_type=F32)
            da_sym = jnp.where(c < r, da, da.T)
            dq = jnp.zeros((BLOCK, D_HEAD), F32)
            dqk = jnp.zeros((BLOCK, D_HEAD), F32)
            db_q = jnp.zeros((BLOCK, D_HEAD), F32)
            db_qk = jnp.zeros((BLOCK, D_HEAD), F32)
            for li, m in enumerate(LEVELS):
                is_q, decay, x16 = _level_operand(q, k, exps[li], m)
                y = jnp.dot(jnp.where(lv == li, da_sym, 0.0).astype(_MXU), x16, preferred_element_type=F32)
                dx = y * decay
                dq = dq + jnp.where(is_q, dx, 0.0)
                dqk = dqk + dx
                p = x16.astype(F32) * y
                db_q = db_q + jnp.where(is_q, p, 0.0)
                db_qk = db_qk + p
            dk = dqk - dq
            db = 2.0 * db_q - db_qk
            a_t = jnp.where(c >= r, sym_ref[0, hh, 0], 0)
            st, dst = sst_ref[0, hh, 0], dst_scr[hh]
            eb = jnp.exp(b)
            b_end = b[BLOCK - 1:BLOCK, :]
            dec = jnp.exp(b_end - b)
            qh16, kt16 = (q * eb).astype(_MXU), (k * dec).astype(_MXU)
            dq_st = _mm_s(do16, st)
            dk_st = _mm_s(v16, dst)
            d_diag = jnp.sum(do * v, axis=1, keepdims=True)
            dq_ref[:, ls] = (dq + d_diag * k + eb * dq_st).astype(dq_ref.dtype)
            dk = dk + d_diag * q + dec * dk_st
            dv_ref[:, ls] = (jnp.dot(a_t, do16, preferred_element_type=F32) + _mm_nt_s(kt16, dst)).astype(dv_ref.dtype)
            dst_scr[hh] = dst * jnp.exp(b_end) + lax.dot_general(do16, qh16, _TN, preferred_element_type=F32)
            db = db + (qh16.astype(F32) * dq_st - kt16.astype(F32) * dk_st)
            dg = _mm01_left((c >= r).astype(_MXU), db) + gsum_scr[:, ls]
            gsum_scr[:, ls] = gsum_scr[:, ls] + jnp.sum(db, axis=0, keepdims=True)
            slope = (1.0 - lam) * sg * (1.0 - sg)
            df_ref[:, ls] = jnp.where(live, dg * slope / f - dk * slope, 0.0).astype(df_ref.dtype)
            dl = jnp.where(live, (dg / f - dk) * (1.0 - sg), 0.0)
            dlam_ref[0, :, ls] = dlam_ref[0, :, ls] + jnp.sum(dl, axis=0, keepdims=True)
        pl.when(last)(finish)

    blk = pl.BlockSpec((BLOCK, width), lambda b, h, n: (b * nb + nb - 1 - n, h))
    hbm = pl.BlockSpec(memory_space=pl.ANY)
    return pl.pallas_call(
        body, name="hgrn_bwd", grid=(bsz, N_HEADS // HEAD_GROUP, nb),
        in_specs=[blk, blk, blk, pl.BlockSpec((1, width), lambda b, h, n: (0, h)),
                  pl.BlockSpec((1, HEAD_GROUP, 1, D_HEAD, D_HEAD), lambda b, h, n: (b, h, nb - 1 - n, 0, 0)),
                  pl.BlockSpec((1, HEAD_GROUP, 1, D_HEAD, D_HEAD), lambda b, h, n: (b, h, nb - 1 - n, 0, 0)),
                  blk, pl.BlockSpec(sums.shape, lambda b, h, n: (0, 0)), pl.BlockSpec(levels.shape, lambda b, h, n: (0, 0))]
        + [hbm] * n_out,
        out_specs=[blk, blk, blk, pl.BlockSpec((1, 1, width), lambda b, h, n: (b, 0, h))] + [hbm] * n_out,
        out_shape=[jax.ShapeDtypeStruct((t, D_MODEL), _MXU)] * 3 + [jax.ShapeDtypeStruct((bsz, 1, D_MODEL), F32)]
        + [jax.ShapeDtypeStruct(a.shape, a.dtype) for a in outgoing],
        scratch_shapes=[pltpu.VMEM((HEAD_GROUP, D_HEAD, D_HEAD), F32), pltpu.VMEM((1, width), F32)]
        + _exchange_scratch(n_out),
        compiler_params=_params(("arbitrary", "arbitrary", "arbitrary")),
    )(qs, fs, vs, lam, sst, sym, do, sums, levels, *outgoing)


def _sb_valid(ahead, col, i, j):
    return (ahead < (i - j) * TILE) & (col >= N_PAD - j * TILE)


def _sb_logits(q16, k_blk, valid):
    z = jnp.where(valid, lax.dot_general(q16, k_blk.astype(_MXU), _NT, preferred_element_type=F32) * SB_SCALE, MASKED)
    softplus = jnp.where(z > SOFTPLUS_LINEAR, z, jnp.log(1.0 + jnp.exp(jnp.minimum(z, SOFTPLUS_LINEAR))))
    return -softplus, z - softplus


def _sb_fwd(qs, ks, vs, bsz, nq):
    t = qs.shape[0]
    lp = nq * TILE
    width = HEAD_GROUP * D_HEAD
    groups = N_HEADS // HEAD_GROUP
    lanes = [slice(hh * D_HEAD, (hh + 1) * D_HEAD) for hh in range(HEAD_GROUP)]

    def body(q_ref, k_ref, v_ref, o_ref, c_ref, n_ref):
        b, h, i = pl.program_id(0), pl.program_id(1), pl.program_id(2)
        q16 = [q_ref[:, ls].astype(_MXU) for ls in lanes]
        r, c = _iota2((TILE, TILE), 0), _iota2((TILE, TILE), 1)
        after = (r > c).astype(_MXU)

        def more(carry):
            jj, _, _, top = carry
            return (jj <= i) & (top > UNDERFLOW)

        def step(carry):
            jj, accs, sums, _ = carry
            j = i - jj
            ks_ = pl.ds(pl.multiple_of(j * TILE, TILE), TILE)
            valid = _sb_valid(c - r, c, i, j)
            new_accs, new_sums = [], []
            for hh, ls in enumerate(lanes):
                keep, log_beta = _sb_logits(q16[hh], k_ref[ks_, ls], valid)
                after_s = _mm01_right(keep, after)
                a = jnp.exp(log_beta + (sums[hh] + after_s))
                new_accs.append(accs[hh] + _mm(a, v_ref[ks_, ls]))
                new_sums.append(sums[hh] + (after_s[:, 0:1] + keep[:, 0:1]))
            top = functools.reduce(jnp.maximum, [jnp.max(x) for x in new_sums])
            return jj + 1, tuple(new_accs), tuple(new_sums), top

        init = (jnp.int32(0), tuple(jnp.zeros((TILE, D_HEAD), F32) for _ in lanes),
                tuple(jnp.zeros((TILE, 1), F32) for _ in lanes), jnp.float32(0.0))
        visited, accs, sums, _ = lax.while_loop(more, step, init)
        for hh, ls in enumerate(lanes):
            o_ref[:, ls] = accs[hh]
            c_ref[:, ls] = jnp.broadcast_to(sums[hh], (TILE, D_HEAD))
        n_ref[(b * groups + h) * nq + i] = visited.astype(F32)

    blk = pl.BlockSpec((TILE, width), lambda b, h, i: (b * nq + i, h))
    seq = pl.BlockSpec((lp, width), lambda b, h, i: (b, h))
    return pl.pallas_call(
        body, name="sb_fwd", grid=(bsz, groups, nq),
        in_specs=[blk, seq, seq], out_specs=[blk, blk, pl.BlockSpec(memory_space=pltpu.SMEM)],
        out_shape=[jax.ShapeDtypeStruct((t, D_MODEL), F32)] * 2 + [jax.ShapeDtypeStruct((bsz * groups * nq,), F32)],
        compiler_params=_params(("arbitrary", "arbitrary", "arbitrary")),
    )(qs, ks, vs)


def _sb_bwd(qs, ks, vs, ctot, visited, do, bsz, nq):
    t = qs.shape[0]
    lp = nq * TILE
    width = HEAD_GROUP * D_HEAD
    groups = N_HEADS // HEAD_GROUP
    lanes = [slice(hh * D_HEAD, (hh + 1) * D_HEAD) for hh in range(HEAD_GROUP)]

    def body(n_ref, q_ref, k_ref, v_ref, c_ref, do_ref, dq_ref, dk_ref, dv_ref, dk_acc, dv_acc):
        b, h, i = pl.program_id(0), pl.program_id(1), pl.program_id(2)

        @pl.when(i == 0)
        def _():
            dk_acc[...] = jnp.zeros_like(dk_acc)
            dv_acc[...] = jnp.zeros_like(dv_acc)

        q16 = [q_ref[:, ls].astype(_MXU) for ls in lanes]
        do16 = [do_ref[:, ls].astype(_MXU) for ls in lanes]
        q16_t = [q_ref[:, ls].astype(F32).T.astype(_MXU) for ls in lanes]
        do16_t = [do_ref[:, ls].astype(F32).T.astype(_MXU) for ls in lanes]
        totals = [c_ref[:, hh * D_HEAD:hh * D_HEAD + 1] for hh in range(HEAD_GROUP)]
        r, c = _iota2((TILE, TILE), 0), _iota2((TILE, TILE), 1)
        upto = (r <= c).astype(_MXU)
        before = (r < c).astype(_MXU)
        first = jnp.maximum(i + 1 - n_ref[(b * groups + h) * nq + i].astype(jnp.int32), 0)

        def step(j, carry):
            ks_ = pl.ds(pl.multiple_of(j * TILE, TILE), TILE)
            valid = _sb_valid(c - r, c, i, j)
            out = []
            for hh, ls in enumerate(lanes):
                dq, keep_pre, g_pre = carry[hh]
                k_blk, v_blk = k_ref[ks_, ls], v_ref[ks_, ls]
                keep, log_beta = _sb_logits(q16[hh], k_blk, valid)
                keep_upto = _mm01_right(keep, upto)
                a = jnp.exp(log_beta + (totals[hh] - keep_pre - keep_upto))
                da = lax.dot_general(do16[hh], v_blk.astype(_MXU), _NT, preferred_element_type=F32)
                g = a * da
                g_inside = _mm01_right(g, before)
                g_before = g_pre + g_inside
                beta = jnp.exp(log_beta)
                dz16 = (g * (1.0 - beta) - beta * g_before).astype(_MXU)
                dq = dq + jnp.dot(dz16, k_blk.astype(_MXU), preferred_element_type=F32)
                dk_acc[ls, ks_] += SB_SCALE * jnp.dot(q16_t[hh], dz16, preferred_element_type=F32)
                dv_acc[ls, ks_] += jnp.dot(do16_t[hh], a.astype(_MXU), preferred_element_type=F32)
                out.append((dq, keep_pre + keep_upto[:, TILE - 1:TILE],
                            g_pre + (g_inside[:, TILE - 1:TILE] + g[:, TILE - 1:TILE])))
            return tuple(out)

        zero_col = jnp.zeros((TILE, 1), F32)
        init = tuple((jnp.zeros((TILE, D_HEAD), F32), zero_col, zero_col) for _ in lanes)
        res = lax.fori_loop(first, i + 1, step, init)
        for hh, ls in enumerate(lanes):
            dq_ref[:, ls] = (SB_SCALE * res[hh][0]).astype(dq_ref.dtype)

        @pl.when(i == nq - 1)
        def _():
            dk_ref[...] = dk_acc[...].T.astype(dk_ref.dtype)
            dv_ref[...] = dv_acc[...].T.astype(dv_ref.dtype)

    blk = pl.BlockSpec((TILE, width), lambda b, h, i: (b * nq + i, h))
    seq = pl.BlockSpec((lp, width), lambda b, h, i: (b, h))
    return pl.pallas_call(
        body, name="sb_bwd", grid=(bsz, groups, nq),
        in_specs=[pl.BlockSpec(memory_space=pltpu.SMEM), blk, seq, seq, blk, blk], out_specs=[blk, seq, seq],
        out_shape=[jax.ShapeDtypeStruct((t, D_MODEL), _MXU)] * 3,
        scratch_shapes=[pltpu.VMEM((width, lp), F32)] * 2,
        compiler_params=_params(("arbitrary", "arbitrary", "arbitrary")),
    )(visited, qs, ks, vs, ctot, do)


def _head_norm(o, head_gain):
    outs, rs = [], []
    for h in range(N_HEADS):
        oh = o[:, h * D_HEAD:(h + 1) * D_HEAD]
        r = lax.rsqrt(jnp.mean(oh * oh, axis=-1, keepdims=True) + EPS)
        outs.append(oh * r)
        rs.append(r)
    return outs, rs


def _mix(o, gate, head_gain):
    if head_gain is None:
        on = o
    else:
        outs, _ = _head_norm(o, head_gain)
        on = jnp.concatenate([x * head_gain for x in outs], axis=1)
    return on, on * (gate * _sigmoid(gate))


def _out_fwd(o, gate, h_in, w_out, post_gain, head_gain, name):
    t = o.shape[0]
    tm = _row_tile(t, 256)

    def body(o_ref, g_ref, h_ref, w_ref, pg_ref, hg_ref, ho_ref, u_ref):
        _, mix = _mix(o_ref[...], g_ref[...], hg_ref[...])
        u = jnp.dot(mix.astype(_MXU), w_ref[...], preferred_element_type=F32)
        u_ref[...] = u
        r = lax.rsqrt(jnp.mean(u * u, axis=-1, keepdims=True) + EPS)
        ho_ref[...] = h_ref[...] + u * r * pg_ref[...]

    row = pl.BlockSpec((tm, D_MODEL), lambda i: (i, 0))
    vec = pl.BlockSpec((1, D_MODEL), lambda i: (0, 0))
    return pl.pallas_call(
        body, name=name, grid=(t // tm,),
        in_specs=[row, row, row, pl.BlockSpec((D_MODEL, D_MODEL), lambda i: (0, 0)), vec,
                  pl.BlockSpec((1, D_HEAD), lambda i: (0, 0))],
        out_specs=[row, row], out_shape=[jax.ShapeDtypeStruct((t, D_MODEL), F32)] * 2,
        compiler_params=_params(("arbitrary",)),
    )(o, gate, h_in, w_out, post_gain, head_gain)


def _out_fwd_loss(o, gate, h_in, w_out, post_gain, target, nq, name):
    t = o.shape[0]

    def body(o_ref, g_ref, h_ref, w_ref, pg_ref, t_ref, dh_ref, u_ref, l_ref):
        i = pl.program_id(0)

        @pl.when(i == 0)
        def _():
            l_ref[...] = jnp.zeros_like(l_ref)

        _, mix = _mix(o_ref[...], g_ref[...], None)
        u = jnp.dot(mix.astype(_MXU), w_ref[...], preferred_element_type=F32)
        u_ref[...] = u

        @pl.when(i % nq == 0)
        def _():
            dh_ref[...] = jnp.zeros_like(dh_ref)

        @pl.when(i % nq != 0)
        def _():
            r = lax.rsqrt(jnp.mean(u * u, axis=-1, keepdims=True) + EPS)
            e = h_ref[...] + u * r * pg_ref[...] - t_ref[...]
            dh_ref[...] = e * (1.0 / D_MODEL)
            l_ref[...] += jnp.sum(e * e) * (0.5 / D_MODEL)

    row = pl.BlockSpec((TILE, D_MODEL), lambda i: (i, 0))
    vec = pl.BlockSpec((1, D_MODEL), lambda i: (0, 0))
    return pl.pallas_call(
        body, name=name, grid=(t // TILE,),
        in_specs=[row, row, row, pl.BlockSpec((D_MODEL, D_MODEL), lambda i: (0, 0)), vec,
                  pl.BlockSpec((TILE, D_MODEL), lambda i: ((i // nq) * (nq - 1) + jnp.maximum(i % nq - 1, 0), 0))],
        out_specs=[row, row, pl.BlockSpec((8, 128), lambda i: (0, 0))],
        out_shape=[jax.ShapeDtypeStruct((t, D_MODEL), F32)] * 2 + [jax.ShapeDtypeStruct((8, 128), F32)],
        compiler_params=_params(("arbitrary",)),
    )(o, gate, h_in, w_out, post_gain, target)


def _out_bwd(dh, u, o, gate, w_out, post_gain, head_gain, name, narrow_do):
    t = o.shape[0]
    tm = _row_tile(t, 2 * TILE)
    has_head = head_gain is not None

    def body(*refs):
        if has_head:
            dh_ref, u_ref, o_ref, g_ref, w_ref, pg_ref, hg_ref, do_ref, dg_ref, gw_ref, gw16_ref, gp_ref, gh_ref = refs
            hg = hg_ref[...]
        else:
            dh_ref, u_ref, o_ref, g_ref, w_ref, pg_ref, do_ref, dg_ref, gw_ref, gw16_ref, gp_ref = refs
            hg = None
        first = pl.program_id(0) == 0

        @pl.when(first)
        def _():
            gw_ref[...] = jnp.zeros_like(gw_ref)
            gp_ref[...] = jnp.zeros_like(gp_ref)
            if has_head:
                gh_ref[...] = jnp.zeros_like(gh_ref)

        dr, u, o, gate = dh_ref[...], u_ref[...], o_ref[...], g_ref[...]
        r = lax.rsqrt(jnp.mean(u * u, axis=-1, keepdims=True) + EPS)
        un = u * r
        gp_ref[...] += jnp.sum(dr * un, axis=0, keepdims=True)
        dun = dr * pg_ref[...]
        du = r * (dun - un * jnp.mean(dun * un, axis=-1, keepdims=True))
        on, mix = _mix(o, gate, hg)
        du16 = du.astype(_MXU)
        gw_ref[...] += lax.dot_general(mix.astype(_MXU), du16, _TN, preferred_element_type=F32)
        dmix = lax.dot_general(du16, w_ref[...], _NT, preferred_element_type=F32)
        sg = _sigmoid(gate)
        dg_ref[...] = (dmix * on * (sg * (1.0 + gate * (1.0 - sg)))).astype(dg_ref.dtype)
        don = dmix * (gate * sg)
        if has_head:
            outs, rs = _head_norm(o, hg)
            gh = jnp.zeros((1, D_HEAD), F32)
            cols = []
            for h in range(N_HEADS):
                dn = don[:, h * D_HEAD:(h + 1) * D_HEAD]
                gh = gh + jnp.sum(dn * outs[h], axis=0, keepdims=True)
                dnn = dn * hg
                cols.append(rs[h] * (dnn - outs[h] * jnp.mean(dnn * outs[h], axis=-1, keepdims=True)))
            gh_ref[...] += gh
            do_ref[...] = jnp.concatenate(cols, axis=1)
        else:
            do_ref[...] = don.astype(do_ref.dtype)

        @pl.when(pl.program_id(0) == pl.num_programs(0) - 1)
        def _():
            gw16_ref[...] = gw_ref[...].astype(_MXU)

    row = pl.BlockSpec((tm, D_MODEL), lambda i: (i, 0))
    vec = pl.BlockSpec((1, D_MODEL), lambda i: (0, 0))
    mat = pl.BlockSpec((D_MODEL, D_MODEL), lambda i: (0, 0))
    in_specs = [row, row, row, row, mat, vec]
    args = [dh, u, o, gate, w_out, post_gain]
    out_specs = [row, row, mat, mat, vec]
    out_shape = [jax.ShapeDtypeStruct((t, D_MODEL), _MXU if narrow_do else F32),
                 jax.ShapeDtypeStruct((t, D_MODEL), _MXU)] + [jax.ShapeDtypeStruct((D_MODEL, D_MODEL), F32),
                                                                  jax.ShapeDtypeStruct((D_MODEL, D_MODEL), _MXU),
                                                                  jax.ShapeDtypeStruct((1, D_MODEL), F32)]
    if has_head:
        in_specs.append(pl.BlockSpec((1, D_HEAD), lambda i: (0, 0)))
        args.append(head_gain)
        out_specs.append(pl.BlockSpec((1, D_HEAD), lambda i: (0, 0)))
        out_shape.append(jax.ShapeDtypeStruct((1, D_HEAD), F32))
    return pl.pallas_call(
        body, name=name, grid=(t // tm,), in_specs=in_specs, out_specs=out_specs, out_shape=out_shape,
        compiler_params=_params(("arbitrary",)),
    )(*args)


def _proj_bwd(ds, w4, h_in, gain, dh_out, name, outgoing=(), nq=None):
    t = h_in.shape[0]
    tm = _row_tile(t, 256) if nq is None else TILE
    n_out = len(outgoing)
    n_dh = 1 if nq is None else 2

    def body(d0, d1, d2, d3, w_ref, h_ref, g_ref, dho_ref, *rest):
        send, dh_refs, gg_ref, rest = rest[:n_out], rest[n_out:n_out + n_dh], rest[n_out + n_dh], rest[n_out + n_dh + 1:]
        landed, sems = rest[:n_out], rest[n_out:]
        if n_out:
            first, last = _grid_ends(1)
            start, finish = _chip_exchange(send, landed, sems, slotted=True)
            pl.when(first)(start)

        @pl.when(pl.program_id(0) == 0)
        def _():
            gg_ref[...] = jnp.zeros_like(gg_ref)

        dy = jnp.zeros((tm, D_MODEL), F32)
        for j, d in enumerate((d0, d1, d2, d3)):
            dy = dy + lax.dot_general(d[...].astype(_MXU), w_ref[j], _NT, preferred_element_type=F32)
        x = h_ref[...]
        r = lax.rsqrt(jnp.mean(x * x, axis=-1, keepdims=True) + EPS)
        xn = x * r
        gg_ref[...] += jnp.sum(dy * xn, axis=0, keepdims=True)
        dxn = dy * g_ref[...]
        dh = dho_ref[...] + r * (dxn - xn * jnp.mean(dxn * xn, axis=-1, keepdims=True))
        if nq is None:
            dh_refs[0][...] = dh
        else:
            in_front = pl.program_id(0) % nq == 0

            @pl.when(in_front)
            def _():
                dh_refs[1][...] = dh

            @pl.when(jnp.logical_not(in_front))
            def _():
                dh_refs[0][...] = dh
        if n_out:
            pl.when(last)(finish)

    row = pl.BlockSpec((tm, D_MODEL), lambda i: (i, 0))
    vec = pl.BlockSpec((1, D_MODEL), lambda i: (0, 0))
    hbm = pl.BlockSpec(memory_space=pl.ANY)
    if nq is None:
        dh_specs, dh_shapes = [row], [jax.ShapeDtypeStruct((t, D_MODEL), F32)]
    else:
        dh_specs = [pl.BlockSpec((TILE, D_MODEL), lambda i: ((i // nq) * (nq - 1) + jnp.maximum(i % nq - 1, 0), 0)),
                    pl.BlockSpec((TILE, D_MODEL), lambda i: (i // nq, 0))]
        dh_shapes = [jax.ShapeDtypeStruct((t // nq * (nq - 1), D_MODEL), F32),
                     jax.ShapeDtypeStruct((t // nq, D_MODEL), F32)]
    return pl.pallas_call(
        body, name=name, grid=(t // tm,),
        in_specs=[row] * 4 + [pl.BlockSpec((4, D_MODEL, D_MODEL), lambda i: (0, 0, 0)), row, vec, row] + [hbm] * n_out,
        out_specs=dh_specs + [vec] + [hbm] * n_out,
        out_shape=dh_shapes + [jax.ShapeDtypeStruct((1, D_MODEL), F32)]
        + [jax.ShapeDtypeStruct(a.shape, a.dtype) for a in outgoing],
        scratch_shapes=_exchange_scratch(n_out) if n_out else [],
        compiler_params=_params(("arbitrary",)),
    )(*ds, w4, h_in, gain, dh_out, *outgoing)


def _weight_grad(y, d, name):
    t = y.shape[0]
    tk = _row_tile(t, t // 4)

    def body(y_ref, d_ref, g_ref, g16_ref):
        @pl.when(pl.program_id(0) == 0)
        def _():
            g_ref[...] = jnp.zeros_like(g_ref)

        g_ref[...] += lax.dot_general(y_ref[...], d_ref[...].astype(_MXU), _TN, preferred_element_type=F32)

        @pl.when(pl.program_id(0) == pl.num_programs(0) - 1)
        def _():
            g16_ref[...] = g_ref[...].astype(_MXU)

    row = pl.BlockSpec((tk, D_MODEL), lambda i: (i, 0))
    mat = pl.BlockSpec((D_MODEL, D_MODEL), lambda i: (0, 0))
    return pl.pallas_call(
        body, name=name, grid=(t // tk,), in_specs=[row, row], out_specs=[mat, mat],
        out_shape=[jax.ShapeDtypeStruct((D_MODEL, D_MODEL), F32), jax.ShapeDtypeStruct((D_MODEL, D_MODEL), _MXU)],
        compiler_params=_params(("arbitrary",)),
    )(y, d)


def _local_step(h0, target, pre_norm, post_norm, lam, head_gain, hw_in, shards):
    bsz, seq, _ = target.shape
    nq = seq // TILE + 1
    nb = nq * (TILE // BLOCK)
    lp = nq * TILE
    t = bsz * lp
    d4 = D_MODEL // N_CHIPS
    h0 = h0.reshape(t, D_MODEL)
    pre0, pre1, post0, post1 = pre_norm[0:1], pre_norm[1:2], post_norm[0:1], post_norm[1:2]

    y0, q0, f0, v0, g0 = _norm_proj(h0, pre0, hw_in, "norm_proj_hgrn", (False,) * 4)
    o0, sst, sym, sw_in, sw_out, hw_out = _hgrn_fwd(q0, f0, v0, lam, bsz, nb, shards)
    sw_out, hw_out = sw_out.reshape(D_MODEL, D_MODEL), hw_out.reshape(D_MODEL, D_MODEL)
    h1, u0 = _out_fwd(o0, g0, h0, hw_out, post0, head_gain, "out_fwd_hgrn")
    y1, q1, k1, v1, g1 = _norm_proj(h1, pre1, sw_in, "norm_proj_sb", (True, True, True, False))
    o1, ctot, visited = _sb_fwd(q1, k1, v1, bsz, nq)
    dh2, u1, loss_blk = _out_fwd_loss(o1, g1, h1, sw_out, post1, target.reshape(bsz * seq, D_MODEL), nq, "out_fwd_sb")

    do1, dg1, g_sw_out, g_sw_out16, g_post1 = _out_bwd(dh2, u1, o1, g1, sw_out, post1, None, "out_bwd_sb", True)
    dq1, dk1, dv1 = _sb_bwd(q1, k1, v1, ctot, visited, do1, bsz, nq)
    ds1 = (dq1, dk1, dv1, dg1)
    dh1, g_pre1 = _proj_bwd(ds1, sw_in, h1, pre1, dh2, "proj_bwd_sb")
    g_sw_in = [_weight_grad(y1, d, "wgrad_sb_%d" % j) for j, d in enumerate(ds1)]

    do0, dg0, g_hw_out, g_hw_out16, g_post0, g_head = _out_bwd(dh1, u0, o0, g0, hw_out, post0, head_gain, "out_bwd_hgrn",
                                                               False)
    ready = (jnp.stack([g16 for _, g16 in g_sw_in]), g_sw_out16.reshape(N_CHIPS, d4, D_MODEL),
             g_hw_out16.reshape(N_CHIPS, d4, D_MODEL))
    dq0, df0, dv0, dlam, land_sw_in, land_sw_out, land_hw_out = _hgrn_bwd(q0, f0, v0, lam, sst, sym, do0, bsz, nb, ready)
    ds0 = (dq0, df0, dv0, dg0)
    g_hw_in = [_weight_grad(y0, d, "wgrad_hgrn_%d" % j) for j, d in enumerate(ds0)]
    last = (jnp.stack([g16 for _, g16 in g_hw_in]),)
    grad_x, dh_front, g_pre0, land_hw_in = _proj_bwd(ds0, hw_in, h0, pre0, dh1, "proj_bwd_hgrn", last, nq)

    grad_x = grad_x.reshape(bsz, seq, D_MODEL)
    g_meta = jnp.sum(dh_front.reshape(bsz, TILE, D_MODEL)[:, N_PAD:, :], axis=0)
    g_lam = jnp.sum(dlam, axis=0)
    small = jnp.concatenate([g_pre0, g_pre1, g_post0, g_post1, g_lam, g_lam,
                             jnp.pad(g_head, ((0, 0), (0, D_MODEL - D_HEAD))), g_meta,
                             jnp.pad(loss_blk[0:1], ((0, 0), (0, D_MODEL - loss_blk.shape[1])))], axis=0)
    rows4 = lambda g: [g[j * d4:(j + 1) * d4] for j in range(N_CHIPS)]
    large = dict(hw_in=(land_hw_in, [g for g, _ in g_hw_in]), sw_in=(land_sw_in, [g for g, _ in g_sw_in]),
                 hw_out=(land_hw_out, rows4(g_hw_out)), sw_out=(land_sw_out, rows4(g_sw_out)))
    return grad_x, small, large


def _prep_weights(hw_in, sw_in, hw_out, sw_out, meta, x):
    bsz, seq, _ = x.shape
    d4 = D_MODEL // N_CHIPS

    def body(hi_ref, si_ref, ho_ref, so_ref, m_ref, x_ref, ghi, gm, si16, so16, ho16, h0_ref,
             far_send, far_recv, near_send, near_recv, zeros, x_sems, pad_sems, meta_sems):
        zeros[...] = jnp.zeros_like(zeros)
        rows_in = [pltpu.make_async_copy(x_ref.at[b], h0_ref.at[b, pl.ds(TILE, seq), :], x_sems.at[b]) for b in range(bsz)]
        rows_in += [pltpu.make_async_copy(zeros, h0_ref.at[b, pl.ds(0, N_PAD), :], pad_sems.at[b]) for b in range(bsz)]
        for cp in rows_in:
            cp.start()
        x, y, c = _place()
        me = 2 * x + y
        ghi[me] = hi_ref[0].astype(_MXU)
        gm[me] = m_ref[...]
        si16[...] = si_ref[0].astype(_MXU)
        so16[...] = so_ref[0].astype(_MXU)
        ho16[...] = ho_ref[0].astype(_MXU)
        outs = (ghi, gm)
        n = len(outs)
        peers = [(1 - x, y), (x, 1 - y), (1 - x, 1 - y)]

        def half(a, slot, which):
            rows = outs[a].shape[1] // 2
            return outs[a].at[slot, pl.ds(which * rows, rows), :]

        def far(r, a, slot):
            px, py = peers[r]
            return pltpu.make_async_remote_copy(
                src_ref=half(a, slot, c), dst_ref=half(a, slot, c), send_sem=far_send.at[r * n + a],
                recv_sem=far_recv.at[r * n + a], device_id=(px, py, c), device_id_type=MESH)

        def near(r, a, which):
            px, py = peers[r]
            return pltpu.make_async_remote_copy(
                src_ref=half(a, 2 * px + py, which), dst_ref=half(a, 2 * px + py, which),
                send_sem=near_send.at[r * n + a], recv_sem=near_recv.at[r * n + a],
                device_id=(x, y, 1 - c), device_id_type=MESH)

        for r in range(3):
            for a in range(n):
                far(r, a, me).start()
        for r, (px, py) in enumerate(peers):
            for a in range(n):
                far(r, a, 2 * px + py).wait_recv()
                near(r, a, c).start()
        for r in range(3):
            for a in range(n):
                near(r, a, 1 - c).wait_recv()
        for r in range(3):
            for a in range(n):
                far(r, a, me).wait_send()
                near(r, a, c).wait_send()
        meta_in = [pltpu.make_async_copy(gm.at[j], h0_ref.at[b, pl.ds(N_PAD, N_META), pl.ds(j * d4, d4)],
                                         meta_sems.at[b * N_CHIPS + j]) for b in range(bsz) for j in range(N_CHIPS)]
        for cp in meta_in:
            cp.start()
        for cp in rows_in + meta_in:
            cp.wait()

    vm = pl.BlockSpec(memory_space=pltpu.VMEM)
    hbm = pl.BlockSpec(memory_space=pl.ANY)
    return pl.pallas_call(
        body, name="prep_weights",
        in_specs=[vm] * 5 + [hbm], out_specs=[vm] * 5 + [hbm],
        out_shape=[jax.ShapeDtypeStruct((N_CHIPS, D_MODEL, D_MODEL), _MXU), jax.ShapeDtypeStruct((N_CHIPS, N_META, d4), F32),
                   jax.ShapeDtypeStruct((D_MODEL, D_MODEL), _MXU), jax.ShapeDtypeStruct((d4, D_MODEL), _MXU),
                   jax.ShapeDtypeStruct((d4, D_MODEL), _MXU), jax.ShapeDtypeStruct((bsz, TILE + seq, D_MODEL), F32)],
        scratch_shapes=[pltpu.SemaphoreType.DMA((6,))] * 4
        + [pltpu.VMEM((N_PAD, D_MODEL), F32), pltpu.SemaphoreType.DMA((bsz,)), pltpu.SemaphoreType.DMA((bsz,)),
           pltpu.SemaphoreType.DMA((bsz * N_CHIPS,))],
        compiler_params=pltpu.CompilerParams(vmem_limit_bytes=VMEM_LIMIT),
    )(hw_in, sw_in, hw_out, sw_out, meta, x)


def _scatter_small(small):
    def body(sm, lsm, send_sems, recv_sems, local_sem):
        x, y, c = _place()
        mine = 4 * x + 2 * y + c
        local = pltpu.make_async_copy(sm, lsm.at[mine], local_sem)
        local.start()

        def copy(rel, src_dev, to):
            return pltpu.make_async_remote_copy(
                src_ref=sm, dst_ref=lsm.at[src_dev], send_sem=send_sems.at[rel - 1], recv_sem=recv_sems.at[rel - 1],
                device_id=to, device_id_type=MESH)

        flip = lambda bit, v: 1 - v if bit else v
        rels = [(rel, flip(rel & 4, x), flip(rel & 2, y), flip(rel & 1, c)) for rel in range(1, N_DEV)]
        sends = [copy(rel, mine, (px, py, pc)) for rel, px, py, pc in rels]
        for cp in sends:
            cp.start()
        for rel, px, py, pc in rels:
            copy(rel, 4 * px + 2 * py + pc, (px, py, pc)).wait_recv()
        for cp in sends:
            cp.wait_send()
        local.wait()

    hbm = pl.BlockSpec(memory_space=pl.ANY)
    return pl.pallas_call(
        body, name="scatter_small", in_specs=[hbm], out_specs=hbm,
        out_shape=jax.ShapeDtypeStruct((N_DEV, SMALL_ROWS, D_MODEL), F32),
        scratch_shapes=[pltpu.SemaphoreType.DMA((N_DEV - 1,)), pltpu.SemaphoreType.DMA((N_DEV - 1,)),
                        pltpu.SemaphoreType.DMA(())],
    )(small)


def _sum_slots(landed, own, me, name):
    n, rows, _ = landed.shape
    tm = rows if rows < 256 else 256

    def body(me_ref, l_ref, o0, o1, o2, o3, out_ref):
        acc = None
        for k, o in enumerate((o0, o1, o2, o3)):
            term = jnp.where(me_ref[0] == k, o[...], l_ref[k].astype(F32))
            acc = term if acc is None else acc + term
        out_ref[...] = acc

    blk = pl.BlockSpec((tm, D_MODEL), lambda i: (i, 0))
    return pl.pallas_call(
        body, name=name, grid=(rows // tm,),
        in_specs=[pl.BlockSpec(memory_space=pltpu.SMEM), pl.BlockSpec((n, tm, D_MODEL), lambda i: (0, i, 0))] + [blk] * 4,
        out_specs=blk, out_shape=jax.ShapeDtypeStruct((rows, D_MODEL), F32),
        compiler_params=_params(("arbitrary",)),
    )(me, landed, *own)


def _swap_with_sibling(parts):
    def body(a0, a1, a2, a3, b0, b1, b2, b3, send_sems, recv_sems):
        x, y, c = _place()
        copies = [pltpu.make_async_remote_copy(src_ref=s, dst_ref=d, send_sem=send_sems.at[a], recv_sem=recv_sems.at[a],
                                               device_id=(x, y, 1 - c), device_id_type=MESH)
                  for a, (s, d) in enumerate(zip((a0, a1, a2, a3), (b0, b1, b2, b3)))]
        for cp in copies:
            cp.start()
        for cp in copies:
            cp.wait()

    hbm = pl.BlockSpec(memory_space=pl.ANY)
    return pl.pallas_call(
        body, name="swap_with_sibling", in_specs=[hbm] * 4, out_specs=[hbm] * 4,
        out_shape=[jax.ShapeDtypeStruct(p.shape, F32) for p in parts],
        scratch_shapes=[pltpu.SemaphoreType.DMA((4,)), pltpu.SemaphoreType.DMA((4,))],
    )(*parts)


def _adamw_math(w, g, m, v):
    m = ADAM_B1 * m + (1.0 - ADAM_B1) * g
    v = ADAM_B2 * v + (1.0 - ADAM_B2) * (g * g)
    m_hat = m / (1.0 - ADAM_B1 ** ADAM_STEP)
    v_hat = v / (1.0 - ADAM_B2 ** ADAM_STEP)
    delta = -ADAM_LR * (m_hat / (jnp.sqrt(v_hat) + ADAM_EPS) + ADAM_WD * w)
    return delta, m, v


def _adamw(w, g_parts, m, v, name):
    rows, cols = w.shape
    tm = rows if rows < 256 else 256
    n = len(g_parts)

    def body(*refs):
        w_ref, m_ref, v_ref = refs[n:n + 3]
        g_ref, d_ref, nm_ref, nv_ref = refs[n + 3:]
        g = refs[0][...]
        for p in refs[1:n]:
            g = g + p[...]
        g_ref[...] = g
        d_ref[...], nm_ref[...], nv_ref[...] = _adamw_math(w_ref[...], g, m_ref[...], v_ref[...])

    blk = pl.BlockSpec((tm, cols), lambda i: (i, 0))
    return pl.pallas_call(
        body, name=name, grid=(rows // tm,), in_specs=[blk] * (n + 3), out_specs=[blk] * 4,
        out_shape=[jax.ShapeDtypeStruct((rows, cols), F32)] * 4,
        compiler_params=_params(("arbitrary",)),
    )(*g_parts, w, m, v)


def _lam_of(hgrn_lb):
    def body(lb_ref, o_ref):
        lb = lb_ref[...]
        e = jnp.exp(lb - jnp.max(lb, axis=0, keepdims=True))
        o_ref[...] = e[0:1, :] / jnp.sum(e, axis=0, keepdims=True)

    return pl.pallas_call(body, name="lam_of", out_shape=jax.ShapeDtypeStruct((1, D_MODEL), F32))(hgrn_lb)


def _small_grads(land_small, lam):
    def body(l_ref, lam_ref, o_ref):
        acc = l_ref[0]
        for k in range(1, N_DEV):
            acc = acc + l_ref[k]
        p = lam_ref[...]
        slope = p * (1.0 - p)
        row = _iota2((SMALL_ROWS, D_MODEL), 0)
        o_ref[...] = acc * jnp.where(row == 4, slope, jnp.where(row == 5, -slope, 1.0))

    return pl.pallas_call(body, name="small_grads",
                          out_shape=jax.ShapeDtypeStruct((SMALL_ROWS, D_MODEL), F32))(land_small, lam)


def kernel(x, meta_tokens, pre_norm, post_norm, hgrn_w_in, hgrn_lb, hgrn_out_norm, hgrn_w_out, sb_w_in, sb_w_out, loss_target, m_meta_tokens, m_pre_norm, m_post_norm, m_hgrn_w_in, m_hgrn_lb, m_hgrn_out_norm, m_hgrn_w_out, m_sb_w_in, m_sb_w_out, v_meta_tokens, v_pre_norm, v_post_norm, v_hgrn_w_in, v_hgrn_lb, v_hgrn_out_norm, v_hgrn_w_out, v_sb_w_in, v_sb_w_out):
    d4 = D_MODEL // N_CHIPS
    chip = 2 * lax.axis_index("x") + lax.axis_index("y")
    hw_in, _, sw_in16, sw_out16, hw_out16, h0 = _prep_weights(hgrn_w_in, sb_w_in, hgrn_w_out, sb_w_out, meta_tokens, x)
    lam = _lam_of(hgrn_lb)
    grad_x, small, large = _local_step(
        h0, loss_target, pre_norm, post_norm, lam, hgrn_out_norm,
        hw_in, (sw_in16, sw_out16, hw_out16))

    me = jnp.reshape(chip, (1,)).astype(jnp.int32)
    parts = [_sum_slots(*large[n], me, "sum_" + n) for n in ("hw_in", "sw_in", "hw_out", "sw_out")]
    sib = _swap_with_sibling(parts)
    small = _small_grads(_scatter_small(small), lam)
    loss = small[SMALL_ROWS - 1, 0]

    res = {}
    res["hgrn_w_in"] = _adamw(hgrn_w_in[0], [parts[0], sib[0]], m_hgrn_w_in[0], v_hgrn_w_in[0], "adamw_hw_in")
    res["sb_w_in"] = _adamw(sb_w_in[0], [parts[1], sib[1]], m_sb_w_in[0], v_sb_w_in[0], "adamw_sw_in")
    res["hgrn_w_out"] = _adamw(hgrn_w_out[0], [parts[2], sib[2]], m_hgrn_w_out[0], v_hgrn_w_out[0], "adamw_hw_out")
    res["sb_w_out"] = _adamw(sb_w_out[0], [parts[3], sib[3]], m_sb_w_out[0], v_sb_w_out[0], "adamw_sw_out")
    res["pre_norm"] = _adamw(pre_norm, [small[0:2]], m_pre_norm, v_pre_norm, "adamw_pre")
    res["post_norm"] = _adamw(post_norm, [small[2:4]], m_post_norm, v_post_norm, "adamw_post")
    res["hgrn_lb"] = _adamw(hgrn_lb, [small[4:6]], m_hgrn_lb, v_hgrn_lb, "adamw_lb")
    res["hgrn_out_norm"] = _adamw(hgrn_out_norm, [small[6:7, :D_HEAD]], m_hgrn_out_norm, v_hgrn_out_norm, "adamw_head")
    g_meta = lax.dynamic_slice_in_dim(small[7:7 + N_META], chip * d4, d4, axis=1)
    res["meta_tokens"] = _adamw(meta_tokens, [g_meta], m_meta_tokens, v_meta_tokens, "adamw_meta")
    for n in ("hgrn_w_in", "hgrn_w_out", "sb_w_in", "sb_w_out"):
        res[n] = tuple(a[None] for a in res[n])
    order = ("meta_tokens", "pre_norm", "post_norm", "hgrn_w_in", "hgrn_lb", "hgrn_out_norm", "hgrn_w_out",
             "sb_w_in", "sb_w_out")
    return (loss, grad_x, *[res[n][0] for n in order], *[res[n][1] for n in order],
            *[res[n][2] for n in order], *[res[n][3] for n in order])
```

```python
import functools

import jax
import numpy as np
import jax.numpy as jnp
from jax import lax
from jax.experimental import pallas as pl
from jax.experimental.pallas import tpu as pltpu

F32 = jnp.float32
_MXU = jnp.bfloat16

D_MODEL = 1024
N_HEADS = 8
D_HEAD = 128
BLOCK = 128
N_META = 16
TILE = 256
N_PAD = TILE - N_META
UNDERFLOW = -105.0
EPS = 1e-6
SB_SCALE = D_HEAD ** -0.5
SOFTPLUS_LINEAR = 20.0
MASKED = -1e30
ADAM_LR, ADAM_B1, ADAM_B2, ADAM_EPS, ADAM_WD, ADAM_STEP = 0.001, 0.9, 0.999, 1e-08, 0.01, 10
N_CHIPS = 4
N_DEV = 8
SMALL_ROWS = 24
VMEM_LIMIT = 56 * 1024 * 1024
MESH = pl.DeviceIdType.MESH

_NT = (((1,), (1,)), ((), ()))
_TN = (((0,), (0,)), ((), ()))


def _mm(a, b):
    return jnp.dot(a.astype(_MXU), b.astype(_MXU), preferred_element_type=F32)


def _mm_nt(a, b):
    return lax.dot_general(a.astype(_MXU), b.astype(_MXU), _NT, preferred_element_type=F32)


def _mm_tn(a, b):
    return lax.dot_general(a.astype(_MXU), b.astype(_MXU), _TN, preferred_element_type=F32)


def _split2(x):
    hi = x.astype(_MXU)
    return hi, (x - hi.astype(F32)).astype(_MXU)


def _mm_s(a16, state):
    hi, lo = _split2(state)
    return jnp.dot(a16, hi, preferred_element_type=F32) + jnp.dot(a16, lo, preferred_element_type=F32)


def _mm_nt_s(a16, state):
    hi, lo = _split2(state)
    return (lax.dot_general(a16, hi, _NT, preferred_element_type=F32)
            + lax.dot_general(a16, lo, _NT, preferred_element_type=F32))


def _mm01_right(x, m01):
    return jnp.dot(x.astype(_MXU), m01, preferred_element_type=F32)


def _mm01_left(m01, x):
    hi, lo = _split2(x)
    return jnp.dot(m01, hi, preferred_element_type=F32) + jnp.dot(m01, lo, preferred_element_type=F32)


def _iota2(shape, dim):
    return lax.broadcasted_iota(jnp.int32, shape, dim)


def _row_tile(total, pref):
    t = pref
    while total % t:
        t -= BLOCK
    return t


def _params(sem, limit=VMEM_LIMIT):
    return pltpu.CompilerParams(dimension_semantics=sem, vmem_limit_bytes=limit)


def _sigmoid(x):
    return 1.0 / (1.0 + jnp.exp(-x))


def _grid_ends(ndim):
    first, last = True, True
    for d in range(ndim):
        first = first & (pl.program_id(d) == 0)
        last = last & (pl.program_id(d) == pl.num_programs(d) - 1)
    return first, last


def _place():
    return lax.axis_index("x"), lax.axis_index("y"), lax.axis_index("c")


def _exchange_scratch(n):
    return [pltpu.SemaphoreType.DMA((3 * n,)), pltpu.SemaphoreType.DMA((3 * n,)), pltpu.SemaphoreType.DMA((n,))]


def _chip_exchange(srcs, dsts, sems, slotted):
    send_sems, recv_sems, local_sems = sems
    x, y, c = _place()
    me = 2 * x + y
    peers = [(1 - x, y), (x, 1 - y), (1 - x, 1 - y)]
    n = len(dsts)

    def remote(r, a, sending):
        px, py = peers[r]
        p = 2 * px + py
        return pltpu.make_async_remote_copy(
            src_ref=srcs[a].at[p] if slotted else srcs[a], dst_ref=dsts[a].at[me if sending else p],
            send_sem=send_sems.at[r * n + a], recv_sem=recv_sems.at[r * n + a],
            device_id=(px, py, c), device_id_type=MESH)

    def local(a):
        return pltpu.make_async_copy(srcs[a].at[me] if slotted else srcs[a], dsts[a].at[me], local_sems.at[a])

    def start():
        for a in range(n):
            local(a).start()
        for r in range(3):
            for a in range(n):
                remote(r, a, True).start()

    def finish():
        for r in range(3):
            for a in range(n):
                remote(r, a, False).wait_recv()
        for r in range(3):
            for a in range(n):
                remote(r, a, True).wait_send()
        for a in range(n):
            local(a).wait()

    return start, finish


def _norm_proj(h, gain, w4, name, narrow):
    t = h.shape[0]
    tm = _row_tile(t, 2 * TILE)

    def body(h_ref, g_ref, w_ref, y_ref, s0, s1, s2, s3):
        x = h_ref[...]
        r = lax.rsqrt(jnp.mean(x * x, axis=-1, keepdims=True) + EPS)
        y = (x * r * g_ref[...]).astype(_MXU)
        y_ref[...] = y
        for j, s in enumerate((s0, s1, s2, s3)):
            s[...] = jnp.dot(y, w_ref[j], preferred_element_type=F32).astype(s.dtype)

    row = pl.BlockSpec((tm, D_MODEL), lambda i: (i, 0))
    return pl.pallas_call(
        body, name=name, grid=(t // tm,),
        in_specs=[row, pl.BlockSpec((1, D_MODEL), lambda i: (0, 0)),
                  pl.BlockSpec((4, D_MODEL, D_MODEL), lambda i: (0, 0, 0))],
        out_specs=[row] * 5,
        out_shape=[jax.ShapeDtypeStruct((t, D_MODEL), _MXU)]
        + [jax.ShapeDtypeStruct((t, D_MODEL), _MXU if n else F32) for n in narrow],
        compiler_params=_params(("arbitrary",)),
    )(h, gain, w4)


LEVELS = (64, 32, 16, 8, 4, 2, 1)
HEAD_GROUP = 2


def _hgrn_tables():
    r = np.arange(BLOCK)
    mats = [r[None, :] <= r[:, None]]
    x = r[:, None] ^ r[None, :]
    lv = np.full((BLOCK, BLOCK), len(LEVELS), np.int32)
    for i, m in enumerate(LEVELS):
        lv[(x >= m) & (x < 2 * m)] = i
    return jnp.asarray(np.concatenate(mats, 0).astype(np.float32), dtype=_MXU), jnp.asarray(lv)


def _hgrn_exponents(g, sums):
    b = _mm01_left(sums, g)
    row = _iota2((BLOCK, D_HEAD), 0)
    out = []
    for m in LEVELS:
        is_q = (row & m) != 0
        if m >= 4:
            grp = b.reshape(BLOCK // (2 * m), 2 * m, D_HEAD)
            ref = jnp.broadcast_to(grp[:, m - 1:m, :], grp.shape).reshape(BLOCK, D_HEAD)
            d = b - ref
            out.append(jnp.where(is_q, d, -d))
        elif m == 2:
            below, above = pltpu.roll(g, 1, axis=0), pltpu.roll(g, BLOCK - 1, axis=0)
            low = row & 3
            out.append(jnp.where(low == 3, g + below, jnp.where(low == 2, g, jnp.where(low == 0, above, 0.0))))
        else:
            out.append(jnp.where(is_q, g, 0.0))
    return b, out


def _hgrn_gates(fz, lam, chunk):
    pos = chunk * BLOCK + _iota2((BLOCK, D_HEAD), 0)
    live = pos >= N_PAD
    sg = _sigmoid(fz)
    f = lam + (1.0 - lam) * sg
    g = jnp.where(live, jnp.log(f), 0.0)
    k = jnp.where(live, (1.0 - lam) * (1.0 - sg), 0.0)
    return sg, f, g, k, live


def _level_operand(q, k, exponent, m):
    decay = jnp.exp(exponent)
    is_q = (_iota2((BLOCK, D_HEAD), 0) & m) != 0
    return is_q, decay, (jnp.where(is_q, q, k) * decay).astype(_MXU)


def _hgrn_fwd(qs, fs, vs, lam, bsz, nb, shards):
    t = qs.shape[0]
    sums, levels = _hgrn_tables()
    width = HEAD_GROUP * D_HEAD

    n_sh = len(shards)

    def body(q_ref, f_ref, v_ref, lam_ref, sums_ref, lv_ref, *rest):
        own, (o_ref, sst_ref, sym_ref), rest = rest[:n_sh], rest[n_sh:n_sh + 3], rest[n_sh + 3:]
        gathered, st_scr, sems = rest[:n_sh], rest[n_sh], rest[n_sh + 1:]
        n = pl.program_id(2)
        first, last = _grid_ends(3)
        start, finish = _chip_exchange(own, gathered, sems, slotted=False)
        pl.when(first)(start)

        @pl.when(n == 0)
        def _():
            st_scr[...] = jnp.zeros_like(st_scr)

        lv = lv_ref[...]
        r, c = _iota2((BLOCK, BLOCK), 0), _iota2((BLOCK, BLOCK), 1)
        for hh in range(HEAD_GROUP):
            ls = slice(hh * D_HEAD, (hh + 1) * D_HEAD)
            st = st_scr[hh]
            sst_ref[0, hh, 0] = st
            q, v = q_ref[:, ls], v_ref[:, ls]
            _, _, g, k, _ = _hgrn_gates(f_ref[:, ls], lam_ref[:, ls], n)
            b, exps = _hgrn_exponents(g, sums_ref[...])
            sym = jnp.zeros((BLOCK, BLOCK), F32)
            for li, m in enumerate(LEVELS):
                _, _, x16 = _level_operand(q, k, exps[li], m)
                sym = jnp.where(lv == li, lax.dot_general(x16, x16, _NT, preferred_element_type=F32), sym)
            sym = jnp.where(c == r, jnp.sum(q * k, axis=1, keepdims=True), sym).astype(_MXU)
            sym_ref[0, hh, 0] = sym
            o_ref[:, ls] = _mm_nt(q * jnp.exp(b), st) + _mm(jnp.where(c <= r, sym, 0), v)
            b_end = b[BLOCK - 1:BLOCK, :]
            st_scr[hh] = st * jnp.exp(b_end) + _mm_tn(v, k * jnp.exp(b_end - b))
        pl.when(last)(finish)

    blk = pl.BlockSpec((BLOCK, width), lambda b, h, n: (b * nb + n, h))
    hbm = pl.BlockSpec(memory_space=pl.ANY)
    return pl.pallas_call(
        body, name="hgrn_fwd", grid=(bsz, N_HEADS // HEAD_GROUP, nb),
        in_specs=[blk, blk, blk, pl.BlockSpec((1, width), lambda b, h, n: (0, h)),
                  pl.BlockSpec(sums.shape, lambda b, h, n: (0, 0)), pl.BlockSpec(levels.shape, lambda b, h, n: (0, 0))]
        + [hbm] * n_sh,
        out_specs=[blk] + [pl.BlockSpec((1, HEAD_GROUP, 1, D_HEAD, D_HEAD), lambda b, h, n: (b, h, n, 0, 0))] * 2
        + [hbm] * n_sh,
        out_shape=[jax.ShapeDtypeStruct((t, D_MODEL), F32),
                   jax.ShapeDtypeStruct((bsz, N_HEADS, nb, D_HEAD, D_HEAD), F32),
                   jax.ShapeDtypeStruct((bsz, N_HEADS, nb, D_HEAD, D_HEAD), _MXU)]
        + [jax.ShapeDtypeStruct((N_CHIPS,) + a.shape, a.dtype) for a in shards],
        scratch_shapes=[pltpu.VMEM((HEAD_GROUP, D_HEAD, D_HEAD), F32)] + _exchange_scratch(n_sh),
        compiler_params=_params(("arbitrary", "arbitrary", "arbitrary")),
    )(qs, fs, vs, lam, sums, levels, *shards)


def _hgrn_bwd(qs, fs, vs, lam, sst, sym, do, bsz, nb, outgoing):
    t = qs.shape[0]
    sums, levels = _hgrn_tables()
    width = HEAD_GROUP * D_HEAD

    n_out = len(outgoing)

    def body(q_ref, f_ref, v_ref, lam_ref, sst_ref, sym_ref, do_ref, sums_ref, lv_ref, *rest):
        send, (dq_ref, df_ref, dv_ref, dlam_ref), rest = rest[:n_out], rest[n_out:n_out + 4], rest[n_out + 4:]
        landed, dst_scr, gsum_scr, sems = rest[:n_out], rest[n_out], rest[n_out + 1], rest[n_out + 2:]
        n = pl.program_id(2)
        chunk = nb - 1 - n
        first, last = _grid_ends(3)
        start, finish = _chip_exchange(send, landed, sems, slotted=True)
        pl.when(first)(start)

        @pl.when(n == 0)
        def _():
            dst_scr[...] = jnp.zeros_like(dst_scr)
            gsum_scr[...] = jnp.zeros_like(gsum_scr)
            dlam_ref[...] = jnp.zeros_like(dlam_ref)

        lv = lv_ref[...]
        r, c = _iota2((BLOCK, BLOCK), 0), _iota2((BLOCK, BLOCK), 1)
        for hh in range(HEAD_GROUP):
            ls = slice(hh * D_HEAD, (hh + 1) * D_HEAD)
            lam = lam_ref[:, ls]
            q, v, do = q_ref[:, ls], v_ref[:, ls], do_ref[:, ls]
            sg, f, g, k, live = _hgrn_gates(f_ref[:, ls], lam, chunk)
            b, exps = _hgrn_exponents(g, sums_ref[...])
            do16, v16 = do.astype(_MXU), v.astype(_MXU)
            da = lax.dot_general(do16, v16, _NT, preferred_element_type=F32)
            da_sym = jnp.where(c < r, da, da.T)
            dq = jnp.zeros((BLOCK, D_HEAD), F32)
            dqk = jnp.zeros((BLOCK, D_HEAD), F32)
            db_q = jnp.zeros((BLOCK, D_HEAD), F32)
            db_qk = jnp.zeros((BLOCK, D_HEAD), F32)
            for li, m in enumerate(LEVELS):
                is_q, decay, x16 = _level_operand(q, k, exps[li], m)
                y = jnp.dot(jnp.where(lv == li, da_sym, 0.0).astype(_MXU), x16, preferred_element_type=F32)
                dx = y * decay
                dq = dq + jnp.where(is_q, dx, 0.0)
                dqk = dqk + dx
                p = x16.astype(F32) * y
                db_q = db_q + jnp.where(is_q, p, 0.0)
                db_qk = db_qk + p
            dk = dqk - dq
            db = 2.0 * db_q - db_qk
            a_t = jnp.where(c >= r, sym_ref[0, hh, 0], 0)
            st, dst = sst_ref[0, hh, 0], dst_scr[hh]
            eb = jnp.exp(b)
            b_end = b[BLOCK - 1:BLOCK, :]
            dec = jnp.exp(b_end - b)
            qh16, kt16 = (q * eb).astype(_MXU), (k * dec).astype(_MXU)
            dq_st = _mm_s(do16, st)
            dk_st = _mm_s(v16, dst)
            d_diag = jnp.sum(do * v, axis=1, keepdims=True)
            dq_ref[:, ls] = (dq + d_diag * k + eb * dq_st).astype(dq_ref.dtype)
            dk = dk + d_diag * q + dec * dk_st
            dv_ref[:, ls] = (jnp.dot(a_t, do16, preferred_element_type=F32) + _mm_nt_s(kt16, dst)).astype(dv_ref.dtype)
            dst_scr[hh] = dst * jnp.exp(b_end) + lax.dot_general(do16, qh16, _TN, preferred_element_type=F32)
            db = db + (qh16.astype(F32) * dq_st - kt16.astype(F32) * dk_st)
            dg = _mm01_left((c >= r).astype(_MXU), db) + gsum_scr[:, ls]
            gsum_scr[:, ls] = gsum_scr[:, ls] + jnp.sum(db, axis=0, keepdims=True)
            slope = (1.0 - lam) * sg * (1.0 - sg)
            df_ref[:, ls] = jnp.where(live, dg * slope / f - dk * slope, 0.0).astype(df_ref.dtype)
            dl = jnp.where(live, (dg / f - dk) * (1.0 - sg), 0.0)
            dlam_ref[0, :, ls] = dlam_ref[0, :, ls] + jnp.sum(dl, axis=0, keepdims=True)
        pl.when(last)(finish)

    blk = pl.BlockSpec((BLOCK, width), lambda b, h, n: (b * nb + nb - 1 - n, h))
    hbm = pl.BlockSpec(memory_space=pl.ANY)
    return pl.pallas_call(
        body, name="hgrn_bwd", grid=(bsz, N_HEADS // HEAD_GROUP, nb),
        in_specs=[blk, blk, blk, pl.BlockSpec((1, width), lambda b, h, n: (0, h)),
                  pl.BlockSpec((1, HEAD_GROUP, 1, D_HEAD, D_HEAD), lambda b, h, n: (b, h, nb - 1 - n, 0, 0)),
                  pl.BlockSpec((1, HEAD_GROUP, 1, D_HEAD, D_HEAD), lambda b, h, n: (b, h, nb - 1 - n, 0, 0)),
                  blk, pl.BlockSpec(sums.shape, lambda b, h, n: (0, 0)), pl.BlockSpec(levels.shape, lambda b, h, n: (0, 0))]
        + [hbm] * n_out,
        out_specs=[blk, blk, blk, pl.BlockSpec((1, 1, width), lambda b, h, n: (b, 0, h))] + [hbm] * n_out,
        out_shape=[jax.ShapeDtypeStruct((t, D_MODEL), _MXU)] * 3 + [jax.ShapeDtypeStruct((bsz, 1, D_MODEL), F32)]
        + [jax.ShapeDtypeStruct(a.shape, a.dtype) for a in outgoing],
        scratch_shapes=[pltpu.VMEM((HEAD_GROUP, D_HEAD, D_HEAD), F32), pltpu.VMEM((1, width), F32)]
        + _exchange_scratch(n_out),
        compiler_params=_params(("arbitrary", "arbitrary", "arbitrary")),
    )(qs, fs, vs, lam, sst, sym, do, sums, levels, *outgoing)


def _sb_valid(ahead, col, i, j):
    return (ahead < (i - j) * TILE) & (col >= N_PAD - j * TILE)


def _sb_logits(q16, k_blk, valid):
    z = jnp.where(valid, lax.dot_general(q16, k_blk.astype(_MXU), _NT, preferred_element_type=F32) * SB_SCALE, MASKED)
    softplus = jnp.where(z > SOFTPLUS_LINEAR, z, jnp.log(1.0 + jnp.exp(jnp.minimum(z, SOFTPLUS_LINEAR))))
    return -softplus, z - softplus


def _sb_fwd(qs, ks, vs, bsz, nq):
    t = qs.shape[0]
    lp = nq * TILE
    width = HEAD_GROUP * D_HEAD
    groups = N_HEADS // HEAD_GROUP
    lanes = [slice(hh * D_HEAD, (hh + 1) * D_HEAD) for hh in range(HEAD_GROUP)]

    def body(q_ref, k_ref, v_ref, o_ref, c_ref, n_ref):
        b, h, i = pl.program_id(0), pl.program_id(1), pl.program_id(2)
        q16 = [q_ref[:, ls].astype(_MXU) for ls in lanes]
        r, c = _iota2((TILE, TILE), 0), _iota2((TILE, TILE), 1)
        after = (r > c).astype(_MXU)

        def more(carry):
            jj, _, _, top = carry
            return (jj <= i) & (top > UNDERFLOW)

        def step(carry):
            jj, accs, sums, _ = carry
            j = i - jj
            ks_ = pl.ds(pl.multiple_of(j * TILE, TILE), TILE)
            valid = _sb_valid(c - r, c, i, j)
            new_accs, new_sums = [], []
            for hh, ls in enumerate(lanes):
                keep, log_beta = _sb_logits(q16[hh], k_ref[ks_, ls], valid)
                after_s = _mm01_right(keep, after)
                a = jnp.exp(log_beta + (sums[hh] + after_s))
                new_accs.append(accs[hh] + _mm(a, v_ref[ks_, ls]))
                new_sums.append(sums[hh] + (after_s[:, 0:1] + keep[:, 0:1]))
            top = functools.reduce(jnp.maximum, [jnp.max(x) for x in new_sums])
            return jj + 1, tuple(new_accs), tuple(new_sums), top

        init = (jnp.int32(0), tuple(jnp.zeros((TILE, D_HEAD), F32) for _ in lanes),
                tuple(jnp.zeros((TILE, 1), F32) for _ in lanes), jnp.float32(0.0))
        visited, accs, sums, _ = lax.while_loop(more, step, init)
        for hh, ls in enumerate(lanes):
            o_ref[:, ls] = accs[hh]
            c_ref[:, ls] = jnp.broadcast_to(sums[hh], (TILE, D_HEAD))
        n_ref[(b * groups + h) * nq + i] = visited.astype(F32)

    blk = pl.BlockSpec((TILE, width), lambda b, h, i: (b * nq + i, h))
    seq = pl.BlockSpec((lp, width), lambda b, h, i: (b, h))
    return pl.pallas_call(
        body, name="sb_fwd", grid=(bsz, groups, nq),
        in_specs=[blk, seq, seq], out_specs=[blk, blk, pl.BlockSpec(memory_space=pltpu.SMEM)],
        out_shape=[jax.ShapeDtypeStruct((t, D_MODEL), F32)] * 2 + [jax.ShapeDtypeStruct((bsz * groups * nq,), F32)],
        compiler_params=_params(("arbitrary", "arbitrary", "arbitrary")),
    )(qs, ks, vs)


def _sb_bwd(qs, ks, vs, ctot, visited, do, bsz, nq):
    t = qs.shape[0]
    lp = nq * TILE
    width = HEAD_GROUP * D_HEAD
    groups = N_HEADS // HEAD_GROUP
    lanes = [slice(hh * D_HEAD, (hh + 1) * D_HEAD) for hh in range(HEAD_GROUP)]

    def body(n_ref, q_ref, k_ref, v_ref, c_ref, do_ref, dq_ref, dk_ref, dv_ref, dk_acc, dv_acc):
        b, h, i = pl.program_id(0), pl.program_id(1), pl.program_id(2)

        @pl.when(i == 0)
        def _():
            dk_acc[...] = jnp.zeros_like(dk_acc)
            dv_acc[...] = jnp.zeros_like(dv_acc)

        q16 = [q_ref[:, ls].astype(_MXU) for ls in lanes]
        do16 = [do_ref[:, ls].astype(_MXU) for ls in lanes]
        q16_t = [q_ref[:, ls].astype(F32).T.astype(_MXU) for ls in lanes]
        do16_t = [do_ref[:, ls].astype(F32).T.astype(_MXU) for ls in lanes]
        totals = [c_ref[:, hh * D_HEAD:hh * D_HEAD + 1] for hh in range(HEAD_GROUP)]
        r, c = _iota2((TILE, TILE), 0), _iota2((TILE, TILE), 1)
        upto = (r <= c).astype(_MXU)
        before = (r < c).astype(_MXU)
        first = jnp.maximum(i + 1 - n_ref[(b * groups + h) * nq + i].astype(jnp.int32), 0)

        def step(j, carry):
            ks_ = pl.ds(pl.multiple_of(j * TILE, TILE), TILE)
            valid = _sb_valid(c - r, c, i, j)
            out = []
            for hh, ls in enumerate(lanes):
                dq, keep_pre, g_pre = carry[hh]
                k_blk, v_blk = k_ref[ks_, ls], v_ref[ks_, ls]
                keep, log_beta = _sb_logits(q16[hh], k_blk, valid)
                keep_upto = _mm01_right(keep, upto)
                a = jnp.exp(log_beta + (totals[hh] - keep_pre - keep_upto))
                da = lax.dot_general(do16[hh], v_blk.astype(_MXU), _NT, preferred_element_type=F32)
                g = a * da
                g_inside = _mm01_right(g, before)
                g_before = g_pre + g_inside
                beta = jnp.exp(log_beta)
                dz16 = (g * (1.0 - beta) - beta * g_before).astype(_MXU)
                dq = dq + jnp.dot(dz16, k_blk.astype(_MXU), preferred_element_type=F32)
                dk_acc[ls, ks_] += SB_SCALE * jnp.dot(q16_t[hh], dz16, preferred_element_type=F32)
                dv_acc[ls, ks_] += jnp.dot(do16_t[hh], a.astype(_MXU), preferred_element_type=F32)
                out.append((dq, keep_pre + keep_upto[:, TILE - 1:TILE],
                            g_pre + (g_inside[:, TILE - 1:TILE] + g[:, TILE - 1:TILE])))
            return tuple(out)

        zero_col = jnp.zeros((TILE, 1), F32)
        init = tuple((jnp.zeros((TILE, D_HEAD), F32), zero_col, zero_col) for _ in lanes)
        res = lax.fori_loop(first, i + 1, step, init)
        for hh, ls in enumerate(lanes):
            dq_ref[:, ls] = (SB_SCALE * res[hh][0]).astype(dq_ref.dtype)

        @pl.when(i == nq - 1)
        def _():
            dk_ref[...] = dk_acc[...].T.astype(dk_ref.dtype)
            dv_ref[...] = dv_acc[...].T.astype(dv_ref.dtype)

    blk = pl.BlockSpec((TILE, width), lambda b, h, i: (b * nq + i, h))
    seq = pl.BlockSpec((lp, width), lambda b, h, i: (b, h))
    return pl.pallas_call(
        body, name="sb_bwd", grid=(bsz, groups, nq),
        in_specs=[pl.BlockSpec(memory_space=pltpu.SMEM), blk, seq, seq, blk, blk], out_specs=[blk, seq, seq],
        out_shape=[jax.ShapeDtypeStruct((t, D_MODEL), _MXU)] * 3,
        scratch_shapes=[pltpu.VMEM((width, lp), F32)] * 2,
        compiler_params=_params(("arbitrary", "arbitrary", "arbitrary")),
    )(visited, qs, ks, vs, ctot, do)


def _head_norm(o, head_gain):
    outs, rs = [], []
    for h in range(N_HEADS):
        oh = o[:, h * D_HEAD:(h + 1) * D_HEAD]
        r = lax.rsqrt(jnp.mean(oh * oh, axis=-1, keepdims=True) + EPS)
        outs.append(oh * r)
        rs.append(r)
    return outs, rs


def _mix(o, gate, head_gain):
    if head_gain is None:
        on = o
    else:
        outs, _ = _head_norm(o, head_gain)
        on = jnp.concatenate([x * head_gain for x in outs], axis=1)
    return on, on * (gate * _sigmoid(gate))


def _out_fwd(o, gate, h_in, w_out, post_gain, head_gain, name):
    t = o.shape[0]
    tm = _row_tile(t, 2 * TILE)

    def body(o_ref, g_ref, h_ref, w_ref, pg_ref, hg_ref, ho_ref, u_ref):
        _, mix = _mix(o_ref[...], g_ref[...], hg_ref[...])
        u = jnp.dot(mix.astype(_MXU), w_ref[...], preferred_element_type=F32)
        u_ref[...] = u
        r = lax.rsqrt(jnp.mean(u * u, axis=-1, keepdims=True) + EPS)
        ho_ref[...] = h_ref[...] + u * r * pg_ref[...]

    row = pl.BlockSpec((tm, D_MODEL), lambda i: (i, 0))
    vec = pl.BlockSpec((1, D_MODEL), lambda i: (0, 0))
    return pl.pallas_call(
        body, name=name, grid=(t // tm,),
        in_specs=[row, row, row, pl.BlockSpec((D_MODEL, D_MODEL), lambda i: (0, 0)), vec,
                  pl.BlockSpec((1, D_HEAD), lambda i: (0, 0))],
        out_specs=[row, row], out_shape=[jax.ShapeDtypeStruct((t, D_MODEL), F32)] * 2,
        compiler_params=_params(("arbitrary",)),
    )(o, gate, h_in, w_out, post_gain, head_gain)


def _out_fwd_loss(o, gate, h_in, w_out, post_gain, target, nq, name):
    t = o.shape[0]

    def body(o_ref, g_ref, h_ref, w_ref, pg_ref, t_ref, dh_ref, u_ref, l_ref):
        i = pl.program_id(0)

        @pl.when(i == 0)
        def _():
            l_ref[...] = jnp.zeros_like(l_ref)

        _, mix = _mix(o_ref[...], g_ref[...], None)
        u = jnp.dot(mix.astype(_MXU), w_ref[...], preferred_element_type=F32)
        u_ref[...] = u

        @pl.when(i % nq == 0)
        def _():
            dh_ref[...] = jnp.zeros_like(dh_ref)

        @pl.when(i % nq != 0)
        def _():
            r = lax.rsqrt(jnp.mean(u * u, axis=-1, keepdims=True) + EPS)
            e = h_ref[...] + u * r * pg_ref[...] - t_ref[...]
            dh_ref[...] = e * (1.0 / D_MODEL)
            l_ref[...] += jnp.sum(e * e) * (0.5 / D_MODEL)

    row = pl.BlockSpec((TILE, D_MODEL), lambda i: (i, 0))
    vec = pl.BlockSpec((1, D_MODEL), lambda i: (0, 0))
    return pl.pallas_call(
        body, name=name, grid=(t // TILE,),
        in_specs=[row, row, row, pl.BlockSpec((D_MODEL, D_MODEL), lambda i: (0, 0)), vec,
                  pl.BlockSpec((TILE, D_MODEL), lambda i: ((i // nq) * (nq - 1) + jnp.maximum(i % nq - 1, 0), 0))],
        out_specs=[row, row, pl.BlockSpec((8, 128), lambda i: (0, 0))],
        out_shape=[jax.ShapeDtypeStruct((t, D_MODEL), F32)] * 2 + [jax.ShapeDtypeStruct((8, 128), F32)],
        compiler_params=_params(("arbitrary",)),
    )(o, gate, h_in, w_out, post_gain, target)


def _out_bwd(dh, u, o, gate, w_out, post_gain, head_gain, name, narrow_do):
    t = o.shape[0]
    tm = _row_tile(t, 2 * TILE)
    has_head = head_gain is not None

    def body(*refs):
        if has_head:
            dh_ref, u_ref, o_ref, g_ref, w_ref, pg_ref, hg_ref, do_ref, dg_ref, gw_ref, gw16_ref, gp_ref, gh_ref = refs
            hg = hg_ref[...]
        else:
            dh_ref, u_ref, o_ref, g_ref, w_ref, pg_ref, do_ref, dg_ref, gw_ref, gw16_ref, gp_ref = refs
            hg = None
        first = pl.program_id(0) == 0

        @pl.when(first)
        def _():
            gw_ref[...] = jnp.zeros_like(gw_ref)
            gp_ref[...] = jnp.zeros_like(gp_ref)
            if has_head:
                gh_ref[...] = jnp.zeros_like(gh_ref)

        dr, u, o, gate = dh_ref[...], u_ref[...], o_ref[...], g_ref[...]
        r = lax.rsqrt(jnp.mean(u * u, axis=-1, keepdims=True) + EPS)
        un = u * r
        gp_ref[...] += jnp.sum(dr * un, axis=0, keepdims=True)
        dun = dr * pg_ref[...]
        du = r * (dun - un * jnp.mean(dun * un, axis=-1, keepdims=True))
        on, mix = _mix(o, gate, hg)
        du16 = du.astype(_MXU)
        gw_ref[...] += lax.dot_general(mix.astype(_MXU), du16, _TN, preferred_element_type=F32)
        dmix = lax.dot_general(du16, w_ref[...], _NT, preferred_element_type=F32)
        sg = _sigmoid(gate)
        dg_ref[...] = (dmix * on * (sg * (1.0 + gate * (1.0 - sg)))).astype(dg_ref.dtype)
        don = dmix * (gate * sg)
        if has_head:
            outs, rs = _head_norm(o, hg)
            gh = jnp.zeros((1, D_HEAD), F32)
            cols = []
            for h in range(N_HEADS):
                dn = don[:, h * D_HEAD:(h + 1) * D_HEAD]
                gh = gh + jnp.sum(dn * outs[h], axis=0, keepdims=True)
                dnn = dn * hg
                cols.append(rs[h] * (dnn - outs[h] * jnp.mean(dnn * outs[h], axis=-1, keepdims=True)))
            gh_ref[...] += gh
            do_ref[...] = jnp.concatenate(cols, axis=1)
        else:
            do_ref[...] = don.astype(do_ref.dtype)

        @pl.when(pl.program_id(0) == pl.num_programs(0) - 1)
        def _():
            gw16_ref[...] = gw_ref[...].astype(_MXU)

    row = pl.BlockSpec((tm, D_MODEL), lambda i: (i, 0))
    vec = pl.BlockSpec((1, D_MODEL), lambda i: (0, 0))
    mat = pl.BlockSpec((D_MODEL, D_MODEL), lambda i: (0, 0))
    in_specs = [row, row, row, row, mat, vec]
    args = [dh, u, o, gate, w_out, post_gain]
    out_specs = [row, row, mat, mat, vec]
    out_shape = [jax.ShapeDtypeStruct((t, D_MODEL), _MXU if narrow_do else F32),
                 jax.ShapeDtypeStruct((t, D_MODEL), _MXU)] + [jax.ShapeDtypeStruct((D_MODEL, D_MODEL), F32),
                                                                  jax.ShapeDtypeStruct((D_MODEL, D_MODEL), _MXU),
                                                                  jax.ShapeDtypeStruct((1, D_MODEL), F32)]
    if has_head:
        in_specs.append(pl.BlockSpec((1, D_HEAD), lambda i: (0, 0)))
        args.append(head_gain)
        out_specs.append(pl.BlockSpec((1, D_HEAD), lambda i: (0, 0)))
        out_shape.append(jax.ShapeDtypeStruct((1, D_HEAD), F32))
    return pl.pallas_call(
        body, name=name, grid=(t // tm,), in_specs=in_specs, out_specs=out_specs, out_shape=out_shape,
        compiler_params=_params(("arbitrary",)),
    )(*args)


def _proj_bwd(ds, w4, h_in, gain, dh_out, name, outgoing=(), nq=None):
    t = h_in.shape[0]
    tm = _row_tile(t, 2 * TILE) if nq is None else TILE
    n_out = len(outgoing)
    n_dh = 1 if nq is None else 2

    def body(d0, d1, d2, d3, w_ref, h_ref, g_ref, dho_ref, *rest):
        send, dh_refs, gg_ref, rest = rest[:n_out], rest[n_out:n_out + n_dh], rest[n_out + n_dh], rest[n_out + n_dh + 1:]
        landed, sems = rest[:n_out], rest[n_out:]
        if n_out:
            first, last = _grid_ends(1)
            start, finish = _chip_exchange(send, landed, sems, slotted=True)
            pl.when(first)(start)

        @pl.when(pl.program_id(0) == 0)
        def _():
            gg_ref[...] = jnp.zeros_like(gg_ref)

        dy = jnp.zeros((tm, D_MODEL), F32)
        for j, d in enumerate((d0, d1, d2, d3)):
            dy = dy + lax.dot_general(d[...].astype(_MXU), w_ref[j], _NT, preferred_element_type=F32)
        x = h_ref[...]
        r = lax.rsqrt(jnp.mean(x * x, axis=-1, keepdims=True) + EPS)
        xn = x * r
        gg_ref[...] += jnp.sum(dy * xn, axis=0, keepdims=True)
        dxn = dy * g_ref[...]
        dh = dho_ref[...] + r * (dxn - xn * jnp.mean(dxn * xn, axis=-1, keepdims=True))
        if nq is None:
            dh_refs[0][...] = dh
        else:
            in_front = pl.program_id(0) % nq == 0

            @pl.when(in_front)
            def _():
                dh_refs[1][...] = dh

            @pl.when(jnp.logical_not(in_front))
            def _():
                dh_refs[0][...] = dh
        if n_out:
            pl.when(last)(finish)

    row = pl.BlockSpec((tm, D_MODEL), lambda i: (i, 0))
    vec = pl.BlockSpec((1, D_MODEL), lambda i: (0, 0))
    hbm = pl.BlockSpec(memory_space=pl.ANY)
    if nq is None:
        dh_specs, dh_shapes = [row], [jax.ShapeDtypeStruct((t, D_MODEL), F32)]
    else:
        dh_specs = [pl.BlockSpec((TILE, D_MODEL), lambda i: ((i // nq) * (nq - 1) + jnp.maximum(i % nq - 1, 0), 0)),
                    pl.BlockSpec((TILE, D_MODEL), lambda i: (i // nq, 0))]
        dh_shapes = [jax.ShapeDtypeStruct((t // nq * (nq - 1), D_MODEL), F32),
                     jax.ShapeDtypeStruct((t // nq, D_MODEL), F32)]
    return pl.pallas_call(
        body, name=name, grid=(t // tm,),
        in_specs=[row] * 4 + [pl.BlockSpec((4, D_MODEL, D_MODEL), lambda i: (0, 0, 0)), row, vec, row] + [hbm] * n_out,
        out_specs=dh_specs + [vec] + [hbm] * n_out,
        out_shape=dh_shapes + [jax.ShapeDtypeStruct((1, D_MODEL), F32)]
        + [jax.ShapeDtypeStruct(a.shape, a.dtype) for a in outgoing],
        scratch_shapes=_exchange_scratch(n_out) if n_out else [],
        compiler_params=_params(("arbitrary",)),
    )(*ds, w4, h_in, gain, dh_out, *outgoing)


def _weight_grad(y, d, name):
    t = y.shape[0]
    tk = _row_tile(t, t // 4)

    def body(y_ref, d_ref, g_ref, g16_ref):
        @pl.when(pl.program_id(0) == 0)
        def _():
            g_ref[...] = jnp.zeros_like(g_ref)

        g_ref[...] += lax.dot_general(y_ref[...], d_ref[...].astype(_MXU), _TN, preferred_element_type=F32)

        @pl.when(pl.program_id(0) == pl.num_programs(0) - 1)
        def _():
            g16_ref[...] = g_ref[...].astype(_MXU)

    row = pl.BlockSpec((tk, D_MODEL), lambda i: (i, 0))
    mat = pl.BlockSpec((D_MODEL, D_MODEL), lambda i: (0, 0))
    return pl.pallas_call(
        body, name=name, grid=(t // tk,), in_specs=[row, row], out_specs=[mat, mat],
        out_shape=[jax.ShapeDtypeStruct((D_MODEL, D_MODEL), F32), jax.ShapeDtypeStruct((D_MODEL, D_MODEL), _MXU)],
        compiler_params=_params(("arbitrary",)),
    )(y, d)


def _local_step(x, target, meta, pre_norm, post_norm, lam, head_gain, hw_in, shards):
    bsz, seq, _ = x.shape
    nq = seq // TILE + 1
    nb = nq * (TILE // BLOCK)
    lp = nq * TILE
    t = bsz * lp
    d4 = D_MODEL // N_CHIPS
    front = jnp.concatenate([jnp.zeros((N_PAD, D_MODEL), F32), meta], axis=0)
    h0 = jnp.concatenate([jnp.broadcast_to(front[None], (bsz, TILE, D_MODEL)), x], axis=1).reshape(t, D_MODEL)
    pre0, pre1, post0, post1 = pre_norm[0:1], pre_norm[1:2], post_norm[0:1], post_norm[1:2]

    y0, q0, f0, v0, g0 = _norm_proj(h0, pre0, hw_in, "norm_proj_hgrn", (False,) * 4)
    o0, sst, sym, sw_in, sw_out, hw_out = _hgrn_fwd(q0, f0, v0, lam, bsz, nb, shards)
    sw_out, hw_out = sw_out.reshape(D_MODEL, D_MODEL), hw_out.reshape(D_MODEL, D_MODEL)
    h1, u0 = _out_fwd(o0, g0, h0, hw_out, post0, head_gain, "out_fwd_hgrn")
    y1, q1, k1, v1, g1 = _norm_proj(h1, pre1, sw_in, "norm_proj_sb", (True, True, True, False))
    o1, ctot, visited = _sb_fwd(q1, k1, v1, bsz, nq)
    dh2, u1, loss_blk = _out_fwd_loss(o1, g1, h1, sw_out, post1, target.reshape(bsz * seq, D_MODEL), nq, "out_fwd_sb")

    do1, dg1, g_sw_out, g_sw_out16, g_post1 = _out_bwd(dh2, u1, o1, g1, sw_out, post1, None, "out_bwd_sb", True)
    dq1, dk1, dv1 = _sb_bwd(q1, k1, v1, ctot, visited, do1, bsz, nq)
    ds1 = (dq1, dk1, dv1, dg1)
    dh1, g_pre1 = _proj_bwd(ds1, sw_in, h1, pre1, dh2, "proj_bwd_sb")
    g_sw_in = [_weight_grad(y1, d, "wgrad_sb_%d" % j) for j, d in enumerate(ds1)]

    do0, dg0, g_hw_out, g_hw_out16, g_post0, g_head = _out_bwd(dh1, u0, o0, g0, hw_out, post0, head_gain, "out_bwd_hgrn",
                                                               False)
    ready = (jnp.stack([g16 for _, g16 in g_sw_in]), g_sw_out16.reshape(N_CHIPS, d4, D_MODEL),
             g_hw_out16.reshape(N_CHIPS, d4, D_MODEL))
    dq0, df0, dv0, dlam, land_sw_in, land_sw_out, land_hw_out = _hgrn_bwd(q0, f0, v0, lam, sst, sym, do0, bsz, nb, ready)
    ds0 = (dq0, df0, dv0, dg0)
    g_hw_in = [_weight_grad(y0, d, "wgrad_hgrn_%d" % j) for j, d in enumerate(ds0)]
    last = (jnp.stack([g16 for _, g16 in g_hw_in]),)
    grad_x, dh_front, g_pre0, land_hw_in = _proj_bwd(ds0, hw_in, h0, pre0, dh1, "proj_bwd_hgrn", last, nq)

    grad_x = grad_x.reshape(bsz, seq, D_MODEL)
    g_meta = jnp.sum(dh_front.reshape(bsz, TILE, D_MODEL)[:, N_PAD:, :], axis=0)
    g_lam = jnp.sum(dlam, axis=0)
    small = jnp.concatenate([g_pre0, g_pre1, g_post0, g_post1, g_lam, g_lam,
                             jnp.pad(g_head, ((0, 0), (0, D_MODEL - D_HEAD))), g_meta,
                             jnp.pad(loss_blk[0:1], ((0, 0), (0, D_MODEL - loss_blk.shape[1])))], axis=0)
    rows4 = lambda g: [g[j * d4:(j + 1) * d4] for j in range(N_CHIPS)]
    large = dict(hw_in=(land_hw_in, [g for g, _ in g_hw_in]), sw_in=(land_sw_in, [g for g, _ in g_sw_in]),
                 hw_out=(land_hw_out, rows4(g_hw_out)), sw_out=(land_sw_out, rows4(g_sw_out)))
    return grad_x, small, large


def _prep_weights(hw_in, sw_in, hw_out, sw_out, meta):
    def body(hi_ref, si_ref, ho_ref, so_ref, m_ref, ghi, gm, si16, so16, ho16, far_send, far_recv, near_send, near_recv):
        x, y, c = _place()
        me = 2 * x + y
        ghi[me] = hi_ref[0].astype(_MXU)
        gm[me] = m_ref[...]
        si16[...] = si_ref[0].astype(_MXU)
        so16[...] = so_ref[0].astype(_MXU)
        ho16[...] = ho_ref[0].astype(_MXU)
        outs = (ghi, gm)
        n = len(outs)
        peers = [(1 - x, y), (x, 1 - y), (1 - x, 1 - y)]

        def half(a, slot, which):
            rows = outs[a].shape[1] // 2
            return outs[a].at[slot, pl.ds(which * rows, rows), :]

        def far(r, a, slot):
            px, py = peers[r]
            return pltpu.make_async_remote_copy(
                src_ref=half(a, slot, c), dst_ref=half(a, slot, c), send_sem=far_send.at[r * n + a],
                recv_sem=far_recv.at[r * n + a], device_id=(px, py, c), device_id_type=MESH)

        def near(r, a, which):
            px, py = peers[r]
            return pltpu.make_async_remote_copy(
                src_ref=half(a, 2 * px + py, which), dst_ref=half(a, 2 * px + py, which),
                send_sem=near_send.at[r * n + a], recv_sem=near_recv.at[r * n + a],
                device_id=(x, y, 1 - c), device_id_type=MESH)

        for r in range(3):
            for a in range(n):
                far(r, a, me).start()
        for r, (px, py) in enumerate(peers):
            for a in range(n):
                far(r, a, 2 * px + py).wait_recv()
                near(r, a, c).start()
        for r in range(3):
            for a in range(n):
                near(r, a, 1 - c).wait_recv()
        for r in range(3):
            for a in range(n):
                far(r, a, me).wait_send()
                near(r, a, c).wait_send()

    d4 = D_MODEL // N_CHIPS
    vm = pl.BlockSpec(memory_space=pltpu.VMEM)
    return pl.pallas_call(
        body, name="prep_weights",
        in_specs=[vm] * 5, out_specs=[vm] * 5,
        out_shape=[jax.ShapeDtypeStruct((N_CHIPS, D_MODEL, D_MODEL), _MXU), jax.ShapeDtypeStruct((N_CHIPS, N_META, d4), F32),
                   jax.ShapeDtypeStruct((D_MODEL, D_MODEL), _MXU), jax.ShapeDtypeStruct((d4, D_MODEL), _MXU),
                   jax.ShapeDtypeStruct((d4, D_MODEL), _MXU)],
        scratch_shapes=[pltpu.SemaphoreType.DMA((6,))] * 4,
        compiler_params=pltpu.CompilerParams(vmem_limit_bytes=VMEM_LIMIT),
    )(hw_in, sw_in, hw_out, sw_out, meta)


def _scatter_small(small):
    def body(sm, lsm, send_sems, recv_sems, local_sem):
        x, y, c = _place()
        mine = 4 * x + 2 * y + c
        local = pltpu.make_async_copy(sm, lsm.at[mine], local_sem)
        local.start()

        def copy(rel, src_dev, to):
            return pltpu.make_async_remote_copy(
                src_ref=sm, dst_ref=lsm.at[src_dev], send_sem=send_sems.at[rel - 1], recv_sem=recv_sems.at[rel - 1],
                device_id=to, device_id_type=MESH)

        flip = lambda bit, v: 1 - v if bit else v
        rels = [(rel, flip(rel & 4, x), flip(rel & 2, y), flip(rel & 1, c)) for rel in range(1, N_DEV)]
        sends = [copy(rel, mine, (px, py, pc)) for rel, px, py, pc in rels]
        for cp in sends:
            cp.start()
        for rel, px, py, pc in rels:
            copy(rel, 4 * px + 2 * py + pc, (px, py, pc)).wait_recv()
        for cp in sends:
            cp.wait_send()
        local.wait()

    hbm = pl.BlockSpec(memory_space=pl.ANY)
    return pl.pallas_call(
        body, name="scatter_small", in_specs=[hbm], out_specs=hbm,
        out_shape=jax.ShapeDtypeStruct((N_DEV, SMALL_ROWS, D_MODEL), F32),
        scratch_shapes=[pltpu.SemaphoreType.DMA((N_DEV - 1,)), pltpu.SemaphoreType.DMA((N_DEV - 1,)),
                        pltpu.SemaphoreType.DMA(())],
    )(small)


def _sum_slots(landed, own, me, name):
    n, rows, _ = landed.shape
    tm = rows if rows < 256 else 256

    def body(me_ref, l_ref, o0, o1, o2, o3, out_ref):
        acc = None
        for k, o in enumerate((o0, o1, o2, o3)):
            term = jnp.where(me_ref[0] == k, o[...], l_ref[k].astype(F32))
            acc = term if acc is None else acc + term
        out_ref[...] = acc

    blk = pl.BlockSpec((tm, D_MODEL), lambda i: (i, 0))
    return pl.pallas_call(
        body, name=name, grid=(rows // tm,),
        in_specs=[pl.BlockSpec(memory_space=pltpu.SMEM), pl.BlockSpec((n, tm, D_MODEL), lambda i: (0, i, 0))] + [blk] * 4,
        out_specs=blk, out_shape=jax.ShapeDtypeStruct((rows, D_MODEL), F32),
        compiler_params=_params(("arbitrary",)),
    )(me, landed, *own)


def _swap_with_sibling(parts):
    def body(a0, a1, a2, a3, b0, b1, b2, b3, send_sems, recv_sems):
        x, y, c = _place()
        copies = [pltpu.make_async_remote_copy(src_ref=s, dst_ref=d, send_sem=send_sems.at[a], recv_sem=recv_sems.at[a],
                                               device_id=(x, y, 1 - c), device_id_type=MESH)
                  for a, (s, d) in enumerate(zip((a0, a1, a2, a3), (b0, b1, b2, b3)))]
        for cp in copies:
            cp.start()
        for cp in copies:
            cp.wait()

    hbm = pl.BlockSpec(memory_space=pl.ANY)
    return pl.pallas_call(
        body, name="swap_with_sibling", in_specs=[hbm] * 4, out_specs=[hbm] * 4,
        out_shape=[jax.ShapeDtypeStruct(p.shape, F32) for p in parts],
        scratch_shapes=[pltpu.SemaphoreType.DMA((4,)), pltpu.SemaphoreType.DMA((4,))],
    )(*parts)


def _adamw_math(w, g, m, v):
    m = ADAM_B1 * m + (1.0 - ADAM_B1) * g
    v = ADAM_B2 * v + (1.0 - ADAM_B2) * (g * g)
    m_hat = m / (1.0 - ADAM_B1 ** ADAM_STEP)
    v_hat = v / (1.0 - ADAM_B2 ** ADAM_STEP)
    delta = -ADAM_LR * (m_hat / (jnp.sqrt(v_hat) + ADAM_EPS) + ADAM_WD * w)
    return delta, m, v


def _adamw(w, g_parts, m, v, name):
    rows, cols = w.shape
    tm = rows if rows < 256 else 256
    n = len(g_parts)

    def body(*refs):
        w_ref, m_ref, v_ref = refs[n:n + 3]
        g_ref, d_ref, nm_ref, nv_ref = refs[n + 3:]
        g = refs[0][...]
        for p in refs[1:n]:
            g = g + p[...]
        g_ref[...] = g
        d_ref[...], nm_ref[...], nv_ref[...] = _adamw_math(w_ref[...], g, m_ref[...], v_ref[...])

    blk = pl.BlockSpec((tm, cols), lambda i: (i, 0))
    return pl.pallas_call(
        body, name=name, grid=(rows // tm,), in_specs=[blk] * (n + 3), out_specs=[blk] * 4,
        out_shape=[jax.ShapeDtypeStruct((rows, cols), F32)] * 4,
        compiler_params=_params(("arbitrary",)),
    )(*g_parts, w, m, v)


def _lam_of(hgrn_lb):
    def body(lb_ref, o_ref):
        lb = lb_ref[...]
        e = jnp.exp(lb - jnp.max(lb, axis=0, keepdims=True))
        o_ref[...] = e[0:1, :] / jnp.sum(e, axis=0, keepdims=True)

    return pl.pallas_call(body, name="lam_of", out_shape=jax.ShapeDtypeStruct((1, D_MODEL), F32))(hgrn_lb)


def _small_grads(land_small, lam):
    def body(l_ref, lam_ref, o_ref):
        acc = l_ref[0]
        for k in range(1, N_DEV):
            acc = acc + l_ref[k]
        p = lam_ref[...]
        slope = p * (1.0 - p)
        row = _iota2((SMALL_ROWS, D_MODEL), 0)
        o_ref[...] = acc * jnp.where(row == 4, slope, jnp.where(row == 5, -slope, 1.0))

    return pl.pallas_call(body, name="small_grads",
                          out_shape=jax.ShapeDtypeStruct((SMALL_ROWS, D_MODEL), F32))(land_small, lam)


def kernel(x, meta_tokens, pre_norm, post_norm, hgrn_w_in, hgrn_lb, hgrn_out_norm, hgrn_w_out, sb_w_in, sb_w_out, loss_target, m_meta_tokens, m_pre_norm, m_post_norm, m_hgrn_w_in, m_hgrn_lb, m_hgrn_out_norm, m_hgrn_w_out, m_sb_w_in, m_sb_w_out, v_meta_tokens, v_pre_norm, v_post_norm, v_hgrn_w_in, v_hgrn_lb, v_hgrn_out_norm, v_hgrn_w_out, v_sb_w_in, v_sb_w_out):
    d4 = D_MODEL // N_CHIPS
    chip = 2 * lax.axis_index("x") + lax.axis_index("y")
    hw_in, meta4, sw_in16, sw_out16, hw_out16 = _prep_weights(hgrn_w_in, sb_w_in, hgrn_w_out, sb_w_out, meta_tokens)
    meta = meta4.transpose(1, 0, 2).reshape(N_META, D_MODEL)
    lam = _lam_of(hgrn_lb)
    grad_x, small, large = _local_step(
        x, loss_target, meta, pre_norm, post_norm, lam, hgrn_out_norm,
        hw_in, (sw_in16, sw_out16, hw_out16))

    me = jnp.reshape(chip, (1,)).astype(jnp.int32)
    parts = [_sum_slots(*large[n], me, "sum_" + n) for n in ("hw_in", "sw_in", "hw_out", "sw_out")]
    sib = _swap_with_sibling(parts)
    small = _small_grads(_scatter_small(small), lam)
    loss = small[SMALL_ROWS - 1, 0]

    res = {}
    res["hgrn_w_in"] = _adamw(hgrn_w_in[0], [parts[0], sib[0]], m_hgrn_w_in[0], v_hgrn_w_in[0], "adamw_hw_in")
    res["sb_w_in"] = _adamw(sb_w_in[0], [parts[1], sib[1]], m_sb_w_in[0], v_sb_w_in[0], "adamw_sw_in")
    res["hgrn_w_out"] = _adamw(hgrn_w_out[0], [parts[2], sib[2]], m_hgrn_w_out[0], v_hgrn_w_out[0], "adamw_hw_out")
    res["sb_w_out"] = _adamw(sb_w_out[0], [parts[3], sib[3]], m_sb_w_out[0], v_sb_w_out[0], "adamw_sw_out")
    res["pre_norm"] = _adamw(pre_norm, [small[0:2]], m_pre_norm, v_pre_norm, "adamw_pre")
    res["post_norm"] = _adamw(post_norm, [small[2:4]], m_post_norm, v_post_norm, "adamw_post")
    res["hgrn_lb"] = _adamw(hgrn_lb, [small[4:6]], m_hgrn_lb, v_hgrn_lb, "adamw_lb")
    res["hgrn_out_norm"] = _adamw(hgrn_out_norm, [small[6:7, :D_HEAD]], m_hgrn_out_norm, v_hgrn_out_norm, "adamw_head")
    g_meta = lax.dynamic_slice_in_dim(small[7:7 + N_META], chip * d4, d4, axis=1)
    res["meta_tokens"] = _adamw(meta_tokens, [g_meta], m_meta_tokens, v_meta_tokens, "adamw_meta")
    for n in ("hgrn_w_in", "hgrn_w_out", "sb_w_in", "sb_w_out"):
        res[n] = tuple(a[None] for a in res[n])
    order = ("meta_tokens", "pre_norm", "post_norm", "hgrn_w_in", "hgrn_lb", "hgrn_out_norm", "hgrn_w_out",
             "sb_w_in", "sb_w_out")
    return (loss, grad_x, *[res[n][0] for n in order], *[res[n][1] for n in order],
            *[res[n][2] for n in order], *[res[n][3] for n in order])
```

```python
import functools

import jax
import numpy as np
import jax.numpy as jnp
from jax import lax
from jax.experimental import pallas as pl
from jax.experimental.pallas import tpu as pltpu

F32 = jnp.float32
_MXU = jnp.bfloat16

D_MODEL = 1024
N_HEADS = 8
D_HEAD = 128
BLOCK = 128
N_META = 16
TILE = 256
N_PAD = TILE - N_META
UNDERFLOW = -105.0
EPS = 1e-6
SB_SCALE = D_HEAD ** -0.5
SOFTPLUS_LINEAR = 20.0
MASKED = -1e30
ADAM_LR, ADAM_B1, ADAM_B2, ADAM_EPS, ADAM_WD, ADAM_STEP = 0.001, 0.9, 0.999, 1e-08, 0.01, 10
N_CHIPS = 4
N_DEV = 8
SMALL_ROWS = 24
VMEM_LIMIT = 56 * 1024 * 1024
MESH = pl.DeviceIdType.MESH

_NT = (((1,), (1,)), ((), ()))
_TN = (((0,), (0,)), ((), ()))


def _mm(a, b):
    return jnp.dot(a.astype(_MXU), b.astype(_MXU), preferred_element_type=F32)


def _mm_nt(a, b):
    return lax.dot_general(a.astype(_MXU), b.astype(_MXU), _NT, preferred_element_type=F32)


def _mm_tn(a, b):
    return lax.dot_general(a.astype(_MXU), b.astype(_MXU), _TN, preferred_element_type=F32)


def _split2(x):
    hi = x.astype(_MXU)
    return hi, (x - hi.astype(F32)).astype(_MXU)


def _mm_s(a16, state):
    hi, lo = _split2(state)
    return jnp.dot(a16, hi, preferred_element_type=F32) + jnp.dot(a16, lo, preferred_element_type=F32)


def _mm_nt_s(a16, state):
    hi, lo = _split2(state)
    return (lax.dot_general(a16, hi, _NT, preferred_element_type=F32)
            + lax.dot_general(a16, lo, _NT, preferred_element_type=F32))


def _mm01_right(x, m01):
    return jnp.dot(x.astype(_MXU), m01, preferred_element_type=F32)


def _mm01_left(m01, x):
    hi, lo = _split2(x)
    return jnp.dot(m01, hi, preferred_element_type=F32) + jnp.dot(m01, lo, preferred_element_type=F32)


def _iota2(shape, dim):
    return lax.broadcasted_iota(jnp.int32, shape, dim)


def _row_tile(total, pref):
    t = pref
    while total % t:
        t -= BLOCK
    return t


def _params(sem, limit=VMEM_LIMIT):
    return pltpu.CompilerParams(dimension_semantics=sem, vmem_limit_bytes=limit)


def _sigmoid(x):
    return 1.0 / (1.0 + jnp.exp(-x))


def _grid_ends(ndim):
    first, last = True, True
    for d in range(ndim):
        first = first & (pl.program_id(d) == 0)
        last = last & (pl.program_id(d) == pl.num_programs(d) - 1)
    return first, last


def _place():
    return lax.axis_index("x"), lax.axis_index("y"), lax.axis_index("c")


def _exchange_scratch(n):
    return [pltpu.SemaphoreType.DMA((3 * n,)), pltpu.SemaphoreType.DMA((3 * n,)), pltpu.SemaphoreType.DMA((n,))]


def _chip_exchange(srcs, dsts, sems, slotted):
    send_sems, recv_sems, local_sems = sems
    x, y, c = _place()
    me = 2 * x + y
    peers = [(1 - x, y), (x, 1 - y), (1 - x, 1 - y)]
    n = len(dsts)

    def remote(r, a, sending):
        px, py = peers[r]
        p = 2 * px + py
        return pltpu.make_async_remote_copy(
            src_ref=srcs[a].at[p] if slotted else srcs[a], dst_ref=dsts[a].at[me if sending else p],
            send_sem=send_sems.at[r * n + a], recv_sem=recv_sems.at[r * n + a],
            device_id=(px, py, c), device_id_type=MESH)

    def local(a):
        return pltpu.make_async_copy(srcs[a].at[me] if slotted else srcs[a], dsts[a].at[me], local_sems.at[a])

    def start():
        for a in range(n):
            local(a).start()
        for r in range(3):
            for a in range(n):
                remote(r, a, True).start()

    def finish():
        for r in range(3):
            for a in range(n):
                remote(r, a, False).wait_recv()
        for r in range(3):
            for a in range(n):
                remote(r, a, True).wait_send()
        for a in range(n):
            local(a).wait()

    return start, finish


def _norm_proj(h, gain, w4, name, narrow):
    t = h.shape[0]
    tm = _row_tile(t, 2 * TILE)

    def body(h_ref, g_ref, w_ref, y_ref, s0, s1, s2, s3):
        x = h_ref[...]
        r = lax.rsqrt(jnp.mean(x * x, axis=-1, keepdims=True) + EPS)
        y = (x * r * g_ref[...]).astype(_MXU)
        y_ref[...] = y
        for j, s in enumerate((s0, s1, s2, s3)):
            s[...] = jnp.dot(y, w_ref[j], preferred_element_type=F32).astype(s.dtype)

    row = pl.BlockSpec((tm, D_MODEL), lambda i: (i, 0))
    return pl.pallas_call(
        body, name=name, grid=(t // tm,),
        in_specs=[row, pl.BlockSpec((1, D_MODEL), lambda i: (0, 0)),
                  pl.BlockSpec((4, D_MODEL, D_MODEL), lambda i: (0, 0, 0))],
        out_specs=[row] * 5,
        out_shape=[jax.ShapeDtypeStruct((t, D_MODEL), _MXU)]
        + [jax.ShapeDtypeStruct((t, D_MODEL), _MXU if n else F32) for n in narrow],
        compiler_params=_params(("arbitrary",)),
    )(h, gain, w4)


LEVELS = (64, 32, 16, 8, 4, 2, 1)
HEAD_GROUP = 2


def _hgrn_tables():
    r = np.arange(BLOCK)
    mats = [r[None, :] <= r[:, None]]
    x = r[:, None] ^ r[None, :]
    lv = np.full((BLOCK, BLOCK), len(LEVELS), np.int32)
    for i, m in enumerate(LEVELS):
        lv[(x >= m) & (x < 2 * m)] = i
    return jnp.asarray(np.concatenate(mats, 0).astype(np.float32), dtype=_MXU), jnp.asarray(lv)


def _hgrn_exponents(g, sums):
    b = jnp.dot(sums, g.astype(_MXU), preferred_element_type=F32)
    row = _iota2((BLOCK, D_HEAD), 0)
    out = []
    for m in LEVELS:
        is_q = (row & m) != 0
        if m >= 4:
            grp = b.reshape(BLOCK // (2 * m), 2 * m, D_HEAD)
            ref = jnp.broadcast_to(grp[:, m - 1:m, :], grp.shape).reshape(BLOCK, D_HEAD)
            d = b - ref
            out.append(jnp.where(is_q, d, -d))
        elif m == 2:
            below, above = pltpu.roll(g, 1, axis=0), pltpu.roll(g, BLOCK - 1, axis=0)
            low = row & 3
            out.append(jnp.where(low == 3, g + below, jnp.where(low == 2, g, jnp.where(low == 0, above, 0.0))))
        else:
            out.append(jnp.where(is_q, g, 0.0))
    return b, out


def _hgrn_gates(fz, lam, chunk):
    pos = chunk * BLOCK + _iota2((BLOCK, D_HEAD), 0)
    live = pos >= N_PAD
    sg = _sigmoid(fz)
    f = lam + (1.0 - lam) * sg
    g = jnp.where(live, jnp.log(f), 0.0)
    k = jnp.where(live, (1.0 - lam) * (1.0 - sg), 0.0)
    return sg, f, g, k, live


def _level_operand(q, k, exponent, m):
    decay = jnp.exp(exponent)
    is_q = (_iota2((BLOCK, D_HEAD), 0) & m) != 0
    return is_q, decay, (jnp.where(is_q, q, k) * decay).astype(_MXU)


def _hgrn_fwd(qs, fs, vs, lam, bsz, nb, shards):
    t = qs.shape[0]
    sums, levels = _hgrn_tables()
    width = HEAD_GROUP * D_HEAD

    n_sh = len(shards)

    def body(q_ref, f_ref, v_ref, lam_ref, sums_ref, lv_ref, *rest):
        own, (o_ref, sst_ref, sym_ref), rest = rest[:n_sh], rest[n_sh:n_sh + 3], rest[n_sh + 3:]
        gathered, st_scr, sems = rest[:n_sh], rest[n_sh], rest[n_sh + 1:]
        n = pl.program_id(2)
        first, last = _grid_ends(3)
        start, finish = _chip_exchange(own, gathered, sems, slotted=False)
        pl.when(first)(start)

        @pl.when(n == 0)
        def _():
            st_scr[...] = jnp.zeros_like(st_scr)

        lv = lv_ref[...]
        r, c = _iota2((BLOCK, BLOCK), 0), _iota2((BLOCK, BLOCK), 1)
        for hh in range(HEAD_GROUP):
            ls = slice(hh * D_HEAD, (hh + 1) * D_HEAD)
            st = st_scr[hh]
            sst_ref[0, hh, 0] = st
            q, v = q_ref[:, ls], v_ref[:, ls]
            _, _, g, k, _ = _hgrn_gates(f_ref[:, ls], lam_ref[:, ls], n)
            b, exps = _hgrn_exponents(g, sums_ref[...])
            sym = jnp.zeros((BLOCK, BLOCK), F32)
            for li, m in enumerate(LEVELS):
                _, _, x16 = _level_operand(q, k, exps[li], m)
                sym = jnp.where(lv == li, lax.dot_general(x16, x16, _NT, preferred_element_type=F32), sym)
            sym = jnp.where(c == r, jnp.sum(q * k, axis=1, keepdims=True), sym).astype(_MXU)
            sym_ref[0, hh, 0] = sym
            o_ref[:, ls] = _mm_nt(q * jnp.exp(b), st) + _mm(jnp.where(c <= r, sym, 0), v)
            b_end = b[BLOCK - 1:BLOCK, :]
            st_scr[hh] = st * jnp.exp(b_end) + _mm_tn(v, k * jnp.exp(b_end - b))
        pl.when(last)(finish)

    blk = pl.BlockSpec((BLOCK, width), lambda b, h, n: (b * nb + n, h))
    hbm = pl.BlockSpec(memory_space=pl.ANY)
    return pl.pallas_call(
        body, name="hgrn_fwd", grid=(bsz, N_HEADS // HEAD_GROUP, nb),
        in_specs=[blk, blk, blk, pl.BlockSpec((1, width), lambda b, h, n: (0, h)),
                  pl.BlockSpec(sums.shape, lambda b, h, n: (0, 0)), pl.BlockSpec(levels.shape, lambda b, h, n: (0, 0))]
        + [hbm] * n_sh,
        out_specs=[blk] + [pl.BlockSpec((1, HEAD_GROUP, 1, D_HEAD, D_HEAD), lambda b, h, n: (b, h, n, 0, 0))] * 2
        + [hbm] * n_sh,
        out_shape=[jax.ShapeDtypeStruct((t, D_MODEL), F32),
                   jax.ShapeDtypeStruct((bsz, N_HEADS, nb, D_HEAD, D_HEAD), F32),
                   jax.ShapeDtypeStruct((bsz, N_HEADS, nb, D_HEAD, D_HEAD), _MXU)]
        + [jax.ShapeDtypeStruct((N_CHIPS,) + a.shape, a.dtype) for a in shards],
        scratch_shapes=[pltpu.VMEM((HEAD_GROUP, D_HEAD, D_HEAD), F32)] + _exchange_scratch(n_sh),
        compiler_params=_params(("arbitrary", "arbitrary", "arbitrary")),
    )(qs, fs, vs, lam, sums, levels, *shards)


def _hgrn_bwd(qs, fs, vs, lam, sst, sym, do, bsz, nb, outgoing):
    t = qs.shape[0]
    sums, levels = _hgrn_tables()
    width = HEAD_GROUP * D_HEAD

    n_out = len(outgoing)

    def body(q_ref, f_ref, v_ref, lam_ref, sst_ref, sym_ref, do_ref, sums_ref, lv_ref, *rest):
        send, (dq_ref, df_ref, dv_ref, dlam_ref), rest = rest[:n_out], rest[n_out:n_out + 4], rest[n_out + 4:]
        landed, dst_scr, gsum_scr, sems = rest[:n_out], rest[n_out], rest[n_out + 1], rest[n_out + 2:]
        n = pl.program_id(2)
        chunk = nb - 1 - n
        first, last = _grid_ends(3)
        start, finish = _chip_exchange(send, landed, sems, slotted=True)
        pl.when(first)(start)

        @pl.when(n == 0)
        def _():
            dst_scr[...] = jnp.zeros_like(dst_scr)
            gsum_scr[...] = jnp.zeros_like(gsum_scr)
            dlam_ref[...] = jnp.zeros_like(dlam_ref)

        lv = lv_ref[...]
        r, c = _iota2((BLOCK, BLOCK), 0), _iota2((BLOCK, BLOCK), 1)
        for hh in range(HEAD_GROUP):
            ls = slice(hh * D_HEAD, (hh + 1) * D_HEAD)
            lam = lam_ref[:, ls]
            q, v, do = q_ref[:, ls], v_ref[:, ls], do_ref[:, ls]
            sg, f, g, k, live = _hgrn_gates(f_ref[:, ls], lam, chunk)
            b, exps = _hgrn_exponents(g, sums_ref[...])
            do16, v16 = do.astype(_MXU), v.astype(_MXU)
            da = lax.dot_general(do16, v16, _NT, preferred_element_type=F32)
            da_sym = jnp.where(c < r, da, da.T)
            dq = jnp.zeros((BLOCK, D_HEAD), F32)
            dqk = jnp.zeros((BLOCK, D_HEAD), F32)
            db_q = jnp.zeros((BLOCK, D_HEAD), F32)
            db_qk = jnp.zeros((BLOCK, D_HEAD), F32)
            for li, m in enumerate(LEVELS):
                is_q, decay, x16 = _level_operand(q, k, exps[li], m)
                y = jnp.dot(jnp.where(lv == li, da_sym, 0.0).astype(_MXU), x16, preferred_element_type=F32)
                dx = y * decay
                dq = dq + jnp.where(is_q, dx, 0.0)
                dqk = dqk + dx
                p = x16.astype(F32) * y
                db_q = db_q + jnp.where(is_q, p, 0.0)
                db_qk = db_qk + p
            dk = dqk - dq
            db = 2.0 * db_q - db_qk
            a_t = jnp.where(c >= r, sym_ref[0, hh, 0], 0)
            st, dst = sst_ref[0, hh, 0], dst_scr[hh]
            eb = jnp.exp(b)
            b_end = b[BLOCK - 1:BLOCK, :]
            dec = jnp.exp(b_end - b)
            qh16, kt16 = (q * eb).astype(_MXU), (k * dec).astype(_MXU)
            dq_st = _mm_s(do16, st)
            dk_st = _mm_s(v16, dst)
            d_diag = jnp.sum(do * v, axis=1, keepdims=True)
            dq_ref[:, ls] = (dq + d_diag * k + eb * dq_st).astype(dq_ref.dtype)
            dk = dk + d_diag * q + dec * dk_st
            dv_ref[:, ls] = (jnp.dot(a_t, do16, preferred_element_type=F32) + _mm_nt_s(kt16, dst)).astype(dv_ref.dtype)
            dst_scr[hh] = dst * jnp.exp(b_end) + lax.dot_general(do16, qh16, _TN, preferred_element_type=F32)
            db = db + (qh16.astype(F32) * dq_st - kt16.astype(F32) * dk_st)
            dg = _mm01_left((c >= r).astype(_MXU), db) + gsum_scr[:, ls]
            gsum_scr[:, ls] = gsum_scr[:, ls] + jnp.sum(db, axis=0, keepdims=True)
            slope = (1.0 - lam) * sg * (1.0 - sg)
            df_ref[:, ls] = jnp.where(live, dg * slope / f - dk * slope, 0.0).astype(df_ref.dtype)
            dl = jnp.where(live, (dg / f - dk) * (1.0 - sg), 0.0)
            dlam_ref[0, :, ls] = dlam_ref[0, :, ls] + jnp.sum(dl, axis=0, keepdims=True)
        pl.when(last)(finish)

    blk = pl.BlockSpec((BLOCK, width), lambda b, h, n: (b * nb + nb - 1 - n, h))
    hbm = pl.BlockSpec(memory_space=pl.ANY)
    return pl.pallas_call(
        body, name="hgrn_bwd", grid=(bsz, N_HEADS // HEAD_GROUP, nb),
        in_specs=[blk, blk, blk, pl.BlockSpec((1, width), lambda b, h, n: (0, h)),
                  pl.BlockSpec((1, HEAD_GROUP, 1, D_HEAD, D_HEAD), lambda b, h, n: (b, h, nb - 1 - n, 0, 0)),
                  pl.BlockSpec((1, HEAD_GROUP, 1, D_HEAD, D_HEAD), lambda b, h, n: (b, h, nb - 1 - n, 0, 0)),
                  blk, pl.BlockSpec(sums.shape, lambda b, h, n: (0, 0)), pl.BlockSpec(levels.shape, lambda b, h, n: (0, 0))]
        + [hbm] * n_out,
        out_specs=[blk, blk, blk, pl.BlockSpec((1, 1, width), lambda b, h, n: (b, 0, h))] + [hbm] * n_out,
        out_shape=[jax.ShapeDtypeStruct((t, D_MODEL), _MXU)] * 3 + [jax.ShapeDtypeStruct((bsz, 1, D_MODEL), F32)]
        + [jax.ShapeDtypeStruct(a.shape, a.dtype) for a in outgoing],
        scratch_shapes=[pltpu.VMEM((HEAD_GROUP, D_HEAD, D_HEAD), F32), pltpu.VMEM((1, width), F32)]
        + _exchange_scratch(n_out),
        compiler_params=_params(("arbitrary", "arbitrary", "arbitrary")),
    )(qs, fs, vs, lam, sst, sym, do, sums, levels, *outgoing)


def _sb_valid(ahead, col, i, j):
    return (ahead < (i - j) * TILE) & (col >= N_PAD - j * TILE)


def _sb_logits(q16, k_blk, valid):
    z = jnp.where(valid, lax.dot_general(q16, k_blk.astype(_MXU), _NT, preferred_element_type=F32) * SB_SCALE, MASKED)
    softplus = jnp.where(z > SOFTPLUS_LINEAR, z, jnp.log(1.0 + jnp.exp(jnp.minimum(z, SOFTPLUS_LINEAR))))
    return -softplus, z - softplus


def _sb_fwd(qs, ks, vs, bsz, nq):
    t = qs.shape[0]
    lp = nq * TILE
    width = HEAD_GROUP * D_HEAD
    groups = N_HEADS // HEAD_GROUP
    lanes = [slice(hh * D_HEAD, (hh + 1) * D_HEAD) for hh in range(HEAD_GROUP)]

    def body(q_ref, k_ref, v_ref, o_ref, c_ref, n_ref):
        b, h, i = pl.program_id(0), pl.program_id(1), pl.program_id(2)
        q16 = [q_ref[:, ls].astype(_MXU) for ls in lanes]
        r, c = _iota2((TILE, TILE), 0), _iota2((TILE, TILE), 1)
        after = (r > c).astype(_MXU)

        def more(carry):
            jj, _, _, top = carry
            return (jj <= i) & (top > UNDERFLOW)

        def step(carry):
            jj, accs, sums, _ = carry
            j = i - jj
            ks_ = pl.ds(pl.multiple_of(j * TILE, TILE), TILE)
            valid = _sb_valid(c - r, c, i, j)
            new_accs, new_sums = [], []
            for hh, ls in enumerate(lanes):
                keep, log_beta = _sb_logits(q16[hh], k_ref[ks_, ls], valid)
                after_s = _mm01_right(keep, after)
                a = jnp.exp(log_beta + (sums[hh] + after_s))
                new_accs.append(accs[hh] + _mm(a, v_ref[ks_, ls]))
                new_sums.append(sums[hh] + (after_s[:, 0:1] + keep[:, 0:1]))
            top = functools.reduce(jnp.maximum, [jnp.max(x) for x in new_sums])
            return jj + 1, tuple(new_accs), tuple(new_sums), top

        init = (jnp.int32(0), tuple(jnp.zeros((TILE, D_HEAD), F32) for _ in lanes),
                tuple(jnp.zeros((TILE, 1), F32) for _ in lanes), jnp.float32(0.0))
        visited, accs, sums, _ = lax.while_loop(more, step, init)
        for hh, ls in enumerate(lanes):
            o_ref[:, ls] = accs[hh]
            c_ref[:, ls] = jnp.broadcast_to(sums[hh], (TILE, D_HEAD))
        n_ref[(b * groups + h) * nq + i] = visited.astype(F32)

    blk = pl.BlockSpec((TILE, width), lambda b, h, i: (b * nq + i, h))
    seq = pl.BlockSpec((lp, width), lambda b, h, i: (b, h))
    return pl.pallas_call(
        body, name="sb_fwd", grid=(bsz, groups, nq),
        in_specs=[blk, seq, seq], out_specs=[blk, blk, pl.BlockSpec(memory_space=pltpu.SMEM)],
        out_shape=[jax.ShapeDtypeStruct((t, D_MODEL), F32)] * 2 + [jax.ShapeDtypeStruct((bsz * groups * nq,), F32)],
        compiler_params=_params(("arbitrary", "arbitrary", "arbitrary")),
    )(qs, ks, vs)


def _sb_bwd(qs, ks, vs, ctot, visited, do, bsz, nq):
    t = qs.shape[0]
    lp = nq * TILE
    width = HEAD_GROUP * D_HEAD
    groups = N_HEADS // HEAD_GROUP
    lanes = [slice(hh * D_HEAD, (hh + 1) * D_HEAD) for hh in range(HEAD_GROUP)]

    def body(n_ref, q_ref, k_ref, v_ref, c_ref, do_ref, dq_ref, dk_ref, dv_ref, dk_acc, dv_acc):
        b, h, i = pl.program_id(0), pl.program_id(1), pl.program_id(2)

        @pl.when(i == 0)
        def _():
            dk_acc[...] = jnp.zeros_like(dk_acc)
            dv_acc[...] = jnp.zeros_like(dv_acc)

        q16 = [q_ref[:, ls].astype(_MXU) for ls in lanes]
        do16 = [do_ref[:, ls].astype(_MXU) for ls in lanes]
        q16_t = [q_ref[:, ls].astype(F32).T.astype(_MXU) for ls in lanes]
        do16_t = [do_ref[:, ls].astype(F32).T.astype(_MXU) for ls in lanes]
        totals = [c_ref[:, hh * D_HEAD:hh * D_HEAD + 1] for hh in range(HEAD_GROUP)]
        r, c = _iota2((TILE, TILE), 0), _iota2((TILE, TILE), 1)
        upto = (r <= c).astype(_MXU)
        before = (r < c).astype(_MXU)
        first = jnp.maximum(i + 1 - n_ref[(b * groups + h) * nq + i].astype(jnp.int32), 0)

        def step(j, carry):
            ks_ = pl.ds(pl.multiple_of(j * TILE, TILE), TILE)
            valid = _sb_valid(c - r, c, i, j)
            out = []
            for hh, ls in enumerate(lanes):
                dq, keep_pre, g_pre = carry[hh]
                k_blk, v_blk = k_ref[ks_, ls], v_ref[ks_, ls]
                keep, log_beta = _sb_logits(q16[hh], k_blk, valid)
                keep_upto = _mm01_right(keep, upto)
                a = jnp.exp(log_beta + (totals[hh] - keep_pre - keep_upto))
                da = lax.dot_general(do16[hh], v_blk.astype(_MXU), _NT, preferred_element_type=F32)
                g = a * da
                g_inside = _mm01_right(g, before)
                g_before = g_pre + g_inside
                beta = jnp.exp(log_beta)
                dz16 = (g * (1.0 - beta) - beta * g_before).astype(_MXU)
                dq = dq + jnp.dot(dz16, k_blk.astype(_MXU), preferred_element_type=F32)
                dk_acc[ls, ks_] += SB_SCALE * jnp.dot(q16_t[hh], dz16, preferred_element_type=F32)
                dv_acc[ls, ks_] += jnp.dot(do16_t[hh], a.astype(_MXU), preferred_element_type=F32)
                out.append((dq, keep_pre + keep_upto[:, TILE - 1:TILE],
                            g_pre + (g_inside[:, TILE - 1:TILE] + g[:, TILE - 1:TILE])))
            return tuple(out)

        zero_col = jnp.zeros((TILE, 1), F32)
        init = tuple((jnp.zeros((TILE, D_HEAD), F32), zero_col, zero_col) for _ in lanes)
        res = lax.fori_loop(first, i + 1, step, init)
        for hh, ls in enumerate(lanes):
            dq_ref[:, ls] = (SB_SCALE * res[hh][0]).astype(dq_ref.dtype)

        @pl.when(i == nq - 1)
        def _():
            dk_ref[...] = dk_acc[...].T.astype(dk_ref.dtype)
            dv_ref[...] = dv_acc[...].T.astype(dv_ref.dtype)

    blk = pl.BlockSpec((TILE, width), lambda b, h, i: (b * nq + i, h))
    seq = pl.BlockSpec((lp, width), lambda b, h, i: (b, h))
    return pl.pallas_call(
        body, name="sb_bwd", grid=(bsz, groups, nq),
        in_specs=[pl.BlockSpec(memory_space=pltpu.SMEM), blk, seq, seq, blk, blk], out_specs=[blk, seq, seq],
        out_shape=[jax.ShapeDtypeStruct((t, D_MODEL), _MXU)] * 3,
        scratch_shapes=[pltpu.VMEM((width, lp), F32)] * 2,
        compiler_params=_params(("arbitrary", "arbitrary", "arbitrary")),
    )(visited, qs, ks, vs, ctot, do)


def _head_norm(o, head_gain):
    outs, rs = [], []
    for h in range(N_HEADS):
        oh = o[:, h * D_HEAD:(h + 1) * D_HEAD]
        r = lax.rsqrt(jnp.mean(oh * oh, axis=-1, keepdims=True) + EPS)
        outs.append(oh * r)
        rs.append(r)
    return outs, rs


def _mix(o, gate, head_gain):
    if head_gain is None:
        on = o
    else:
        outs, _ = _head_norm(o, head_gain)
        on = jnp.concatenate([x * head_gain for x in outs], axis=1)
    return on, on * (gate * _sigmoid(gate))


def _out_fwd(o, gate, h_in, w_out, post_gain, head_gain, name):
    t = o.shape[0]
    tm = _row_tile(t, 2 * TILE)

    def body(o_ref, g_ref, h_ref, w_ref, pg_ref, hg_ref, ho_ref, u_ref):
        _, mix = _mix(o_ref[...], g_ref[...], hg_ref[...])
        u = jnp.dot(mix.astype(_MXU), w_ref[...], preferred_element_type=F32)
        u_ref[...] = u
        r = lax.rsqrt(jnp.mean(u * u, axis=-1, keepdims=True) + EPS)
        ho_ref[...] = h_ref[...] + u * r * pg_ref[...]

    row = pl.BlockSpec((tm, D_MODEL), lambda i: (i, 0))
    vec = pl.BlockSpec((1, D_MODEL), lambda i: (0, 0))
    return pl.pallas_call(
        body, name=name, grid=(t // tm,),
        in_specs=[row, row, row, pl.BlockSpec((D_MODEL, D_MODEL), lambda i: (0, 0)), vec,
                  pl.BlockSpec((1, D_HEAD), lambda i: (0, 0))],
        out_specs=[row, row], out_shape=[jax.ShapeDtypeStruct((t, D_MODEL), F32)] * 2,
        compiler_params=_params(("arbitrary",)),
    )(o, gate, h_in, w_out, post_gain, head_gain)


def _out_fwd_loss(o, gate, h_in, w_out, post_gain, target, nq, name):
    t = o.shape[0]

    def body(o_ref, g_ref, h_ref, w_ref, pg_ref, t_ref, dh_ref, u_ref, l_ref):
        i = pl.program_id(0)

        @pl.when(i == 0)
        def _():
            l_ref[...] = jnp.zeros_like(l_ref)

        _, mix = _mix(o_ref[...], g_ref[...], None)
        u = jnp.dot(mix.astype(_MXU), w_ref[...], preferred_element_type=F32)
        u_ref[...] = u

        @pl.when(i % nq == 0)
        def _():
            dh_ref[...] = jnp.zeros_like(dh_ref)

        @pl.when(i % nq != 0)
        def _():
            r = lax.rsqrt(jnp.mean(u * u, axis=-1, keepdims=True) + EPS)
            e = h_ref[...] + u * r * pg_ref[...] - t_ref[...]
            dh_ref[...] = e * (1.0 / D_MODEL)
            l_ref[...] += jnp.sum(e * e) * (0.5 / D_MODEL)

    row = pl.BlockSpec((TILE, D_MODEL), lambda i: (i, 0))
    vec = pl.BlockSpec((1, D_MODEL), lambda i: (0, 0))
    return pl.pallas_call(
        body, name=name, grid=(t // TILE,),
        in_specs=[row, row, row, pl.BlockSpec((D_MODEL, D_MODEL), lambda i: (0, 0)), vec,
                  pl.BlockSpec((TILE, D_MODEL), lambda i: ((i // nq) * (nq - 1) + jnp.maximum(i % nq - 1, 0), 0))],
        out_specs=[row, row, pl.BlockSpec((8, 128), lambda i: (0, 0))],
        out_shape=[jax.ShapeDtypeStruct((t, D_MODEL), F32)] * 2 + [jax.ShapeDtypeStruct((8, 128), F32)],
        compiler_params=_params(("arbitrary",)),
    )(o, gate, h_in, w_out, post_gain, target)


def _out_bwd(dh, u, o, gate, w_out, post_gain, head_gain, name, narrow_do):
    t = o.shape[0]
    tm = _row_tile(t, 2 * TILE)
    has_head = head_gain is not None

    def body(*refs):
        if has_head:
            dh_ref, u_ref, o_ref, g_ref, w_ref, pg_ref, hg_ref, do_ref, dg_ref, gw_ref, gw16_ref, gp_ref, gh_ref = refs
            hg = hg_ref[...]
        else:
            dh_ref, u_ref, o_ref, g_ref, w_ref, pg_ref, do_ref, dg_ref, gw_ref, gw16_ref, gp_ref = refs
            hg = None
        first = pl.program_id(0) == 0

        @pl.when(first)
        def _():
            gw_ref[...] = jnp.zeros_like(gw_ref)
            gp_ref[...] = jnp.zeros_like(gp_ref)
            if has_head:
                gh_ref[...] = jnp.zeros_like(gh_ref)

        dr, u, o, gate = dh_ref[...], u_ref[...], o_ref[...], g_ref[...]
        r = lax.rsqrt(jnp.mean(u * u, axis=-1, keepdims=True) + EPS)
        un = u * r
        gp_ref[...] += jnp.sum(dr * un, axis=0, keepdims=True)
        dun = dr * pg_ref[...]
        du = r * (dun - un * jnp.mean(dun * un, axis=-1, keepdims=True))
        on, mix = _mix(o, gate, hg)
        du16 = du.astype(_MXU)
        gw_ref[...] += lax.dot_general(mix.astype(_MXU), du16, _TN, preferred_element_type=F32)
        dmix = lax.dot_general(du16, w_ref[...], _NT, preferred_element_type=F32)
        sg = _sigmoid(gate)
        dg_ref[...] = (dmix * on * (sg * (1.0 + gate * (1.0 - sg)))).astype(dg_ref.dtype)
        don = dmix * (gate * sg)
        if has_head:
            outs, rs = _head_norm(o, hg)
            gh = jnp.zeros((1, D_HEAD), F32)
            cols = []
            for h in range(N_HEADS):
                dn = don[:, h * D_HEAD:(h + 1) * D_HEAD]
                gh = gh + jnp.sum(dn * outs[h], axis=0, keepdims=True)
                dnn = dn * hg
                cols.append(rs[h] * (dnn - outs[h] * jnp.mean(dnn * outs[h], axis=-1, keepdims=True)))
            gh_ref[...] += gh
            do_ref[...] = jnp.concatenate(cols, axis=1)
        else:
            do_ref[...] = don.astype(do_ref.dtype)

        @pl.when(pl.program_id(0) == pl.num_programs(0) - 1)
        def _():
            gw16_ref[...] = gw_ref[...].astype(_MXU)

    row = pl.BlockSpec((tm, D_MODEL), lambda i: (i, 0))
    vec = pl.BlockSpec((1, D_MODEL), lambda i: (0, 0))
    mat = pl.BlockSpec((D_MODEL, D_MODEL), lambda i: (0, 0))
    in_specs = [row, row, row, row, mat, vec]
    args = [dh, u, o, gate, w_out, post_gain]
    out_specs = [row, row, mat, mat, vec]
    out_shape = [jax.ShapeDtypeStruct((t, D_MODEL), _MXU if narrow_do else F32),
                 jax.ShapeDtypeStruct((t, D_MODEL), _MXU)] + [jax.ShapeDtypeStruct((D_MODEL, D_MODEL), F32),
                                                                  jax.ShapeDtypeStruct((D_MODEL, D_MODEL), _MXU),
                                                                  jax.ShapeDtypeStruct((1, D_MODEL), F32)]
    if has_head:
        in_specs.append(pl.BlockSpec((1, D_HEAD), lambda i: (0, 0)))
        args.append(head_gain)
        out_specs.append(pl.BlockSpec((1, D_HEAD), lambda i: (0, 0)))
        out_shape.append(jax.ShapeDtypeStruct((1, D_HEAD), F32))
    return pl.pallas_call(
        body, name=name, grid=(t // tm,), in_specs=in_specs, out_specs=out_specs, out_shape=out_shape,
        compiler_params=_params(("arbitrary",)),
    )(*args)


def _proj_bwd(ds, w4, h_in, gain, dh_out, name, outgoing=(), nq=None):
    t = h_in.shape[0]
    tm = _row_tile(t, 2 * TILE) if nq is None else TILE
    n_out = len(outgoing)
    n_dh = 1 if nq is None else 2

    def body(d0, d1, d2, d3, w_ref, h_ref, g_ref, dho_ref, *rest):
        send, dh_refs, gg_ref, rest = rest[:n_out], rest[n_out:n_out + n_dh], rest[n_out + n_dh], rest[n_out + n_dh + 1:]
        landed, sems = rest[:n_out], rest[n_out:]
        if n_out:
            first, last = _grid_ends(1)
            start, finish = _chip_exchange(send, landed, sems, slotted=True)
            pl.when(first)(start)

        @pl.when(pl.program_id(0) == 0)
        def _():
            gg_ref[...] = jnp.zeros_like(gg_ref)

        dy = jnp.zeros((tm, D_MODEL), F32)
        for j, d in enumerate((d0, d1, d2, d3)):
            dy = dy + lax.dot_general(d[...].astype(_MXU), w_ref[j], _NT, preferred_element_type=F32)
        x = h_ref[...]
        r = lax.rsqrt(jnp.mean(x * x, axis=-1, keepdims=True) + EPS)
        xn = x * r
        gg_ref[...] += jnp.sum(dy * xn, axis=0, keepdims=True)
        dxn = dy * g_ref[...]
        dh = dho_ref[...] + r * (dxn - xn * jnp.mean(dxn * xn, axis=-1, keepdims=True))
        if nq is None:
            dh_refs[0][...] = dh
        else:
            in_front = pl.program_id(0) % nq == 0

            @pl.when(in_front)
            def _():
                dh_refs[1][...] = dh

            @pl.when(jnp.logical_not(in_front))
            def _():
                dh_refs[0][...] = dh
        if n_out:
            pl.when(last)(finish)

    row = pl.BlockSpec((tm, D_MODEL), lambda i: (i, 0))
    vec = pl.BlockSpec((1, D_MODEL), lambda i: (0, 0))
    hbm = pl.BlockSpec(memory_space=pl.ANY)
    if nq is None:
        dh_specs, dh_shapes = [row], [jax.ShapeDtypeStruct((t, D_MODEL), F32)]
    else:
        dh_specs = [pl.BlockSpec((TILE, D_MODEL), lambda i: ((i // nq) * (nq - 1) + jnp.maximum(i % nq - 1, 0), 0)),
                    pl.BlockSpec((TILE, D_MODEL), lambda i: (i // nq, 0))]
        dh_shapes = [jax.ShapeDtypeStruct((t // nq * (nq - 1), D_MODEL), F32),
                     jax.ShapeDtypeStruct((t // nq, D_MODEL), F32)]
    return pl.pallas_call(
        body, name=name, grid=(t // tm,),
        in_specs=[row] * 4 + [pl.BlockSpec((4, D_MODEL, D_MODEL), lambda i: (0, 0, 0)), row, vec, row] + [hbm] * n_out,
        out_specs=dh_specs + [vec] + [hbm] * n_out,
        out_shape=dh_shapes + [jax.ShapeDtypeStruct((1, D_MODEL), F32)]
        + [jax.ShapeDtypeStruct(a.shape, a.dtype) for a in outgoing],
        scratch_shapes=_exchange_scratch(n_out) if n_out else [],
        compiler_params=_params(("arbitrary",)),
    )(*ds, w4, h_in, gain, dh_out, *outgoing)


def _weight_grad(y, d, name):
    t = y.shape[0]
    tk = _row_tile(t, t // 4)

    def body(y_ref, d_ref, g_ref, g16_ref):
        @pl.when(pl.program_id(0) == 0)
        def _():
            g_ref[...] = jnp.zeros_like(g_ref)

        g_ref[...] += lax.dot_general(y_ref[...], d_ref[...].astype(_MXU), _TN, preferred_element_type=F32)

        @pl.when(pl.program_id(0) == pl.num_programs(0) - 1)
        def _():
            g16_ref[...] = g_ref[...].astype(_MXU)

    row = pl.BlockSpec((tk, D_MODEL), lambda i: (i, 0))
    mat = pl.BlockSpec((D_MODEL, D_MODEL), lambda i: (0, 0))
    return pl.pallas_call(
        body, name=name, grid=(t // tk,), in_specs=[row, row], out_specs=[mat, mat],
        out_shape=[jax.ShapeDtypeStruct((D_MODEL, D_MODEL), F32), jax.ShapeDtypeStruct((D_MODEL, D_MODEL), _MXU)],
        compiler_params=_params(("arbitrary",)),
    )(y, d)


def _local_step(x, target, meta, pre_norm, post_norm, lam, head_gain, hw_in, shards):
    bsz, seq, _ = x.shape
    nq = seq // TILE + 1
    nb = nq * (TILE // BLOCK)
    lp = nq * TILE
    t = bsz * lp
    d4 = D_MODEL // N_CHIPS
    front = jnp.concatenate([jnp.zeros((N_PAD, D_MODEL), F32), meta], axis=0)
    h0 = jnp.concatenate([jnp.broadcast_to(front[None], (bsz, TILE, D_MODEL)), x], axis=1).reshape(t, D_MODEL)
    pre0, pre1, post0, post1 = pre_norm[0:1], pre_norm[1:2], post_norm[0:1], post_norm[1:2]

    y0, q0, f0, v0, g0 = _norm_proj(h0, pre0, hw_in, "norm_proj_hgrn", (False,) * 4)
    o0, sst, sym, sw_in, sw_out, hw_out = _hgrn_fwd(q0, f0, v0, lam, bsz, nb, shards)
    sw_out, hw_out = sw_out.reshape(D_MODEL, D_MODEL), hw_out.reshape(D_MODEL, D_MODEL)
    h1, u0 = _out_fwd(o0, g0, h0, hw_out, post0, head_gain, "out_fwd_hgrn")
    y1, q1, k1, v1, g1 = _norm_proj(h1, pre1, sw_in, "norm_proj_sb", (True, True, True, False))
    o1, ctot, visited = _sb_fwd(q1, k1, v1, bsz, nq)
    dh2, u1, loss_blk = _out_fwd_loss(o1, g1, h1, sw_out, post1, target.reshape(bsz * seq, D_MODEL), nq, "out_fwd_sb")

    do1, dg1, g_sw_out, g_sw_out16, g_post1 = _out_bwd(dh2, u1, o1, g1, sw_out, post1, None, "out_bwd_sb", True)
    dq1, dk1, dv1 = _sb_bwd(q1, k1, v1, ctot, visited, do1, bsz, nq)
    ds1 = (dq1, dk1, dv1, dg1)
    dh1, g_pre1 = _proj_bwd(ds1, sw_in, h1, pre1, dh2, "proj_bwd_sb")
    g_sw_in = [_weight_grad(y1, d, "wgrad_sb_%d" % j) for j, d in enumerate(ds1)]

    do0, dg0, g_hw_out, g_hw_out16, g_post0, g_head = _out_bwd(dh1, u0, o0, g0, hw_out, post0, head_gain, "out_bwd_hgrn",
                                                               False)
    ready = (jnp.stack([g16 for _, g16 in g_sw_in]), g_sw_out16.reshape(N_CHIPS, d4, D_MODEL),
             g_hw_out16.reshape(N_CHIPS, d4, D_MODEL))
    dq0, df0, dv0, dlam, land_sw_in, land_sw_out, land_hw_out = _hgrn_bwd(q0, f0, v0, lam, sst, sym, do0, bsz, nb, ready)
    ds0 = (dq0, df0, dv0, dg0)
    g_hw_in = [_weight_grad(y0, d, "wgrad_hgrn_%d" % j) for j, d in enumerate(ds0)]
    last = (jnp.stack([g16 for _, g16 in g_hw_in]),)
    grad_x, dh_front, g_pre0, land_hw_in = _proj_bwd(ds0, hw_in, h0, pre0, dh1, "proj_bwd_hgrn", last, nq)

    grad_x = grad_x.reshape(bsz, seq, D_MODEL)
    g_meta = jnp.sum(dh_front.reshape(bsz, TILE, D_MODEL)[:, N_PAD:, :], axis=0)
    g_lam = jnp.sum(dlam, axis=0)
    small = jnp.concatenate([g_pre0, g_pre1, g_post0, g_post1, g_lam, g_lam,
                             jnp.pad(g_head, ((0, 0), (0, D_MODEL - D_HEAD))), g_meta,
                             jnp.pad(loss_blk[0:1], ((0, 0), (0, D_MODEL - loss_blk.shape[1])))], axis=0)
    rows4 = lambda g: [g[j * d4:(j + 1) * d4] for j in range(N_CHIPS)]
    large = dict(hw_in=(land_hw_in, [g for g, _ in g_hw_in]), sw_in=(land_sw_in, [g for g, _ in g_sw_in]),
                 hw_out=(land_hw_out, rows4(g_hw_out)), sw_out=(land_sw_out, rows4(g_sw_out)))
    return grad_x, small, large


def _prep_weights(hw_in, sw_in, hw_out, sw_out, meta):
    def body(hi_ref, si_ref, ho_ref, so_ref, m_ref, ghi, gm, si16, so16, ho16, far_send, far_recv, near_send, near_recv):
        x, y, c = _place()
        me = 2 * x + y
        ghi[me] = hi_ref[0].astype(_MXU)
        gm[me] = m_ref[...]
        si16[...] = si_ref[0].astype(_MXU)
        so16[...] = so_ref[0].astype(_MXU)
        ho16[...] = ho_ref[0].astype(_MXU)
        outs = (ghi, gm)
        n = len(outs)
        peers = [(1 - x, y), (x, 1 - y), (1 - x, 1 - y)]

        def half(a, slot, which):
            rows = outs[a].shape[1] // 2
            return outs[a].at[slot, pl.ds(which * rows, rows), :]

        def far(r, a, slot):
            px, py = peers[r]
            return pltpu.make_async_remote_copy(
                src_ref=half(a, slot, c), dst_ref=half(a, slot, c), send_sem=far_send.at[r * n + a],
                recv_sem=far_recv.at[r * n + a], device_id=(px, py, c), device_id_type=MESH)

        def near(r, a, which):
            px, py = peers[r]
            return pltpu.make_async_remote_copy(
                src_ref=half(a, 2 * px + py, which), dst_ref=half(a, 2 * px + py, which),
                send_sem=near_send.at[r * n + a], recv_sem=near_recv.at[r * n + a],
                device_id=(x, y, 1 - c), device_id_type=MESH)

        for r in range(3):
            for a in range(n):
                far(r, a, me).start()
        for r, (px, py) in enumerate(peers):
            for a in range(n):
                far(r, a, 2 * px + py).wait_recv()
                near(r, a, c).start()
        for r in range(3):
            for a in range(n):
                near(r, a, 1 - c).wait_recv()
        for r in range(3):
            for a in range(n):
                far(r, a, me).wait_send()
                near(r, a, c).wait_send()

    d4 = D_MODEL // N_CHIPS
    vm = pl.BlockSpec(memory_space=pltpu.VMEM)
    return pl.pallas_call(
        body, name="prep_weights",
        in_specs=[vm] * 5, out_specs=[vm] * 5,
        out_shape=[jax.ShapeDtypeStruct((N_CHIPS, D_MODEL, D_MODEL), _MXU), jax.ShapeDtypeStruct((N_CHIPS, N_META, d4), F32),
                   jax.ShapeDtypeStruct((D_MODEL, D_MODEL), _MXU), jax.ShapeDtypeStruct((d4, D_MODEL), _MXU),
                   jax.ShapeDtypeStruct((d4, D_MODEL), _MXU)],
        scratch_shapes=[pltpu.SemaphoreType.DMA((6,))] * 4,
        compiler_params=pltpu.CompilerParams(vmem_limit_bytes=VMEM_LIMIT),
    )(hw_in, sw_in, hw_out, sw_out, meta)


def _scatter_small(small):
    def body(sm, lsm, send_sems, recv_sems, local_sem):
        x, y, c = _place()
        mine = 4 * x + 2 * y + c
        local = pltpu.make_async_copy(sm, lsm.at[mine], local_sem)
        local.start()

        def copy(rel, src_dev, to):
            return pltpu.make_async_remote_copy(
                src_ref=sm, dst_ref=lsm.at[src_dev], send_sem=send_sems.at[rel - 1], recv_sem=recv_sems.at[rel - 1],
                device_id=to, device_id_type=MESH)

        flip = lambda bit, v: 1 - v if bit else v
        rels = [(rel, flip(rel & 4, x), flip(rel & 2, y), flip(rel & 1, c)) for rel in range(1, N_DEV)]
        sends = [copy(rel, mine, (px, py, pc)) for rel, px, py, pc in rels]
        for cp in sends:
            cp.start()
        for rel, px, py, pc in rels:
            copy(rel, 4 * px + 2 * py + pc, (px, py, pc)).wait_recv()
        for cp in sends:
            cp.wait_send()
        local.wait()

    hbm = pl.BlockSpec(memory_space=pl.ANY)
    return pl.pallas_call(
        body, name="scatter_small", in_specs=[hbm], out_specs=hbm,
        out_shape=jax.ShapeDtypeStruct((N_DEV, SMALL_ROWS, D_MODEL), F32),
        scratch_shapes=[pltpu.SemaphoreType.DMA((N_DEV - 1,)), pltpu.SemaphoreType.DMA((N_DEV - 1,)),
                        pltpu.SemaphoreType.DMA(())],
    )(small)


def _sum_slots(landed, own, me, name):
    n, rows, _ = landed.shape
    tm = rows if rows < 256 else 256

    def body(me_ref, l_ref, o0, o1, o2, o3, out_ref):
        acc = None
        for k, o in enumerate((o0, o1, o2, o3)):
            term = jnp.where(me_ref[0] == k, o[...], l_ref[k].astype(F32))
            acc = term if acc is None else acc + term
        out_ref[...] = acc

    blk = pl.BlockSpec((tm, D_MODEL), lambda i: (i, 0))
    return pl.pallas_call(
        body, name=name, grid=(rows // tm,),
        in_specs=[pl.BlockSpec(memory_space=pltpu.SMEM), pl.BlockSpec((n, tm, D_MODEL), lambda i: (0, i, 0))] + [blk] * 4,
        out_specs=blk, out_shape=jax.ShapeDtypeStruct((rows, D_MODEL), F32),
        compiler_params=_params(("arbitrary",)),
    )(me, landed, *own)


def _swap_with_sibling(parts):
    def body(a0, a1, a2, a3, b0, b1, b2, b3, send_sems, recv_sems):
        x, y, c = _place()
        copies = [pltpu.make_async_remote_copy(src_ref=s, dst_ref=d, send_sem=send_sems.at[a], recv_sem=recv_sems.at[a],
                                               device_id=(x, y, 1 - c), device_id_type=MESH)
                  for a, (s, d) in enumerate(zip((a0, a1, a2, a3), (b0, b1, b2, b3)))]
        for cp in copies:
            cp.start()
        for cp in copies:
            cp.wait()

    hbm = pl.BlockSpec(memory_space=pl.ANY)
    return pl.pallas_call(
        body, name="swap_with_sibling", in_specs=[hbm] * 4, out_specs=[hbm] * 4,
        out_shape=[jax.ShapeDtypeStruct(p.shape, F32) for p in parts],
        scratch_shapes=[pltpu.SemaphoreType.DMA((4,)), pltpu.SemaphoreType.DMA((4,))],
    )(*parts)


def _adamw_math(w, g, m, v):
    m = ADAM_B1 * m + (1.0 - ADAM_B1) * g
    v = ADAM_B2 * v + (1.0 - ADAM_B2) * (g * g)
    m_hat = m / (1.0 - ADAM_B1 ** ADAM_STEP)
    v_hat = v / (1.0 - ADAM_B2 ** ADAM_STEP)
    delta = -ADAM_LR * (m_hat / (jnp.sqrt(v_hat) + ADAM_EPS) + ADAM_WD * w)
    return delta, m, v


def _adamw(w, g_parts, m, v, name):
    rows, cols = w.shape
    tm = rows if rows < 256 else 256
    n = len(g_parts)

    def body(*refs):
        w_ref, m_ref, v_ref = refs[n:n + 3]
        g_ref, d_ref, nm_ref, nv_ref = refs[n + 3:]
        g = refs[0][...]
        for p in refs[1:n]:
            g = g + p[...]
        g_ref[...] = g
        d_ref[...], nm_ref[...], nv_ref[...] = _adamw_math(w_ref[...], g, m_ref[...], v_ref[...])

    blk = pl.BlockSpec((tm, cols), lambda i: (i, 0))
    return pl.pallas_call(
        body, name=name, grid=(rows // tm,), in_specs=[blk] * (n + 3), out_specs=[blk] * 4,
        out_shape=[jax.ShapeDtypeStruct((rows, cols), F32)] * 4,
        compiler_params=_params(("arbitrary",)),
    )(*g_parts, w, m, v)


def _lam_of(hgrn_lb):
    def body(lb_ref, o_ref):
        lb = lb_ref[...]
        e = jnp.exp(lb - jnp.max(lb, axis=0, keepdims=True))
        o_ref[...] = e[0:1, :] / jnp.sum(e, axis=0, keepdims=True)

    return pl.pallas_call(body, name="lam_of", out_shape=jax.ShapeDtypeStruct((1, D_MODEL), F32))(hgrn_lb)


def _small_grads(land_small, lam):
    def body(l_ref, lam_ref, o_ref):
        acc = l_ref[0]
        for k in range(1, N_DEV):
            acc = acc + l_ref[k]
        p = lam_ref[...]
        slope = p * (1.0 - p)
        row = _iota2((SMALL_ROWS, D_MODEL), 0)
        o_ref[...] = acc * jnp.where(row == 4, slope, jnp.where(row == 5, -slope, 1.0))

    return pl.pallas_call(body, name="small_grads",
                          out_shape=jax.ShapeDtypeStruct((SMALL_ROWS, D_MODEL), F32))(land_small, lam)


def kernel(x, meta_tokens, pre_norm, post_norm, hgrn_w_in, hgrn_lb, hgrn_out_norm, hgrn_w_out, sb_w_in, sb_w_out, loss_target, m_meta_tokens, m_pre_norm, m_post_norm, m_hgrn_w_in, m_hgrn_lb, m_hgrn_out_norm, m_hgrn_w_out, m_sb_w_in, m_sb_w_out, v_meta_tokens, v_pre_norm, v_post_norm, v_hgrn_w_in, v_hgrn_lb, v_hgrn_out_norm, v_hgrn_w_out, v_sb_w_in, v_sb_w_out):
    d4 = D_MODEL // N_CHIPS
    chip = 2 * lax.axis_index("x") + lax.axis_index("y")
    hw_in, meta4, sw_in16, sw_out16, hw_out16 = _prep_weights(hgrn_w_in, sb_w_in, hgrn_w_out, sb_w_out, meta_tokens)
    meta = meta4.transpose(1, 0, 2).reshape(N_META, D_MODEL)
    lam = _lam_of(hgrn_lb)
    grad_x, small, large = _local_step(
        x, loss_target, meta, pre_norm, post_norm, lam, hgrn_out_norm,
        hw_in, (sw_in16, sw_out16, hw_out16))

    me = jnp.reshape(chip, (1,)).astype(jnp.int32)
    parts = [_sum_slots(*large[n], me, "sum_" + n) for n in ("hw_in", "sw_in", "hw_out", "sw_out")]
    sib = _swap_with_sibling(parts)
    small = _small_grads(_scatter_small(small), lam)
    loss = small[SMALL_ROWS - 1, 0]

    res = {}
    res["hgrn_w_in"] = _adamw(hgrn_w_in[0], [parts[0], sib[0]], m_hgrn_w_in[0], v_hgrn_w_in[0], "adamw_hw_in")
    res["sb_w_in"] = _adamw(sb_w_in[0], [parts[1], sib[1]], m_sb_w_in[0], v_sb_w_in[0], "adamw_sw_in")
    res["hgrn_w_out"] = _adamw(hgrn_w_out[0], [parts[2], sib[2]], m_hgrn_w_out[0], v_hgrn_w_out[0], "adamw_hw_out")
    res["sb_w_out"] = _adamw(sb_w_out[0], [parts[3], sib[3]], m_sb_w_out[0], v_sb_w_out[0], "adamw_sw_out")
    res["pre_norm"] = _adamw(pre_norm, [small[0:2]], m_pre_norm, v_pre_norm, "adamw_pre")
    res["post_norm"] = _adamw(post_norm, [small[2:4]], m_post_norm, v_post_norm, "adamw_post")
    res["hgrn_lb"] = _adamw(hgrn_lb, [small[4:6]], m_hgrn_lb, v_hgrn_lb, "adamw_lb")
    res["hgrn_out_norm"] = _adamw(hgrn_out_norm, [small[6:7, :D_HEAD]], m_hgrn_out_norm, v_hgrn_out_norm, "adamw_head")
    g_meta = lax.dynamic_slice_in_dim(small[7:7 + N_META], chip * d4, d4, axis=1)
    res["meta_tokens"] = _adamw(meta_tokens, [g_meta], m_meta_tokens, v_meta_tokens, "adamw_meta")
    for n in ("hgrn_w_in", "hgrn_w_out", "sb_w_in", "sb_w_out"):
        res[n] = tuple(a[None] for a in res[n])
    order = ("meta_tokens", "pre_norm", "post_norm", "hgrn_w_in", "hgrn_lb", "hgrn_out_norm", "hgrn_w_out",
             "sb_w_in", "sb_w_out")
    return (loss, grad_x, *[res[n][0] for n in order], *[res[n][1] for n in order],
            *[res[n][2] for n in order], *[res[n][3] for n in order])
```

```python
import functools

import jax
import numpy as np
import jax.numpy as jnp
from jax import lax
from jax.experimental import pallas as pl
from jax.experimental.pallas import tpu as pltpu

F32 = jnp.float32
_MXU = jnp.bfloat16

D_MODEL = 1024
N_HEADS = 8
D_HEAD = 128
BLOCK = 128
N_META = 16
TILE = 256
N_PAD = TILE - N_META
UNDERFLOW = -105.0
EPS = 1e-6
SB_SCALE = D_HEAD ** -0.5
SOFTPLUS_LINEAR = 20.0
MASKED = -1e30
ADAM_LR, ADAM_B1, ADAM_B2, ADAM_EPS, ADAM_WD, ADAM_STEP = 0.001, 0.9, 0.999, 1e-08, 0.01, 10
N_CHIPS = 4
N_DEV = 8
SMALL_ROWS = 24
VMEM_LIMIT = 56 * 1024 * 1024
MESH = pl.DeviceIdType.MESH

_NT = (((1,), (1,)), ((), ()))
_TN = (((0,), (0,)), ((), ()))


def _mm(a, b):
    return jnp.dot(a.astype(_MXU), b.astype(_MXU), preferred_element_type=F32)


def _mm_nt(a, b):
    return lax.dot_general(a.astype(_MXU), b.astype(_MXU), _NT, preferred_element_type=F32)


def _mm_tn(a, b):
    return lax.dot_general(a.astype(_MXU), b.astype(_MXU), _TN, preferred_element_type=F32)


def _split2(x):
    hi = x.astype(_MXU)
    return hi, (x - hi.astype(F32)).astype(_MXU)


def _mm_s(a16, state):
    hi, lo = _split2(state)
    return jnp.dot(a16, hi, preferred_element_type=F32) + jnp.dot(a16, lo, preferred_element_type=F32)


def _mm_nt_s(a16, state):
    hi, lo = _split2(state)
    return (lax.dot_general(a16, hi, _NT, preferred_element_type=F32)
            + lax.dot_general(a16, lo, _NT, preferred_element_type=F32))


def _mm01_right(x, m01):
    return jnp.dot(x.astype(_MXU), m01, preferred_element_type=F32)


def _mm01_left(m01, x):
    hi, lo = _split2(x)
    return jnp.dot(m01, hi, preferred_element_type=F32) + jnp.dot(m01, lo, preferred_element_type=F32)


def _iota2(shape, dim):
    return lax.broadcasted_iota(jnp.int32, shape, dim)


def _row_tile(total, pref):
    t = pref
    while total % t:
        t -= BLOCK
    return t


def _params(sem, limit=VMEM_LIMIT):
    return pltpu.CompilerParams(dimension_semantics=sem, vmem_limit_bytes=limit)


def _sigmoid(x):
    return 1.0 / (1.0 + jnp.exp(-x))


def _grid_ends(ndim):
    first, last = True, True
    for d in range(ndim):
        first = first & (pl.program_id(d) == 0)
        last = last & (pl.program_id(d) == pl.num_programs(d) - 1)
    return first, last


def _place():
    return lax.axis_index("x"), lax.axis_index("y"), lax.axis_index("c")


def _exchange_scratch(n):
    return [pltpu.SemaphoreType.DMA((3 * n,)), pltpu.SemaphoreType.DMA((3 * n,)), pltpu.SemaphoreType.DMA((n,))]


def _chip_exchange(srcs, dsts, sems, slotted):
    send_sems, recv_sems, local_sems = sems
    x, y, c = _place()
    me = 2 * x + y
    peers = [(1 - x, y), (x, 1 - y), (1 - x, 1 - y)]
    n = len(dsts)

    def remote(r, a, sending):
        px, py = peers[r]
        p = 2 * px + py
        return pltpu.make_async_remote_copy(
            src_ref=srcs[a].at[p] if slotted else srcs[a], dst_ref=dsts[a].at[me if sending else p],
            send_sem=send_sems.at[r * n + a], recv_sem=recv_sems.at[r * n + a],
            device_id=(px, py, c), device_id_type=MESH)

    def local(a):
        return pltpu.make_async_copy(srcs[a].at[me] if slotted else srcs[a], dsts[a].at[me], local_sems.at[a])

    def start():
        for a in range(n):
            local(a).start()
        for r in range(3):
            for a in range(n):
                remote(r, a, True).start()

    def finish():
        for r in range(3):
            for a in range(n):
                remote(r, a, False).wait_recv()
        for r in range(3):
            for a in range(n):
                remote(r, a, True).wait_send()
        for a in range(n):
            local(a).wait()

    return start, finish


def _norm_proj(h, gain, w4, name, narrow):
    t = h.shape[0]
    tm = _row_tile(t, 2 * TILE)

    def body(h_ref, g_ref, w_ref, y_ref, s0, s1, s2, s3):
        x = h_ref[...]
        r = lax.rsqrt(jnp.mean(x * x, axis=-1, keepdims=True) + EPS)
        y = (x * r * g_ref[...]).astype(_MXU)
        y_ref[...] = y
        for j, s in enumerate((s0, s1, s2, s3)):
            s[...] = jnp.dot(y, w_ref[j], preferred_element_type=F32).astype(s.dtype)

    row = pl.BlockSpec((tm, D_MODEL), lambda i: (i, 0))
    return pl.pallas_call(
        body, name=name, grid=(t // tm,),
        in_specs=[row, pl.BlockSpec((1, D_MODEL), lambda i: (0, 0)),
                  pl.BlockSpec((4, D_MODEL, D_MODEL), lambda i: (0, 0, 0))],
        out_specs=[row] * 5,
        out_shape=[jax.ShapeDtypeStruct((t, D_MODEL), _MXU)]
        + [jax.ShapeDtypeStruct((t, D_MODEL), _MXU if n else F32) for n in narrow],
        compiler_params=_params(("arbitrary",)),
    )(h, gain, w4)


LEVELS = (64, 32, 16, 8, 4, 2, 1)
HEAD_GROUP = 2
SB_FWD_GROUP = 4


def _hgrn_tables():
    r = np.arange(BLOCK)
    mats = [r[None, :] <= r[:, None]]
    x = r[:, None] ^ r[None, :]
    lv = np.full((BLOCK, BLOCK), len(LEVELS), np.int32)
    for i, m in enumerate(LEVELS):
        lv[(x >= m) & (x < 2 * m)] = i
    return jnp.asarray(np.concatenate(mats, 0).astype(np.float32), dtype=_MXU), jnp.asarray(lv)


def _hgrn_exponents(g, sums):
    b = _mm01_left(sums, g)
    row = _iota2((BLOCK, D_HEAD), 0)
    out = []
    for m in LEVELS:
        is_q = (row & m) != 0
        if m >= 4:
            grp = b.reshape(BLOCK // (2 * m), 2 * m, D_HEAD)
            ref = jnp.broadcast_to(grp[:, m - 1:m, :], grp.shape).reshape(BLOCK, D_HEAD)
            d = b - ref
            out.append(jnp.where(is_q, d, -d))
        elif m == 2:
            below, above = pltpu.roll(g, 1, axis=0), pltpu.roll(g, BLOCK - 1, axis=0)
            low = row & 3
            out.append(jnp.where(low == 3, g + below, jnp.where(low == 2, g, jnp.where(low == 0, above, 0.0))))
        else:
            out.append(jnp.where(is_q, g, 0.0))
    return b, out


def _hgrn_gates(fz, lam, chunk):
    pos = chunk * BLOCK + _iota2((BLOCK, D_HEAD), 0)
    live = pos >= N_PAD
    sg = _sigmoid(fz)
    f = lam + (1.0 - lam) * sg
    g = jnp.where(live, jnp.log(f), 0.0)
    k = jnp.where(live, (1.0 - lam) * (1.0 - sg), 0.0)
    return sg, f, g, k, live


def _level_operand(q, k, exponent, m):
    decay = jnp.exp(exponent)
    is_q = (_iota2((BLOCK, D_HEAD), 0) & m) != 0
    return is_q, decay, (jnp.where(is_q, q, k) * decay).astype(_MXU)


def _hgrn_fwd(qs, fs, vs, lam, bsz, nb, shards):
    t = qs.shape[0]
    sums, levels = _hgrn_tables()
    width = HEAD_GROUP * D_HEAD

    n_sh = len(shards)

    def body(q_ref, f_ref, v_ref, lam_ref, sums_ref, lv_ref, *rest):
        own, (o_ref, sst_ref, sym_ref), rest = rest[:n_sh], rest[n_sh:n_sh + 3], rest[n_sh + 3:]
        gathered, st_scr, sems = rest[:n_sh], rest[n_sh], rest[n_sh + 1:]
        n = pl.program_id(2)
        first, last = _grid_ends(3)
        start, finish = _chip_exchange(own, gathered, sems, slotted=False)
        pl.when(first)(start)

        @pl.when(n == 0)
        def _():
            st_scr[...] = jnp.zeros_like(st_scr)

        lv = lv_ref[...]
        r, c = _iota2((BLOCK, BLOCK), 0), _iota2((BLOCK, BLOCK), 1)
        for hh in range(HEAD_GROUP):
            ls = slice(hh * D_HEAD, (hh + 1) * D_HEAD)
            st = st_scr[hh]
            sst_ref[0, hh, 0] = st
            q, v = q_ref[:, ls], v_ref[:, ls]
            _, _, g, k, _ = _hgrn_gates(f_ref[:, ls], lam_ref[:, ls], n)
            b, exps = _hgrn_exponents(g, sums_ref[...])
            sym = jnp.zeros((BLOCK, BLOCK), F32)
            for li, m in enumerate(LEVELS):
                _, _, x16 = _level_operand(q, k, exps[li], m)
                sym = jnp.where(lv == li, lax.dot_general(x16, x16, _NT, preferred_element_type=F32), sym)
            sym = jnp.where(c == r, jnp.sum(q * k, axis=1, keepdims=True), sym).astype(_MXU)
            sym_ref[0, hh, 0] = sym
            o_ref[:, ls] = _mm_nt(q * jnp.exp(b), st) + _mm(jnp.where(c <= r, sym, 0), v)
            b_end = b[BLOCK - 1:BLOCK, :]
            st_scr[hh] = st * jnp.exp(b_end) + _mm_tn(v, k * jnp.exp(b_end - b))
        pl.when(last)(finish)

    blk = pl.BlockSpec((BLOCK, width), lambda b, h, n: (b * nb + n, h))
    hbm = pl.BlockSpec(memory_space=pl.ANY)
    return pl.pallas_call(
        body, name="hgrn_fwd", grid=(bsz, N_HEADS // HEAD_GROUP, nb),
        in_specs=[blk, blk, blk, pl.BlockSpec((1, width), lambda b, h, n: (0, h)),
                  pl.BlockSpec(sums.shape, lambda b, h, n: (0, 0)), pl.BlockSpec(levels.shape, lambda b, h, n: (0, 0))]
        + [hbm] * n_sh,
        out_specs=[blk] + [pl.BlockSpec((1, HEAD_GROUP, 1, D_HEAD, D_HEAD), lambda b, h, n: (b, h, n, 0, 0))] * 2
        + [hbm] * n_sh,
        out_shape=[jax.ShapeDtypeStruct((t, D_MODEL), F32),
                   jax.ShapeDtypeStruct((bsz, N_HEADS, nb, D_HEAD, D_HEAD), F32),
                   jax.ShapeDtypeStruct((bsz, N_HEADS, nb, D_HEAD, D_HEAD), _MXU)]
        + [jax.ShapeDtypeStruct((N_CHIPS,) + a.shape, a.dtype) for a in shards],
        scratch_shapes=[pltpu.VMEM((HEAD_GROUP, D_HEAD, D_HEAD), F32)] + _exchange_scratch(n_sh),
        compiler_params=_params(("arbitrary", "arbitrary", "arbitrary")),
    )(qs, fs, vs, lam, sums, levels, *shards)


def _hgrn_bwd(qs, fs, vs, lam, sst, sym, do, bsz, nb, outgoing):
    t = qs.shape[0]
    sums, levels = _hgrn_tables()
    width = HEAD_GROUP * D_HEAD

    n_out = len(outgoing)

    def body(q_ref, f_ref, v_ref, lam_ref, sst_ref, sym_ref, do_ref, sums_ref, lv_ref, *rest):
        send, (dq_ref, df_ref, dv_ref, dlam_ref), rest = rest[:n_out], rest[n_out:n_out + 4], rest[n_out + 4:]
        landed, dst_scr, gsum_scr, sems = rest[:n_out], rest[n_out], rest[n_out + 1], rest[n_out + 2:]
        n = pl.program_id(2)
        chunk = nb - 1 - n
        first, last = _grid_ends(3)
        start, finish = _chip_exchange(send, landed, sems, slotted=True)
        pl.when(first)(start)

        @pl.when(n == 0)
        def _():
            dst_scr[...] = jnp.zeros_like(dst_scr)
            gsum_scr[...] = jnp.zeros_like(gsum_scr)
            dlam_ref[...] = jnp.zeros_like(dlam_ref)

        lv = lv_ref[...]
        r, c = _iota2((BLOCK, BLOCK), 0), _iota2((BLOCK, BLOCK), 1)
        for hh in range(HEAD_GROUP):
            ls = slice(hh * D_HEAD, (hh + 1) * D_HEAD)
            lam = lam_ref[:, ls]
            q, v, do = q_ref[:, ls], v_ref[:, ls], do_ref[:, ls]
            sg, f, g, k, live = _hgrn_gates(f_ref[:, ls], lam, chunk)
            b, exps = _hgrn_exponents(g, sums_ref[...])
            do16, v16 = do.astype(_MXU), v.astype(_MXU)
            da = lax.dot_general(do16, v16, _NT, preferred_element_type=F32)
            da_sym = jnp.where(c < r, da, da.T)
            dq = jnp.zeros((BLOCK, D_HEAD), F32)
            dqk = jnp.zeros((BLOCK, D_HEAD), F32)
            db_q = jnp.zeros((BLOCK, D_HEAD), F32)
            db_qk = jnp.zeros((BLOCK, D_HEAD), F32)
            for li, m in enumerate(LEVELS):
                is_q, decay, x16 = _level_operand(q, k, exps[li], m)
                y = jnp.dot(jnp.where(lv == li, da_sym, 0.0).astype(_MXU), x16, preferred_element_type=F32)
                dx = y * decay
                dq = dq + jnp.where(is_q, dx, 0.0)
                dqk = dqk + dx
                p = x16.astype(F32) * y
                db_q = db_q + jnp.where(is_q, p, 0.0)
                db_qk = db_qk + p
            dk = dqk - dq
            db = 2.0 * db_q - db_qk
            a_t = jnp.where(c >= r, sym_ref[0, hh, 0], 0)
            st, dst = sst_ref[0, hh, 0], dst_scr[hh]
            eb = jnp.exp(b)
            b_end = b[BLOCK - 1:BLOCK, :]
            dec = jnp.exp(b_end - b)
            qh16, kt16 = (q * eb).astype(_MXU), (k * dec).astype(_MXU)
            dq_st = _mm_s(do16, st)
            dk_st = _mm_s(v16, dst)
            d_diag = jnp.sum(do * v, axis=1, keepdims=True)
            dq_ref[:, ls] = (dq + d_diag * k + eb * dq_st).astype(dq_ref.dtype)
            dk = dk + d_diag * q + dec * dk_st
            dv_ref[:, ls] = (jnp.dot(a_t, do16, preferred_element_type=F32) + _mm_nt_s(kt16, dst)).astype(dv_ref.dtype)
            dst_scr[hh] = dst * jnp.exp(b_end) + lax.dot_general(do16, qh16, _TN, preferred_element_type=F32)
            db = db + (qh16.astype(F32) * dq_st - kt16.astype(F32) * dk_st)
            dg = _mm01_left((c >= r).astype(_MXU), db) + gsum_scr[:, ls]
            gsum_scr[:, ls] = gsum_scr[:, ls] + jnp.sum(db, axis=0, keepdims=True)
            slope = (1.0 - lam) * sg * (1.0 - sg)
            df_ref[:, ls] = jnp.where(live, dg * slope / f - dk * slope, 0.0).astype(df_ref.dtype)
            dl = jnp.where(live, (dg / f - dk) * (1.0 - sg), 0.0)
            dlam_ref[0, :, ls] = dlam_ref[0, :, ls] + jnp.sum(dl, axis=0, keepdims=True)
        pl.when(last)(finish)

    blk = pl.BlockSpec((BLOCK, width), lambda b, h, n: (b * nb + nb - 1 - n, h))
    hbm = pl.BlockSpec(memory_space=pl.ANY)
    return pl.pallas_call(
        body, name="hgrn_bwd", grid=(bsz, N_HEADS // HEAD_GROUP, nb),
        in_specs=[blk, blk, blk, pl.BlockSpec((1, width), lambda b, h, n: (0, h)),
                  pl.BlockSpec((1, HEAD_GROUP, 1, D_HEAD, D_HEAD), lambda b, h, n: (b, h, nb - 1 - n, 0, 0)),
                  pl.BlockSpec((1, HEAD_GROUP, 1, D_HEAD, D_HEAD), lambda b, h, n: (b, h, nb - 1 - n, 0, 0)),
                  blk, pl.BlockSpec(sums.shape, lambda b, h, n: (0, 0)), pl.BlockSpec(levels.shape, lambda b, h, n: (0, 0))]
        + [hbm] * n_out,
        out_specs=[blk, blk, blk, pl.BlockSpec((1, 1, width), lambda b, h, n: (b, 0, h))] + [hbm] * n_out,
        out_shape=[jax.ShapeDtypeStruct((t, D_MODEL), _MXU)] * 3 + [jax.ShapeDtypeStruct((bsz, 1, D_MODEL), F32)]
        + [jax.ShapeDtypeStruct(a.shape, a.dtype) for a in outgoing],
        scratch_shapes=[pltpu.VMEM((HEAD_GROUP, D_HEAD, D_HEAD), F32), pltpu.VMEM((1, width), F32)]
        + _exchange_scratch(n_out),
        compiler_params=_params(("arbitrary", "arbitrary", "arbitrary")),
    )(qs, fs, vs, lam, sst, sym, do, sums, levels, *outgoing)


def _sb_valid(ahead, col, i, j):
    return (ahead < (i - j) * TILE) & (col >= N_PAD - j * TILE)


def _sb_logits(q16, k_blk, valid):
    z = jnp.where(valid, lax.dot_general(q16, k_blk.astype(_MXU), _NT, preferred_element_type=F32) * SB_SCALE, MASKED)
    softplus = jnp.where(z > SOFTPLUS_LINEAR, z, jnp.log(1.0 + jnp.exp(jnp.minimum(z, SOFTPLUS_LINEAR))))
    return -softplus, z - softplus


def _sb_fwd(qs, ks, vs, bsz, nq):
    t = qs.shape[0]
    lp = nq * TILE
    width = SB_FWD_GROUP * D_HEAD
    groups = N_HEADS // SB_FWD_GROUP
    lanes = [slice(hh * D_HEAD, (hh + 1) * D_HEAD) for hh in range(SB_FWD_GROUP)]

    def body(q_ref, k_ref, v_ref, o_ref, c_ref, n_ref):
        b, h, i = pl.program_id(0), pl.program_id(1), pl.program_id(2)
        q16 = [q_ref[:, ls].astype(_MXU) for ls in lanes]
        r, c = _iota2((TILE, TILE), 0), _iota2((TILE, TILE), 1)
        after = (r > c).astype(_MXU)

        def more(carry):
            jj, _, _, top = carry
            return (jj <= i) & (top > UNDERFLOW)

        def step(carry):
            jj, accs, sums, _ = carry
            j = i - jj
            ks_ = pl.ds(pl.multiple_of(j * TILE, TILE), TILE)
            valid = _sb_valid(c - r, c, i, j)
            new_accs, new_sums = [], []
            for hh, ls in enumerate(lanes):
                keep, log_beta = _sb_logits(q16[hh], k_ref[ks_, ls], valid)
                after_s = _mm01_right(keep, after)
                a = jnp.exp(log_beta + (sums[hh] + after_s))
                new_accs.append(accs[hh] + _mm(a, v_ref[ks_, ls]))
                new_sums.append(sums[hh] + (after_s[:, 0:1] + keep[:, 0:1]))
            top = functools.reduce(jnp.maximum, [jnp.max(x) for x in new_sums])
            return jj + 1, tuple(new_accs), tuple(new_sums), top

        init = (jnp.int32(0), tuple(jnp.zeros((TILE, D_HEAD), F32) for _ in lanes),
                tuple(jnp.zeros((TILE, 1), F32) for _ in lanes), jnp.float32(0.0))
        visited, accs, sums, _ = lax.while_loop(more, step, init)
        for hh, ls in enumerate(lanes):
            o_ref[:, ls] = accs[hh]
            c_ref[:, ls] = jnp.broadcast_to(sums[hh], (TILE, D_HEAD))
        n_ref[(b * groups + h) * nq + i] = visited.astype(F32)

    blk = pl.BlockSpec((TILE, width), lambda b, h, i: (b * nq + i, h))
    seq = pl.BlockSpec((lp, width), lambda b, h, i: (b, h))
    return pl.pallas_call(
        body, name="sb_fwd", grid=(bsz, groups, nq),
        in_specs=[blk, seq, seq], out_specs=[blk, blk, pl.BlockSpec(memory_space=pltpu.SMEM)],
        out_shape=[jax.ShapeDtypeStruct((t, D_MODEL), F32)] * 2 + [jax.ShapeDtypeStruct((bsz * groups * nq,), F32)],
        compiler_params=_params(("arbitrary", "arbitrary", "arbitrary")),
    )(qs, ks, vs)


def _sb_bwd(qs, ks, vs, ctot, visited, do, bsz, nq):
    t = qs.shape[0]
    lp = nq * TILE
    width = HEAD_GROUP * D_HEAD
    groups = N_HEADS // HEAD_GROUP
    lanes = [slice(hh * D_HEAD, (hh + 1) * D_HEAD) for hh in range(HEAD_GROUP)]

    def body(n_ref, q_ref, k_ref, v_ref, c_ref, do_ref, dq_ref, dk_ref, dv_ref, dk_acc, dv_acc):
        b, h, i = pl.program_id(0), pl.program_id(1), pl.program_id(2)

        @pl.when(i == 0)
        def _():
            dk_acc[...] = jnp.zeros_like(dk_acc)
            dv_acc[...] = jnp.zeros_like(dv_acc)

        q16 = [q_ref[:, ls].astype(_MXU) for ls in lanes]
        do16 = [do_ref[:, ls].astype(_MXU) for ls in lanes]
        q16_t = [q_ref[:, ls].astype(F32).T.astype(_MXU) for ls in lanes]
        do16_t = [do_ref[:, ls].astype(F32).T.astype(_MXU) for ls in lanes]
        totals = [c_ref[:, hh * D_HEAD:hh * D_HEAD + 1] for hh in range(HEAD_GROUP)]
        r, c = _iota2((TILE, TILE), 0), _iota2((TILE, TILE), 1)
        upto = (r <= c).astype(_MXU)
        before = (r < c).astype(_MXU)
        seen = n_ref[(b * (N_HEADS // SB_FWD_GROUP) + h * HEAD_GROUP // SB_FWD_GROUP) * nq + i]
        first = jnp.maximum(i + 1 - seen.astype(jnp.int32), 0)

        def step(j, carry):
            ks_ = pl.ds(pl.multiple_of(j * TILE, TILE), TILE)
            valid = _sb_valid(c - r, c, i, j)
            out = []
            for hh, ls in enumerate(lanes):
                dq, keep_pre, g_pre = carry[hh]
                k_blk, v_blk = k_ref[ks_, ls], v_ref[ks_, ls]
                keep, log_beta = _sb_logits(q16[hh], k_blk, valid)
                keep_upto = _mm01_right(keep, upto)
                a = jnp.exp(log_beta + (totals[hh] - keep_pre - keep_upto))
                da = lax.dot_general(do16[hh], v_blk.astype(_MXU), _NT, preferred_element_type=F32)
                g = a * da
                g_inside = _mm01_right(g, before)
                g_before = g_pre + g_inside
                beta = jnp.exp(log_beta)
                dz16 = (g * (1.0 - beta) - beta * g_before).astype(_MXU)
                dq = dq + jnp.dot(dz16, k_blk.astype(_MXU), preferred_element_type=F32)
                dk_acc[ls, ks_] += SB_SCALE * jnp.dot(q16_t[hh], dz16, preferred_element_type=F32)
                dv_acc[ls, ks_] += jnp.dot(do16_t[hh], a.astype(_MXU), preferred_element_type=F32)
                out.append((dq, keep_pre + keep_upto[:, TILE - 1:TILE],
                            g_pre + (g_inside[:, TILE - 1:TILE] + g[:, TILE - 1:TILE])))
            return tuple(out)

        zero_col = jnp.zeros((TILE, 1), F32)
        init = tuple((jnp.zeros((TILE, D_HEAD), F32), zero_col, zero_col) for _ in lanes)
        res = lax.fori_loop(first, i + 1, step, init)
        for hh, ls in enumerate(lanes):
            dq_ref[:, ls] = (SB_SCALE * res[hh][0]).astype(dq_ref.dtype)

        @pl.when(i == nq - 1)
        def _():
            dk_ref[...] = dk_acc[...].T.astype(dk_ref.dtype)
            dv_ref[...] = dv_acc[...].T.astype(dv_ref.dtype)

    blk = pl.BlockSpec((TILE, width), lambda b, h, i: (b * nq + i, h))
    seq = pl.BlockSpec((lp, width), lambda b, h, i: (b, h))
    return pl.pallas_call(
        body, name="sb_bwd", grid=(bsz, groups, nq),
        in_specs=[pl.BlockSpec(memory_space=pltpu.SMEM), blk, seq, seq, blk, blk], out_specs=[blk, seq, seq],
        out_shape=[jax.ShapeDtypeStruct((t, D_MODEL), _MXU)] * 3,
        scratch_shapes=[pltpu.VMEM((width, lp), F32)] * 2,
        compiler_params=_params(("arbitrary", "arbitrary", "arbitrary")),
    )(visited, qs, ks, vs, ctot, do)


def _head_norm(o, head_gain):
    outs, rs = [], []
    for h in range(N_HEADS):
        oh = o[:, h * D_HEAD:(h + 1) * D_HEAD]
        r = lax.rsqrt(jnp.mean(oh * oh, axis=-1, keepdims=True) + EPS)
        outs.append(oh * r)
        rs.append(r)
    return outs, rs


def _mix(o, gate, head_gain):
    if head_gain is None:
        on = o
    else:
        outs, _ = _head_norm(o, head_gain)
        on = jnp.concatenate([x * head_gain for x in outs], axis=1)
    return on, on * (gate * _sigmoid(gate))


def _out_fwd(o, gate, h_in, w_out, post_gain, head_gain, name):
    t = o.shape[0]
    tm = _row_tile(t, 2 * TILE)

    def body(o_ref, g_ref, h_ref, w_ref, pg_ref, hg_ref, ho_ref, u_ref):
        _, mix = _mix(o_ref[...], g_ref[...], hg_ref[...])
        u = jnp.dot(mix.astype(_MXU), w_ref[...], preferred_element_type=F32)
        u_ref[...] = u
        r = lax.rsqrt(jnp.mean(u * u, axis=-1, keepdims=True) + EPS)
        ho_ref[...] = h_ref[...] + u * r * pg_ref[...]

    row = pl.BlockSpec((tm, D_MODEL), lambda i: (i, 0))
    vec = pl.BlockSpec((1, D_MODEL), lambda i: (0, 0))
    return pl.pallas_call(
        body, name=name, grid=(t // tm,),
        in_specs=[row, row, row, pl.BlockSpec((D_MODEL, D_MODEL), lambda i: (0, 0)), vec,
                  pl.BlockSpec((1, D_HEAD), lambda i: (0, 0))],
        out_specs=[row, row], out_shape=[jax.ShapeDtypeStruct((t, D_MODEL), F32)] * 2,
        compiler_params=_params(("arbitrary",)),
    )(o, gate, h_in, w_out, post_gain, head_gain)


def _out_fwd_loss(o, gate, h_in, w_out, post_gain, target, nq, name):
    t = o.shape[0]

    def body(o_ref, g_ref, h_ref, w_ref, pg_ref, t_ref, dh_ref, u_ref, l_ref):
        i = pl.program_id(0)

        @pl.when(i == 0)
        def _():
            l_ref[...] = jnp.zeros_like(l_ref)

        _, mix = _mix(o_ref[...], g_ref[...], None)
        u = jnp.dot(mix.astype(_MXU), w_ref[...], preferred_element_type=F32)
        u_ref[...] = u

        @pl.when(i % nq == 0)
        def _():
            dh_ref[...] = jnp.zeros_like(dh_ref)

        @pl.when(i % nq != 0)
        def _():
            r = lax.rsqrt(jnp.mean(u * u, axis=-1, keepdims=True) + EPS)
            e = h_ref[...] + u * r * pg_ref[...] - t_ref[...]
            dh_ref[...] = e * (1.0 / D_MODEL)
            l_ref[...] += jnp.sum(e * e) * (0.5 / D_MODEL)

    row = pl.BlockSpec((TILE, D_MODEL), lambda i: (i, 0))
    vec = pl.BlockSpec((1, D_MODEL), lambda i: (0, 0))
    return pl.pallas_call(
        body, name=name, grid=(t // TILE,),
        in_specs=[row, row, row, pl.BlockSpec((D_MODEL, D_MODEL), lambda i: (0, 0)), vec,
                  pl.BlockSpec((TILE, D_MODEL), lambda i: ((i // nq) * (nq - 1) + jnp.maximum(i % nq - 1, 0), 0))],
        out_specs=[row, row, pl.BlockSpec((8, 128), lambda i: (0, 0))],
        out_shape=[jax.ShapeDtypeStruct((t, D_MODEL), F32)] * 2 + [jax.ShapeDtypeStruct((8, 128), F32)],
        compiler_params=_params(("arbitrary",)),
    )(o, gate, h_in, w_out, post_gain, target)


def _out_bwd(dh, u, o, gate, w_out, post_gain, head_gain, name, narrow_do):
    t = o.shape[0]
    tm = _row_tile(t, 2 * TILE)
    has_head = head_gain is not None

    def body(*refs):
        if has_head:
            dh_ref, u_ref, o_ref, g_ref, w_ref, pg_ref, hg_ref, do_ref, dg_ref, gw_ref, gw16_ref, gp_ref, gh_ref = refs
            hg = hg_ref[...]
        else:
            dh_ref, u_ref, o_ref, g_ref, w_ref, pg_ref, do_ref, dg_ref, gw_ref, gw16_ref, gp_ref = refs
            hg = None
        first = pl.program_id(0) == 0

        @pl.when(first)
        def _():
            gw_ref[...] = jnp.zeros_like(gw_ref)
            gp_ref[...] = jnp.zeros_like(gp_ref)
            if has_head:
                gh_ref[...] = jnp.zeros_like(gh_ref)

        dr, u, o, gate = dh_ref[...], u_ref[...], o_ref[...], g_ref[...]
        r = lax.rsqrt(jnp.mean(u * u, axis=-1, keepdims=True) + EPS)
        un = u * r
        gp_ref[...] += jnp.sum(dr * un, axis=0, keepdims=True)
        dun = dr * pg_ref[...]
        du = r * (dun - un * jnp.mean(dun * un, axis=-1, keepdims=True))
        on, mix = _mix(o, gate, hg)
        du16 = du.astype(_MXU)
        gw_ref[...] += lax.dot_general(mix.astype(_MXU), du16, _TN, preferred_element_type=F32)
        dmix = lax.dot_general(du16, w_ref[...], _NT, preferred_element_type=F32)
        sg = _sigmoid(gate)
        dg_ref[...] = (dmix * on * (sg * (1.0 + gate * (1.0 - sg)))).astype(dg_ref.dtype)
        don = dmix * (gate * sg)
        if has_head:
            outs, rs = _head_norm(o, hg)
            gh = jnp.zeros((1, D_HEAD), F32)
            cols = []
            for h in range(N_HEADS):
                dn = don[:, h * D_HEAD:(h + 1) * D_HEAD]
                gh = gh + jnp.sum(dn * outs[h], axis=0, keepdims=True)
                dnn = dn * hg
                cols.append(rs[h] * (dnn - outs[h] * jnp.mean(dnn * outs[h], axis=-1, keepdims=True)))
            gh_ref[...] += gh
            do_ref[...] = jnp.concatenate(cols, axis=1)
        else:
            do_ref[...] = don.astype(do_ref.dtype)

        @pl.when(pl.program_id(0) == pl.num_programs(0) - 1)
        def _():
            gw16_ref[...] = gw_ref[...].astype(_MXU)

    row = pl.BlockSpec((tm, D_MODEL), lambda i: (i, 0))
    vec = pl.BlockSpec((1, D_MODEL), lambda i: (0, 0))
    mat = pl.BlockSpec((D_MODEL, D_MODEL), lambda i: (0, 0))
    in_specs = [row, row, row, row, mat, vec]
    args = [dh, u, o, gate, w_out, post_gain]
    out_specs = [row, row, mat, mat, vec]
    out_shape = [jax.ShapeDtypeStruct((t, D_MODEL), _MXU if narrow_do else F32),
                 jax.ShapeDtypeStruct((t, D_MODEL), _MXU)] + [jax.ShapeDtypeStruct((D_MODEL, D_MODEL), F32),
                                                                  jax.ShapeDtypeStruct((D_MODEL, D_MODEL), _MXU),
                                                                  jax.ShapeDtypeStruct((1, D_MODEL), F32)]
    if has_head:
        in_specs.append(pl.BlockSpec((1, D_HEAD), lambda i: (0, 0)))
        args.append(head_gain)
        out_specs.append(pl.BlockSpec((1, D_HEAD), lambda i: (0, 0)))
        out_shape.append(jax.ShapeDtypeStruct((1, D_HEAD), F32))
    return pl.pallas_call(
        body, name=name, grid=(t // tm,), in_specs=in_specs, out_specs=out_specs, out_shape=out_shape,
        compiler_params=_params(("arbitrary",)),
    )(*args)


def _proj_bwd(ds, w4, h_in, gain, dh_out, name, outgoing=(), nq=None):
    t = h_in.shape[0]
    tm = _row_tile(t, 2 * TILE) if nq is None else TILE
    n_out = len(outgoing)
    n_dh = 1 if nq is None else 2

    def body(d0, d1, d2, d3, w_ref, h_ref, g_ref, dho_ref, *rest):
        send, dh_refs, gg_ref, rest = rest[:n_out], rest[n_out:n_out + n_dh], rest[n_out + n_dh], rest[n_out + n_dh + 1:]
        landed, sems = rest[:n_out], rest[n_out:]
        if n_out:
            first, last = _grid_ends(1)
            start, finish = _chip_exchange(send, landed, sems, slotted=True)
            pl.when(first)(start)

        @pl.when(pl.program_id(0) == 0)
        def _():
            gg_ref[...] = jnp.zeros_like(gg_ref)

        dy = jnp.zeros((tm, D_MODEL), F32)
        for j, d in enumerate((d0, d1, d2, d3)):
            dy = dy + lax.dot_general(d[...].astype(_MXU), w_ref[j], _NT, preferred_element_type=F32)
        x = h_ref[...]
        r = lax.rsqrt(jnp.mean(x * x, axis=-1, keepdims=True) + EPS)
        xn = x * r
        gg_ref[...] += jnp.sum(dy * xn, axis=0, keepdims=True)
        dxn = dy * g_ref[...]
        dh = dho_ref[...] + r * (dxn - xn * jnp.mean(dxn * xn, axis=-1, keepdims=True))
        if nq is None:
            dh_refs[0][...] = dh
        else:
            in_front = pl.program_id(0) % nq == 0

            @pl.when(in_front)
            def _():
                dh_refs[1][...] = dh

            @pl.when(jnp.logical_not(in_front))
            def _():
                dh_refs[0][...] = dh
        if n_out:
            pl.when(last)(finish)

    row = pl.BlockSpec((tm, D_MODEL), lambda i: (i, 0))
    vec = pl.BlockSpec((1, D_MODEL), lambda i: (0, 0))
    hbm = pl.BlockSpec(memory_space=pl.ANY)
    if nq is None:
        dh_specs, dh_shapes = [row], [jax.ShapeDtypeStruct((t, D_MODEL), F32)]
    else:
        dh_specs = [pl.BlockSpec((TILE, D_MODEL), lambda i: ((i // nq) * (nq - 1) + jnp.maximum(i % nq - 1, 0), 0)),
                    pl.BlockSpec((TILE, D_MODEL), lambda i: (i // nq, 0))]
        dh_shapes = [jax.ShapeDtypeStruct((t // nq * (nq - 1), D_MODEL), F32),
                     jax.ShapeDtypeStruct((t // nq, D_MODEL), F32)]
    return pl.pallas_call(
        body, name=name, grid=(t // tm,),
        in_specs=[row] * 4 + [pl.BlockSpec((4, D_MODEL, D_MODEL), lambda i: (0, 0, 0)), row, vec, row] + [hbm] * n_out,
        out_specs=dh_specs + [vec] + [hbm] * n_out,
        out_shape=dh_shapes + [jax.ShapeDtypeStruct((1, D_MODEL), F32)]
        + [jax.ShapeDtypeStruct(a.shape, a.dtype) for a in outgoing],
        scratch_shapes=_exchange_scratch(n_out) if n_out else [],
        compiler_params=_params(("arbitrary",)),
    )(*ds, w4, h_in, gain, dh_out, *outgoing)


def _weight_grad(y, d, name):
    t = y.shape[0]
    tk = _row_tile(t, t // 4)

    def body(y_ref, d_ref, g_ref, g16_ref):
        @pl.when(pl.program_id(0) == 0)
        def _():
            g_ref[...] = jnp.zeros_like(g_ref)

        g_ref[...] += lax.dot_general(y_ref[...], d_ref[...].astype(_MXU), _TN, preferred_element_type=F32)

        @pl.when(pl.program_id(0) == pl.num_programs(0) - 1)
        def _():
            g16_ref[...] = g_ref[...].astype(_MXU)

    row = pl.BlockSpec((tk, D_MODEL), lambda i: (i, 0))
    mat = pl.BlockSpec((D_MODEL, D_MODEL), lambda i: (0, 0))
    return pl.pallas_call(
        body, name=name, grid=(t // tk,), in_specs=[row, row], out_specs=[mat, mat],
        out_shape=[jax.ShapeDtypeStruct((D_MODEL, D_MODEL), F32), jax.ShapeDtypeStruct((D_MODEL, D_MODEL), _MXU)],
        compiler_params=_params(("arbitrary",)),
    )(y, d)


def _local_step(x, target, meta, pre_norm, post_norm, lam, head_gain, hw_in, shards):
    bsz, seq, _ = x.shape
    nq = seq // TILE + 1
    nb = nq * (TILE // BLOCK)
    lp = nq * TILE
    t = bsz * lp
    d4 = D_MODEL // N_CHIPS
    front = jnp.concatenate([jnp.zeros((N_PAD, D_MODEL), F32), meta], axis=0)
    h0 = jnp.concatenate([jnp.broadcast_to(front[None], (bsz, TILE, D_MODEL)), x], axis=1).reshape(t, D_MODEL)
    pre0, pre1, post0, post1 = pre_norm[0:1], pre_norm[1:2], post_norm[0:1], post_norm[1:2]

    y0, q0, f0, v0, g0 = _norm_proj(h0, pre0, hw_in, "norm_proj_hgrn", (False,) * 4)
    o0, sst, sym, sw_in, sw_out, hw_out = _hgrn_fwd(q0, f0, v0, lam, bsz, nb, shards)
    sw_out, hw_out = sw_out.reshape(D_MODEL, D_MODEL), hw_out.reshape(D_MODEL, D_MODEL)
    h1, u0 = _out_fwd(o0, g0, h0, hw_out, post0, head_gain, "out_fwd_hgrn")
    y1, q1, k1, v1, g1 = _norm_proj(h1, pre1, sw_in, "norm_proj_sb", (True, True, True, False))
    o1, ctot, visited = _sb_fwd(q1, k1, v1, bsz, nq)
    dh2, u1, loss_blk = _out_fwd_loss(o1, g1, h1, sw_out, post1, target.reshape(bsz * seq, D_MODEL), nq, "out_fwd_sb")

    do1, dg1, g_sw_out, g_sw_out16, g_post1 = _out_bwd(dh2, u1, o1, g1, sw_out, post1, None, "out_bwd_sb", True)
    dq1, dk1, dv1 = _sb_bwd(q1, k1, v1, ctot, visited, do1, bsz, nq)
    ds1 = (dq1, dk1, dv1, dg1)
    dh1, g_pre1 = _proj_bwd(ds1, sw_in, h1, pre1, dh2, "proj_bwd_sb")
    g_sw_in = [_weight_grad(y1, d, "wgrad_sb_%d" % j) for j, d in enumerate(ds1)]

    do0, dg0, g_hw_out, g_hw_out16, g_post0, g_head = _out_bwd(dh1, u0, o0, g0, hw_out, post0, head_gain, "out_bwd_hgrn",
                                                               False)
    ready = (jnp.stack([g16 for _, g16 in g_sw_in]), g_sw_out16.reshape(N_CHIPS, d4, D_MODEL),
             g_hw_out16.reshape(N_CHIPS, d4, D_MODEL))
    dq0, df0, dv0, dlam, land_sw_in, land_sw_out, land_hw_out = _hgrn_bwd(q0, f0, v0, lam, sst, sym, do0, bsz, nb, ready)
    ds0 = (dq0, df0, dv0, dg0)
    g_hw_in = [_weight_grad(y0, d, "wgrad_hgrn_%d" % j) for j, d in enumerate(ds0)]
    last = (jnp.stack([g16 for _, g16 in g_hw_in]),)
    grad_x, dh_front, g_pre0, land_hw_in = _proj_bwd(ds0, hw_in, h0, pre0, dh1, "proj_bwd_hgrn", last, nq)

    grad_x = grad_x.reshape(bsz, seq, D_MODEL)
    g_meta = jnp.sum(dh_front.reshape(bsz, TILE, D_MODEL)[:, N_PAD:, :], axis=0)
    g_lam = jnp.sum(dlam, axis=0)
    small = jnp.concatenate([g_pre0, g_pre1, g_post0, g_post1, g_lam, g_lam,
                             jnp.pad(g_head, ((0, 0), (0, D_MODEL - D_HEAD))), g_meta,
                             jnp.pad(loss_blk[0:1], ((0, 0), (0, D_MODEL - loss_blk.shape[1])))], axis=0)
    rows4 = lambda g: [g[j * d4:(j + 1) * d4] for j in range(N_CHIPS)]
    large = dict(hw_in=(land_hw_in, [g for g, _ in g_hw_in]), sw_in=(land_sw_in, [g for g, _ in g_sw_in]),
                 hw_out=(land_hw_out, rows4(g_hw_out)), sw_out=(land_sw_out, rows4(g_sw_out)))
    return grad_x, small, large


def _prep_weights(hw_in, sw_in, hw_out, sw_out, meta):
    def body(hi_ref, si_ref, ho_ref, so_ref, m_ref, ghi, gm, si16, so16, ho16, far_send, far_recv, near_send, near_recv):
        x, y, c = _place()
        me = 2 * x + y
        ghi[me] = hi_ref[0].astype(_MXU)
        gm[me] = m_ref[...]
        si16[...] = si_ref[0].astype(_MXU)
        so16[...] = so_ref[0].astype(_MXU)
        ho16[...] = ho_ref[0].astype(_MXU)
        outs = (ghi, gm)
        n = len(outs)
        peers = [(1 - x, y), (x, 1 - y), (1 - x, 1 - y)]

        def half(a, slot, which):
            rows = outs[a].shape[1] // 2
            return outs[a].at[slot, pl.ds(which * rows, rows), :]

        def far(r, a, slot):
            px, py = peers[r]
            return pltpu.make_async_remote_copy(
                src_ref=half(a, slot, c), dst_ref=half(a, slot, c), send_sem=far_send.at[r * n + a],
                recv_sem=far_recv.at[r * n + a], device_id=(px, py, c), device_id_type=MESH)

        def near(r, a, which):
            px, py = peers[r]
            return pltpu.make_async_remote_copy(
                src_ref=half(a, 2 * px + py, which), dst_ref=half(a, 2 * px + py, which),
                send_sem=near_send.at[r * n + a], recv_sem=near_recv.at[r * n + a],
                device_id=(x, y, 1 - c), device_id_type=MESH)

        for r in range(3):
            for a in range(n):
                far(r, a, me).start()
        for r, (px, py) in enumerate(peers):
            for a in range(n):
                far(r, a, 2 * px + py).wait_recv()
                near(r, a, c).start()
        for r in range(3):
            for a in range(n):
                near(r, a, 1 - c).wait_recv()
        for r in range(3):
            for a in range(n):
                far(r, a, me).wait_send()
                near(r, a, c).wait_send()

    d4 = D_MODEL // N_CHIPS
    vm = pl.BlockSpec(memory_space=pltpu.VMEM)
    return pl.pallas_call(
        body, name="prep_weights",
        in_specs=[vm] * 5, out_specs=[vm] * 5,
        out_shape=[jax.ShapeDtypeStruct((N_CHIPS, D_MODEL, D_MODEL), _MXU), jax.ShapeDtypeStruct((N_CHIPS, N_META, d4), F32),
                   jax.ShapeDtypeStruct((D_MODEL, D_MODEL), _MXU), jax.ShapeDtypeStruct((d4, D_MODEL), _MXU),
                   jax.ShapeDtypeStruct((d4, D_MODEL), _MXU)],
        scratch_shapes=[pltpu.SemaphoreType.DMA((6,))] * 4,
        compiler_params=pltpu.CompilerParams(vmem_limit_bytes=VMEM_LIMIT),
    )(hw_in, sw_in, hw_out, sw_out, meta)


def _scatter_small(small):
    def body(sm, lsm, send_sems, recv_sems, local_sem):
        x, y, c = _place()
        mine = 4 * x + 2 * y + c
        local = pltpu.make_async_copy(sm, lsm.at[mine], local_sem)
        local.start()

        def copy(rel, src_dev, to):
            return pltpu.make_async_remote_copy(
                src_ref=sm, dst_ref=lsm.at[src_dev], send_sem=send_sems.at[rel - 1], recv_sem=recv_sems.at[rel - 1],
                device_id=to, device_id_type=MESH)

        flip = lambda bit, v: 1 - v if bit else v
        rels = [(rel, flip(rel & 4, x), flip(rel & 2, y), flip(rel & 1, c)) for rel in range(1, N_DEV)]
        sends = [copy(rel, mine, (px, py, pc)) for rel, px, py, pc in rels]
        for cp in sends:
            cp.start()
        for rel, px, py, pc in rels:
            copy(rel, 4 * px + 2 * py + pc, (px, py, pc)).wait_recv()
        for cp in sends:
            cp.wait_send()
        local.wait()

    hbm = pl.BlockSpec(memory_space=pl.ANY)
    return pl.pallas_call(
        body, name="scatter_small", in_specs=[hbm], out_specs=hbm,
        out_shape=jax.ShapeDtypeStruct((N_DEV, SMALL_ROWS, D_MODEL), F32),
        scratch_shapes=[pltpu.SemaphoreType.DMA((N_DEV - 1,)), pltpu.SemaphoreType.DMA((N_DEV - 1,)),
                        pltpu.SemaphoreType.DMA(())],
    )(small)


def _sum_slots(landed, own, me, name):
    n, rows, _ = landed.shape
    tm = rows if rows < 256 else 256

    def body(me_ref, l_ref, o0, o1, o2, o3, out_ref):
        acc = None
        for k, o in enumerate((o0, o1, o2, o3)):
            term = jnp.where(me_ref[0] == k, o[...], l_ref[k].astype(F32))
            acc = term if acc is None else acc + term
        out_ref[...] = acc

    blk = pl.BlockSpec((tm, D_MODEL), lambda i: (i, 0))
    return pl.pallas_call(
        body, name=name, grid=(rows // tm,),
        in_specs=[pl.BlockSpec(memory_space=pltpu.SMEM), pl.BlockSpec((n, tm, D_MODEL), lambda i: (0, i, 0))] + [blk] * 4,
        out_specs=blk, out_shape=jax.ShapeDtypeStruct((rows, D_MODEL), F32),
        compiler_params=_params(("arbitrary",)),
    )(me, landed, *own)


def _swap_with_sibling(parts):
    def body(a0, a1, a2, a3, b0, b1, b2, b3, send_sems, recv_sems):
        x, y, c = _place()
        copies = [pltpu.make_async_remote_copy(src_ref=s, dst_ref=d, send_sem=send_sems.at[a], recv_sem=recv_sems.at[a],
                                               device_id=(x, y, 1 - c), device_id_type=MESH)
                  for a, (s, d) in enumerate(zip((a0, a1, a2, a3), (b0, b1, b2, b3)))]
        for cp in copies:
            cp.start()
        for cp in copies:
            cp.wait()

    hbm = pl.BlockSpec(memory_space=pl.ANY)
    return pl.pallas_call(
        body, name="swap_with_sibling", in_specs=[hbm] * 4, out_specs=[hbm] * 4,
        out_shape=[jax.ShapeDtypeStruct(p.shape, F32) for p in parts],
        scratch_shapes=[pltpu.SemaphoreType.DMA((4,)), pltpu.SemaphoreType.DMA((4,))],
    )(*parts)


def _adamw_math(w, g, m, v):
    m = ADAM_B1 * m + (1.0 - ADAM_B1) * g
    v = ADAM_B2 * v + (1.0 - ADAM_B2) * (g * g)
    m_hat = m / (1.0 - ADAM_B1 ** ADAM_STEP)
    v_hat = v / (1.0 - ADAM_B2 ** ADAM_STEP)
    delta = -ADAM_LR * (m_hat / (jnp.sqrt(v_hat) + ADAM_EPS) + ADAM_WD * w)
    return delta, m, v


def _adamw(w, g_parts, m, v, name):
    rows, cols = w.shape
    tm = rows if rows < 256 else 256
    n = len(g_parts)

    def body(*refs):
        w_ref, m_ref, v_ref = refs[n:n + 3]
        g_ref, d_ref, nm_ref, nv_ref = refs[n + 3:]
        g = refs[0][...]
        for p in refs[1:n]:
            g = g + p[...]
        g_ref[...] = g
        d_ref[...], nm_ref[...], nv_ref[...] = _adamw_math(w_ref[...], g, m_ref[...], v_ref[...])

    blk = pl.BlockSpec((tm, cols), lambda i: (i, 0))
    return pl.pallas_call(
        body, name=name, grid=(rows // tm,), in_specs=[blk] * (n + 3), out_specs=[blk] * 4,
        out_shape=[jax.ShapeDtypeStruct((rows, cols), F32)] * 4,
        compiler_params=_params(("arbitrary",)),
    )(*g_parts, w, m, v)


def _lam_of(hgrn_lb):
    def body(lb_ref, o_ref):
        lb = lb_ref[...]
        e = jnp.exp(lb - jnp.max(lb, axis=0, keepdims=True))
        o_ref[...] = e[0:1, :] / jnp.sum(e, axis=0, keepdims=True)

    return pl.pallas_call(body, name="lam_of", out_shape=jax.ShapeDtypeStruct((1, D_MODEL), F32))(hgrn_lb)


def _small_grads(land_small, lam):
    def body(l_ref, lam_ref, o_ref):
        acc = l_ref[0]
        for k in range(1, N_DEV):
            acc = acc + l_ref[k]
        p = lam_ref[...]
        slope = p * (1.0 - p)
        row = _iota2((SMALL_ROWS, D_MODEL), 0)
        o_ref[...] = acc * jnp.where(row == 4, slope, jnp.where(row == 5, -slope, 1.0))

    return pl.pallas_call(body, name="small_grads",
                          out_shape=jax.ShapeDtypeStruct((SMALL_ROWS, D_MODEL), F32))(land_small, lam)


def kernel(x, meta_tokens, pre_norm, post_norm, hgrn_w_in, hgrn_lb, hgrn_out_norm, hgrn_w_out, sb_w_in, sb_w_out, loss_target, m_meta_tokens, m_pre_norm, m_post_norm, m_hgrn_w_in, m_hgrn_lb, m_hgrn_out_norm, m_hgrn_w_out, m_sb_w_in, m_sb_w_out, v_meta_tokens, v_pre_norm, v_post_norm, v_hgrn_w_in, v_hgrn_lb, v_hgrn_out_norm, v_hgrn_w_out, v_sb_w_in, v_sb_w_out):
    d4 = D_MODEL // N_CHIPS
    chip = 2 * lax.axis_index("x") + lax.axis_index("y")
    hw_in, meta4, sw_in16, sw_out16, hw_out16 = _prep_weights(hgrn_w_in, sb_w_in, hgrn_w_out, sb_w_out, meta_tokens)
    meta = meta4.transpose(1, 0, 2).reshape(N_META, D_MODEL)
    lam = _lam_of(hgrn_lb)
    grad_x, small, large = _local_step(
        x, loss_target, meta, pre_norm, post_norm, lam, hgrn_out_norm,
        hw_in, (sw_in16, sw_out16, hw_out16))

    me = jnp.reshape(chip, (1,)).astype(jnp.int32)
    parts = [_sum_slots(*large[n], me, "sum_" + n) for n in ("hw_in", "sw_in", "hw_out", "sw_out")]
    sib = _swap_with_sibling(parts)
    small = _small_grads(_scatter_small(small), lam)
    loss = small[SMALL_ROWS - 1, 0]

    res = {}
    res["hgrn_w_in"] = _adamw(hgrn_w_in[0], [parts[0], sib[0]], m_hgrn_w_in[0], v_hgrn_w_in[0], "adamw_hw_in")
    res["sb_w_in"] = _adamw(sb_w_in[0], [parts[1], sib[1]], m_sb_w_in[0], v_sb_w_in[0], "adamw_sw_in")
    res["hgrn_w_out"] = _adamw(hgrn_w_out[0], [parts[2], sib[2]], m_hgrn_w_out[0], v_hgrn_w_out[0], "adamw_hw_out")
    res["sb_w_out"] = _adamw(sb_w_out[0], [parts[3], sib[3]], m_sb_w_out[0], v_sb_w_out[0], "adamw_sw_out")
    res["pre_norm"] = _adamw(pre_norm, [small[0:2]], m_pre_norm, v_pre_norm, "adamw_pre")
    res["post_norm"] = _adamw(post_norm, [small[2:4]], m_post_norm, v_post_norm, "adamw_post")
    res["hgrn_lb"] = _adamw(hgrn_lb, [small[4:6]], m_hgrn_lb, v_hgrn_lb, "adamw_lb")
    res["hgrn_out_norm"] = _adamw(hgrn_out_norm, [small[6:7, :D_HEAD]], m_hgrn_out_norm, v_hgrn_out_norm, "adamw_head")
    g_meta = lax.dynamic_slice_in_dim(small[7:7 + N_META], chip * d4, d4, axis=1)
    res["meta_tokens"] = _adamw(meta_tokens, [g_meta], m_meta_tokens, v_meta_tokens, "adamw_meta")
    for n in ("hgrn_w_in", "hgrn_w_out", "sb_w_in", "sb_w_out"):
        res[n] = tuple(a[None] for a in res[n])
    order = ("meta_tokens", "pre_norm", "post_norm", "hgrn_w_in", "hgrn_lb", "hgrn_out_norm", "hgrn_w_out",
             "sb_w_in", "sb_w_out")
    return (loss, grad_x, *[res[n][0] for n in order], *[res[n][1] for n in order],
            *[res[n][2] for n in order], *[res[n][3] for n in order])
```

```python
import functools

import jax
import numpy as np
import jax.numpy as jnp
from jax import lax
from jax.experimental import pallas as pl
from jax.experimental.pallas import tpu as pltpu

F32 = jnp.float32
_MXU = jnp.bfloat16

D_MODEL = 1024
N_HEADS = 8
D_HEAD = 128
BLOCK = 128
N_META = 16
TILE = 256
N_PAD = TILE - N_META
UNDERFLOW = -105.0
EPS = 1e-6
SB_SCALE = D_HEAD ** -0.5
SOFTPLUS_LINEAR = 20.0
MASKED = -1e30
ADAM_LR, ADAM_B1, ADAM_B2, ADAM_EPS, ADAM_WD, ADAM_STEP = 0.001, 0.9, 0.999, 1e-08, 0.01, 10
N_CHIPS = 4
N_DEV = 8
SMALL_ROWS = 24
VMEM_LIMIT = 56 * 1024 * 1024
MESH = pl.DeviceIdType.MESH

_NT = (((1,), (1,)), ((), ()))
_TN = (((0,), (0,)), ((), ()))


def _mm(a, b):
    return jnp.dot(a.astype(_MXU), b.astype(_MXU), preferred_element_type=F32)


def _mm_nt(a, b):
    return lax.dot_general(a.astype(_MXU), b.astype(_MXU), _NT, preferred_element_type=F32)


def _mm_tn(a, b):
    return lax.dot_general(a.astype(_MXU), b.astype(_MXU), _TN, preferred_element_type=F32)


def _split2(x):
    hi = x.astype(_MXU)
    return hi, (x - hi.astype(F32)).astype(_MXU)


def _mm_s(a16, state):
    hi, lo = _split2(state)
    return jnp.dot(a16, hi, preferred_element_type=F32) + jnp.dot(a16, lo, preferred_element_type=F32)


def _mm_nt_s(a16, state):
    hi, lo = _split2(state)
    return (lax.dot_general(a16, hi, _NT, preferred_element_type=F32)
            + lax.dot_general(a16, lo, _NT, preferred_element_type=F32))


def _mm01_right(x, m01):
    return jnp.dot(x.astype(_MXU), m01, preferred_element_type=F32)


def _mm01_left(m01, x):
    hi, lo = _split2(x)
    return jnp.dot(m01, hi, preferred_element_type=F32) + jnp.dot(m01, lo, preferred_element_type=F32)


def _iota2(shape, dim):
    return lax.broadcasted_iota(jnp.int32, shape, dim)


def _row_tile(total, pref):
    t = pref
    while total % t:
        t -= BLOCK
    return t


def _params(sem, limit=VMEM_LIMIT):
    return pltpu.CompilerParams(dimension_semantics=sem, vmem_limit_bytes=limit)


def _sigmoid(x):
    return 1.0 / (1.0 + jnp.exp(-x))


def _grid_ends(ndim):
    first, last = True, True
    for d in range(ndim):
        first = first & (pl.program_id(d) == 0)
        last = last & (pl.program_id(d) == pl.num_programs(d) - 1)
    return first, last


def _place():
    return lax.axis_index("x"), lax.axis_index("y"), lax.axis_index("c")


def _exchange_scratch(n):
    return [pltpu.SemaphoreType.DMA((3 * n,)), pltpu.SemaphoreType.DMA((3 * n,)), pltpu.SemaphoreType.DMA((n,))]


def _chip_exchange(srcs, dsts, sems, slotted):
    send_sems, recv_sems, local_sems = sems
    x, y, c = _place()
    me = 2 * x + y
    peers = [(1 - x, y), (x, 1 - y), (1 - x, 1 - y)]
    n = len(dsts)

    def remote(r, a, sending):
        px, py = peers[r]
        p = 2 * px + py
        return pltpu.make_async_remote_copy(
            src_ref=srcs[a].at[p] if slotted else srcs[a], dst_ref=dsts[a].at[me if sending else p],
            send_sem=send_sems.at[r * n + a], recv_sem=recv_sems.at[r * n + a],
            device_id=(px, py, c), device_id_type=MESH)

    def local(a):
        return pltpu.make_async_copy(srcs[a].at[me] if slotted else srcs[a], dsts[a].at[me], local_sems.at[a])

    def start():
        for a in range(n):
            local(a).start()
        for r in range(3):
            for a in range(n):
                remote(r, a, True).start()

    def finish():
        for r in range(3):
            for a in range(n):
                remote(r, a, False).wait_recv()
        for r in range(3):
            for a in range(n):
                remote(r, a, True).wait_send()
        for a in range(n):
            local(a).wait()

    return start, finish


def _norm_proj(h, gain, w4, name, narrow):
    t = h.shape[0]
    tm = _row_tile(t, 2 * TILE)

    def body(h_ref, g_ref, w_ref, y_ref, s0, s1, s2, s3):
        x = h_ref[...]
        r = lax.rsqrt(jnp.mean(x * x, axis=-1, keepdims=True) + EPS)
        y = (x * r * g_ref[...]).astype(_MXU)
        y_ref[...] = y
        for j, s in enumerate((s0, s1, s2, s3)):
            s[...] = jnp.dot(y, w_ref[j], preferred_element_type=F32).astype(s.dtype)

    row = pl.BlockSpec((tm, D_MODEL), lambda i: (i, 0))
    return pl.pallas_call(
        body, name=name, grid=(t // tm,),
        in_specs=[row, pl.BlockSpec((1, D_MODEL), lambda i: (0, 0)),
                  pl.BlockSpec((4, D_MODEL, D_MODEL), lambda i: (0, 0, 0))],
        out_specs=[row] * 5,
        out_shape=[jax.ShapeDtypeStruct((t, D_MODEL), _MXU)]
        + [jax.ShapeDtypeStruct((t, D_MODEL), _MXU if n else F32) for n in narrow],
        compiler_params=_params(("arbitrary",)),
    )(h, gain, w4)


LEVELS = (64, 32, 16, 8, 4, 2, 1)
HEAD_GROUP = 2
SB_FWD_GROUP = 4


def _hgrn_tables():
    r = np.arange(BLOCK)
    mats = [r[None, :] <= r[:, None]]
    x = r[:, None] ^ r[None, :]
    lv = np.full((BLOCK, BLOCK), len(LEVELS), np.int32)
    for i, m in enumerate(LEVELS):
        lv[(x >= m) & (x < 2 * m)] = i
    return jnp.asarray(np.concatenate(mats, 0).astype(np.float32), dtype=_MXU), jnp.asarray(lv)


def _hgrn_exponents(g, sums):
    b = _mm01_left(sums, g)
    row = _iota2((BLOCK, D_HEAD), 0)
    out = []
    for m in LEVELS:
        is_q = (row & m) != 0
        if m >= 4:
            grp = b.reshape(BLOCK // (2 * m), 2 * m, D_HEAD)
            ref = jnp.broadcast_to(grp[:, m - 1:m, :], grp.shape).reshape(BLOCK, D_HEAD)
            d = b - ref
            out.append(jnp.where(is_q, d, -d))
        elif m == 2:
            below, above = pltpu.roll(g, 1, axis=0), pltpu.roll(g, BLOCK - 1, axis=0)
            low = row & 3
            out.append(jnp.where(low == 3, g + below, jnp.where(low == 2, g, jnp.where(low == 0, above, 0.0))))
        else:
            out.append(jnp.where(is_q, g, 0.0))
    return b, out


def _hgrn_gates(fz, lam, chunk):
    pos = chunk * BLOCK + _iota2((BLOCK, D_HEAD), 0)
    live = pos >= N_PAD
    sg = _sigmoid(fz)
    f = lam + (1.0 - lam) * sg
    g = jnp.where(live, jnp.log(f), 0.0)
    k = jnp.where(live, (1.0 - lam) * (1.0 - sg), 0.0)
    return sg, f, g, k, live


def _level_operand(q, k, exponent, m):
    decay = jnp.exp(exponent)
    is_q = (_iota2((BLOCK, D_HEAD), 0) & m) != 0
    return is_q, decay, (jnp.where(is_q, q, k) * decay).astype(_MXU)


def _hgrn_fwd(qs, fs, vs, lam, bsz, nb, shards):
    t = qs.shape[0]
    sums, levels = _hgrn_tables()
    width = HEAD_GROUP * D_HEAD

    n_sh = len(shards)

    def body(q_ref, f_ref, v_ref, lam_ref, sums_ref, lv_ref, *rest):
        own, (o_ref, sst_ref, sym_ref), rest = rest[:n_sh], rest[n_sh:n_sh + 3], rest[n_sh + 3:]
        gathered, st_scr, sems = rest[:n_sh], rest[n_sh], rest[n_sh + 1:]
        n = pl.program_id(2)
        first, last = _grid_ends(3)
        start, finish = _chip_exchange(own, gathered, sems, slotted=False)
        pl.when(first)(start)

        @pl.when(n == 0)
        def _():
            st_scr[...] = jnp.zeros_like(st_scr)

        lv = lv_ref[...]
        r, c = _iota2((BLOCK, BLOCK), 0), _iota2((BLOCK, BLOCK), 1)
        for hh in range(HEAD_GROUP):
            ls = slice(hh * D_HEAD, (hh + 1) * D_HEAD)
            st = st_scr[hh]
            sst_ref[0, hh, 0] = st
            q, v = q_ref[:, ls], v_ref[:, ls]
            _, _, g, k, _ = _hgrn_gates(f_ref[:, ls], lam_ref[:, ls], n)
            b, exps = _hgrn_exponents(g, sums_ref[...])
            sym = jnp.zeros((BLOCK, BLOCK), F32)
            for li, m in enumerate(LEVELS):
                _, _, x16 = _level_operand(q, k, exps[li], m)
                sym = jnp.where(lv == li, lax.dot_general(x16, x16, _NT, preferred_element_type=F32), sym)
            sym = jnp.where(c == r, jnp.sum(q * k, axis=1, keepdims=True), sym).astype(_MXU)
            sym_ref[0, hh, 0] = sym
            o_ref[:, ls] = _mm_nt(q * jnp.exp(b), st) + _mm(jnp.where(c <= r, sym, 0), v)
            b_end = b[BLOCK - 1:BLOCK, :]
            st_scr[hh] = st * jnp.exp(b_end) + _mm_tn(v, k * jnp.exp(b_end - b))
        pl.when(last)(finish)

    blk = pl.BlockSpec((BLOCK, width), lambda b, h, n: (b * nb + n, h))
    hbm = pl.BlockSpec(memory_space=pl.ANY)
    return pl.pallas_call(
        body, name="hgrn_fwd", grid=(bsz, N_HEADS // HEAD_GROUP, nb),
        in_specs=[blk, blk, blk, pl.BlockSpec((1, width), lambda b, h, n: (0, h)),
                  pl.BlockSpec(sums.shape, lambda b, h, n: (0, 0)), pl.BlockSpec(levels.shape, lambda b, h, n: (0, 0))]
        + [hbm] * n_sh,
        out_specs=[blk] + [pl.BlockSpec((1, HEAD_GROUP, 1, D_HEAD, D_HEAD), lambda b, h, n: (b, h, n, 0, 0))] * 2
        + [hbm] * n_sh,
        out_shape=[jax.ShapeDtypeStruct((t, D_MODEL), F32),
                   jax.ShapeDtypeStruct((bsz, N_HEADS, nb, D_HEAD, D_HEAD), F32),
                   jax.ShapeDtypeStruct((bsz, N_HEADS, nb, D_HEAD, D_HEAD), _MXU)]
        + [jax.ShapeDtypeStruct((N_CHIPS,) + a.shape, a.dtype) for a in shards],
        scratch_shapes=[pltpu.VMEM((HEAD_GROUP, D_HEAD, D_HEAD), F32)] + _exchange_scratch(n_sh),
        compiler_params=_params(("arbitrary", "arbitrary", "arbitrary")),
    )(qs, fs, vs, lam, sums, levels, *shards)


def _hgrn_bwd(qs, fs, vs, lam, sst, sym, do, bsz, nb, outgoing):
    t = qs.shape[0]
    sums, levels = _hgrn_tables()
    width = HEAD_GROUP * D_HEAD

    n_out = len(outgoing)

    def body(q_ref, f_ref, v_ref, lam_ref, sst_ref, sym_ref, do_ref, sums_ref, lv_ref, *rest):
        send, (dq_ref, df_ref, dv_ref, dlam_ref), rest = rest[:n_out], rest[n_out:n_out + 4], rest[n_out + 4:]
        landed, dst_scr, gsum_scr, sems = rest[:n_out], rest[n_out], rest[n_out + 1], rest[n_out + 2:]
        n = pl.program_id(2)
        chunk = nb - 1 - n
        first, last = _grid_ends(3)
        start, finish = _chip_exchange(send, landed, sems, slotted=True)
        pl.when(first)(start)

        @pl.when(n == 0)
        def _():
            dst_scr[...] = jnp.zeros_like(dst_scr)
            gsum_scr[...] = jnp.zeros_like(gsum_scr)
            dlam_ref[...] = jnp.zeros_like(dlam_ref)

        lv = lv_ref[...]
        r, c = _iota2((BLOCK, BLOCK), 0), _iota2((BLOCK, BLOCK), 1)
        for hh in range(HEAD_GROUP):
            ls = slice(hh * D_HEAD, (hh + 1) * D_HEAD)
            lam = lam_ref[:, ls]
            q, v, do = q_ref[:, ls], v_ref[:, ls], do_ref[:, ls]
            sg, f, g, k, live = _hgrn_gates(f_ref[:, ls], lam, chunk)
            b, exps = _hgrn_exponents(g, sums_ref[...])
            do16, v16 = do.astype(_MXU), v.astype(_MXU)
            da = lax.dot_general(do16, v16, _NT, preferred_element_type=F32)
            da_sym = jnp.where(c < r, da, da.T)
            dq = jnp.zeros((BLOCK, D_HEAD), F32)
            dqk = jnp.zeros((BLOCK, D_HEAD), F32)
            db_q = jnp.zeros((BLOCK, D_HEAD), F32)
            db_qk = jnp.zeros((BLOCK, D_HEAD), F32)
            for li, m in enumerate(LEVELS):
                is_q, decay, x16 = _level_operand(q, k, exps[li], m)
                y = jnp.dot(jnp.where(lv == li, da_sym, 0.0).astype(_MXU), x16, preferred_element_type=F32)
                dx = y * decay
                dq = dq + jnp.where(is_q, dx, 0.0)
                dqk = dqk + dx
                p = x16.astype(F32) * y
                db_q = db_q + jnp.where(is_q, p, 0.0)
                db_qk = db_qk + p
            dk = dqk - dq
            db = 2.0 * db_q - db_qk
            a_t = jnp.where(c >= r, sym_ref[0, hh, 0], 0)
            st, dst = sst_ref[0, hh, 0], dst_scr[hh]
            eb = jnp.exp(b)
            b_end = b[BLOCK - 1:BLOCK, :]
            dec = jnp.exp(b_end - b)
            qh16, kt16 = (q * eb).astype(_MXU), (k * dec).astype(_MXU)
            dq_st = _mm_s(do16, st)
            dk_st = _mm_s(v16, dst)
            d_diag = jnp.sum(do * v, axis=1, keepdims=True)
            dq_ref[:, ls] = (dq + d_diag * k + eb * dq_st).astype(dq_ref.dtype)
            dk = dk + d_diag * q + dec * dk_st
            dv_ref[:, ls] = (jnp.dot(a_t, do16, preferred_element_type=F32) + _mm_nt_s(kt16, dst)).astype(dv_ref.dtype)
            dst_scr[hh] = dst * jnp.exp(b_end) + lax.dot_general(do16, qh16, _TN, preferred_element_type=F32)
            db = db + (qh16.astype(F32) * dq_st - kt16.astype(F32) * dk_st)
            dg = _mm01_left((c >= r).astype(_MXU), db) + gsum_scr[:, ls]
            gsum_scr[:, ls] = gsum_scr[:, ls] + jnp.sum(db, axis=0, keepdims=True)
            slope = (1.0 - lam) * sg * (1.0 - sg)
            df_ref[:, ls] = jnp.where(live, dg * slope / f - dk * slope, 0.0).astype(df_ref.dtype)
            dl = jnp.where(live, (dg / f - dk) * (1.0 - sg), 0.0)
            dlam_ref[0, :, ls] = dlam_ref[0, :, ls] + jnp.sum(dl, axis=0, keepdims=True)
        pl.when(last)(finish)

    blk = pl.BlockSpec((BLOCK, width), lambda b, h, n: (b * nb + nb - 1 - n, h))
    hbm = pl.BlockSpec(memory_space=pl.ANY)
    return pl.pallas_call(
        body, name="hgrn_bwd", grid=(bsz, N_HEADS // HEAD_GROUP, nb),
        in_specs=[blk, blk, blk, pl.BlockSpec((1, width), lambda b, h, n: (0, h)),
                  pl.BlockSpec((1, HEAD_GROUP, 1, D_HEAD, D_HEAD), lambda b, h, n: (b, h, nb - 1 - n, 0, 0)),
                  pl.BlockSpec((1, HEAD_GROUP, 1, D_HEAD, D_HEAD), lambda b, h, n: (b, h, nb - 1 - n, 0, 0)),
                  blk, pl.BlockSpec(sums.shape, lambda b, h, n: (0, 0)), pl.BlockSpec(levels.shape, lambda b, h, n: (0, 0))]
        + [hbm] * n_out,
        out_specs=[blk, blk, blk, pl.BlockSpec((1, 1, width), lambda b, h, n: (b, 0, h))] + [hbm] * n_out,
        out_shape=[jax.ShapeDtypeStruct((t, D_MODEL), _MXU)] * 3 + [jax.ShapeDtypeStruct((bsz, 1, D_MODEL), F32)]
        + [jax.ShapeDtypeStruct(a.shape, a.dtype) for a in outgoing],
        scratch_shapes=[pltpu.VMEM((HEAD_GROUP, D_HEAD, D_HEAD), F32), pltpu.VMEM((1, width), F32)]
        + _exchange_scratch(n_out),
        compiler_params=_params(("arbitrary", "arbitrary", "arbitrary")),
    )(qs, fs, vs, lam, sst, sym, do, sums, levels, *outgoing)


def _sb_valid(ahead, col, i, j):
    return (ahead < (i - j) * TILE) & (col >= N_PAD - j * TILE)


def _sb_logits(q16, k_blk, valid):
    z = jnp.where(valid, lax.dot_general(q16, k_blk.astype(_MXU), _NT, preferred_element_type=F32) * SB_SCALE, MASKED)
    softplus = jnp.where(z > SOFTPLUS_LINEAR, z, jnp.log(1.0 + jnp.exp(jnp.minimum(z, SOFTPLUS_LINEAR))))
    return -softplus, z - softplus


def _sb_fwd(qs, ks, vs, bsz, nq):
    t = qs.shape[0]
    lp = nq * TILE
    width = SB_FWD_GROUP * D_HEAD
    groups = N_HEADS // SB_FWD_GROUP
    lanes = [slice(hh * D_HEAD, (hh + 1) * D_HEAD) for hh in range(SB_FWD_GROUP)]

    def body(q_ref, k_ref, v_ref, o_ref, c_ref, n_ref):
        b, h, i = pl.program_id(0), pl.program_id(1), pl.program_id(2)
        q16 = [q_ref[:, ls].astype(_MXU) for ls in lanes]
        r, c = _iota2((TILE, TILE), 0), _iota2((TILE, TILE), 1)
        after = (r > c).astype(_MXU)

        def more(carry):
            jj, _, _, top = carry
            return (jj <= i) & (top > UNDERFLOW)

        def step(carry):
            jj, accs, sums, _ = carry
            j = i - jj
            ks_ = pl.ds(pl.multiple_of(j * TILE, TILE), TILE)
            valid = _sb_valid(c - r, c, i, j)
            new_accs, new_sums = [], []
            for hh, ls in enumerate(lanes):
                keep, log_beta = _sb_logits(q16[hh], k_ref[ks_, ls], valid)
                after_s = _mm01_right(keep, after)
                a = jnp.exp(log_beta + (sums[hh] + after_s))
                new_accs.append(accs[hh] + _mm(a, v_ref[ks_, ls]))
                new_sums.append(sums[hh] + (after_s[:, 0:1] + keep[:, 0:1]))
            top = functools.reduce(jnp.maximum, [jnp.max(x) for x in new_sums])
            return jj + 1, tuple(new_accs), tuple(new_sums), top

        init = (jnp.int32(0), tuple(jnp.zeros((TILE, D_HEAD), F32) for _ in lanes),
                tuple(jnp.zeros((TILE, 1), F32) for _ in lanes), jnp.float32(0.0))
        visited, accs, sums, _ = lax.while_loop(more, step, init)
        for hh, ls in enumerate(lanes):
            o_ref[:, ls] = accs[hh]
            c_ref[:, ls] = jnp.broadcast_to(sums[hh], (TILE, D_HEAD))
        n_ref[(b * groups + h) * nq + i] = visited.astype(F32)

    blk = pl.BlockSpec((TILE, width), lambda b, h, i: (b * nq + i, h))
    seq = pl.BlockSpec((lp, width), lambda b, h, i: (b, h))
    return pl.pallas_call(
        body, name="sb_fwd", grid=(bsz, groups, nq),
        in_specs=[blk, seq, seq], out_specs=[blk, blk, pl.BlockSpec(memory_space=pltpu.SMEM)],
        out_shape=[jax.ShapeDtypeStruct((t, D_MODEL), F32)] * 2 + [jax.ShapeDtypeStruct((bsz * groups * nq,), F32)],
        compiler_params=_params(("arbitrary", "arbitrary", "arbitrary")),
    )(qs, ks, vs)


def _sb_bwd(qs, ks, vs, ctot, visited, do, bsz, nq):
    t = qs.shape[0]
    lp = nq * TILE
    width = SB_FWD_GROUP * D_HEAD
    groups = N_HEADS // SB_FWD_GROUP
    lanes = [slice(hh * D_HEAD, (hh + 1) * D_HEAD) for hh in range(SB_FWD_GROUP)]

    def body(n_ref, q_ref, k_ref, v_ref, c_ref, do_ref, dq_ref, dk_ref, dv_ref, dk_acc, dv_acc):
        b, h, i = pl.program_id(0), pl.program_id(1), pl.program_id(2)

        @pl.when(i == 0)
        def _():
            dk_acc[...] = jnp.zeros_like(dk_acc)
            dv_acc[...] = jnp.zeros_like(dv_acc)

        q16 = [q_ref[:, ls].astype(_MXU) for ls in lanes]
        do16 = [do_ref[:, ls].astype(_MXU) for ls in lanes]
        q16_t = [q_ref[:, ls].astype(F32).T.astype(_MXU) for ls in lanes]
        do16_t = [do_ref[:, ls].astype(F32).T.astype(_MXU) for ls in lanes]
        totals = [c_ref[:, hh * D_HEAD:hh * D_HEAD + 1] for hh in range(SB_FWD_GROUP)]
        r, c = _iota2((TILE, TILE), 0), _iota2((TILE, TILE), 1)
        upto = (r <= c).astype(_MXU)
        before = (r < c).astype(_MXU)
        seen = n_ref[(b * groups + h) * nq + i]
        first = jnp.maximum(i + 1 - seen.astype(jnp.int32), 0)

        def step(j, carry):
            ks_ = pl.ds(pl.multiple_of(j * TILE, TILE), TILE)
            valid = _sb_valid(c - r, c, i, j)
            out = []
            for hh, ls in enumerate(lanes):
                dq, keep_pre, g_pre = carry[hh]
                k_blk, v_blk = k_ref[ks_, ls], v_ref[ks_, ls]
                keep, log_beta = _sb_logits(q16[hh], k_blk, valid)
                keep_upto = _mm01_right(keep, upto)
                a = jnp.exp(log_beta + (totals[hh] - keep_pre - keep_upto))
                da = lax.dot_general(do16[hh], v_blk.astype(_MXU), _NT, preferred_element_type=F32)
                g = a * da
                g_inside = _mm01_right(g, before)
                g_before = g_pre + g_inside
                beta = jnp.exp(log_beta)
                dz16 = (g * (1.0 - beta) - beta * g_before).astype(_MXU)
                dq = dq + jnp.dot(dz16, k_blk.astype(_MXU), preferred_element_type=F32)
                dk_acc[ls, ks_] += SB_SCALE * jnp.dot(q16_t[hh], dz16, preferred_element_type=F32)
                dv_acc[ls, ks_] += jnp.dot(do16_t[hh], a.astype(_MXU), preferred_element_type=F32)
                out.append((dq, keep_pre + keep_upto[:, TILE - 1:TILE],
                            g_pre + (g_inside[:, TILE - 1:TILE] + g[:, TILE - 1:TILE])))
            return tuple(out)

        zero_col = jnp.zeros((TILE, 1), F32)
        init = tuple((jnp.zeros((TILE, D_HEAD), F32), zero_col, zero_col) for _ in lanes)
        res = lax.fori_loop(first, i + 1, step, init)
        for hh, ls in enumerate(lanes):
            dq_ref[:, ls] = (SB_SCALE * res[hh][0]).astype(dq_ref.dtype)

        @pl.when(i == nq - 1)
        def _():
            dk_ref[...] = dk_acc[...].T.astype(dk_ref.dtype)
            dv_ref[...] = dv_acc[...].T.astype(dv_ref.dtype)

    blk = pl.BlockSpec((TILE, width), lambda b, h, i: (b * nq + i, h))
    seq = pl.BlockSpec((lp, width), lambda b, h, i: (b, h))
    seq_out = pl.BlockSpec((lp, width), lambda b, h, i: (b, h), pipeline_mode=pl.Buffered(1))
    return pl.pallas_call(
        body, name="sb_bwd", grid=(bsz, groups, nq),
        in_specs=[pl.BlockSpec(memory_space=pltpu.SMEM), blk, seq, seq, blk, blk], out_specs=[blk, seq_out, seq_out],
        out_shape=[jax.ShapeDtypeStruct((t, D_MODEL), _MXU)] * 3,
        scratch_shapes=[pltpu.VMEM((width, lp), F32)] * 2,
        compiler_params=_params(("arbitrary", "arbitrary", "arbitrary")),
    )(visited, qs, ks, vs, ctot, do)


def _head_norm(o, head_gain):
    outs, rs = [], []
    for h in range(N_HEADS):
        oh = o[:, h * D_HEAD:(h + 1) * D_HEAD]
        r = lax.rsqrt(jnp.mean(oh * oh, axis=-1, keepdims=True) + EPS)
        outs.append(oh * r)
        rs.append(r)
    return outs, rs


def _mix(o, gate, head_gain):
    if head_gain is None:
        on = o
    else:
        outs, _ = _head_norm(o, head_gain)
        on = jnp.concatenate([x * head_gain for x in outs], axis=1)
    return on, on * (gate * _sigmoid(gate))


def _out_fwd(o, gate, h_in, w_out, post_gain, head_gain, name):
    t = o.shape[0]
    tm = _row_tile(t, 2 * TILE)

    def body(o_ref, g_ref, h_ref, w_ref, pg_ref, hg_ref, ho_ref, u_ref):
        _, mix = _mix(o_ref[...], g_ref[...], hg_ref[...])
        u = jnp.dot(mix.astype(_MXU), w_ref[...], preferred_element_type=F32)
        u_ref[...] = u
        r = lax.rsqrt(jnp.mean(u * u, axis=-1, keepdims=True) + EPS)
        ho_ref[...] = h_ref[...] + u * r * pg_ref[...]

    row = pl.BlockSpec((tm, D_MODEL), lambda i: (i, 0))
    vec = pl.BlockSpec((1, D_MODEL), lambda i: (0, 0))
    return pl.pallas_call(
        body, name=name, grid=(t // tm,),
        in_specs=[row, row, row, pl.BlockSpec((D_MODEL, D_MODEL), lambda i: (0, 0)), vec,
                  pl.BlockSpec((1, D_HEAD), lambda i: (0, 0))],
        out_specs=[row, row], out_shape=[jax.ShapeDtypeStruct((t, D_MODEL), F32)] * 2,
        compiler_params=_params(("arbitrary",)),
    )(o, gate, h_in, w_out, post_gain, head_gain)


def _out_fwd_loss(o, gate, h_in, w_out, post_gain, target, nq, name):
    t = o.shape[0]

    def body(o_ref, g_ref, h_ref, w_ref, pg_ref, t_ref, dh_ref, u_ref, l_ref):
        i = pl.program_id(0)

        @pl.when(i == 0)
        def _():
            l_ref[...] = jnp.zeros_like(l_ref)

        _, mix = _mix(o_ref[...], g_ref[...], None)
        u = jnp.dot(mix.astype(_MXU), w_ref[...], preferred_element_type=F32)
        u_ref[...] = u

        @pl.when(i % nq == 0)
        def _():
            dh_ref[...] = jnp.zeros_like(dh_ref)

        @pl.when(i % nq != 0)
        def _():
            r = lax.rsqrt(jnp.mean(u * u, axis=-1, keepdims=True) + EPS)
            e = h_ref[...] + u * r * pg_ref[...] - t_ref[...]
            dh_ref[...] = e * (1.0 / D_MODEL)
            l_ref[...] += jnp.sum(e * e) * (0.5 / D_MODEL)

    row = pl.BlockSpec((TILE, D_MODEL), lambda i: (i, 0))
    vec = pl.BlockSpec((1, D_MODEL), lambda i: (0, 0))
    return pl.pallas_call(
        body, name=name, grid=(t // TILE,),
        in_specs=[row, row, row, pl.BlockSpec((D_MODEL, D_MODEL), lambda i: (0, 0)), vec,
                  pl.BlockSpec((TILE, D_MODEL), lambda i: ((i // nq) * (nq - 1) + jnp.maximum(i % nq - 1, 0), 0))],
        out_specs=[row, row, pl.BlockSpec((8, 128), lambda i: (0, 0))],
        out_shape=[jax.ShapeDtypeStruct((t, D_MODEL), F32)] * 2 + [jax.ShapeDtypeStruct((8, 128), F32)],
        compiler_params=_params(("arbitrary",)),
    )(o, gate, h_in, w_out, post_gain, target)


def _out_bwd(dh, u, o, gate, w_out, post_gain, head_gain, name, narrow_do):
    t = o.shape[0]
    tm = _row_tile(t, 2 * TILE)
    has_head = head_gain is not None

    def body(*refs):
        if has_head:
            dh_ref, u_ref, o_ref, g_ref, w_ref, pg_ref, hg_ref, do_ref, dg_ref, gw_ref, gw16_ref, gp_ref, gh_ref = refs
            hg = hg_ref[...]
        else:
            dh_ref, u_ref, o_ref, g_ref, w_ref, pg_ref, do_ref, dg_ref, gw_ref, gw16_ref, gp_ref = refs
            hg = None
        first = pl.program_id(0) == 0

        @pl.when(first)
        def _():
            gw_ref[...] = jnp.zeros_like(gw_ref)
            gp_ref[...] = jnp.zeros_like(gp_ref)
            if has_head:
                gh_ref[...] = jnp.zeros_like(gh_ref)

        dr, u, o, gate = dh_ref[...], u_ref[...], o_ref[...], g_ref[...]
        r = lax.rsqrt(jnp.mean(u * u, axis=-1, keepdims=True) + EPS)
        un = u * r
        gp_ref[...] += jnp.sum(dr * un, axis=0, keepdims=True)
        dun = dr * pg_ref[...]
        du = r * (dun - un * jnp.mean(dun * un, axis=-1, keepdims=True))
        on, mix = _mix(o, gate, hg)
        du16 = du.astype(_MXU)
        gw_ref[...] += lax.dot_general(mix.astype(_MXU), du16, _TN, preferred_element_type=F32)
        dmix = lax.dot_general(du16, w_ref[...], _NT, preferred_element_type=F32)
        sg = _sigmoid(gate)
        dg_ref[...] = (dmix * on * (sg * (1.0 + gate * (1.0 - sg)))).astype(dg_ref.dtype)
        don = dmix * (gate * sg)
        if has_head:
            outs, rs = _head_norm(o, hg)
            gh = jnp.zeros((1, D_HEAD), F32)
            cols = []
            for h in range(N_HEADS):
                dn = don[:, h * D_HEAD:(h + 1) * D_HEAD]
                gh = gh + jnp.sum(dn * outs[h], axis=0, keepdims=True)
                dnn = dn * hg
                cols.append(rs[h] * (dnn - outs[h] * jnp.mean(dnn * outs[h], axis=-1, keepdims=True)))
            gh_ref[...] += gh
            do_ref[...] = jnp.concatenate(cols, axis=1)
        else:
            do_ref[...] = don.astype(do_ref.dtype)

        @pl.when(pl.program_id(0) == pl.num_programs(0) - 1)
        def _():
            gw16_ref[...] = gw_ref[...].astype(_MXU)

    row = pl.BlockSpec((tm, D_MODEL), lambda i: (i, 0))
    vec = pl.BlockSpec((1, D_MODEL), lambda i: (0, 0))
    mat = pl.BlockSpec((D_MODEL, D_MODEL), lambda i: (0, 0))
    in_specs = [row, row, row, row, mat, vec]
    args = [dh, u, o, gate, w_out, post_gain]
    out_specs = [row, row, mat, mat, vec]
    out_shape = [jax.ShapeDtypeStruct((t, D_MODEL), _MXU if narrow_do else F32),
                 jax.ShapeDtypeStruct((t, D_MODEL), _MXU)] + [jax.ShapeDtypeStruct((D_MODEL, D_MODEL), F32),
                                                                  jax.ShapeDtypeStruct((D_MODEL, D_MODEL), _MXU),
                                                                  jax.ShapeDtypeStruct((1, D_MODEL), F32)]
    if has_head:
        in_specs.append(pl.BlockSpec((1, D_HEAD), lambda i: (0, 0)))
        args.append(head_gain)
        out_specs.append(pl.BlockSpec((1, D_HEAD), lambda i: (0, 0)))
        out_shape.append(jax.ShapeDtypeStruct((1, D_HEAD), F32))
    return pl.pallas_call(
        body, name=name, grid=(t // tm,), in_specs=in_specs, out_specs=out_specs, out_shape=out_shape,
        compiler_params=_params(("arbitrary",)),
    )(*args)


def _proj_bwd(ds, w4, h_in, gain, dh_out, name, outgoing=(), nq=None):
    t = h_in.shape[0]
    tm = _row_tile(t, 2 * TILE) if nq is None else TILE
    n_out = len(outgoing)
    n_dh = 1 if nq is None else 2

    def body(d0, d1, d2, d3, w_ref, h_ref, g_ref, dho_ref, *rest):
        send, dh_refs, gg_ref, rest = rest[:n_out], rest[n_out:n_out + n_dh], rest[n_out + n_dh], rest[n_out + n_dh + 1:]
        landed, sems = rest[:n_out], rest[n_out:]
        if n_out:
            first, last = _grid_ends(1)
            start, finish = _chip_exchange(send, landed, sems, slotted=True)
            pl.when(first)(start)

        @pl.when(pl.program_id(0) == 0)
        def _():
            gg_ref[...] = jnp.zeros_like(gg_ref)

        dy = jnp.zeros((tm, D_MODEL), F32)
        for j, d in enumerate((d0, d1, d2, d3)):
            dy = dy + lax.dot_general(d[...].astype(_MXU), w_ref[j], _NT, preferred_element_type=F32)
        x = h_ref[...]
        r = lax.rsqrt(jnp.mean(x * x, axis=-1, keepdims=True) + EPS)
        xn = x * r
        gg_ref[...] += jnp.sum(dy * xn, axis=0, keepdims=True)
        dxn = dy * g_ref[...]
        dh = dho_ref[...] + r * (dxn - xn * jnp.mean(dxn * xn, axis=-1, keepdims=True))
        if nq is None:
            dh_refs[0][...] = dh
        else:
            in_front = pl.program_id(0) % nq == 0

            @pl.when(in_front)
            def _():
                dh_refs[1][...] = dh

            @pl.when(jnp.logical_not(in_front))
            def _():
                dh_refs[0][...] = dh
        if n_out:
            pl.when(last)(finish)

    row = pl.BlockSpec((tm, D_MODEL), lambda i: (i, 0))
    vec = pl.BlockSpec((1, D_MODEL), lambda i: (0, 0))
    hbm = pl.BlockSpec(memory_space=pl.ANY)
    if nq is None:
        dh_specs, dh_shapes = [row], [jax.ShapeDtypeStruct((t, D_MODEL), F32)]
    else:
        dh_specs = [pl.BlockSpec((TILE, D_MODEL), lambda i: ((i // nq) * (nq - 1) + jnp.maximum(i % nq - 1, 0), 0)),
                    pl.BlockSpec((TILE, D_MODEL), lambda i: (i // nq, 0))]
        dh_shapes = [jax.ShapeDtypeStruct((t // nq * (nq - 1), D_MODEL), F32),
                     jax.ShapeDtypeStruct((t // nq, D_MODEL), F32)]
    return pl.pallas_call(
        body, name=name, grid=(t // tm,),
        in_specs=[row] * 4 + [pl.BlockSpec((4, D_MODEL, D_MODEL), lambda i: (0, 0, 0)), row, vec, row] + [hbm] * n_out,
        out_specs=dh_specs + [vec] + [hbm] * n_out,
        out_shape=dh_shapes + [jax.ShapeDtypeStruct((1, D_MODEL), F32)]
        + [jax.ShapeDtypeStruct(a.shape, a.dtype) for a in outgoing],
        scratch_shapes=_exchange_scratch(n_out) if n_out else [],
        compiler_params=_params(("arbitrary",)),
    )(*ds, w4, h_in, gain, dh_out, *outgoing)


def _weight_grad(y, d, name):
    t = y.shape[0]
    tk = _row_tile(t, t // 4)

    def body(y_ref, d_ref, g_ref, g16_ref):
        @pl.when(pl.program_id(0) == 0)
        def _():
            g_ref[...] = jnp.zeros_like(g_ref)

        g_ref[...] += lax.dot_general(y_ref[...], d_ref[...].astype(_MXU), _TN, preferred_element_type=F32)

        @pl.when(pl.program_id(0) == pl.num_programs(0) - 1)
        def _():
            g16_ref[...] = g_ref[...].astype(_MXU)

    row = pl.BlockSpec((tk, D_MODEL), lambda i: (i, 0))
    mat = pl.BlockSpec((D_MODEL, D_MODEL), lambda i: (0, 0))
    return pl.pallas_call(
        body, name=name, grid=(t // tk,), in_specs=[row, row], out_specs=[mat, mat],
        out_shape=[jax.ShapeDtypeStruct((D_MODEL, D_MODEL), F32), jax.ShapeDtypeStruct((D_MODEL, D_MODEL), _MXU)],
        compiler_params=_params(("arbitrary",)),
    )(y, d)


def _local_step(x, target, meta, pre_norm, post_norm, lam, head_gain, hw_in, shards):
    bsz, seq, _ = x.shape
    nq = seq // TILE + 1
    nb = nq * (TILE // BLOCK)
    lp = nq * TILE
    t = bsz * lp
    d4 = D_MODEL // N_CHIPS
    front = jnp.concatenate([jnp.zeros((N_PAD, D_MODEL), F32), meta], axis=0)
    h0 = jnp.concatenate([jnp.broadcast_to(front[None], (bsz, TILE, D_MODEL)), x], axis=1).reshape(t, D_MODEL)
    pre0, pre1, post0, post1 = pre_norm[0:1], pre_norm[1:2], post_norm[0:1], post_norm[1:2]

    y0, q0, f0, v0, g0 = _norm_proj(h0, pre0, hw_in, "norm_proj_hgrn", (False,) * 4)
    o0, sst, sym, sw_in, sw_out, hw_out = _hgrn_fwd(q0, f0, v0, lam, bsz, nb, shards)
    sw_out, hw_out = sw_out.reshape(D_MODEL, D_MODEL), hw_out.reshape(D_MODEL, D_MODEL)
    h1, u0 = _out_fwd(o0, g0, h0, hw_out, post0, head_gain, "out_fwd_hgrn")
    y1, q1, k1, v1, g1 = _norm_proj(h1, pre1, sw_in, "norm_proj_sb", (True, True, True, False))
    o1, ctot, visited = _sb_fwd(q1, k1, v1, bsz, nq)
    dh2, u1, loss_blk = _out_fwd_loss(o1, g1, h1, sw_out, post1, target.reshape(bsz * seq, D_MODEL), nq, "out_fwd_sb")

    do1, dg1, g_sw_out, g_sw_out16, g_post1 = _out_bwd(dh2, u1, o1, g1, sw_out, post1, None, "out_bwd_sb", True)
    dq1, dk1, dv1 = _sb_bwd(q1, k1, v1, ctot, visited, do1, bsz, nq)
    ds1 = (dq1, dk1, dv1, dg1)
    dh1, g_pre1 = _proj_bwd(ds1, sw_in, h1, pre1, dh2, "proj_bwd_sb")
    g_sw_in = [_weight_grad(y1, d, "wgrad_sb_%d" % j) for j, d in enumerate(ds1)]

    do0, dg0, g_hw_out, g_hw_out16, g_post0, g_head = _out_bwd(dh1, u0, o0, g0, hw_out, post0, head_gain, "out_bwd_hgrn",
                                                               False)
    ready = (jnp.stack([g16 for _, g16 in g_sw_in]), g_sw_out16.reshape(N_CHIPS, d4, D_MODEL),
             g_hw_out16.reshape(N_CHIPS, d4, D_MODEL))
    dq0, df0, dv0, dlam, land_sw_in, land_sw_out, land_hw_out = _hgrn_bwd(q0, f0, v0, lam, sst, sym, do0, bsz, nb, ready)
    ds0 = (dq0, df0, dv0, dg0)
    g_hw_in = [_weight_grad(y0, d, "wgrad_hgrn_%d" % j) for j, d in enumerate(ds0)]
    last = (jnp.stack([g16 for _, g16 in g_hw_in]),)
    grad_x, dh_front, g_pre0, land_hw_in = _proj_bwd(ds0, hw_in, h0, pre0, dh1, "proj_bwd_hgrn", last, nq)

    grad_x = grad_x.reshape(bsz, seq, D_MODEL)
    g_meta = jnp.sum(dh_front.reshape(bsz, TILE, D_MODEL)[:, N_PAD:, :], axis=0)
    g_lam = jnp.sum(dlam, axis=0)
    small = jnp.concatenate([g_pre0, g_pre1, g_post0, g_post1, g_lam, g_lam,
                             jnp.pad(g_head, ((0, 0), (0, D_MODEL - D_HEAD))), g_meta,
                             jnp.pad(loss_blk[0:1], ((0, 0), (0, D_MODEL - loss_blk.shape[1])))], axis=0)
    rows4 = lambda g: [g[j * d4:(j + 1) * d4] for j in range(N_CHIPS)]
    large = dict(hw_in=(land_hw_in, [g for g, _ in g_hw_in]), sw_in=(land_sw_in, [g for g, _ in g_sw_in]),
                 hw_out=(land_hw_out, rows4(g_hw_out)), sw_out=(land_sw_out, rows4(g_sw_out)))
    return grad_x, small, large


def _prep_weights(hw_in, sw_in, hw_out, sw_out, meta):
    def body(hi_ref, si_ref, ho_ref, so_ref, m_ref, ghi, gm, si16, so16, ho16, far_send, far_recv, near_send, near_recv):
        x, y, c = _place()
        me = 2 * x + y
        ghi[me] = hi_ref[0].astype(_MXU)
        gm[me] = m_ref[...]
        si16[...] = si_ref[0].astype(_MXU)
        so16[...] = so_ref[0].astype(_MXU)
        ho16[...] = ho_ref[0].astype(_MXU)
        outs = (ghi, gm)
        n = len(outs)
        peers = [(1 - x, y), (x, 1 - y), (1 - x, 1 - y)]

        def half(a, slot, which):
            rows = outs[a].shape[1] // 2
            return outs[a].at[slot, pl.ds(which * rows, rows), :]

        def far(r, a, slot):
            px, py = peers[r]
            return pltpu.make_async_remote_copy(
                src_ref=half(a, slot, c), dst_ref=half(a, slot, c), send_sem=far_send.at[r * n + a],
                recv_sem=far_recv.at[r * n + a], device_id=(px, py, c), device_id_type=MESH)

        def near(r, a, which):
            px, py = peers[r]
            return pltpu.make_async_remote_copy(
                src_ref=half(a, 2 * px + py, which), dst_ref=half(a, 2 * px + py, which),
                send_sem=near_send.at[r * n + a], recv_sem=near_recv.at[r * n + a],
                device_id=(x, y, 1 - c), device_id_type=MESH)

        for r in range(3):
            for a in range(n):
                far(r, a, me).start()
        for r, (px, py) in enumerate(peers):
            for a in range(n):
                far(r, a, 2 * px + py).wait_recv()
                near(r, a, c).start()
        for r in range(3):
            for a in range(n):
                near(r, a, 1 - c).wait_recv()
        for r in range(3):
            for a in range(n):
                far(r, a, me).wait_send()
                near(r, a, c).wait_send()

    d4 = D_MODEL // N_CHIPS
    vm = pl.BlockSpec(memory_space=pltpu.VMEM)
    return pl.pallas_call(
        body, name="prep_weights",
        in_specs=[vm] * 5, out_specs=[vm] * 5,
        out_shape=[jax.ShapeDtypeStruct((N_CHIPS, D_MODEL, D_MODEL), _MXU), jax.ShapeDtypeStruct((N_CHIPS, N_META, d4), F32),
                   jax.ShapeDtypeStruct((D_MODEL, D_MODEL), _MXU), jax.ShapeDtypeStruct((d4, D_MODEL), _MXU),
                   jax.ShapeDtypeStruct((d4, D_MODEL), _MXU)],
        scratch_shapes=[pltpu.SemaphoreType.DMA((6,))] * 4,
        compiler_params=pltpu.CompilerParams(vmem_limit_bytes=VMEM_LIMIT),
    )(hw_in, sw_in, hw_out, sw_out, meta)


def _scatter_small(small):
    def body(sm, lsm, send_sems, recv_sems, local_sem):
        x, y, c = _place()
        mine = 4 * x + 2 * y + c
        local = pltpu.make_async_copy(sm, lsm.at[mine], local_sem)
        local.start()

        def copy(rel, src_dev, to):
            return pltpu.make_async_remote_copy(
                src_ref=sm, dst_ref=lsm.at[src_dev], send_sem=send_sems.at[rel - 1], recv_sem=recv_sems.at[rel - 1],
                device_id=to, device_id_type=MESH)

        flip = lambda bit, v: 1 - v if bit else v
        rels = [(rel, flip(rel & 4, x), flip(rel & 2, y), flip(rel & 1, c)) for rel in range(1, N_DEV)]
        sends = [copy(rel, mine, (px, py, pc)) for rel, px, py, pc in rels]
        for cp in sends:
            cp.start()
        for rel, px, py, pc in rels:
            copy(rel, 4 * px + 2 * py + pc, (px, py, pc)).wait_recv()
        for cp in sends:
            cp.wait_send()
        local.wait()

    hbm = pl.BlockSpec(memory_space=pl.ANY)
    return pl.pallas_call(
        body, name="scatter_small", in_specs=[hbm], out_specs=hbm,
        out_shape=jax.ShapeDtypeStruct((N_DEV, SMALL_ROWS, D_MODEL), F32),
        scratch_shapes=[pltpu.SemaphoreType.DMA((N_DEV - 1,)), pltpu.SemaphoreType.DMA((N_DEV - 1,)),
                        pltpu.SemaphoreType.DMA(())],
    )(small)


def _sum_slots(landed, own, me, name):
    n, rows, _ = landed.shape
    tm = rows if rows < 256 else 256

    def body(me_ref, l_ref, o0, o1, o2, o3, out_ref):
        acc = None
        for k, o in enumerate((o0, o1, o2, o3)):
            term = jnp.where(me_ref[0] == k, o[...], l_ref[k].astype(F32))
            acc = term if acc is None else acc + term
        out_ref[...] = acc

    blk = pl.BlockSpec((tm, D_MODEL), lambda i: (i, 0))
    return pl.pallas_call(
        body, name=name, grid=(rows // tm,),
        in_specs=[pl.BlockSpec(memory_space=pltpu.SMEM), pl.BlockSpec((n, tm, D_MODEL), lambda i: (0, i, 0))] + [blk] * 4,
        out_specs=blk, out_shape=jax.ShapeDtypeStruct((rows, D_MODEL), F32),
        compiler_params=_params(("arbitrary",)),
    )(me, landed, *own)


def _swap_with_sibling(parts):
    def body(a0, a1, a2, a3, b0, b1, b2, b3, send_sems, recv_sems):
        x, y, c = _place()
        copies = [pltpu.make_async_remote_copy(src_ref=s, dst_ref=d, send_sem=send_sems.at[a], recv_sem=recv_sems.at[a],
                                               device_id=(x, y, 1 - c), device_id_type=MESH)
                  for a, (s, d) in enumerate(zip((a0, a1, a2, a3), (b0, b1, b2, b3)))]
        for cp in copies:
            cp.start()
        for cp in copies:
            cp.wait()

    hbm = pl.BlockSpec(memory_space=pl.ANY)
    return pl.pallas_call(
        body, name="swap_with_sibling", in_specs=[hbm] * 4, out_specs=[hbm] * 4,
        out_shape=[jax.ShapeDtypeStruct(p.shape, F32) for p in parts],
        scratch_shapes=[pltpu.SemaphoreType.DMA((4,)), pltpu.SemaphoreType.DMA((4,))],
    )(*parts)


def _adamw_math(w, g, m, v):
    m = ADAM_B1 * m + (1.0 - ADAM_B1) * g
    v = ADAM_B2 * v + (1.0 - ADAM_B2) * (g * g)
    m_hat = m / (1.0 - ADAM_B1 ** ADAM_STEP)
    v_hat = v / (1.0 - ADAM_B2 ** ADAM_STEP)
    delta = -ADAM_LR * (m_hat / (jnp.sqrt(v_hat) + ADAM_EPS) + ADAM_WD * w)
    return delta, m, v


def _adamw(w, g_parts, m, v, name):
    rows, cols = w.shape
    tm = rows if rows < 256 else 256
    n = len(g_parts)

    def body(*refs):
        w_ref, m_ref, v_ref = refs[n:n + 3]
        g_ref, d_ref, nm_ref, nv_ref = refs[n + 3:]
        g = refs[0][...]
        for p in refs[1:n]:
            g = g + p[...]
        g_ref[...] = g
        d_ref[...], nm_ref[...], nv_ref[...] = _adamw_math(w_ref[...], g, m_ref[...], v_ref[...])

    blk = pl.BlockSpec((tm, cols), lambda i: (i, 0))
    return pl.pallas_call(
        body, name=name, grid=(rows // tm,), in_specs=[blk] * (n + 3), out_specs=[blk] * 4,
        out_shape=[jax.ShapeDtypeStruct((rows, cols), F32)] * 4,
        compiler_params=_params(("arbitrary",)),
    )(*g_parts, w, m, v)


def _lam_of(hgrn_lb):
    def body(lb_ref, o_ref):
        lb = lb_ref[...]
        e = jnp.exp(lb - jnp.max(lb, axis=0, keepdims=True))
        o_ref[...] = e[0:1, :] / jnp.sum(e, axis=0, keepdims=True)

    return pl.pallas_call(body, name="lam_of", out_shape=jax.ShapeDtypeStruct((1, D_MODEL), F32))(hgrn_lb)


def _small_grads(land_small, lam):
    def body(l_ref, lam_ref, o_ref):
        acc = l_ref[0]
        for k in range(1, N_DEV):
            acc = acc + l_ref[k]
        p = lam_ref[...]
        slope = p * (1.0 - p)
        row = _iota2((SMALL_ROWS, D_MODEL), 0)
        o_ref[...] = acc * jnp.where(row == 4, slope, jnp.where(row == 5, -slope, 1.0))

    return pl.pallas_call(body, name="small_grads",
                          out_shape=jax.ShapeDtypeStruct((SMALL_ROWS, D_MODEL), F32))(land_small, lam)


def kernel(x, meta_tokens, pre_norm, post_norm, hgrn_w_in, hgrn_lb, hgrn_out_norm, hgrn_w_out, sb_w_in, sb_w_out, loss_target, m_meta_tokens, m_pre_norm, m_post_norm, m_hgrn_w_in, m_hgrn_lb, m_hgrn_out_norm, m_hgrn_w_out, m_sb_w_in, m_sb_w_out, v_meta_tokens, v_pre_norm, v_post_norm, v_hgrn_w_in, v_hgrn_lb, v_hgrn_out_norm, v_hgrn_w_out, v_sb_w_in, v_sb_w_out):
    d4 = D_MODEL // N_CHIPS
    chip = 2 * lax.axis_index("x") + lax.axis_index("y")
    hw_in, meta4, sw_in16, sw_out16, hw_out16 = _prep_weights(hgrn_w_in, sb_w_in, hgrn_w_out, sb_w_out, meta_tokens)
    meta = meta4.transpose(1, 0, 2).reshape(N_META, D_MODEL)
    lam = _lam_of(hgrn_lb)
    grad_x, small, large = _local_step(
        x, loss_target, meta, pre_norm, post_norm, lam, hgrn_out_norm,
        hw_in, (sw_in16, sw_out16, hw_out16))

    me = jnp.reshape(chip, (1,)).astype(jnp.int32)
    parts = [_sum_slots(*large[n], me, "sum_" + n) for n in ("hw_in", "sw_in", "hw_out", "sw_out")]
    sib = _swap_with_sibling(parts)
    small = _small_grads(_scatter_small(small), lam)
    loss = small[SMALL_ROWS - 1, 0]

    res = {}
    res["hgrn_w_in"] = _adamw(hgrn_w_in[0], [parts[0], sib[0]], m_hgrn_w_in[0], v_hgrn_w_in[0], "adamw_hw_in")
    res["sb_w_in"] = _adamw(sb_w_in[0], [parts[1], sib[1]], m_sb_w_in[0], v_sb_w_in[0], "adamw_sw_in")
    res["hgrn_w_out"] = _adamw(hgrn_w_out[0], [parts[2], sib[2]], m_hgrn_w_out[0], v_hgrn_w_out[0], "adamw_hw_out")
    res["sb_w_out"] = _adamw(sb_w_out[0], [parts[3], sib[3]], m_sb_w_out[0], v_sb_w_out[0], "adamw_sw_out")
    res["pre_norm"] = _adamw(pre_norm, [small[0:2]], m_pre_norm, v_pre_norm, "adamw_pre")
    res["post_norm"] = _adamw(post_norm, [small[2:4]], m_post_norm, v_post_norm, "adamw_post")
    res["hgrn_lb"] = _adamw(hgrn_lb, [small[4:6]], m_hgrn_lb, v_hgrn_lb, "adamw_lb")
    res["hgrn_out_norm"] = _adamw(hgrn_out_norm, [small[6:7, :D_HEAD]], m_hgrn_out_norm, v_hgrn_out_norm, "adamw_head")
    g_meta = lax.dynamic_slice_in_dim(small[7:7 + N_META], chip * d4, d4, axis=1)
    res["meta_tokens"] = _adamw(meta_tokens, [g_meta], m_meta_tokens, v_meta_tokens, "adamw_meta")
    for n in ("hgrn_w_in", "hgrn_w_out", "sb_w_in", "sb_w_out"):
        res[n] = tuple(a[None] for a in res[n])
    order = ("meta_tokens", "pre_norm", "post_norm", "hgrn_w_in", "hgrn_lb", "hgrn_out_norm", "hgrn_w_out",
             "sb_w_in", "sb_w_out")
    return (loss, grad_x, *[res[n][0] for n in order], *[res[n][1] for n in order],
            *[res[n][2] for n in order], *[res[n][3] for n in order])
```

```python
import functools

import jax
import numpy as np
import jax.numpy as jnp
from jax import lax
from jax.experimental import pallas as pl
from jax.experimental.pallas import tpu as pltpu

F32 = jnp.float32
_MXU = jnp.bfloat16

D_MODEL = 1024
N_HEADS = 8
D_HEAD = 128
BLOCK = 128
N_META = 16
TILE = 256
N_PAD = TILE - N_META
UNDERFLOW = -105.0
EPS = 1e-6
SB_SCALE = D_HEAD ** -0.5
SOFTPLUS_LINEAR = 20.0
MASKED = -1e30
ADAM_LR, ADAM_B1, ADAM_B2, ADAM_EPS, ADAM_WD, ADAM_STEP = 0.001, 0.9, 0.999, 1e-08, 0.01, 10
N_CHIPS = 4
N_DEV = 8
SMALL_ROWS = 24
VMEM_LIMIT = 56 * 1024 * 1024
MESH = pl.DeviceIdType.MESH

_NT = (((1,), (1,)), ((), ()))
_TN = (((0,), (0,)), ((), ()))


def _mm(a, b):
    return jnp.dot(a.astype(_MXU), b.astype(_MXU), preferred_element_type=F32)


def _mm_nt(a, b):
    return lax.dot_general(a.astype(_MXU), b.astype(_MXU), _NT, preferred_element_type=F32)


def _mm_tn(a, b):
    return lax.dot_general(a.astype(_MXU), b.astype(_MXU), _TN, preferred_element_type=F32)


def _split2(x):
    hi = x.astype(_MXU)
    return hi, (x - hi.astype(F32)).astype(_MXU)


def _mm_s(a16, state):
    hi, lo = _split2(state)
    return jnp.dot(a16, hi, preferred_element_type=F32) + jnp.dot(a16, lo, preferred_element_type=F32)


def _mm_nt_s(a16, state):
    hi, lo = _split2(state)
    return (lax.dot_general(a16, hi, _NT, preferred_element_type=F32)
            + lax.dot_general(a16, lo, _NT, preferred_element_type=F32))


def _mm01_right(x, m01):
    return jnp.dot(x.astype(_MXU), m01, preferred_element_type=F32)


def _mm01_left(m01, x):
    hi, lo = _split2(x)
    return jnp.dot(m01, hi, preferred_element_type=F32) + jnp.dot(m01, lo, preferred_element_type=F32)


def _iota2(shape, dim):
    return lax.broadcasted_iota(jnp.int32, shape, dim)


def _row_tile(total, pref):
    t = pref
    while total % t:
        t -= BLOCK
    return t


def _params(sem, limit=VMEM_LIMIT):
    return pltpu.CompilerParams(dimension_semantics=sem, vmem_limit_bytes=limit)


def _sigmoid(x):
    return 1.0 / (1.0 + jnp.exp(-x))


def _grid_ends(ndim):
    first, last = True, True
    for d in range(ndim):
        first = first & (pl.program_id(d) == 0)
        last = last & (pl.program_id(d) == pl.num_programs(d) - 1)
    return first, last


def _place():
    return lax.axis_index("x"), lax.axis_index("y"), lax.axis_index("c")


def _exchange_scratch(n):
    return [pltpu.SemaphoreType.DMA((3 * n,)), pltpu.SemaphoreType.DMA((3 * n,)), pltpu.SemaphoreType.DMA((n,))]


def _chip_exchange(srcs, dsts, sems, slotted):
    send_sems, recv_sems, local_sems = sems
    x, y, c = _place()
    me = 2 * x + y
    peers = [(1 - x, y), (x, 1 - y), (1 - x, 1 - y)]
    n = len(dsts)

    def remote(r, a, sending):
        px, py = peers[r]
        p = 2 * px + py
        return pltpu.make_async_remote_copy(
            src_ref=srcs[a].at[p] if slotted else srcs[a], dst_ref=dsts[a].at[me if sending else p],
            send_sem=send_sems.at[r * n + a], recv_sem=recv_sems.at[r * n + a],
            device_id=(px, py, c), device_id_type=MESH)

    def local(a):
        return pltpu.make_async_copy(srcs[a].at[me] if slotted else srcs[a], dsts[a].at[me], local_sems.at[a])

    def start():
        for a in range(n):
            local(a).start()
        for r in range(3):
            for a in range(n):
                remote(r, a, True).start()

    def finish():
        for r in range(3):
            for a in range(n):
                remote(r, a, False).wait_recv()
        for r in range(3):
            for a in range(n):
                remote(r, a, True).wait_send()
        for a in range(n):
            local(a).wait()

    return start, finish


def _norm_proj(h, gain, w4, name, narrow):
    t = h.shape[0]
    tm = _row_tile(t, 2 * TILE)

    def body(h_ref, g_ref, w_ref, y_ref, s0, s1, s2, s3):
        x = h_ref[...]
        r = lax.rsqrt(jnp.mean(x * x, axis=-1, keepdims=True) + EPS)
        y = (x * r * g_ref[...]).astype(_MXU)
        y_ref[...] = y
        for j, s in enumerate((s0, s1, s2, s3)):
            s[...] = jnp.dot(y, w_ref[j], preferred_element_type=F32).astype(s.dtype)

    row = pl.BlockSpec((tm, D_MODEL), lambda i: (i, 0))
    return pl.pallas_call(
        body, name=name, grid=(t // tm,),
        in_specs=[row, pl.BlockSpec((1, D_MODEL), lambda i: (0, 0)),
                  pl.BlockSpec((4, D_MODEL, D_MODEL), lambda i: (0, 0, 0))],
        out_specs=[row] * 5,
        out_shape=[jax.ShapeDtypeStruct((t, D_MODEL), _MXU)]
        + [jax.ShapeDtypeStruct((t, D_MODEL), _MXU if n else F32) for n in narrow],
        compiler_params=_params(("arbitrary",)),
    )(h, gain, w4)


LEVELS = (64, 32, 16, 8, 4, 2, 1)
HEAD_GROUP = 4
SB_FWD_GROUP = 4


def _hgrn_tables():
    r = np.arange(BLOCK)
    mats = [r[None, :] <= r[:, None]]
    x = r[:, None] ^ r[None, :]
    lv = np.full((BLOCK, BLOCK), len(LEVELS), np.int32)
    for i, m in enumerate(LEVELS):
        lv[(x >= m) & (x < 2 * m)] = i
    return jnp.asarray(np.concatenate(mats, 0).astype(np.float32), dtype=_MXU), jnp.asarray(lv)


def _hgrn_exponents(g, sums):
    b = _mm01_left(sums, g)
    row = _iota2((BLOCK, D_HEAD), 0)
    out = []
    for m in LEVELS:
        is_q = (row & m) != 0
        if m >= 4:
            grp = b.reshape(BLOCK // (2 * m), 2 * m, D_HEAD)
            ref = jnp.broadcast_to(grp[:, m - 1:m, :], grp.shape).reshape(BLOCK, D_HEAD)
            d = b - ref
            out.append(jnp.where(is_q, d, -d))
        elif m == 2:
            below, above = pltpu.roll(g, 1, axis=0), pltpu.roll(g, BLOCK - 1, axis=0)
            low = row & 3
            out.append(jnp.where(low == 3, g + below, jnp.where(low == 2, g, jnp.where(low == 0, above, 0.0))))
        else:
            out.append(jnp.where(is_q, g, 0.0))
    return b, out


def _hgrn_gates(fz, lam, chunk):
    pos = chunk * BLOCK + _iota2((BLOCK, D_HEAD), 0)
    live = pos >= N_PAD
    sg = _sigmoid(fz)
    f = lam + (1.0 - lam) * sg
    g = jnp.where(live, jnp.log(f), 0.0)
    k = jnp.where(live, (1.0 - lam) * (1.0 - sg), 0.0)
    return sg, f, g, k, live


def _level_operand(q, k, exponent, m):
    decay = jnp.exp(exponent)
    is_q = (_iota2((BLOCK, D_HEAD), 0) & m) != 0
    return is_q, decay, (jnp.where(is_q, q, k) * decay).astype(_MXU)


def _hgrn_fwd(qs, fs, vs, lam, bsz, nb, shards):
    t = qs.shape[0]
    sums, levels = _hgrn_tables()
    width = HEAD_GROUP * D_HEAD

    n_sh = len(shards)

    def body(q_ref, f_ref, v_ref, lam_ref, sums_ref, lv_ref, *rest):
        own, (o_ref, sst_ref, sym_ref), rest = rest[:n_sh], rest[n_sh:n_sh + 3], rest[n_sh + 3:]
        gathered, st_scr, sems = rest[:n_sh], rest[n_sh], rest[n_sh + 1:]
        n = pl.program_id(2)
        first, last = _grid_ends(3)
        start, finish = _chip_exchange(own, gathered, sems, slotted=False)
        pl.when(first)(start)

        @pl.when(n == 0)
        def _():
            st_scr[...] = jnp.zeros_like(st_scr)

        lv = lv_ref[...]
        r, c = _iota2((BLOCK, BLOCK), 0), _iota2((BLOCK, BLOCK), 1)
        for hh in range(HEAD_GROUP):
            ls = slice(hh * D_HEAD, (hh + 1) * D_HEAD)
            st = st_scr[hh]
            sst_ref[0, hh, 0] = st
            q, v = q_ref[:, ls], v_ref[:, ls]
            _, _, g, k, _ = _hgrn_gates(f_ref[:, ls], lam_ref[:, ls], n)
            b, exps = _hgrn_exponents(g, sums_ref[...])
            sym = jnp.zeros((BLOCK, BLOCK), F32)
            for li, m in enumerate(LEVELS):
                _, _, x16 = _level_operand(q, k, exps[li], m)
                sym = jnp.where(lv == li, lax.dot_general(x16, x16, _NT, preferred_element_type=F32), sym)
            sym = jnp.where(c == r, jnp.sum(q * k, axis=1, keepdims=True), sym).astype(_MXU)
            sym_ref[0, hh, 0] = sym
            o_ref[:, ls] = _mm_nt(q * jnp.exp(b), st) + _mm(jnp.where(c <= r, sym, 0), v)
            b_end = b[BLOCK - 1:BLOCK, :]
            st_scr[hh] = st * jnp.exp(b_end) + _mm_tn(v, k * jnp.exp(b_end - b))
        pl.when(last)(finish)

    blk = pl.BlockSpec((BLOCK, width), lambda b, h, n: (b * nb + n, h))
    hbm = pl.BlockSpec(memory_space=pl.ANY)
    return pl.pallas_call(
        body, name="hgrn_fwd", grid=(bsz, N_HEADS // HEAD_GROUP, nb),
        in_specs=[blk, blk, blk, pl.BlockSpec((1, width), lambda b, h, n: (0, h)),
                  pl.BlockSpec(sums.shape, lambda b, h, n: (0, 0)), pl.BlockSpec(levels.shape, lambda b, h, n: (0, 0))]
        + [hbm] * n_sh,
        out_specs=[blk] + [pl.BlockSpec((1, HEAD_GROUP, 1, D_HEAD, D_HEAD), lambda b, h, n: (b, h, n, 0, 0))] * 2
        + [hbm] * n_sh,
        out_shape=[jax.ShapeDtypeStruct((t, D_MODEL), F32),
                   jax.ShapeDtypeStruct((bsz, N_HEADS, nb, D_HEAD, D_HEAD), F32),
                   jax.ShapeDtypeStruct((bsz, N_HEADS, nb, D_HEAD, D_HEAD), _MXU)]
        + [jax.ShapeDtypeStruct((N_CHIPS,) + a.shape, a.dtype) for a in shards],
        scratch_shapes=[pltpu.VMEM((HEAD_GROUP, D_HEAD, D_HEAD), F32)] + _exchange_scratch(n_sh),
        compiler_params=_params(("arbitrary", "arbitrary", "arbitrary")),
    )(qs, fs, vs, lam, sums, levels, *shards)


def _hgrn_bwd(qs, fs, vs, lam, sst, sym, do, bsz, nb, outgoing):
    t = qs.shape[0]
    sums, levels = _hgrn_tables()
    width = HEAD_GROUP * D_HEAD

    n_out = len(outgoing)

    def body(q_ref, f_ref, v_ref, lam_ref, sst_ref, sym_ref, do_ref, sums_ref, lv_ref, *rest):
        send, (dq_ref, df_ref, dv_ref, dlam_ref), rest = rest[:n_out], rest[n_out:n_out + 4], rest[n_out + 4:]
        landed, dst_scr, gsum_scr, sems = rest[:n_out], rest[n_out], rest[n_out + 1], rest[n_out + 2:]
        n = pl.program_id(2)
        chunk = nb - 1 - n
        first, last = _grid_ends(3)
        start, finish = _chip_exchange(send, landed, sems, slotted=True)
        pl.when(first)(start)

        @pl.when(n == 0)
        def _():
            dst_scr[...] = jnp.zeros_like(dst_scr)
            gsum_scr[...] = jnp.zeros_like(gsum_scr)
            dlam_ref[...] = jnp.zeros_like(dlam_ref)

        lv = lv_ref[...]
        r, c = _iota2((BLOCK, BLOCK), 0), _iota2((BLOCK, BLOCK), 1)
        for hh in range(HEAD_GROUP):
            ls = slice(hh * D_HEAD, (hh + 1) * D_HEAD)
            lam = lam_ref[:, ls]
            q, v, do = q_ref[:, ls], v_ref[:, ls], do_ref[:, ls]
            sg, f, g, k, live = _hgrn_gates(f_ref[:, ls], lam, chunk)
            b, exps = _hgrn_exponents(g, sums_ref[...])
            do16, v16 = do.astype(_MXU), v.astype(_MXU)
            da = lax.dot_general(do16, v16, _NT, preferred_element_type=F32)
            da_sym = jnp.where(c < r, da, da.T)
            dq = jnp.zeros((BLOCK, D_HEAD), F32)
            dqk = jnp.zeros((BLOCK, D_HEAD), F32)
            db_q = jnp.zeros((BLOCK, D_HEAD), F32)
            db_qk = jnp.zeros((BLOCK, D_HEAD), F32)
            for li, m in enumerate(LEVELS):
                is_q, decay, x16 = _level_operand(q, k, exps[li], m)
                y = jnp.dot(jnp.where(lv == li, da_sym, 0.0).astype(_MXU), x16, preferred_element_type=F32)
                dx = y * decay
                dq = dq + jnp.where(is_q, dx, 0.0)
                dqk = dqk + dx
                p = x16.astype(F32) * y
                db_q = db_q + jnp.where(is_q, p, 0.0)
                db_qk = db_qk + p
            dk = dqk - dq
            db = 2.0 * db_q - db_qk
            a_t = jnp.where(c >= r, sym_ref[0, hh, 0], 0)
            st, dst = sst_ref[0, hh, 0], dst_scr[hh]
            eb = jnp.exp(b)
            b_end = b[BLOCK - 1:BLOCK, :]
            dec = jnp.exp(b_end - b)
            qh16, kt16 = (q * eb).astype(_MXU), (k * dec).astype(_MXU)
            dq_st = _mm_s(do16, st)
            dk_st = _mm_s(v16, dst)
            d_diag = jnp.sum(do * v, axis=1, keepdims=True)
            dq_ref[:, ls] = (dq + d_diag * k + eb * dq_st).astype(dq_ref.dtype)
            dk = dk + d_diag * q + dec * dk_st
            dv_ref[:, ls] = (jnp.dot(a_t, do16, preferred_element_type=F32) + _mm_nt_s(kt16, dst)).astype(dv_ref.dtype)
            dst_scr[hh] = dst * jnp.exp(b_end) + lax.dot_general(do16, qh16, _TN, preferred_element_type=F32)
            db = db + (qh16.astype(F32) * dq_st - kt16.astype(F32) * dk_st)
            dg = _mm01_left((c >= r).astype(_MXU), db) + gsum_scr[:, ls]
            gsum_scr[:, ls] = gsum_scr[:, ls] + jnp.sum(db, axis=0, keepdims=True)
            slope = (1.0 - lam) * sg * (1.0 - sg)
            df_ref[:, ls] = jnp.where(live, dg * slope / f - dk * slope, 0.0).astype(df_ref.dtype)
            dl = jnp.where(live, (dg / f - dk) * (1.0 - sg), 0.0)
            dlam_ref[0, :, ls] = dlam_ref[0, :, ls] + jnp.sum(dl, axis=0, keepdims=True)
        pl.when(last)(finish)

    blk = pl.BlockSpec((BLOCK, width), lambda b, h, n: (b * nb + nb - 1 - n, h))
    hbm = pl.BlockSpec(memory_space=pl.ANY)
    return pl.pallas_call(
        body, name="hgrn_bwd", grid=(bsz, N_HEADS // HEAD_GROUP, nb),
        in_specs=[blk, blk, blk, pl.BlockSpec((1, width), lambda b, h, n: (0, h)),
                  pl.BlockSpec((1, HEAD_GROUP, 1, D_HEAD, D_HEAD), lambda b, h, n: (b, h, nb - 1 - n, 0, 0)),
                  pl.BlockSpec((1, HEAD_GROUP, 1, D_HEAD, D_HEAD), lambda b, h, n: (b, h, nb - 1 - n, 0, 0)),
                  blk, pl.BlockSpec(sums.shape, lambda b, h, n: (0, 0)), pl.BlockSpec(levels.shape, lambda b, h, n: (0, 0))]
        + [hbm] * n_out,
        out_specs=[blk, blk, blk, pl.BlockSpec((1, 1, width), lambda b, h, n: (b, 0, h))] + [hbm] * n_out,
        out_shape=[jax.ShapeDtypeStruct((t, D_MODEL), _MXU)] * 3 + [jax.ShapeDtypeStruct((bsz, 1, D_MODEL), F32)]
        + [jax.ShapeDtypeStruct(a.shape, a.dtype) for a in outgoing],
        scratch_shapes=[pltpu.VMEM((HEAD_GROUP, D_HEAD, D_HEAD), F32), pltpu.VMEM((1, width), F32)]
        + _exchange_scratch(n_out),
        compiler_params=_params(("arbitrary", "arbitrary", "arbitrary")),
    )(qs, fs, vs, lam, sst, sym, do, sums, levels, *outgoing)


def _sb_valid(ahead, col, i, j):
    return (ahead < (i - j) * TILE) & (col >= N_PAD - j * TILE)


def _sb_logits(q16, k_blk, valid):
    z = jnp.where(valid, lax.dot_general(q16, k_blk.astype(_MXU), _NT, preferred_element_type=F32) * SB_SCALE, MASKED)
    softplus = jnp.where(z > SOFTPLUS_LINEAR, z, jnp.log(1.0 + jnp.exp(jnp.minimum(z, SOFTPLUS_LINEAR))))
    return -softplus, z - softplus


def _sb_fwd(qs, ks, vs, bsz, nq):
    t = qs.shape[0]
    lp = nq * TILE
    width = SB_FWD_GROUP * D_HEAD
    groups = N_HEADS // SB_FWD_GROUP
    lanes = [slice(hh * D_HEAD, (hh + 1) * D_HEAD) for hh in range(SB_FWD_GROUP)]

    def body(q_ref, k_ref, v_ref, o_ref, c_ref, n_ref):
        b, h, i = pl.program_id(0), pl.program_id(1), pl.program_id(2)
        q16 = [q_ref[:, ls].astype(_MXU) for ls in lanes]
        r, c = _iota2((TILE, TILE), 0), _iota2((TILE, TILE), 1)
        after = (r > c).astype(_MXU)

        def more(carry):
            jj, _, _, top = carry
            return (jj <= i) & (top > UNDERFLOW)

        def step(carry):
            jj, accs, sums, _ = carry
            j = i - jj
            ks_ = pl.ds(pl.multiple_of(j * TILE, TILE), TILE)
            valid = _sb_valid(c - r, c, i, j)
            new_accs, new_sums = [], []
            for hh, ls in enumerate(lanes):
                keep, log_beta = _sb_logits(q16[hh], k_ref[ks_, ls], valid)
                after_s = _mm01_right(keep, after)
                a = jnp.exp(log_beta + (sums[hh] + after_s))
                new_accs.append(accs[hh] + _mm(a, v_ref[ks_, ls]))
                new_sums.append(sums[hh] + (after_s[:, 0:1] + keep[:, 0:1]))
            top = functools.reduce(jnp.maximum, [jnp.max(x) for x in new_sums])
            return jj + 1, tuple(new_accs), tuple(new_sums), top

        init = (jnp.int32(0), tuple(jnp.zeros((TILE, D_HEAD), F32) for _ in lanes),
                tuple(jnp.zeros((TILE, 1), F32) for _ in lanes), jnp.float32(0.0))
        visited, accs, sums, _ = lax.while_loop(more, step, init)
        for hh, ls in enumerate(lanes):
            o_ref[:, ls] = accs[hh]
            c_ref[:, ls] = jnp.broadcast_to(sums[hh], (TILE, D_HEAD))
        n_ref[(b * groups + h) * nq + i] = visited.astype(F32)

    blk = pl.BlockSpec((TILE, width), lambda b, h, i: (b * nq + i, h))
    seq = pl.BlockSpec((lp, width), lambda b, h, i: (b, h))
    return pl.pallas_call(
        body, name="sb_fwd", grid=(bsz, groups, nq),
        in_specs=[blk, seq, seq], out_specs=[blk, blk, pl.BlockSpec(memory_space=pltpu.SMEM)],
        out_shape=[jax.ShapeDtypeStruct((t, D_MODEL), F32)] * 2 + [jax.ShapeDtypeStruct((bsz * groups * nq,), F32)],
        compiler_params=_params(("arbitrary", "arbitrary", "arbitrary")),
    )(qs, ks, vs)


def _sb_bwd(qs, ks, vs, ctot, visited, do, bsz, nq):
    t = qs.shape[0]
    lp = nq * TILE
    width = SB_FWD_GROUP * D_HEAD
    groups = N_HEADS // SB_FWD_GROUP
    lanes = [slice(hh * D_HEAD, (hh + 1) * D_HEAD) for hh in range(SB_FWD_GROUP)]

    def body(n_ref, q_ref, k_ref, v_ref, c_ref, do_ref, dq_ref, dk_ref, dv_ref, dk_acc, dv_acc):
        b, h, i = pl.program_id(0), pl.program_id(1), pl.program_id(2)

        @pl.when(i == 0)
        def _():
            dk_acc[...] = jnp.zeros_like(dk_acc)
            dv_acc[...] = jnp.zeros_like(dv_acc)

        q16 = [q_ref[:, ls].astype(_MXU) for ls in lanes]
        do16 = [do_ref[:, ls].astype(_MXU) for ls in lanes]
        q16_t = [q_ref[:, ls].astype(F32).T.astype(_MXU) for ls in lanes]
        do16_t = [do_ref[:, ls].astype(F32).T.astype(_MXU) for ls in lanes]
        totals = [c_ref[:, hh * D_HEAD:hh * D_HEAD + 1] for hh in range(SB_FWD_GROUP)]
        r, c = _iota2((TILE, TILE), 0), _iota2((TILE, TILE), 1)
        upto = (r <= c).astype(_MXU)
        before = (r < c).astype(_MXU)
        seen = n_ref[(b * groups + h) * nq + i]
        first = jnp.maximum(i + 1 - seen.astype(jnp.int32), 0)

        def step(j, carry):
            ks_ = pl.ds(pl.multiple_of(j * TILE, TILE), TILE)
            valid = _sb_valid(c - r, c, i, j)
            out = []
            for hh, ls in enumerate(lanes):
                dq, keep_pre, g_pre = carry[hh]
                k_blk, v_blk = k_ref[ks_, ls], v_ref[ks_, ls]
                keep, log_beta = _sb_logits(q16[hh], k_blk, valid)
                keep_upto = _mm01_right(keep, upto)
                a = jnp.exp(log_beta + (totals[hh] - keep_pre - keep_upto))
                da = lax.dot_general(do16[hh], v_blk.astype(_MXU), _NT, preferred_element_type=F32)
                g = a * da
                g_inside = _mm01_right(g, before)
                g_before = g_pre + g_inside
                beta = jnp.exp(log_beta)
                dz16 = (g * (1.0 - beta) - beta * g_before).astype(_MXU)
                dq = dq + jnp.dot(dz16, k_blk.astype(_MXU), preferred_element_type=F32)
                dk_acc[ls, ks_] += SB_SCALE * jnp.dot(q16_t[hh], dz16, preferred_element_type=F32)
                dv_acc[ls, ks_] += jnp.dot(do16_t[hh], a.astype(_MXU), preferred_element_type=F32)
                out.append((dq, keep_pre + keep_upto[:, TILE - 1:TILE],
                            g_pre + (g_inside[:, TILE - 1:TILE] + g[:, TILE - 1:TILE])))
            return tuple(out)

        zero_col = jnp.zeros((TILE, 1), F32)
        init = tuple((jnp.zeros((TILE, D_HEAD), F32), zero_col, zero_col) for _ in lanes)
        res = lax.fori_loop(first, i + 1, step, init)
        for hh, ls in enumerate(lanes):
            dq_ref[:, ls] = (SB_SCALE * res[hh][0]).astype(dq_ref.dtype)

        @pl.when(i == nq - 1)
        def _():
            dk_ref[...] = dk_acc[...].T.astype(dk_ref.dtype)
            dv_ref[...] = dv_acc[...].T.astype(dv_ref.dtype)

    blk = pl.BlockSpec((TILE, width), lambda b, h, i: (b * nq + i, h))
    seq = pl.BlockSpec((lp, width), lambda b, h, i: (b, h))
    seq_out = pl.BlockSpec((lp, width), lambda b, h, i: (b, h), pipeline_mode=pl.Buffered(1))
    return pl.pallas_call(
        body, name="sb_bwd", grid=(bsz, groups, nq),
        in_specs=[pl.BlockSpec(memory_space=pltpu.SMEM), blk, seq, seq, blk, blk], out_specs=[blk, seq_out, seq_out],
        out_shape=[jax.ShapeDtypeStruct((t, D_MODEL), _MXU)] * 3,
        scratch_shapes=[pltpu.VMEM((width, lp), F32)] * 2,
        compiler_params=_params(("arbitrary", "arbitrary", "arbitrary")),
    )(visited, qs, ks, vs, ctot, do)


def _head_norm(o, head_gain):
    outs, rs = [], []
    for h in range(N_HEADS):
        oh = o[:, h * D_HEAD:(h + 1) * D_HEAD]
        r = lax.rsqrt(jnp.mean(oh * oh, axis=-1, keepdims=True) + EPS)
        outs.append(oh * r)
        rs.append(r)
    return outs, rs


def _mix(o, gate, head_gain):
    if head_gain is None:
        on = o
    else:
        outs, _ = _head_norm(o, head_gain)
        on = jnp.concatenate([x * head_gain for x in outs], axis=1)
    return on, on * (gate * _sigmoid(gate))


def _out_fwd(o, gate, h_in, w_out, post_gain, head_gain, name):
    t = o.shape[0]
    tm = _row_tile(t, 2 * TILE)

    def body(o_ref, g_ref, h_ref, w_ref, pg_ref, hg_ref, ho_ref, u_ref):
        _, mix = _mix(o_ref[...], g_ref[...], hg_ref[...])
        u = jnp.dot(mix.astype(_MXU), w_ref[...], preferred_element_type=F32)
        u_ref[...] = u
        r = lax.rsqrt(jnp.mean(u * u, axis=-1, keepdims=True) + EPS)
        ho_ref[...] = h_ref[...] + u * r * pg_ref[...]

    row = pl.BlockSpec((tm, D_MODEL), lambda i: (i, 0))
    vec = pl.BlockSpec((1, D_MODEL), lambda i: (0, 0))
    return pl.pallas_call(
        body, name=name, grid=(t // tm,),
        in_specs=[row, row, row, pl.BlockSpec((D_MODEL, D_MODEL), lambda i: (0, 0)), vec,
                  pl.BlockSpec((1, D_HEAD), lambda i: (0, 0))],
        out_specs=[row, row], out_shape=[jax.ShapeDtypeStruct((t, D_MODEL), F32)] * 2,
        compiler_params=_params(("arbitrary",)),
    )(o, gate, h_in, w_out, post_gain, head_gain)


def _out_fwd_loss(o, gate, h_in, w_out, post_gain, target, nq, name):
    t = o.shape[0]

    def body(o_ref, g_ref, h_ref, w_ref, pg_ref, t_ref, dh_ref, u_ref, l_ref):
        i = pl.program_id(0)

        @pl.when(i == 0)
        def _():
            l_ref[...] = jnp.zeros_like(l_ref)

        _, mix = _mix(o_ref[...], g_ref[...], None)
        u = jnp.dot(mix.astype(_MXU), w_ref[...], preferred_element_type=F32)
        u_ref[...] = u

        @pl.when(i % nq == 0)
        def _():
            dh_ref[...] = jnp.zeros_like(dh_ref)

        @pl.when(i % nq != 0)
        def _():
            r = lax.rsqrt(jnp.mean(u * u, axis=-1, keepdims=True) + EPS)
            e = h_ref[...] + u * r * pg_ref[...] - t_ref[...]
            dh_ref[...] = e * (1.0 / D_MODEL)
            l_ref[...] += jnp.sum(e * e) * (0.5 / D_MODEL)

    row = pl.BlockSpec((TILE, D_MODEL), lambda i: (i, 0))
    vec = pl.BlockSpec((1, D_MODEL), lambda i: (0, 0))
    return pl.pallas_call(
        body, name=name, grid=(t // TILE,),
        in_specs=[row, row, row, pl.BlockSpec((D_MODEL, D_MODEL), lambda i: (0, 0)), vec,
                  pl.BlockSpec((TILE, D_MODEL), lambda i: ((i // nq) * (nq - 1) + jnp.maximum(i % nq - 1, 0), 0))],
        out_specs=[row, row, pl.BlockSpec((8, 128), lambda i: (0, 0))],
        out_shape=[jax.ShapeDtypeStruct((t, D_MODEL), F32)] * 2 + [jax.ShapeDtypeStruct((8, 128), F32)],
        compiler_params=_params(("arbitrary",)),
    )(o, gate, h_in, w_out, post_gain, target)


def _out_bwd(dh, u, o, gate, w_out, post_gain, head_gain, name, narrow_do):
    t = o.shape[0]
    tm = _row_tile(t, 2 * TILE)
    has_head = head_gain is not None

    def body(*refs):
        if has_head:
            dh_ref, u_ref, o_ref, g_ref, w_ref, pg_ref, hg_ref, do_ref, dg_ref, gw_ref, gw16_ref, gp_ref, gh_ref = refs
            hg = hg_ref[...]
        else:
            dh_ref, u_ref, o_ref, g_ref, w_ref, pg_ref, do_ref, dg_ref, gw_ref, gw16_ref, gp_ref = refs
            hg = None
        first = pl.program_id(0) == 0

        @pl.when(first)
        def _():
            gw_ref[...] = jnp.zeros_like(gw_ref)
            gp_ref[...] = jnp.zeros_like(gp_ref)
            if has_head:
                gh_ref[...] = jnp.zeros_like(gh_ref)

        dr, u, o, gate = dh_ref[...], u_ref[...], o_ref[...], g_ref[...]
        r = lax.rsqrt(jnp.mean(u * u, axis=-1, keepdims=True) + EPS)
        un = u * r
        gp_ref[...] += jnp.sum(dr * un, axis=0, keepdims=True)
        dun = dr * pg_ref[...]
        du = r * (dun - un * jnp.mean(dun * un, axis=-1, keepdims=True))
        on, mix = _mix(o, gate, hg)
        du16 = du.astype(_MXU)
        gw_ref[...] += lax.dot_general(mix.astype(_MXU), du16, _TN, preferred_element_type=F32)
        dmix = lax.dot_general(du16, w_ref[...], _NT, preferred_element_type=F32)
        sg = _sigmoid(gate)
        dg_ref[...] = (dmix * on * (sg * (1.0 + gate * (1.0 - sg)))).astype(dg_ref.dtype)
        don = dmix * (gate * sg)
        if has_head:
            outs, rs = _head_norm(o, hg)
            gh = jnp.zeros((1, D_HEAD), F32)
            cols = []
            for h in range(N_HEADS):
                dn = don[:, h * D_HEAD:(h + 1) * D_HEAD]
                gh = gh + jnp.sum(dn * outs[h], axis=0, keepdims=True)
                dnn = dn * hg
                cols.append(rs[h] * (dnn - outs[h] * jnp.mean(dnn * outs[h], axis=-1, keepdims=True)))
            gh_ref[...] += gh
            do_ref[...] = jnp.concatenate(cols, axis=1)
        else:
            do_ref[...] = don.astype(do_ref.dtype)

        @pl.when(pl.program_id(0) == pl.num_programs(0) - 1)
        def _():
            gw16_ref[...] = gw_ref[...].astype(_MXU)

    row = pl.BlockSpec((tm, D_MODEL), lambda i: (i, 0))
    vec = pl.BlockSpec((1, D_MODEL), lambda i: (0, 0))
    mat = pl.BlockSpec((D_MODEL, D_MODEL), lambda i: (0, 0))
    in_specs = [row, row, row, row, mat, vec]
    args = [dh, u, o, gate, w_out, post_gain]
    out_specs = [row, row, mat, mat, vec]
    out_shape = [jax.ShapeDtypeStruct((t, D_MODEL), _MXU if narrow_do else F32),
                 jax.ShapeDtypeStruct((t, D_MODEL), _MXU)] + [jax.ShapeDtypeStruct((D_MODEL, D_MODEL), F32),
                                                                  jax.ShapeDtypeStruct((D_MODEL, D_MODEL), _MXU),
                                                                  jax.ShapeDtypeStruct((1, D_MODEL), F32)]
    if has_head:
        in_specs.append(pl.BlockSpec((1, D_HEAD), lambda i: (0, 0)))
        args.append(head_gain)
        out_specs.append(pl.BlockSpec((1, D_HEAD), lambda i: (0, 0)))
        out_shape.append(jax.ShapeDtypeStruct((1, D_HEAD), F32))
    return pl.pallas_call(
        body, name=name, grid=(t // tm,), in_specs=in_specs, out_specs=out_specs, out_shape=out_shape,
        compiler_params=_params(("arbitrary",)),
    )(*args)


def _proj_bwd(ds, w4, h_in, gain, dh_out, name, outgoing=(), nq=None):
    t = h_in.shape[0]
    tm = _row_tile(t, 2 * TILE) if nq is None else TILE
    n_out = len(outgoing)
    n_dh = 1 if nq is None else 2

    def body(d0, d1, d2, d3, w_ref, h_ref, g_ref, dho_ref, *rest):
        send, dh_refs, gg_ref, rest = rest[:n_out], rest[n_out:n_out + n_dh], rest[n_out + n_dh], rest[n_out + n_dh + 1:]
        landed, sems = rest[:n_out], rest[n_out:]
        if n_out:
            first, last = _grid_ends(1)
            start, finish = _chip_exchange(send, landed, sems, slotted=True)
            pl.when(first)(start)

        @pl.when(pl.program_id(0) == 0)
        def _():
            gg_ref[...] = jnp.zeros_like(gg_ref)

        dy = jnp.zeros((tm, D_MODEL), F32)
        for j, d in enumerate((d0, d1, d2, d3)):
            dy = dy + lax.dot_general(d[...].astype(_MXU), w_ref[j], _NT, preferred_element_type=F32)
        x = h_ref[...]
        r = lax.rsqrt(jnp.mean(x * x, axis=-1, keepdims=True) + EPS)
        xn = x * r
        gg_ref[...] += jnp.sum(dy * xn, axis=0, keepdims=True)
        dxn = dy * g_ref[...]
        dh = dho_ref[...] + r * (dxn - xn * jnp.mean(dxn * xn, axis=-1, keepdims=True))
        if nq is None:
            dh_refs[0][...] = dh
        else:
            in_front = pl.program_id(0) % nq == 0

            @pl.when(in_front)
            def _():
                dh_refs[1][...] = dh

            @pl.when(jnp.logical_not(in_front))
            def _():
                dh_refs[0][...] = dh
        if n_out:
            pl.when(last)(finish)

    row = pl.BlockSpec((tm, D_MODEL), lambda i: (i, 0))
    vec = pl.BlockSpec((1, D_MODEL), lambda i: (0, 0))
    hbm = pl.BlockSpec(memory_space=pl.ANY)
    if nq is None:
        dh_specs, dh_shapes = [row], [jax.ShapeDtypeStruct((t, D_MODEL), F32)]
    else:
        dh_specs = [pl.BlockSpec((TILE, D_MODEL), lambda i: ((i // nq) * (nq - 1) + jnp.maximum(i % nq - 1, 0), 0)),
                    pl.BlockSpec((TILE, D_MODEL), lambda i: (i // nq, 0))]
        dh_shapes = [jax.ShapeDtypeStruct((t // nq * (nq - 1), D_MODEL), F32),
                     jax.ShapeDtypeStruct((t // nq, D_MODEL), F32)]
    return pl.pallas_call(
        body, name=name, grid=(t // tm,),
        in_specs=[row] * 4 + [pl.BlockSpec((4, D_MODEL, D_MODEL), lambda i: (0, 0, 0)), row, vec, row] + [hbm] * n_out,
        out_specs=dh_specs + [vec] + [hbm] * n_out,
        out_shape=dh_shapes + [jax.ShapeDtypeStruct((1, D_MODEL), F32)]
        + [jax.ShapeDtypeStruct(a.shape, a.dtype) for a in outgoing],
        scratch_shapes=_exchange_scratch(n_out) if n_out else [],
        compiler_params=_params(("arbitrary",)),
    )(*ds, w4, h_in, gain, dh_out, *outgoing)


def _weight_grad(y, d, name):
    t = y.shape[0]
    tk = _row_tile(t, t // 4)

    def body(y_ref, d_ref, g_ref, g16_ref):
        @pl.when(pl.program_id(0) == 0)
        def _():
            g_ref[...] = jnp.zeros_like(g_ref)

        g_ref[...] += lax.dot_general(y_ref[...], d_ref[...].astype(_MXU), _TN, preferred_element_type=F32)

        @pl.when(pl.program_id(0) == pl.num_programs(0) - 1)
        def _():
            g16_ref[...] = g_ref[...].astype(_MXU)

    row = pl.BlockSpec((tk, D_MODEL), lambda i: (i, 0))
    mat = pl.BlockSpec((D_MODEL, D_MODEL), lambda i: (0, 0))
    return pl.pallas_call(
        body, name=name, grid=(t // tk,), in_specs=[row, row], out_specs=[mat, mat],
        out_shape=[jax.ShapeDtypeStruct((D_MODEL, D_MODEL), F32), jax.ShapeDtypeStruct((D_MODEL, D_MODEL), _MXU)],
        compiler_params=_params(("arbitrary",)),
    )(y, d)


def _local_step(x, target, meta, pre_norm, post_norm, lam, head_gain, hw_in, shards):
    bsz, seq, _ = x.shape
    nq = seq // TILE + 1
    nb = nq * (TILE // BLOCK)
    lp = nq * TILE
    t = bsz * lp
    d4 = D_MODEL // N_CHIPS
    front = jnp.concatenate([jnp.zeros((N_PAD, D_MODEL), F32), meta], axis=0)
    h0 = jnp.concatenate([jnp.broadcast_to(front[None], (bsz, TILE, D_MODEL)), x], axis=1).reshape(t, D_MODEL)
    pre0, pre1, post0, post1 = pre_norm[0:1], pre_norm[1:2], post_norm[0:1], post_norm[1:2]

    y0, q0, f0, v0, g0 = _norm_proj(h0, pre0, hw_in, "norm_proj_hgrn", (False,) * 4)
    o0, sst, sym, sw_in, sw_out, hw_out = _hgrn_fwd(q0, f0, v0, lam, bsz, nb, shards)
    sw_out, hw_out = sw_out.reshape(D_MODEL, D_MODEL), hw_out.reshape(D_MODEL, D_MODEL)
    h1, u0 = _out_fwd(o0, g0, h0, hw_out, post0, head_gain, "out_fwd_hgrn")
    y1, q1, k1, v1, g1 = _norm_proj(h1, pre1, sw_in, "norm_proj_sb", (True, True, True, False))
    o1, ctot, visited = _sb_fwd(q1, k1, v1, bsz, nq)
    dh2, u1, loss_blk = _out_fwd_loss(o1, g1, h1, sw_out, post1, target.reshape(bsz * seq, D_MODEL), nq, "out_fwd_sb")

    do1, dg1, g_sw_out, g_sw_out16, g_post1 = _out_bwd(dh2, u1, o1, g1, sw_out, post1, None, "out_bwd_sb", True)
    dq1, dk1, dv1 = _sb_bwd(q1, k1, v1, ctot, visited, do1, bsz, nq)
    ds1 = (dq1, dk1, dv1, dg1)
    dh1, g_pre1 = _proj_bwd(ds1, sw_in, h1, pre1, dh2, "proj_bwd_sb")
    g_sw_in = [_weight_grad(y1, d, "wgrad_sb_%d" % j) for j, d in enumerate(ds1)]

    do0, dg0, g_hw_out, g_hw_out16, g_post0, g_head = _out_bwd(dh1, u0, o0, g0, hw_out, post0, head_gain, "out_bwd_hgrn",
                                                               False)
    ready = (jnp.stack([g16 for _, g16 in g_sw_in]), g_sw_out16.reshape(N_CHIPS, d4, D_MODEL),
             g_hw_out16.reshape(N_CHIPS, d4, D_MODEL))
    dq0, df0, dv0, dlam, land_sw_in, land_sw_out, land_hw_out = _hgrn_bwd(q0, f0, v0, lam, sst, sym, do0, bsz, nb, ready)
    ds0 = (dq0, df0, dv0, dg0)
    g_hw_in = [_weight_grad(y0, d, "wgrad_hgrn_%d" % j) for j, d in enumerate(ds0)]
    last = (jnp.stack([g16 for _, g16 in g_hw_in]),)
    grad_x, dh_front, g_pre0, land_hw_in = _proj_bwd(ds0, hw_in, h0, pre0, dh1, "proj_bwd_hgrn", last, nq)

    grad_x = grad_x.reshape(bsz, seq, D_MODEL)
    g_meta = jnp.sum(dh_front.reshape(bsz, TILE, D_MODEL)[:, N_PAD:, :], axis=0)
    g_lam = jnp.sum(dlam, axis=0)
    small = jnp.concatenate([g_pre0, g_pre1, g_post0, g_post1, g_lam, g_lam,
                             jnp.pad(g_head, ((0, 0), (0, D_MODEL - D_HEAD))), g_meta,
                             jnp.pad(loss_blk[0:1], ((0, 0), (0, D_MODEL - loss_blk.shape[1])))], axis=0)
    rows4 = lambda g: [g[j * d4:(j + 1) * d4] for j in range(N_CHIPS)]
    large = dict(hw_in=(land_hw_in, [g for g, _ in g_hw_in]), sw_in=(land_sw_in, [g for g, _ in g_sw_in]),
                 hw_out=(land_hw_out, rows4(g_hw_out)), sw_out=(land_sw_out, rows4(g_sw_out)))
    return grad_x, small, large


def _prep_weights(hw_in, sw_in, hw_out, sw_out, meta):
    def body(hi_ref, si_ref, ho_ref, so_ref, m_ref, ghi, gm, si16, so16, ho16, far_send, far_recv, near_send, near_recv):
        x, y, c = _place()
        me = 2 * x + y
        ghi[me] = hi_ref[0].astype(_MXU)
        gm[me] = m_ref[...]
        si16[...] = si_ref[0].astype(_MXU)
        so16[...] = so_ref[0].astype(_MXU)
        ho16[...] = ho_ref[0].astype(_MXU)
        outs = (ghi, gm)
        n = len(outs)
        peers = [(1 - x, y), (x, 1 - y), (1 - x, 1 - y)]

        def half(a, slot, which):
            rows = outs[a].shape[1] // 2
            return outs[a].at[slot, pl.ds(which * rows, rows), :]

        def far(r, a, slot):
            px, py = peers[r]
            return pltpu.make_async_remote_copy(
                src_ref=half(a, slot, c), dst_ref=half(a, slot, c), send_sem=far_send.at[r * n + a],
                recv_sem=far_recv.at[r * n + a], device_id=(px, py, c), device_id_type=MESH)

        def near(r, a, which):
            px, py = peers[r]
            return pltpu.make_async_remote_copy(
                src_ref=half(a, 2 * px + py, which), dst_ref=half(a, 2 * px + py, which),
                send_sem=near_send.at[r * n + a], recv_sem=near_recv.at[r * n + a],
                device_id=(x, y, 1 - c), device_id_type=MESH)

        for r in range(3):
            for a in range(n):
                far(r, a, me).start()
        for r, (px, py) in enumerate(peers):
            for a in range(n):
                far(r, a, 2 * px + py).wait_recv()
                near(r, a, c).start()
        for r in range(3):
            for a in range(n):
                near(r, a, 1 - c).wait_recv()
        for r in range(3):
            for a in range(n):
                far(r, a, me).wait_send()
                near(r, a, c).wait_send()

    d4 = D_MODEL // N_CHIPS
    vm = pl.BlockSpec(memory_space=pltpu.VMEM)
    return pl.pallas_call(
        body, name="prep_weights",
        in_specs=[vm] * 5, out_specs=[vm] * 5,
        out_shape=[jax.ShapeDtypeStruct((N_CHIPS, D_MODEL, D_MODEL), _MXU), jax.ShapeDtypeStruct((N_CHIPS, N_META, d4), F32),
                   jax.ShapeDtypeStruct((D_MODEL, D_MODEL), _MXU), jax.ShapeDtypeStruct((d4, D_MODEL), _MXU),
                   jax.ShapeDtypeStruct((d4, D_MODEL), _MXU)],
        scratch_shapes=[pltpu.SemaphoreType.DMA((6,))] * 4,
        compiler_params=pltpu.CompilerParams(vmem_limit_bytes=VMEM_LIMIT),
    )(hw_in, sw_in, hw_out, sw_out, meta)


def _scatter_small(small):
    def body(sm, lsm, send_sems, recv_sems, local_sem):
        x, y, c = _place()
        mine = 4 * x + 2 * y + c
        local = pltpu.make_async_copy(sm, lsm.at[mine], local_sem)
        local.start()

        def copy(rel, src_dev, to):
            return pltpu.make_async_remote_copy(
                src_ref=sm, dst_ref=lsm.at[src_dev], send_sem=send_sems.at[rel - 1], recv_sem=recv_sems.at[rel - 1],
                device_id=to, device_id_type=MESH)

        flip = lambda bit, v: 1 - v if bit else v
        rels = [(rel, flip(rel & 4, x), flip(rel & 2, y), flip(rel & 1, c)) for rel in range(1, N_DEV)]
        sends = [copy(rel, mine, (px, py, pc)) for rel, px, py, pc in rels]
        for cp in sends:
            cp.start()
        for rel, px, py, pc in rels:
            copy(rel, 4 * px + 2 * py + pc, (px, py, pc)).wait_recv()
        for cp in sends:
            cp.wait_send()
        local.wait()

    hbm = pl.BlockSpec(memory_space=pl.ANY)
    return pl.pallas_call(
        body, name="scatter_small", in_specs=[hbm], out_specs=hbm,
        out_shape=jax.ShapeDtypeStruct((N_DEV, SMALL_ROWS, D_MODEL), F32),
        scratch_shapes=[pltpu.SemaphoreType.DMA((N_DEV - 1,)), pltpu.SemaphoreType.DMA((N_DEV - 1,)),
                        pltpu.SemaphoreType.DMA(())],
    )(small)


def _sum_slots(landed, own, me, name):
    n, rows, _ = landed.shape
    tm = rows if rows < 256 else 256

    def body(me_ref, l_ref, o0, o1, o2, o3, out_ref):
        acc = None
        for k, o in enumerate((o0, o1, o2, o3)):
            term = jnp.where(me_ref[0] == k, o[...], l_ref[k].astype(F32))
            acc = term if acc is None else acc + term
        out_ref[...] = acc

    blk = pl.BlockSpec((tm, D_MODEL), lambda i: (i, 0))
    return pl.pallas_call(
        body, name=name, grid=(rows // tm,),
        in_specs=[pl.BlockSpec(memory_space=pltpu.SMEM), pl.BlockSpec((n, tm, D_MODEL), lambda i: (0, i, 0))] + [blk] * 4,
        out_specs=blk, out_shape=jax.ShapeDtypeStruct((rows, D_MODEL), F32),
        compiler_params=_params(("arbitrary",)),
    )(me, landed, *own)


def _swap_with_sibling(parts):
    def body(a0, a1, a2, a3, b0, b1, b2, b3, send_sems, recv_sems):
        x, y, c = _place()
        copies = [pltpu.make_async_remote_copy(src_ref=s, dst_ref=d, send_sem=send_sems.at[a], recv_sem=recv_sems.at[a],
                                               device_id=(x, y, 1 - c), device_id_type=MESH)
                  for a, (s, d) in enumerate(zip((a0, a1, a2, a3), (b0, b1, b2, b3)))]
        for cp in copies:
            cp.start()
        for cp in copies:
            cp.wait()

    hbm = pl.BlockSpec(memory_space=pl.ANY)
    return pl.pallas_call(
        body, name="swap_with_sibling", in_specs=[hbm] * 4, out_specs=[hbm] * 4,
        out_shape=[jax.ShapeDtypeStruct(p.shape, F32) for p in parts],
        scratch_shapes=[pltpu.SemaphoreType.DMA((4,)), pltpu.SemaphoreType.DMA((4,))],
    )(*parts)


def _adamw_math(w, g, m, v):
    m = ADAM_B1 * m + (1.0 - ADAM_B1) * g
    v = ADAM_B2 * v + (1.0 - ADAM_B2) * (g * g)
    m_hat = m / (1.0 - ADAM_B1 ** ADAM_STEP)
    v_hat = v / (1.0 - ADAM_B2 ** ADAM_STEP)
    delta = -ADAM_LR * (m_hat / (jnp.sqrt(v_hat) + ADAM_EPS) + ADAM_WD * w)
    return delta, m, v


def _adamw(w, g_parts, m, v, name):
    rows, cols = w.shape
    tm = rows if rows < 256 else 256
    n = len(g_parts)

    def body(*refs):
        w_ref, m_ref, v_ref = refs[n:n + 3]
        g_ref, d_ref, nm_ref, nv_ref = refs[n + 3:]
        g = refs[0][...]
        for p in refs[1:n]:
            g = g + p[...]
        g_ref[...] = g
        d_ref[...], nm_ref[...], nv_ref[...] = _adamw_math(w_ref[...], g, m_ref[...], v_ref[...])

    blk = pl.BlockSpec((tm, cols), lambda i: (i, 0))
    return pl.pallas_call(
        body, name=name, grid=(rows // tm,), in_specs=[blk] * (n + 3), out_specs=[blk] * 4,
        out_shape=[jax.ShapeDtypeStruct((rows, cols), F32)] * 4,
        compiler_params=_params(("arbitrary",)),
    )(*g_parts, w, m, v)


def _lam_of(hgrn_lb):
    def body(lb_ref, o_ref):
        lb = lb_ref[...]
        e = jnp.exp(lb - jnp.max(lb, axis=0, keepdims=True))
        o_ref[...] = e[0:1, :] / jnp.sum(e, axis=0, keepdims=True)

    return pl.pallas_call(body, name="lam_of", out_shape=jax.ShapeDtypeStruct((1, D_MODEL), F32))(hgrn_lb)


def _small_grads(land_small, lam):
    def body(l_ref, lam_ref, o_ref):
        acc = l_ref[0]
        for k in range(1, N_DEV):
            acc = acc + l_ref[k]
        p = lam_ref[...]
        slope = p * (1.0 - p)
        row = _iota2((SMALL_ROWS, D_MODEL), 0)
        o_ref[...] = acc * jnp.where(row == 4, slope, jnp.where(row == 5, -slope, 1.0))

    return pl.pallas_call(body, name="small_grads",
                          out_shape=jax.ShapeDtypeStruct((SMALL_ROWS, D_MODEL), F32))(land_small, lam)


def kernel(x, meta_tokens, pre_norm, post_norm, hgrn_w_in, hgrn_lb, hgrn_out_norm, hgrn_w_out, sb_w_in, sb_w_out, loss_target, m_meta_tokens, m_pre_norm, m_post_norm, m_hgrn_w_in, m_hgrn_lb, m_hgrn_out_norm, m_hgrn_w_out, m_sb_w_in, m_sb_w_out, v_meta_tokens, v_pre_norm, v_post_norm, v_hgrn_w_in, v_hgrn_lb, v_hgrn_out_norm, v_hgrn_w_out, v_sb_w_in, v_sb_w_out):
    d4 = D_MODEL // N_CHIPS
    chip = 2 * lax.axis_index("x") + lax.axis_index("y")
    hw_in, meta4, sw_in16, sw_out16, hw_out16 = _prep_weights(hgrn_w_in, sb_w_in, hgrn_w_out, sb_w_out, meta_tokens)
    meta = meta4.transpose(1, 0, 2).reshape(N_META, D_MODEL)
    lam = _lam_of(hgrn_lb)
    grad_x, small, large = _local_step(
        x, loss_target, meta, pre_norm, post_norm, lam, hgrn_out_norm,
        hw_in, (sw_in16, sw_out16, hw_out16))

    me = jnp.reshape(chip, (1,)).astype(jnp.int32)
    parts = [_sum_slots(*large[n], me, "sum_" + n) for n in ("hw_in", "sw_in", "hw_out", "sw_out")]
    sib = _swap_with_sibling(parts)
    small = _small_grads(_scatter_small(small), lam)
    loss = small[SMALL_ROWS - 1, 0]

    res = {}
    res["hgrn_w_in"] = _adamw(hgrn_w_in[0], [parts[0], sib[0]], m_hgrn_w_in[0], v_hgrn_w_in[0], "adamw_hw_in")
    res["sb_w_in"] = _adamw(sb_w_in[0], [parts[1], sib[1]], m_sb_w_in[0], v_sb_w_in[0], "adamw_sw_in")
    res["hgrn_w_out"] = _adamw(hgrn_w_out[0], [parts[2], sib[2]], m_hgrn_w_out[0], v_hgrn_w_out[0], "adamw_hw_out")
    res["sb_w_out"] = _adamw(sb_w_out[0], [parts[3], sib[3]], m_sb_w_out[0], v_sb_w_out[0], "adamw_sw_out")
    res["pre_norm"] = _adamw(pre_norm, [small[0:2]], m_pre_norm, v_pre_norm, "adamw_pre")
    res["post_norm"] = _adamw(post_norm, [small[2:4]], m_post_norm, v_post_norm, "adamw_post")
    res["hgrn_lb"] = _adamw(hgrn_lb, [small[4:6]], m_hgrn_lb, v_hgrn_lb, "adamw_lb")
    res["hgrn_out_norm"] = _adamw(hgrn_out_norm, [small[6:7, :D_HEAD]], m_hgrn_out_norm, v_hgrn_out_norm, "adamw_head")
    g_meta = lax.dynamic_slice_in_dim(small[7:7 + N_META], chip * d4, d4, axis=1)
    res["meta_tokens"] = _adamw(meta_tokens, [g_meta], m_meta_tokens, v_meta_tokens, "adamw_meta")
    for n in ("hgrn_w_in", "hgrn_w_out", "sb_w_in", "sb_w_out"):
        res[n] = tuple(a[None] for a in res[n])
    order = ("meta_tokens", "pre_norm", "post_norm", "hgrn_w_in", "hgrn_lb", "hgrn_out_norm", "hgrn_w_out",
             "sb_w_in", "sb_w_out")
    return (loss, grad_x, *[res[n][0] for n in order], *[res[n][1] for n in order],
            *[res[n][2] for n in order], *[res[n][3] for n in order])
```

```python
import functools

import jax
import numpy as np
import jax.numpy as jnp
from jax import lax
from jax.experimental import pallas as pl
from jax.experimental.pallas import tpu as pltpu

F32 = jnp.float32
_MXU = jnp.bfloat16

D_MODEL = 1024
N_HEADS = 8
D_HEAD = 128
BLOCK = 128
N_META = 16
TILE = 256
N_PAD = TILE - N_META
UNDERFLOW = -105.0
EPS = 1e-6
SB_SCALE = D_HEAD ** -0.5
SOFTPLUS_LINEAR = 20.0
MASKED = -1e30
ADAM_LR, ADAM_B1, ADAM_B2, ADAM_EPS, ADAM_WD, ADAM_STEP = 0.001, 0.9, 0.999, 1e-08, 0.01, 10
N_CHIPS = 4
N_DEV = 8
SMALL_ROWS = 24
VMEM_LIMIT = 56 * 1024 * 1024
MESH = pl.DeviceIdType.MESH

_NT = (((1,), (1,)), ((), ()))
_TN = (((0,), (0,)), ((), ()))


def _mm(a, b):
    return jnp.dot(a.astype(_MXU), b.astype(_MXU), preferred_element_type=F32)


def _mm_nt(a, b):
    return lax.dot_general(a.astype(_MXU), b.astype(_MXU), _NT, preferred_element_type=F32)


def _mm_tn(a, b):
    return lax.dot_general(a.astype(_MXU), b.astype(_MXU), _TN, preferred_element_type=F32)


def _split2(x):
    hi = x.astype(_MXU)
    return hi, (x - hi.astype(F32)).astype(_MXU)


def _mm_s(a16, state):
    hi, lo = _split2(state)
    return jnp.dot(a16, hi, preferred_element_type=F32) + jnp.dot(a16, lo, preferred_element_type=F32)


def _mm_nt_s(a16, state):
    hi, lo = _split2(state)
    return (lax.dot_general(a16, hi, _NT, preferred_element_type=F32)
            + lax.dot_general(a16, lo, _NT, preferred_element_type=F32))


def _mm01_right(x, m01):
    return jnp.dot(x.astype(_MXU), m01, preferred_element_type=F32)


def _mm01_left(m01, x):
    hi, lo = _split2(x)
    return jnp.dot(m01, hi, preferred_element_type=F32) + jnp.dot(m01, lo, preferred_element_type=F32)


def _iota2(shape, dim):
    return lax.broadcasted_iota(jnp.int32, shape, dim)


def _row_tile(total, pref):
    t = pref
    while total % t:
        t -= BLOCK
    return t


def _params(sem, limit=VMEM_LIMIT):
    return pltpu.CompilerParams(dimension_semantics=sem, vmem_limit_bytes=limit)


def _sigmoid(x):
    return 1.0 / (1.0 + jnp.exp(-x))


def _grid_ends(ndim):
    first, last = True, True
    for d in range(ndim):
        first = first & (pl.program_id(d) == 0)
        last = last & (pl.program_id(d) == pl.num_programs(d) - 1)
    return first, last


def _place():
    return lax.axis_index("x"), lax.axis_index("y"), lax.axis_index("c")


def _exchange_scratch(n):
    return [pltpu.SemaphoreType.DMA((3 * n,)), pltpu.SemaphoreType.DMA((3 * n,)), pltpu.SemaphoreType.DMA((n,))]


def _chip_exchange(srcs, dsts, sems, slotted):
    send_sems, recv_sems, local_sems = sems
    x, y, c = _place()
    me = 2 * x + y
    peers = [(1 - x, y), (x, 1 - y), (1 - x, 1 - y)]
    n = len(dsts)

    def remote(r, a, sending):
        px, py = peers[r]
        p = 2 * px + py
        return pltpu.make_async_remote_copy(
            src_ref=srcs[a].at[p] if slotted else srcs[a], dst_ref=dsts[a].at[me if sending else p],
            send_sem=send_sems.at[r * n + a], recv_sem=recv_sems.at[r * n + a],
            device_id=(px, py, c), device_id_type=MESH)

    def local(a):
        return pltpu.make_async_copy(srcs[a].at[me] if slotted else srcs[a], dsts[a].at[me], local_sems.at[a])

    def start():
        for a in range(n):
            local(a).start()
        for r in range(3):
            for a in range(n):
                remote(r, a, True).start()

    def finish():
        for r in range(3):
            for a in range(n):
                remote(r, a, False).wait_recv()
        for r in range(3):
            for a in range(n):
                remote(r, a, True).wait_send()
        for a in range(n):
            local(a).wait()

    return start, finish


def _norm_proj(h, gain, w4, name, narrow):
    t = h.shape[0]
    tm = _row_tile(t, 2 * TILE)

    def body(h_ref, g_ref, w_ref, y_ref, s0, s1, s2, s3):
        x = h_ref[...]
        r = lax.rsqrt(jnp.mean(x * x, axis=-1, keepdims=True) + EPS)
        y = (x * r * g_ref[...]).astype(_MXU)
        y_ref[...] = y
        for j, s in enumerate((s0, s1, s2, s3)):
            s[...] = jnp.dot(y, w_ref[j], preferred_element_type=F32).astype(s.dtype)

    row = pl.BlockSpec((tm, D_MODEL), lambda i: (i, 0))
    return pl.pallas_call(
        body, name=name, grid=(t // tm,),
        in_specs=[row, pl.BlockSpec((1, D_MODEL), lambda i: (0, 0)),
                  pl.BlockSpec((4, D_MODEL, D_MODEL), lambda i: (0, 0, 0))],
        out_specs=[row] * 5,
        out_shape=[jax.ShapeDtypeStruct((t, D_MODEL), _MXU)]
        + [jax.ShapeDtypeStruct((t, D_MODEL), _MXU if n else F32) for n in narrow],
        compiler_params=_params(("arbitrary",)),
    )(h, gain, w4)


LEVELS = (64, 32, 16, 8, 4, 2, 1)
HEAD_GROUP = 8
SB_FWD_GROUP = 8
SB_BWD_GROUP = 4


def _hgrn_tables():
    r = np.arange(BLOCK)
    mats = [r[None, :] <= r[:, None]]
    x = r[:, None] ^ r[None, :]
    lv = np.full((BLOCK, BLOCK), len(LEVELS), np.int32)
    for i, m in enumerate(LEVELS):
        lv[(x >= m) & (x < 2 * m)] = i
    return jnp.asarray(np.concatenate(mats, 0).astype(np.float32), dtype=_MXU), jnp.asarray(lv)


def _hgrn_exponents(g, sums):
    b = _mm01_left(sums, g)
    row = _iota2((BLOCK, D_HEAD), 0)
    out = []
    for m in LEVELS:
        is_q = (row & m) != 0
        if m >= 4:
            grp = b.reshape(BLOCK // (2 * m), 2 * m, D_HEAD)
            ref = jnp.broadcast_to(grp[:, m - 1:m, :], grp.shape).reshape(BLOCK, D_HEAD)
            d = b - ref
            out.append(jnp.where(is_q, d, -d))
        elif m == 2:
            below, above = pltpu.roll(g, 1, axis=0), pltpu.roll(g, BLOCK - 1, axis=0)
            low = row & 3
            out.append(jnp.where(low == 3, g + below, jnp.where(low == 2, g, jnp.where(low == 0, above, 0.0))))
        else:
            out.append(jnp.where(is_q, g, 0.0))
    return b, out


def _hgrn_gates(fz, lam, chunk):
    pos = chunk * BLOCK + _iota2((BLOCK, D_HEAD), 0)
    live = pos >= N_PAD
    sg = _sigmoid(fz)
    f = lam + (1.0 - lam) * sg
    g = jnp.where(live, jnp.log(f), 0.0)
    k = jnp.where(live, (1.0 - lam) * (1.0 - sg), 0.0)
    return sg, f, g, k, live


def _level_operand(q, k, exponent, m):
    decay = jnp.exp(exponent)
    is_q = (_iota2((BLOCK, D_HEAD), 0) & m) != 0
    return is_q, decay, (jnp.where(is_q, q, k) * decay).astype(_MXU)


def _hgrn_fwd(qs, fs, vs, lam, bsz, nb, shards):
    t = qs.shape[0]
    sums, levels = _hgrn_tables()
    width = HEAD_GROUP * D_HEAD

    n_sh = len(shards)

    def body(q_ref, f_ref, v_ref, lam_ref, sums_ref, lv_ref, *rest):
        own, (o_ref, sst_ref, sym_ref), rest = rest[:n_sh], rest[n_sh:n_sh + 3], rest[n_sh + 3:]
        gathered, st_scr, sems = rest[:n_sh], rest[n_sh], rest[n_sh + 1:]
        n = pl.program_id(2)
        first, last = _grid_ends(3)
        start, finish = _chip_exchange(own, gathered, sems, slotted=False)
        pl.when(first)(start)

        @pl.when(n == 0)
        def _():
            st_scr[...] = jnp.zeros_like(st_scr)

        lv = lv_ref[...]
        r, c = _iota2((BLOCK, BLOCK), 0), _iota2((BLOCK, BLOCK), 1)
        for hh in range(HEAD_GROUP):
            ls = slice(hh * D_HEAD, (hh + 1) * D_HEAD)
            st = st_scr[hh]
            sst_ref[0, hh, 0] = st
            q, v = q_ref[:, ls], v_ref[:, ls]
            _, _, g, k, _ = _hgrn_gates(f_ref[:, ls], lam_ref[:, ls], n)
            b, exps = _hgrn_exponents(g, sums_ref[...])
            sym = jnp.zeros((BLOCK, BLOCK), F32)
            for li, m in enumerate(LEVELS):
                _, _, x16 = _level_operand(q, k, exps[li], m)
                sym = jnp.where(lv == li, lax.dot_general(x16, x16, _NT, preferred_element_type=F32), sym)
            sym = jnp.where(c == r, jnp.sum(q * k, axis=1, keepdims=True), sym).astype(_MXU)
            sym_ref[0, hh, 0] = sym
            o_ref[:, ls] = _mm_nt(q * jnp.exp(b), st) + _mm(jnp.where(c <= r, sym, 0), v)
            b_end = b[BLOCK - 1:BLOCK, :]
            st_scr[hh] = st * jnp.exp(b_end) + _mm_tn(v, k * jnp.exp(b_end - b))
        pl.when(last)(finish)

    blk = pl.BlockSpec((BLOCK, width), lambda b, h, n: (b * nb + n, h))
    hbm = pl.BlockSpec(memory_space=pl.ANY)
    return pl.pallas_call(
        body, name="hgrn_fwd", grid=(bsz, N_HEADS // HEAD_GROUP, nb),
        in_specs=[blk, blk, blk, pl.BlockSpec((1, width), lambda b, h, n: (0, h)),
                  pl.BlockSpec(sums.shape, lambda b, h, n: (0, 0)), pl.BlockSpec(levels.shape, lambda b, h, n: (0, 0))]
        + [hbm] * n_sh,
        out_specs=[blk] + [pl.BlockSpec((1, HEAD_GROUP, 1, D_HEAD, D_HEAD), lambda b, h, n: (b, h, n, 0, 0))] * 2
        + [hbm] * n_sh,
        out_shape=[jax.ShapeDtypeStruct((t, D_MODEL), F32),
                   jax.ShapeDtypeStruct((bsz, N_HEADS, nb, D_HEAD, D_HEAD), F32),
                   jax.ShapeDtypeStruct((bsz, N_HEADS, nb, D_HEAD, D_HEAD), _MXU)]
        + [jax.ShapeDtypeStruct((N_CHIPS,) + a.shape, a.dtype) for a in shards],
        scratch_shapes=[pltpu.VMEM((HEAD_GROUP, D_HEAD, D_HEAD), F32)] + _exchange_scratch(n_sh),
        compiler_params=_params(("arbitrary", "arbitrary", "arbitrary")),
    )(qs, fs, vs, lam, sums, levels, *shards)


def _hgrn_bwd(qs, fs, vs, lam, sst, sym, do, bsz, nb, outgoing):
    t = qs.shape[0]
    sums, levels = _hgrn_tables()
    width = HEAD_GROUP * D_HEAD

    n_out = len(outgoing)

    def body(q_ref, f_ref, v_ref, lam_ref, sst_ref, sym_ref, do_ref, sums_ref, lv_ref, *rest):
        send, (dq_ref, df_ref, dv_ref, dlam_ref), rest = rest[:n_out], rest[n_out:n_out + 4], rest[n_out + 4:]
        landed, dst_scr, gsum_scr, sems = rest[:n_out], rest[n_out], rest[n_out + 1], rest[n_out + 2:]
        n = pl.program_id(2)
        chunk = nb - 1 - n
        first, last = _grid_ends(3)
        start, finish = _chip_exchange(send, landed, sems, slotted=True)
        pl.when(first)(start)

        @pl.when(n == 0)
        def _():
            dst_scr[...] = jnp.zeros_like(dst_scr)
            gsum_scr[...] = jnp.zeros_like(gsum_scr)
            dlam_ref[...] = jnp.zeros_like(dlam_ref)

        lv = lv_ref[...]
        r, c = _iota2((BLOCK, BLOCK), 0), _iota2((BLOCK, BLOCK), 1)
        for hh in range(HEAD_GROUP):
            ls = slice(hh * D_HEAD, (hh + 1) * D_HEAD)
            lam = lam_ref[:, ls]
            q, v, do = q_ref[:, ls], v_ref[:, ls], do_ref[:, ls]
            sg, f, g, k, live = _hgrn_gates(f_ref[:, ls], lam, chunk)
            b, exps = _hgrn_exponents(g, sums_ref[...])
            do16, v16 = do.astype(_MXU), v.astype(_MXU)
            da = lax.dot_general(do16, v16, _NT, preferred_element_type=F32)
            da_sym = jnp.where(c < r, da, da.T)
            dq = jnp.zeros((BLOCK, D_HEAD), F32)
            dqk = jnp.zeros((BLOCK, D_HEAD), F32)
            db_q = jnp.zeros((BLOCK, D_HEAD), F32)
            db_qk = jnp.zeros((BLOCK, D_HEAD), F32)
            for li, m in enumerate(LEVELS):
                is_q, decay, x16 = _level_operand(q, k, exps[li], m)
                y = jnp.dot(jnp.where(lv == li, da_sym, 0.0).astype(_MXU), x16, preferred_element_type=F32)
                dx = y * decay
                dq = dq + jnp.where(is_q, dx, 0.0)
                dqk = dqk + dx
                p = x16.astype(F32) * y
                db_q = db_q + jnp.where(is_q, p, 0.0)
                db_qk = db_qk + p
            dk = dqk - dq
            db = 2.0 * db_q - db_qk
            a_t = jnp.where(c >= r, sym_ref[0, hh, 0], 0)
            st, dst = sst_ref[0, hh, 0], dst_scr[hh]
            eb = jnp.exp(b)
            b_end = b[BLOCK - 1:BLOCK, :]
            dec = jnp.exp(b_end - b)
            qh16, kt16 = (q * eb).astype(_MXU), (k * dec).astype(_MXU)
            dq_st = _mm_s(do16, st)
            dk_st = _mm_s(v16, dst)
            d_diag = jnp.sum(do * v, axis=1, keepdims=True)
            dq_ref[:, ls] = (dq + d_diag * k + eb * dq_st).astype(dq_ref.dtype)
            dk = dk + d_diag * q + dec * dk_st
            dv_ref[:, ls] = (jnp.dot(a_t, do16, preferred_element_type=F32) + _mm_nt_s(kt16, dst)).astype(dv_ref.dtype)
            dst_scr[hh] = dst * jnp.exp(b_end) + lax.dot_general(do16, qh16, _TN, preferred_element_type=F32)
            db = db + (qh16.astype(F32) * dq_st - kt16.astype(F32) * dk_st)
            dg = _mm01_left((c >= r).astype(_MXU), db) + gsum_scr[:, ls]
            gsum_scr[:, ls] = gsum_scr[:, ls] + jnp.sum(db, axis=0, keepdims=True)
            slope = (1.0 - lam) * sg * (1.0 - sg)
            df_ref[:, ls] = jnp.where(live, dg * slope / f - dk * slope, 0.0).astype(df_ref.dtype)
            dl = jnp.where(live, (dg / f - dk) * (1.0 - sg), 0.0)
            dlam_ref[0, :, ls] = dlam_ref[0, :, ls] + jnp.sum(dl, axis=0, keepdims=True)
        pl.when(last)(finish)

    blk = pl.BlockSpec((BLOCK, width), lambda b, h, n: (b * nb + nb - 1 - n, h))
    hbm = pl.BlockSpec(memory_space=pl.ANY)
    return pl.pallas_call(
        body, name="hgrn_bwd", grid=(bsz, N_HEADS // HEAD_GROUP, nb),
        in_specs=[blk, blk, blk, pl.BlockSpec((1, width), lambda b, h, n: (0, h)),
                  pl.BlockSpec((1, HEAD_GROUP, 1, D_HEAD, D_HEAD), lambda b, h, n: (b, h, nb - 1 - n, 0, 0)),
                  pl.BlockSpec((1, HEAD_GROUP, 1, D_HEAD, D_HEAD), lambda b, h, n: (b, h, nb - 1 - n, 0, 0)),
                  blk, pl.BlockSpec(sums.shape, lambda b, h, n: (0, 0)), pl.BlockSpec(levels.shape, lambda b, h, n: (0, 0))]
        + [hbm] * n_out,
        out_specs=[blk, blk, blk, pl.BlockSpec((1, 1, width), lambda b, h, n: (b, 0, h))] + [hbm] * n_out,
        out_shape=[jax.ShapeDtypeStruct((t, D_MODEL), _MXU)] * 3 + [jax.ShapeDtypeStruct((bsz, 1, D_MODEL), F32)]
        + [jax.ShapeDtypeStruct(a.shape, a.dtype) for a in outgoing],
        scratch_shapes=[pltpu.VMEM((HEAD_GROUP, D_HEAD, D_HEAD), F32), pltpu.VMEM((1, width), F32)]
        + _exchange_scratch(n_out),
        compiler_params=_params(("arbitrary", "arbitrary", "arbitrary")),
    )(qs, fs, vs, lam, sst, sym, do, sums, levels, *outgoing)


def _sb_valid(ahead, col, i, j):
    return (ahead < (i - j) * TILE) & (col >= N_PAD - j * TILE)


def _sb_logits(q16, k_blk, valid):
    z = jnp.where(valid, lax.dot_general(q16, k_blk.astype(_MXU), _NT, preferred_element_type=F32) * SB_SCALE, MASKED)
    softplus = jnp.where(z > SOFTPLUS_LINEAR, z, jnp.log(1.0 + jnp.exp(jnp.minimum(z, SOFTPLUS_LINEAR))))
    return -softplus, z - softplus


def _sb_fwd(qs, ks, vs, bsz, nq):
    t = qs.shape[0]
    lp = nq * TILE
    width = SB_FWD_GROUP * D_HEAD
    groups = N_HEADS // SB_FWD_GROUP
    lanes = [slice(hh * D_HEAD, (hh + 1) * D_HEAD) for hh in range(SB_FWD_GROUP)]

    def body(q_ref, k_ref, v_ref, o_ref, c_ref, n_ref):
        b, h, i = pl.program_id(0), pl.program_id(1), pl.program_id(2)
        q16 = [q_ref[:, ls].astype(_MXU) for ls in lanes]
        r, c = _iota2((TILE, TILE), 0), _iota2((TILE, TILE), 1)
        after = (r > c).astype(_MXU)

        def more(carry):
            jj, _, _, top = carry
            return (jj <= i) & (top > UNDERFLOW)

        def step(carry):
            jj, accs, sums, _ = carry
            j = i - jj
            ks_ = pl.ds(pl.multiple_of(j * TILE, TILE), TILE)
            valid = _sb_valid(c - r, c, i, j)
            new_accs, new_sums = [], []
            for hh, ls in enumerate(lanes):
                keep, log_beta = _sb_logits(q16[hh], k_ref[ks_, ls], valid)
                after_s = _mm01_right(keep, after)
                a = jnp.exp(log_beta + (sums[hh] + after_s))
                new_accs.append(accs[hh] + _mm(a, v_ref[ks_, ls]))
                new_sums.append(sums[hh] + (after_s[:, 0:1] + keep[:, 0:1]))
            top = functools.reduce(jnp.maximum, [jnp.max(x) for x in new_sums])
            return jj + 1, tuple(new_accs), tuple(new_sums), top

        init = (jnp.int32(0), tuple(jnp.zeros((TILE, D_HEAD), F32) for _ in lanes),
                tuple(jnp.zeros((TILE, 1), F32) for _ in lanes), jnp.float32(0.0))
        visited, accs, sums, _ = lax.while_loop(more, step, init)
        for hh, ls in enumerate(lanes):
            o_ref[:, ls] = accs[hh]
            c_ref[:, ls] = jnp.broadcast_to(sums[hh], (TILE, D_HEAD))
        n_ref[(b * groups + h) * nq + i] = visited.astype(F32)

    blk = pl.BlockSpec((TILE, width), lambda b, h, i: (b * nq + i, h))
    seq = pl.BlockSpec((lp, width), lambda b, h, i: (b, h))
    return pl.pallas_call(
        body, name="sb_fwd", grid=(bsz, groups, nq),
        in_specs=[blk, seq, seq], out_specs=[blk, blk, pl.BlockSpec(memory_space=pltpu.SMEM)],
        out_shape=[jax.ShapeDtypeStruct((t, D_MODEL), F32)] * 2 + [jax.ShapeDtypeStruct((bsz * groups * nq,), F32)],
        compiler_params=_params(("arbitrary", "arbitrary", "arbitrary")),
    )(qs, ks, vs)


def _sb_bwd(qs, ks, vs, ctot, visited, do, bsz, nq):
    t = qs.shape[0]
    lp = nq * TILE
    width = SB_BWD_GROUP * D_HEAD
    groups = N_HEADS // SB_BWD_GROUP
    lanes = [slice(hh * D_HEAD, (hh + 1) * D_HEAD) for hh in range(SB_BWD_GROUP)]

    def body(n_ref, q_ref, k_ref, v_ref, c_ref, do_ref, dq_ref, dk_ref, dv_ref, dk_acc, dv_acc):
        b, h, i = pl.program_id(0), pl.program_id(1), pl.program_id(2)

        @pl.when(i == 0)
        def _():
            dk_acc[...] = jnp.zeros_like(dk_acc)
            dv_acc[...] = jnp.zeros_like(dv_acc)

        q16 = [q_ref[:, ls].astype(_MXU) for ls in lanes]
        do16 = [do_ref[:, ls].astype(_MXU) for ls in lanes]
        q16_t = [q_ref[:, ls].astype(F32).T.astype(_MXU) for ls in lanes]
        do16_t = [do_ref[:, ls].astype(F32).T.astype(_MXU) for ls in lanes]
        totals = [c_ref[:, hh * D_HEAD:hh * D_HEAD + 1] for hh in range(SB_BWD_GROUP)]
        r, c = _iota2((TILE, TILE), 0), _iota2((TILE, TILE), 1)
        upto = (r <= c).astype(_MXU)
        before = (r < c).astype(_MXU)
        seen = n_ref[(b * (N_HEADS // SB_FWD_GROUP) + h * SB_BWD_GROUP // SB_FWD_GROUP) * nq + i]
        first = jnp.maximum(i + 1 - seen.astype(jnp.int32), 0)

        def step(j, carry):
            ks_ = pl.ds(pl.multiple_of(j * TILE, TILE), TILE)
            valid = _sb_valid(c - r, c, i, j)
            out = []
            for hh, ls in enumerate(lanes):
                dq, keep_pre, g_pre = carry[hh]
                k_blk, v_blk = k_ref[ks_, ls], v_ref[ks_, ls]
                keep, log_beta = _sb_logits(q16[hh], k_blk, valid)
                keep_upto = _mm01_right(keep, upto)
                a = jnp.exp(log_beta + (totals[hh] - keep_pre - keep_upto))
                da = lax.dot_general(do16[hh], v_blk.astype(_MXU), _NT, preferred_element_type=F32)
                g = a * da
                g_inside = _mm01_right(g, before)
                g_before = g_pre + g_inside
                beta = jnp.exp(log_beta)
                dz16 = (g * (1.0 - beta) - beta * g_before).astype(_MXU)
                dq = dq + jnp.dot(dz16, k_blk.astype(_MXU), preferred_element_type=F32)
                dk_acc[ls, ks_] += SB_SCALE * jnp.dot(q16_t[hh], dz16, preferred_element_type=F32)
                dv_acc[ls, ks_] += jnp.dot(do16_t[hh], a.astype(_MXU), preferred_element_type=F32)
                out.append((dq, keep_pre + keep_upto[:, TILE - 1:TILE],
                            g_pre + (g_inside[:, TILE - 1:TILE] + g[:, TILE - 1:TILE])))
            return tuple(out)

        zero_col = jnp.zeros((TILE, 1), F32)
        init = tuple((jnp.zeros((TILE, D_HEAD), F32), zero_col, zero_col) for _ in lanes)
        res = lax.fori_loop(first, i + 1, step, init)
        for hh, ls in enumerate(lanes):
            dq_ref[:, ls] = (SB_SCALE * res[hh][0]).astype(dq_ref.dtype)

        @pl.when(i == nq - 1)
        def _():
            dk_ref[...] = dk_acc[...].T.astype(dk_ref.dtype)
            dv_ref[...] = dv_acc[...].T.astype(dv_ref.dtype)

    blk = pl.BlockSpec((TILE, width), lambda b, h, i: (b * nq + i, h))
    seq = pl.BlockSpec((lp, width), lambda b, h, i: (b, h))
    seq_out = pl.BlockSpec((lp, width), lambda b, h, i: (b, h), pipeline_mode=pl.Buffered(1))
    return pl.pallas_call(
        body, name="sb_bwd", grid=(bsz, groups, nq),
        in_specs=[pl.BlockSpec(memory_space=pltpu.SMEM), blk, seq, seq, blk, blk], out_specs=[blk, seq_out, seq_out],
        out_shape=[jax.ShapeDtypeStruct((t, D_MODEL), _MXU)] * 3,
        scratch_shapes=[pltpu.VMEM((width, lp), F32)] * 2,
        compiler_params=_params(("arbitrary", "arbitrary", "arbitrary")),
    )(visited, qs, ks, vs, ctot, do)


def _head_norm(o, head_gain):
    outs, rs = [], []
    for h in range(N_HEADS):
        oh = o[:, h * D_HEAD:(h + 1) * D_HEAD]
        r = lax.rsqrt(jnp.mean(oh * oh, axis=-1, keepdims=True) + EPS)
        outs.append(oh * r)
        rs.append(r)
    return outs, rs


def _mix(o, gate, head_gain):
    if head_gain is None:
        on = o
    else:
        outs, _ = _head_norm(o, head_gain)
        on = jnp.concatenate([x * head_gain for x in outs], axis=1)
    return on, on * (gate * _sigmoid(gate))


def _out_fwd(o, gate, h_in, w_out, post_gain, head_gain, name):
    t = o.shape[0]
    tm = _row_tile(t, 2 * TILE)

    def body(o_ref, g_ref, h_ref, w_ref, pg_ref, hg_ref, ho_ref, u_ref):
        _, mix = _mix(o_ref[...], g_ref[...], hg_ref[...])
        u = jnp.dot(mix.astype(_MXU), w_ref[...], preferred_element_type=F32)
        u_ref[...] = u
        r = lax.rsqrt(jnp.mean(u * u, axis=-1, keepdims=True) + EPS)
        ho_ref[...] = h_ref[...] + u * r * pg_ref[...]

    row = pl.BlockSpec((tm, D_MODEL), lambda i: (i, 0))
    vec = pl.BlockSpec((1, D_MODEL), lambda i: (0, 0))
    return pl.pallas_call(
        body, name=name, grid=(t // tm,),
        in_specs=[row, row, row, pl.BlockSpec((D_MODEL, D_MODEL), lambda i: (0, 0)), vec,
                  pl.BlockSpec((1, D_HEAD), lambda i: (0, 0))],
        out_specs=[row, row], out_shape=[jax.ShapeDtypeStruct((t, D_MODEL), F32)] * 2,
        compiler_params=_params(("arbitrary",)),
    )(o, gate, h_in, w_out, post_gain, head_gain)


def _out_fwd_loss(o, gate, h_in, w_out, post_gain, target, nq, name):
    t = o.shape[0]

    def body(o_ref, g_ref, h_ref, w_ref, pg_ref, t_ref, dh_ref, u_ref, l_ref):
        i = pl.program_id(0)

        @pl.when(i == 0)
        def _():
            l_ref[...] = jnp.zeros_like(l_ref)

        _, mix = _mix(o_ref[...], g_ref[...], None)
        u = jnp.dot(mix.astype(_MXU), w_ref[...], preferred_element_type=F32)
        u_ref[...] = u

        @pl.when(i % nq == 0)
        def _():
            dh_ref[...] = jnp.zeros_like(dh_ref)

        @pl.when(i % nq != 0)
        def _():
            r = lax.rsqrt(jnp.mean(u * u, axis=-1, keepdims=True) + EPS)
            e = h_ref[...] + u * r * pg_ref[...] - t_ref[...]
            dh_ref[...] = e * (1.0 / D_MODEL)
            l_ref[...] += jnp.sum(e * e) * (0.5 / D_MODEL)

    row = pl.BlockSpec((TILE, D_MODEL), lambda i: (i, 0))
    vec = pl.BlockSpec((1, D_MODEL), lambda i: (0, 0))
    return pl.pallas_call(
        body, name=name, grid=(t // TILE,),
        in_specs=[row, row, row, pl.BlockSpec((D_MODEL, D_MODEL), lambda i: (0, 0)), vec,
                  pl.BlockSpec((TILE, D_MODEL), lambda i: ((i // nq) * (nq - 1) + jnp.maximum(i % nq - 1, 0), 0))],
        out_specs=[row, row, pl.BlockSpec((8, 128), lambda i: (0, 0))],
        out_shape=[jax.ShapeDtypeStruct((t, D_MODEL), F32)] * 2 + [jax.ShapeDtypeStruct((8, 128), F32)],
        compiler_params=_params(("arbitrary",)),
    )(o, gate, h_in, w_out, post_gain, target)


def _out_bwd(dh, u, o, gate, w_out, post_gain, head_gain, name, narrow_do):
    t = o.shape[0]
    tm = _row_tile(t, 2 * TILE)
    has_head = head_gain is not None

    def body(*refs):
        if has_head:
            dh_ref, u_ref, o_ref, g_ref, w_ref, pg_ref, hg_ref, do_ref, dg_ref, gw_ref, gw16_ref, gp_ref, gh_ref = refs
            hg = hg_ref[...]
        else:
            dh_ref, u_ref, o_ref, g_ref, w_ref, pg_ref, do_ref, dg_ref, gw_ref, gw16_ref, gp_ref = refs
            hg = None
        first = pl.program_id(0) == 0

        @pl.when(first)
        def _():
            gw_ref[...] = jnp.zeros_like(gw_ref)
            gp_ref[...] = jnp.zeros_like(gp_ref)
            if has_head:
                gh_ref[...] = jnp.zeros_like(gh_ref)

        dr, u, o, gate = dh_ref[...], u_ref[...], o_ref[...], g_ref[...]
        r = lax.rsqrt(jnp.mean(u * u, axis=-1, keepdims=True) + EPS)
        un = u * r
        gp_ref[...] += jnp.sum(dr * un, axis=0, keepdims=True)
        dun = dr * pg_ref[...]
        du = r * (dun - un * jnp.mean(dun * un, axis=-1, keepdims=True))
        on, mix = _mix(o, gate, hg)
        du16 = du.astype(_MXU)
        gw_ref[...] += lax.dot_general(mix.astype(_MXU), du16, _TN, preferred_element_type=F32)
        dmix = lax.dot_general(du16, w_ref[...], _NT, preferred_element_type=F32)
        sg = _sigmoid(gate)
        dg_ref[...] = (dmix * on * (sg * (1.0 + gate * (1.0 - sg)))).astype(dg_ref.dtype)
        don = dmix * (gate * sg)
        if has_head:
            outs, rs = _head_norm(o, hg)
            gh = jnp.zeros((1, D_HEAD), F32)
            cols = []
            for h in range(N_HEADS):
                dn = don[:, h * D_HEAD:(h + 1) * D_HEAD]
                gh = gh + jnp.sum(dn * outs[h], axis=0, keepdims=True)
                dnn = dn * hg
                cols.append(rs[h] * (dnn - outs[h] * jnp.mean(dnn * outs[h], axis=-1, keepdims=True)))
            gh_ref[...] += gh
            do_ref[...] = jnp.concatenate(cols, axis=1)
        else:
            do_ref[...] = don.astype(do_ref.dtype)

        @pl.when(pl.program_id(0) == pl.num_programs(0) - 1)
        def _():
            gw16_ref[...] = gw_ref[...].astype(_MXU)

    row = pl.BlockSpec((tm, D_MODEL), lambda i: (i, 0))
    vec = pl.BlockSpec((1, D_MODEL), lambda i: (0, 0))
    mat = pl.BlockSpec((D_MODEL, D_MODEL), lambda i: (0, 0))
    in_specs = [row, row, row, row, mat, vec]
    args = [dh, u, o, gate, w_out, post_gain]
    out_specs = [row, row, mat, mat, vec]
    out_shape = [jax.ShapeDtypeStruct((t, D_MODEL), _MXU if narrow_do else F32),
                 jax.ShapeDtypeStruct((t, D_MODEL), _MXU)] + [jax.ShapeDtypeStruct((D_MODEL, D_MODEL), F32),
                                                                  jax.ShapeDtypeStruct((D_MODEL, D_MODEL), _MXU),
                                                                  jax.ShapeDtypeStruct((1, D_MODEL), F32)]
    if has_head:
        in_specs.append(pl.BlockSpec((1, D_HEAD), lambda i: (0, 0)))
        args.append(head_gain)
        out_specs.append(pl.BlockSpec((1, D_HEAD), lambda i: (0, 0)))
        out_shape.append(jax.ShapeDtypeStruct((1, D_HEAD), F32))
    return pl.pallas_call(
        body, name=name, grid=(t // tm,), in_specs=in_specs, out_specs=out_specs, out_shape=out_shape,
        compiler_params=_params(("arbitrary",)),
    )(*args)


def _proj_bwd(ds, w4, h_in, gain, dh_out, name, outgoing=(), nq=None):
    t = h_in.shape[0]
    tm = _row_tile(t, 2 * TILE) if nq is None else TILE
    n_out = len(outgoing)
    n_dh = 1 if nq is None else 2

    def body(d0, d1, d2, d3, w_ref, h_ref, g_ref, dho_ref, *rest):
        send, dh_refs, gg_ref, rest = rest[:n_out], rest[n_out:n_out + n_dh], rest[n_out + n_dh], rest[n_out + n_dh + 1:]
        landed, sems = rest[:n_out], rest[n_out:]
        if n_out:
            first, last = _grid_ends(1)
            start, finish = _chip_exchange(send, landed, sems, slotted=True)
            pl.when(first)(start)

        @pl.when(pl.program_id(0) == 0)
        def _():
            gg_ref[...] = jnp.zeros_like(gg_ref)

        dy = jnp.zeros((tm, D_MODEL), F32)
        for j, d in enumerate((d0, d1, d2, d3)):
            dy = dy + lax.dot_general(d[...].astype(_MXU), w_ref[j], _NT, preferred_element_type=F32)
        x = h_ref[...]
        r = lax.rsqrt(jnp.mean(x * x, axis=-1, keepdims=True) + EPS)
        xn = x * r
        gg_ref[...] += jnp.sum(dy * xn, axis=0, keepdims=True)
        dxn = dy * g_ref[...]
        dh = dho_ref[...] + r * (dxn - xn * jnp.mean(dxn * xn, axis=-1, keepdims=True))
        if nq is None:
            dh_refs[0][...] = dh
        else:
            in_front = pl.program_id(0) % nq == 0

            @pl.when(in_front)
            def _():
                dh_refs[1][...] = dh

            @pl.when(jnp.logical_not(in_front))
            def _():
                dh_refs[0][...] = dh
        if n_out:
            pl.when(last)(finish)

    row = pl.BlockSpec((tm, D_MODEL), lambda i: (i, 0))
    vec = pl.BlockSpec((1, D_MODEL), lambda i: (0, 0))
    hbm = pl.BlockSpec(memory_space=pl.ANY)
    if nq is None:
        dh_specs, dh_shapes = [row], [jax.ShapeDtypeStruct((t, D_MODEL), F32)]
    else:
        dh_specs = [pl.BlockSpec((TILE, D_MODEL), lambda i: ((i // nq) * (nq - 1) + jnp.maximum(i % nq - 1, 0), 0)),
                    pl.BlockSpec((TILE, D_MODEL), lambda i: (i // nq, 0))]
        dh_shapes = [jax.ShapeDtypeStruct((t // nq * (nq - 1), D_MODEL), F32),
                     jax.ShapeDtypeStruct((t // nq, D_MODEL), F32)]
    return pl.pallas_call(
        body, name=name, grid=(t // tm,),
        in_specs=[row] * 4 + [pl.BlockSpec((4, D_MODEL, D_MODEL), lambda i: (0, 0, 0)), row, vec, row] + [hbm] * n_out,
        out_specs=dh_specs + [vec] + [hbm] * n_out,
        out_shape=dh_shapes + [jax.ShapeDtypeStruct((1, D_MODEL), F32)]
        + [jax.ShapeDtypeStruct(a.shape, a.dtype) for a in outgoing],
        scratch_shapes=_exchange_scratch(n_out) if n_out else [],
        compiler_params=_params(("arbitrary",)),
    )(*ds, w4, h_in, gain, dh_out, *outgoing)


def _weight_grad(y, d, name):
    t = y.shape[0]
    tk = _row_tile(t, t // 4)

    def body(y_ref, d_ref, g_ref, g16_ref):
        @pl.when(pl.program_id(0) == 0)
        def _():
            g_ref[...] = jnp.zeros_like(g_ref)

        g_ref[...] += lax.dot_general(y_ref[...], d_ref[...].astype(_MXU), _TN, preferred_element_type=F32)

        @pl.when(pl.program_id(0) == pl.num_programs(0) - 1)
        def _():
            g16_ref[...] = g_ref[...].astype(_MXU)

    row = pl.BlockSpec((tk, D_MODEL), lambda i: (i, 0))
    mat = pl.BlockSpec((D_MODEL, D_MODEL), lambda i: (0, 0))
    return pl.pallas_call(
        body, name=name, grid=(t // tk,), in_specs=[row, row], out_specs=[mat, mat],
        out_shape=[jax.ShapeDtypeStruct((D_MODEL, D_MODEL), F32), jax.ShapeDtypeStruct((D_MODEL, D_MODEL), _MXU)],
        compiler_params=_params(("arbitrary",)),
    )(y, d)


def _local_step(x, target, meta, pre_norm, post_norm, lam, head_gain, hw_in, shards):
    bsz, seq, _ = x.shape
    nq = seq // TILE + 1
    nb = nq * (TILE // BLOCK)
    lp = nq * TILE
    t = bsz * lp
    d4 = D_MODEL // N_CHIPS
    front = jnp.concatenate([jnp.zeros((N_PAD, D_MODEL), F32), meta], axis=0)
    h0 = jnp.concatenate([jnp.broadcast_to(front[None], (bsz, TILE, D_MODEL)), x], axis=1).reshape(t, D_MODEL)
    pre0, pre1, post0, post1 = pre_norm[0:1], pre_norm[1:2], post_norm[0:1], post_norm[1:2]

    y0, q0, f0, v0, g0 = _norm_proj(h0, pre0, hw_in, "norm_proj_hgrn", (False,) * 4)
    o0, sst, sym, sw_in, sw_out, hw_out = _hgrn_fwd(q0, f0, v0, lam, bsz, nb, shards)
    sw_out, hw_out = sw_out.reshape(D_MODEL, D_MODEL), hw_out.reshape(D_MODEL, D_MODEL)
    h1, u0 = _out_fwd(o0, g0, h0, hw_out, post0, head_gain, "out_fwd_hgrn")
    y1, q1, k1, v1, g1 = _norm_proj(h1, pre1, sw_in, "norm_proj_sb", (True, True, True, False))
    o1, ctot, visited = _sb_fwd(q1, k1, v1, bsz, nq)
    dh2, u1, loss_blk = _out_fwd_loss(o1, g1, h1, sw_out, post1, target.reshape(bsz * seq, D_MODEL), nq, "out_fwd_sb")

    do1, dg1, g_sw_out, g_sw_out16, g_post1 = _out_bwd(dh2, u1, o1, g1, sw_out, post1, None, "out_bwd_sb", True)
    dq1, dk1, dv1 = _sb_bwd(q1, k1, v1, ctot, visited, do1, bsz, nq)
    ds1 = (dq1, dk1, dv1, dg1)
    dh1, g_pre1 = _proj_bwd(ds1, sw_in, h1, pre1, dh2, "proj_bwd_sb")
    g_sw_in = [_weight_grad(y1, d, "wgrad_sb_%d" % j) for j, d in enumerate(ds1)]

    do0, dg0, g_hw_out, g_hw_out16, g_post0, g_head = _out_bwd(dh1, u0, o0, g0, hw_out, post0, head_gain, "out_bwd_hgrn",
                                                               False)
    ready = (jnp.stack([g16 for _, g16 in g_sw_in]), g_sw_out16.reshape(N_CHIPS, d4, D_MODEL),
             g_hw_out16.reshape(N_CHIPS, d4, D_MODEL))
    dq0, df0, dv0, dlam, land_sw_in, land_sw_out, land_hw_out = _hgrn_bwd(q0, f0, v0, lam, sst, sym, do0, bsz, nb, ready)
    ds0 = (dq0, df0, dv0, dg0)
    g_hw_in = [_weight_grad(y0, d, "wgrad_hgrn_%d" % j) for j, d in enumerate(ds0)]
    last = (jnp.stack([g16 for _, g16 in g_hw_in]),)
    grad_x, dh_front, g_pre0, land_hw_in = _proj_bwd(ds0, hw_in, h0, pre0, dh1, "proj_bwd_hgrn", last, nq)

    grad_x = grad_x.reshape(bsz, seq, D_MODEL)
    g_meta = jnp.sum(dh_front.reshape(bsz, TILE, D_MODEL)[:, N_PAD:, :], axis=0)
    g_lam = jnp.sum(dlam, axis=0)
    small = jnp.concatenate([g_pre0, g_pre1, g_post0, g_post1, g_lam, g_lam,
                             jnp.pad(g_head, ((0, 0), (0, D_MODEL - D_HEAD))), g_meta,
                             jnp.pad(loss_blk[0:1], ((0, 0), (0, D_MODEL - loss_blk.shape[1])))], axis=0)
    rows4 = lambda g: [g[j * d4:(j + 1) * d4] for j in range(N_CHIPS)]
    large = dict(hw_in=(land_hw_in, [g for g, _ in g_hw_in]), sw_in=(land_sw_in, [g for g, _ in g_sw_in]),
                 hw_out=(land_hw_out, rows4(g_hw_out)), sw_out=(land_sw_out, rows4(g_sw_out)))
    return grad_x, small, large


def _prep_weights(hw_in, sw_in, hw_out, sw_out, meta):
    def body(hi_ref, si_ref, ho_ref, so_ref, m_ref, ghi, gm, si16, so16, ho16, far_send, far_recv, near_send, near_recv):
        x, y, c = _place()
        me = 2 * x + y
        ghi[me] = hi_ref[0].astype(_MXU)
        gm[me] = m_ref[...]
        si16[...] = si_ref[0].astype(_MXU)
        so16[...] = so_ref[0].astype(_MXU)
        ho16[...] = ho_ref[0].astype(_MXU)
        outs = (ghi, gm)
        n = len(outs)
        peers = [(1 - x, y), (x, 1 - y), (1 - x, 1 - y)]

        def half(a, slot, which):
            rows = outs[a].shape[1] // 2
            return outs[a].at[slot, pl.ds(which * rows, rows), :]

        def far(r, a, slot):
            px, py = peers[r]
            return pltpu.make_async_remote_copy(
                src_ref=half(a, slot, c), dst_ref=half(a, slot, c), send_sem=far_send.at[r * n + a],
                recv_sem=far_recv.at[r * n + a], device_id=(px, py, c), device_id_type=MESH)

        def near(r, a, which):
            px, py = peers[r]
            return pltpu.make_async_remote_copy(
                src_ref=half(a, 2 * px + py, which), dst_ref=half(a, 2 * px + py, which),
                send_sem=near_send.at[r * n + a], recv_sem=near_recv.at[r * n + a],
                device_id=(x, y, 1 - c), device_id_type=MESH)

        for r in range(3):
            for a in range(n):
                far(r, a, me).start()
        for r, (px, py) in enumerate(peers):
            for a in range(n):
                far(r, a, 2 * px + py).wait_recv()
                near(r, a, c).start()
        for r in range(3):
            for a in range(n):
                near(r, a, 1 - c).wait_recv()
        for r in range(3):
            for a in range(n):
                far(r, a, me).wait_send()
                near(r, a, c).wait_send()

    d4 = D_MODEL // N_CHIPS
    vm = pl.BlockSpec(memory_space=pltpu.VMEM)
    return pl.pallas_call(
        body, name="prep_weights",
        in_specs=[vm] * 5, out_specs=[vm] * 5,
        out_shape=[jax.ShapeDtypeStruct((N_CHIPS, D_MODEL, D_MODEL), _MXU), jax.ShapeDtypeStruct((N_CHIPS, N_META, d4), F32),
                   jax.ShapeDtypeStruct((D_MODEL, D_MODEL), _MXU), jax.ShapeDtypeStruct((d4, D_MODEL), _MXU),
                   jax.ShapeDtypeStruct((d4, D_MODEL), _MXU)],
        scratch_shapes=[pltpu.SemaphoreType.DMA((6,))] * 4,
        compiler_params=pltpu.CompilerParams(vmem_limit_bytes=VMEM_LIMIT),
    )(hw_in, sw_in, hw_out, sw_out, meta)


def _scatter_small(small):
    def body(sm, lsm, send_sems, recv_sems, local_sem):
        x, y, c = _place()
        mine = 4 * x + 2 * y + c
        local = pltpu.make_async_copy(sm, lsm.at[mine], local_sem)
        local.start()

        def copy(rel, src_dev, to):
            return pltpu.make_async_remote_copy(
                src_ref=sm, dst_ref=lsm.at[src_dev], send_sem=send_sems.at[rel - 1], recv_sem=recv_sems.at[rel - 1],
                device_id=to, device_id_type=MESH)

        flip = lambda bit, v: 1 - v if bit else v
        rels = [(rel, flip(rel & 4, x), flip(rel & 2, y), flip(rel & 1, c)) for rel in range(1, N_DEV)]
        sends = [copy(rel, mine, (px, py, pc)) for rel, px, py, pc in rels]
        for cp in sends:
            cp.start()
        for rel, px, py, pc in rels:
            copy(rel, 4 * px + 2 * py + pc, (px, py, pc)).wait_recv()
        for cp in sends:
            cp.wait_send()
        local.wait()

    hbm = pl.BlockSpec(memory_space=pl.ANY)
    return pl.pallas_call(
        body, name="scatter_small", in_specs=[hbm], out_specs=hbm,
        out_shape=jax.ShapeDtypeStruct((N_DEV, SMALL_ROWS, D_MODEL), F32),
        scratch_shapes=[pltpu.SemaphoreType.DMA((N_DEV - 1,)), pltpu.SemaphoreType.DMA((N_DEV - 1,)),
                        pltpu.SemaphoreType.DMA(())],
    )(small)


def _sum_slots(landed, own, me, name):
    n, rows, _ = landed.shape
    tm = rows if rows < 256 else 256

    def body(me_ref, l_ref, o0, o1, o2, o3, out_ref):
        acc = None
        for k, o in enumerate((o0, o1, o2, o3)):
            term = jnp.where(me_ref[0] == k, o[...], l_ref[k].astype(F32))
            acc = term if acc is None else acc + term
        out_ref[...] = acc

    blk = pl.BlockSpec((tm, D_MODEL), lambda i: (i, 0))
    return pl.pallas_call(
        body, name=name, grid=(rows // tm,),
        in_specs=[pl.BlockSpec(memory_space=pltpu.SMEM), pl.BlockSpec((n, tm, D_MODEL), lambda i: (0, i, 0))] + [blk] * 4,
        out_specs=blk, out_shape=jax.ShapeDtypeStruct((rows, D_MODEL), F32),
        compiler_params=_params(("arbitrary",)),
    )(me, landed, *own)


def _swap_with_sibling(parts):
    def body(a0, a1, a2, a3, b0, b1, b2, b3, send_sems, recv_sems):
        x, y, c = _place()
        copies = [pltpu.make_async_remote_copy(src_ref=s, dst_ref=d, send_sem=send_sems.at[a], recv_sem=recv_sems.at[a],
                                               device_id=(x, y, 1 - c), device_id_type=MESH)
                  for a, (s, d) in enumerate(zip((a0, a1, a2, a3), (b0, b1, b2, b3)))]
        for cp in copies:
            cp.start()
        for cp in copies:
            cp.wait()

    hbm = pl.BlockSpec(memory_space=pl.ANY)
    return pl.pallas_call(
        body, name="swap_with_sibling", in_specs=[hbm] * 4, out_specs=[hbm] * 4,
        out_shape=[jax.ShapeDtypeStruct(p.shape, F32) for p in parts],
        scratch_shapes=[pltpu.SemaphoreType.DMA((4,)), pltpu.SemaphoreType.DMA((4,))],
    )(*parts)


def _adamw_math(w, g, m, v):
    m = ADAM_B1 * m + (1.0 - ADAM_B1) * g
    v = ADAM_B2 * v + (1.0 - ADAM_B2) * (g * g)
    m_hat = m / (1.0 - ADAM_B1 ** ADAM_STEP)
    v_hat = v / (1.0 - ADAM_B2 ** ADAM_STEP)
    delta = -ADAM_LR * (m_hat / (jnp.sqrt(v_hat) + ADAM_EPS) + ADAM_WD * w)
    return delta, m, v


def _adamw(w, g_parts, m, v, name):
    rows, cols = w.shape
    tm = rows if rows < 256 else 256
    n = len(g_parts)

    def body(*refs):
        w_ref, m_ref, v_ref = refs[n:n + 3]
        g_ref, d_ref, nm_ref, nv_ref = refs[n + 3:]
        g = refs[0][...]
        for p in refs[1:n]:
            g = g + p[...]
        g_ref[...] = g
        d_ref[...], nm_ref[...], nv_ref[...] = _adamw_math(w_ref[...], g, m_ref[...], v_ref[...])

    blk = pl.BlockSpec((tm, cols), lambda i: (i, 0))
    return pl.pallas_call(
        body, name=name, grid=(rows // tm,), in_specs=[blk] * (n + 3), out_specs=[blk] * 4,
        out_shape=[jax.ShapeDtypeStruct((rows, cols), F32)] * 4,
        compiler_params=_params(("arbitrary",)),
    )(*g_parts, w, m, v)


def _lam_of(hgrn_lb):
    def body(lb_ref, o_ref):
        lb = lb_ref[...]
        e = jnp.exp(lb - jnp.max(lb, axis=0, keepdims=True))
        o_ref[...] = e[0:1, :] / jnp.sum(e, axis=0, keepdims=True)

    return pl.pallas_call(body, name="lam_of", out_shape=jax.ShapeDtypeStruct((1, D_MODEL), F32))(hgrn_lb)


def _small_grads(land_small, lam):
    def body(l_ref, lam_ref, o_ref):
        acc = l_ref[0]
        for k in range(1, N_DEV):
            acc = acc + l_ref[k]
        p = lam_ref[...]
        slope = p * (1.0 - p)
        row = _iota2((SMALL_ROWS, D_MODEL), 0)
        o_ref[...] = acc * jnp.where(row == 4, slope, jnp.where(row == 5, -slope, 1.0))

    return pl.pallas_call(body, name="small_grads",
                          out_shape=jax.ShapeDtypeStruct((SMALL_ROWS, D_MODEL), F32))(land_small, lam)


def kernel(x, meta_tokens, pre_norm, post_norm, hgrn_w_in, hgrn_lb, hgrn_out_norm, hgrn_w_out, sb_w_in, sb_w_out, loss_target, m_meta_tokens, m_pre_norm, m_post_norm, m_hgrn_w_in, m_hgrn_lb, m_hgrn_out_norm, m_hgrn_w_out, m_sb_w_in, m_sb_w_out, v_meta_tokens, v_pre_norm, v_post_norm, v_hgrn_w_in, v_hgrn_lb, v_hgrn_out_norm, v_hgrn_w_out, v_sb_w_in, v_sb_w_out):
    d4 = D_MODEL // N_CHIPS
    chip = 2 * lax.axis_index("x") + lax.axis_index("y")
    hw_in, meta4, sw_in16, sw_out16, hw_out16 = _prep_weights(hgrn_w_in, sb_w_in, hgrn_w_out, sb_w_out, meta_tokens)
    meta = meta4.transpose(1, 0, 2).reshape(N_META, D_MODEL)
    lam = _lam_of(hgrn_lb)
    grad_x, small, large = _local_step(
        x, loss_target, meta, pre_norm, post_norm, lam, hgrn_out_norm,
        hw_in, (sw_in16, sw_out16, hw_out16))

    me = jnp.reshape(chip, (1,)).astype(jnp.int32)
    parts = [_sum_slots(*large[n], me, "sum_" + n) for n in ("hw_in", "sw_in", "hw_out", "sw_out")]
    sib = _swap_with_sibling(parts)
    small = _small_grads(_scatter_small(small), lam)
    loss = small[SMALL_ROWS - 1, 0]

    res = {}
    res["hgrn_w_in"] = _adamw(hgrn_w_in[0], [parts[0], sib[0]], m_hgrn_w_in[0], v_hgrn_w_in[0], "adamw_hw_in")
    res["sb_w_in"] = _adamw(sb_w_in[0], [parts[1], sib[1]], m_sb_w_in[0], v_sb_w_in[0], "adamw_sw_in")
    res["hgrn_w_out"] = _adamw(hgrn_w_out[0], [parts[2], sib[2]], m_hgrn_w_out[0], v_hgrn_w_out[0], "adamw_hw_out")
    res["sb_w_out"] = _adamw(sb_w_out[0], [parts[3], sib[3]], m_sb_w_out[0], v_sb_w_out[0], "adamw_sw_out")
    res["pre_norm"] = _adamw(pre_norm, [small[0:2]], m_pre_norm, v_pre_norm, "adamw_pre")
    res["post_norm"] = _adamw(post_norm, [small[2:4]], m_post_norm, v_post_norm, "adamw_post")
    res["hgrn_lb"] = _adamw(hgrn_lb, [small[4:6]], m_hgrn_lb, v_hgrn_lb, "adamw_lb")
    res["hgrn_out_norm"] = _adamw(hgrn_out_norm, [small[6:7, :D_HEAD]], m_hgrn_out_norm, v_hgrn_out_norm, "adamw_head")
    g_meta = lax.dynamic_slice_in_dim(small[7:7 + N_META], chip * d4, d4, axis=1)
    res["meta_tokens"] = _adamw(meta_tokens, [g_meta], m_meta_tokens, v_meta_tokens, "adamw_meta")
    for n in ("hgrn_w_in", "hgrn_w_out", "sb_w_in", "sb_w_out"):
        res[n] = tuple(a[None] for a in res[n])
    order = ("meta_tokens", "pre_norm", "post_norm", "hgrn_w_in", "hgrn_lb", "hgrn_out_norm", "hgrn_w_out",
             "sb_w_in", "sb_w_out")
    return (loss, grad_x, *[res[n][0] for n in order], *[res[n][1] for n in order],
            *[res[n][2] for n in order], *[res[n][3] for n in order])
```

```python
import functools

import jax
import numpy as np
import jax.numpy as jnp
from jax import lax
from jax.experimental import pallas as pl
from jax.experimental.pallas import tpu as pltpu

F32 = jnp.float32
_MXU = jnp.bfloat16

D_MODEL = 1024
N_HEADS = 8
D_HEAD = 128
BLOCK = 128
N_META = 16
TILE = 256
N_PAD = TILE - N_META
UNDERFLOW = -105.0
EPS = 1e-6
SB_SCALE = D_HEAD ** -0.5
SOFTPLUS_LINEAR = 20.0
MASKED = -1e30
ADAM_LR, ADAM_B1, ADAM_B2, ADAM_EPS, ADAM_WD, ADAM_STEP = 0.001, 0.9, 0.999, 1e-08, 0.01, 10
N_CHIPS = 4
N_DEV = 8
SMALL_ROWS = 24
VMEM_LIMIT = 56 * 1024 * 1024
MESH = pl.DeviceIdType.MESH

_NT = (((1,), (1,)), ((), ()))
_TN = (((0,), (0,)), ((), ()))


def _mm(a, b):
    return jnp.dot(a.astype(_MXU), b.astype(_MXU), preferred_element_type=F32)


def _mm_nt(a, b):
    return lax.dot_general(a.astype(_MXU), b.astype(_MXU), _NT, preferred_element_type=F32)


def _mm_tn(a, b):
    return lax.dot_general(a.astype(_MXU), b.astype(_MXU), _TN, preferred_element_type=F32)


def _split2(x):
    hi = x.astype(_MXU)
    return hi, (x - hi.astype(F32)).astype(_MXU)


def _mm_s(a16, state):
    hi, lo = _split2(state)
    return jnp.dot(a16, hi, preferred_element_type=F32) + jnp.dot(a16, lo, preferred_element_type=F32)


def _mm_nt_s(a16, state):
    hi, lo = _split2(state)
    return (lax.dot_general(a16, hi, _NT, preferred_element_type=F32)
            + lax.dot_general(a16, lo, _NT, preferred_element_type=F32))


def _mm01_right(x, m01):
    return jnp.dot(x.astype(_MXU), m01, preferred_element_type=F32)


def _mm01_left(m01, x):
    hi, lo = _split2(x)
    return jnp.dot(m01, hi, preferred_element_type=F32) + jnp.dot(m01, lo, preferred_element_type=F32)


def _iota2(shape, dim):
    return lax.broadcasted_iota(jnp.int32, shape, dim)


def _row_tile(total, pref):
    t = pref
    while total % t:
        t -= BLOCK
    return t


def _params(sem, limit=VMEM_LIMIT):
    return pltpu.CompilerParams(dimension_semantics=sem, vmem_limit_bytes=limit)


def _sigmoid(x):
    return 1.0 / (1.0 + jnp.exp(-x))


def _grid_ends(ndim):
    first, last = True, True
    for d in range(ndim):
        first = first & (pl.program_id(d) == 0)
        last = last & (pl.program_id(d) == pl.num_programs(d) - 1)
    return first, last


def _place():
    return lax.axis_index("x"), lax.axis_index("y"), lax.axis_index("c")


def _exchange_scratch(n):
    return [pltpu.SemaphoreType.DMA((3 * n,)), pltpu.SemaphoreType.DMA((3 * n,)), pltpu.SemaphoreType.DMA((n,))]


def _chip_exchange(srcs, dsts, sems, slotted):
    send_sems, recv_sems, local_sems = sems
    x, y, c = _place()
    me = 2 * x + y
    peers = [(1 - x, y), (x, 1 - y), (1 - x, 1 - y)]
    n = len(dsts)

    def remote(r, a, sending):
        px, py = peers[r]
        p = 2 * px + py
        return pltpu.make_async_remote_copy(
            src_ref=srcs[a].at[p] if slotted else srcs[a], dst_ref=dsts[a].at[me if sending else p],
            send_sem=send_sems.at[r * n + a], recv_sem=recv_sems.at[r * n + a],
            device_id=(px, py, c), device_id_type=MESH)

    def local(a):
        return pltpu.make_async_copy(srcs[a].at[me] if slotted else srcs[a], dsts[a].at[me], local_sems.at[a])

    def start():
        for a in range(n):
            local(a).start()
        for r in range(3):
            for a in range(n):
                remote(r, a, True).start()

    def finish():
        for r in range(3):
            for a in range(n):
                remote(r, a, False).wait_recv()
        for r in range(3):
            for a in range(n):
                remote(r, a, True).wait_send()
        for a in range(n):
            local(a).wait()

    return start, finish


def _norm_proj(h, gain, w4, name, narrow):
    t = h.shape[0]
    tm = _row_tile(t, 2 * TILE)

    def body(h_ref, g_ref, w_ref, y_ref, s0, s1, s2, s3):
        x = h_ref[...]
        r = lax.rsqrt(jnp.mean(x * x, axis=-1, keepdims=True) + EPS)
        y = (x * r * g_ref[...]).astype(_MXU)
        y_ref[...] = y
        for j, s in enumerate((s0, s1, s2, s3)):
            s[...] = jnp.dot(y, w_ref[j], preferred_element_type=F32).astype(s.dtype)

    row = pl.BlockSpec((tm, D_MODEL), lambda i: (i, 0))
    return pl.pallas_call(
        body, name=name, grid=(t // tm,),
        in_specs=[row, pl.BlockSpec((1, D_MODEL), lambda i: (0, 0)),
                  pl.BlockSpec((4, D_MODEL, D_MODEL), lambda i: (0, 0, 0))],
        out_specs=[row] * 5,
        out_shape=[jax.ShapeDtypeStruct((t, D_MODEL), _MXU)]
        + [jax.ShapeDtypeStruct((t, D_MODEL), _MXU if n else F32) for n in narrow],
        compiler_params=_params(("arbitrary",)),
    )(h, gain, w4)


LEVELS = (64, 32, 16, 8, 4, 2, 1)
HEAD_GROUP = 8
CHUNKS = 2
SB_FWD_GROUP = 8
SB_BWD_GROUP = 4


def _hgrn_tables():
    r = np.arange(BLOCK)
    mats = [r[None, :] <= r[:, None]]
    x = r[:, None] ^ r[None, :]
    lv = np.full((BLOCK, BLOCK), len(LEVELS), np.int32)
    for i, m in enumerate(LEVELS):
        lv[(x >= m) & (x < 2 * m)] = i
    return jnp.asarray(np.concatenate(mats, 0).astype(np.float32), dtype=_MXU), jnp.asarray(lv)


def _hgrn_exponents(g, sums):
    b = _mm01_left(sums, g)
    row = _iota2((BLOCK, D_HEAD), 0)
    out = []
    for m in LEVELS:
        is_q = (row & m) != 0
        if m >= 4:
            grp = b.reshape(BLOCK // (2 * m), 2 * m, D_HEAD)
            ref = jnp.broadcast_to(grp[:, m - 1:m, :], grp.shape).reshape(BLOCK, D_HEAD)
            d = b - ref
            out.append(jnp.where(is_q, d, -d))
        elif m == 2:
            below, above = pltpu.roll(g, 1, axis=0), pltpu.roll(g, BLOCK - 1, axis=0)
            low = row & 3
            out.append(jnp.where(low == 3, g + below, jnp.where(low == 2, g, jnp.where(low == 0, above, 0.0))))
        else:
            out.append(jnp.where(is_q, g, 0.0))
    return b, out


def _hgrn_gates(fz, lam, chunk):
    pos = chunk * BLOCK + _iota2((BLOCK, D_HEAD), 0)
    live = pos >= N_PAD
    sg = _sigmoid(fz)
    f = lam + (1.0 - lam) * sg
    g = jnp.where(live, jnp.log(f), 0.0)
    k = jnp.where(live, (1.0 - lam) * (1.0 - sg), 0.0)
    return sg, f, g, k, live


def _level_operand(q, k, exponent, m):
    decay = jnp.exp(exponent)
    is_q = (_iota2((BLOCK, D_HEAD), 0) & m) != 0
    return is_q, decay, (jnp.where(is_q, q, k) * decay).astype(_MXU)


def _hgrn_fwd(qs, fs, vs, lam, bsz, nb, shards):
    t = qs.shape[0]
    sums, levels = _hgrn_tables()
    width = HEAD_GROUP * D_HEAD

    n_sh = len(shards)

    def body(q_ref, f_ref, v_ref, lam_ref, sums_ref, lv_ref, *rest):
        own, (o_ref, sst_ref, sym_ref), rest = rest[:n_sh], rest[n_sh:n_sh + 3], rest[n_sh + 3:]
        gathered, st_scr, sems = rest[:n_sh], rest[n_sh], rest[n_sh + 1:]
        n = pl.program_id(2)
        first, last = _grid_ends(3)
        start, finish = _chip_exchange(own, gathered, sems, slotted=False)
        pl.when(first)(start)

        @pl.when(n == 0)
        def _():
            st_scr[...] = jnp.zeros_like(st_scr)

        lv = lv_ref[...]
        r, c = _iota2((BLOCK, BLOCK), 0), _iota2((BLOCK, BLOCK), 1)
        for cc, hh in [(cc, hh) for cc in range(CHUNKS) for hh in range(HEAD_GROUP)]:
            ls, rows = slice(hh * D_HEAD, (hh + 1) * D_HEAD), slice(cc * BLOCK, (cc + 1) * BLOCK)
            st = st_scr[hh]
            sst_ref[0, hh, cc] = st
            q, v = q_ref[rows, ls], v_ref[rows, ls]
            _, _, g, k, _ = _hgrn_gates(f_ref[rows, ls], lam_ref[:, ls], CHUNKS * n + cc)
            b, exps = _hgrn_exponents(g, sums_ref[...])
            sym = jnp.zeros((BLOCK, BLOCK), F32)
            for li, m in enumerate(LEVELS):
                _, _, x16 = _level_operand(q, k, exps[li], m)
                sym = jnp.where(lv == li, lax.dot_general(x16, x16, _NT, preferred_element_type=F32), sym)
            sym = jnp.where(c == r, jnp.sum(q * k, axis=1, keepdims=True), sym).astype(_MXU)
            sym_ref[0, hh, cc] = sym
            o_ref[rows, ls] = _mm_nt(q * jnp.exp(b), st) + _mm(jnp.where(c <= r, sym, 0), v)
            b_end = b[BLOCK - 1:BLOCK, :]
            st_scr[hh] = st * jnp.exp(b_end) + _mm_tn(v, k * jnp.exp(b_end - b))
        pl.when(last)(finish)

    steps = nb // CHUNKS
    blk = pl.BlockSpec((CHUNKS * BLOCK, width), lambda b, h, n: (b * steps + n, h))
    hbm = pl.BlockSpec(memory_space=pl.ANY)
    return pl.pallas_call(
        body, name="hgrn_fwd", grid=(bsz, N_HEADS // HEAD_GROUP, steps),
        in_specs=[blk, blk, blk, pl.BlockSpec((1, width), lambda b, h, n: (0, h)),
                  pl.BlockSpec(sums.shape, lambda b, h, n: (0, 0)), pl.BlockSpec(levels.shape, lambda b, h, n: (0, 0))]
        + [hbm] * n_sh,
        out_specs=[blk] + [pl.BlockSpec((1, HEAD_GROUP, CHUNKS, D_HEAD, D_HEAD), lambda b, h, n: (b, h, n, 0, 0))] * 2
        + [hbm] * n_sh,
        out_shape=[jax.ShapeDtypeStruct((t, D_MODEL), F32),
                   jax.ShapeDtypeStruct((bsz, N_HEADS, nb, D_HEAD, D_HEAD), F32),
                   jax.ShapeDtypeStruct((bsz, N_HEADS, nb, D_HEAD, D_HEAD), _MXU)]
        + [jax.ShapeDtypeStruct((N_CHIPS,) + a.shape, a.dtype) for a in shards],
        scratch_shapes=[pltpu.VMEM((HEAD_GROUP, D_HEAD, D_HEAD), F32)] + _exchange_scratch(n_sh),
        compiler_params=_params(("arbitrary", "arbitrary", "arbitrary")),
    )(qs, fs, vs, lam, sums, levels, *shards)


def _hgrn_bwd(qs, fs, vs, lam, sst, sym, do, bsz, nb, outgoing):
    t = qs.shape[0]
    sums, levels = _hgrn_tables()
    width = HEAD_GROUP * D_HEAD

    n_out = len(outgoing)
    steps = nb // CHUNKS

    def body(q_ref, f_ref, v_ref, lam_ref, sst_ref, sym_ref, do_ref, sums_ref, lv_ref, *rest):
        send, (dq_ref, df_ref, dv_ref, dlam_ref), rest = rest[:n_out], rest[n_out:n_out + 4], rest[n_out + 4:]
        landed, dst_scr, gsum_scr, sems = rest[:n_out], rest[n_out], rest[n_out + 1], rest[n_out + 2:]
        n = pl.program_id(2)
        first, last = _grid_ends(3)
        start, finish = _chip_exchange(send, landed, sems, slotted=True)
        pl.when(first)(start)

        @pl.when(n == 0)
        def _():
            dst_scr[...] = jnp.zeros_like(dst_scr)
            gsum_scr[...] = jnp.zeros_like(gsum_scr)
            dlam_ref[...] = jnp.zeros_like(dlam_ref)

        lv = lv_ref[...]
        r, c = _iota2((BLOCK, BLOCK), 0), _iota2((BLOCK, BLOCK), 1)
        for cc, hh in [(cc, hh) for cc in reversed(range(CHUNKS)) for hh in range(HEAD_GROUP)]:
            ls, rows = slice(hh * D_HEAD, (hh + 1) * D_HEAD), slice(cc * BLOCK, (cc + 1) * BLOCK)
            lam = lam_ref[:, ls]
            q, v, do = q_ref[rows, ls], v_ref[rows, ls], do_ref[rows, ls]
            sg, f, g, k, live = _hgrn_gates(f_ref[rows, ls], lam, CHUNKS * (steps - 1 - n) + cc)
            b, exps = _hgrn_exponents(g, sums_ref[...])
            do16, v16 = do.astype(_MXU), v.astype(_MXU)
            da = lax.dot_general(do16, v16, _NT, preferred_element_type=F32)
            da_sym = jnp.where(c < r, da, da.T)
            dq = jnp.zeros((BLOCK, D_HEAD), F32)
            dqk = jnp.zeros((BLOCK, D_HEAD), F32)
            db_q = jnp.zeros((BLOCK, D_HEAD), F32)
            db_qk = jnp.zeros((BLOCK, D_HEAD), F32)
            for li, m in enumerate(LEVELS):
                is_q, decay, x16 = _level_operand(q, k, exps[li], m)
                y = jnp.dot(jnp.where(lv == li, da_sym, 0.0).astype(_MXU), x16, preferred_element_type=F32)
                dx = y * decay
                dq = dq + jnp.where(is_q, dx, 0.0)
                dqk = dqk + dx
                p = x16.astype(F32) * y
                db_q = db_q + jnp.where(is_q, p, 0.0)
                db_qk = db_qk + p
            dk = dqk - dq
            db = 2.0 * db_q - db_qk
            a_t = jnp.where(c >= r, sym_ref[0, hh, cc], 0)
            st, dst = sst_ref[0, hh, cc], dst_scr[hh]
            eb = jnp.exp(b)
            b_end = b[BLOCK - 1:BLOCK, :]
            dec = jnp.exp(b_end - b)
            qh16, kt16 = (q * eb).astype(_MXU), (k * dec).astype(_MXU)
            dq_st = _mm_s(do16, st)
            dk_st = _mm_s(v16, dst)
            d_diag = jnp.sum(do * v, axis=1, keepdims=True)
            dq_ref[rows, ls] = (dq + d_diag * k + eb * dq_st).astype(dq_ref.dtype)
            dk = dk + d_diag * q + dec * dk_st
            dv_ref[rows, ls] = (jnp.dot(a_t, do16, preferred_element_type=F32) + _mm_nt_s(kt16, dst)).astype(dv_ref.dtype)
            dst_scr[hh] = dst * jnp.exp(b_end) + lax.dot_general(do16, qh16, _TN, preferred_element_type=F32)
            db = db + (qh16.astype(F32) * dq_st - kt16.astype(F32) * dk_st)
            dg = _mm01_left((c >= r).astype(_MXU), db) + gsum_scr[:, ls]
            gsum_scr[:, ls] = gsum_scr[:, ls] + jnp.sum(db, axis=0, keepdims=True)
            slope = (1.0 - lam) * sg * (1.0 - sg)
            df_ref[rows, ls] = jnp.where(live, dg * slope / f - dk * slope, 0.0).astype(df_ref.dtype)
            dl = jnp.where(live, (dg / f - dk) * (1.0 - sg), 0.0)
            dlam_ref[0, :, ls] = dlam_ref[0, :, ls] + jnp.sum(dl, axis=0, keepdims=True)
        pl.when(last)(finish)

    blk = pl.BlockSpec((CHUNKS * BLOCK, width), lambda b, h, n: (b * steps + steps - 1 - n, h))
    hbm = pl.BlockSpec(memory_space=pl.ANY)
    return pl.pallas_call(
        body, name="hgrn_bwd", grid=(bsz, N_HEADS // HEAD_GROUP, steps),
        in_specs=[blk, blk, blk, pl.BlockSpec((1, width), lambda b, h, n: (0, h)),
                  pl.BlockSpec((1, HEAD_GROUP, CHUNKS, D_HEAD, D_HEAD), lambda b, h, n: (b, h, steps - 1 - n, 0, 0)),
                  pl.BlockSpec((1, HEAD_GROUP, CHUNKS, D_HEAD, D_HEAD), lambda b, h, n: (b, h, steps - 1 - n, 0, 0)),
                  blk, pl.BlockSpec(sums.shape, lambda b, h, n: (0, 0)), pl.BlockSpec(levels.shape, lambda b, h, n: (0, 0))]
        + [hbm] * n_out,
        out_specs=[blk, blk, blk, pl.BlockSpec((1, 1, width), lambda b, h, n: (b, 0, h))] + [hbm] * n_out,
        out_shape=[jax.ShapeDtypeStruct((t, D_MODEL), _MXU)] * 3 + [jax.ShapeDtypeStruct((bsz, 1, D_MODEL), F32)]
        + [jax.ShapeDtypeStruct(a.shape, a.dtype) for a in outgoing],
        scratch_shapes=[pltpu.VMEM((HEAD_GROUP, D_HEAD, D_HEAD), F32), pltpu.VMEM((1, width), F32)]
        + _exchange_scratch(n_out),
        compiler_params=_params(("arbitrary", "arbitrary", "arbitrary")),
    )(qs, fs, vs, lam, sst, sym, do, sums, levels, *outgoing)


def _sb_valid(ahead, col, i, j):
    return (ahead < (i - j) * TILE) & (col >= N_PAD - j * TILE)


def _sb_logits(q16, k_blk, valid):
    z = jnp.where(valid, lax.dot_general(q16, k_blk.astype(_MXU), _NT, preferred_element_type=F32) * SB_SCALE, MASKED)
    softplus = jnp.where(z > SOFTPLUS_LINEAR, z, jnp.log(1.0 + jnp.exp(jnp.minimum(z, SOFTPLUS_LINEAR))))
    return -softplus, z - softplus


def _sb_fwd(qs, ks, vs, bsz, nq):
    t = qs.shape[0]
    lp = nq * TILE
    width = SB_FWD_GROUP * D_HEAD
    groups = N_HEADS // SB_FWD_GROUP
    lanes = [slice(hh * D_HEAD, (hh + 1) * D_HEAD) for hh in range(SB_FWD_GROUP)]

    def body(q_ref, k_ref, v_ref, o_ref, c_ref, n_ref):
        b, h, i = pl.program_id(0), pl.program_id(1), pl.program_id(2)
        q16 = [q_ref[:, ls].astype(_MXU) for ls in lanes]
        r, c = _iota2((TILE, TILE), 0), _iota2((TILE, TILE), 1)
        after = (r > c).astype(_MXU)

        def more(carry):
            jj, _, _, top = carry
            return (jj <= i) & (top > UNDERFLOW)

        def step(carry):
            jj, accs, sums, _ = carry
            j = i - jj
            ks_ = pl.ds(pl.multiple_of(j * TILE, TILE), TILE)
            valid = _sb_valid(c - r, c, i, j)
            new_accs, new_sums = [], []
            for hh, ls in enumerate(lanes):
                keep, log_beta = _sb_logits(q16[hh], k_ref[ks_, ls], valid)
                after_s = _mm01_right(keep, after)
                a = jnp.exp(log_beta + (sums[hh] + after_s))
                new_accs.append(accs[hh] + _mm(a, v_ref[ks_, ls]))
                new_sums.append(sums[hh] + (after_s[:, 0:1] + keep[:, 0:1]))
            top = functools.reduce(jnp.maximum, [jnp.max(x) for x in new_sums])
            return jj + 1, tuple(new_accs), tuple(new_sums), top

        init = (jnp.int32(0), tuple(jnp.zeros((TILE, D_HEAD), F32) for _ in lanes),
                tuple(jnp.zeros((TILE, 1), F32) for _ in lanes), jnp.float32(0.0))
        visited, accs, sums, _ = lax.while_loop(more, step, init)
        for hh, ls in enumerate(lanes):
            o_ref[:, ls] = accs[hh]
            c_ref[:, ls] = jnp.broadcast_to(sums[hh], (TILE, D_HEAD))
        n_ref[(b * groups + h) * nq + i] = visited.astype(F32)

    blk = pl.BlockSpec((TILE, width), lambda b, h, i: (b * nq + i, h))
    seq = pl.BlockSpec((lp, width), lambda b, h, i: (b, h))
    return pl.pallas_call(
        body, name="sb_fwd", grid=(bsz, groups, nq),
        in_specs=[blk, seq, seq], out_specs=[blk, blk, pl.BlockSpec(memory_space=pltpu.SMEM)],
        out_shape=[jax.ShapeDtypeStruct((t, D_MODEL), F32)] * 2 + [jax.ShapeDtypeStruct((bsz * groups * nq,), F32)],
        compiler_params=_params(("arbitrary", "arbitrary", "arbitrary")),
    )(qs, ks, vs)


def _sb_bwd(qs, ks, vs, ctot, visited, do, bsz, nq):
    t = qs.shape[0]
    lp = nq * TILE
    width = SB_BWD_GROUP * D_HEAD
    groups = N_HEADS // SB_BWD_GROUP
    lanes = [slice(hh * D_HEAD, (hh + 1) * D_HEAD) for hh in range(SB_BWD_GROUP)]

    def body(n_ref, q_ref, k_ref, v_ref, c_ref, do_ref, dq_ref, dk_ref, dv_ref, dk_acc, dv_acc):
        b, h, i = pl.program_id(0), pl.program_id(1), pl.program_id(2)

        @pl.when(i == 0)
        def _():
            dk_acc[...] = jnp.zeros_like(dk_acc)
            dv_acc[...] = jnp.zeros_like(dv_acc)

        q16 = [q_ref[:, ls].astype(_MXU) for ls in lanes]
        do16 = [do_ref[:, ls].astype(_MXU) for ls in lanes]
        q16_t = [q_ref[:, ls].astype(F32).T.astype(_MXU) for ls in lanes]
        do16_t = [do_ref[:, ls].astype(F32).T.astype(_MXU) for ls in lanes]
        totals = [c_ref[:, hh * D_HEAD:hh * D_HEAD + 1] for hh in range(SB_BWD_GROUP)]
        r, c = _iota2((TILE, TILE), 0), _iota2((TILE, TILE), 1)
        upto = (r <= c).astype(_MXU)
        before = (r < c).astype(_MXU)
        seen = n_ref[(b * (N_HEADS // SB_FWD_GROUP) + h * SB_BWD_GROUP // SB_FWD_GROUP) * nq + i]
        first = jnp.maximum(i + 1 - seen.astype(jnp.int32), 0)

        def step(j, carry):
            ks_ = pl.ds(pl.multiple_of(j * TILE, TILE), TILE)
            valid = _sb_valid(c - r, c, i, j)
            out = []
            for hh, ls in enumerate(lanes):
                dq, keep_pre, g_pre = carry[hh]
                k_blk, v_blk = k_ref[ks_, ls], v_ref[ks_, ls]
                keep, log_beta = _sb_logits(q16[hh], k_blk, valid)
                keep_upto = _mm01_right(keep, upto)
                a = jnp.exp(log_beta + (totals[hh] - keep_pre - keep_upto))
                da = lax.dot_general(do16[hh], v_blk.astype(_MXU), _NT, preferred_element_type=F32)
                g = a * da
                g_inside = _mm01_right(g, before)
                g_before = g_pre + g_inside
                beta = jnp.exp(log_beta)
                dz16 = (g * (1.0 - beta) - beta * g_before).astype(_MXU)
                dq = dq + jnp.dot(dz16, k_blk.astype(_MXU), preferred_element_type=F32)
                dk_acc[ls, ks_] += SB_SCALE * jnp.dot(q16_t[hh], dz16, preferred_element_type=F32)
                dv_acc[ls, ks_] += jnp.dot(do16_t[hh], a.astype(_MXU), preferred_element_type=F32)
                out.append((dq, keep_pre + keep_upto[:, TILE - 1:TILE],
                            g_pre + (g_inside[:, TILE - 1:TILE] + g[:, TILE - 1:TILE])))
            return tuple(out)

        zero_col = jnp.zeros((TILE, 1), F32)
        init = tuple((jnp.zeros((TILE, D_HEAD), F32), zero_col, zero_col) for _ in lanes)
        res = lax.fori_loop(first, i + 1, step, init)
        for hh, ls in enumerate(lanes):
            dq_ref[:, ls] = (SB_SCALE * res[hh][0]).astype(dq_ref.dtype)

        @pl.when(i == nq - 1)
        def _():
            dk_ref[...] = dk_acc[...].T.astype(dk_ref.dtype)
            dv_ref[...] = dv_acc[...].T.astype(dv_ref.dtype)

    blk = pl.BlockSpec((TILE, width), lambda b, h, i: (b * nq + i, h))
    seq = pl.BlockSpec((lp, width), lambda b, h, i: (b, h))
    seq_out = pl.BlockSpec((lp, width), lambda b, h, i: (b, h), pipeline_mode=pl.Buffered(1))
    return pl.pallas_call(
        body, name="sb_bwd", grid=(bsz, groups, nq),
        in_specs=[pl.BlockSpec(memory_space=pltpu.SMEM), blk, seq, seq, blk, blk], out_specs=[blk, seq_out, seq_out],
        out_shape=[jax.ShapeDtypeStruct((t, D_MODEL), _MXU)] * 3,
        scratch_shapes=[pltpu.VMEM((width, lp), F32)] * 2,
        compiler_params=_params(("arbitrary", "arbitrary", "arbitrary")),
    )(visited, qs, ks, vs, ctot, do)


def _head_norm(o, head_gain):
    outs, rs = [], []
    for h in range(N_HEADS):
        oh = o[:, h * D_HEAD:(h + 1) * D_HEAD]
        r = lax.rsqrt(jnp.mean(oh * oh, axis=-1, keepdims=True) + EPS)
        outs.append(oh * r)
        rs.append(r)
    return outs, rs


def _mix(o, gate, head_gain):
    if head_gain is None:
        on = o
    else:
        outs, _ = _head_norm(o, head_gain)
        on = jnp.concatenate([x * head_gain for x in outs], axis=1)
    return on, on * (gate * _sigmoid(gate))


def _out_fwd(o, gate, h_in, w_out, post_gain, head_gain, name):
    t = o.shape[0]
    tm = _row_tile(t, 2 * TILE)

    def body(o_ref, g_ref, h_ref, w_ref, pg_ref, hg_ref, ho_ref, u_ref):
        _, mix = _mix(o_ref[...], g_ref[...], hg_ref[...])
        u = jnp.dot(mix.astype(_MXU), w_ref[...], preferred_element_type=F32)
        u_ref[...] = u
        r = lax.rsqrt(jnp.mean(u * u, axis=-1, keepdims=True) + EPS)
        ho_ref[...] = h_ref[...] + u * r * pg_ref[...]

    row = pl.BlockSpec((tm, D_MODEL), lambda i: (i, 0))
    vec = pl.BlockSpec((1, D_MODEL), lambda i: (0, 0))
    return pl.pallas_call(
        body, name=name, grid=(t // tm,),
        in_specs=[row, row, row, pl.BlockSpec((D_MODEL, D_MODEL), lambda i: (0, 0)), vec,
                  pl.BlockSpec((1, D_HEAD), lambda i: (0, 0))],
        out_specs=[row, row], out_shape=[jax.ShapeDtypeStruct((t, D_MODEL), F32)] * 2,
        compiler_params=_params(("arbitrary",)),
    )(o, gate, h_in, w_out, post_gain, head_gain)


def _out_fwd_loss(o, gate, h_in, w_out, post_gain, target, nq, name):
    t = o.shape[0]

    def body(o_ref, g_ref, h_ref, w_ref, pg_ref, t_ref, dh_ref, u_ref, l_ref):
        i = pl.program_id(0)

        @pl.when(i == 0)
        def _():
            l_ref[...] = jnp.zeros_like(l_ref)

        _, mix = _mix(o_ref[...], g_ref[...], None)
        u = jnp.dot(mix.astype(_MXU), w_ref[...], preferred_element_type=F32)
        u_ref[...] = u

        @pl.when(i % nq == 0)
        def _():
            dh_ref[...] = jnp.zeros_like(dh_ref)

        @pl.when(i % nq != 0)
        def _():
            r = lax.rsqrt(jnp.mean(u * u, axis=-1, keepdims=True) + EPS)
            e = h_ref[...] + u * r * pg_ref[...] - t_ref[...]
            dh_ref[...] = e * (1.0 / D_MODEL)
            l_ref[...] += jnp.sum(e * e) * (0.5 / D_MODEL)

    row = pl.BlockSpec((TILE, D_MODEL), lambda i: (i, 0))
    vec = pl.BlockSpec((1, D_MODEL), lambda i: (0, 0))
    return pl.pallas_call(
        body, name=name, grid=(t // TILE,),
        in_specs=[row, row, row, pl.BlockSpec((D_MODEL, D_MODEL), lambda i: (0, 0)), vec,
                  pl.BlockSpec((TILE, D_MODEL), lambda i: ((i // nq) * (nq - 1) + jnp.maximum(i % nq - 1, 0), 0))],
        out_specs=[row, row, pl.BlockSpec((8, 128), lambda i: (0, 0))],
        out_shape=[jax.ShapeDtypeStruct((t, D_MODEL), F32)] * 2 + [jax.ShapeDtypeStruct((8, 128), F32)],
        compiler_params=_params(("arbitrary",)),
    )(o, gate, h_in, w_out, post_gain, target)


def _out_bwd(dh, u, o, gate, w_out, post_gain, head_gain, name, narrow_do):
    t = o.shape[0]
    tm = _row_tile(t, 2 * TILE)
    has_head = head_gain is not None

    def body(*refs):
        if has_head:
            dh_ref, u_ref, o_ref, g_ref, w_ref, pg_ref, hg_ref, do_ref, dg_ref, gw_ref, gw16_ref, gp_ref, gh_ref = refs
            hg = hg_ref[...]
        else:
            dh_ref, u_ref, o_ref, g_ref, w_ref, pg_ref, do_ref, dg_ref, gw_ref, gw16_ref, gp_ref = refs
            hg = None
        first = pl.program_id(0) == 0

        @pl.when(first)
        def _():
            gw_ref[...] = jnp.zeros_like(gw_ref)
            gp_ref[...] = jnp.zeros_like(gp_ref)
            if has_head:
                gh_ref[...] = jnp.zeros_like(gh_ref)

        dr, u, o, gate = dh_ref[...], u_ref[...], o_ref[...], g_ref[...]
        r = lax.rsqrt(jnp.mean(u * u, axis=-1, keepdims=True) + EPS)
        un = u * r
        gp_ref[...] += jnp.sum(dr * un, axis=0, keepdims=True)
        dun = dr * pg_ref[...]
        du = r * (dun - un * jnp.mean(dun * un, axis=-1, keepdims=True))
        on, mix = _mix(o, gate, hg)
        du16 = du.astype(_MXU)
        gw_ref[...] += lax.dot_general(mix.astype(_MXU), du16, _TN, preferred_element_type=F32)
        dmix = lax.dot_general(du16, w_ref[...], _NT, preferred_element_type=F32)
        sg = _sigmoid(gate)
        dg_ref[...] = (dmix * on * (sg * (1.0 + gate * (1.0 - sg)))).astype(dg_ref.dtype)
        don = dmix * (gate * sg)
        if has_head:
            outs, rs = _head_norm(o, hg)
            gh = jnp.zeros((1, D_HEAD), F32)
            cols = []
            for h in range(N_HEADS):
                dn = don[:, h * D_HEAD:(h + 1) * D_HEAD]
                gh = gh + jnp.sum(dn * outs[h], axis=0, keepdims=True)
                dnn = dn * hg
                cols.append(rs[h] * (dnn - outs[h] * jnp.mean(dnn * outs[h], axis=-1, keepdims=True)))
            gh_ref[...] += gh
            do_ref[...] = jnp.concatenate(cols, axis=1)
        else:
            do_ref[...] = don.astype(do_ref.dtype)

        @pl.when(pl.program_id(0) == pl.num_programs(0) - 1)
        def _():
            gw16_ref[...] = gw_ref[...].astype(_MXU)

    row = pl.BlockSpec((tm, D_MODEL), lambda i: (i, 0))
    vec = pl.BlockSpec((1, D_MODEL), lambda i: (0, 0))
    mat = pl.BlockSpec((D_MODEL, D_MODEL), lambda i: (0, 0))
    in_specs = [row, row, row, row, mat, vec]
    args = [dh, u, o, gate, w_out, post_gain]
    out_specs = [row, row, mat, mat, vec]
    out_shape = [jax.ShapeDtypeStruct((t, D_MODEL), _MXU if narrow_do else F32),
                 jax.ShapeDtypeStruct((t, D_MODEL), _MXU)] + [jax.ShapeDtypeStruct((D_MODEL, D_MODEL), F32),
                                                                  jax.ShapeDtypeStruct((D_MODEL, D_MODEL), _MXU),
                                                                  jax.ShapeDtypeStruct((1, D_MODEL), F32)]
    if has_head:
        in_specs.append(pl.BlockSpec((1, D_HEAD), lambda i: (0, 0)))
        args.append(head_gain)
        out_specs.append(pl.BlockSpec((1, D_HEAD), lambda i: (0, 0)))
        out_shape.append(jax.ShapeDtypeStruct((1, D_HEAD), F32))
    return pl.pallas_call(
        body, name=name, grid=(t // tm,), in_specs=in_specs, out_specs=out_specs, out_shape=out_shape,
        compiler_params=_params(("arbitrary",)),
    )(*args)


def _proj_bwd(ds, w4, h_in, gain, dh_out, name, outgoing=(), nq=None):
    t = h_in.shape[0]
    tm = _row_tile(t, 2 * TILE) if nq is None else TILE
    n_out = len(outgoing)
    n_dh = 1 if nq is None else 2

    def body(d0, d1, d2, d3, w_ref, h_ref, g_ref, dho_ref, *rest):
        send, dh_refs, gg_ref, rest = rest[:n_out], rest[n_out:n_out + n_dh], rest[n_out + n_dh], rest[n_out + n_dh + 1:]
        landed, sems = rest[:n_out], rest[n_out:]
        if n_out:
            first, last = _grid_ends(1)
            start, finish = _chip_exchange(send, landed, sems, slotted=True)
            pl.when(first)(start)

        @pl.when(pl.program_id(0) == 0)
        def _():
            gg_ref[...] = jnp.zeros_like(gg_ref)

        dy = jnp.zeros((tm, D_MODEL), F32)
        for j, d in enumerate((d0, d1, d2, d3)):
            dy = dy + lax.dot_general(d[...].astype(_MXU), w_ref[j], _NT, preferred_element_type=F32)
        x = h_ref[...]
        r = lax.rsqrt(jnp.mean(x * x, axis=-1, keepdims=True) + EPS)
        xn = x * r
        gg_ref[...] += jnp.sum(dy * xn, axis=0, keepdims=True)
        dxn = dy * g_ref[...]
        dh = dho_ref[...] + r * (dxn - xn * jnp.mean(dxn * xn, axis=-1, keepdims=True))
        if nq is None:
            dh_refs[0][...] = dh
        else:
            in_front = pl.program_id(0) % nq == 0

            @pl.when(in_front)
            def _():
                dh_refs[1][...] = dh

            @pl.when(jnp.logical_not(in_front))
            def _():
                dh_refs[0][...] = dh
        if n_out:
            pl.when(last)(finish)

    row = pl.BlockSpec((tm, D_MODEL), lambda i: (i, 0))
    vec = pl.BlockSpec((1, D_MODEL), lambda i: (0, 0))
    hbm = pl.BlockSpec(memory_space=pl.ANY)
    if nq is None:
        dh_specs, dh_shapes = [row], [jax.ShapeDtypeStruct((t, D_MODEL), F32)]
    else:
        dh_specs = [pl.BlockSpec((TILE, D_MODEL), lambda i: ((i // nq) * (nq - 1) + jnp.maximum(i % nq - 1, 0), 0)),
                    pl.BlockSpec((TILE, D_MODEL), lambda i: (i // nq, 0))]
        dh_shapes = [jax.ShapeDtypeStruct((t // nq * (nq - 1), D_MODEL), F32),
                     jax.ShapeDtypeStruct((t // nq, D_MODEL), F32)]
    return pl.pallas_call(
        body, name=name, grid=(t // tm,),
        in_specs=[row] * 4 + [pl.BlockSpec((4, D_MODEL, D_MODEL), lambda i: (0, 0, 0)), row, vec, row] + [hbm] * n_out,
        out_specs=dh_specs + [vec] + [hbm] * n_out,
        out_shape=dh_shapes + [jax.ShapeDtypeStruct((1, D_MODEL), F32)]
        + [jax.ShapeDtypeStruct(a.shape, a.dtype) for a in outgoing],
        scratch_shapes=_exchange_scratch(n_out) if n_out else [],
        compiler_params=_params(("arbitrary",)),
    )(*ds, w4, h_in, gain, dh_out, *outgoing)


def _weight_grad(y, d, name):
    t = y.shape[0]
    tk = _row_tile(t, t // 4)

    def body(y_ref, d_ref, g_ref, g16_ref):
        @pl.when(pl.program_id(0) == 0)
        def _():
            g_ref[...] = jnp.zeros_like(g_ref)

        g_ref[...] += lax.dot_general(y_ref[...], d_ref[...].astype(_MXU), _TN, preferred_element_type=F32)

        @pl.when(pl.program_id(0) == pl.num_programs(0) - 1)
        def _():
            g16_ref[...] = g_ref[...].astype(_MXU)

    row = pl.BlockSpec((tk, D_MODEL), lambda i: (i, 0))
    mat = pl.BlockSpec((D_MODEL, D_MODEL), lambda i: (0, 0))
    return pl.pallas_call(
        body, name=name, grid=(t // tk,), in_specs=[row, row], out_specs=[mat, mat],
        out_shape=[jax.ShapeDtypeStruct((D_MODEL, D_MODEL), F32), jax.ShapeDtypeStruct((D_MODEL, D_MODEL), _MXU)],
        compiler_params=_params(("arbitrary",)),
    )(y, d)


def _local_step(x, target, meta, pre_norm, post_norm, lam, head_gain, hw_in, shards):
    bsz, seq, _ = x.shape
    nq = seq // TILE + 1
    nb = nq * (TILE // BLOCK)
    lp = nq * TILE
    t = bsz * lp
    d4 = D_MODEL // N_CHIPS
    front = jnp.concatenate([jnp.zeros((N_PAD, D_MODEL), F32), meta], axis=0)
    h0 = jnp.concatenate([jnp.broadcast_to(front[None], (bsz, TILE, D_MODEL)), x], axis=1).reshape(t, D_MODEL)
    pre0, pre1, post0, post1 = pre_norm[0:1], pre_norm[1:2], post_norm[0:1], post_norm[1:2]

    y0, q0, f0, v0, g0 = _norm_proj(h0, pre0, hw_in, "norm_proj_hgrn", (False,) * 4)
    o0, sst, sym, sw_in, sw_out, hw_out = _hgrn_fwd(q0, f0, v0, lam, bsz, nb, shards)
    sw_out, hw_out = sw_out.reshape(D_MODEL, D_MODEL), hw_out.reshape(D_MODEL, D_MODEL)
    h1, u0 = _out_fwd(o0, g0, h0, hw_out, post0, head_gain, "out_fwd_hgrn")
    y1, q1, k1, v1, g1 = _norm_proj(h1, pre1, sw_in, "norm_proj_sb", (True, True, True, False))
    o1, ctot, visited = _sb_fwd(q1, k1, v1, bsz, nq)
    dh2, u1, loss_blk = _out_fwd_loss(o1, g1, h1, sw_out, post1, target.reshape(bsz * seq, D_MODEL), nq, "out_fwd_sb")

    do1, dg1, g_sw_out, g_sw_out16, g_post1 = _out_bwd(dh2, u1, o1, g1, sw_out, post1, None, "out_bwd_sb", True)
    dq1, dk1, dv1 = _sb_bwd(q1, k1, v1, ctot, visited, do1, bsz, nq)
    ds1 = (dq1, dk1, dv1, dg1)
    dh1, g_pre1 = _proj_bwd(ds1, sw_in, h1, pre1, dh2, "proj_bwd_sb")
    g_sw_in = [_weight_grad(y1, d, "wgrad_sb_%d" % j) for j, d in enumerate(ds1)]

    do0, dg0, g_hw_out, g_hw_out16, g_post0, g_head = _out_bwd(dh1, u0, o0, g0, hw_out, post0, head_gain, "out_bwd_hgrn",
                                                               False)
    ready = (jnp.stack([g16 for _, g16 in g_sw_in]), g_sw_out16.reshape(N_CHIPS, d4, D_MODEL),
             g_hw_out16.reshape(N_CHIPS, d4, D_MODEL))
    dq0, df0, dv0, dlam, land_sw_in, land_sw_out, land_hw_out = _hgrn_bwd(q0, f0, v0, lam, sst, sym, do0, bsz, nb, ready)
    ds0 = (dq0, df0, dv0, dg0)
    g_hw_in = [_weight_grad(y0, d, "wgrad_hgrn_%d" % j) for j, d in enumerate(ds0)]
    last = (jnp.stack([g16 for _, g16 in g_hw_in]),)
    grad_x, dh_front, g_pre0, land_hw_in = _proj_bwd(ds0, hw_in, h0, pre0, dh1, "proj_bwd_hgrn", last, nq)

    grad_x = grad_x.reshape(bsz, seq, D_MODEL)
    g_meta = jnp.sum(dh_front.reshape(bsz, TILE, D_MODEL)[:, N_PAD:, :], axis=0)
    g_lam = jnp.sum(dlam, axis=0)
    small = jnp.concatenate([g_pre0, g_pre1, g_post0, g_post1, g_lam, g_lam,
                             jnp.pad(g_head, ((0, 0), (0, D_MODEL - D_HEAD))), g_meta,
                             jnp.pad(loss_blk[0:1], ((0, 0), (0, D_MODEL - loss_blk.shape[1])))], axis=0)
    rows4 = lambda g: [g[j * d4:(j + 1) * d4] for j in range(N_CHIPS)]
    large = dict(hw_in=(land_hw_in, [g for g, _ in g_hw_in]), sw_in=(land_sw_in, [g for g, _ in g_sw_in]),
                 hw_out=(land_hw_out, rows4(g_hw_out)), sw_out=(land_sw_out, rows4(g_sw_out)))
    return grad_x, small, large


def _prep_weights(hw_in, sw_in, hw_out, sw_out, meta):
    def body(hi_ref, si_ref, ho_ref, so_ref, m_ref, ghi, gm, si16, so16, ho16, far_send, far_recv, near_send, near_recv):
        x, y, c = _place()
        me = 2 * x + y
        ghi[me] = hi_ref[0].astype(_MXU)
        gm[me] = m_ref[...]
        si16[...] = si_ref[0].astype(_MXU)
        so16[...] = so_ref[0].astype(_MXU)
        ho16[...] = ho_ref[0].astype(_MXU)
        outs = (ghi, gm)
        n = len(outs)
        peers = [(1 - x, y), (x, 1 - y), (1 - x, 1 - y)]

        def half(a, slot, which):
            rows = outs[a].shape[1] // 2
            return outs[a].at[slot, pl.ds(which * rows, rows), :]

        def far(r, a, slot):
            px, py = peers[r]
            return pltpu.make_async_remote_copy(
                src_ref=half(a, slot, c), dst_ref=half(a, slot, c), send_sem=far_send.at[r * n + a],
                recv_sem=far_recv.at[r * n + a], device_id=(px, py, c), device_id_type=MESH)

        def near(r, a, which):
            px, py = peers[r]
            return pltpu.make_async_remote_copy(
                src_ref=half(a, 2 * px + py, which), dst_ref=half(a, 2 * px + py, which),
                send_sem=near_send.at[r * n + a], recv_sem=near_recv.at[r * n + a],
                device_id=(x, y, 1 - c), device_id_type=MESH)

        for r in range(3):
            for a in range(n):
                far(r, a, me).start()
        for r, (px, py) in enumerate(peers):
            for a in range(n):
                far(r, a, 2 * px + py).wait_recv()
                near(r, a, c).start()
        for r in range(3):
            for a in range(n):
                near(r, a, 1 - c).wait_recv()
        for r in range(3):
            for a in range(n):
                far(r, a, me).wait_send()
                near(r, a, c).wait_send()

    d4 = D_MODEL // N_CHIPS
    vm = pl.BlockSpec(memory_space=pltpu.VMEM)
    return pl.pallas_call(
        body, name="prep_weights",
        in_specs=[vm] * 5, out_specs=[vm] * 5,
        out_shape=[jax.ShapeDtypeStruct((N_CHIPS, D_MODEL, D_MODEL), _MXU), jax.ShapeDtypeStruct((N_CHIPS, N_META, d4), F32),
                   jax.ShapeDtypeStruct((D_MODEL, D_MODEL), _MXU), jax.ShapeDtypeStruct((d4, D_MODEL), _MXU),
                   jax.ShapeDtypeStruct((d4, D_MODEL), _MXU)],
        scratch_shapes=[pltpu.SemaphoreType.DMA((6,))] * 4,
        compiler_params=pltpu.CompilerParams(vmem_limit_bytes=VMEM_LIMIT),
    )(hw_in, sw_in, hw_out, sw_out, meta)


def _scatter_small(small):
    def body(sm, lsm, send_sems, recv_sems, local_sem):
        x, y, c = _place()
        mine = 4 * x + 2 * y + c
        local = pltpu.make_async_copy(sm, lsm.at[mine], local_sem)
        local.start()

        def copy(rel, src_dev, to):
            return pltpu.make_async_remote_copy(
                src_ref=sm, dst_ref=lsm.at[src_dev], send_sem=send_sems.at[rel - 1], recv_sem=recv_sems.at[rel - 1],
                device_id=to, device_id_type=MESH)

        flip = lambda bit, v: 1 - v if bit else v
        rels = [(rel, flip(rel & 4, x), flip(rel & 2, y), flip(rel & 1, c)) for rel in range(1, N_DEV)]
        sends = [copy(rel, mine, (px, py, pc)) for rel, px, py, pc in rels]
        for cp in sends:
            cp.start()
        for rel, px, py, pc in rels:
            copy(rel, 4 * px + 2 * py + pc, (px, py, pc)).wait_recv()
        for cp in sends:
            cp.wait_send()
        local.wait()

    hbm = pl.BlockSpec(memory_space=pl.ANY)
    return pl.pallas_call(
        body, name="scatter_small", in_specs=[hbm], out_specs=hbm,
        out_shape=jax.ShapeDtypeStruct((N_DEV, SMALL_ROWS, D_MODEL), F32),
        scratch_shapes=[pltpu.SemaphoreType.DMA((N_DEV - 1,)), pltpu.SemaphoreType.DMA((N_DEV - 1,)),
                        pltpu.SemaphoreType.DMA(())],
    )(small)


def _sum_slots(landed, own, me, name):
    n, rows, _ = landed.shape
    tm = rows if rows < 256 else 256

    def body(me_ref, l_ref, o0, o1, o2, o3, out_ref):
        acc = None
        for k, o in enumerate((o0, o1, o2, o3)):
            term = jnp.where(me_ref[0] == k, o[...], l_ref[k].astype(F32))
            acc = term if acc is None else acc + term
        out_ref[...] = acc

    blk = pl.BlockSpec((tm, D_MODEL), lambda i: (i, 0))
    return pl.pallas_call(
        body, name=name, grid=(rows // tm,),
        in_specs=[pl.BlockSpec(memory_space=pltpu.SMEM), pl.BlockSpec((n, tm, D_MODEL), lambda i: (0, i, 0))] + [blk] * 4,
        out_specs=blk, out_shape=jax.ShapeDtypeStruct((rows, D_MODEL), F32),
        compiler_params=_params(("arbitrary",)),
    )(me, landed, *own)


def _swap_with_sibling(parts):
    def body(a0, a1, a2, a3, b0, b1, b2, b3, send_sems, recv_sems):
        x, y, c = _place()
        copies = [pltpu.make_async_remote_copy(src_ref=s, dst_ref=d, send_sem=send_sems.at[a], recv_sem=recv_sems.at[a],
                                               device_id=(x, y, 1 - c), device_id_type=MESH)
                  for a, (s, d) in enumerate(zip((a0, a1, a2, a3), (b0, b1, b2, b3)))]
        for cp in copies:
            cp.start()
        for cp in copies:
            cp.wait()

    hbm = pl.BlockSpec(memory_space=pl.ANY)
    return pl.pallas_call(
        body, name="swap_with_sibling", in_specs=[hbm] * 4, out_specs=[hbm] * 4,
        out_shape=[jax.ShapeDtypeStruct(p.shape, F32) for p in parts],
        scratch_shapes=[pltpu.SemaphoreType.DMA((4,)), pltpu.SemaphoreType.DMA((4,))],
    )(*parts)


def _adamw_math(w, g, m, v):
    m = ADAM_B1 * m + (1.0 - ADAM_B1) * g
    v = ADAM_B2 * v + (1.0 - ADAM_B2) * (g * g)
    m_hat = m / (1.0 - ADAM_B1 ** ADAM_STEP)
    v_hat = v / (1.0 - ADAM_B2 ** ADAM_STEP)
    delta = -ADAM_LR * (m_hat / (jnp.sqrt(v_hat) + ADAM_EPS) + ADAM_WD * w)
    return delta, m, v


def _adamw(w, g_parts, m, v, name):
    rows, cols = w.shape
    tm = rows if rows < 256 else 256
    n = len(g_parts)

    def body(*refs):
        w_ref, m_ref, v_ref = refs[n:n + 3]
        g_ref, d_ref, nm_ref, nv_ref = refs[n + 3:]
        g = refs[0][...]
        for p in refs[1:n]:
            g = g + p[...]
        g_ref[...] = g
        d_ref[...], nm_ref[...], nv_ref[...] = _adamw_math(w_ref[...], g, m_ref[...], v_ref[...])

    blk = pl.BlockSpec((tm, cols), lambda i: (i, 0))
    return pl.pallas_call(
        body, name=name, grid=(rows // tm,), in_specs=[blk] * (n + 3), out_specs=[blk] * 4,
        out_shape=[jax.ShapeDtypeStruct((rows, cols), F32)] * 4,
        compiler_params=_params(("arbitrary",)),
    )(*g_parts, w, m, v)


def _lam_of(hgrn_lb):
    def body(lb_ref, o_ref):
        lb = lb_ref[...]
        e = jnp.exp(lb - jnp.max(lb, axis=0, keepdims=True))
        o_ref[...] = e[0:1, :] / jnp.sum(e, axis=0, keepdims=True)

    return pl.pallas_call(body, name="lam_of", out_shape=jax.ShapeDtypeStruct((1, D_MODEL), F32))(hgrn_lb)


def _small_grads(land_small, lam):
    def body(l_ref, lam_ref, o_ref):
        acc = l_ref[0]
        for k in range(1, N_DEV):
            acc = acc + l_ref[k]
        p = lam_ref[...]
        slope = p * (1.0 - p)
        row = _iota2((SMALL_ROWS, D_MODEL), 0)
        o_ref[...] = acc * jnp.where(row == 4, slope, jnp.where(row == 5, -slope, 1.0))

    return pl.pallas_call(body, name="small_grads",
                          out_shape=jax.ShapeDtypeStruct((SMALL_ROWS, D_MODEL), F32))(land_small, lam)


def kernel(x, meta_tokens, pre_norm, post_norm, hgrn_w_in, hgrn_lb, hgrn_out_norm, hgrn_w_out, sb_w_in, sb_w_out, loss_target, m_meta_tokens, m_pre_norm, m_post_norm, m_hgrn_w_in, m_hgrn_lb, m_hgrn_out_norm, m_hgrn_w_out, m_sb_w_in, m_sb_w_out, v_meta_tokens, v_pre_norm, v_post_norm, v_hgrn_w_in, v_hgrn_lb, v_hgrn_out_norm, v_hgrn_w_out, v_sb_w_in, v_sb_w_out):
    d4 = D_MODEL // N_CHIPS
    chip = 2 * lax.axis_index("x") + lax.axis_index("y")
    hw_in, meta4, sw_in16, sw_out16, hw_out16 = _prep_weights(hgrn_w_in, sb_w_in, hgrn_w_out, sb_w_out, meta_tokens)
    meta = meta4.transpose(1, 0, 2).reshape(N_META, D_MODEL)
    lam = _lam_of(hgrn_lb)
    grad_x, small, large = _local_step(
        x, loss_target, meta, pre_norm, post_norm, lam, hgrn_out_norm,
        hw_in, (sw_in16, sw_out16, hw_out16))

    me = jnp.reshape(chip, (1,)).astype(jnp.int32)
    parts = [_sum_slots(*large[n], me, "sum_" + n) for n in ("hw_in", "sw_in", "hw_out", "sw_out")]
    sib = _swap_with_sibling(parts)
    small = _small_grads(_scatter_small(small), lam)
    loss = small[SMALL_ROWS - 1, 0]

    res = {}
    res["hgrn_w_in"] = _adamw(hgrn_w_in[0], [parts[0], sib[0]], m_hgrn_w_in[0], v_hgrn_w_in[0], "adamw_hw_in")
    res["sb_w_in"] = _adamw(sb_w_in[0], [parts[1], sib[1]], m_sb_w_in[0], v_sb_w_in[0], "adamw_sw_in")
    res["hgrn_w_out"] = _adamw(hgrn_w_out[0], [parts[2], sib[2]], m_hgrn_w_out[0], v_hgrn_w_out[0], "adamw_hw_out")
    res["sb_w_out"] = _adamw(sb_w_out[0], [parts[3], sib[3]], m_sb_w_out[0], v_sb_w_out[0], "adamw_sw_out")
    res["pre_norm"] = _adamw(pre_norm, [small[0:2]], m_pre_norm, v_pre_norm, "adamw_pre")
    res["post_norm"] = _adamw(post_norm, [small[2:4]], m_post_norm, v_post_norm, "adamw_post")
    res["hgrn_lb"] = _adamw(hgrn_lb, [small[4:6]], m_hgrn_lb, v_hgrn_lb, "adamw_lb")
    res["hgrn_out_norm"] = _adamw(hgrn_out_norm, [small[6:7, :D_HEAD]], m_hgrn_out_norm, v_hgrn_out_norm, "adamw_head")
    g_meta = lax.dynamic_slice_in_dim(small[7:7 + N_META], chip * d4, d4, axis=1)
    res["meta_tokens"] = _adamw(meta_tokens, [g_meta], m_meta_tokens, v_meta_tokens, "adamw_meta")
    for n in ("hgrn_w_in", "hgrn_w_out", "sb_w_in", "sb_w_out"):
        res[n] = tuple(a[None] for a in res[n])
    order = ("meta_tokens", "pre_norm", "post_norm", "hgrn_w_in", "hgrn_lb", "hgrn_out_norm", "hgrn_w_out",
             "sb_w_in", "sb_w_out")
    return (loss, grad_x, *[res[n][0] for n in order], *[res[n][1] for n in order],
            *[res[n][2] for n in order], *[res[n][3] for n in order])
```

```python
import functools

import jax
import numpy as np
import jax.numpy as jnp
from jax import lax
from jax.experimental import pallas as pl
from jax.experimental.pallas import tpu as pltpu

F32 = jnp.float32
_MXU = jnp.bfloat16

D_MODEL = 1024
N_HEADS = 8
D_HEAD = 128
BLOCK = 128
N_META = 16
TILE = 256
N_PAD = TILE - N_META
UNDERFLOW = -105.0
EPS = 1e-6
SB_SCALE = D_HEAD ** -0.5
SOFTPLUS_LINEAR = 20.0
MASKED = -1e30
ADAM_LR, ADAM_B1, ADAM_B2, ADAM_EPS, ADAM_WD, ADAM_STEP = 0.001, 0.9, 0.999, 1e-08, 0.01, 10
N_CHIPS = 4
N_DEV = 8
SMALL_ROWS = 24
VMEM_LIMIT = 56 * 1024 * 1024
MESH = pl.DeviceIdType.MESH

_NT = (((1,), (1,)), ((), ()))
_TN = (((0,), (0,)), ((), ()))


def _mm(a, b):
    return jnp.dot(a.astype(_MXU), b.astype(_MXU), preferred_element_type=F32)


def _mm_nt(a, b):
    return lax.dot_general(a.astype(_MXU), b.astype(_MXU), _NT, preferred_element_type=F32)


def _mm_tn(a, b):
    return lax.dot_general(a.astype(_MXU), b.astype(_MXU), _TN, preferred_element_type=F32)


def _split2(x):
    hi = x.astype(_MXU)
    return hi, (x - hi.astype(F32)).astype(_MXU)


def _mm_s(a16, state):
    hi, lo = _split2(state)
    return jnp.dot(a16, hi, preferred_element_type=F32) + jnp.dot(a16, lo, preferred_element_type=F32)


def _mm_nt_s(a16, state):
    hi, lo = _split2(state)
    return (lax.dot_general(a16, hi, _NT, preferred_element_type=F32)
            + lax.dot_general(a16, lo, _NT, preferred_element_type=F32))


def _mm01_right(x, m01):
    return jnp.dot(x.astype(_MXU), m01, preferred_element_type=F32)


def _mm01_left(m01, x):
    hi, lo = _split2(x)
    return jnp.dot(m01, hi, preferred_element_type=F32) + jnp.dot(m01, lo, preferred_element_type=F32)


def _iota2(shape, dim):
    return lax.broadcasted_iota(jnp.int32, shape, dim)


def _row_tile(total, pref):
    t = pref
    while total % t:
        t -= BLOCK
    return t


def _params(sem, limit=VMEM_LIMIT):
    return pltpu.CompilerParams(dimension_semantics=sem, vmem_limit_bytes=limit)


def _sigmoid(x):
    return 1.0 / (1.0 + jnp.exp(-x))


def _grid_ends(ndim):
    first, last = True, True
    for d in range(ndim):
        first = first & (pl.program_id(d) == 0)
        last = last & (pl.program_id(d) == pl.num_programs(d) - 1)
    return first, last


def _place():
    return lax.axis_index("x"), lax.axis_index("y"), lax.axis_index("c")


def _exchange_scratch(n):
    return [pltpu.SemaphoreType.DMA((3 * n,)), pltpu.SemaphoreType.DMA((3 * n,)), pltpu.SemaphoreType.DMA((n,))]


def _chip_exchange(srcs, dsts, sems, slotted):
    send_sems, recv_sems, local_sems = sems
    x, y, c = _place()
    me = 2 * x + y
    peers = [(1 - x, y), (x, 1 - y), (1 - x, 1 - y)]
    n = len(dsts)

    def remote(r, a, sending):
        px, py = peers[r]
        p = 2 * px + py
        return pltpu.make_async_remote_copy(
            src_ref=srcs[a].at[p] if slotted else srcs[a], dst_ref=dsts[a].at[me if sending else p],
            send_sem=send_sems.at[r * n + a], recv_sem=recv_sems.at[r * n + a],
            device_id=(px, py, c), device_id_type=MESH)

    def local(a):
        return pltpu.make_async_copy(srcs[a].at[me] if slotted else srcs[a], dsts[a].at[me], local_sems.at[a])

    def start():
        for a in range(n):
            local(a).start()
        for r in range(3):
            for a in range(n):
                remote(r, a, True).start()

    def finish():
        for r in range(3):
            for a in range(n):
                remote(r, a, False).wait_recv()
        for r in range(3):
            for a in range(n):
                remote(r, a, True).wait_send()
        for a in range(n):
            local(a).wait()

    return start, finish


def _norm_proj(h, gain, w4, name, narrow):
    t = h.shape[0]
    tm = _row_tile(t, 2 * TILE)

    def body(h_ref, g_ref, w_ref, y_ref, s0, s1, s2, s3):
        x = h_ref[...]
        r = lax.rsqrt(jnp.mean(x * x, axis=-1, keepdims=True) + EPS)
        y = (x * r * g_ref[...]).astype(_MXU)
        y_ref[...] = y
        for j, s in enumerate((s0, s1, s2, s3)):
            s[...] = jnp.dot(y, w_ref[j], preferred_element_type=F32).astype(s.dtype)

    row = pl.BlockSpec((tm, D_MODEL), lambda i: (i, 0))
    return pl.pallas_call(
        body, name=name, grid=(t // tm,),
        in_specs=[row, pl.BlockSpec((1, D_MODEL), lambda i: (0, 0)),
                  pl.BlockSpec((4, D_MODEL, D_MODEL), lambda i: (0, 0, 0))],
        out_specs=[row] * 5,
        out_shape=[jax.ShapeDtypeStruct((t, D_MODEL), _MXU)]
        + [jax.ShapeDtypeStruct((t, D_MODEL), _MXU if n else F32) for n in narrow],
        compiler_params=_params(("arbitrary",)),
    )(h, gain, w4)


LEVELS = (64, 32, 16, 8, 4, 2, 1)
HEAD_GROUP = 8
CHUNKS = 2
SB_FWD_GROUP = 8
SB_BWD_GROUP = 4


def _hgrn_tables():
    r = np.arange(BLOCK)
    mats = [r[None, :] <= r[:, None]]
    x = r[:, None] ^ r[None, :]
    lv = np.full((BLOCK, BLOCK), len(LEVELS), np.int32)
    for i, m in enumerate(LEVELS):
        lv[(x >= m) & (x < 2 * m)] = i
    return jnp.asarray(np.concatenate(mats, 0).astype(np.float32), dtype=_MXU), jnp.asarray(lv)


def _hgrn_exponents(g, sums):
    b = _mm01_left(sums, g)
    row = _iota2((BLOCK, D_HEAD), 0)
    out = []
    for m in LEVELS:
        is_q = (row & m) != 0
        if m >= 4:
            grp = b.reshape(BLOCK // (2 * m), 2 * m, D_HEAD)
            ref = jnp.broadcast_to(grp[:, m - 1:m, :], grp.shape).reshape(BLOCK, D_HEAD)
            d = b - ref
            out.append(jnp.where(is_q, d, -d))
        elif m == 2:
            below, above = pltpu.roll(g, 1, axis=0), pltpu.roll(g, BLOCK - 1, axis=0)
            low = row & 3
            out.append(jnp.where(low == 3, g + below, jnp.where(low == 2, g, jnp.where(low == 0, above, 0.0))))
        else:
            out.append(jnp.where(is_q, g, 0.0))
    return b, out


def _hgrn_gates(fz, lam, chunk):
    pos = chunk * BLOCK + _iota2((BLOCK, D_HEAD), 0)
    live = pos >= N_PAD
    sg = _sigmoid(fz)
    f = lam + (1.0 - lam) * sg
    g = jnp.where(live, jnp.log(f), 0.0)
    k = jnp.where(live, (1.0 - lam) * (1.0 - sg), 0.0)
    return sg, f, g, k, live


def _level_operand(q, k, exponent, m):
    decay = jnp.exp(exponent)
    is_q = (_iota2((BLOCK, D_HEAD), 0) & m) != 0
    return is_q, decay, (jnp.where(is_q, q, k) * decay).astype(_MXU)


def _hgrn_fwd(qs, fs, vs, lam, bsz, nb, shards):
    t = qs.shape[0]
    sums, levels = _hgrn_tables()
    width = HEAD_GROUP * D_HEAD

    n_sh = len(shards)

    def body(q_ref, f_ref, v_ref, lam_ref, sums_ref, lv_ref, *rest):
        own, (o_ref, sst_ref, sym_ref), rest = rest[:n_sh], rest[n_sh:n_sh + 3], rest[n_sh + 3:]
        gathered, st_scr, sems = rest[:n_sh], rest[n_sh], rest[n_sh + 1:]
        n = pl.program_id(2)
        first, last = _grid_ends(3)
        start, finish = _chip_exchange(own, gathered, sems, slotted=False)
        pl.when(first)(start)

        @pl.when(n == 0)
        def _():
            st_scr[...] = jnp.zeros_like(st_scr)

        lv = lv_ref[...]
        r, c = _iota2((BLOCK, BLOCK), 0), _iota2((BLOCK, BLOCK), 1)
        for cc, hh in [(cc, hh) for cc in range(CHUNKS) for hh in range(HEAD_GROUP)]:
            ls, rows = slice(hh * D_HEAD, (hh + 1) * D_HEAD), slice(cc * BLOCK, (cc + 1) * BLOCK)
            st = st_scr[hh]
            sst_ref[0, hh, cc] = st
            q, v = q_ref[rows, ls], v_ref[rows, ls]
            _, _, g, k, _ = _hgrn_gates(f_ref[rows, ls], lam_ref[:, ls], CHUNKS * n + cc)
            b, exps = _hgrn_exponents(g, sums_ref[...])
            sym = jnp.zeros((BLOCK, BLOCK), F32)
            for li, m in enumerate(LEVELS):
                _, _, x16 = _level_operand(q, k, exps[li], m)
                sym = jnp.where(lv == li, lax.dot_general(x16, x16, _NT, preferred_element_type=F32), sym)
            sym = jnp.where(c == r, jnp.sum(q * k, axis=1, keepdims=True), sym).astype(_MXU)
            sym_ref[0, hh, cc] = sym
            o_ref[rows, ls] = _mm_nt(q * jnp.exp(b), st) + _mm(jnp.where(c <= r, sym, 0), v)
            b_end = b[BLOCK - 1:BLOCK, :]
            st_scr[hh] = st * jnp.exp(b_end) + _mm_tn(v, k * jnp.exp(b_end - b))
        pl.when(last)(finish)

    steps = nb // CHUNKS
    blk = pl.BlockSpec((CHUNKS * BLOCK, width), lambda b, h, n: (b * steps + n, h))
    hbm = pl.BlockSpec(memory_space=pl.ANY)
    return pl.pallas_call(
        body, name="hgrn_fwd", grid=(bsz, N_HEADS // HEAD_GROUP, steps),
        in_specs=[blk, blk, blk, pl.BlockSpec((1, width), lambda b, h, n: (0, h)),
                  pl.BlockSpec(sums.shape, lambda b, h, n: (0, 0)), pl.BlockSpec(levels.shape, lambda b, h, n: (0, 0))]
        + [hbm] * n_sh,
        out_specs=[blk] + [pl.BlockSpec((1, HEAD_GROUP, CHUNKS, D_HEAD, D_HEAD), lambda b, h, n: (b, h, n, 0, 0))] * 2
        + [hbm] * n_sh,
        out_shape=[jax.ShapeDtypeStruct((t, D_MODEL), F32),
                   jax.ShapeDtypeStruct((bsz, N_HEADS, nb, D_HEAD, D_HEAD), F32),
                   jax.ShapeDtypeStruct((bsz, N_HEADS, nb, D_HEAD, D_HEAD), _MXU)]
        + [jax.ShapeDtypeStruct((N_CHIPS,) + a.shape, a.dtype) for a in shards],
        scratch_shapes=[pltpu.VMEM((HEAD_GROUP, D_HEAD, D_HEAD), F32)] + _exchange_scratch(n_sh),
        compiler_params=_params(("arbitrary", "arbitrary", "arbitrary")),
    )(qs, fs, vs, lam, sums, levels, *shards)


def _hgrn_bwd(qs, fs, vs, lam, sst, sym, do, bsz, nb, outgoing):
    t = qs.shape[0]
    sums, levels = _hgrn_tables()
    width = HEAD_GROUP * D_HEAD

    n_out = len(outgoing)
    steps = nb // CHUNKS

    def body(q_ref, f_ref, v_ref, lam_ref, sst_ref, sym_ref, do_ref, sums_ref, lv_ref, *rest):
        send, (dq_ref, df_ref, dv_ref, dlam_ref), rest = rest[:n_out], rest[n_out:n_out + 4], rest[n_out + 4:]
        landed, dst_scr, gsum_scr, sems = rest[:n_out], rest[n_out], rest[n_out + 1], rest[n_out + 2:]
        n = pl.program_id(2)
        first, last = _grid_ends(3)
        start, finish = _chip_exchange(send, landed, sems, slotted=True)
        pl.when(first)(start)

        @pl.when(n == 0)
        def _():
            dst_scr[...] = jnp.zeros_like(dst_scr)
            gsum_scr[...] = jnp.zeros_like(gsum_scr)
            dlam_ref[...] = jnp.zeros_like(dlam_ref)

        lv = lv_ref[...]
        r, c = _iota2((BLOCK, BLOCK), 0), _iota2((BLOCK, BLOCK), 1)
        for cc, hh in [(cc, hh) for cc in reversed(range(CHUNKS)) for hh in range(HEAD_GROUP)]:
            ls, rows = slice(hh * D_HEAD, (hh + 1) * D_HEAD), slice(cc * BLOCK, (cc + 1) * BLOCK)
            lam = lam_ref[:, ls]
            q, v, do = q_ref[rows, ls], v_ref[rows, ls], do_ref[rows, ls]
            sg, f, g, k, live = _hgrn_gates(f_ref[rows, ls], lam, CHUNKS * (steps - 1 - n) + cc)
            b, exps = _hgrn_exponents(g, sums_ref[...])
            do16, v16 = do.astype(_MXU), v.astype(_MXU)
            da = lax.dot_general(do16, v16, _NT, preferred_element_type=F32)
            da_sym = jnp.where(c < r, da, da.T)
            dq = jnp.zeros((BLOCK, D_HEAD), F32)
            dqk = jnp.zeros((BLOCK, D_HEAD), F32)
            db_q = jnp.zeros((BLOCK, D_HEAD), F32)
            db_qk = jnp.zeros((BLOCK, D_HEAD), F32)
            for li, m in enumerate(LEVELS):
                is_q, decay, x16 = _level_operand(q, k, exps[li], m)
                y = jnp.dot(jnp.where(lv == li, da_sym, 0.0).astype(_MXU), x16, preferred_element_type=F32)
                dx = y * decay
                dq = dq + jnp.where(is_q, dx, 0.0)
                dqk = dqk + dx
                p = x16.astype(F32) * y
                db_q = db_q + jnp.where(is_q, p, 0.0)
                db_qk = db_qk + p
            dk = dqk - dq
            db = 2.0 * db_q - db_qk
            a_t = jnp.where(c >= r, sym_ref[0, hh, cc], 0)
            st, dst = sst_ref[0, hh, cc], dst_scr[hh]
            eb = jnp.exp(b)
            b_end = b[BLOCK - 1:BLOCK, :]
            dec = jnp.exp(b_end - b)
            qh16, kt16 = (q * eb).astype(_MXU), (k * dec).astype(_MXU)
            dq_st = _mm_s(do16, st)
            dk_st = _mm_s(v16, dst)
            d_diag = jnp.sum(do * v, axis=1, keepdims=True)
            dq_ref[rows, ls] = (dq + d_diag * k + eb * dq_st).astype(dq_ref.dtype)
            dk = dk + d_diag * q + dec * dk_st
            dv_ref[rows, ls] = (jnp.dot(a_t, do16, preferred_element_type=F32) + _mm_nt_s(kt16, dst)).astype(dv_ref.dtype)
            dst_scr[hh] = dst * jnp.exp(b_end) + lax.dot_general(do16, qh16, _TN, preferred_element_type=F32)
            db = db + (qh16.astype(F32) * dq_st - kt16.astype(F32) * dk_st)
            dg = _mm01_left((c >= r).astype(_MXU), db) + gsum_scr[:, ls]
            gsum_scr[:, ls] = gsum_scr[:, ls] + jnp.sum(db, axis=0, keepdims=True)
            slope = (1.0 - lam) * sg * (1.0 - sg)
            df_ref[rows, ls] = jnp.where(live, dg * slope / f - dk * slope, 0.0).astype(df_ref.dtype)
            dl = jnp.where(live, (dg / f - dk) * (1.0 - sg), 0.0)
            dlam_ref[0, :, ls] = dlam_ref[0, :, ls] + jnp.sum(dl, axis=0, keepdims=True)
        pl.when(last)(finish)

    blk = pl.BlockSpec((CHUNKS * BLOCK, width), lambda b, h, n: (b * steps + steps - 1 - n, h))
    hbm = pl.BlockSpec(memory_space=pl.ANY)
    return pl.pallas_call(
        body, name="hgrn_bwd", grid=(bsz, N_HEADS // HEAD_GROUP, steps),
        in_specs=[blk, blk, blk, pl.BlockSpec((1, width), lambda b, h, n: (0, h)),
                  pl.BlockSpec((1, HEAD_GROUP, CHUNKS, D_HEAD, D_HEAD), lambda b, h, n: (b, h, steps - 1 - n, 0, 0)),
                  pl.BlockSpec((1, HEAD_GROUP, CHUNKS, D_HEAD, D_HEAD), lambda b, h, n: (b, h, steps - 1 - n, 0, 0)),
                  blk, pl.BlockSpec(sums.shape, lambda b, h, n: (0, 0)), pl.BlockSpec(levels.shape, lambda b, h, n: (0, 0))]
        + [hbm] * n_out,
        out_specs=[blk, blk, blk, pl.BlockSpec((1, 1, width), lambda b, h, n: (b, 0, h))] + [hbm] * n_out,
        out_shape=[jax.ShapeDtypeStruct((t, D_MODEL), _MXU)] * 3 + [jax.ShapeDtypeStruct((bsz, 1, D_MODEL), F32)]
        + [jax.ShapeDtypeStruct(a.shape, a.dtype) for a in outgoing],
        scratch_shapes=[pltpu.VMEM((HEAD_GROUP, D_HEAD, D_HEAD), F32), pltpu.VMEM((1, width), F32)]
        + _exchange_scratch(n_out),
        compiler_params=_params(("arbitrary", "arbitrary", "arbitrary")),
    )(qs, fs, vs, lam, sst, sym, do, sums, levels, *outgoing)


def _sb_valid(ahead, col, i, j):
    return (ahead < (i - j) * TILE) & (col >= N_PAD - j * TILE)


def _sb_logits(q16, k_blk, valid):
    z = jnp.where(valid, lax.dot_general(q16, k_blk.astype(_MXU), _NT, preferred_element_type=F32) * SB_SCALE, MASKED)
    softplus = jnp.where(z > SOFTPLUS_LINEAR, z, jnp.log(1.0 + jnp.exp(jnp.minimum(z, SOFTPLUS_LINEAR))))
    return -softplus, z - softplus


def _sb_fwd(qs, ks, vs, bsz, nq):
    t = qs.shape[0]
    lp = nq * TILE
    width = SB_FWD_GROUP * D_HEAD
    groups = N_HEADS // SB_FWD_GROUP
    lanes = [slice(hh * D_HEAD, (hh + 1) * D_HEAD) for hh in range(SB_FWD_GROUP)]

    def body(q_ref, k_ref, v_ref, o_ref, c_ref, n_ref):
        b, h, i = pl.program_id(0), pl.program_id(1), pl.program_id(2)
        q16 = [q_ref[:, ls].astype(_MXU) for ls in lanes]
        r, c = _iota2((TILE, TILE), 0), _iota2((TILE, TILE), 1)
        after = (r > c).astype(_MXU)

        def more(carry):
            jj, _, _, top = carry
            return (jj <= i) & (top > UNDERFLOW)

        def step(carry):
            jj, accs, sums, _ = carry
            j = i - jj
            ks_ = pl.ds(pl.multiple_of(j * TILE, TILE), TILE)
            valid = _sb_valid(c - r, c, i, j)
            new_accs, new_sums = [], []
            for hh, ls in enumerate(lanes):
                keep, log_beta = _sb_logits(q16[hh], k_ref[ks_, ls], valid)
                after_s = _mm01_right(keep, after)
                a = jnp.exp(log_beta + (sums[hh] + after_s))
                new_accs.append(accs[hh] + _mm(a, v_ref[ks_, ls]))
                new_sums.append(sums[hh] + (after_s[:, 0:1] + keep[:, 0:1]))
            top = functools.reduce(jnp.maximum, [jnp.max(x) for x in new_sums])
            return jj + 1, tuple(new_accs), tuple(new_sums), top

        init = (jnp.int32(0), tuple(jnp.zeros((TILE, D_HEAD), F32) for _ in lanes),
                tuple(jnp.zeros((TILE, 1), F32) for _ in lanes), jnp.float32(0.0))
        visited, accs, sums, _ = lax.while_loop(more, step, init)
        for hh, ls in enumerate(lanes):
            o_ref[:, ls] = accs[hh]
            c_ref[:, ls] = jnp.broadcast_to(sums[hh], (TILE, D_HEAD))
        n_ref[(b * groups + h) * nq + i] = visited.astype(F32)

    blk = pl.BlockSpec((TILE, width), lambda b, h, i: (b * nq + i, h))
    seq = pl.BlockSpec((lp, width), lambda b, h, i: (b, h))
    return pl.pallas_call(
        body, name="sb_fwd", grid=(bsz, groups, nq),
        in_specs=[blk, seq, seq], out_specs=[blk, blk, pl.BlockSpec(memory_space=pltpu.SMEM)],
        out_shape=[jax.ShapeDtypeStruct((t, D_MODEL), F32)] * 2 + [jax.ShapeDtypeStruct((bsz * groups * nq,), F32)],
        compiler_params=_params(("arbitrary", "arbitrary", "arbitrary")),
    )(qs, ks, vs)


def _sb_bwd(qs, ks, vs, ctot, visited, do, bsz, nq):
    t = qs.shape[0]
    lp = nq * TILE
    width = SB_BWD_GROUP * D_HEAD
    groups = N_HEADS // SB_BWD_GROUP
    lanes = [slice(hh * D_HEAD, (hh + 1) * D_HEAD) for hh in range(SB_BWD_GROUP)]

    def body(n_ref, q_ref, k_ref, v_ref, c_ref, do_ref, dq_ref, dk_ref, dv_ref, dk_acc, dv_acc):
        b, h, i = pl.program_id(0), pl.program_id(1), pl.program_id(2)

        @pl.when(i == 0)
        def _():
            dk_acc[...] = jnp.zeros_like(dk_acc)
            dv_acc[...] = jnp.zeros_like(dv_acc)

        q16 = [q_ref[:, ls].astype(_MXU) for ls in lanes]
        do16 = [do_ref[:, ls].astype(_MXU) for ls in lanes]
        q16_t = [q_ref[:, ls].astype(F32).T.astype(_MXU) for ls in lanes]
        do16_t = [do_ref[:, ls].astype(F32).T.astype(_MXU) for ls in lanes]
        totals = [c_ref[:, hh * D_HEAD:hh * D_HEAD + 1] for hh in range(SB_BWD_GROUP)]
        r, c = _iota2((TILE, TILE), 0), _iota2((TILE, TILE), 1)
        upto = (r <= c).astype(_MXU)
        before = (r < c).astype(_MXU)
        seen = n_ref[(b * (N_HEADS // SB_FWD_GROUP) + h * SB_BWD_GROUP // SB_FWD_GROUP) * nq + i]
        first = jnp.maximum(i + 1 - seen.astype(jnp.int32), 0)

        def step(j, carry):
            ks_ = pl.ds(pl.multiple_of(j * TILE, TILE), TILE)
            valid = _sb_valid(c - r, c, i, j)
            out = []
            for hh, ls in enumerate(lanes):
                dq, keep_pre, g_pre = carry[hh]
                k_blk, v_blk = k_ref[ks_, ls], v_ref[ks_, ls]
                keep, log_beta = _sb_logits(q16[hh], k_blk, valid)
                keep_upto = _mm01_right(keep, upto)
                a = jnp.exp(log_beta + (totals[hh] - keep_pre - keep_upto))
                da = lax.dot_general(do16[hh], v_blk.astype(_MXU), _NT, preferred_element_type=F32)
                g = a * da
                g_inside = _mm01_right(g, before)
                g_before = g_pre + g_inside
                beta = jnp.exp(log_beta)
                dz16 = (g * (1.0 - beta) - beta * g_before).astype(_MXU)
                dq = dq + jnp.dot(dz16, k_blk.astype(_MXU), preferred_element_type=F32)
                dk_acc[ls, ks_] += SB_SCALE * jnp.dot(q16_t[hh], dz16, preferred_element_type=F32)
                dv_acc[ls, ks_] += jnp.dot(do16_t[hh], a.astype(_MXU), preferred_element_type=F32)
                out.append((dq, keep_pre + keep_upto[:, TILE - 1:TILE],
                            g_pre + (g_inside[:, TILE - 1:TILE] + g[:, TILE - 1:TILE])))
            return tuple(out)

        zero_col = jnp.zeros((TILE, 1), F32)
        init = tuple((jnp.zeros((TILE, D_HEAD), F32), zero_col, zero_col) for _ in lanes)
        res = lax.fori_loop(first, i + 1, step, init)
        for hh, ls in enumerate(lanes):
            dq_ref[:, ls] = (SB_SCALE * res[hh][0]).astype(dq_ref.dtype)

        @pl.when(i == nq - 1)
        def _():
            dk_ref[...] = dk_acc[...].T.astype(dk_ref.dtype)
            dv_ref[...] = dv_acc[...].T.astype(dv_ref.dtype)

    blk = pl.BlockSpec((TILE, width), lambda b, h, i: (b * nq + i, h))
    seq = pl.BlockSpec((lp, width), lambda b, h, i: (b, h))
    seq_out = pl.BlockSpec((lp, width), lambda b, h, i: (b, h), pipeline_mode=pl.Buffered(1))
    return pl.pallas_call(
        body, name="sb_bwd", grid=(bsz, groups, nq),
        in_specs=[pl.BlockSpec(memory_space=pltpu.SMEM), blk, seq, seq, blk, blk], out_specs=[blk, seq_out, seq_out],
        out_shape=[jax.ShapeDtypeStruct((t, D_MODEL), _MXU)] * 3,
        scratch_shapes=[pltpu.VMEM((width, lp), F32)] * 2,
        compiler_params=_params(("arbitrary", "arbitrary", "arbitrary")),
    )(visited, qs, ks, vs, ctot, do)


def _head_norm(o, head_gain):
    outs, rs = [], []
    for h in range(N_HEADS):
        oh = o[:, h * D_HEAD:(h + 1) * D_HEAD]
        r = lax.rsqrt(jnp.mean(oh * oh, axis=-1, keepdims=True) + EPS)
        outs.append(oh * r)
        rs.append(r)
    return outs, rs


def _mix(o, gate, head_gain):
    if head_gain is None:
        on = o
    else:
        outs, _ = _head_norm(o, head_gain)
        on = jnp.concatenate([x * head_gain for x in outs], axis=1)
    return on, on * (gate * _sigmoid(gate))


def _out_fwd(o, gate, h_in, w_out, post_gain, head_gain, name):
    t = o.shape[0]
    tm = _row_tile(t, 2 * TILE)

    def body(o_ref, g_ref, h_ref, w_ref, pg_ref, hg_ref, ho_ref, u_ref):
        _, mix = _mix(o_ref[...], g_ref[...], hg_ref[...])
        u = jnp.dot(mix.astype(_MXU), w_ref[...], preferred_element_type=F32)
        u_ref[...] = u
        r = lax.rsqrt(jnp.mean(u * u, axis=-1, keepdims=True) + EPS)
        ho_ref[...] = h_ref[...] + u * r * pg_ref[...]

    row = pl.BlockSpec((tm, D_MODEL), lambda i: (i, 0))
    vec = pl.BlockSpec((1, D_MODEL), lambda i: (0, 0))
    return pl.pallas_call(
        body, name=name, grid=(t // tm,),
        in_specs=[row, row, row, pl.BlockSpec((D_MODEL, D_MODEL), lambda i: (0, 0)), vec,
                  pl.BlockSpec((1, D_HEAD), lambda i: (0, 0))],
        out_specs=[row, row], out_shape=[jax.ShapeDtypeStruct((t, D_MODEL), F32)] * 2,
        compiler_params=_params(("arbitrary",)),
    )(o, gate, h_in, w_out, post_gain, head_gain)


def _out_fwd_loss(o, gate, h_in, w_out, post_gain, target, nq, name):
    t = o.shape[0]

    def body(o_ref, g_ref, h_ref, w_ref, pg_ref, t_ref, dh_ref, u_ref, l_ref):
        i = pl.program_id(0)

        @pl.when(i == 0)
        def _():
            l_ref[...] = jnp.zeros_like(l_ref)

        _, mix = _mix(o_ref[...], g_ref[...], None)
        u = jnp.dot(mix.astype(_MXU), w_ref[...], preferred_element_type=F32)
        u_ref[...] = u

        @pl.when(i % nq == 0)
        def _():
            dh_ref[...] = jnp.zeros_like(dh_ref)

        @pl.when(i % nq != 0)
        def _():
            r = lax.rsqrt(jnp.mean(u * u, axis=-1, keepdims=True) + EPS)
            e = h_ref[...] + u * r * pg_ref[...] - t_ref[...]
            dh_ref[...] = e * (1.0 / D_MODEL)
            l_ref[...] += jnp.sum(e * e) * (0.5 / D_MODEL)

    row = pl.BlockSpec((TILE, D_MODEL), lambda i: (i, 0))
    vec = pl.BlockSpec((1, D_MODEL), lambda i: (0, 0))
    return pl.pallas_call(
        body, name=name, grid=(t // TILE,),
        in_specs=[row, row, row, pl.BlockSpec((D_MODEL, D_MODEL), lambda i: (0, 0)), vec,
                  pl.BlockSpec((TILE, D_MODEL), lambda i: ((i // nq) * (nq - 1) + jnp.maximum(i % nq - 1, 0), 0))],
        out_specs=[row, row, pl.BlockSpec((8, 128), lambda i: (0, 0))],
        out_shape=[jax.ShapeDtypeStruct((t, D_MODEL), F32)] * 2 + [jax.ShapeDtypeStruct((8, 128), F32)],
        compiler_params=_params(("arbitrary",)),
    )(o, gate, h_in, w_out, post_gain, target)


def _out_bwd(dh, u, o, gate, w_out, post_gain, head_gain, name, narrow_do):
    t = o.shape[0]
    tm = _row_tile(t, 2 * TILE)
    has_head = head_gain is not None

    def body(*refs):
        if has_head:
            dh_ref, u_ref, o_ref, g_ref, w_ref, pg_ref, hg_ref, do_ref, dg_ref, gw_ref, gw16_ref, gp_ref, gh_ref = refs
            hg = hg_ref[...]
        else:
            dh_ref, u_ref, o_ref, g_ref, w_ref, pg_ref, do_ref, dg_ref, gw_ref, gw16_ref, gp_ref = refs
            hg = None
        first = pl.program_id(0) == 0

        @pl.when(first)
        def _():
            gw_ref[...] = jnp.zeros_like(gw_ref)
            gp_ref[...] = jnp.zeros_like(gp_ref)
            if has_head:
                gh_ref[...] = jnp.zeros_like(gh_ref)

        dr, u, o, gate = dh_ref[...], u_ref[...], o_ref[...], g_ref[...]
        r = lax.rsqrt(jnp.mean(u * u, axis=-1, keepdims=True) + EPS)
        un = u * r
        gp_ref[...] += jnp.sum(dr * un, axis=0, keepdims=True)
        dun = dr * pg_ref[...]
        du = r * (dun - un * jnp.mean(dun * un, axis=-1, keepdims=True))
        on, mix = _mix(o, gate, hg)
        du16 = du.astype(_MXU)
        gw_ref[...] += lax.dot_general(mix.astype(_MXU), du16, _TN, preferred_element_type=F32)
        dmix = lax.dot_general(du16, w_ref[...], _NT, preferred_element_type=F32)
        sg = _sigmoid(gate)
        dg_ref[...] = (dmix * on * (sg * (1.0 + gate * (1.0 - sg)))).astype(dg_ref.dtype)
        don = dmix * (gate * sg)
        if has_head:
            outs, rs = _head_norm(o, hg)
            gh = jnp.zeros((1, D_HEAD), F32)
            cols = []
            for h in range(N_HEADS):
                dn = don[:, h * D_HEAD:(h + 1) * D_HEAD]
                gh = gh + jnp.sum(dn * outs[h], axis=0, keepdims=True)
                dnn = dn * hg
                cols.append(rs[h] * (dnn - outs[h] * jnp.mean(dnn * outs[h], axis=-1, keepdims=True)))
            gh_ref[...] += gh
            do_ref[...] = jnp.concatenate(cols, axis=1)
        else:
            do_ref[...] = don.astype(do_ref.dtype)

        @pl.when(pl.program_id(0) == pl.num_programs(0) - 1)
        def _():
            gw16_ref[...] = gw_ref[...].astype(_MXU)

    row = pl.BlockSpec((tm, D_MODEL), lambda i: (i, 0))
    vec = pl.BlockSpec((1, D_MODEL), lambda i: (0, 0))
    mat = pl.BlockSpec((D_MODEL, D_MODEL), lambda i: (0, 0))
    in_specs = [row, row, row, row, mat, vec]
    args = [dh, u, o, gate, w_out, post_gain]
    out_specs = [row, row, mat, mat, vec]
    out_shape = [jax.ShapeDtypeStruct((t, D_MODEL), _MXU if narrow_do else F32),
                 jax.ShapeDtypeStruct((t, D_MODEL), _MXU)] + [jax.ShapeDtypeStruct((D_MODEL, D_MODEL), F32),
                                                                  jax.ShapeDtypeStruct((D_MODEL, D_MODEL), _MXU),
                                                                  jax.ShapeDtypeStruct((1, D_MODEL), F32)]
    if has_head:
        in_specs.append(pl.BlockSpec((1, D_HEAD), lambda i: (0, 0)))
        args.append(head_gain)
        out_specs.append(pl.BlockSpec((1, D_HEAD), lambda i: (0, 0)))
        out_shape.append(jax.ShapeDtypeStruct((1, D_HEAD), F32))
    return pl.pallas_call(
        body, name=name, grid=(t // tm,), in_specs=in_specs, out_specs=out_specs, out_shape=out_shape,
        compiler_params=_params(("arbitrary",)),
    )(*args)


def _proj_bwd(ds, w4, h_in, gain, dh_out, name, outgoing=(), nq=None):
    t = h_in.shape[0]
    tm = _row_tile(t, 2 * TILE) if nq is None else TILE
    n_out = len(outgoing)
    n_dh = 1 if nq is None else 2

    def body(d0, d1, d2, d3, w_ref, h_ref, g_ref, dho_ref, *rest):
        send, dh_refs, gg_ref, rest = rest[:n_out], rest[n_out:n_out + n_dh], rest[n_out + n_dh], rest[n_out + n_dh + 1:]
        landed, sems = rest[:n_out], rest[n_out:]
        if n_out:
            first, last = _grid_ends(1)
            start, finish = _chip_exchange(send, landed, sems, slotted=True)
            pl.when(first)(start)

        @pl.when(pl.program_id(0) == 0)
        def _():
            gg_ref[...] = jnp.zeros_like(gg_ref)

        dy = jnp.zeros((tm, D_MODEL), F32)
        for j, d in enumerate((d0, d1, d2, d3)):
            dy = dy + lax.dot_general(d[...].astype(_MXU), w_ref[j], _NT, preferred_element_type=F32)
        x = h_ref[...]
        r = lax.rsqrt(jnp.mean(x * x, axis=-1, keepdims=True) + EPS)
        xn = x * r
        gg_ref[...] += jnp.sum(dy * xn, axis=0, keepdims=True)
        dxn = dy * g_ref[...]
        dh = dho_ref[...] + r * (dxn - xn * jnp.mean(dxn * xn, axis=-1, keepdims=True))
        if nq is None:
            dh_refs[0][...] = dh
        else:
            in_front = pl.program_id(0) % nq == 0

            @pl.when(in_front)
            def _():
                dh_refs[1][...] = dh

            @pl.when(jnp.logical_not(in_front))
            def _():
                dh_refs[0][...] = dh
        if n_out:
            pl.when(last)(finish)

    row = pl.BlockSpec((tm, D_MODEL), lambda i: (i, 0))
    vec = pl.BlockSpec((1, D_MODEL), lambda i: (0, 0))
    hbm = pl.BlockSpec(memory_space=pl.ANY)
    if nq is None:
        dh_specs, dh_shapes = [row], [jax.ShapeDtypeStruct((t, D_MODEL), F32)]
    else:
        dh_specs = [pl.BlockSpec((TILE, D_MODEL), lambda i: ((i // nq) * (nq - 1) + jnp.maximum(i % nq - 1, 0), 0)),
                    pl.BlockSpec((TILE, D_MODEL), lambda i: (i // nq, 0))]
        dh_shapes = [jax.ShapeDtypeStruct((t // nq * (nq - 1), D_MODEL), F32),
                     jax.ShapeDtypeStruct((t // nq, D_MODEL), F32)]
    return pl.pallas_call(
        body, name=name, grid=(t // tm,),
        in_specs=[row] * 4 + [pl.BlockSpec((4, D_MODEL, D_MODEL), lambda i: (0, 0, 0)), row, vec, row] + [hbm] * n_out,
        out_specs=dh_specs + [vec] + [hbm] * n_out,
        out_shape=dh_shapes + [jax.ShapeDtypeStruct((1, D_MODEL), F32)]
        + [jax.ShapeDtypeStruct(a.shape, a.dtype) for a in outgoing],
        scratch_shapes=_exchange_scratch(n_out) if n_out else [],
        compiler_params=_params(("arbitrary",)),
    )(*ds, w4, h_in, gain, dh_out, *outgoing)


def _weight_grad(y, d, name):
    t = y.shape[0]
    tk = _row_tile(t, t // 4)

    def body(y_ref, d_ref, g_ref, g16_ref):
        @pl.when(pl.program_id(0) == 0)
        def _():
            g_ref[...] = jnp.zeros_like(g_ref)

        g_ref[...] += lax.dot_general(y_ref[...], d_ref[...].astype(_MXU), _TN, preferred_element_type=F32)

        @pl.when(pl.program_id(0) == pl.num_programs(0) - 1)
        def _():
            g16_ref[...] = g_ref[...].astype(_MXU)

    row = pl.BlockSpec((tk, D_MODEL), lambda i: (i, 0))
    mat = pl.BlockSpec((D_MODEL, D_MODEL), lambda i: (0, 0))
    return pl.pallas_call(
        body, name=name, grid=(t // tk,), in_specs=[row, row], out_specs=[mat, mat],
        out_shape=[jax.ShapeDtypeStruct((D_MODEL, D_MODEL), F32), jax.ShapeDtypeStruct((D_MODEL, D_MODEL), _MXU)],
        compiler_params=_params(("arbitrary",)),
    )(y, d)


def _local_step(x, target, meta, pre_norm, post_norm, lam, head_gain, hw_in, shards):
    bsz, seq, _ = x.shape
    nq = seq // TILE + 1
    nb = nq * (TILE // BLOCK)
    lp = nq * TILE
    t = bsz * lp
    d4 = D_MODEL // N_CHIPS
    front = jnp.concatenate([jnp.zeros((N_PAD, D_MODEL), F32), meta], axis=0)
    h0 = jnp.concatenate([jnp.broadcast_to(front[None], (bsz, TILE, D_MODEL)), x], axis=1).reshape(t, D_MODEL)
    pre0, pre1, post0, post1 = pre_norm[0:1], pre_norm[1:2], post_norm[0:1], post_norm[1:2]

    y0, q0, f0, v0, g0 = _norm_proj(h0, pre0, hw_in, "norm_proj_hgrn", (False,) * 4)
    o0, sst, sym, sw_in, sw_out, hw_out = _hgrn_fwd(q0, f0, v0, lam, bsz, nb, shards)
    sw_out, hw_out = sw_out.reshape(D_MODEL, D_MODEL), hw_out.reshape(D_MODEL, D_MODEL)
    h1, u0 = _out_fwd(o0, g0, h0, hw_out, post0, head_gain, "out_fwd_hgrn")
    y1, q1, k1, v1, g1 = _norm_proj(h1, pre1, sw_in, "norm_proj_sb", (True, True, True, False))
    o1, ctot, visited = _sb_fwd(q1, k1, v1, bsz, nq)
    dh2, u1, loss_blk = _out_fwd_loss(o1, g1, h1, sw_out, post1, target.reshape(bsz * seq, D_MODEL), nq, "out_fwd_sb")

    do1, dg1, g_sw_out, g_sw_out16, g_post1 = _out_bwd(dh2, u1, o1, g1, sw_out, post1, None, "out_bwd_sb", True)
    dq1, dk1, dv1 = _sb_bwd(q1, k1, v1, ctot, visited, do1, bsz, nq)
    ds1 = (dq1, dk1, dv1, dg1)
    dh1, g_pre1 = _proj_bwd(ds1, sw_in, h1, pre1, dh2, "proj_bwd_sb")
    g_sw_in = [_weight_grad(y1, d, "wgrad_sb_%d" % j) for j, d in enumerate(ds1)]

    do0, dg0, g_hw_out, g_hw_out16, g_post0, g_head = _out_bwd(dh1, u0, o0, g0, hw_out, post0, head_gain, "out_bwd_hgrn",
                                                               False)
    ready = (jnp.stack([g16 for _, g16 in g_sw_in]), g_sw_out16.reshape(N_CHIPS, d4, D_MODEL),
             g_hw_out16.reshape(N_CHIPS, d4, D_MODEL))
    dq0, df0, dv0, dlam, land_sw_in, land_sw_out, land_hw_out = _hgrn_bwd(q0, f0, v0, lam, sst, sym, do0, bsz, nb, ready)
    ds0 = (dq0, df0, dv0, dg0)
    g_hw_in = [_weight_grad(y0, d, "wgrad_hgrn_%d" % j) for j, d in enumerate(ds0)]
    last = (jnp.stack([g16 for _, g16 in g_hw_in]),)
    grad_x, dh_front, g_pre0, land_hw_in = _proj_bwd(ds0, hw_in, h0, pre0, dh1, "proj_bwd_hgrn", last, nq)

    grad_x = grad_x.reshape(bsz, seq, D_MODEL)
    g_meta = jnp.sum(dh_front.reshape(bsz, TILE, D_MODEL)[:, N_PAD:, :], axis=0)
    g_lam = jnp.sum(dlam, axis=0)
    small = jnp.concatenate([g_pre0, g_pre1, g_post0, g_post1, g_lam, g_lam,
                             jnp.pad(g_head, ((0, 0), (0, D_MODEL - D_HEAD))), g_meta,
                             jnp.pad(loss_blk[0:1], ((0, 0), (0, D_MODEL - loss_blk.shape[1])))], axis=0)
    rows4 = lambda g: [g[j * d4:(j + 1) * d4] for j in range(N_CHIPS)]
    large = dict(hw_in=(land_hw_in, [g for g, _ in g_hw_in]), sw_in=(land_sw_in, [g for g, _ in g_sw_in]),
                 hw_out=(land_hw_out, rows4(g_hw_out)), sw_out=(land_sw_out, rows4(g_sw_out)))
    return grad_x, small, large


def _prep_weights(hw_in, sw_in, hw_out, sw_out, meta):
    def body(hi_ref, si_ref, ho_ref, so_ref, m_ref, ghi, gm, si16, so16, ho16, far_send, far_recv, near_send, near_recv):
        x, y, c = _place()
        me = 2 * x + y
        ghi[me] = hi_ref[0].astype(_MXU)
        gm[me] = m_ref[...]
        si16[...] = si_ref[0].astype(_MXU)
        so16[...] = so_ref[0].astype(_MXU)
        ho16[...] = ho_ref[0].astype(_MXU)
        outs = (ghi, gm)
        n = len(outs)
        peers = [(1 - x, y), (x, 1 - y), (1 - x, 1 - y)]

        def half(a, slot, which):
            rows = outs[a].shape[1] // 2
            return outs[a].at[slot, pl.ds(which * rows, rows), :]

        def far(r, a, slot):
            px, py = peers[r]
            return pltpu.make_async_remote_copy(
                src_ref=half(a, slot, c), dst_ref=half(a, slot, c), send_sem=far_send.at[r * n + a],
                recv_sem=far_recv.at[r * n + a], device_id=(px, py, c), device_id_type=MESH)

        def near(r, a, which):
            px, py = peers[r]
            return pltpu.make_async_remote_copy(
                src_ref=half(a, 2 * px + py, which), dst_ref=half(a, 2 * px + py, which),
                send_sem=near_send.at[r * n + a], recv_sem=near_recv.at[r * n + a],
                device_id=(x, y, 1 - c), device_id_type=MESH)

        for r in range(3):
            for a in range(n):
                far(r, a, me).start()
        for r, (px, py) in enumerate(peers):
            for a in range(n):
                far(r, a, 2 * px + py).wait_recv()
                near(r, a, c).start()
        for r in range(3):
            for a in range(n):
                near(r, a, 1 - c).wait_recv()
        for r in range(3):
            for a in range(n):
                far(r, a, me).wait_send()
                near(r, a, c).wait_send()

    d4 = D_MODEL // N_CHIPS
    vm = pl.BlockSpec(memory_space=pltpu.VMEM)
    return pl.pallas_call(
        body, name="prep_weights",
        in_specs=[vm] * 5, out_specs=[vm] * 5,
        out_shape=[jax.ShapeDtypeStruct((N_CHIPS, D_MODEL, D_MODEL), _MXU), jax.ShapeDtypeStruct((N_CHIPS, N_META, d4), F32),
                   jax.ShapeDtypeStruct((D_MODEL, D_MODEL), _MXU), jax.ShapeDtypeStruct((d4, D_MODEL), _MXU),
                   jax.ShapeDtypeStruct((d4, D_MODEL), _MXU)],
        scratch_shapes=[pltpu.SemaphoreType.DMA((6,))] * 4,
        compiler_params=pltpu.CompilerParams(vmem_limit_bytes=VMEM_LIMIT),
    )(hw_in, sw_in, hw_out, sw_out, meta)


def _scatter_small(small):
    def body(sm, lsm, send_sems, recv_sems, local_sem):
        x, y, c = _place()
        mine = 4 * x + 2 * y + c
        local = pltpu.make_async_copy(sm, lsm.at[mine], local_sem)
        local.start()

        def copy(rel, src_dev, to):
            return pltpu.make_async_remote_copy(
                src_ref=sm, dst_ref=lsm.at[src_dev], send_sem=send_sems.at[rel - 1], recv_sem=recv_sems.at[rel - 1],
                device_id=to, device_id_type=MESH)

        flip = lambda bit, v: 1 - v if bit else v
        rels = [(rel, flip(rel & 4, x), flip(rel & 2, y), flip(rel & 1, c)) for rel in range(1, N_DEV)]
        sends = [copy(rel, mine, (px, py, pc)) for rel, px, py, pc in rels]
        for cp in sends:
            cp.start()
        for rel, px, py, pc in rels:
            copy(rel, 4 * px + 2 * py + pc, (px, py, pc)).wait_recv()
        for cp in sends:
            cp.wait_send()
        local.wait()

    hbm = pl.BlockSpec(memory_space=pl.ANY)
    return pl.pallas_call(
        body, name="scatter_small", in_specs=[hbm], out_specs=hbm,
        out_shape=jax.ShapeDtypeStruct((N_DEV, SMALL_ROWS, D_MODEL), F32),
        scratch_shapes=[pltpu.SemaphoreType.DMA((N_DEV - 1,)), pltpu.SemaphoreType.DMA((N_DEV - 1,)),
                        pltpu.SemaphoreType.DMA(())],
    )(small)


def _sum_slots(landed, own, me, name):
    n, rows, _ = landed.shape
    tm = rows if rows < 2 * TILE else 2 * TILE

    def body(me_ref, l_ref, o0, o1, o2, o3, out_ref):
        acc = None
        for k, o in enumerate((o0, o1, o2, o3)):
            term = jnp.where(me_ref[0] == k, o[...], l_ref[k].astype(F32))
            acc = term if acc is None else acc + term
        out_ref[...] = acc

    blk = pl.BlockSpec((tm, D_MODEL), lambda i: (i, 0))
    return pl.pallas_call(
        body, name=name, grid=(rows // tm,),
        in_specs=[pl.BlockSpec(memory_space=pltpu.SMEM), pl.BlockSpec((n, tm, D_MODEL), lambda i: (0, i, 0))] + [blk] * 4,
        out_specs=blk, out_shape=jax.ShapeDtypeStruct((rows, D_MODEL), F32),
        compiler_params=_params(("arbitrary",)),
    )(me, landed, *own)


def _swap_with_sibling(parts):
    def body(a0, a1, a2, a3, b0, b1, b2, b3, send_sems, recv_sems):
        x, y, c = _place()
        copies = [pltpu.make_async_remote_copy(src_ref=s, dst_ref=d, send_sem=send_sems.at[a], recv_sem=recv_sems.at[a],
                                               device_id=(x, y, 1 - c), device_id_type=MESH)
                  for a, (s, d) in enumerate(zip((a0, a1, a2, a3), (b0, b1, b2, b3)))]
        for cp in copies:
            cp.start()
        for cp in copies:
            cp.wait()

    hbm = pl.BlockSpec(memory_space=pl.ANY)
    return pl.pallas_call(
        body, name="swap_with_sibling", in_specs=[hbm] * 4, out_specs=[hbm] * 4,
        out_shape=[jax.ShapeDtypeStruct(p.shape, F32) for p in parts],
        scratch_shapes=[pltpu.SemaphoreType.DMA((4,)), pltpu.SemaphoreType.DMA((4,))],
    )(*parts)


def _adamw_math(w, g, m, v):
    m = ADAM_B1 * m + (1.0 - ADAM_B1) * g
    v = ADAM_B2 * v + (1.0 - ADAM_B2) * (g * g)
    m_hat = m / (1.0 - ADAM_B1 ** ADAM_STEP)
    v_hat = v / (1.0 - ADAM_B2 ** ADAM_STEP)
    delta = -ADAM_LR * (m_hat / (jnp.sqrt(v_hat) + ADAM_EPS) + ADAM_WD * w)
    return delta, m, v


def _adamw(w, g_parts, m, v, name):
    rows, cols = w.shape
    tm = rows if rows < 2 * TILE else 2 * TILE
    n = len(g_parts)

    def body(*refs):
        w_ref, m_ref, v_ref = refs[n:n + 3]
        g_ref, d_ref, nm_ref, nv_ref = refs[n + 3:]
        g = refs[0][...]
        for p in refs[1:n]:
            g = g + p[...]
        g_ref[...] = g
        d_ref[...], nm_ref[...], nv_ref[...] = _adamw_math(w_ref[...], g, m_ref[...], v_ref[...])

    blk = pl.BlockSpec((tm, cols), lambda i: (i, 0))
    return pl.pallas_call(
        body, name=name, grid=(rows // tm,), in_specs=[blk] * (n + 3), out_specs=[blk] * 4,
        out_shape=[jax.ShapeDtypeStruct((rows, cols), F32)] * 4,
        compiler_params=_params(("arbitrary",)),
    )(*g_parts, w, m, v)


def _lam_of(hgrn_lb):
    def body(lb_ref, o_ref):
        lb = lb_ref[...]
        e = jnp.exp(lb - jnp.max(lb, axis=0, keepdims=True))
        o_ref[...] = e[0:1, :] / jnp.sum(e, axis=0, keepdims=True)

    return pl.pallas_call(body, name="lam_of", out_shape=jax.ShapeDtypeStruct((1, D_MODEL), F32))(hgrn_lb)


def _small_grads(land_small, lam):
    def body(l_ref, lam_ref, o_ref):
        acc = l_ref[0]
        for k in range(1, N_DEV):
            acc = acc + l_ref[k]
        p = lam_ref[...]
        slope = p * (1.0 - p)
        row = _iota2((SMALL_ROWS, D_MODEL), 0)
        o_ref[...] = acc * jnp.where(row == 4, slope, jnp.where(row == 5, -slope, 1.0))

    return pl.pallas_call(body, name="small_grads",
                          out_shape=jax.ShapeDtypeStruct((SMALL_ROWS, D_MODEL), F32))(land_small, lam)


def kernel(x, meta_tokens, pre_norm, post_norm, hgrn_w_in, hgrn_lb, hgrn_out_norm, hgrn_w_out, sb_w_in, sb_w_out, loss_target, m_meta_tokens, m_pre_norm, m_post_norm, m_hgrn_w_in, m_hgrn_lb, m_hgrn_out_norm, m_hgrn_w_out, m_sb_w_in, m_sb_w_out, v_meta_tokens, v_pre_norm, v_post_norm, v_hgrn_w_in, v_hgrn_lb, v_hgrn_out_norm, v_hgrn_w_out, v_sb_w_in, v_sb_w_out):
    d4 = D_MODEL // N_CHIPS
    chip = 2 * lax.axis_index("x") + lax.axis_index("y")
    hw_in, meta4, sw_in16, sw_out16, hw_out16 = _prep_weights(hgrn_w_in, sb_w_in, hgrn_w_out, sb_w_out, meta_tokens)
    meta = meta4.transpose(1, 0, 2).reshape(N_META, D_MODEL)
    lam = _lam_of(hgrn_lb)
    grad_x, small, large = _local_step(
        x, loss_target, meta, pre_norm, post_norm, lam, hgrn_out_norm,
        hw_in, (sw_in16, sw_out16, hw_out16))

    me = jnp.reshape(chip, (1,)).astype(jnp.int32)
    parts = [_sum_slots(*large[n], me, "sum_" + n) for n in ("hw_in", "sw_in", "hw_out", "sw_out")]
    sib = _swap_with_sibling(parts)
    small = _small_grads(_scatter_small(small), lam)
    loss = small[SMALL_ROWS - 1, 0]

    res = {}
    res["hgrn_w_in"] = _adamw(hgrn_w_in[0], [parts[0], sib[0]], m_hgrn_w_in[0], v_hgrn_w_in[0], "adamw_hw_in")
    res["sb_w_in"] = _adamw(sb_w_in[0], [parts[1], sib[1]], m_sb_w_in[0], v_sb_w_in[0], "adamw_sw_in")
    res["hgrn_w_out"] = _adamw(hgrn_w_out[0], [parts[2], sib[2]], m_hgrn_w_out[0], v_hgrn_w_out[0], "adamw_hw_out")
    res["sb_w_out"] = _adamw(sb_w_out[0], [parts[3], sib[3]], m_sb_w_out[0], v_sb_w_out[0], "adamw_sw_out")
    res["pre_norm"] = _adamw(pre_norm, [small[0:2]], m_pre_norm, v_pre_norm, "adamw_pre")
    res["post_norm"] = _adamw(post_norm, [small[2:4]], m_post_norm, v_post_norm, "adamw_post")
    res["hgrn_lb"] = _adamw(hgrn_lb, [small[4:6]], m_hgrn_lb, v_hgrn_lb, "adamw_lb")
    res["hgrn_out_norm"] = _adamw(hgrn_out_norm, [small[6:7, :D_HEAD]], m_hgrn_out_norm, v_hgrn_out_norm, "adamw_head")
    g_meta = lax.dynamic_slice_in_dim(small[7:7 + N_META], chip * d4, d4, axis=1)
    res["meta_tokens"] = _adamw(meta_tokens, [g_meta], m_meta_tokens, v_meta_tokens, "adamw_meta")
    for n in ("hgrn_w_in", "hgrn_w_out", "sb_w_in", "sb_w_out"):
        res[n] = tuple(a[None] for a in res[n])
    order = ("meta_tokens", "pre_norm", "post_norm", "hgrn_w_in", "hgrn_lb", "hgrn_out_norm", "hgrn_w_out",
             "sb_w_in", "sb_w_out")
    return (loss, grad_x, *[res[n][0] for n in order], *[res[n][1] for n in order],
            *[res[n][2] for n in order], *[res[n][3] for n in order])
```

```python
import functools

import jax
import numpy as np
import jax.numpy as jnp
from jax import lax
from jax.experimental import pallas as pl
from jax.experimental.pallas import tpu as pltpu

F32 = jnp.float32
_MXU = jnp.bfloat16

D_MODEL = 1024
N_HEADS = 8
D_HEAD = 128
BLOCK = 128
N_META = 16
TILE = 256
N_PAD = TILE - N_META
UNDERFLOW = -105.0
EPS = 1e-6
SB_SCALE = D_HEAD ** -0.5
SOFTPLUS_LINEAR = 20.0
MASKED = -1e30
ADAM_LR, ADAM_B1, ADAM_B2, ADAM_EPS, ADAM_WD, ADAM_STEP = 0.001, 0.9, 0.999, 1e-08, 0.01, 10
N_CHIPS = 4
N_DEV = 8
SMALL_ROWS = 24
VMEM_LIMIT = 56 * 1024 * 1024
MESH = pl.DeviceIdType.MESH

_NT = (((1,), (1,)), ((), ()))
_TN = (((0,), (0,)), ((), ()))


def _mm(a, b):
    return jnp.dot(a.astype(_MXU), b.astype(_MXU), preferred_element_type=F32)


def _mm_nt(a, b):
    return lax.dot_general(a.astype(_MXU), b.astype(_MXU), _NT, preferred_element_type=F32)


def _mm_tn(a, b):
    return lax.dot_general(a.astype(_MXU), b.astype(_MXU), _TN, preferred_element_type=F32)


def _split2(x):
    hi = x.astype(_MXU)
    return hi, (x - hi.astype(F32)).astype(_MXU)


def _mm_s(a16, state):
    hi, lo = _split2(state)
    return jnp.dot(a16, hi, preferred_element_type=F32) + jnp.dot(a16, lo, preferred_element_type=F32)


def _mm_nt_s(a16, state):
    hi, lo = _split2(state)
    return (lax.dot_general(a16, hi, _NT, preferred_element_type=F32)
            + lax.dot_general(a16, lo, _NT, preferred_element_type=F32))


def _mm01_right(x, m01):
    return jnp.dot(x.astype(_MXU), m01, preferred_element_type=F32)


def _mm01_left(m01, x):
    hi, lo = _split2(x)
    return jnp.dot(m01, hi, preferred_element_type=F32) + jnp.dot(m01, lo, preferred_element_type=F32)


def _iota2(shape, dim):
    return lax.broadcasted_iota(jnp.int32, shape, dim)


def _row_tile(total, pref):
    t = pref
    while total % t:
        t -= BLOCK
    return t


def _params(sem, limit=VMEM_LIMIT):
    return pltpu.CompilerParams(dimension_semantics=sem, vmem_limit_bytes=limit)


def _sigmoid(x):
    return 1.0 / (1.0 + jnp.exp(-x))


def _grid_ends(ndim):
    first, last = True, True
    for d in range(ndim):
        first = first & (pl.program_id(d) == 0)
        last = last & (pl.program_id(d) == pl.num_programs(d) - 1)
    return first, last


def _place():
    return lax.axis_index("x"), lax.axis_index("y"), lax.axis_index("c")


def _exchange_scratch(n):
    return [pltpu.SemaphoreType.DMA((3 * n,)), pltpu.SemaphoreType.DMA((3 * n,)), pltpu.SemaphoreType.DMA((n,))]


def _chip_exchange(srcs, dsts, sems, slotted):
    send_sems, recv_sems, local_sems = sems
    x, y, c = _place()
    me = 2 * x + y
    peers = [(1 - x, y), (x, 1 - y), (1 - x, 1 - y)]
    n = len(dsts)

    def remote(r, a, sending):
        px, py = peers[r]
        p = 2 * px + py
        return pltpu.make_async_remote_copy(
            src_ref=srcs[a].at[p] if slotted else srcs[a], dst_ref=dsts[a].at[me if sending else p],
            send_sem=send_sems.at[r * n + a], recv_sem=recv_sems.at[r * n + a],
            device_id=(px, py, c), device_id_type=MESH)

    own = [] if slotted else [pltpu.make_async_copy(srcs[a], dsts[a].at[me], local_sems.at[a]) for a in range(n)]

    def start():
        for cp in own:
            cp.start()
        for r in range(3):
            for a in range(n):
                remote(r, a, True).start()

    def finish():
        for r in range(3):
            for a in range(n):
                remote(r, a, False).wait_recv()
        for r in range(3):
            for a in range(n):
                remote(r, a, True).wait_send()
        for cp in own:
            cp.wait()

    return start, finish


def _norm_proj(h, gain, w4, name, narrow):
    t = h.shape[0]
    tm = _row_tile(t, 2 * TILE)

    def body(h_ref, g_ref, w_ref, y_ref, s0, s1, s2, s3):
        x = h_ref[...]
        r = lax.rsqrt(jnp.mean(x * x, axis=-1, keepdims=True) + EPS)
        y = (x * r * g_ref[...]).astype(_MXU)
        y_ref[...] = y
        for j, s in enumerate((s0, s1, s2, s3)):
            s[...] = jnp.dot(y, w_ref[j], preferred_element_type=F32).astype(s.dtype)

    row = pl.BlockSpec((tm, D_MODEL), lambda i: (i, 0))
    return pl.pallas_call(
        body, name=name, grid=(t // tm,),
        in_specs=[row, pl.BlockSpec((1, D_MODEL), lambda i: (0, 0)),
                  pl.BlockSpec((4, D_MODEL, D_MODEL), lambda i: (0, 0, 0))],
        out_specs=[row] * 5,
        out_shape=[jax.ShapeDtypeStruct((t, D_MODEL), _MXU)]
        + [jax.ShapeDtypeStruct((t, D_MODEL), _MXU if n else F32) for n in narrow],
        compiler_params=_params(("arbitrary",)),
    )(h, gain, w4)


LEVELS = (64, 32, 16, 8, 4, 2, 1)
HEAD_GROUP = 8
CHUNKS = 2
SB_FWD_GROUP = 8
SB_BWD_GROUP = 4


def _hgrn_tables():
    r = np.arange(BLOCK)
    mats = [r[None, :] <= r[:, None]]
    x = r[:, None] ^ r[None, :]
    lv = np.full((BLOCK, BLOCK), len(LEVELS), np.int32)
    for i, m in enumerate(LEVELS):
        lv[(x >= m) & (x < 2 * m)] = i
    return jnp.asarray(np.concatenate(mats, 0).astype(np.float32), dtype=_MXU), jnp.asarray(lv)


def _hgrn_exponents(g, sums):
    b = _mm01_left(sums, g)
    row = _iota2((BLOCK, D_HEAD), 0)
    out = []
    for m in LEVELS:
        is_q = (row & m) != 0
        if m >= 4:
            grp = b.reshape(BLOCK // (2 * m), 2 * m, D_HEAD)
            ref = jnp.broadcast_to(grp[:, m - 1:m, :], grp.shape).reshape(BLOCK, D_HEAD)
            d = b - ref
            out.append(jnp.where(is_q, d, -d))
        elif m == 2:
            below, above = pltpu.roll(g, 1, axis=0), pltpu.roll(g, BLOCK - 1, axis=0)
            low = row & 3
            out.append(jnp.where(low == 3, g + below, jnp.where(low == 2, g, jnp.where(low == 0, above, 0.0))))
        else:
            out.append(jnp.where(is_q, g, 0.0))
    return b, out


def _hgrn_gates(fz, lam, chunk):
    pos = chunk * BLOCK + _iota2((BLOCK, D_HEAD), 0)
    live = pos >= N_PAD
    sg = _sigmoid(fz)
    f = lam + (1.0 - lam) * sg
    g = jnp.where(live, jnp.log(f), 0.0)
    k = jnp.where(live, (1.0 - lam) * (1.0 - sg), 0.0)
    return sg, f, g, k, live


def _level_operand(q, k, exponent, m):
    decay = jnp.exp(exponent)
    is_q = (_iota2((BLOCK, D_HEAD), 0) & m) != 0
    return is_q, decay, (jnp.where(is_q, q, k) * decay).astype(_MXU)


def _hgrn_fwd(qs, fs, vs, lam, bsz, nb, shards):
    t = qs.shape[0]
    sums, levels = _hgrn_tables()
    width = HEAD_GROUP * D_HEAD

    n_sh = len(shards)

    def body(q_ref, f_ref, v_ref, lam_ref, sums_ref, lv_ref, *rest):
        own, (o_ref, sst_ref, sym_ref), rest = rest[:n_sh], rest[n_sh:n_sh + 3], rest[n_sh + 3:]
        gathered, st_scr, sems = rest[:n_sh], rest[n_sh], rest[n_sh + 1:]
        n = pl.program_id(2)
        first, last = _grid_ends(3)
        start, finish = _chip_exchange(own, gathered, sems, slotted=False)
        pl.when(first)(start)

        @pl.when(n == 0)
        def _():
            st_scr[...] = jnp.zeros_like(st_scr)

        lv = lv_ref[...]
        r, c = _iota2((BLOCK, BLOCK), 0), _iota2((BLOCK, BLOCK), 1)
        for cc, hh in [(cc, hh) for cc in range(CHUNKS) for hh in range(HEAD_GROUP)]:
            ls, rows = slice(hh * D_HEAD, (hh + 1) * D_HEAD), slice(cc * BLOCK, (cc + 1) * BLOCK)
            st = st_scr[hh]
            sst_ref[0, hh, cc] = st
            q, v = q_ref[rows, ls], v_ref[rows, ls]
            _, _, g, k, _ = _hgrn_gates(f_ref[rows, ls], lam_ref[:, ls], CHUNKS * n + cc)
            b, exps = _hgrn_exponents(g, sums_ref[...])
            sym = jnp.zeros((BLOCK, BLOCK), F32)
            for li, m in enumerate(LEVELS):
                _, _, x16 = _level_operand(q, k, exps[li], m)
                sym = jnp.where(lv == li, lax.dot_general(x16, x16, _NT, preferred_element_type=F32), sym)
            sym = jnp.where(c == r, jnp.sum(q * k, axis=1, keepdims=True), sym).astype(_MXU)
            sym_ref[0, hh, cc] = sym
            o_ref[rows, ls] = _mm_nt(q * jnp.exp(b), st) + _mm(jnp.where(c <= r, sym, 0), v)
            b_end = b[BLOCK - 1:BLOCK, :]
            st_scr[hh] = st * jnp.exp(b_end) + _mm_tn(v, k * jnp.exp(b_end - b))
        pl.when(last)(finish)

    steps = nb // CHUNKS
    blk = pl.BlockSpec((CHUNKS * BLOCK, width), lambda b, h, n: (b * steps + n, h))
    hbm = pl.BlockSpec(memory_space=pl.ANY)
    return pl.pallas_call(
        body, name="hgrn_fwd", grid=(bsz, N_HEADS // HEAD_GROUP, steps),
        in_specs=[blk, blk, blk, pl.BlockSpec((1, width), lambda b, h, n: (0, h)),
                  pl.BlockSpec(sums.shape, lambda b, h, n: (0, 0)), pl.BlockSpec(levels.shape, lambda b, h, n: (0, 0))]
        + [hbm] * n_sh,
        out_specs=[blk] + [pl.BlockSpec((1, HEAD_GROUP, CHUNKS, D_HEAD, D_HEAD), lambda b, h, n: (b, h, n, 0, 0))] * 2
        + [hbm] * n_sh,
        out_shape=[jax.ShapeDtypeStruct((t, D_MODEL), F32),
                   jax.ShapeDtypeStruct((bsz, N_HEADS, nb, D_HEAD, D_HEAD), F32),
                   jax.ShapeDtypeStruct((bsz, N_HEADS, nb, D_HEAD, D_HEAD), _MXU)]
        + [jax.ShapeDtypeStruct((N_CHIPS,) + a.shape, a.dtype) for a in shards],
        scratch_shapes=[pltpu.VMEM((HEAD_GROUP, D_HEAD, D_HEAD), F32)] + _exchange_scratch(n_sh),
        compiler_params=_params(("arbitrary", "arbitrary", "arbitrary")),
    )(qs, fs, vs, lam, sums, levels, *shards)


def _hgrn_bwd(qs, fs, vs, lam, sst, sym, do, bsz, nb, outgoing):
    t = qs.shape[0]
    sums, levels = _hgrn_tables()
    width = HEAD_GROUP * D_HEAD

    n_out = len(outgoing)
    steps = nb // CHUNKS

    def body(q_ref, f_ref, v_ref, lam_ref, sst_ref, sym_ref, do_ref, sums_ref, lv_ref, *rest):
        send, (dq_ref, df_ref, dv_ref, dlam_ref), rest = rest[:n_out], rest[n_out:n_out + 4], rest[n_out + 4:]
        landed, dst_scr, gsum_scr, sems = rest[:n_out], rest[n_out], rest[n_out + 1], rest[n_out + 2:]
        n = pl.program_id(2)
        first, last = _grid_ends(3)
        start, finish = _chip_exchange(send, landed, sems, slotted=True)
        pl.when(first)(start)

        @pl.when(n == 0)
        def _():
            dst_scr[...] = jnp.zeros_like(dst_scr)
            gsum_scr[...] = jnp.zeros_like(gsum_scr)
            dlam_ref[...] = jnp.zeros_like(dlam_ref)

        lv = lv_ref[...]
        r, c = _iota2((BLOCK, BLOCK), 0), _iota2((BLOCK, BLOCK), 1)
        for cc, hh in [(cc, hh) for cc in reversed(range(CHUNKS)) for hh in range(HEAD_GROUP)]:
            ls, rows = slice(hh * D_HEAD, (hh + 1) * D_HEAD), slice(cc * BLOCK, (cc + 1) * BLOCK)
            lam = lam_ref[:, ls]
            q, v, do = q_ref[rows, ls], v_ref[rows, ls], do_ref[rows, ls]
            sg, f, g, k, live = _hgrn_gates(f_ref[rows, ls], lam, CHUNKS * (steps - 1 - n) + cc)
            b, exps = _hgrn_exponents(g, sums_ref[...])
            do16, v16 = do.astype(_MXU), v.astype(_MXU)
            da = lax.dot_general(do16, v16, _NT, preferred_element_type=F32)
            da_sym = jnp.where(c < r, da, da.T)
            dq = jnp.zeros((BLOCK, D_HEAD), F32)
            dqk = jnp.zeros((BLOCK, D_HEAD), F32)
            db_q = jnp.zeros((BLOCK, D_HEAD), F32)
            db_qk = jnp.zeros((BLOCK, D_HEAD), F32)
            for li, m in enumerate(LEVELS):
                is_q, decay, x16 = _level_operand(q, k, exps[li], m)
                y = jnp.dot(jnp.where(lv == li, da_sym, 0.0).astype(_MXU), x16, preferred_element_type=F32)
                dx = y * decay
                dq = dq + jnp.where(is_q, dx, 0.0)
                dqk = dqk + dx
                p = x16.astype(F32) * y
                db_q = db_q + jnp.where(is_q, p, 0.0)
                db_qk = db_qk + p
            dk = dqk - dq
            db = 2.0 * db_q - db_qk
            a_t = jnp.where(c >= r, sym_ref[0, hh, cc], 0)
            st, dst = sst_ref[0, hh, cc], dst_scr[hh]
            eb = jnp.exp(b)
            b_end = b[BLOCK - 1:BLOCK, :]
            dec = jnp.exp(b_end - b)
            qh16, kt16 = (q * eb).astype(_MXU), (k * dec).astype(_MXU)
            dq_st = _mm_s(do16, st)
            dk_st = _mm_s(v16, dst)
            d_diag = jnp.sum(do * v, axis=1, keepdims=True)
            dq_ref[rows, ls] = (dq + d_diag * k + eb * dq_st).astype(dq_ref.dtype)
            dk = dk + d_diag * q + dec * dk_st
            dv_ref[rows, ls] = (jnp.dot(a_t, do16, preferred_element_type=F32) + _mm_nt_s(kt16, dst)).astype(dv_ref.dtype)
            dst_scr[hh] = dst * jnp.exp(b_end) + lax.dot_general(do16, qh16, _TN, preferred_element_type=F32)
            db = db + (qh16.astype(F32) * dq_st - kt16.astype(F32) * dk_st)
            dg = _mm01_left((c >= r).astype(_MXU), db) + gsum_scr[:, ls]
            gsum_scr[:, ls] = gsum_scr[:, ls] + jnp.sum(db, axis=0, keepdims=True)
            slope = (1.0 - lam) * sg * (1.0 - sg)
            df_ref[rows, ls] = jnp.where(live, dg * slope / f - dk * slope, 0.0).astype(df_ref.dtype)
            dl = jnp.where(live, (dg / f - dk) * (1.0 - sg), 0.0)
            dlam_ref[0, :, ls] = dlam_ref[0, :, ls] + jnp.sum(dl, axis=0, keepdims=True)
        pl.when(last)(finish)

    blk = pl.BlockSpec((CHUNKS * BLOCK, width), lambda b, h, n: (b * steps + steps - 1 - n, h))
    hbm = pl.BlockSpec(memory_space=pl.ANY)
    return pl.pallas_call(
        body, name="hgrn_bwd", grid=(bsz, N_HEADS // HEAD_GROUP, steps),
        in_specs=[blk, blk, blk, pl.BlockSpec((1, width), lambda b, h, n: (0, h)),
                  pl.BlockSpec((1, HEAD_GROUP, CHUNKS, D_HEAD, D_HEAD), lambda b, h, n: (b, h, steps - 1 - n, 0, 0)),
                  pl.BlockSpec((1, HEAD_GROUP, CHUNKS, D_HEAD, D_HEAD), lambda b, h, n: (b, h, steps - 1 - n, 0, 0)),
                  blk, pl.BlockSpec(sums.shape, lambda b, h, n: (0, 0)), pl.BlockSpec(levels.shape, lambda b, h, n: (0, 0))]
        + [hbm] * n_out,
        out_specs=[blk, blk, blk, pl.BlockSpec((1, 1, width), lambda b, h, n: (b, 0, h))] + [hbm] * n_out,
        out_shape=[jax.ShapeDtypeStruct((t, D_MODEL), _MXU)] * 3 + [jax.ShapeDtypeStruct((bsz, 1, D_MODEL), F32)]
        + [jax.ShapeDtypeStruct(a.shape, a.dtype) for a in outgoing],
        scratch_shapes=[pltpu.VMEM((HEAD_GROUP, D_HEAD, D_HEAD), F32), pltpu.VMEM((1, width), F32)]
        + _exchange_scratch(n_out),
        compiler_params=_params(("arbitrary", "arbitrary", "arbitrary")),
    )(qs, fs, vs, lam, sst, sym, do, sums, levels, *outgoing)


def _sb_valid(ahead, col, i, j):
    return (ahead < (i - j) * TILE) & (col >= N_PAD - j * TILE)


def _sb_logits(q16, k_blk, valid):
    z = jnp.where(valid, lax.dot_general(q16, k_blk.astype(_MXU), _NT, preferred_element_type=F32) * SB_SCALE, MASKED)
    softplus = jnp.where(z > SOFTPLUS_LINEAR, z, jnp.log(1.0 + jnp.exp(jnp.minimum(z, SOFTPLUS_LINEAR))))
    return -softplus, z - softplus


def _sb_fwd(qs, ks, vs, bsz, nq):
    t = qs.shape[0]
    lp = nq * TILE
    width = SB_FWD_GROUP * D_HEAD
    groups = N_HEADS // SB_FWD_GROUP
    lanes = [slice(hh * D_HEAD, (hh + 1) * D_HEAD) for hh in range(SB_FWD_GROUP)]

    def body(q_ref, k_ref, v_ref, o_ref, c_ref, n_ref):
        b, h, i = pl.program_id(0), pl.program_id(1), pl.program_id(2)
        q16 = [q_ref[:, ls].astype(_MXU) for ls in lanes]
        r, c = _iota2((TILE, TILE), 0), _iota2((TILE, TILE), 1)
        after = (r > c).astype(_MXU)

        def more(carry):
            jj, _, _, top = carry
            return (jj <= i) & (top > UNDERFLOW)

        def step(carry):
            jj, accs, sums, _ = carry
            j = i - jj
            ks_ = pl.ds(pl.multiple_of(j * TILE, TILE), TILE)
            valid = _sb_valid(c - r, c, i, j)
            new_accs, new_sums = [], []
            for hh, ls in enumerate(lanes):
                keep, log_beta = _sb_logits(q16[hh], k_ref[ks_, ls], valid)
                after_s = _mm01_right(keep, after)
                a = jnp.exp(log_beta + (sums[hh] + after_s))
                new_accs.append(accs[hh] + _mm(a, v_ref[ks_, ls]))
                new_sums.append(sums[hh] + (after_s[:, 0:1] + keep[:, 0:1]))
            top = functools.reduce(jnp.maximum, [jnp.max(x) for x in new_sums])
            return jj + 1, tuple(new_accs), tuple(new_sums), top

        init = (jnp.int32(0), tuple(jnp.zeros((TILE, D_HEAD), F32) for _ in lanes),
                tuple(jnp.zeros((TILE, 1), F32) for _ in lanes), jnp.float32(0.0))
        visited, accs, sums, _ = lax.while_loop(more, step, init)
        for hh, ls in enumerate(lanes):
            o_ref[:, ls] = accs[hh]
            c_ref[:, ls] = jnp.broadcast_to(sums[hh], (TILE, D_HEAD))
        n_ref[(b * groups + h) * nq + i] = visited.astype(F32)

    blk = pl.BlockSpec((TILE, width), lambda b, h, i: (b * nq + i, h))
    seq = pl.BlockSpec((lp, width), lambda b, h, i: (b, h))
    return pl.pallas_call(
        body, name="sb_fwd", grid=(bsz, groups, nq),
        in_specs=[blk, seq, seq], out_specs=[blk, blk, pl.BlockSpec(memory_space=pltpu.SMEM)],
        out_shape=[jax.ShapeDtypeStruct((t, D_MODEL), F32)] * 2 + [jax.ShapeDtypeStruct((bsz * groups * nq,), F32)],
        compiler_params=_params(("arbitrary", "arbitrary", "arbitrary")),
    )(qs, ks, vs)


def _sb_bwd(qs, ks, vs, ctot, visited, do, bsz, nq):
    t = qs.shape[0]
    lp = nq * TILE
    width = SB_BWD_GROUP * D_HEAD
    groups = N_HEADS // SB_BWD_GROUP
    lanes = [slice(hh * D_HEAD, (hh + 1) * D_HEAD) for hh in range(SB_BWD_GROUP)]

    def body(n_ref, q_ref, k_ref, v_ref, c_ref, do_ref, dq_ref, dk_ref, dv_ref, dk_acc, dv_acc):
        b, h, i = pl.program_id(0), pl.program_id(1), pl.program_id(2)

        @pl.when(i == 0)
        def _():
            dk_acc[...] = jnp.zeros_like(dk_acc)
            dv_acc[...] = jnp.zeros_like(dv_acc)

        q16 = [q_ref[:, ls].astype(_MXU) for ls in lanes]
        do16 = [do_ref[:, ls].astype(_MXU) for ls in lanes]
        q16_t = [q_ref[:, ls].astype(F32).T.astype(_MXU) for ls in lanes]
        do16_t = [do_ref[:, ls].astype(F32).T.astype(_MXU) for ls in lanes]
        totals = [c_ref[:, hh * D_HEAD:hh * D_HEAD + 1] for hh in range(SB_BWD_GROUP)]
        r, c = _iota2((TILE, TILE), 0), _iota2((TILE, TILE), 1)
        upto = (r <= c).astype(_MXU)
        before = (r < c).astype(_MXU)
        seen = n_ref[(b * (N_HEADS // SB_FWD_GROUP) + h * SB_BWD_GROUP // SB_FWD_GROUP) * nq + i]
        first = jnp.maximum(i + 1 - seen.astype(jnp.int32), 0)

        def step(j, carry):
            ks_ = pl.ds(pl.multiple_of(j * TILE, TILE), TILE)
            valid = _sb_valid(c - r, c, i, j)
            out = []
            for hh, ls in enumerate(lanes):
                dq, keep_pre, g_pre = carry[hh]
                k_blk, v_blk = k_ref[ks_, ls], v_ref[ks_, ls]
                keep, log_beta = _sb_logits(q16[hh], k_blk, valid)
                keep_upto = _mm01_right(keep, upto)
                a = jnp.exp(log_beta + (totals[hh] - keep_pre - keep_upto))
                da = lax.dot_general(do16[hh], v_blk.astype(_MXU), _NT, preferred_element_type=F32)
                g = a * da
                g_inside = _mm01_right(g, before)
                g_before = g_pre + g_inside
                beta = jnp.exp(log_beta)
                dz16 = (g * (1.0 - beta) - beta * g_before).astype(_MXU)
                dq = dq + jnp.dot(dz16, k_blk.astype(_MXU), preferred_element_type=F32)
                dk_acc[ls, ks_] += SB_SCALE * jnp.dot(q16_t[hh], dz16, preferred_element_type=F32)
                dv_acc[ls, ks_] += jnp.dot(do16_t[hh], a.astype(_MXU), preferred_element_type=F32)
                out.append((dq, keep_pre + keep_upto[:, TILE - 1:TILE],
                            g_pre + (g_inside[:, TILE - 1:TILE] + g[:, TILE - 1:TILE])))
            return tuple(out)

        zero_col = jnp.zeros((TILE, 1), F32)
        init = tuple((jnp.zeros((TILE, D_HEAD), F32), zero_col, zero_col) for _ in lanes)
        res = lax.fori_loop(first, i + 1, step, init)
        for hh, ls in enumerate(lanes):
            dq_ref[:, ls] = (SB_SCALE * res[hh][0]).astype(dq_ref.dtype)

        @pl.when(i == nq - 1)
        def _():
            dk_ref[...] = dk_acc[...].T.astype(dk_ref.dtype)
            dv_ref[...] = dv_acc[...].T.astype(dv_ref.dtype)

    blk = pl.BlockSpec((TILE, width), lambda b, h, i: (b * nq + i, h))
    seq = pl.BlockSpec((lp, width), lambda b, h, i: (b, h))
    seq_out = pl.BlockSpec((lp, width), lambda b, h, i: (b, h), pipeline_mode=pl.Buffered(1))
    return pl.pallas_call(
        body, name="sb_bwd", grid=(bsz, groups, nq),
        in_specs=[pl.BlockSpec(memory_space=pltpu.SMEM), blk, seq, seq, blk, blk], out_specs=[blk, seq_out, seq_out],
        out_shape=[jax.ShapeDtypeStruct((t, D_MODEL), _MXU)] * 3,
        scratch_shapes=[pltpu.VMEM((width, lp), F32)] * 2,
        compiler_params=_params(("arbitrary", "arbitrary", "arbitrary")),
    )(visited, qs, ks, vs, ctot, do)


def _head_norm(o, head_gain):
    outs, rs = [], []
    for h in range(N_HEADS):
        oh = o[:, h * D_HEAD:(h + 1) * D_HEAD]
        r = lax.rsqrt(jnp.mean(oh * oh, axis=-1, keepdims=True) + EPS)
        outs.append(oh * r)
        rs.append(r)
    return outs, rs


def _mix(o, gate, head_gain):
    if head_gain is None:
        on = o
    else:
        outs, _ = _head_norm(o, head_gain)
        on = jnp.concatenate([x * head_gain for x in outs], axis=1)
    return on, on * (gate * _sigmoid(gate))


def _out_fwd(o, gate, h_in, w_out, post_gain, head_gain, name):
    t = o.shape[0]
    tm = _row_tile(t, 2 * TILE)

    def body(o_ref, g_ref, h_ref, w_ref, pg_ref, hg_ref, ho_ref, u_ref):
        _, mix = _mix(o_ref[...], g_ref[...], hg_ref[...])
        u = jnp.dot(mix.astype(_MXU), w_ref[...], preferred_element_type=F32)
        u_ref[...] = u
        r = lax.rsqrt(jnp.mean(u * u, axis=-1, keepdims=True) + EPS)
        ho_ref[...] = h_ref[...] + u * r * pg_ref[...]

    row = pl.BlockSpec((tm, D_MODEL), lambda i: (i, 0))
    vec = pl.BlockSpec((1, D_MODEL), lambda i: (0, 0))
    return pl.pallas_call(
        body, name=name, grid=(t // tm,),
        in_specs=[row, row, row, pl.BlockSpec((D_MODEL, D_MODEL), lambda i: (0, 0)), vec,
                  pl.BlockSpec((1, D_HEAD), lambda i: (0, 0))],
        out_specs=[row, row], out_shape=[jax.ShapeDtypeStruct((t, D_MODEL), F32)] * 2,
        compiler_params=_params(("arbitrary",)),
    )(o, gate, h_in, w_out, post_gain, head_gain)


def _out_fwd_loss(o, gate, h_in, w_out, post_gain, target, nq, name):
    t = o.shape[0]

    def body(o_ref, g_ref, h_ref, w_ref, pg_ref, t_ref, dh_ref, u_ref, l_ref):
        i = pl.program_id(0)

        @pl.when(i == 0)
        def _():
            l_ref[...] = jnp.zeros_like(l_ref)

        _, mix = _mix(o_ref[...], g_ref[...], None)
        u = jnp.dot(mix.astype(_MXU), w_ref[...], preferred_element_type=F32)
        u_ref[...] = u

        @pl.when(i % nq == 0)
        def _():
            dh_ref[...] = jnp.zeros_like(dh_ref)

        @pl.when(i % nq != 0)
        def _():
            r = lax.rsqrt(jnp.mean(u * u, axis=-1, keepdims=True) + EPS)
            e = h_ref[...] + u * r * pg_ref[...] - t_ref[...]
            dh_ref[...] = e * (1.0 / D_MODEL)
            l_ref[...] += jnp.sum(e * e) * (0.5 / D_MODEL)

    row = pl.BlockSpec((TILE, D_MODEL), lambda i: (i, 0))
    vec = pl.BlockSpec((1, D_MODEL), lambda i: (0, 0))
    return pl.pallas_call(
        body, name=name, grid=(t // TILE,),
        in_specs=[row, row, row, pl.BlockSpec((D_MODEL, D_MODEL), lambda i: (0, 0)), vec,
                  pl.BlockSpec((TILE, D_MODEL), lambda i: ((i // nq) * (nq - 1) + jnp.maximum(i % nq - 1, 0), 0))],
        out_specs=[row, row, pl.BlockSpec((8, 128), lambda i: (0, 0))],
        out_shape=[jax.ShapeDtypeStruct((t, D_MODEL), F32)] * 2 + [jax.ShapeDtypeStruct((8, 128), F32)],
        compiler_params=_params(("arbitrary",)),
    )(o, gate, h_in, w_out, post_gain, target)


def _out_bwd(dh, u, o, gate, w_out, post_gain, head_gain, name, narrow_do):
    t = o.shape[0]
    tm = _row_tile(t, 2 * TILE)
    has_head = head_gain is not None

    def body(*refs):
        if has_head:
            dh_ref, u_ref, o_ref, g_ref, w_ref, pg_ref, hg_ref, do_ref, dg_ref, gw_ref, gw16_ref, gp_ref, gh_ref = refs
            hg = hg_ref[...]
        else:
            dh_ref, u_ref, o_ref, g_ref, w_ref, pg_ref, do_ref, dg_ref, gw_ref, gw16_ref, gp_ref = refs
            hg = None
        first = pl.program_id(0) == 0

        @pl.when(first)
        def _():
            gw_ref[...] = jnp.zeros_like(gw_ref)
            gp_ref[...] = jnp.zeros_like(gp_ref)
            if has_head:
                gh_ref[...] = jnp.zeros_like(gh_ref)

        dr, u, o, gate = dh_ref[...], u_ref[...], o_ref[...], g_ref[...]
        r = lax.rsqrt(jnp.mean(u * u, axis=-1, keepdims=True) + EPS)
        un = u * r
        gp_ref[...] += jnp.sum(dr * un, axis=0, keepdims=True)
        dun = dr * pg_ref[...]
        du = r * (dun - un * jnp.mean(dun * un, axis=-1, keepdims=True))
        on, mix = _mix(o, gate, hg)
        du16 = du.astype(_MXU)
        gw_ref[...] += lax.dot_general(mix.astype(_MXU), du16, _TN, preferred_element_type=F32)
        dmix = lax.dot_general(du16, w_ref[...], _NT, preferred_element_type=F32)
        sg = _sigmoid(gate)
        dg_ref[...] = (dmix * on * (sg * (1.0 + gate * (1.0 - sg)))).astype(dg_ref.dtype)
        don = dmix * (gate * sg)
        if has_head:
            outs, rs = _head_norm(o, hg)
            gh = jnp.zeros((1, D_HEAD), F32)
            cols = []
            for h in range(N_HEADS):
                dn = don[:, h * D_HEAD:(h + 1) * D_HEAD]
                gh = gh + jnp.sum(dn * outs[h], axis=0, keepdims=True)
                dnn = dn * hg
                cols.append(rs[h] * (dnn - outs[h] * jnp.mean(dnn * outs[h], axis=-1, keepdims=True)))
            gh_ref[...] += gh
            do_ref[...] = jnp.concatenate(cols, axis=1)
        else:
            do_ref[...] = don.astype(do_ref.dtype)

        @pl.when(pl.program_id(0) == pl.num_programs(0) - 1)
        def _():
            gw16_ref[...] = gw_ref[...].astype(_MXU)

    row = pl.BlockSpec((tm, D_MODEL), lambda i: (i, 0))
    vec = pl.BlockSpec((1, D_MODEL), lambda i: (0, 0))
    mat = pl.BlockSpec((D_MODEL, D_MODEL), lambda i: (0, 0))
    in_specs = [row, row, row, row, mat, vec]
    args = [dh, u, o, gate, w_out, post_gain]
    out_specs = [row, row, mat, mat, vec]
    out_shape = [jax.ShapeDtypeStruct((t, D_MODEL), _MXU if narrow_do else F32),
                 jax.ShapeDtypeStruct((t, D_MODEL), _MXU)] + [jax.ShapeDtypeStruct((D_MODEL, D_MODEL), F32),
                                                                  jax.ShapeDtypeStruct((D_MODEL, D_MODEL), _MXU),
                                                                  jax.ShapeDtypeStruct((1, D_MODEL), F32)]
    if has_head:
        in_specs.append(pl.BlockSpec((1, D_HEAD), lambda i: (0, 0)))
        args.append(head_gain)
        out_specs.append(pl.BlockSpec((1, D_HEAD), lambda i: (0, 0)))
        out_shape.append(jax.ShapeDtypeStruct((1, D_HEAD), F32))
    return pl.pallas_call(
        body, name=name, grid=(t // tm,), in_specs=in_specs, out_specs=out_specs, out_shape=out_shape,
        compiler_params=_params(("arbitrary",)),
    )(*args)


def _proj_bwd(ds, w4, h_in, gain, dh_out, name, outgoing=(), nq=None):
    t = h_in.shape[0]
    tm = _row_tile(t, 2 * TILE) if nq is None else TILE
    n_out = len(outgoing)
    n_dh = 1 if nq is None else 2

    def body(d0, d1, d2, d3, w_ref, h_ref, g_ref, dho_ref, *rest):
        send, dh_refs, gg_ref, rest = rest[:n_out], rest[n_out:n_out + n_dh], rest[n_out + n_dh], rest[n_out + n_dh + 1:]
        landed, sems = rest[:n_out], rest[n_out:]
        if n_out:
            first, last = _grid_ends(1)
            start, finish = _chip_exchange(send, landed, sems, slotted=True)
            pl.when(first)(start)

        @pl.when(pl.program_id(0) == 0)
        def _():
            gg_ref[...] = jnp.zeros_like(gg_ref)

        dy = jnp.zeros((tm, D_MODEL), F32)
        for j, d in enumerate((d0, d1, d2, d3)):
            dy = dy + lax.dot_general(d[...].astype(_MXU), w_ref[j], _NT, preferred_element_type=F32)
        x = h_ref[...]
        r = lax.rsqrt(jnp.mean(x * x, axis=-1, keepdims=True) + EPS)
        xn = x * r
        gg_ref[...] += jnp.sum(dy * xn, axis=0, keepdims=True)
        dxn = dy * g_ref[...]
        dh = dho_ref[...] + r * (dxn - xn * jnp.mean(dxn * xn, axis=-1, keepdims=True))
        if nq is None:
            dh_refs[0][...] = dh
        else:
            in_front = pl.program_id(0) % nq == 0

            @pl.when(in_front)
            def _():
                dh_refs[1][...] = dh

            @pl.when(jnp.logical_not(in_front))
            def _():
                dh_refs[0][...] = dh
        if n_out:
            pl.when(last)(finish)

    row = pl.BlockSpec((tm, D_MODEL), lambda i: (i, 0))
    vec = pl.BlockSpec((1, D_MODEL), lambda i: (0, 0))
    hbm = pl.BlockSpec(memory_space=pl.ANY)
    if nq is None:
        dh_specs, dh_shapes = [row], [jax.ShapeDtypeStruct((t, D_MODEL), F32)]
    else:
        dh_specs = [pl.BlockSpec((TILE, D_MODEL), lambda i: ((i // nq) * (nq - 1) + jnp.maximum(i % nq - 1, 0), 0)),
                    pl.BlockSpec((TILE, D_MODEL), lambda i: (i // nq, 0))]
        dh_shapes = [jax.ShapeDtypeStruct((t // nq * (nq - 1), D_MODEL), F32),
                     jax.ShapeDtypeStruct((t // nq, D_MODEL), F32)]
    return pl.pallas_call(
        body, name=name, grid=(t // tm,),
        in_specs=[row] * 4 + [pl.BlockSpec((4, D_MODEL, D_MODEL), lambda i: (0, 0, 0)), row, vec, row] + [hbm] * n_out,
        out_specs=dh_specs + [vec] + [hbm] * n_out,
        out_shape=dh_shapes + [jax.ShapeDtypeStruct((1, D_MODEL), F32)]
        + [jax.ShapeDtypeStruct(a.shape, a.dtype) for a in outgoing],
        scratch_shapes=_exchange_scratch(n_out) if n_out else [],
        compiler_params=_params(("arbitrary",)),
    )(*ds, w4, h_in, gain, dh_out, *outgoing)


def _weight_grad(y, d, name):
    t = y.shape[0]
    tk = _row_tile(t, t // 4)

    def body(y_ref, d_ref, g_ref, g16_ref):
        @pl.when(pl.program_id(0) == 0)
        def _():
            g_ref[...] = jnp.zeros_like(g_ref)

        g_ref[...] += lax.dot_general(y_ref[...], d_ref[...].astype(_MXU), _TN, preferred_element_type=F32)

        @pl.when(pl.program_id(0) == pl.num_programs(0) - 1)
        def _():
            g16_ref[...] = g_ref[...].astype(_MXU)

    row = pl.BlockSpec((tk, D_MODEL), lambda i: (i, 0))
    mat = pl.BlockSpec((D_MODEL, D_MODEL), lambda i: (0, 0))
    return pl.pallas_call(
        body, name=name, grid=(t // tk,), in_specs=[row, row], out_specs=[mat, mat],
        out_shape=[jax.ShapeDtypeStruct((D_MODEL, D_MODEL), F32), jax.ShapeDtypeStruct((D_MODEL, D_MODEL), _MXU)],
        compiler_params=_params(("arbitrary",)),
    )(y, d)


def _local_step(x, target, meta, pre_norm, post_norm, lam, head_gain, hw_in, shards):
    bsz, seq, _ = x.shape
    nq = seq // TILE + 1
    nb = nq * (TILE // BLOCK)
    lp = nq * TILE
    t = bsz * lp
    d4 = D_MODEL // N_CHIPS
    front = jnp.concatenate([jnp.zeros((N_PAD, D_MODEL), F32), meta], axis=0)
    h0 = jnp.concatenate([jnp.broadcast_to(front[None], (bsz, TILE, D_MODEL)), x], axis=1).reshape(t, D_MODEL)
    pre0, pre1, post0, post1 = pre_norm[0:1], pre_norm[1:2], post_norm[0:1], post_norm[1:2]

    y0, q0, f0, v0, g0 = _norm_proj(h0, pre0, hw_in, "norm_proj_hgrn", (False,) * 4)
    o0, sst, sym, sw_in, sw_out, hw_out = _hgrn_fwd(q0, f0, v0, lam, bsz, nb, shards)
    sw_out, hw_out = sw_out.reshape(D_MODEL, D_MODEL), hw_out.reshape(D_MODEL, D_MODEL)
    h1, u0 = _out_fwd(o0, g0, h0, hw_out, post0, head_gain, "out_fwd_hgrn")
    y1, q1, k1, v1, g1 = _norm_proj(h1, pre1, sw_in, "norm_proj_sb", (True, True, True, False))
    o1, ctot, visited = _sb_fwd(q1, k1, v1, bsz, nq)
    dh2, u1, loss_blk = _out_fwd_loss(o1, g1, h1, sw_out, post1, target.reshape(bsz * seq, D_MODEL), nq, "out_fwd_sb")

    do1, dg1, g_sw_out, g_sw_out16, g_post1 = _out_bwd(dh2, u1, o1, g1, sw_out, post1, None, "out_bwd_sb", True)
    dq1, dk1, dv1 = _sb_bwd(q1, k1, v1, ctot, visited, do1, bsz, nq)
    ds1 = (dq1, dk1, dv1, dg1)
    dh1, g_pre1 = _proj_bwd(ds1, sw_in, h1, pre1, dh2, "proj_bwd_sb")
    g_sw_in = [_weight_grad(y1, d, "wgrad_sb_%d" % j) for j, d in enumerate(ds1)]

    do0, dg0, g_hw_out, g_hw_out16, g_post0, g_head = _out_bwd(dh1, u0, o0, g0, hw_out, post0, head_gain, "out_bwd_hgrn",
                                                               False)
    ready = (jnp.stack([g16 for _, g16 in g_sw_in]), g_sw_out16.reshape(N_CHIPS, d4, D_MODEL),
             g_hw_out16.reshape(N_CHIPS, d4, D_MODEL))
    dq0, df0, dv0, dlam, land_sw_in, land_sw_out, land_hw_out = _hgrn_bwd(q0, f0, v0, lam, sst, sym, do0, bsz, nb, ready)
    ds0 = (dq0, df0, dv0, dg0)
    g_hw_in = [_weight_grad(y0, d, "wgrad_hgrn_%d" % j) for j, d in enumerate(ds0)]
    last = (jnp.stack([g16 for _, g16 in g_hw_in]),)
    grad_x, dh_front, g_pre0, land_hw_in = _proj_bwd(ds0, hw_in, h0, pre0, dh1, "proj_bwd_hgrn", last, nq)

    grad_x = grad_x.reshape(bsz, seq, D_MODEL)
    g_meta = jnp.sum(dh_front.reshape(bsz, TILE, D_MODEL)[:, N_PAD:, :], axis=0)
    g_lam = jnp.sum(dlam, axis=0)
    small = jnp.concatenate([g_pre0, g_pre1, g_post0, g_post1, g_lam, g_lam,
                             jnp.pad(g_head, ((0, 0), (0, D_MODEL - D_HEAD))), g_meta,
                             jnp.pad(loss_blk[0:1], ((0, 0), (0, D_MODEL - loss_blk.shape[1])))], axis=0)
    rows4 = lambda g: [g[j * d4:(j + 1) * d4] for j in range(N_CHIPS)]
    large = dict(hw_in=(land_hw_in, [g for g, _ in g_hw_in]), sw_in=(land_sw_in, [g for g, _ in g_sw_in]),
                 hw_out=(land_hw_out, rows4(g_hw_out)), sw_out=(land_sw_out, rows4(g_sw_out)))
    return grad_x, small, large


def _prep_weights(hw_in, sw_in, hw_out, sw_out, meta):
    def body(hi_ref, si_ref, ho_ref, so_ref, m_ref, ghi, gm, si16, so16, ho16, far_send, far_recv, near_send, near_recv):
        x, y, c = _place()
        me = 2 * x + y
        ghi[me] = hi_ref[0].astype(_MXU)
        gm[me] = m_ref[...]
        si16[...] = si_ref[0].astype(_MXU)
        so16[...] = so_ref[0].astype(_MXU)
        ho16[...] = ho_ref[0].astype(_MXU)
        outs = (ghi, gm)
        n = len(outs)
        peers = [(1 - x, y), (x, 1 - y), (1 - x, 1 - y)]

        def half(a, slot, which):
            rows = outs[a].shape[1] // 2
            return outs[a].at[slot, pl.ds(which * rows, rows), :]

        def far(r, a, slot):
            px, py = peers[r]
            return pltpu.make_async_remote_copy(
                src_ref=half(a, slot, c), dst_ref=half(a, slot, c), send_sem=far_send.at[r * n + a],
                recv_sem=far_recv.at[r * n + a], device_id=(px, py, c), device_id_type=MESH)

        def near(r, a, which):
            px, py = peers[r]
            return pltpu.make_async_remote_copy(
                src_ref=half(a, 2 * px + py, which), dst_ref=half(a, 2 * px + py, which),
                send_sem=near_send.at[r * n + a], recv_sem=near_recv.at[r * n + a],
                device_id=(x, y, 1 - c), device_id_type=MESH)

        for r in range(3):
            for a in range(n):
                far(r, a, me).start()
        for r, (px, py) in enumerate(peers):
            for a in range(n):
                far(r, a, 2 * px + py).wait_recv()
                near(r, a, c).start()
        for r in range(3):
            for a in range(n):
                near(r, a, 1 - c).wait_recv()
        for r in range(3):
            for a in range(n):
                far(r, a, me).wait_send()
                near(r, a, c).wait_send()

    d4 = D_MODEL // N_CHIPS
    vm = pl.BlockSpec(memory_space=pltpu.VMEM)
    return pl.pallas_call(
        body, name="prep_weights",
        in_specs=[vm] * 5, out_specs=[vm] * 5,
        out_shape=[jax.ShapeDtypeStruct((N_CHIPS, D_MODEL, D_MODEL), _MXU), jax.ShapeDtypeStruct((N_CHIPS, N_META, d4), F32),
                   jax.ShapeDtypeStruct((D_MODEL, D_MODEL), _MXU), jax.ShapeDtypeStruct((d4, D_MODEL), _MXU),
                   jax.ShapeDtypeStruct((d4, D_MODEL), _MXU)],
        scratch_shapes=[pltpu.SemaphoreType.DMA((6,))] * 4,
        compiler_params=pltpu.CompilerParams(vmem_limit_bytes=VMEM_LIMIT),
    )(hw_in, sw_in, hw_out, sw_out, meta)


def _scatter_small(small):
    def body(sm, lsm, send_sems, recv_sems, local_sem):
        x, y, c = _place()
        mine = 4 * x + 2 * y + c
        local = pltpu.make_async_copy(sm, lsm.at[mine], local_sem)
        local.start()

        def copy(rel, src_dev, to):
            return pltpu.make_async_remote_copy(
                src_ref=sm, dst_ref=lsm.at[src_dev], send_sem=send_sems.at[rel - 1], recv_sem=recv_sems.at[rel - 1],
                device_id=to, device_id_type=MESH)

        flip = lambda bit, v: 1 - v if bit else v
        rels = [(rel, flip(rel & 4, x), flip(rel & 2, y), flip(rel & 1, c)) for rel in range(1, N_DEV)]
        sends = [copy(rel, mine, (px, py, pc)) for rel, px, py, pc in rels]
        for cp in sends:
            cp.start()
        for rel, px, py, pc in rels:
            copy(rel, 4 * px + 2 * py + pc, (px, py, pc)).wait_recv()
        for cp in sends:
            cp.wait_send()
        local.wait()

    hbm = pl.BlockSpec(memory_space=pl.ANY)
    return pl.pallas_call(
        body, name="scatter_small", in_specs=[hbm], out_specs=hbm,
        out_shape=jax.ShapeDtypeStruct((N_DEV, SMALL_ROWS, D_MODEL), F32),
        scratch_shapes=[pltpu.SemaphoreType.DMA((N_DEV - 1,)), pltpu.SemaphoreType.DMA((N_DEV - 1,)),
                        pltpu.SemaphoreType.DMA(())],
    )(small)


def _sum_slots(landed, own, me, name):
    n, rows, _ = landed.shape
    tm = rows if rows < 256 else 256

    def body(me_ref, l_ref, o0, o1, o2, o3, out_ref):
        acc = None
        for k, o in enumerate((o0, o1, o2, o3)):
            term = jnp.where(me_ref[0] == k, o[...], l_ref[k].astype(F32))
            acc = term if acc is None else acc + term
        out_ref[...] = acc

    blk = pl.BlockSpec((tm, D_MODEL), lambda i: (i, 0))
    return pl.pallas_call(
        body, name=name, grid=(rows // tm,),
        in_specs=[pl.BlockSpec(memory_space=pltpu.SMEM), pl.BlockSpec((n, tm, D_MODEL), lambda i: (0, i, 0))] + [blk] * 4,
        out_specs=blk, out_shape=jax.ShapeDtypeStruct((rows, D_MODEL), F32),
        compiler_params=_params(("arbitrary",)),
    )(me, landed, *own)


def _swap_with_sibling(parts):
    def body(a0, a1, a2, a3, b0, b1, b2, b3, send_sems, recv_sems):
        x, y, c = _place()
        copies = [pltpu.make_async_remote_copy(src_ref=s, dst_ref=d, send_sem=send_sems.at[a], recv_sem=recv_sems.at[a],
                                               device_id=(x, y, 1 - c), device_id_type=MESH)
                  for a, (s, d) in enumerate(zip((a0, a1, a2, a3), (b0, b1, b2, b3)))]
        for cp in copies:
            cp.start()
        for cp in copies:
            cp.wait()

    hbm = pl.BlockSpec(memory_space=pl.ANY)
    return pl.pallas_call(
        body, name="swap_with_sibling", in_specs=[hbm] * 4, out_specs=[hbm] * 4,
        out_shape=[jax.ShapeDtypeStruct(p.shape, F32) for p in parts],
        scratch_shapes=[pltpu.SemaphoreType.DMA((4,)), pltpu.SemaphoreType.DMA((4,))],
    )(*parts)


def _adamw_math(w, g, m, v):
    m = ADAM_B1 * m + (1.0 - ADAM_B1) * g
    v = ADAM_B2 * v + (1.0 - ADAM_B2) * (g * g)
    m_hat = m / (1.0 - ADAM_B1 ** ADAM_STEP)
    v_hat = v / (1.0 - ADAM_B2 ** ADAM_STEP)
    delta = -ADAM_LR * (m_hat / (jnp.sqrt(v_hat) + ADAM_EPS) + ADAM_WD * w)
    return delta, m, v


def _adamw(w, g_parts, m, v, name):
    rows, cols = w.shape
    tm = rows if rows < 256 else 256
    n = len(g_parts)

    def body(*refs):
        w_ref, m_ref, v_ref = refs[n:n + 3]
        g_ref, d_ref, nm_ref, nv_ref = refs[n + 3:]
        g = refs[0][...]
        for p in refs[1:n]:
            g = g + p[...]
        g_ref[...] = g
        d_ref[...], nm_ref[...], nv_ref[...] = _adamw_math(w_ref[...], g, m_ref[...], v_ref[...])

    blk = pl.BlockSpec((tm, cols), lambda i: (i, 0))
    return pl.pallas_call(
        body, name=name, grid=(rows // tm,), in_specs=[blk] * (n + 3), out_specs=[blk] * 4,
        out_shape=[jax.ShapeDtypeStruct((rows, cols), F32)] * 4,
        compiler_params=_params(("arbitrary",)),
    )(*g_parts, w, m, v)


def _lam_of(hgrn_lb):
    def body(lb_ref, o_ref):
        lb = lb_ref[...]
        e = jnp.exp(lb - jnp.max(lb, axis=0, keepdims=True))
        o_ref[...] = e[0:1, :] / jnp.sum(e, axis=0, keepdims=True)

    return pl.pallas_call(body, name="lam_of", out_shape=jax.ShapeDtypeStruct((1, D_MODEL), F32))(hgrn_lb)


def _small_grads(land_small, lam):
    def body(l_ref, lam_ref, o_ref):
        acc = l_ref[0]
        for k in range(1, N_DEV):
            acc = acc + l_ref[k]
        p = lam_ref[...]
        slope = p * (1.0 - p)
        row = _iota2((SMALL_ROWS, D_MODEL), 0)
        o_ref[...] = acc * jnp.where(row == 4, slope, jnp.where(row == 5, -slope, 1.0))

    return pl.pallas_call(body, name="small_grads",
                          out_shape=jax.ShapeDtypeStruct((SMALL_ROWS, D_MODEL), F32))(land_small, lam)


def kernel(x, meta_tokens, pre_norm, post_norm, hgrn_w_in, hgrn_lb, hgrn_out_norm, hgrn_w_out, sb_w_in, sb_w_out, loss_target, m_meta_tokens, m_pre_norm, m_post_norm, m_hgrn_w_in, m_hgrn_lb, m_hgrn_out_norm, m_hgrn_w_out, m_sb_w_in, m_sb_w_out, v_meta_tokens, v_pre_norm, v_post_norm, v_hgrn_w_in, v_hgrn_lb, v_hgrn_out_norm, v_hgrn_w_out, v_sb_w_in, v_sb_w_out):
    d4 = D_MODEL // N_CHIPS
    chip = 2 * lax.axis_index("x") + lax.axis_index("y")
    hw_in, meta4, sw_in16, sw_out16, hw_out16 = _prep_weights(hgrn_w_in, sb_w_in, hgrn_w_out, sb_w_out, meta_tokens)
    meta = meta4.transpose(1, 0, 2).reshape(N_META, D_MODEL)
    lam = _lam_of(hgrn_lb)
    grad_x, small, large = _local_step(
        x, loss_target, meta, pre_norm, post_norm, lam, hgrn_out_norm,
        hw_in, (sw_in16, sw_out16, hw_out16))

    me = jnp.reshape(chip, (1,)).astype(jnp.int32)
    parts = [_sum_slots(*large[n], me, "sum_" + n) for n in ("hw_in", "sw_in", "hw_out", "sw_out")]
    sib = _swap_with_sibling(parts)
    small = _small_grads(_scatter_small(small), lam)
    loss = small[SMALL_ROWS - 1, 0]

    res = {}
    res["hgrn_w_in"] = _adamw(hgrn_w_in[0], [parts[0], sib[0]], m_hgrn_w_in[0], v_hgrn_w_in[0], "adamw_hw_in")
    res["sb_w_in"] = _adamw(sb_w_in[0], [parts[1], sib[1]], m_sb_w_in[0], v_sb_w_in[0], "adamw_sw_in")
    res["hgrn_w_out"] = _adamw(hgrn_w_out[0], [parts[2], sib[2]], m_hgrn_w_out[0], v_hgrn_w_out[0], "adamw_hw_out")
    res["sb_w_out"] = _adamw(sb_w_out[0], [parts[3], sib[3]], m_sb_w_out[0], v_sb_w_out[0], "adamw_sw_out")
    res["pre_norm"] = _adamw(pre_norm, [small[0:2]], m_pre_norm, v_pre_norm, "adamw_pre")
    res["post_norm"] = _adamw(post_norm, [small[2:4]], m_post_norm, v_post_norm, "adamw_post")
    res["hgrn_lb"] = _adamw(hgrn_lb, [small[4:6]], m_hgrn_lb, v_hgrn_lb, "adamw_lb")
    res["hgrn_out_norm"] = _adamw(hgrn_out_norm, [small[6:7, :D_HEAD]], m_hgrn_out_norm, v_hgrn_out_norm, "adamw_head")
    g_meta = lax.dynamic_slice_in_dim(small[7:7 + N_META], chip * d4, d4, axis=1)
    res["meta_tokens"] = _adamw(meta_tokens, [g_meta], m_meta_tokens, v_meta_tokens, "adamw_meta")
    for n in ("hgrn_w_in", "hgrn_w_out", "sb_w_in", "sb_w_out"):
        res[n] = tuple(a[None] for a in res[n])
    order = ("meta_tokens", "pre_norm", "post_norm", "hgrn_w_in", "hgrn_lb", "hgrn_out_norm", "hgrn_w_out",
             "sb_w_in", "sb_w_out")
    return (loss, grad_x, *[res[n][0] for n in order], *[res[n][1] for n in order],
            *[res[n][2] for n in order], *[res[n][3] for n in order])
```
